```python
import math
import jax, jax.numpy as jnp
from jax import lax
import numpy as np

D_MODEL = 1024
BATCH = 32
SEQ = 256
DEPTH = 2
DEC_BATCH = 2
DEC_SEQ = 1024
PAST_LEN = 512

GRID_W = 64
HEAD_DIM = 64
ROPE_BASE = 10000.0
H_RET = 8
RET_CHUNK = 128
H_WIN = 8
KV_WIN = 2
G_WIN = H_WIN // KV_WIN
WINDOW = 128
Q_BLOCK = 128
H_DIFF = 6
FNET_GROUPS = 4
FNET_DIM = 64
D_FF = 256 * math.ceil(8 * D_MODEL / 3 / 256)
N_AB = (DEPTH + 1) // 2
N_CD = DEPTH // 2
AB_IN = 4 * H_RET * HEAD_DIM + (H_WIN + 2 * KV_WIN) * HEAD_DIM
AB_OUT = (H_RET + H_WIN) * HEAD_DIM
CD_IN = 3 * H_DIFF * 2 * HEAD_DIM + FNET_GROUPS * FNET_DIM
CD_OUT = H_DIFF * 2 * HEAD_DIM + FNET_GROUPS * FNET_DIM
ALPHA = (2 * DEPTH) ** 0.25
BETA = (8 * DEPTH) ** -0.25
LN_EPS = 1e-5

kernel_name = 'hybrid_retention_window_diff_fnet_dit_step'

f32 = jnp.float32


def layer_norm(x, g=None, b=None):
    xf = x.astype(f32)
    mu = jnp.mean(xf, -1, keepdims=True)
    var = jnp.mean(jnp.square(xf - mu), -1, keepdims=True)
    y = (xf - mu) * lax.rsqrt(var + LN_EPS)
    if g is not None:
        y = y * g.astype(f32) + b.astype(f32)
    return y.astype(x.dtype)


def rms_norm(x, g):
    xf = x.astype(f32)
    y = xf * lax.rsqrt(jnp.mean(jnp.square(xf), -1, keepdims=True) + LN_EPS) * g.astype(f32)
    return y.astype(x.dtype)


def axial_rope(n_tokens, dim):
    rows = n_tokens // GRID_W
    r, col = jnp.meshgrid(jnp.arange(rows), jnp.arange(GRID_W), indexing='ij')
    r = r.reshape(-1).astype(f32)
    col = col.reshape(-1).astype(f32)
    quarter = dim // 4
    inv = ROPE_BASE ** (-jnp.arange(quarter, dtype=f32) / quarter)
    ang = jnp.concatenate([r[:, None] * inv, col[:, None] * inv], -1)
    return jnp.cos(ang), jnp.sin(ang)


def apply_rope(x, cos, sin):
    half = x.shape[-1] // 2
    shape = (1, x.shape[1]) + (1,) * (x.ndim - 3) + (half,)
    cos = cos.reshape(shape)
    sin = sin.reshape(shape)
    xf = x.astype(f32)
    x1, x2 = xf[..., :half], xf[..., half:]
    return jnp.concatenate([x1 * cos - x2 * sin, x1 * sin + x2 * cos], -1).astype(x.dtype)


def map_query_blocks(fn, q):
    B, T = q.shape[:2]
    nb = T // Q_BLOCK

    def body(n):
        start = n * Q_BLOCK
        return fn(lax.dynamic_slice_in_dim(q, start, Q_BLOCK, axis=1), start)

    out = jnp.moveaxis(lax.map(body, jnp.arange(nb)), 0, 1)
    return out.reshape((B, T) + out.shape[3:])


def gqa_attend(q, k, v, mask, sink):
    s = jnp.einsum('bqhgd,bkhd->bhgqk', q, k).astype(f32) * (q.shape[-1] ** -0.5)
    if mask is not None:
        s = jnp.where(mask, s, -jnp.inf)
    sink_col = jnp.broadcast_to(sink.astype(f32)[None, :, :, None, None], s.shape[:-1] + (1,))
    p = jax.nn.softmax(jnp.concatenate([s, sink_col], -1), axis=-1)[..., :-1]
    return jnp.einsum('bhgqk,bkhd->bqhgd', p.astype(v.dtype), v)


def window_attention_latent(q, k, v, ctx_k, ctx_v, sink):
    T = q.shape[1]
    Tc = ctx_k.shape[1]
    pad = ((0, 0), (Q_BLOCK, Q_BLOCK), (0, 0), (0, 0))
    kp = jnp.pad(k, pad)
    vp = jnp.pad(v, pad)
    a = jnp.arange(Q_BLOCK)[:, None]
    b = jnp.arange(3 * Q_BLOCK)[None, :]
    band = jnp.abs(a - b + Q_BLOCK) <= WINDOW
    ctx_mask = jnp.ones((Q_BLOCK, Tc), bool)

    def block(qb, start):
        kb = lax.dynamic_slice_in_dim(kp, start, 3 * Q_BLOCK, axis=1)
        vb = lax.dynamic_slice_in_dim(vp, start, 3 * Q_BLOCK, axis=1)
        j = start - Q_BLOCK + b
        mask = jnp.concatenate([band & (j >= 0) & (j < T), ctx_mask], 1)
        return gqa_attend(qb, jnp.concatenate([kb, ctx_k], 1), jnp.concatenate([vb, ctx_v], 1), mask, sink)

    return map_query_blocks(block, q)


def retention_chunkwise(q, k, v, log_gamma, s0):
    B, H, T, _ = q.shape
    dv = v.shape[-1]
    n = T // RET_CHUNK

    def chunks(x):
        return jnp.moveaxis(x.reshape(B, H, n, RET_CHUNK, x.shape[-1]), 2, 0)

    idx = jnp.arange(RET_CHUNK, dtype=f32)
    lg = log_gamma[:, None]
    diff = idx[:, None] - idx[None, :]
    intra = jnp.where(diff >= 0, jnp.exp(lg[:, :, None] * jnp.maximum(diff, 0.0)), 0.0)
    q_decay = jnp.exp(lg * (idx + 1.0))
    k_decay = jnp.exp(lg * (RET_CHUNK - 1.0 - idx))
    chunk_decay = jnp.exp(lg * RET_CHUNK)[..., None]

    def step(s, qkv):
        qc, kc, vc = qkv
        att = jnp.einsum('bhid,bhjd->bhij', qc, kc) * intra
        o = jnp.einsum('bhij,bhjv->bhiv', att, vc) + jnp.einsum('bhid,bhdv->bhiv', qc, s) * q_decay[..., None]
        s = s * chunk_decay + jnp.einsum('bhjd,hj,bhjv->bhdv', kc, k_decay, vc)
        return s, o

    s_fin, o = lax.scan(step, s0, (chunks(q), chunks(k), chunks(v)))
    return jnp.moveaxis(o, 0, 2).reshape(B, H, T, dv), s_fin


def bidir_retention(q, k, v, log_gamma, s0):
    def flip(t):
        return jnp.flip(t, 2)

    o, s = jax.vmap(retention_chunkwise)(jnp.stack([q, flip(q)]), jnp.stack([k, flip(k)]),
                                         jnp.stack([v, flip(v)]), log_gamma, s0)
    return o[0] + flip(o[1]), s


def mixer_ab(u, w_in, w_out, log_gamma, gn_g, gn_b, sink, ctx):
    B, T, _ = u.shape
    hr = H_RET * HEAD_DIM
    sizes = [hr, hr, hr, hr, H_WIN * HEAD_DIM, KV_WIN * HEAD_DIM]
    rq, rk, rv, rg, wq, wk, wv = jnp.split(u @ w_in, np.cumsum(sizes).tolist(), axis=-1)
    rq = rq.reshape(B, T, H_RET, HEAD_DIM)
    rk = rk.reshape(B, T, H_RET, HEAD_DIM)
    rv = rv.reshape(B, T, H_RET, HEAD_DIM)
    wq = wq.reshape(B, T, KV_WIN, G_WIN, HEAD_DIM)
    wk = wk.reshape(B, T, KV_WIN, HEAD_DIM)
    wv = wv.reshape(B, T, KV_WIN, HEAD_DIM)
    if ctx is None:
        s0 = jnp.zeros((2, B, H_RET, HEAD_DIM, HEAD_DIM), f32)
        wo = map_query_blocks(lambda qb, start: gqa_attend(qb, wk, wv, None, sink), wq)
    else:
        state, ck, cv = ctx
        cos, sin = axial_rope(T, HEAD_DIM)
        rq, rk = apply_rope(rq, cos, sin), apply_rope(rk, cos, sin)
        wq, wk = apply_rope(wq, cos, sin), apply_rope(wk, cos, sin)
        s0 = jnp.moveaxis(state.astype(f32), 1, 0)
        wo = window_attention_latent(wq, wk, wv, ck, cv, sink)

    def to_bh(t):
        return jnp.transpose(t, (0, 2, 1, 3)).astype(f32)

    ro, s_fin = bidir_retention(to_bh(rq), to_bh(rk) * (HEAD_DIM ** -0.5), to_bh(rv), log_gamma.astype(f32), s0)
    ro = layer_norm(ro)
    ro = jnp.transpose(ro, (0, 2, 1, 3)).reshape(B, T, hr) * gn_g.astype(f32) + gn_b.astype(f32)
    ro = jax.nn.silu(rg) * ro.astype(u.dtype)
    out = jnp.concatenate([ro, wo.reshape(B, T, -1)], -1) @ w_out
    return out, (jnp.moveaxis(s_fin, 0, 1), wk, wv)


def mixer_cd(u, w_in, w_out, lam, subln_g, lam_init, ctx):
    B, T, _ = u.shape
    qd = H_DIFF * 2 * HEAD_DIM
    dq, dk, dv, fz = jnp.split(u @ w_in, [qd, 2 * qd, 3 * qd], axis=-1)
    dq = dq.reshape(B, T, H_DIFF, 2, HEAD_DIM)
    dk = dk.reshape(B, T, H_DIFF, 2, HEAD_DIM)
    dv = dv.reshape(B, T, H_DIFF, 2 * HEAD_DIM)
    lam = lam.astype(f32)
    lam_full = jnp.exp(jnp.sum(lam[0] * lam[1])) - jnp.exp(jnp.sum(lam[2] * lam[3])) + lam_init
    if ctx is None:
        keys, vals = dk, dv
    else:
        ck, cv = ctx
        cos, sin = axial_rope(T, HEAD_DIM)
        dq, dk = apply_rope(dq, cos, sin), apply_rope(dk, cos, sin)
        keys = jnp.concatenate([dk, ck], 1)
        vals = jnp.concatenate([dv, cv], 1)

    def block(qb, start):
        s = jnp.einsum('bqhmd,bkhmd->bhmqk', qb, keys).astype(f32) * (HEAD_DIM ** -0.5)
        p = jax.nn.softmax(s, axis=-1)
        p = p[:, :, 0] - lam_full * p[:, :, 1]
        return jnp.einsum('bhqk,bkhe->bqhe', p.astype(vals.dtype), vals)

    a = map_query_blocks(block, dq)
    a = rms_norm(a, subln_g) * (1.0 - lam_init)
    z = fz.reshape(B, T, FNET_GROUPS, FNET_DIM).astype(f32)
    z = jnp.real(jnp.fft.fft2(z, axes=(1, 3), norm='ortho')).astype(u.dtype)
    out = jnp.concatenate([a.reshape(B, T, -1), z.reshape(B, T, -1)], -1) @ w_out
    return out, (dk, dv)


def swiglu(u, w_gate, w_up, w_down):
    return (jax.nn.silu(u @ w_gate) * (u @ w_up)) @ w_down


def trunk(x, cond, ctx_ab, ctx_cd, w_mod, b_mod, ln_g, ln_b, w_in_ab, w_out_ab, ret_log_gamma, ret_gn_g,
          ret_gn_b, win_sink, w_in_cd, w_out_cd, diff_lambda, diff_subln_g, w_gate, w_up, w_down):
    new_ab, new_cd = [], []
    for l in range(DEPTH):
        mod = jax.nn.silu(cond) @ w_mod[l] + b_mod[l]
        sh1, sc1, g1, sh2, sc2, g2 = jnp.split(mod[:, None, :], 6, axis=-1)
        u = layer_norm(x) * (1.0 + sc1) + sh1
        i = l // 2
        if l % 2 == 0:
            ctx = None if ctx_ab is None else tuple(t[:, i] for t in ctx_ab)
            h, new = mixer_ab(u, w_in_ab[i], w_out_ab[i], ret_log_gamma[i], ret_gn_g[i], ret_gn_b[i],
                              win_sink[i].reshape(KV_WIN, G_WIN), ctx)
            new_ab.append(new)
        else:
            ctx = None if ctx_cd is None else tuple(t[:, i] for t in ctx_cd)
            lam_init = 0.8 - 0.6 * math.exp(-0.3 * l)
            h, new = mixer_cd(u, w_in_cd[i], w_out_cd[i], diff_lambda[i], diff_subln_g[i], lam_init, ctx)
            new_cd.append(new)
        x = layer_norm(ALPHA * x + g1 * h, ln_g[l, 0], ln_b[l, 0])
        u = layer_norm(x) * (1.0 + sc2) + sh2
        x = layer_norm(ALPHA * x + g2 * swiglu(u, w_gate[l], w_up[l], w_down[l]), ln_g[l, 1], ln_b[l, 1])
    return x, new_ab, new_cd


def setup_inputs(seed: int = 0) -> dict:
    key = jax.random.key(seed)
    ks = jax.random.split(key, 32)

    def nrm(k, shape, s):
        return jax.random.normal(k, shape, f32) * s

    base_lg = jnp.log1p(-(2.0 ** (-5.0 - jnp.arange(H_RET, dtype=f32))))
    return {
        'x_prompt': nrm(ks[0], (BATCH, SEQ, D_MODEL), 1.0),
        'x_sample': nrm(ks[1], (DEC_BATCH, DEC_SEQ, D_MODEL), 1.0),
        'state_ret': nrm(ks[2], (DEC_BATCH, N_AB, 2, H_RET, HEAD_DIM, HEAD_DIM), 1.0),
        'cache_win_k': nrm(ks[3], (DEC_BATCH, N_AB, PAST_LEN, KV_WIN, HEAD_DIM), 1.0),
        'cache_win_v': nrm(ks[4], (DEC_BATCH, N_AB, PAST_LEN, KV_WIN, HEAD_DIM), 1.0),
        'cache_diff_k': nrm(ks[5], (DEC_BATCH, N_CD, PAST_LEN, H_DIFF, 2, HEAD_DIM), 1.0),
        'cache_diff_v': nrm(ks[6], (DEC_BATCH, N_CD, PAST_LEN, H_DIFF, 2 * HEAD_DIM), 1.0),
        'c': nrm(ks[7], (DEC_BATCH, D_MODEL), 1.0),
        'c_ctx': nrm(ks[8], (D_MODEL,), 1.0),
        'w_mod': nrm(ks[9], (DEPTH, D_MODEL, 6 * D_MODEL), D_MODEL ** -0.5),
        'b_mod': nrm(ks[10], (DEPTH, 6 * D_MODEL), 0.02),
        'ln_g': 1.0 + nrm(ks[11], (DEPTH, 2, D_MODEL), 0.02),
        'ln_b': nrm(ks[12], (DEPTH, 2, D_MODEL), 0.02),
        'w_in_ab': nrm(ks[13], (N_AB, D_MODEL, AB_IN), D_MODEL ** -0.5),
        'w_out_ab': nrm(ks[14], (N_AB, AB_OUT, D_MODEL), BETA * AB_OUT ** -0.5),
        'ret_log_gamma': base_lg * (1.0 + nrm(ks[15], (N_AB, 2, H_RET), 0.1)),
        'ret_gn_g': 1.0 + nrm(ks[16], (N_AB, H_RET * HEAD_DIM), 0.02),
        'ret_gn_b': nrm(ks[17], (N_AB, H_RET * HEAD_DIM), 0.02),
        'win_sink': nrm(ks[18], (N_AB, H_WIN), 1.0),
        'w_in_cd': nrm(ks[19], (N_CD, D_MODEL, CD_IN), D_MODEL ** -0.5),
        'w_out_cd': nrm(ks[20], (N_CD, CD_OUT, D_MODEL), BETA * CD_OUT ** -0.5),
        'diff_lambda': nrm(ks[21], (N_CD, 4, HEAD_DIM), 0.1),
        'diff_subln_g': 1.0 + nrm(ks[22], (N_CD, 2 * HEAD_DIM), 0.02),
        'w_gate': nrm(ks[23], (DEPTH, D_MODEL, D_FF), D_MODEL ** -0.5),
        'w_up': nrm(ks[24], (DEPTH, D_MODEL, D_FF), D_MODEL ** -0.5),
        'w_down': nrm(ks[25], (DEPTH, D_FF, D_MODEL), BETA * D_FF ** -0.5),
    }


def reference(x_prompt, x_sample, state_ret, cache_win_k, cache_win_v, cache_diff_k, cache_diff_v, c, c_ctx,
              w_mod, b_mod, ln_g, ln_b, w_in_ab, w_out_ab, ret_log_gamma, ret_gn_g, ret_gn_b, win_sink,
              w_in_cd, w_out_cd, diff_lambda, diff_subln_g, w_gate, w_up, w_down):
    weights = (w_mod, b_mod, ln_g, ln_b, w_in_ab, w_out_ab, ret_log_gamma, ret_gn_g, ret_gn_b, win_sink,
               w_in_cd, w_out_cd, diff_lambda, diff_subln_g, w_gate, w_up, w_down)
    y_prompt, new_ab, new_cd = trunk(x_prompt, c_ctx[None, :], None, None, *weights)
    y_sample, _, _ = trunk(x_sample, c, (state_ret, cache_win_k, cache_win_v), (cache_diff_k, cache_diff_v), *weights)
    new_state_ret = jnp.stack([n[0] for n in new_ab], axis=1)
    new_cache_win_k = jnp.stack([n[1] for n in new_ab], axis=1)
    new_cache_win_v = jnp.stack([n[2] for n in new_ab], axis=1)
    new_cache_diff_k = jnp.stack([n[0] for n in new_cd], axis=1)
    new_cache_diff_v = jnp.stack([n[1] for n in new_cd], axis=1)
    return (y_prompt, y_sample, new_state_ret, new_cache_win_k, new_cache_win_v, new_cache_diff_k, new_cache_diff_v)
```

```python
import functools
import math

import jax
import jax.numpy as jnp
import numpy as np
from jax import lax
from jax.experimental import pallas as pl
from jax.experimental.pallas import tpu as pltpu

D_MODEL = 1024
BATCH = 32
SEQ = 256
DEPTH = 2
DEC_BATCH = 2
DEC_SEQ = 1024
PAST_LEN = 512
GRID_W = 64
HEAD_DIM = 64
ROPE_BASE = 10000.0
H_RET = 8
H_WIN = 8
KV_WIN = 2
G_WIN = H_WIN // KV_WIN
WINDOW = 128
H_DIFF = 6
FNET_GROUPS = 4
FNET_DIM = 64
D_FF = 256 * math.ceil(8 * D_MODEL / 3 / 256)
RET_W = H_RET * HEAD_DIM
WIN_W = H_WIN * HEAD_DIM
KV_W = KV_WIN * HEAD_DIM
AB_IN = 4 * RET_W + WIN_W + 2 * KV_W
DIFF_W = H_DIFF * 2 * HEAD_DIM
FNET_W = FNET_GROUPS * FNET_DIM
CD_IN = 3 * DIFF_W + FNET_W
ALPHA = (2 * DEPTH) ** 0.25
LN_EPS = 1e-5
QK_SCALE = HEAD_DIM ** -0.5

LANES = 128
PAIR_W = 2 * HEAD_DIM
TM = 512
TQ = 256
N_CHUNK = 256
NEG_BIG = -1e30
VMEM_LIMIT = 56 * 1024 * 1024

f32 = jnp.float32
bf16 = jnp.bfloat16


def _params(n_axes):
    return pltpu.CompilerParams(dimension_semantics=("arbitrary",) * n_axes,
                                vmem_limit_bytes=VMEM_LIMIT)


def _dot(a, b):
    return jnp.dot(a, b, preferred_element_type=f32)


def _dot_nt(a, b):
    return lax.dot_general(a, b, (((1,), (1,)), ((), ())), preferred_element_type=f32)


def _ln(x):
    mu = jnp.mean(x, -1, keepdims=True)
    d = x - mu
    var = jnp.mean(d * d, -1, keepdims=True)
    return d * lax.rsqrt(var + LN_EPS)


def _silu(x):
    return x * jax.nn.sigmoid(x)


def _split_bf16(x):
    hi = x.astype(bf16)
    lo = (x - hi.astype(f32)).astype(bf16)
    return hi, lo


def _lane_half_mask(shape):
    return (lax.broadcasted_iota(jnp.int32, shape, len(shape) - 1) & HEAD_DIM) == 0


def _mod_kernel(c_ref, w_ref, b_ref, o_ref):
    a = _silu(c_ref[...])
    a_hi, a_lo = _split_bf16(a)
    w_hi, w_lo = _split_bf16(w_ref[0])
    acc = _dot(a_hi, w_hi) + _dot(a_lo, w_hi) + _dot(a_hi, w_lo)
    o_ref[0] = acc + b_ref[0]


def _modulation(cond, w_mod, b_mod):
    tn = 1536
    rows = cond.shape[0]
    return pl.pallas_call(
        _mod_kernel,
        grid=(DEPTH, 6 * D_MODEL // tn),
        in_specs=[pl.BlockSpec((rows, D_MODEL), lambda l, j: (0, 0)),
                  pl.BlockSpec((1, D_MODEL, tn), lambda l, j: (l, 0, j)),
                  pl.BlockSpec((1, 1, tn), lambda l, j: (l, 0, j))],
        out_specs=pl.BlockSpec((1, rows, tn), lambda l, j: (l, 0, j)),
        out_shape=jax.ShapeDtypeStruct((DEPTH, rows, 6 * D_MODEL), f32),
        compiler_params=_params(2),
        name="modulation",
    )(cond, w_mod, b_mod.reshape(DEPTH, 1, 6 * D_MODEL))


def _rope_pair(y, cos, sin_signed):
    first_half = (lax.broadcasted_iota(jnp.int32, y.shape, 1) & (HEAD_DIM // 2)) == 0
    swapped = jnp.where(first_half, pltpu.roll(y, LANES - HEAD_DIM // 2, 1), pltpu.roll(y, HEAD_DIM // 2, 1))
    return y * cos + swapped * sin_signed


def _proj_kernel(*refs, n_out, rope_tiles):
    if rope_tiles:
        x_ref, mod_ref, w_ref, cos_ref, sin_ref, o_ref, u_ref = refs
    else:
        x_ref, mod_ref, w_ref, o_ref, u_ref = refs
    shift = mod_ref[0, 0:1, :]
    scale = mod_ref[0, 1:2, :]
    u_ref[...] = (_ln(x_ref[...]) * (1.0 + scale) + shift).astype(bf16)
    for j in range(0, n_out, N_CHUNK):
        y = _dot(u_ref[...], w_ref[:, j:j + N_CHUNK])
        for t in range(N_CHUNK // LANES):
            col = j + t * LANES
            piece = y[:, t * LANES:(t + 1) * LANES]
            if col // LANES in rope_tiles:
                piece = _rope_pair(piece, cos_ref[...], sin_ref[...])
            o_ref[:, col:col + LANES] = piece


def _proj(x, mod, w, rows_per_mod, rope_tabs=None, rope_tiles=()):
    n = x.shape[0]
    n_out = w.shape[1]
    in_specs = [pl.BlockSpec((TM, D_MODEL), lambda i: (i, 0)),
                pl.BlockSpec((1, 6, D_MODEL), lambda i: ((i * TM) // rows_per_mod, 0, 0)),
                pl.BlockSpec((D_MODEL, n_out), lambda i: (0, 0))]
    args = [x, mod, w]
    if rope_tiles:
        nb = DEC_SEQ // TM
        in_specs += [pl.BlockSpec((TM, LANES), lambda i: (i % nb, 0))] * 2
        args += list(rope_tabs)
    return pl.pallas_call(
        functools.partial(_proj_kernel, n_out=n_out, rope_tiles=frozenset(rope_tiles)),
        grid=(n // TM,),
        in_specs=in_specs,
        out_specs=pl.BlockSpec((TM, n_out), lambda i: (i, 0)),
        out_shape=jax.ShapeDtypeStruct((n, n_out), f32),
        scratch_shapes=[pltpu.VMEM((TM, D_MODEL), bf16)],
        compiler_params=_params(1),
        name="proj",
    )(*args)


def _post_kernel(x_ref, a_ref, b_ref, mod_ref, lng_ref, lnb_ref, wo_ref, wg_ref, wu_ref, wd_ref,
                 o_ref, u_ref, h_ref, *, ka):
    gate1 = mod_ref[0, 2:3, :]
    shift2 = mod_ref[0, 3:4, :]
    scale2 = mod_ref[0, 4:5, :]
    gate2 = mod_ref[0, 5:6, :]
    h = _dot(a_ref[...], wo_ref[0:ka, :]) + _dot(b_ref[...], wo_ref[ka:, :])
    x1 = _ln(ALPHA * x_ref[...] + gate1 * h) * lng_ref[0:1, :] + lnb_ref[0:1, :]
    u_ref[...] = (_ln(x1) * (1.0 + scale2) + shift2).astype(bf16)
    for j in range(0, D_FF, N_CHUNK):
        g = _dot(u_ref[...], wg_ref[:, j:j + N_CHUNK])
        up = _dot(u_ref[...], wu_ref[:, j:j + N_CHUNK])
        h_ref[:, j:j + N_CHUNK] = (_silu(g) * up).astype(bf16)
    ffn = _dot(h_ref[...], wd_ref[...])
    o_ref[...] = _ln(ALPHA * x1 + gate2 * ffn) * lng_ref[1:2, :] + lnb_ref[1:2, :]


def _post(x, a, b, mod, ln_g, ln_b, w_out, w_gate, w_up, w_down, rows_per_mod):
    n = x.shape[0]
    ka, kb = a.shape[1], b.shape[1]
    const = lambda i: (0, 0)
    return pl.pallas_call(
        functools.partial(_post_kernel, ka=ka),
        grid=(n // TM,),
        in_specs=[pl.BlockSpec((TM, D_MODEL), lambda i: (i, 0)),
                  pl.BlockSpec((TM, ka), lambda i: (i, 0)),
                  pl.BlockSpec((TM, kb), lambda i: (i, 0)),
                  pl.BlockSpec((1, 6, D_MODEL), lambda i: ((i * TM) // rows_per_mod, 0, 0)),
                  pl.BlockSpec((2, D_MODEL), const),
                  pl.BlockSpec((2, D_MODEL), const),
                  pl.BlockSpec((ka + kb, D_MODEL), const, pipeline_mode=pl.Buffered(1)),
                  pl.BlockSpec((D_MODEL, D_FF), const, pipeline_mode=pl.Buffered(1)),
                  pl.BlockSpec((D_MODEL, D_FF), const, pipeline_mode=pl.Buffered(1)),
                  pl.BlockSpec((D_FF, D_MODEL), const, pipeline_mode=pl.Buffered(1))],
        out_specs=pl.BlockSpec((TM, D_MODEL), lambda i: (i, 0)),
        out_shape=jax.ShapeDtypeStruct((n, D_MODEL), f32),
        scratch_shapes=[pltpu.VMEM((TM, D_MODEL), bf16), pltpu.VMEM((TM, D_FF), bf16)],
        compiler_params=_params(1),
        name="post",
    )(x, a, b, mod, ln_g, ln_b, w_out, w_gate, w_up, w_down)


def _group_norm_gate(ro, rg, gmat, gn_g, gn_b):
    def gmean(v):
        hi, lo = _split_bf16(v)
        return _dot(hi, gmat) + _dot(lo, gmat)

    d = ro - gmean(ro)
    var = gmean(d * d)
    y = d * lax.rsqrt(var + LN_EPS) * gn_g + gn_b
    return _silu(rg) * y


def _dup_head(x, j):
    first = _lane_half_mask(x.shape)
    keep = first if j == 0 else jnp.logical_not(first)
    xm = jnp.where(keep, x, 0.0)
    return xm + pltpu.roll(xm, HEAD_DIM, 1)


def _softmax_parts(scores, sink):
    m = sink
    for s in scores:
        m = jnp.maximum(m, jnp.max(s, -1, keepdims=True))
    es = [jnp.exp(s - m) for s in scores]
    denom = jnp.exp(sink - m)
    for e in es:
        denom = denom + jnp.sum(e, -1, keepdims=True)
    return es, denom


def _ctx_ab_kernel(lg_ref, sink_ref, rq_ref, rk_ref, rv_ref, rg_ref, wq_ref, wk_ref, wv_ref,
                   lgf_ref, lgb_ref, gmat_ref, gng_ref, gnb_ref,
                   ro_ref, wo_ref, st_ref, dmask_ref, kdec_ref, ret_ref):
    t_len = SEQ

    @pl.when(pl.program_id(0) == 0)
    def _():
        row = lax.broadcasted_iota(jnp.int32, (t_len, t_len), 0)
        col = lax.broadcasted_iota(jnp.int32, (t_len, t_len), 1)
        diff = (row - col).astype(f32)
        diag = jnp.where(row == col, 2.0 * QK_SCALE, QK_SCALE)
        for h in range(H_RET):
            dmask_ref[h] = jnp.exp(jnp.where(diff >= 0, lg_ref[0, h] * diff, -lg_ref[1, h] * diff)) * diag
        t = lax.broadcasted_iota(jnp.int32, (t_len, RET_W), 0).astype(f32)
        kdec_ref[0] = jnp.exp(lgf_ref[...] * (t_len - 1.0 - t)) * QK_SCALE
        kdec_ref[1] = jnp.exp(lgb_ref[...] * t) * QK_SCALE

    first = _lane_half_mask((t_len, PAIR_W))
    for p in range(H_RET // 2):
        sl = slice(p * PAIR_W, (p + 1) * PAIR_W)
        q = rq_ref[:, sl].astype(bf16)
        k = rk_ref[:, sl]
        kb = k.astype(bf16)
        v = rv_ref[:, sl].astype(bf16)
        outs = []
        for e in range(2):
            keep = first if e == 0 else jnp.logical_not(first)
            s = _dot_nt(q, jnp.where(keep, kb, jnp.zeros_like(kb))) * dmask_ref[2 * p + e]
            outs.append(_dot(s.astype(bf16), v))
        ret_ref[:, sl] = jnp.where(first, outs[0], outs[1])
        for d in range(2):
            kd_t = (k * kdec_ref[d, :, sl]).T.astype(bf16)
            st = _dot(kd_t, v)
            st_ref[0, d, 2 * p] = st[0:HEAD_DIM, 0:HEAD_DIM]
            st_ref[0, d, 2 * p + 1] = pltpu.roll(st[HEAD_DIM:, :], HEAD_DIM, 1)[:, 0:HEAD_DIM]
    ro_ref[...] = _group_norm_gate(ret_ref[...], rg_ref[...], gmat_ref[...], gng_ref[...], gnb_ref[...]).astype(bf16)

    for j in range(KV_WIN):
        k_dup = _dup_head(wk_ref[...], j).astype(bf16)
        v_dup = _dup_head(wv_ref[...], j).astype(bf16)
        for pp in range(G_WIN // 2):
            col = (j * G_WIN + 2 * pp) * HEAD_DIM
            qb = wq_ref[:, col:col + PAIR_W].astype(bf16)
            outs = []
            for e in range(2):
                keep = first if e == 0 else jnp.logical_not(first)
                s = _dot_nt(jnp.where(keep, qb, jnp.zeros_like(qb)), k_dup) * QK_SCALE
                (es,), denom = _softmax_parts([s], sink_ref[0, j * G_WIN + 2 * pp + e])
                outs.append(_dot(es.astype(bf16), v_dup) / denom)
            wo_ref[:, col:col + PAIR_W] = jnp.where(first, outs[0], outs[1]).astype(bf16)


def _ctx_ab(proj, log_gamma, sink, lgf_lanes, lgb_lanes, gmat, gn_g, gn_b):
    t = SEQ
    smem = pl.BlockSpec(memory_space=pltpu.SMEM)
    const = lambda b: (0, 0)
    col = lambda c: (lambda b: (b, c))
    return pl.pallas_call(
        _ctx_ab_kernel,
        grid=(BATCH,),
        in_specs=[smem, smem,
                  pl.BlockSpec((t, RET_W), col(0)), pl.BlockSpec((t, RET_W), col(1)),
                  pl.BlockSpec((t, RET_W), col(2)), pl.BlockSpec((t, RET_W), col(3)),
                  pl.BlockSpec((t, WIN_W), col(4)),
                  pl.BlockSpec((t, KV_W), col((4 * RET_W + WIN_W) // KV_W)),
                  pl.BlockSpec((t, KV_W), col((4 * RET_W + WIN_W) // KV_W + 1)),
                  pl.BlockSpec((1, RET_W), const), pl.BlockSpec((1, RET_W), const),
                  pl.BlockSpec((RET_W, RET_W), const),
                  pl.BlockSpec((1, RET_W), const), pl.BlockSpec((1, RET_W), const)],
        out_specs=[pl.BlockSpec((t, RET_W), lambda b: (b, 0)),
                   pl.BlockSpec((t, WIN_W), lambda b: (b, 0)),
                   pl.BlockSpec((1, 2, H_RET, HEAD_DIM, HEAD_DIM), lambda b: (b, 0, 0, 0, 0))],
        out_shape=[jax.ShapeDtypeStruct((BATCH * t, RET_W), bf16),
                   jax.ShapeDtypeStruct((BATCH * t, WIN_W), bf16),
                   jax.ShapeDtypeStruct((BATCH, 2, H_RET, HEAD_DIM, HEAD_DIM), f32)],
        scratch_shapes=[pltpu.VMEM((H_RET, t, t), f32), pltpu.VMEM((2, t, RET_W), f32),
                        pltpu.VMEM((t, RET_W), f32)],
        compiler_params=_params(1),
        name="ctx_ab",
    )(log_gamma, sink, proj, proj, proj, proj, proj, proj, proj, lgf_lanes, lgb_lanes, gmat, gn_g, gn_b)


def _lat_ab_kernel(lg_ref, sink_ref, rq_ref, rk_ref, rv_ref, rg_ref, wq_ref, wk_ref, wv_ref, ck_ref, cv_ref,
                   s0_ref, lgf_ref, lgb_ref, gmat_ref, gng_ref, gnb_ref,
                   ro_ref, wo_ref, ret_ref):
    t_len = DEC_SEQ
    q0 = pl.program_id(1) * TQ
    first = _lane_half_mask((TQ, PAIR_W))
    first_k = _lane_half_mask((t_len, PAIR_W))
    row = q0 + lax.broadcasted_iota(jnp.int32, (TQ, t_len), 0)
    col = lax.broadcasted_iota(jnp.int32, (TQ, t_len), 1)
    diff = (row - col).astype(f32)
    diag = jnp.where(row == col, 2.0 * QK_SCALE, QK_SCALE)
    t_q = (q0 + lax.broadcasted_iota(jnp.int32, (TQ, PAIR_W), 0)).astype(f32)
    for p in range(H_RET // 2):
        sl = slice(p * PAIR_W, (p + 1) * PAIR_W)
        q = rq_ref[:, sl].astype(bf16)
        kb = rk_ref[:, sl].astype(bf16)
        v = rv_ref[:, sl].astype(bf16)
        outs = []
        for e in range(2):
            h = 2 * p + e
            keep = first_k if e == 0 else jnp.logical_not(first_k)
            dmask = jnp.exp(jnp.where(diff >= 0, lg_ref[0, h] * diff, -lg_ref[1, h] * diff)) * diag
            s = _dot_nt(q, jnp.where(keep, kb, jnp.zeros_like(kb))) * dmask
            outs.append(_dot(s.astype(bf16), v))
        o = jnp.where(first, outs[0], outs[1])
        o = o + _dot(q, s0_ref[0, 0, p].astype(bf16)) * jnp.exp(lgf_ref[:, sl] * (t_q + 1.0))
        o = o + _dot(q, s0_ref[0, 1, p].astype(bf16)) * jnp.exp(lgb_ref[:, sl] * (t_len - t_q))
        ret_ref[:, sl] = o
    ro_ref[...] = _group_norm_gate(ret_ref[...], rg_ref[...], gmat_ref[...], gng_ref[...], gnb_ref[...]).astype(bf16)

    band = TQ + 2 * WINDOW
    k_start = pl.multiple_of(jnp.clip(q0 - WINDOW, 0, t_len - band), LANES)
    qi = q0 + lax.broadcasted_iota(jnp.int32, (TQ, band), 0)
    kj = k_start + lax.broadcasted_iota(jnp.int32, (TQ, band), 1)
    in_band = jnp.abs(qi - kj) <= WINDOW
    for j in range(KV_WIN):
        k_dup = _dup_head(wk_ref[pl.ds(k_start, band), :], j).astype(bf16)
        v_dup = _dup_head(wv_ref[pl.ds(k_start, band), :], j).astype(bf16)
        ck_dup = _dup_head(ck_ref[0], j).astype(bf16)
        cv_dup = _dup_head(cv_ref[0], j).astype(bf16)
        for pp in range(G_WIN // 2):
            c0 = (j * G_WIN + 2 * pp) * HEAD_DIM
            qb = wq_ref[:, c0:c0 + PAIR_W].astype(bf16)
            outs = []
            for e in range(2):
                keep = first if e == 0 else jnp.logical_not(first)
                qm = jnp.where(keep, qb, jnp.zeros_like(qb))
                s_band = jnp.where(in_band, _dot_nt(qm, k_dup) * QK_SCALE, NEG_BIG)
                s_ctx = _dot_nt(qm, ck_dup) * QK_SCALE
                (e_band, e_ctx), denom = _softmax_parts([s_band, s_ctx], sink_ref[0, j * G_WIN + 2 * pp + e])
                outs.append((_dot(e_band.astype(bf16), v_dup) + _dot(e_ctx.astype(bf16), cv_dup)) / denom)
            wo_ref[:, c0:c0 + PAIR_W] = jnp.where(first, outs[0], outs[1]).astype(bf16)


def _lat_ab(proj, log_gamma, sink, ck, cv, s0_pairs, lgf_lanes, lgb_lanes, gmat, gn_g, gn_b):
    t = DEC_SEQ
    nq = t // TQ
    smem = pl.BlockSpec(memory_space=pltpu.SMEM)
    const = lambda b, i: (0, 0)
    qcol = lambda c: (lambda b, i: (b * nq + i, c))
    bcol = lambda c: (lambda b, i: (b, c))
    kv_col = (4 * RET_W + WIN_W) // KV_W
    return pl.pallas_call(
        _lat_ab_kernel,
        grid=(DEC_BATCH, nq),
        in_specs=[smem, smem,
                  pl.BlockSpec((TQ, RET_W), qcol(0)), pl.BlockSpec((t, RET_W), bcol(1)),
                  pl.BlockSpec((t, RET_W), bcol(2)), pl.BlockSpec((TQ, RET_W), qcol(3)),
                  pl.BlockSpec((TQ, WIN_W), qcol(4)),
                  pl.BlockSpec((t, KV_W), bcol(kv_col)), pl.BlockSpec((t, KV_W), bcol(kv_col + 1)),
                  pl.BlockSpec((1, PAST_LEN, KV_W), lambda b, i: (b, 0, 0)),
                  pl.BlockSpec((1, PAST_LEN, KV_W), lambda b, i: (b, 0, 0)),
                  pl.BlockSpec((1, 2, H_RET // 2, PAIR_W, PAIR_W), lambda b, i: (b, 0, 0, 0, 0)),
                  pl.BlockSpec((1, RET_W), const), pl.BlockSpec((1, RET_W), const),
                  pl.BlockSpec((RET_W, RET_W), const),
                  pl.BlockSpec((1, RET_W), const), pl.BlockSpec((1, RET_W), const)],
        out_specs=[pl.BlockSpec((TQ, RET_W), lambda b, i: (b * nq + i, 0)),
                   pl.BlockSpec((TQ, WIN_W), lambda b, i: (b * nq + i, 0))],
        out_shape=[jax.ShapeDtypeStruct((DEC_BATCH * t, RET_W), bf16),
                   jax.ShapeDtypeStruct((DEC_BATCH * t, WIN_W), bf16)],
        scratch_shapes=[pltpu.VMEM((TQ, RET_W), f32)],
        compiler_params=_params(2),
        name="lat_ab",
    )(log_gamma, sink, proj, proj, proj, proj, proj, proj, proj, ck, cv, s0_pairs,
      lgf_lanes, lgb_lanes, gmat, gn_g, gn_b)


def _lambda_full(lam_ref, lam_init):
    lam = lam_ref[...]
    a = jnp.sum(lam[0:1, :] * lam[1:2, :], -1, keepdims=True)
    b = jnp.sum(lam[2:3, :] * lam[3:4, :], -1, keepdims=True)
    return jnp.exp(a) - jnp.exp(b) + lam_init


def _diff_head(q, k_parts, v_parts, lam, subln, lam_init):
    first = _lane_half_mask(k_parts[0].shape)
    outs = []
    for e in range(2):
        scores = []
        for k in k_parts:
            fm = _lane_half_mask(k.shape)
            keep = fm if e == 0 else jnp.logical_not(fm)
            scores.append(_dot_nt(q, jnp.where(keep, k, jnp.zeros_like(k))) * QK_SCALE)
        m = scores[0].max(-1, keepdims=True)
        for s in scores[1:]:
            m = jnp.maximum(m, s.max(-1, keepdims=True))
        es = [jnp.exp(s - m) for s in scores]
        denom = es[0].sum(-1, keepdims=True)
        for ex in es[1:]:
            denom = denom + ex.sum(-1, keepdims=True)
        pv = _dot(es[0].astype(bf16), v_parts[0])
        for ex, v in zip(es[1:], v_parts[1:]):
            pv = pv + _dot(ex.astype(bf16), v)
        outs.append(pv / denom)
    del first
    a = outs[0] - lam * outs[1]
    return a * lax.rsqrt(jnp.mean(a * a, -1, keepdims=True) + LN_EPS) * subln * (1.0 - lam_init)


def _fourier_rows(ct_ref, st_ref, z, bdc_ref, bds_ref):
    zb = z.astype(bf16)
    zc = _dot(zb, bdc_ref[...]).astype(bf16)
    zs = _dot(zb, bds_ref[...]).astype(bf16)
    return _dot(ct_ref[...], zc) - _dot(st_ref[...], zs)


def _ctx_cd_kernel(q_ref, k_ref, v_ref, z_ref, lam_ref, subln_ref, ct_ref, st_ref, bdc_ref, bds_ref,
                   a_ref, zf_ref, *, lam_init):
    lam = _lambda_full(lam_ref, lam_init)
    for h in range(H_DIFF):
        sl = slice(h * PAIR_W, (h + 1) * PAIR_W)
        a_ref[:, sl] = _diff_head(q_ref[:, sl].astype(bf16), [k_ref[:, sl].astype(bf16)],
                                  [v_ref[:, sl].astype(bf16)], lam, subln_ref[...], lam_init).astype(bf16)
    zf_ref[...] = _fourier_rows(ct_ref, st_ref, z_ref[...], bdc_ref, bds_ref).astype(bf16)


def _ctx_cd(proj, lam, subln, ct, st, bdc, bds, lam_init):
    t = SEQ
    const = lambda b: (0, 0)
    col = lambda c: (lambda b: (b, c))
    return pl.pallas_call(
        functools.partial(_ctx_cd_kernel, lam_init=lam_init),
        grid=(BATCH,),
        in_specs=[pl.BlockSpec((t, DIFF_W), col(0)), pl.BlockSpec((t, DIFF_W), col(1)),
                  pl.BlockSpec((t, DIFF_W), col(2)), pl.BlockSpec((t, FNET_W), col(3 * DIFF_W // FNET_W)),
                  pl.BlockSpec((4, HEAD_DIM), const), pl.BlockSpec((1, PAIR_W), const),
                  pl.BlockSpec((t, t), const), pl.BlockSpec((t, t), const),
                  pl.BlockSpec((FNET_W, FNET_W), const), pl.BlockSpec((FNET_W, FNET_W), const)],
        out_specs=[pl.BlockSpec((t, DIFF_W), lambda b: (b, 0)), pl.BlockSpec((t, FNET_W), lambda b: (b, 0))],
        out_shape=[jax.ShapeDtypeStruct((BATCH * t, DIFF_W), bf16),
                   jax.ShapeDtypeStruct((BATCH * t, FNET_W), bf16)],
        compiler_params=_params(1),
        name="ctx_cd",
    )(proj, proj, proj, proj, lam, subln, ct, st, bdc, bds)


def _lat_cd_kernel(q_ref, k_ref, v_ref, z_ref, ck_ref, cv_ref, lam_ref, subln_ref, ct_ref, st_ref, bdc_ref, bds_ref,
                   a_ref, zf_ref, *, lam_init):
    lam = _lambda_full(lam_ref, lam_init)
    for h in range(H_DIFF):
        sl = slice(h * PAIR_W, (h + 1) * PAIR_W)
        a_ref[:, sl] = _diff_head(q_ref[:, sl].astype(bf16),
                                  [k_ref[:, sl].astype(bf16), ck_ref[0, :, sl].astype(bf16)],
                                  [v_ref[:, sl].astype(bf16), cv_ref[0, :, sl].astype(bf16)],
                                  lam, subln_ref[...], lam_init).astype(bf16)
    zf_ref[...] = _fourier_rows(ct_ref, st_ref, z_ref[...], bdc_ref, bds_ref).astype(bf16)


def _lat_cd(proj, ck, cv, lam, subln, ct, st, bdc, bds, lam_init):
    t = DEC_SEQ
    nq = t // TQ
    const = lambda b, i: (0, 0)
    return pl.pallas_call(
        functools.partial(_lat_cd_kernel, lam_init=lam_init),
        grid=(DEC_BATCH, nq),
        in_specs=[pl.BlockSpec((TQ, DIFF_W), lambda b, i: (b * nq + i, 0)),
                  pl.BlockSpec((t, DIFF_W), lambda b, i: (b, 1)),
                  pl.BlockSpec((t, DIFF_W), lambda b, i: (b, 2)),
                  pl.BlockSpec((t, FNET_W), lambda b, i: (b, 3 * DIFF_W // FNET_W)),
                  pl.BlockSpec((1, PAST_LEN, DIFF_W), lambda b, i: (b, 0, 0)),
                  pl.BlockSpec((1, PAST_LEN, DIFF_W), lambda b, i: (b, 0, 0)),
                  pl.BlockSpec((4, HEAD_DIM), const), pl.BlockSpec((1, PAIR_W), const),
                  pl.BlockSpec((TQ, t), lambda b, i: (i, 0)), pl.BlockSpec((TQ, t), lambda b, i: (i, 0)),
                  pl.BlockSpec((FNET_W, FNET_W), const), pl.BlockSpec((FNET_W, FNET_W), const)],
        out_specs=[pl.BlockSpec((TQ, DIFF_W), lambda b, i: (b * nq + i, 0)),
                   pl.BlockSpec((TQ, FNET_W), lambda b, i: (b * nq + i, 0))],
        out_shape=[jax.ShapeDtypeStruct((DEC_BATCH * t, DIFF_W), bf16),
                   jax.ShapeDtypeStruct((DEC_BATCH * t, FNET_W), bf16)],
        compiler_params=_params(2),
        name="lat_cd",
    )(proj, proj, proj, proj, ck, cv, lam, subln, ct, st, bdc, bds)


def _rope_tables():
    t = jnp.arange(DEC_SEQ)
    quarter = HEAD_DIM // 4
    inv = ROPE_BASE ** (-jnp.arange(quarter, dtype=f32) / quarter)
    ang = jnp.concatenate([(t // GRID_W).astype(f32)[:, None] * inv, (t % GRID_W).astype(f32)[:, None] * inv], -1)
    cos, sin = jnp.cos(ang), jnp.sin(ang)
    reps = LANES // HEAD_DIM
    return jnp.tile(jnp.concatenate([cos, cos], -1), (1, reps)), jnp.tile(jnp.concatenate([-sin, sin], -1), (1, reps))


def _dft_tables(n):
    k = jnp.arange(n)
    ang = (2.0 * math.pi / n) * ((k[:, None] * k[None, :]) % n).astype(f32)
    return jnp.cos(ang) / math.sqrt(n), jnp.sin(ang) / math.sqrt(n)


def _block_diag(m, reps):
    return jnp.kron(jnp.eye(reps, dtype=m.dtype), m)


def kernel(x_prompt, x_sample, state_ret, cache_win_k, cache_win_v, cache_diff_k, cache_diff_v, c, c_ctx, w_mod, b_mod, ln_g, ln_b, w_in_ab, w_out_ab, ret_log_gamma, ret_gn_g, ret_gn_b, win_sink, w_in_cd, w_out_cd, diff_lambda, diff_subln_g, w_gate, w_up, w_down):
    n_ctx = BATCH * SEQ
    n_lat = DEC_BATCH * DEC_SEQ

    cond = jnp.concatenate([c_ctx[None, :], c, jnp.zeros((8 - 1 - DEC_BATCH, D_MODEL), f32)], 0)
    mod = _modulation(cond, w_mod, b_mod).reshape(DEPTH, 8, 6, D_MODEL)

    rope_tabs = _rope_tables()
    gmat = _block_diag(jnp.full((HEAD_DIM, HEAD_DIM), 1.0 / HEAD_DIM, f32), H_RET).astype(bf16)
    c64, s64 = _dft_tables(FNET_DIM)
    bdc = _block_diag(c64, FNET_GROUPS).astype(bf16)
    bds = _block_diag(s64, FNET_GROUPS).astype(bf16)
    dft_ctx = tuple(m.astype(bf16) for m in _dft_tables(SEQ))
    dft_lat = tuple(m.astype(bf16) for m in _dft_tables(DEC_SEQ))

    xs = [x_prompt.reshape(n_ctx, D_MODEL), x_sample.reshape(n_lat, D_MODEL)]
    rows_per_mod = [n_ctx, DEC_SEQ]
    outs = {}
    for l in range(DEPTH):
        i = l // 2
        mods = [mod[l, 0:1], mod[l, 1:1 + DEC_BATCH]]
        if l % 2 == 0:
            w_in = w_in_ab[i].astype(bf16)
            w_out = w_out_ab[i].astype(bf16)
            lgf = jnp.repeat(ret_log_gamma[i, 0], HEAD_DIM)[None, :]
            lgb = jnp.repeat(ret_log_gamma[i, 1], HEAD_DIM)[None, :]
            gn_g = ret_gn_g[i][None, :]
            gn_b = ret_gn_b[i][None, :]
            sink = win_sink[i][None, :]
            rope_tiles = tuple(range(0, 2 * RET_W // LANES)) + tuple(
                range(4 * RET_W // LANES, (4 * RET_W + WIN_W + KV_W) // LANES))
            proj_ctx = _proj(xs[0], mods[0], w_in, rows_per_mod[0])
            proj_lat = _proj(xs[1], mods[1], w_in, rows_per_mod[1], rope_tabs, rope_tiles)
            ro_c, wo_c, st_c = _ctx_ab(proj_ctx, ret_log_gamma[i], sink, lgf, lgb, gmat, gn_g, gn_b)
            s0 = state_ret[:, i]
            s0 = s0.reshape(DEC_BATCH, 2, H_RET // 2, 2, HEAD_DIM, HEAD_DIM)
            eye2 = jnp.eye(2, dtype=f32)
            s0_pairs = jnp.einsum('bdpeij,ef->bdpeifj', s0, eye2).reshape(
                DEC_BATCH, 2, H_RET // 2, PAIR_W, PAIR_W)
            ck = cache_win_k[:, i].reshape(DEC_BATCH, PAST_LEN, KV_W)
            cv = cache_win_v[:, i].reshape(DEC_BATCH, PAST_LEN, KV_W)
            ro_l, wo_l = _lat_ab(proj_lat, ret_log_gamma[i], sink, ck, cv, s0_pairs, lgf, lgb, gmat, gn_g, gn_b)
            mixed = [(ro_c, wo_c), (ro_l, wo_l)]
            outs.setdefault('state', []).append(st_c)
            kv0 = 4 * RET_W + WIN_W
            outs.setdefault('win_k', []).append(proj_ctx[:, kv0:kv0 + KV_W].reshape(BATCH, SEQ, KV_WIN, HEAD_DIM))
            outs.setdefault('win_v', []).append(
                proj_ctx[:, kv0 + KV_W:kv0 + 2 * KV_W].reshape(BATCH, SEQ, KV_WIN, HEAD_DIM))
        else:
            w_in = w_in_cd[i].astype(bf16)
            w_out = w_out_cd[i].astype(bf16)
            lam_init = 0.8 - 0.6 * math.exp(-0.3 * l)
            subln = diff_subln_g[i][None, :]
            rope_tiles = tuple(range(0, 2 * DIFF_W // LANES))
            proj_ctx = _proj(xs[0], mods[0], w_in, rows_per_mod[0])
            proj_lat = _proj(xs[1], mods[1], w_in, rows_per_mod[1], rope_tabs, rope_tiles)
            a_c, z_c = _ctx_cd(proj_ctx, diff_lambda[i], subln, dft_ctx[0], dft_ctx[1], bdc, bds, lam_init)
            ck = cache_diff_k[:, i].reshape(DEC_BATCH, PAST_LEN, DIFF_W)
            cv = cache_diff_v[:, i].reshape(DEC_BATCH, PAST_LEN, DIFF_W)
            a_l, z_l = _lat_cd(proj_lat, ck, cv, diff_lambda[i], subln, dft_lat[0], dft_lat[1], bdc, bds, lam_init)
            mixed = [(a_c, z_c), (a_l, z_l)]
            outs.setdefault('diff_k', []).append(
                proj_ctx[:, DIFF_W:2 * DIFF_W].reshape(BATCH, SEQ, H_DIFF, 2, HEAD_DIM))
            outs.setdefault('diff_v', []).append(
                proj_ctx[:, 2 * DIFF_W:3 * DIFF_W].reshape(BATCH, SEQ, H_DIFF, 2 * HEAD_DIM))
        wg, wu, wd = w_gate[l].astype(bf16), w_up[l].astype(bf16), w_down[l].astype(bf16)
        xs = [_post(xs[p], mixed[p][0], mixed[p][1], mods[p], ln_g[l], ln_b[l], w_out, wg, wu, wd, rows_per_mod[p])
              for p in range(2)]

    y_prompt = xs[0].reshape(BATCH, SEQ, D_MODEL)
    y_sample = xs[1].reshape(DEC_BATCH, DEC_SEQ, D_MODEL)
    return (y_prompt, y_sample, jnp.stack(outs['state'], 1), jnp.stack(outs['win_k'], 1),
            jnp.stack(outs['win_v'], 1), jnp.stack(outs['diff_k'], 1), jnp.stack(outs['diff_v'], 1))
```

```python
import functools
import math

import jax
import jax.numpy as jnp
import numpy as np
from jax import lax
from jax.experimental import pallas as pl
from jax.experimental.pallas import tpu as pltpu

D_MODEL = 1024
BATCH = 32
SEQ = 256
DEPTH = 2
DEC_BATCH = 2
DEC_SEQ = 1024
PAST_LEN = 512
GRID_W = 64
HEAD_DIM = 64
ROPE_BASE = 10000.0
H_RET = 8
H_WIN = 8
KV_WIN = 2
G_WIN = H_WIN // KV_WIN
WINDOW = 128
H_DIFF = 6
FNET_GROUPS = 4
FNET_DIM = 64
D_FF = 256 * math.ceil(8 * D_MODEL / 3 / 256)
RET_W = H_RET * HEAD_DIM
WIN_W = H_WIN * HEAD_DIM
KV_W = KV_WIN * HEAD_DIM
AB_IN = 4 * RET_W + WIN_W + 2 * KV_W
DIFF_W = H_DIFF * 2 * HEAD_DIM
FNET_W = FNET_GROUPS * FNET_DIM
CD_IN = 3 * DIFF_W + FNET_W
ALPHA = (2 * DEPTH) ** 0.25
LN_EPS = 1e-5
QK_SCALE = HEAD_DIM ** -0.5

LANES = 128
PAIR_W = 2 * HEAD_DIM
TM = 512
TQ = 256
N_CHUNK = 256
NEG_BIG = -1e30
VMEM_LIMIT = 56 * 1024 * 1024

f32 = jnp.float32
bf16 = jnp.bfloat16


def _params(n_axes):
    return pltpu.CompilerParams(dimension_semantics=("arbitrary",) * n_axes,
                                vmem_limit_bytes=VMEM_LIMIT)


def _dot(a, b):
    return jnp.dot(a, b, preferred_element_type=f32)


def _dot_nt(a, b):
    return lax.dot_general(a, b, (((1,), (1,)), ((), ())), preferred_element_type=f32)


def _ln(x):
    mu = jnp.mean(x, -1, keepdims=True)
    d = x - mu
    var = jnp.mean(d * d, -1, keepdims=True)
    return d * lax.rsqrt(var + LN_EPS)


def _silu(x):
    return x * jax.nn.sigmoid(x)


def _split_bf16(x):
    hi = x.astype(bf16)
    lo = (x - hi.astype(f32)).astype(bf16)
    return hi, lo


def _lane_half_mask(shape):
    return (lax.broadcasted_iota(jnp.int32, shape, len(shape) - 1) & HEAD_DIM) == 0


def _mod_kernel(c_ref, w_ref, b_ref, o_ref):
    a = _silu(c_ref[...])
    a_hi, a_lo = _split_bf16(a)
    w_hi, w_lo = _split_bf16(w_ref[0])
    acc = _dot(a_hi, w_hi) + _dot(a_lo, w_hi) + _dot(a_hi, w_lo)
    o_ref[0] = acc + b_ref[0]


def _modulation(cond, w_mod, b_mod):
    tn = 1536
    rows = cond.shape[0]
    return pl.pallas_call(
        _mod_kernel,
        grid=(DEPTH, 6 * D_MODEL // tn),
        in_specs=[pl.BlockSpec((rows, D_MODEL), lambda l, j: (0, 0)),
                  pl.BlockSpec((1, D_MODEL, tn), lambda l, j: (l, 0, j)),
                  pl.BlockSpec((1, 1, tn), lambda l, j: (l, 0, j))],
        out_specs=pl.BlockSpec((1, rows, tn), lambda l, j: (l, 0, j)),
        out_shape=jax.ShapeDtypeStruct((DEPTH, rows, 6 * D_MODEL), f32),
        compiler_params=_params(2),
        name="modulation",
    )(cond, w_mod, b_mod.reshape(DEPTH, 1, 6 * D_MODEL))


def _rope_pair(y, cos, sin_signed):
    first_half = (lax.broadcasted_iota(jnp.int32, y.shape, 1) & (HEAD_DIM // 2)) == 0
    swapped = jnp.where(first_half, pltpu.roll(y, LANES - HEAD_DIM // 2, 1), pltpu.roll(y, HEAD_DIM // 2, 1))
    return y * cos + swapped * sin_signed


def _proj_kernel(*refs, n_out, rope_tiles, cache_plan):
    n_in = 5 if rope_tiles else 3
    x_ref, mod_ref, w_ref = refs[:3]
    o_ref = refs[n_in]
    cache_refs = refs[n_in + 1:-1]
    u_ref = refs[-1]
    shift = mod_ref[0, 0:1, :]
    scale = mod_ref[0, 1:2, :]
    u_ref[...] = (_ln(x_ref[...]) * (1.0 + scale) + shift).astype(bf16)
    for j in range(0, n_out, N_CHUNK):
        y = _dot(u_ref[...], w_ref[:, j:j + N_CHUNK])
        for t in range(N_CHUNK // LANES):
            tile = j // LANES + t
            piece = y[:, t * LANES:(t + 1) * LANES]
            if tile in rope_tiles:
                piece = _rope_pair(piece, refs[3][...], refs[4][...])
            o_ref[:, tile * LANES:(tile + 1) * LANES] = piece
            if tile in cache_plan:
                kind, out_idx, slot = cache_plan[tile]
                c_ref = cache_refs[out_idx]
                for b in range(TM // SEQ):
                    rows = piece[b * SEQ:(b + 1) * SEQ, :]
                    if kind == "plain":
                        c_ref[b, 0, slot] = rows
                    else:
                        rows_t = rows.T
                        if kind == "heads":
                            c_ref[b, 0, 0] = rows_t[0:HEAD_DIM]
                            c_ref[b, 0, 1] = rows_t[HEAD_DIM:]
                        else:
                            c_ref[b, 0, slot, 0] = rows_t[0:HEAD_DIM]
                            c_ref[b, 0, slot, 1] = rows_t[HEAD_DIM:]


def _proj(x, mod, w, rows_per_mod, rope_tabs=None, rope_tiles=(), cache_shapes=(), cache_plan=None):
    n = x.shape[0]
    n_out = w.shape[1]
    in_specs = [pl.BlockSpec((TM, D_MODEL), lambda i: (i, 0)),
                pl.BlockSpec((1, 6, D_MODEL), lambda i: ((i * TM) // rows_per_mod, 0, 0)),
                pl.BlockSpec((D_MODEL, n_out), lambda i: (0, 0))]
    args = [x, mod, w]
    if rope_tiles:
        nb = DEC_SEQ // TM
        in_specs += [pl.BlockSpec((TM, LANES), lambda i: (i % nb, 0)),
                     pl.BlockSpec((TM, LANES), lambda i: (i % nb, 0))]
        args += list(rope_tabs)
    out_specs = [pl.BlockSpec((TM, n_out), lambda i: (i, 0))]
    out_shape = [jax.ShapeDtypeStruct((n, n_out), f32)]
    for shp in cache_shapes:
        blk = (TM // SEQ,) + tuple(shp[1:])
        out_specs.append(pl.BlockSpec(blk, lambda i, nd=len(shp): (i,) + (0,) * (nd - 1)))
        out_shape.append(jax.ShapeDtypeStruct(tuple(shp), f32))
    return pl.pallas_call(
        functools.partial(_proj_kernel, n_out=n_out, rope_tiles=frozenset(rope_tiles),
                          cache_plan=dict(cache_plan or {})),
        grid=(n // TM,),
        in_specs=in_specs,
        out_specs=out_specs,
        out_shape=out_shape,
        scratch_shapes=[pltpu.VMEM((TM, D_MODEL), bf16)],
        compiler_params=_params(1),
        name="proj",
    )(*args)


def _post_kernel(x_ref, a_ref, b_ref, mod_ref, lng_ref, lnb_ref, wo_ref, wg_ref, wu_ref, wd_ref,
                 o_ref, u_ref, h_ref, *, ka):
    gate1 = mod_ref[0, 2:3, :]
    shift2 = mod_ref[0, 3:4, :]
    scale2 = mod_ref[0, 4:5, :]
    gate2 = mod_ref[0, 5:6, :]
    h = _dot(a_ref[...], wo_ref[0:ka, :]) + _dot(b_ref[...], wo_ref[ka:, :])
    x1 = _ln(ALPHA * x_ref[...] + gate1 * h) * lng_ref[0:1, :] + lnb_ref[0:1, :]
    u_ref[...] = (_ln(x1) * (1.0 + scale2) + shift2).astype(bf16)
    for j in range(0, D_FF, N_CHUNK):
        g = _dot(u_ref[...], wg_ref[:, j:j + N_CHUNK])
        up = _dot(u_ref[...], wu_ref[:, j:j + N_CHUNK])
        h_ref[:, j:j + N_CHUNK] = (_silu(g) * up).astype(bf16)
    ffn = _dot(h_ref[...], wd_ref[...])
    o_ref[...] = _ln(ALPHA * x1 + gate2 * ffn) * lng_ref[1:2, :] + lnb_ref[1:2, :]


def _post(x, a, b, mod, ln_g, ln_b, w_out, w_gate, w_up, w_down, rows_per_mod):
    n = x.shape[0]
    ka, kb = a.shape[1], b.shape[1]
    const = lambda i: (0, 0)
    return pl.pallas_call(
        functools.partial(_post_kernel, ka=ka),
        grid=(n // TM,),
        in_specs=[pl.BlockSpec((TM, D_MODEL), lambda i: (i, 0)),
                  pl.BlockSpec((TM, ka), lambda i: (i, 0)),
                  pl.BlockSpec((TM, kb), lambda i: (i, 0)),
                  pl.BlockSpec((1, 6, D_MODEL), lambda i: ((i * TM) // rows_per_mod, 0, 0)),
                  pl.BlockSpec((2, D_MODEL), const),
                  pl.BlockSpec((2, D_MODEL), const),
                  pl.BlockSpec((ka + kb, D_MODEL), const, pipeline_mode=pl.Buffered(1)),
                  pl.BlockSpec((D_MODEL, D_FF), const, pipeline_mode=pl.Buffered(1)),
                  pl.BlockSpec((D_MODEL, D_FF), const, pipeline_mode=pl.Buffered(1)),
                  pl.BlockSpec((D_FF, D_MODEL), const, pipeline_mode=pl.Buffered(1))],
        out_specs=pl.BlockSpec((TM, D_MODEL), lambda i: (i, 0)),
        out_shape=jax.ShapeDtypeStruct((n, D_MODEL), f32),
        scratch_shapes=[pltpu.VMEM((TM, D_MODEL), bf16), pltpu.VMEM((TM, D_FF), bf16)],
        compiler_params=_params(1),
        name="post",
    )(x, a, b, mod, ln_g, ln_b, w_out, w_gate, w_up, w_down)


def _group_norm_gate(ro, rg, gmat, gn_g, gn_b):
    def gmean(v):
        hi, lo = _split_bf16(v)
        return _dot(hi, gmat) + _dot(lo, gmat)

    d = ro - gmean(ro)
    var = gmean(d * d)
    y = d * lax.rsqrt(var + LN_EPS) * gn_g + gn_b
    return _silu(rg) * y


def _dup_head(x, j):
    first = _lane_half_mask(x.shape)
    keep = first if j == 0 else jnp.logical_not(first)
    xm = jnp.where(keep, x, 0.0)
    return xm + pltpu.roll(xm, HEAD_DIM, 1)


def _softmax_parts(scores, sink):
    m = sink
    for s in scores:
        m = jnp.maximum(m, jnp.max(s, -1, keepdims=True))
    es = [jnp.exp(s - m) for s in scores]
    denom = jnp.exp(sink - m)
    for e in es:
        denom = denom + jnp.sum(e, -1, keepdims=True)
    return es, denom


def _ctx_ab_kernel(lg_ref, sink_ref, rq_ref, rk_ref, rv_ref, rg_ref, wq_ref, wk_ref, wv_ref,
                   lgf_ref, lgb_ref, gmat_ref, gng_ref, gnb_ref,
                   ro_ref, wo_ref, st_ref, dmask_ref, kdec_ref, ret_ref):
    t_len = SEQ

    @pl.when(pl.program_id(0) == 0)
    def _():
        row = lax.broadcasted_iota(jnp.int32, (t_len, t_len), 0)
        col = lax.broadcasted_iota(jnp.int32, (t_len, t_len), 1)
        diff = (row - col).astype(f32)
        diag = jnp.where(row == col, 2.0 * QK_SCALE, QK_SCALE)
        for h in range(H_RET):
            dmask_ref[h] = jnp.exp(jnp.where(diff >= 0, lg_ref[0, h] * diff, -lg_ref[1, h] * diff)) * diag
        t = lax.broadcasted_iota(jnp.int32, (t_len, RET_W), 0).astype(f32)
        kdec_ref[0] = jnp.exp(lgf_ref[...] * (t_len - 1.0 - t)) * QK_SCALE
        kdec_ref[1] = jnp.exp(lgb_ref[...] * t) * QK_SCALE

    first = _lane_half_mask((t_len, PAIR_W))
    for p in range(H_RET // 2):
        sl = slice(p * PAIR_W, (p + 1) * PAIR_W)
        q = rq_ref[:, sl].astype(bf16)
        k = rk_ref[:, sl]
        kb = k.astype(bf16)
        v = rv_ref[:, sl].astype(bf16)
        outs = []
        for e in range(2):
            keep = first if e == 0 else jnp.logical_not(first)
            s = _dot_nt(q, jnp.where(keep, kb, jnp.zeros_like(kb))) * dmask_ref[2 * p + e]
            outs.append(_dot(s.astype(bf16), v))
        ret_ref[:, sl] = jnp.where(first, outs[0], outs[1])
        for d in range(2):
            kd_t = (k * kdec_ref[d, :, sl]).T.astype(bf16)
            st = _dot(kd_t, v)
            st_ref[0, d, 2 * p] = st[0:HEAD_DIM, 0:HEAD_DIM]
            st_ref[0, d, 2 * p + 1] = pltpu.roll(st[HEAD_DIM:, :], HEAD_DIM, 1)[:, 0:HEAD_DIM]
    ro_ref[...] = _group_norm_gate(ret_ref[...], rg_ref[...], gmat_ref[...], gng_ref[...], gnb_ref[...]).astype(bf16)

    for j in range(KV_WIN):
        k_dup = _dup_head(wk_ref[...], j).astype(bf16)
        v_dup = _dup_head(wv_ref[...], j).astype(bf16)
        for pp in range(G_WIN // 2):
            col = (j * G_WIN + 2 * pp) * HEAD_DIM
            qb = wq_ref[:, col:col + PAIR_W].astype(bf16)
            outs = []
            for e in range(2):
                keep = first if e == 0 else jnp.logical_not(first)
                s = _dot_nt(jnp.where(keep, qb, jnp.zeros_like(qb)), k_dup) * QK_SCALE
                (es,), denom = _softmax_parts([s], sink_ref[0, j * G_WIN + 2 * pp + e])
                outs.append(_dot(es.astype(bf16), v_dup) / denom)
            wo_ref[:, col:col + PAIR_W] = jnp.where(first, outs[0], outs[1]).astype(bf16)


def _ctx_ab(proj, log_gamma, sink, lgf_lanes, lgb_lanes, gmat, gn_g, gn_b):
    t = SEQ
    smem = pl.BlockSpec(memory_space=pltpu.SMEM)
    const = lambda b: (0, 0)
    col = lambda c: (lambda b: (b, c))
    return pl.pallas_call(
        _ctx_ab_kernel,
        grid=(BATCH,),
        in_specs=[smem, smem,
                  pl.BlockSpec((t, RET_W), col(0)), pl.BlockSpec((t, RET_W), col(1)),
                  pl.BlockSpec((t, RET_W), col(2)), pl.BlockSpec((t, RET_W), col(3)),
                  pl.BlockSpec((t, WIN_W), col(4)),
                  pl.BlockSpec((t, KV_W), col((4 * RET_W + WIN_W) // KV_W)),
                  pl.BlockSpec((t, KV_W), col((4 * RET_W + WIN_W) // KV_W + 1)),
                  pl.BlockSpec((1, RET_W), const), pl.BlockSpec((1, RET_W), const),
                  pl.BlockSpec((RET_W, RET_W), const),
                  pl.BlockSpec((1, RET_W), const), pl.BlockSpec((1, RET_W), const)],
        out_specs=[pl.BlockSpec((t, RET_W), lambda b: (b, 0)),
                   pl.BlockSpec((t, WIN_W), lambda b: (b, 0)),
                   pl.BlockSpec((1, 2, H_RET, HEAD_DIM, HEAD_DIM), lambda b: (b, 0, 0, 0, 0))],
        out_shape=[jax.ShapeDtypeStruct((BATCH * t, RET_W), bf16),
                   jax.ShapeDtypeStruct((BATCH * t, WIN_W), bf16),
                   jax.ShapeDtypeStruct((BATCH, 2, H_RET, HEAD_DIM, HEAD_DIM), f32)],
        scratch_shapes=[pltpu.VMEM((H_RET, t, t), f32), pltpu.VMEM((2, t, RET_W), f32),
                        pltpu.VMEM((t, RET_W), f32)],
        compiler_params=_params(1),
        name="ctx_ab",
    )(log_gamma, sink, proj, proj, proj, proj, proj, proj, proj, lgf_lanes, lgb_lanes, gmat, gn_g, gn_b)


def _lat_ab_kernel(lg_ref, sink_ref, rq_ref, rk_ref, rv_ref, rg_ref, wq_ref, wk_ref, wv_ref, ck_ref, cv_ref,
                   s0_ref, lgf_ref, lgb_ref, gmat_ref, gng_ref, gnb_ref,
                   ro_ref, wo_ref, ret_ref):
    t_len = DEC_SEQ
    q0 = pl.program_id(1) * TQ
    first = _lane_half_mask((TQ, PAIR_W))
    first_k = _lane_half_mask((t_len, PAIR_W))
    row = q0 + lax.broadcasted_iota(jnp.int32, (TQ, t_len), 0)
    col = lax.broadcasted_iota(jnp.int32, (TQ, t_len), 1)
    diff = (row - col).astype(f32)
    diag = jnp.where(row == col, 2.0 * QK_SCALE, QK_SCALE)
    t_q = (q0 + lax.broadcasted_iota(jnp.int32, (TQ, PAIR_W), 0)).astype(f32)
    for p in range(H_RET // 2):
        sl = slice(p * PAIR_W, (p + 1) * PAIR_W)
        q = rq_ref[:, sl].astype(bf16)
        kb = rk_ref[:, sl].astype(bf16)
        v = rv_ref[:, sl].astype(bf16)
        outs = []
        for e in range(2):
            h = 2 * p + e
            keep = first_k if e == 0 else jnp.logical_not(first_k)
            dmask = jnp.exp(jnp.where(diff >= 0, lg_ref[0, h] * diff, -lg_ref[1, h] * diff)) * diag
            s = _dot_nt(q, jnp.where(keep, kb, jnp.zeros_like(kb))) * dmask
            outs.append(_dot(s.astype(bf16), v))
        o = jnp.where(first, outs[0], outs[1])
        o = o + _dot(q, s0_ref[0, 0, p].astype(bf16)) * jnp.exp(lgf_ref[:, sl] * (t_q + 1.0))
        o = o + _dot(q, s0_ref[0, 1, p].astype(bf16)) * jnp.exp(lgb_ref[:, sl] * (t_len - t_q))
        ret_ref[:, sl] = o
    ro_ref[...] = _group_norm_gate(ret_ref[...], rg_ref[...], gmat_ref[...], gng_ref[...], gnb_ref[...]).astype(bf16)

    band = TQ + 2 * WINDOW
    k_start = pl.multiple_of(jnp.clip(q0 - WINDOW, 0, t_len - band), LANES)
    qi = q0 + lax.broadcasted_iota(jnp.int32, (TQ, band), 0)
    kj = k_start + lax.broadcasted_iota(jnp.int32, (TQ, band), 1)
    in_band = jnp.abs(qi - kj) <= WINDOW
    for j in range(KV_WIN):
        k_dup = _dup_head(wk_ref[pl.ds(k_start, band), :], j).astype(bf16)
        v_dup = _dup_head(wv_ref[pl.ds(k_start, band), :], j).astype(bf16)
        ck_dup = _dup_head(ck_ref[0], j).astype(bf16)
        cv_dup = _dup_head(cv_ref[0], j).astype(bf16)
        for pp in range(G_WIN // 2):
            c0 = (j * G_WIN + 2 * pp) * HEAD_DIM
            qb = wq_ref[:, c0:c0 + PAIR_W].astype(bf16)
            outs = []
            for e in range(2):
                keep = first if e == 0 else jnp.logical_not(first)
                qm = jnp.where(keep, qb, jnp.zeros_like(qb))
                s_band = jnp.where(in_band, _dot_nt(qm, k_dup) * QK_SCALE, NEG_BIG)
                s_ctx = _dot_nt(qm, ck_dup) * QK_SCALE
                (e_band, e_ctx), denom = _softmax_parts([s_band, s_ctx], sink_ref[0, j * G_WIN + 2 * pp + e])
                outs.append((_dot(e_band.astype(bf16), v_dup) + _dot(e_ctx.astype(bf16), cv_dup)) / denom)
            wo_ref[:, c0:c0 + PAIR_W] = jnp.where(first, outs[0], outs[1]).astype(bf16)


def _lat_ab(proj, log_gamma, sink, ck, cv, s0_pairs, lgf_lanes, lgb_lanes, gmat, gn_g, gn_b):
    t = DEC_SEQ
    nq = t // TQ
    smem = pl.BlockSpec(memory_space=pltpu.SMEM)
    const = lambda b, i: (0, 0)
    qcol = lambda c: (lambda b, i: (b * nq + i, c))
    bcol = lambda c: (lambda b, i: (b, c))
    kv_col = (4 * RET_W + WIN_W) // KV_W
    return pl.pallas_call(
        _lat_ab_kernel,
        grid=(DEC_BATCH, nq),
        in_specs=[smem, smem,
                  pl.BlockSpec((TQ, RET_W), qcol(0)), pl.BlockSpec((t, RET_W), bcol(1)),
                  pl.BlockSpec((t, RET_W), bcol(2)), pl.BlockSpec((TQ, RET_W), qcol(3)),
                  pl.BlockSpec((TQ, WIN_W), qcol(4)),
                  pl.BlockSpec((t, KV_W), bcol(kv_col)), pl.BlockSpec((t, KV_W), bcol(kv_col + 1)),
                  pl.BlockSpec((1, PAST_LEN, KV_W), lambda b, i: (b, 0, 0)),
                  pl.BlockSpec((1, PAST_LEN, KV_W), lambda b, i: (b, 0, 0)),
                  pl.BlockSpec((1, 2, H_RET // 2, PAIR_W, PAIR_W), lambda b, i: (b, 0, 0, 0, 0)),
                  pl.BlockSpec((1, RET_W), const), pl.BlockSpec((1, RET_W), const),
                  pl.BlockSpec((RET_W, RET_W), const),
                  pl.BlockSpec((1, RET_W), const), pl.BlockSpec((1, RET_W), const)],
        out_specs=[pl.BlockSpec((TQ, RET_W), lambda b, i: (b * nq + i, 0)),
                   pl.BlockSpec((TQ, WIN_W), lambda b, i: (b * nq + i, 0))],
        out_shape=[jax.ShapeDtypeStruct((DEC_BATCH * t, RET_W), bf16),
                   jax.ShapeDtypeStruct((DEC_BATCH * t, WIN_W), bf16)],
        scratch_shapes=[pltpu.VMEM((TQ, RET_W), f32)],
        compiler_params=_params(2),
        name="lat_ab",
    )(log_gamma, sink, proj, proj, proj, proj, proj, proj, proj, ck, cv, s0_pairs,
      lgf_lanes, lgb_lanes, gmat, gn_g, gn_b)


def _lambda_full(lam_ref, lam_init):
    lam = lam_ref[...]
    a = jnp.sum(lam[0:1, :] * lam[1:2, :], -1, keepdims=True)
    b = jnp.sum(lam[2:3, :] * lam[3:4, :], -1, keepdims=True)
    return jnp.exp(a) - jnp.exp(b) + lam_init


def _diff_head(q, k_parts, v_parts, lam, subln, lam_init):
    first = _lane_half_mask(k_parts[0].shape)
    outs = []
    for e in range(2):
        scores = []
        for k in k_parts:
            fm = _lane_half_mask(k.shape)
            keep = fm if e == 0 else jnp.logical_not(fm)
            scores.append(_dot_nt(q, jnp.where(keep, k, jnp.zeros_like(k))) * QK_SCALE)
        m = scores[0].max(-1, keepdims=True)
        for s in scores[1:]:
            m = jnp.maximum(m, s.max(-1, keepdims=True))
        es = [jnp.exp(s - m) for s in scores]
        denom = es[0].sum(-1, keepdims=True)
        for ex in es[1:]:
            denom = denom + ex.sum(-1, keepdims=True)
        pv = _dot(es[0].astype(bf16), v_parts[0])
        for ex, v in zip(es[1:], v_parts[1:]):
            pv = pv + _dot(ex.astype(bf16), v)
        outs.append(pv / denom)
    del first
    a = outs[0] - lam * outs[1]
    return a * lax.rsqrt(jnp.mean(a * a, -1, keepdims=True) + LN_EPS) * subln * (1.0 - lam_init)


def _fourier_rows(ct_ref, st_ref, z, bdc_ref, bds_ref):
    zb = z.astype(bf16)
    zc = _dot(zb, bdc_ref[...].astype(bf16)).astype(bf16)
    zs = _dot(zb, bds_ref[...].astype(bf16)).astype(bf16)
    return _dot(ct_ref[...].astype(bf16), zc) - _dot(st_ref[...].astype(bf16), zs)


def _ctx_cd_kernel(q_ref, k_ref, v_ref, z_ref, lam_ref, subln_ref, ct_ref, st_ref, bdc_ref, bds_ref,
                   a_ref, zf_ref, *, lam_init):
    lam = _lambda_full(lam_ref, lam_init)
    for h in range(H_DIFF):
        sl = slice(h * PAIR_W, (h + 1) * PAIR_W)
        a_ref[:, sl] = _diff_head(q_ref[:, sl].astype(bf16), [k_ref[:, sl].astype(bf16)],
                                  [v_ref[:, sl].astype(bf16)], lam, subln_ref[...], lam_init).astype(bf16)
    zf_ref[...] = _fourier_rows(ct_ref, st_ref, z_ref[...], bdc_ref, bds_ref).astype(bf16)


def _ctx_cd(proj, lam, subln, ct, st, bdc, bds, lam_init):
    t = SEQ
    const = lambda b: (0, 0)
    col = lambda c: (lambda b: (b, c))
    return pl.pallas_call(
        functools.partial(_ctx_cd_kernel, lam_init=lam_init),
        grid=(BATCH,),
        in_specs=[pl.BlockSpec((t, DIFF_W), col(0)), pl.BlockSpec((t, DIFF_W), col(1)),
                  pl.BlockSpec((t, DIFF_W), col(2)), pl.BlockSpec((t, FNET_W), col(3 * DIFF_W // FNET_W)),
                  pl.BlockSpec((4, HEAD_DIM), const), pl.BlockSpec((1, PAIR_W), const),
                  pl.BlockSpec((t, t), const), pl.BlockSpec((t, t), const),
                  pl.BlockSpec((FNET_W, FNET_W), const), pl.BlockSpec((FNET_W, FNET_W), const)],
        out_specs=[pl.BlockSpec((t, DIFF_W), lambda b: (b, 0)), pl.BlockSpec((t, FNET_W), lambda b: (b, 0))],
        out_shape=[jax.ShapeDtypeStruct((BATCH * t, DIFF_W), bf16),
                   jax.ShapeDtypeStruct((BATCH * t, FNET_W), bf16)],
        compiler_params=_params(1),
        name="ctx_cd",
    )(proj, proj, proj, proj, lam, subln, ct, st, bdc, bds)


def _lat_cd_kernel(q_ref, k_ref, v_ref, z_ref, ck_ref, cv_ref, lam_ref, subln_ref, ct_ref, st_ref, bdc_ref, bds_ref,
                   a_ref, zf_ref, *, lam_init):
    lam = _lambda_full(lam_ref, lam_init)
    for h in range(H_DIFF):
        sl = slice(h * PAIR_W, (h + 1) * PAIR_W)
        a_ref[:, sl] = _diff_head(q_ref[:, sl].astype(bf16),
                                  [k_ref[:, sl].astype(bf16), ck_ref[0, :, sl].astype(bf16)],
                                  [v_ref[:, sl].astype(bf16), cv_ref[0, :, sl].astype(bf16)],
                                  lam, subln_ref[...], lam_init).astype(bf16)
    zf_ref[...] = _fourier_rows(ct_ref, st_ref, z_ref[...], bdc_ref, bds_ref).astype(bf16)


def _lat_cd(proj, ck, cv, lam, subln, ct, st, bdc, bds, lam_init):
    t = DEC_SEQ
    nq = t // TQ
    const = lambda b, i: (0, 0)
    return pl.pallas_call(
        functools.partial(_lat_cd_kernel, lam_init=lam_init),
        grid=(DEC_BATCH, nq),
        in_specs=[pl.BlockSpec((TQ, DIFF_W), lambda b, i: (b * nq + i, 0)),
                  pl.BlockSpec((t, DIFF_W), lambda b, i: (b, 1)),
                  pl.BlockSpec((t, DIFF_W), lambda b, i: (b, 2)),
                  pl.BlockSpec((t, FNET_W), lambda b, i: (b, 3 * DIFF_W // FNET_W)),
                  pl.BlockSpec((1, PAST_LEN, DIFF_W), lambda b, i: (b, 0, 0)),
                  pl.BlockSpec((1, PAST_LEN, DIFF_W), lambda b, i: (b, 0, 0)),
                  pl.BlockSpec((4, HEAD_DIM), const), pl.BlockSpec((1, PAIR_W), const),
                  pl.BlockSpec((TQ, t), lambda b, i: (i, 0)), pl.BlockSpec((TQ, t), lambda b, i: (i, 0)),
                  pl.BlockSpec((FNET_W, FNET_W), const), pl.BlockSpec((FNET_W, FNET_W), const)],
        out_specs=[pl.BlockSpec((TQ, DIFF_W), lambda b, i: (b * nq + i, 0)),
                   pl.BlockSpec((TQ, FNET_W), lambda b, i: (b * nq + i, 0))],
        out_shape=[jax.ShapeDtypeStruct((DEC_BATCH * t, DIFF_W), bf16),
                   jax.ShapeDtypeStruct((DEC_BATCH * t, FNET_W), bf16)],
        compiler_params=_params(2),
        name="lat_cd",
    )(proj, proj, proj, proj, ck, cv, lam, subln, ct, st, bdc, bds)


def _rope_tables():
    t = np.arange(DEC_SEQ)
    quarter = HEAD_DIM // 4
    inv = ROPE_BASE ** (-np.arange(quarter, dtype=np.float64) / quarter)
    ang = np.concatenate([(t // GRID_W)[:, None] * inv, (t % GRID_W)[:, None] * inv], -1)
    cos, sin = np.cos(ang), np.sin(ang)
    reps = LANES // HEAD_DIM
    return (np.tile(np.concatenate([cos, cos], -1), (1, reps)).astype(np.float32),
            np.tile(np.concatenate([-sin, sin], -1), (1, reps)).astype(np.float32))


def _dft_tables(n):
    k = np.arange(n)
    ang = (2.0 * math.pi / n) * ((k[:, None] * k[None, :]) % n)
    return (np.cos(ang) / math.sqrt(n)).astype(np.float32), (np.sin(ang) / math.sqrt(n)).astype(np.float32)


def _block_diag(m, reps):
    return np.kron(np.eye(reps, dtype=m.dtype), m)


def kernel(x_prompt, x_sample, state_ret, cache_win_k, cache_win_v, cache_diff_k, cache_diff_v, c, c_ctx, w_mod, b_mod, ln_g, ln_b, w_in_ab, w_out_ab, ret_log_gamma, ret_gn_g, ret_gn_b, win_sink, w_in_cd, w_out_cd, diff_lambda, diff_subln_g, w_gate, w_up, w_down):
    n_ctx = BATCH * SEQ
    n_lat = DEC_BATCH * DEC_SEQ

    cond = jnp.concatenate([c_ctx[None, :], c, jnp.zeros((8 - 1 - DEC_BATCH, D_MODEL), f32)], 0)
    mod = _modulation(cond, w_mod, b_mod).reshape(DEPTH, 8, 6, D_MODEL)

    rope_tabs = _rope_tables()
    gmat = jnp.asarray(_block_diag(np.full((HEAD_DIM, HEAD_DIM), 1.0 / HEAD_DIM, np.float32), H_RET), bf16)
    c64, s64 = _dft_tables(FNET_DIM)
    bdc = _block_diag(c64, FNET_GROUPS)
    bds = _block_diag(s64, FNET_GROUPS)
    dft_ctx = _dft_tables(SEQ)
    dft_lat = _dft_tables(DEC_SEQ)

    xs = [x_prompt.reshape(n_ctx, D_MODEL), x_sample.reshape(n_lat, D_MODEL)]
    rows_per_mod = [n_ctx, DEC_SEQ]
    outs = {}
    for l in range(DEPTH):
        i = l // 2
        mods = [mod[l, 0:1], mod[l, 1:1 + DEC_BATCH]]
        if l % 2 == 0:
            w_in = w_in_ab[i].astype(bf16)
            w_out = w_out_ab[i].astype(bf16)
            lgf = jnp.repeat(ret_log_gamma[i, 0], HEAD_DIM)[None, :]
            lgb = jnp.repeat(ret_log_gamma[i, 1], HEAD_DIM)[None, :]
            gn_g = ret_gn_g[i][None, :]
            gn_b = ret_gn_b[i][None, :]
            sink = win_sink[i][None, :]
            rope_tiles = tuple(range(0, 2 * RET_W // LANES)) + tuple(
                range(4 * RET_W // LANES, (4 * RET_W + WIN_W + KV_W) // LANES))
            kv_tile = (4 * RET_W + WIN_W) // LANES
            kv_shape = (BATCH, 1, KV_WIN, HEAD_DIM, SEQ)
            proj_ctx, wk_t, wv_t = _proj(xs[0], mods[0], w_in, rows_per_mod[0], cache_shapes=(kv_shape, kv_shape),
                                         cache_plan={kv_tile: ("heads", 0, 0), kv_tile + 1: ("heads", 1, 0)})
            (proj_lat,) = _proj(xs[1], mods[1], w_in, rows_per_mod[1], rope_tabs, rope_tiles)
            ro_c, wo_c, st_c = _ctx_ab(proj_ctx, ret_log_gamma[i], sink, lgf, lgb, gmat, gn_g, gn_b)
            s0 = state_ret[:, i]
            s0 = s0.reshape(DEC_BATCH, 2, H_RET // 2, 2, HEAD_DIM, HEAD_DIM)
            eye2 = jnp.eye(2, dtype=f32)
            s0_pairs = jnp.einsum('bdpeij,ef->bdpeifj', s0, eye2).reshape(
                DEC_BATCH, 2, H_RET // 2, PAIR_W, PAIR_W)
            ck = cache_win_k[:, i].reshape(DEC_BATCH, PAST_LEN, KV_W)
            cv = cache_win_v[:, i].reshape(DEC_BATCH, PAST_LEN, KV_W)
            ro_l, wo_l = _lat_ab(proj_lat, ret_log_gamma[i], sink, ck, cv, s0_pairs, lgf, lgb, gmat, gn_g, gn_b)
            mixed = [(ro_c, wo_c), (ro_l, wo_l)]
            outs.setdefault('state', []).append(st_c[:, None])
            outs.setdefault('win_k', []).append(jnp.transpose(wk_t, (0, 1, 4, 2, 3)))
            outs.setdefault('win_v', []).append(jnp.transpose(wv_t, (0, 1, 4, 2, 3)))
        else:
            w_in = w_in_cd[i].astype(bf16)
            w_out = w_out_cd[i].astype(bf16)
            lam_init = 0.8 - 0.6 * math.exp(-0.3 * l)
            subln = diff_subln_g[i][None, :]
            rope_tiles = tuple(range(0, 2 * DIFF_W // LANES))
            plan = {}
            for h in range(H_DIFF):
                plan[DIFF_W // LANES + h] = ("pairs", 0, h)
                plan[2 * DIFF_W // LANES + h] = ("plain", 1, h)
            proj_ctx, dk_t, dv_h = _proj(
                xs[0], mods[0], w_in, rows_per_mod[0],
                cache_shapes=((BATCH, 1, H_DIFF, 2, HEAD_DIM, SEQ), (BATCH, 1, H_DIFF, SEQ, 2 * HEAD_DIM)),
                cache_plan=plan)
            (proj_lat,) = _proj(xs[1], mods[1], w_in, rows_per_mod[1], rope_tabs, rope_tiles)
            a_c, z_c = _ctx_cd(proj_ctx, diff_lambda[i], subln, dft_ctx[0], dft_ctx[1], bdc, bds, lam_init)
            ck = cache_diff_k[:, i].reshape(DEC_BATCH, PAST_LEN, DIFF_W)
            cv = cache_diff_v[:, i].reshape(DEC_BATCH, PAST_LEN, DIFF_W)
            a_l, z_l = _lat_cd(proj_lat, ck, cv, diff_lambda[i], subln, dft_lat[0], dft_lat[1], bdc, bds, lam_init)
            mixed = [(a_c, z_c), (a_l, z_l)]
            outs.setdefault('diff_k', []).append(jnp.transpose(dk_t, (0, 1, 5, 2, 3, 4)))
            outs.setdefault('diff_v', []).append(jnp.transpose(dv_h, (0, 1, 3, 2, 4)))
        wg, wu, wd = w_gate[l].astype(bf16), w_up[l].astype(bf16), w_down[l].astype(bf16)
        xs = [_post(xs[p], mixed[p][0], mixed[p][1], mods[p], ln_g[l], ln_b[l], w_out, wg, wu, wd, rows_per_mod[p])
              for p in range(2)]

    y_prompt = xs[0].reshape(BATCH, SEQ, D_MODEL)
    y_sample = xs[1].reshape(DEC_BATCH, DEC_SEQ, D_MODEL)
    cat = lambda parts: parts[0] if len(parts) == 1 else jnp.concatenate(parts, 1)
    return (y_prompt, y_sample, cat(outs['state']), cat(outs['win_k']), cat(outs['win_v']),
            cat(outs['diff_k']), cat(outs['diff_v']))
```

```python
import functools
import math

import jax
import jax.numpy as jnp
import numpy as np
from jax import lax
from jax.experimental import pallas as pl
from jax.experimental.pallas import tpu as pltpu

D_MODEL = 1024
BATCH = 32
SEQ = 256
DEPTH = 2
DEC_BATCH = 2
DEC_SEQ = 1024
PAST_LEN = 512
GRID_W = 64
HEAD_DIM = 64
ROPE_BASE = 10000.0
H_RET = 8
H_WIN = 8
KV_WIN = 2
G_WIN = H_WIN // KV_WIN
WINDOW = 128
H_DIFF = 6
FNET_GROUPS = 4
FNET_DIM = 64
D_FF = 256 * math.ceil(8 * D_MODEL / 3 / 256)
RET_W = H_RET * HEAD_DIM
WIN_W = H_WIN * HEAD_DIM
KV_W = KV_WIN * HEAD_DIM
AB_IN = 4 * RET_W + WIN_W + 2 * KV_W
DIFF_W = H_DIFF * 2 * HEAD_DIM
FNET_W = FNET_GROUPS * FNET_DIM
CD_IN = 3 * DIFF_W + FNET_W
ALPHA = (2 * DEPTH) ** 0.25
LN_EPS = 1e-5
QK_SCALE = HEAD_DIM ** -0.5

LANES = 128
PAIR_W = 2 * HEAD_DIM
TM = 512
ROW_GROUPS = 2
TQ = 256
N_CHUNK = 256
NEG_BIG = -1e30
VMEM_LIMIT = 56 * 1024 * 1024

f32 = jnp.float32
bf16 = jnp.bfloat16


def _params(n_axes):
    return pltpu.CompilerParams(dimension_semantics=("arbitrary",) * n_axes,
                                vmem_limit_bytes=VMEM_LIMIT)


def _dot(a, b):
    return jnp.dot(a, b, preferred_element_type=f32)


def _dot_nt(a, b):
    return lax.dot_general(a, b, (((1,), (1,)), ((), ())), preferred_element_type=f32)


def _ln(x):
    mu = jnp.mean(x, -1, keepdims=True)
    d = x - mu
    var = jnp.mean(d * d, -1, keepdims=True)
    return d * lax.rsqrt(var + LN_EPS)


def _silu(x):
    return x * jax.nn.sigmoid(x)


def _split_bf16(x):
    hi = x.astype(bf16)
    lo = (x - hi.astype(f32)).astype(bf16)
    return hi, lo


def _lane_half_mask(shape):
    return (lax.broadcasted_iota(jnp.int32, shape, len(shape) - 1) & HEAD_DIM) == 0


def _mod_kernel(c_ref, w_ref, b_ref, o_ref):
    a = _silu(c_ref[...])
    a_hi, a_lo = _split_bf16(a)
    w_hi, w_lo = _split_bf16(w_ref[0])
    acc = _dot(a_hi, w_hi) + _dot(a_lo, w_hi) + _dot(a_hi, w_lo)
    o_ref[0] = acc + b_ref[0]


def _modulation(cond, w_mod, b_mod):
    tn = 1536
    rows = cond.shape[0]
    return pl.pallas_call(
        _mod_kernel,
        grid=(DEPTH, 6 * D_MODEL // tn),
        in_specs=[pl.BlockSpec((rows, D_MODEL), lambda l, j: (0, 0)),
                  pl.BlockSpec((1, D_MODEL, tn), lambda l, j: (l, 0, j)),
                  pl.BlockSpec((1, 1, tn), lambda l, j: (l, 0, j))],
        out_specs=pl.BlockSpec((1, rows, tn), lambda l, j: (l, 0, j)),
        out_shape=jax.ShapeDtypeStruct((DEPTH, rows, 6 * D_MODEL), f32),
        compiler_params=_params(2),
        name="modulation",
    )(cond, w_mod, b_mod.reshape(DEPTH, 1, 6 * D_MODEL))


def _rope_pair(y, cos, sin_signed):
    first_half = (lax.broadcasted_iota(jnp.int32, y.shape, 1) & (HEAD_DIM // 2)) == 0
    swapped = jnp.where(first_half, pltpu.roll(y, LANES - HEAD_DIM // 2, 1), pltpu.roll(y, HEAD_DIM // 2, 1))
    return y * cos + swapped * sin_signed


def _proj_kernel(*refs, n_out, rope_tiles, scale_tiles, cache_plan):
    n_in = 5 if rope_tiles else 3
    x_ref, mod_ref, w_ref = refs[:3]
    o_ref = refs[n_in]
    cache_refs = refs[n_in + 1:-1]
    u_ref = refs[-1]
    shift = mod_ref[0, 0:1, :]
    scale = mod_ref[0, 1:2, :]
    u_ref[...] = (_ln(x_ref[...]) * (1.0 + scale) + shift).astype(bf16)
    for j in range(0, n_out, N_CHUNK):
        y = _dot(u_ref[...], w_ref[:, j:j + N_CHUNK])
        for t in range(N_CHUNK // LANES):
            tile = j // LANES + t
            piece = y[:, t * LANES:(t + 1) * LANES]
            if tile in rope_tiles:
                piece = _rope_pair(piece, refs[3][...], refs[4][...])
            if tile in scale_tiles:
                piece = piece * QK_SCALE
            o_ref[:, tile * LANES:(tile + 1) * LANES] = piece.astype(o_ref.dtype)
            if tile in cache_plan:
                kind, out_idx, slot = cache_plan[tile]
                c_ref = cache_refs[out_idx]
                for b in range(TM // SEQ):
                    rows = piece[b * SEQ:(b + 1) * SEQ, :]
                    if kind == "plain":
                        c_ref[b, 0, slot] = rows
                    else:
                        rows_t = rows.T
                        if kind == "heads":
                            c_ref[b, 0, 0] = rows_t[0:HEAD_DIM]
                            c_ref[b, 0, 1] = rows_t[HEAD_DIM:]
                        else:
                            c_ref[b, 0, slot, 0] = rows_t[0:HEAD_DIM]
                            c_ref[b, 0, slot, 1] = rows_t[HEAD_DIM:]


def _proj(x, mod, w, rows_per_mod, scale_tiles, rope_tabs=None, rope_tiles=(), cache_shapes=(), cache_plan=None):
    n = x.shape[0]
    n_out = w.shape[1]
    in_specs = [pl.BlockSpec((TM, D_MODEL), lambda i: (i, 0)),
                pl.BlockSpec((1, 6, D_MODEL), lambda i: ((i * TM) // rows_per_mod, 0, 0)),
                pl.BlockSpec((D_MODEL, n_out), lambda i: (0, 0))]
    args = [x, mod, w]
    if rope_tiles:
        nb = DEC_SEQ // TM
        in_specs += [pl.BlockSpec((TM, LANES), lambda i: (i % nb, 0)),
                     pl.BlockSpec((TM, LANES), lambda i: (i % nb, 0))]
        args += list(rope_tabs)
    out_specs = [pl.BlockSpec((TM, n_out), lambda i: (i, 0))]
    out_shape = [jax.ShapeDtypeStruct((n, n_out), bf16)]
    for shp in cache_shapes:
        blk = (TM // SEQ,) + tuple(shp[1:])
        out_specs.append(pl.BlockSpec(blk, lambda i, nd=len(shp): (i,) + (0,) * (nd - 1)))
        out_shape.append(jax.ShapeDtypeStruct(tuple(shp), f32))
    return pl.pallas_call(
        functools.partial(_proj_kernel, n_out=n_out, rope_tiles=frozenset(rope_tiles),
                          scale_tiles=frozenset(scale_tiles),
                          cache_plan=dict(cache_plan or {})),
        grid=(n // TM,),
        in_specs=in_specs,
        out_specs=out_specs,
        out_shape=out_shape,
        scratch_shapes=[pltpu.VMEM((TM, D_MODEL), bf16)],
        compiler_params=_params(1),
        name="proj",
    )(*args)


def _post_kernel(x_ref, a_ref, b_ref, mod_ref, lng_ref, lnb_ref, wo_ref, wg_ref, wu_ref, wd_ref,
                 o_ref, x1_ref, u_ref, h_ref, *, ka):
    gate1 = mod_ref[0, 2:3, :]
    shift2 = mod_ref[0, 3:4, :]
    scale2 = mod_ref[0, 4:5, :]
    gate2 = mod_ref[0, 5:6, :]
    groups = [slice(r * TM // ROW_GROUPS, (r + 1) * TM // ROW_GROUPS) for r in range(ROW_GROUPS)]
    for rs in groups:
        h = _dot(a_ref[rs, :], wo_ref[0, 0:ka, :]) + _dot(b_ref[rs, :], wo_ref[0, ka:, :])
        x1 = _ln(ALPHA * x_ref[rs, :] + gate1 * h) * lng_ref[0, 0:1, :] + lnb_ref[0, 0:1, :]
        x1_ref[rs, :] = x1
        u_ref[rs, :] = (_ln(x1) * (1.0 + scale2) + shift2).astype(bf16)
    for j in range(0, D_FF, N_CHUNK):
        for rs in groups:
            g = _dot(u_ref[rs, :], wg_ref[0, :, j:j + N_CHUNK])
            up = _dot(u_ref[rs, :], wu_ref[0, :, j:j + N_CHUNK])
            h_ref[rs, j:j + N_CHUNK] = (_silu(g) * up).astype(bf16)
    for rs in groups:
        ffn = _dot(h_ref[rs, :], wd_ref[0])
        o_ref[rs, :] = _ln(ALPHA * x1_ref[rs, :] + gate2 * ffn) * lng_ref[0, 1:2, :] + lnb_ref[0, 1:2, :]


def _post(x, a, b, mod, ln_g, ln_b, w_out, w_gate, w_up, w_down, rows_per_mod, layer, mix_layer):
    n = x.shape[0]
    ka, kb = a.shape[1], b.shape[1]
    lay = lambda i: (layer, 0, 0)
    return pl.pallas_call(
        functools.partial(_post_kernel, ka=ka),
        grid=(n // TM,),
        in_specs=[pl.BlockSpec((TM, D_MODEL), lambda i: (i, 0)),
                  pl.BlockSpec((TM, ka), lambda i: (i, 0)),
                  pl.BlockSpec((TM, kb), lambda i: (i, 0)),
                  pl.BlockSpec((1, 6, D_MODEL), lambda i: ((i * TM) // rows_per_mod, 0, 0)),
                  pl.BlockSpec((1, 2, D_MODEL), lay),
                  pl.BlockSpec((1, 2, D_MODEL), lay),
                  pl.BlockSpec((1, ka + kb, D_MODEL), lambda i: (mix_layer, 0, 0), pipeline_mode=pl.Buffered(1)),
                  pl.BlockSpec((1, D_MODEL, D_FF), lay, pipeline_mode=pl.Buffered(1)),
                  pl.BlockSpec((1, D_MODEL, D_FF), lay, pipeline_mode=pl.Buffered(1)),
                  pl.BlockSpec((1, D_FF, D_MODEL), lay, pipeline_mode=pl.Buffered(1))],
        out_specs=pl.BlockSpec((TM, D_MODEL), lambda i: (i, 0)),
        out_shape=jax.ShapeDtypeStruct((n, D_MODEL), f32),
        scratch_shapes=[pltpu.VMEM((TM, D_MODEL), f32), pltpu.VMEM((TM, D_MODEL), bf16),
                        pltpu.VMEM((TM, D_FF), bf16)],
        compiler_params=_params(1),
        name="post",
    )(x, a, b, mod, ln_g, ln_b, w_out, w_gate, w_up, w_down)


def _group_norm_gate(ro, rg, gmat, gn_g, gn_b):
    def gmean(v):
        hi, lo = _split_bf16(v)
        return _dot(hi, gmat) + _dot(lo, gmat)

    d = ro - gmean(ro)
    var = gmean(d * d)
    y = d * lax.rsqrt(var + LN_EPS) * gn_g + gn_b
    return _silu(rg.astype(f32)) * y


def _dup_head(x, j):
    first = _lane_half_mask(x.shape)
    keep = first if j == 0 else jnp.logical_not(first)
    xm = jnp.where(keep, x.astype(f32), 0.0)
    return xm + pltpu.roll(xm, HEAD_DIM, 1)


def _softmax_parts(scores, sink):
    m = sink
    for s in scores:
        m = jnp.maximum(m, jnp.max(s, -1, keepdims=True))
    es = [jnp.exp(s - m) for s in scores]
    denom = jnp.exp(sink - m)
    for e in es:
        denom = denom + jnp.sum(e, -1, keepdims=True)
    return es, denom


def _ctx_ab_kernel(lg_ref, sink_ref, rq_ref, rk_ref, rv_ref, rg_ref, wq_ref, wk_ref, wv_ref,
                   lgf_ref, lgb_ref, gmat_ref, gng_ref, gnb_ref,
                   ro_ref, wo_ref, st_ref, dmask_ref, kdec_ref, ret_ref):
    t_len = SEQ

    @pl.when(pl.program_id(0) == 0)
    def _():
        row = lax.broadcasted_iota(jnp.int32, (t_len, t_len), 0)
        col = lax.broadcasted_iota(jnp.int32, (t_len, t_len), 1)
        diff = (row - col).astype(f32)
        diag = jnp.where(row == col, 2.0 * QK_SCALE, QK_SCALE)
        for h in range(H_RET):
            dmask_ref[h] = jnp.exp(jnp.where(diff >= 0, lg_ref[0, h] * diff, -lg_ref[1, h] * diff)) * diag
        t = lax.broadcasted_iota(jnp.int32, (t_len, RET_W), 0).astype(f32)
        kdec_ref[0] = jnp.exp(lgf_ref[...] * (t_len - 1.0 - t)) * QK_SCALE
        kdec_ref[1] = jnp.exp(lgb_ref[...] * t) * QK_SCALE

    first = _lane_half_mask((t_len, PAIR_W))
    for p in range(H_RET // 2):
        sl = slice(p * PAIR_W, (p + 1) * PAIR_W)
        q = rq_ref[:, sl]
        kb = rk_ref[:, sl]
        v = rv_ref[:, sl]
        outs = []
        for e in range(2):
            keep = first if e == 0 else jnp.logical_not(first)
            s = _dot_nt(q, jnp.where(keep, kb, jnp.zeros_like(kb))) * dmask_ref[2 * p + e]
            outs.append(_dot(s.astype(bf16), v))
        ret_ref[:, sl] = jnp.where(first, outs[0], outs[1])
        for d in range(2):
            kd_t = (kb * kdec_ref[d, :, sl]).T.astype(bf16)
            st = _dot(kd_t, v)
            st_ref[0, d, 2 * p] = st[0:HEAD_DIM, 0:HEAD_DIM]
            st_ref[0, d, 2 * p + 1] = pltpu.roll(st[HEAD_DIM:, :], HEAD_DIM, 1)[:, 0:HEAD_DIM]
    ro_ref[...] = _group_norm_gate(ret_ref[...], rg_ref[...], gmat_ref[...], gng_ref[...], gnb_ref[...]).astype(bf16)

    for j in range(KV_WIN):
        k_dup = _dup_head(wk_ref[...], j).astype(bf16)
        v_dup = _dup_head(wv_ref[...], j).astype(bf16)
        for pp in range(G_WIN // 2):
            col = (j * G_WIN + 2 * pp) * HEAD_DIM
            qb = wq_ref[:, col:col + PAIR_W]
            outs = []
            for e in range(2):
                keep = first if e == 0 else jnp.logical_not(first)
                s = _dot_nt(jnp.where(keep, qb, jnp.zeros_like(qb)), k_dup)
                (es,), denom = _softmax_parts([s], sink_ref[0, j * G_WIN + 2 * pp + e])
                outs.append(_dot(es.astype(bf16), v_dup) / denom)
            wo_ref[:, col:col + PAIR_W] = jnp.where(first, outs[0], outs[1]).astype(bf16)


def _ctx_ab(proj, log_gamma, sink, lgf_lanes, lgb_lanes, gmat, gn_g, gn_b):
    t = SEQ
    smem = pl.BlockSpec(memory_space=pltpu.SMEM)
    const = lambda b: (0, 0)
    col = lambda c: (lambda b: (b, c))
    return pl.pallas_call(
        _ctx_ab_kernel,
        grid=(BATCH,),
        in_specs=[smem, smem,
                  pl.BlockSpec((t, RET_W), col(0)), pl.BlockSpec((t, RET_W), col(1)),
                  pl.BlockSpec((t, RET_W), col(2)), pl.BlockSpec((t, RET_W), col(3)),
                  pl.BlockSpec((t, WIN_W), col(4)),
                  pl.BlockSpec((t, KV_W), col((4 * RET_W + WIN_W) // KV_W)),
                  pl.BlockSpec((t, KV_W), col((4 * RET_W + WIN_W) // KV_W + 1)),
                  pl.BlockSpec((1, RET_W), const), pl.BlockSpec((1, RET_W), const),
                  pl.BlockSpec((RET_W, RET_W), const),
                  pl.BlockSpec((1, RET_W), const), pl.BlockSpec((1, RET_W), const)],
        out_specs=[pl.BlockSpec((t, RET_W), lambda b: (b, 0)),
                   pl.BlockSpec((t, WIN_W), lambda b: (b, 0)),
                   pl.BlockSpec((1, 2, H_RET, HEAD_DIM, HEAD_DIM), lambda b: (b, 0, 0, 0, 0))],
        out_shape=[jax.ShapeDtypeStruct((BATCH * t, RET_W), bf16),
                   jax.ShapeDtypeStruct((BATCH * t, WIN_W), bf16),
                   jax.ShapeDtypeStruct((BATCH, 2, H_RET, HEAD_DIM, HEAD_DIM), f32)],
        scratch_shapes=[pltpu.VMEM((H_RET, t, t), f32), pltpu.VMEM((2, t, RET_W), f32),
                        pltpu.VMEM((t, RET_W), f32)],
        compiler_params=_params(1),
        name="ctx_ab",
    )(log_gamma, sink, proj, proj, proj, proj, proj, proj, proj, lgf_lanes, lgb_lanes, gmat, gn_g, gn_b)


def _lat_ab_kernel(lg_ref, sink_ref, rq_ref, rk_ref, rv_ref, rg_ref, wq_ref, wk_ref, wv_ref, ck_ref, cv_ref,
                   s0_ref, lgf_ref, lgb_ref, gmat_ref, gng_ref, gnb_ref,
                   ro_ref, wo_ref, ret_ref):
    t_len = DEC_SEQ
    q0 = pl.program_id(1) * TQ
    first = _lane_half_mask((TQ, PAIR_W))
    first_k = _lane_half_mask((t_len, PAIR_W))
    row = q0 + lax.broadcasted_iota(jnp.int32, (TQ, t_len), 0)
    col = lax.broadcasted_iota(jnp.int32, (TQ, t_len), 1)
    diff = (row - col).astype(f32)
    diag = jnp.where(row == col, 2.0 * QK_SCALE, QK_SCALE)
    t_q = (q0 + lax.broadcasted_iota(jnp.int32, (TQ, PAIR_W), 0)).astype(f32)
    for p in range(H_RET // 2):
        sl = slice(p * PAIR_W, (p + 1) * PAIR_W)
        q = rq_ref[:, sl]
        kb = rk_ref[:, sl]
        v = rv_ref[:, sl]
        outs = []
        for e in range(2):
            h = 2 * p + e
            keep = first_k if e == 0 else jnp.logical_not(first_k)
            dmask = jnp.exp(jnp.where(diff >= 0, lg_ref[0, h] * diff, -lg_ref[1, h] * diff)) * diag
            s = _dot_nt(q, jnp.where(keep, kb, jnp.zeros_like(kb))) * dmask
            outs.append(_dot(s.astype(bf16), v))
        o = jnp.where(first, outs[0], outs[1])
        o = o + _dot(q, s0_ref[0, 0, p].astype(bf16)) * jnp.exp(lgf_ref[:, sl] * (t_q + 1.0))
        o = o + _dot(q, s0_ref[0, 1, p].astype(bf16)) * jnp.exp(lgb_ref[:, sl] * (t_len - t_q))
        ret_ref[:, sl] = o
    ro_ref[...] = _group_norm_gate(ret_ref[...], rg_ref[...], gmat_ref[...], gng_ref[...], gnb_ref[...]).astype(bf16)

    band = TQ + 2 * WINDOW
    k_start = pl.multiple_of(jnp.clip(q0 - WINDOW, 0, t_len - band), LANES)
    qi = q0 + lax.broadcasted_iota(jnp.int32, (TQ, band), 0)
    kj = k_start + lax.broadcasted_iota(jnp.int32, (TQ, band), 1)
    in_band = jnp.abs(qi - kj) <= WINDOW
    for j in range(KV_WIN):
        k_dup = _dup_head(wk_ref[pl.ds(k_start, band), :], j).astype(bf16)
        v_dup = _dup_head(wv_ref[pl.ds(k_start, band), :], j).astype(bf16)
        ck_dup = _dup_head(ck_ref[0], j).astype(bf16)
        cv_dup = _dup_head(cv_ref[0], j).astype(bf16)
        for pp in range(G_WIN // 2):
            c0 = (j * G_WIN + 2 * pp) * HEAD_DIM
            qb = wq_ref[:, c0:c0 + PAIR_W]
            outs = []
            for e in range(2):
                keep = first if e == 0 else jnp.logical_not(first)
                qm = jnp.where(keep, qb, jnp.zeros_like(qb))
                s_band = jnp.where(in_band, _dot_nt(qm, k_dup), NEG_BIG)
                s_ctx = _dot_nt(qm, ck_dup)
                (e_band, e_ctx), denom = _softmax_parts([s_band, s_ctx], sink_ref[0, j * G_WIN + 2 * pp + e])
                outs.append((_dot(e_band.astype(bf16), v_dup) + _dot(e_ctx.astype(bf16), cv_dup)) / denom)
            wo_ref[:, c0:c0 + PAIR_W] = jnp.where(first, outs[0], outs[1]).astype(bf16)


def _lat_ab(proj, log_gamma, sink, ck, cv, s0_pairs, lgf_lanes, lgb_lanes, gmat, gn_g, gn_b):
    t = DEC_SEQ
    nq = t // TQ
    smem = pl.BlockSpec(memory_space=pltpu.SMEM)
    const = lambda b, i: (0, 0)
    qcol = lambda c: (lambda b, i: (b * nq + i, c))
    bcol = lambda c: (lambda b, i: (b, c))
    kv_col = (4 * RET_W + WIN_W) // KV_W
    return pl.pallas_call(
        _lat_ab_kernel,
        grid=(DEC_BATCH, nq),
        in_specs=[smem, smem,
                  pl.BlockSpec((TQ, RET_W), qcol(0)), pl.BlockSpec((t, RET_W), bcol(1)),
                  pl.BlockSpec((t, RET_W), bcol(2)), pl.BlockSpec((TQ, RET_W), qcol(3)),
                  pl.BlockSpec((TQ, WIN_W), qcol(4)),
                  pl.BlockSpec((t, KV_W), bcol(kv_col)), pl.BlockSpec((t, KV_W), bcol(kv_col + 1)),
                  pl.BlockSpec((1, PAST_LEN, KV_W), lambda b, i: (b, 0, 0)),
                  pl.BlockSpec((1, PAST_LEN, KV_W), lambda b, i: (b, 0, 0)),
                  pl.BlockSpec((1, 2, H_RET // 2, PAIR_W, PAIR_W), lambda b, i: (b, 0, 0, 0, 0)),
                  pl.BlockSpec((1, RET_W), const), pl.BlockSpec((1, RET_W), const),
                  pl.BlockSpec((RET_W, RET_W), const),
                  pl.BlockSpec((1, RET_W), const), pl.BlockSpec((1, RET_W), const)],
        out_specs=[pl.BlockSpec((TQ, RET_W), lambda b, i: (b * nq + i, 0)),
                   pl.BlockSpec((TQ, WIN_W), lambda b, i: (b * nq + i, 0))],
        out_shape=[jax.ShapeDtypeStruct((DEC_BATCH * t, RET_W), bf16),
                   jax.ShapeDtypeStruct((DEC_BATCH * t, WIN_W), bf16)],
        scratch_shapes=[pltpu.VMEM((TQ, RET_W), f32)],
        compiler_params=_params(2),
        name="lat_ab",
    )(log_gamma, sink, proj, proj, proj, proj, proj, proj, proj, ck, cv, s0_pairs,
      lgf_lanes, lgb_lanes, gmat, gn_g, gn_b)


def _lambda_full(lam_ref, lam_init):
    lam = lam_ref[...]
    a = jnp.sum(lam[0:1, :] * lam[1:2, :], -1, keepdims=True)
    b = jnp.sum(lam[2:3, :] * lam[3:4, :], -1, keepdims=True)
    return jnp.exp(a) - jnp.exp(b) + lam_init


def _diff_head(q, k_parts, v_parts, lam, subln, lam_init):
    first = _lane_half_mask(k_parts[0].shape)
    outs = []
    for e in range(2):
        scores = []
        for k in k_parts:
            fm = _lane_half_mask(k.shape)
            keep = fm if e == 0 else jnp.logical_not(fm)
            scores.append(_dot_nt(q, jnp.where(keep, k, jnp.zeros_like(k))))
        m = scores[0].max(-1, keepdims=True)
        for s in scores[1:]:
            m = jnp.maximum(m, s.max(-1, keepdims=True))
        es = [jnp.exp(s - m) for s in scores]
        denom = es[0].sum(-1, keepdims=True)
        for ex in es[1:]:
            denom = denom + ex.sum(-1, keepdims=True)
        pv = _dot(es[0].astype(bf16), v_parts[0])
        for ex, v in zip(es[1:], v_parts[1:]):
            pv = pv + _dot(ex.astype(bf16), v)
        outs.append(pv / denom)
    del first
    a = outs[0] - lam * outs[1]
    return a * lax.rsqrt(jnp.mean(a * a, -1, keepdims=True) + LN_EPS) * subln * (1.0 - lam_init)


def _fourier_rows(ct_ref, st_ref, z, bdc_ref, bds_ref):
    zb = z
    zc = _dot(zb, bdc_ref[...].astype(bf16)).astype(bf16)
    zs = _dot(zb, bds_ref[...].astype(bf16)).astype(bf16)
    return _dot(ct_ref[...].astype(bf16), zc) - _dot(st_ref[...].astype(bf16), zs)


def _ctx_cd_kernel(q_ref, k_ref, v_ref, z_ref, lam_ref, subln_ref, ct_ref, st_ref, bdc_ref, bds_ref,
                   a_ref, zf_ref, *, lam_init):
    lam = _lambda_full(lam_ref, lam_init)
    for h in range(H_DIFF):
        sl = slice(h * PAIR_W, (h + 1) * PAIR_W)
        a_ref[:, sl] = _diff_head(q_ref[:, sl], [k_ref[:, sl]],
                                  [v_ref[:, sl]], lam, subln_ref[...], lam_init).astype(bf16)
    zf_ref[...] = _fourier_rows(ct_ref, st_ref, z_ref[...], bdc_ref, bds_ref).astype(bf16)


def _ctx_cd(proj, lam, subln, ct, st, bdc, bds, lam_init):
    t = SEQ
    const = lambda b: (0, 0)
    col = lambda c: (lambda b: (b, c))
    return pl.pallas_call(
        functools.partial(_ctx_cd_kernel, lam_init=lam_init),
        grid=(BATCH,),
        in_specs=[pl.BlockSpec((t, DIFF_W), col(0)), pl.BlockSpec((t, DIFF_W), col(1)),
                  pl.BlockSpec((t, DIFF_W), col(2)), pl.BlockSpec((t, FNET_W), col(3 * DIFF_W // FNET_W)),
                  pl.BlockSpec((4, HEAD_DIM), const), pl.BlockSpec((1, PAIR_W), const),
                  pl.BlockSpec((t, t), const), pl.BlockSpec((t, t), const),
                  pl.BlockSpec((FNET_W, FNET_W), const), pl.BlockSpec((FNET_W, FNET_W), const)],
        out_specs=[pl.BlockSpec((t, DIFF_W), lambda b: (b, 0)), pl.BlockSpec((t, FNET_W), lambda b: (b, 0))],
        out_shape=[jax.ShapeDtypeStruct((BATCH * t, DIFF_W), bf16),
                   jax.ShapeDtypeStruct((BATCH * t, FNET_W), bf16)],
        compiler_params=_params(1),
        name="ctx_cd",
    )(proj, proj, proj, proj, lam, subln, ct, st, bdc, bds)


def _lat_cd_kernel(q_ref, k_ref, v_ref, z_ref, ck_ref, cv_ref, lam_ref, subln_ref, ct_ref, st_ref, bdc_ref, bds_ref,
                   a_ref, zf_ref, *, lam_init):
    lam = _lambda_full(lam_ref, lam_init)
    for h in range(H_DIFF):
        sl = slice(h * PAIR_W, (h + 1) * PAIR_W)
        a_ref[:, sl] = _diff_head(q_ref[:, sl],
                                  [k_ref[:, sl], ck_ref[0, :, sl].astype(bf16)],
                                  [v_ref[:, sl], cv_ref[0, :, sl].astype(bf16)],
                                  lam, subln_ref[...], lam_init).astype(bf16)
    zf_ref[...] = _fourier_rows(ct_ref, st_ref, z_ref[...], bdc_ref, bds_ref).astype(bf16)


def _lat_cd(proj, ck, cv, lam, subln, ct, st, bdc, bds, lam_init):
    t = DEC_SEQ
    nq = t // TQ
    const = lambda b, i: (0, 0)
    return pl.pallas_call(
        functools.partial(_lat_cd_kernel, lam_init=lam_init),
        grid=(DEC_BATCH, nq),
        in_specs=[pl.BlockSpec((TQ, DIFF_W), lambda b, i: (b * nq + i, 0)),
                  pl.BlockSpec((t, DIFF_W), lambda b, i: (b, 1)),
                  pl.BlockSpec((t, DIFF_W), lambda b, i: (b, 2)),
                  pl.BlockSpec((t, FNET_W), lambda b, i: (b, 3 * DIFF_W // FNET_W)),
                  pl.BlockSpec((1, PAST_LEN, DIFF_W), lambda b, i: (b, 0, 0)),
                  pl.BlockSpec((1, PAST_LEN, DIFF_W), lambda b, i: (b, 0, 0)),
                  pl.BlockSpec((4, HEAD_DIM), const), pl.BlockSpec((1, PAIR_W), const),
                  pl.BlockSpec((TQ, t), lambda b, i: (i, 0)), pl.BlockSpec((TQ, t), lambda b, i: (i, 0)),
                  pl.BlockSpec((FNET_W, FNET_W), const), pl.BlockSpec((FNET_W, FNET_W), const)],
        out_specs=[pl.BlockSpec((TQ, DIFF_W), lambda b, i: (b * nq + i, 0)),
                   pl.BlockSpec((TQ, FNET_W), lambda b, i: (b * nq + i, 0))],
        out_shape=[jax.ShapeDtypeStruct((DEC_BATCH * t, DIFF_W), bf16),
                   jax.ShapeDtypeStruct((DEC_BATCH * t, FNET_W), bf16)],
        compiler_params=_params(2),
        name="lat_cd",
    )(proj, proj, proj, proj, ck, cv, lam, subln, ct, st, bdc, bds)


def _rope_tables():
    t = np.arange(DEC_SEQ)
    quarter = HEAD_DIM // 4
    inv = ROPE_BASE ** (-np.arange(quarter, dtype=np.float64) / quarter)
    ang = np.concatenate([(t // GRID_W)[:, None] * inv, (t % GRID_W)[:, None] * inv], -1)
    cos, sin = np.cos(ang), np.sin(ang)
    reps = LANES // HEAD_DIM
    return (np.tile(np.concatenate([cos, cos], -1), (1, reps)).astype(np.float32),
            np.tile(np.concatenate([-sin, sin], -1), (1, reps)).astype(np.float32))


def _dft_tables(n):
    k = np.arange(n)
    ang = (2.0 * math.pi / n) * ((k[:, None] * k[None, :]) % n)
    return (np.cos(ang) / math.sqrt(n)).astype(np.float32), (np.sin(ang) / math.sqrt(n)).astype(np.float32)


def _block_diag(m, reps):
    return np.kron(np.eye(reps, dtype=m.dtype), m)


def kernel(x_prompt, x_sample, state_ret, cache_win_k, cache_win_v, cache_diff_k, cache_diff_v, c, c_ctx, w_mod, b_mod, ln_g, ln_b, w_in_ab, w_out_ab, ret_log_gamma, ret_gn_g, ret_gn_b, win_sink, w_in_cd, w_out_cd, diff_lambda, diff_subln_g, w_gate, w_up, w_down):
    n_ctx = BATCH * SEQ
    n_lat = DEC_BATCH * DEC_SEQ

    cond = jnp.concatenate([c_ctx[None, :], c, jnp.zeros((8 - 1 - DEC_BATCH, D_MODEL), f32)], 0)
    mod = _modulation(cond, w_mod, b_mod).reshape(DEPTH, 8, 6, D_MODEL)

    rope_tabs = _rope_tables()
    gmat = jnp.asarray(_block_diag(np.full((HEAD_DIM, HEAD_DIM), 1.0 / HEAD_DIM, np.float32), H_RET), bf16)
    c64, s64 = _dft_tables(FNET_DIM)
    bdc = _block_diag(c64, FNET_GROUPS)
    bds = _block_diag(s64, FNET_GROUPS)
    dft_ctx = _dft_tables(SEQ)
    dft_lat = _dft_tables(DEC_SEQ)

    xs = [x_prompt.reshape(n_ctx, D_MODEL), x_sample.reshape(n_lat, D_MODEL)]
    rows_per_mod = [n_ctx, DEC_SEQ]
    wg_all, wu_all, wd_all = w_gate.astype(bf16), w_up.astype(bf16), w_down.astype(bf16)
    w_out_ab_all, w_out_cd_all = w_out_ab.astype(bf16), w_out_cd.astype(bf16)
    outs = {}
    for l in range(DEPTH):
        i = l // 2
        mods = [mod[l, 0:1], mod[l, 1:1 + DEC_BATCH]]
        if l % 2 == 0:
            w_in = w_in_ab[i].astype(bf16)
            w_out = w_out_ab_all
            lgf = jnp.repeat(ret_log_gamma[i, 0], HEAD_DIM)[None, :]
            lgb = jnp.repeat(ret_log_gamma[i, 1], HEAD_DIM)[None, :]
            gn_g = ret_gn_g[i][None, :]
            gn_b = ret_gn_b[i][None, :]
            sink = win_sink[i][None, :]
            rope_tiles = tuple(range(0, 2 * RET_W // LANES)) + tuple(
                range(4 * RET_W // LANES, (4 * RET_W + WIN_W + KV_W) // LANES))
            kv_tile = (4 * RET_W + WIN_W) // LANES
            kv_shape = (BATCH, 1, KV_WIN, HEAD_DIM, SEQ)
            scale_tiles = tuple(range(4 * RET_W // LANES, (4 * RET_W + WIN_W) // LANES))
            proj_ctx, wk_t, wv_t = _proj(xs[0], mods[0], w_in, rows_per_mod[0], scale_tiles,
                                         cache_shapes=(kv_shape, kv_shape),
                                         cache_plan={kv_tile: ("heads", 0, 0), kv_tile + 1: ("heads", 1, 0)})
            (proj_lat,) = _proj(xs[1], mods[1], w_in, rows_per_mod[1], scale_tiles, rope_tabs, rope_tiles)
            ro_c, wo_c, st_c = _ctx_ab(proj_ctx, ret_log_gamma[i], sink, lgf, lgb, gmat, gn_g, gn_b)
            s0 = state_ret[:, i]
            s0 = s0.reshape(DEC_BATCH, 2, H_RET // 2, 2, HEAD_DIM, HEAD_DIM)
            eye2 = jnp.eye(2, dtype=f32)
            s0_pairs = jnp.einsum('bdpeij,ef->bdpeifj', s0, eye2).reshape(
                DEC_BATCH, 2, H_RET // 2, PAIR_W, PAIR_W)
            ck = cache_win_k[:, i].reshape(DEC_BATCH, PAST_LEN, KV_W)
            cv = cache_win_v[:, i].reshape(DEC_BATCH, PAST_LEN, KV_W)
            ro_l, wo_l = _lat_ab(proj_lat, ret_log_gamma[i], sink, ck, cv, s0_pairs, lgf, lgb, gmat, gn_g, gn_b)
            mixed = [(ro_c, wo_c), (ro_l, wo_l)]
            outs.setdefault('state', []).append(st_c[:, None])
            outs.setdefault('win_k', []).append(jnp.transpose(wk_t, (0, 1, 4, 2, 3)))
            outs.setdefault('win_v', []).append(jnp.transpose(wv_t, (0, 1, 4, 2, 3)))
        else:
            w_in = w_in_cd[i].astype(bf16)
            w_out = w_out_cd_all
            lam_init = 0.8 - 0.6 * math.exp(-0.3 * l)
            subln = diff_subln_g[i][None, :]
            rope_tiles = tuple(range(0, 2 * DIFF_W // LANES))
            plan = {}
            for h in range(H_DIFF):
                plan[DIFF_W // LANES + h] = ("pairs", 0, h)
                plan[2 * DIFF_W // LANES + h] = ("plain", 1, h)
            scale_tiles = tuple(range(0, DIFF_W // LANES))
            proj_ctx, dk_t, dv_h = _proj(
                xs[0], mods[0], w_in, rows_per_mod[0], scale_tiles,
                cache_shapes=((BATCH, 1, H_DIFF, 2, HEAD_DIM, SEQ), (BATCH, 1, H_DIFF, SEQ, 2 * HEAD_DIM)),
                cache_plan=plan)
            (proj_lat,) = _proj(xs[1], mods[1], w_in, rows_per_mod[1], scale_tiles, rope_tabs, rope_tiles)
            a_c, z_c = _ctx_cd(proj_ctx, diff_lambda[i], subln, dft_ctx[0], dft_ctx[1], bdc, bds, lam_init)
            ck = cache_diff_k[:, i].reshape(DEC_BATCH, PAST_LEN, DIFF_W)
            cv = cache_diff_v[:, i].reshape(DEC_BATCH, PAST_LEN, DIFF_W)
            a_l, z_l = _lat_cd(proj_lat, ck, cv, diff_lambda[i], subln, dft_lat[0], dft_lat[1], bdc, bds, lam_init)
            mixed = [(a_c, z_c), (a_l, z_l)]
            outs.setdefault('diff_k', []).append(jnp.transpose(dk_t, (0, 1, 5, 2, 3, 4)))
            outs.setdefault('diff_v', []).append(jnp.transpose(dv_h, (0, 1, 3, 2, 4)))
        xs = [_post(xs[p], mixed[p][0], mixed[p][1], mods[p], ln_g, ln_b, w_out, wg_all, wu_all, wd_all,
                    rows_per_mod[p], l, i) for p in range(2)]

    y_prompt = xs[0].reshape(BATCH, SEQ, D_MODEL)
    y_sample = xs[1].reshape(DEC_BATCH, DEC_SEQ, D_MODEL)
    cat = lambda parts: parts[0] if len(parts) == 1 else jnp.concatenate(parts, 1)
    return (y_prompt, y_sample, cat(outs['state']), cat(outs['win_k']), cat(outs['win_v']),
            cat(outs['diff_k']), cat(outs['diff_v']))
```

```python
import functools
import math

import jax
import jax.numpy as jnp
import numpy as np
from jax import lax
from jax.experimental import pallas as pl
from jax.experimental.pallas import tpu as pltpu

D_MODEL = 1024
BATCH = 32
SEQ = 256
DEPTH = 2
DEC_BATCH = 2
DEC_SEQ = 1024
PAST_LEN = 512
GRID_W = 64
HEAD_DIM = 64
ROPE_BASE = 10000.0
H_RET = 8
H_WIN = 8
KV_WIN = 2
G_WIN = H_WIN // KV_WIN
WINDOW = 128
H_DIFF = 6
FNET_GROUPS = 4
FNET_DIM = 64
D_FF = 256 * math.ceil(8 * D_MODEL / 3 / 256)
RET_W = H_RET * HEAD_DIM
WIN_W = H_WIN * HEAD_DIM
KV_W = KV_WIN * HEAD_DIM
AB_IN = 4 * RET_W + WIN_W + 2 * KV_W
DIFF_W = H_DIFF * 2 * HEAD_DIM
FNET_W = FNET_GROUPS * FNET_DIM
CD_IN = 3 * DIFF_W + FNET_W
ALPHA = (2 * DEPTH) ** 0.25
LN_EPS = 1e-5
QK_SCALE = HEAD_DIM ** -0.5
LOG2_E = math.log2(math.e)

LANES = 128
PAIR_W = 2 * HEAD_DIM
TM = 512
ROW_GROUPS = 2
TQ = 256
N_CHUNK = 256
NEG_BIG = -1e30
VMEM_LIMIT = 56 * 1024 * 1024

f32 = jnp.float32
bf16 = jnp.bfloat16


def _params(n_axes):
    return pltpu.CompilerParams(dimension_semantics=("arbitrary",) * n_axes,
                                vmem_limit_bytes=VMEM_LIMIT)


def _dot(a, b):
    return jnp.dot(a, b, preferred_element_type=f32)


def _dot_nt(a, b):
    return lax.dot_general(a, b, (((1,), (1,)), ((), ())), preferred_element_type=f32)


def _ln(x):
    mu = jnp.mean(x, -1, keepdims=True)
    d = x - mu
    var = jnp.mean(d * d, -1, keepdims=True)
    return d * lax.rsqrt(var + LN_EPS)


def _silu(x):
    return x * jax.nn.sigmoid(x)


def _split_bf16(x):
    hi = x.astype(bf16)
    lo = (x - hi.astype(f32)).astype(bf16)
    return hi, lo


def _lane_half_mask(shape):
    return (lax.broadcasted_iota(jnp.int32, shape, len(shape) - 1) & HEAD_DIM) == 0


def _mod_kernel(c_ref, w_ref, b_ref, o_ref):
    a = _silu(c_ref[...])
    a_hi, a_lo = _split_bf16(a)
    w_hi, w_lo = _split_bf16(w_ref[0])
    acc = _dot(a_hi, w_hi) + _dot(a_lo, w_hi) + _dot(a_hi, w_lo)
    o_ref[0] = acc + b_ref[0]


def _modulation(cond, w_mod, b_mod):
    tn = 1536
    rows = cond.shape[0]
    return pl.pallas_call(
        _mod_kernel,
        grid=(DEPTH, 6 * D_MODEL // tn),
        in_specs=[pl.BlockSpec((rows, D_MODEL), lambda l, j: (0, 0)),
                  pl.BlockSpec((1, D_MODEL, tn), lambda l, j: (l, 0, j)),
                  pl.BlockSpec((1, 1, tn), lambda l, j: (l, 0, j))],
        out_specs=pl.BlockSpec((1, rows, tn), lambda l, j: (l, 0, j)),
        out_shape=jax.ShapeDtypeStruct((DEPTH, rows, 6 * D_MODEL), f32),
        compiler_params=_params(2),
        name="modulation",
    )(cond, w_mod, b_mod.reshape(DEPTH, 1, 6 * D_MODEL))


def _rope_pair(y, cos, sin_signed):
    first_half = (lax.broadcasted_iota(jnp.int32, y.shape, 1) & (HEAD_DIM // 2)) == 0
    swapped = jnp.where(first_half, pltpu.roll(y, LANES - HEAD_DIM // 2, 1), pltpu.roll(y, HEAD_DIM // 2, 1))
    return y * cos + swapped * sin_signed


def _proj_kernel(*refs, n_out, rope_tiles, scale_tiles, cache_plan):
    n_in = 5 if rope_tiles else 3
    x_ref, mod_ref, w_ref = refs[:3]
    o_ref = refs[n_in]
    cache_refs = refs[n_in + 1:-1]
    u_ref = refs[-1]
    shift = mod_ref[0, 0:1, :]
    scale = mod_ref[0, 1:2, :]
    for b in range(TM // SEQ):
        rs = slice(b * SEQ, (b + 1) * SEQ)
        u_ref[rs, :] = (_ln(x_ref[rs, :]) * (1.0 + scale) + shift).astype(bf16)
    for j in range(0, n_out, N_CHUNK):
        for b in range(TM // SEQ):
            rs = slice(b * SEQ, (b + 1) * SEQ)
            y = _dot(u_ref[rs, :], w_ref[:, j:j + N_CHUNK])
            for t in range(N_CHUNK // LANES):
                tile = j // LANES + t
                piece = y[:, t * LANES:(t + 1) * LANES]
                if tile in rope_tiles:
                    piece = _rope_pair(piece, refs[3][rs, :], refs[4][rs, :])
                if tile in scale_tiles:
                    piece = piece * (QK_SCALE * LOG2_E)
                o_ref[rs, tile * LANES:(tile + 1) * LANES] = piece.astype(o_ref.dtype)
                if tile in cache_plan:
                    kind, out_idx, slot = cache_plan[tile]
                    c_ref = cache_refs[out_idx]
                    if kind == "plain":
                        c_ref[b, 0, slot] = piece
                    else:
                        piece_t = piece.T
                        if kind == "heads":
                            c_ref[b, 0, 0] = piece_t[0:HEAD_DIM]
                            c_ref[b, 0, 1] = piece_t[HEAD_DIM:]
                        else:
                            c_ref[b, 0, slot, 0] = piece_t[0:HEAD_DIM]
                            c_ref[b, 0, slot, 1] = piece_t[HEAD_DIM:]


def _proj(x, mod, w, rows_per_mod, scale_tiles, rope_tabs=None, rope_tiles=(), cache_shapes=(), cache_plan=None):
    n = x.shape[0]
    n_out = w.shape[1]
    in_specs = [pl.BlockSpec((TM, D_MODEL), lambda i: (i, 0)),
                pl.BlockSpec((1, 6, D_MODEL), lambda i: ((i * TM) // rows_per_mod, 0, 0)),
                pl.BlockSpec((D_MODEL, n_out), lambda i: (0, 0))]
    args = [x, mod, w]
    if rope_tiles:
        nb = DEC_SEQ // TM
        in_specs += [pl.BlockSpec((TM, LANES), lambda i: (i % nb, 0)),
                     pl.BlockSpec((TM, LANES), lambda i: (i % nb, 0))]
        args += list(rope_tabs)
    out_specs = [pl.BlockSpec((TM, n_out), lambda i: (i, 0))]
    out_shape = [jax.ShapeDtypeStruct((n, n_out), bf16)]
    for shp in cache_shapes:
        blk = (TM // SEQ,) + tuple(shp[1:])
        out_specs.append(pl.BlockSpec(blk, lambda i, nd=len(shp): (i,) + (0,) * (nd - 1)))
        out_shape.append(jax.ShapeDtypeStruct(tuple(shp), f32))
    return pl.pallas_call(
        functools.partial(_proj_kernel, n_out=n_out, rope_tiles=frozenset(rope_tiles),
                          scale_tiles=frozenset(scale_tiles),
                          cache_plan=dict(cache_plan or {})),
        grid=(n // TM,),
        in_specs=in_specs,
        out_specs=out_specs,
        out_shape=out_shape,
        scratch_shapes=[pltpu.VMEM((TM, D_MODEL), bf16)],
        compiler_params=_params(1),
        name="proj",
    )(*args)


def _post_kernel(x_ref, a_ref, b_ref, mod_ref, lng_ref, lnb_ref, wo_ref, wg_ref, wu_ref, wd_ref,
                 o_ref, x1_ref, u_ref, h_ref, *, ka):
    gate1 = mod_ref[0, 2:3, :]
    shift2 = mod_ref[0, 3:4, :]
    scale2 = mod_ref[0, 4:5, :]
    gate2 = mod_ref[0, 5:6, :]
    groups = [slice(r * TM // ROW_GROUPS, (r + 1) * TM // ROW_GROUPS) for r in range(ROW_GROUPS)]
    for rs in groups:
        h = _dot(a_ref[rs, :], wo_ref[0, 0:ka, :]) + _dot(b_ref[rs, :], wo_ref[0, ka:, :])
        x1 = _ln(ALPHA * x_ref[rs, :] + gate1 * h) * lng_ref[0, 0:1, :] + lnb_ref[0, 0:1, :]
        x1_ref[rs, :] = x1
        u_ref[rs, :] = (_ln(x1) * (1.0 + scale2) + shift2).astype(bf16)
    for j in range(0, D_FF, N_CHUNK):
        for rs in groups:
            g = _dot(u_ref[rs, :], wg_ref[0, :, j:j + N_CHUNK])
            up = _dot(u_ref[rs, :], wu_ref[0, :, j:j + N_CHUNK])
            h_ref[rs, j:j + N_CHUNK] = (_silu(g) * up).astype(bf16)
    for rs in groups:
        ffn = _dot(h_ref[rs, :], wd_ref[0])
        o_ref[rs, :] = _ln(ALPHA * x1_ref[rs, :] + gate2 * ffn) * lng_ref[0, 1:2, :] + lnb_ref[0, 1:2, :]


def _post(x, a, b, mod, ln_g, ln_b, w_out, w_gate, w_up, w_down, rows_per_mod, layer, mix_layer):
    n = x.shape[0]
    ka, kb = a.shape[1], b.shape[1]
    lay = lambda i: (layer, 0, 0)
    return pl.pallas_call(
        functools.partial(_post_kernel, ka=ka),
        grid=(n // TM,),
        in_specs=[pl.BlockSpec((TM, D_MODEL), lambda i: (i, 0)),
                  pl.BlockSpec((TM, ka), lambda i: (i, 0)),
                  pl.BlockSpec((TM, kb), lambda i: (i, 0)),
                  pl.BlockSpec((1, 6, D_MODEL), lambda i: ((i * TM) // rows_per_mod, 0, 0)),
                  pl.BlockSpec((1, 2, D_MODEL), lay),
                  pl.BlockSpec((1, 2, D_MODEL), lay),
                  pl.BlockSpec((1, ka + kb, D_MODEL), lambda i: (mix_layer, 0, 0), pipeline_mode=pl.Buffered(1)),
                  pl.BlockSpec((1, D_MODEL, D_FF), lay, pipeline_mode=pl.Buffered(1)),
                  pl.BlockSpec((1, D_MODEL, D_FF), lay, pipeline_mode=pl.Buffered(1)),
                  pl.BlockSpec((1, D_FF, D_MODEL), lay, pipeline_mode=pl.Buffered(1))],
        out_specs=pl.BlockSpec((TM, D_MODEL), lambda i: (i, 0)),
        out_shape=jax.ShapeDtypeStruct((n, D_MODEL), f32),
        scratch_shapes=[pltpu.VMEM((TM, D_MODEL), f32), pltpu.VMEM((TM, D_MODEL), bf16),
                        pltpu.VMEM((TM, D_FF), bf16)],
        compiler_params=_params(1),
        name="post",
    )(x, a, b, mod, ln_g, ln_b, w_out, w_gate, w_up, w_down)


def _group_norm_gate(ro, rg, gmat, gn_g, gn_b):
    def gmean(parts):
        cols = []
        for c in range(0, RET_W, N_CHUNK):
            cols.append(sum(_dot(p[:, c:c + N_CHUNK], gmat) for p in parts))
        return jnp.concatenate(cols, -1)

    d = ro - gmean(_split_bf16(ro))
    var = gmean([(d * d).astype(bf16)])
    y = d * lax.rsqrt(var + LN_EPS) * gn_g + gn_b
    return _silu(rg.astype(f32)) * y


def _dup_head(x, j):
    first = _lane_half_mask(x.shape)
    keep = first if j == 0 else jnp.logical_not(first)
    xm = jnp.where(keep, x.astype(f32), 0.0)
    return xm + pltpu.roll(xm, HEAD_DIM, 1)


def _softmax_parts(scores, sink):
    m = sink
    for s in scores:
        m = jnp.maximum(m, jnp.max(s, -1, keepdims=True))
    es = [jnp.exp2(s - m) for s in scores]
    denom = jnp.exp2(sink - m)
    for e in es:
        denom = denom + jnp.sum(e, -1, keepdims=True)
    return es, denom


def _ctx_ab_kernel(lg_ref, sink_ref, rq_ref, rk_ref, rv_ref, rg_ref, wq_ref, wk_ref, wv_ref,
                   lgf_ref, lgb_ref, gmat_ref, gng_ref, gnb_ref,
                   ro_ref, wo_ref, st_ref, dmask_ref, kdec_ref, ret_ref):
    t_len = SEQ

    @pl.when(pl.program_id(0) == 0)
    def _():
        row = lax.broadcasted_iota(jnp.int32, (t_len, t_len), 0)
        col = lax.broadcasted_iota(jnp.int32, (t_len, t_len), 1)
        diff = (row - col).astype(f32)
        diag = jnp.where(row == col, 2.0 * QK_SCALE, QK_SCALE)
        for h in range(H_RET):
            dmask_ref[h] = jnp.exp(jnp.where(diff >= 0, lg_ref[0, h] * diff, -lg_ref[1, h] * diff)) * diag
        t = lax.broadcasted_iota(jnp.int32, (t_len, RET_W), 0).astype(f32)
        kdec_ref[0] = jnp.exp(lgf_ref[...] * (t_len - 1.0 - t)) * QK_SCALE
        kdec_ref[1] = jnp.exp(lgb_ref[...] * t) * QK_SCALE

    first = _lane_half_mask((t_len, PAIR_W))
    for p in range(H_RET // 2):
        sl = slice(p * PAIR_W, (p + 1) * PAIR_W)
        q = rq_ref[:, sl]
        kb = rk_ref[:, sl]
        v = rv_ref[:, sl]
        outs = []
        for e in range(2):
            keep = first if e == 0 else jnp.logical_not(first)
            s = _dot_nt(q, jnp.where(keep, kb, jnp.zeros_like(kb))) * dmask_ref[2 * p + e]
            outs.append(_dot(s.astype(bf16), v))
        ret_ref[:, sl] = jnp.where(first, outs[0], outs[1])
        for d in range(2):
            kd_t = (kb * kdec_ref[d, :, sl]).T.astype(bf16)
            st = _dot(kd_t, v)
            st_ref[0, d, 2 * p] = st[0:HEAD_DIM, 0:HEAD_DIM]
            st_ref[0, d, 2 * p + 1] = pltpu.roll(st[HEAD_DIM:, :], HEAD_DIM, 1)[:, 0:HEAD_DIM]
    ro_ref[...] = _group_norm_gate(ret_ref[...], rg_ref[...], gmat_ref[...], gng_ref[...], gnb_ref[...]).astype(bf16)

    for j in range(KV_WIN):
        k_dup = _dup_head(wk_ref[...], j).astype(bf16)
        v_dup = _dup_head(wv_ref[...], j).astype(bf16)
        for pp in range(G_WIN // 2):
            col = (j * G_WIN + 2 * pp) * HEAD_DIM
            qb = wq_ref[:, col:col + PAIR_W]
            outs = []
            for e in range(2):
                keep = first if e == 0 else jnp.logical_not(first)
                s = _dot_nt(jnp.where(keep, qb, jnp.zeros_like(qb)), k_dup)
                (es,), denom = _softmax_parts([s], sink_ref[0, j * G_WIN + 2 * pp + e] * LOG2_E)
                outs.append(_dot(es.astype(bf16), v_dup) / denom)
            wo_ref[:, col:col + PAIR_W] = jnp.where(first, outs[0], outs[1]).astype(bf16)


def _ctx_ab(proj, log_gamma, sink, lgf_lanes, lgb_lanes, gmat, gn_g, gn_b):
    t = SEQ
    smem = pl.BlockSpec(memory_space=pltpu.SMEM)
    const = lambda b: (0, 0)
    col = lambda c: (lambda b: (b, c))
    return pl.pallas_call(
        _ctx_ab_kernel,
        grid=(BATCH,),
        in_specs=[smem, smem,
                  pl.BlockSpec((t, RET_W), col(0)), pl.BlockSpec((t, RET_W), col(1)),
                  pl.BlockSpec((t, RET_W), col(2)), pl.BlockSpec((t, RET_W), col(3)),
                  pl.BlockSpec((t, WIN_W), col(4)),
                  pl.BlockSpec((t, KV_W), col((4 * RET_W + WIN_W) // KV_W)),
                  pl.BlockSpec((t, KV_W), col((4 * RET_W + WIN_W) // KV_W + 1)),
                  pl.BlockSpec((1, RET_W), const), pl.BlockSpec((1, RET_W), const),
                  pl.BlockSpec((N_CHUNK, N_CHUNK), const),
                  pl.BlockSpec((1, RET_W), const), pl.BlockSpec((1, RET_W), const)],
        out_specs=[pl.BlockSpec((t, RET_W), lambda b: (b, 0)),
                   pl.BlockSpec((t, WIN_W), lambda b: (b, 0)),
                   pl.BlockSpec((1, 2, H_RET, HEAD_DIM, HEAD_DIM), lambda b: (b, 0, 0, 0, 0))],
        out_shape=[jax.ShapeDtypeStruct((BATCH * t, RET_W), bf16),
                   jax.ShapeDtypeStruct((BATCH * t, WIN_W), bf16),
                   jax.ShapeDtypeStruct((BATCH, 2, H_RET, HEAD_DIM, HEAD_DIM), f32)],
        scratch_shapes=[pltpu.VMEM((H_RET, t, t), f32), pltpu.VMEM((2, t, RET_W), f32),
                        pltpu.VMEM((t, RET_W), f32)],
        compiler_params=_params(1),
        name="ctx_ab",
    )(log_gamma, sink, proj, proj, proj, proj, proj, proj, proj, lgf_lanes, lgb_lanes, gmat, gn_g, gn_b)


def _lat_ab_kernel(lg_ref, sink_ref, rq_ref, rk_ref, rv_ref, rg_ref, wq_ref, wk_ref, wv_ref, ck_ref, cv_ref,
                   s0_ref, lgf_ref, lgb_ref, gmat_ref, gng_ref, gnb_ref,
                   ro_ref, wo_ref, ret_ref):
    t_len = DEC_SEQ
    q0 = pl.program_id(1) * TQ
    first = _lane_half_mask((TQ, PAIR_W))
    first_k = _lane_half_mask((t_len, PAIR_W))
    row = q0 + lax.broadcasted_iota(jnp.int32, (TQ, t_len), 0)
    col = lax.broadcasted_iota(jnp.int32, (TQ, t_len), 1)
    diff = (row - col).astype(f32)
    diag = jnp.where(row == col, 2.0 * QK_SCALE, QK_SCALE)
    t_q = (q0 + lax.broadcasted_iota(jnp.int32, (TQ, PAIR_W), 0)).astype(f32)
    for p in range(H_RET // 2):
        sl = slice(p * PAIR_W, (p + 1) * PAIR_W)
        q = rq_ref[:, sl]
        kb = rk_ref[:, sl]
        v = rv_ref[:, sl]
        outs = []
        for e in range(2):
            h = 2 * p + e
            keep = first_k if e == 0 else jnp.logical_not(first_k)
            dmask = jnp.exp(jnp.where(diff >= 0, lg_ref[0, h] * diff, -lg_ref[1, h] * diff)) * diag
            s = _dot_nt(q, jnp.where(keep, kb, jnp.zeros_like(kb))) * dmask
            outs.append(_dot(s.astype(bf16), v))
        o = jnp.where(first, outs[0], outs[1])
        o = o + _dot(q, s0_ref[0, 0, p].astype(bf16)) * jnp.exp(lgf_ref[:, sl] * (t_q + 1.0))
        o = o + _dot(q, s0_ref[0, 1, p].astype(bf16)) * jnp.exp(lgb_ref[:, sl] * (t_len - t_q))
        ret_ref[:, sl] = o
    ro_ref[...] = _group_norm_gate(ret_ref[...], rg_ref[...], gmat_ref[...], gng_ref[...], gnb_ref[...]).astype(bf16)

    band = TQ + 2 * WINDOW
    k_start = pl.multiple_of(jnp.clip(q0 - WINDOW, 0, t_len - band), LANES)
    qi = q0 + lax.broadcasted_iota(jnp.int32, (TQ, band), 0)
    kj = k_start + lax.broadcasted_iota(jnp.int32, (TQ, band), 1)
    in_band = jnp.abs(qi - kj) <= WINDOW
    for j in range(KV_WIN):
        k_dup = _dup_head(wk_ref[pl.ds(k_start, band), :], j).astype(bf16)
        v_dup = _dup_head(wv_ref[pl.ds(k_start, band), :], j).astype(bf16)
        ck_dup = _dup_head(ck_ref[0], j).astype(bf16)
        cv_dup = _dup_head(cv_ref[0], j).astype(bf16)
        for pp in range(G_WIN // 2):
            c0 = (j * G_WIN + 2 * pp) * HEAD_DIM
            qb = wq_ref[:, c0:c0 + PAIR_W]
            outs = []
            for e in range(2):
                keep = first if e == 0 else jnp.logical_not(first)
                qm = jnp.where(keep, qb, jnp.zeros_like(qb))
                s_band = jnp.where(in_band, _dot_nt(qm, k_dup), NEG_BIG)
                s_ctx = _dot_nt(qm, ck_dup)
                (e_band, e_ctx), denom = _softmax_parts([s_band, s_ctx],
                                                        sink_ref[0, j * G_WIN + 2 * pp + e] * LOG2_E)
                outs.append((_dot(e_band.astype(bf16), v_dup) + _dot(e_ctx.astype(bf16), cv_dup)) / denom)
            wo_ref[:, c0:c0 + PAIR_W] = jnp.where(first, outs[0], outs[1]).astype(bf16)


def _lat_ab(proj, log_gamma, sink, ck, cv, s0_pairs, lgf_lanes, lgb_lanes, gmat, gn_g, gn_b):
    t = DEC_SEQ
    nq = t // TQ
    smem = pl.BlockSpec(memory_space=pltpu.SMEM)
    const = lambda b, i: (0, 0)
    qcol = lambda c: (lambda b, i: (b * nq + i, c))
    bcol = lambda c: (lambda b, i: (b, c))
    kv_col = (4 * RET_W + WIN_W) // KV_W
    return pl.pallas_call(
        _lat_ab_kernel,
        grid=(DEC_BATCH, nq),
        in_specs=[smem, smem,
                  pl.BlockSpec((TQ, RET_W), qcol(0)), pl.BlockSpec((t, RET_W), bcol(1)),
                  pl.BlockSpec((t, RET_W), bcol(2)), pl.BlockSpec((TQ, RET_W), qcol(3)),
                  pl.BlockSpec((TQ, WIN_W), qcol(4)),
                  pl.BlockSpec((t, KV_W), bcol(kv_col)), pl.BlockSpec((t, KV_W), bcol(kv_col + 1)),
                  pl.BlockSpec((1, PAST_LEN, KV_W), lambda b, i: (b, 0, 0)),
                  pl.BlockSpec((1, PAST_LEN, KV_W), lambda b, i: (b, 0, 0)),
                  pl.BlockSpec((1, 2, H_RET // 2, PAIR_W, PAIR_W), lambda b, i: (b, 0, 0, 0, 0)),
                  pl.BlockSpec((1, RET_W), const), pl.BlockSpec((1, RET_W), const),
                  pl.BlockSpec((N_CHUNK, N_CHUNK), const),
                  pl.BlockSpec((1, RET_W), const), pl.BlockSpec((1, RET_W), const)],
        out_specs=[pl.BlockSpec((TQ, RET_W), lambda b, i: (b * nq + i, 0)),
                   pl.BlockSpec((TQ, WIN_W), lambda b, i: (b * nq + i, 0))],
        out_shape=[jax.ShapeDtypeStruct((DEC_BATCH * t, RET_W), bf16),
                   jax.ShapeDtypeStruct((DEC_BATCH * t, WIN_W), bf16)],
        scratch_shapes=[pltpu.VMEM((TQ, RET_W), f32)],
        compiler_params=_params(2),
        name="lat_ab",
    )(log_gamma, sink, proj, proj, proj, proj, proj, proj, proj, ck, cv, s0_pairs,
      lgf_lanes, lgb_lanes, gmat, gn_g, gn_b)


def _lambda_full(lam_ref, lam_init):
    lam = lam_ref[...]
    a = jnp.sum(lam[0:1, :] * lam[1:2, :], -1, keepdims=True)
    b = jnp.sum(lam[2:3, :] * lam[3:4, :], -1, keepdims=True)
    return jnp.exp(a) - jnp.exp(b) + lam_init


def _diff_head(q, k_parts, v_parts, lam, subln, lam_init):
    first = _lane_half_mask(k_parts[0].shape)
    outs = []
    for e in range(2):
        scores = []
        for k in k_parts:
            fm = _lane_half_mask(k.shape)
            keep = fm if e == 0 else jnp.logical_not(fm)
            scores.append(_dot_nt(q, jnp.where(keep, k, jnp.zeros_like(k))))
        m = scores[0].max(-1, keepdims=True)
        for s in scores[1:]:
            m = jnp.maximum(m, s.max(-1, keepdims=True))
        es = [jnp.exp2(s - m) for s in scores]
        denom = es[0].sum(-1, keepdims=True)
        for ex in es[1:]:
            denom = denom + ex.sum(-1, keepdims=True)
        pv = _dot(es[0].astype(bf16), v_parts[0])
        for ex, v in zip(es[1:], v_parts[1:]):
            pv = pv + _dot(ex.astype(bf16), v)
        outs.append(pv / denom)
    del first
    a = outs[0] - lam * outs[1]
    return a * lax.rsqrt(jnp.mean(a * a, -1, keepdims=True) + LN_EPS) * subln * (1.0 - lam_init)


def _fourier_rows(ct_ref, st_ref, z, bdc_ref, bds_ref):
    zb = z
    zc = _dot(zb, bdc_ref[...].astype(bf16)).astype(bf16)
    zs = _dot(zb, bds_ref[...].astype(bf16)).astype(bf16)
    return _dot(ct_ref[...].astype(bf16), zc) - _dot(st_ref[...].astype(bf16), zs)


def _ctx_cd_kernel(q_ref, k_ref, v_ref, z_ref, lam_ref, subln_ref, ct_ref, st_ref, bdc_ref, bds_ref,
                   a_ref, zf_ref, *, lam_init):
    lam = _lambda_full(lam_ref, lam_init)
    for h in range(H_DIFF):
        sl = slice(h * PAIR_W, (h + 1) * PAIR_W)
        a_ref[:, sl] = _diff_head(q_ref[:, sl], [k_ref[:, sl]],
                                  [v_ref[:, sl]], lam, subln_ref[...], lam_init).astype(bf16)
    zf_ref[...] = _fourier_rows(ct_ref, st_ref, z_ref[...], bdc_ref, bds_ref).astype(bf16)


def _ctx_cd(proj, lam, subln, ct, st, bdc, bds, lam_init):
    t = SEQ
    const = lambda b: (0, 0)
    col = lambda c: (lambda b: (b, c))
    return pl.pallas_call(
        functools.partial(_ctx_cd_kernel, lam_init=lam_init),
        grid=(BATCH,),
        in_specs=[pl.BlockSpec((t, DIFF_W), col(0)), pl.BlockSpec((t, DIFF_W), col(1)),
                  pl.BlockSpec((t, DIFF_W), col(2)), pl.BlockSpec((t, FNET_W), col(3 * DIFF_W // FNET_W)),
                  pl.BlockSpec((4, HEAD_DIM), const), pl.BlockSpec((1, PAIR_W), const),
                  pl.BlockSpec((t, t), const), pl.BlockSpec((t, t), const),
                  pl.BlockSpec((FNET_W, FNET_W), const), pl.BlockSpec((FNET_W, FNET_W), const)],
        out_specs=[pl.BlockSpec((t, DIFF_W), lambda b: (b, 0)), pl.BlockSpec((t, FNET_W), lambda b: (b, 0))],
        out_shape=[jax.ShapeDtypeStruct((BATCH * t, DIFF_W), bf16),
                   jax.ShapeDtypeStruct((BATCH * t, FNET_W), bf16)],
        compiler_params=_params(1),
        name="ctx_cd",
    )(proj, proj, proj, proj, lam, subln, ct, st, bdc, bds)


def _lat_cd_kernel(q_ref, k_ref, v_ref, z_ref, ck_ref, cv_ref, lam_ref, subln_ref, ct_ref, st_ref, bdc_ref, bds_ref,
                   a_ref, zf_ref, *, lam_init):
    lam = _lambda_full(lam_ref, lam_init)
    for h in range(H_DIFF):
        sl = slice(h * PAIR_W, (h + 1) * PAIR_W)
        a_ref[:, sl] = _diff_head(q_ref[:, sl],
                                  [k_ref[:, sl], ck_ref[0, :, sl].astype(bf16)],
                                  [v_ref[:, sl], cv_ref[0, :, sl].astype(bf16)],
                                  lam, subln_ref[...], lam_init).astype(bf16)
    zf_ref[...] = _fourier_rows(ct_ref, st_ref, z_ref[...], bdc_ref, bds_ref).astype(bf16)


def _lat_cd(proj, ck, cv, lam, subln, ct, st, bdc, bds, lam_init):
    t = DEC_SEQ
    nq = t // TQ
    const = lambda b, i: (0, 0)
    return pl.pallas_call(
        functools.partial(_lat_cd_kernel, lam_init=lam_init),
        grid=(DEC_BATCH, nq),
        in_specs=[pl.BlockSpec((TQ, DIFF_W), lambda b, i: (b * nq + i, 0)),
                  pl.BlockSpec((t, DIFF_W), lambda b, i: (b, 1)),
                  pl.BlockSpec((t, DIFF_W), lambda b, i: (b, 2)),
                  pl.BlockSpec((t, FNET_W), lambda b, i: (b, 3 * DIFF_W // FNET_W)),
                  pl.BlockSpec((1, PAST_LEN, DIFF_W), lambda b, i: (b, 0, 0)),
                  pl.BlockSpec((1, PAST_LEN, DIFF_W), lambda b, i: (b, 0, 0)),
                  pl.BlockSpec((4, HEAD_DIM), const), pl.BlockSpec((1, PAIR_W), const),
                  pl.BlockSpec((TQ, t), lambda b, i: (i, 0)), pl.BlockSpec((TQ, t), lambda b, i: (i, 0)),
                  pl.BlockSpec((FNET_W, FNET_W), const), pl.BlockSpec((FNET_W, FNET_W), const)],
        out_specs=[pl.BlockSpec((TQ, DIFF_W), lambda b, i: (b * nq + i, 0)),
                   pl.BlockSpec((TQ, FNET_W), lambda b, i: (b * nq + i, 0))],
        out_shape=[jax.ShapeDtypeStruct((DEC_BATCH * t, DIFF_W), bf16),
                   jax.ShapeDtypeStruct((DEC_BATCH * t, FNET_W), bf16)],
        compiler_params=_params(2),
        name="lat_cd",
    )(proj, proj, proj, proj, ck, cv, lam, subln, ct, st, bdc, bds)


def _rope_tables():
    t = np.arange(DEC_SEQ)
    quarter = HEAD_DIM // 4
    inv = ROPE_BASE ** (-np.arange(quarter, dtype=np.float64) / quarter)
    ang = np.concatenate([(t // GRID_W)[:, None] * inv, (t % GRID_W)[:, None] * inv], -1)
    cos, sin = np.cos(ang), np.sin(ang)
    reps = LANES // HEAD_DIM
    return (np.tile(np.concatenate([cos, cos], -1), (1, reps)).astype(np.float32),
            np.tile(np.concatenate([-sin, sin], -1), (1, reps)).astype(np.float32))


def _dft_tables(n):
    k = np.arange(n)
    ang = (2.0 * math.pi / n) * ((k[:, None] * k[None, :]) % n)
    return (np.cos(ang) / math.sqrt(n)).astype(np.float32), (np.sin(ang) / math.sqrt(n)).astype(np.float32)


def _block_diag(m, reps):
    return np.kron(np.eye(reps, dtype=m.dtype), m)


def kernel(x_prompt, x_sample, state_ret, cache_win_k, cache_win_v, cache_diff_k, cache_diff_v, c, c_ctx, w_mod, b_mod, ln_g, ln_b, w_in_ab, w_out_ab, ret_log_gamma, ret_gn_g, ret_gn_b, win_sink, w_in_cd, w_out_cd, diff_lambda, diff_subln_g, w_gate, w_up, w_down):
    n_ctx = BATCH * SEQ
    n_lat = DEC_BATCH * DEC_SEQ

    cond = jnp.concatenate([c_ctx[None, :], c, jnp.zeros((8 - 1 - DEC_BATCH, D_MODEL), f32)], 0)
    mod = _modulation(cond, w_mod, b_mod).reshape(DEPTH, 8, 6, D_MODEL)

    rope_tabs = _rope_tables()
    gmat = jnp.asarray(_block_diag(np.full((HEAD_DIM, HEAD_DIM), 1.0 / HEAD_DIM, np.float32),
                                   N_CHUNK // HEAD_DIM), bf16)
    c64, s64 = _dft_tables(FNET_DIM)
    bdc = _block_diag(c64, FNET_GROUPS)
    bds = _block_diag(s64, FNET_GROUPS)
    dft_ctx = _dft_tables(SEQ)
    dft_lat = _dft_tables(DEC_SEQ)

    xs = [x_prompt.reshape(n_ctx, D_MODEL), x_sample.reshape(n_lat, D_MODEL)]
    rows_per_mod = [n_ctx, DEC_SEQ]
    wg_all, wu_all, wd_all = w_gate.astype(bf16), w_up.astype(bf16), w_down.astype(bf16)
    w_out_ab_all, w_out_cd_all = w_out_ab.astype(bf16), w_out_cd.astype(bf16)
    outs = {}
    for l in range(DEPTH):
        i = l // 2
        mods = [mod[l, 0:1], mod[l, 1:1 + DEC_BATCH]]
        if l % 2 == 0:
            w_in = w_in_ab[i].astype(bf16)
            w_out = w_out_ab_all
            lgf = jnp.repeat(ret_log_gamma[i, 0], HEAD_DIM)[None, :]
            lgb = jnp.repeat(ret_log_gamma[i, 1], HEAD_DIM)[None, :]
            gn_g = ret_gn_g[i][None, :]
            gn_b = ret_gn_b[i][None, :]
            sink = win_sink[i][None, :]
            rope_tiles = tuple(range(0, 2 * RET_W // LANES)) + tuple(
                range(4 * RET_W // LANES, (4 * RET_W + WIN_W + KV_W) // LANES))
            kv_tile = (4 * RET_W + WIN_W) // LANES
            kv_shape = (BATCH, 1, KV_WIN, HEAD_DIM, SEQ)
            scale_tiles = tuple(range(4 * RET_W // LANES, (4 * RET_W + WIN_W) // LANES))
            proj_ctx, wk_t, wv_t = _proj(xs[0], mods[0], w_in, rows_per_mod[0], scale_tiles,
                                         cache_shapes=(kv_shape, kv_shape),
                                         cache_plan={kv_tile: ("heads", 0, 0), kv_tile + 1: ("heads", 1, 0)})
            (proj_lat,) = _proj(xs[1], mods[1], w_in, rows_per_mod[1], scale_tiles, rope_tabs, rope_tiles)
            ro_c, wo_c, st_c = _ctx_ab(proj_ctx, ret_log_gamma[i], sink, lgf, lgb, gmat, gn_g, gn_b)
            s0 = state_ret[:, i]
            s0 = s0.reshape(DEC_BATCH, 2, H_RET // 2, 2, HEAD_DIM, HEAD_DIM)
            eye2 = jnp.eye(2, dtype=f32)
            s0_pairs = jnp.einsum('bdpeij,ef->bdpeifj', s0, eye2).reshape(
                DEC_BATCH, 2, H_RET // 2, PAIR_W, PAIR_W)
            ck = cache_win_k[:, i].reshape(DEC_BATCH, PAST_LEN, KV_W)
            cv = cache_win_v[:, i].reshape(DEC_BATCH, PAST_LEN, KV_W)
            ro_l, wo_l = _lat_ab(proj_lat, ret_log_gamma[i], sink, ck, cv, s0_pairs, lgf, lgb, gmat, gn_g, gn_b)
            mixed = [(ro_c, wo_c), (ro_l, wo_l)]
            outs.setdefault('state', []).append(st_c[:, None])
            outs.setdefault('win_k', []).append(jnp.transpose(wk_t, (0, 1, 4, 2, 3)))
            outs.setdefault('win_v', []).append(jnp.transpose(wv_t, (0, 1, 4, 2, 3)))
        else:
            w_in = w_in_cd[i].astype(bf16)
            w_out = w_out_cd_all
            lam_init = 0.8 - 0.6 * math.exp(-0.3 * l)
            subln = diff_subln_g[i][None, :]
            rope_tiles = tuple(range(0, 2 * DIFF_W // LANES))
            plan = {}
            for h in range(H_DIFF):
                plan[DIFF_W // LANES + h] = ("pairs", 0, h)
                plan[2 * DIFF_W // LANES + h] = ("plain", 1, h)
            scale_tiles = tuple(range(0, DIFF_W // LANES))
            proj_ctx, dk_t, dv_h = _proj(
                xs[0], mods[0], w_in, rows_per_mod[0], scale_tiles,
                cache_shapes=((BATCH, 1, H_DIFF, 2, HEAD_DIM, SEQ), (BATCH, 1, H_DIFF, SEQ, 2 * HEAD_DIM)),
                cache_plan=plan)
            (proj_lat,) = _proj(xs[1], mods[1], w_in, rows_per_mod[1], scale_tiles, rope_tabs, rope_tiles)
            a_c, z_c = _ctx_cd(proj_ctx, diff_lambda[i], subln, dft_ctx[0], dft_ctx[1], bdc, bds, lam_init)
            ck = cache_diff_k[:, i].reshape(DEC_BATCH, PAST_LEN, DIFF_W)
            cv = cache_diff_v[:, i].reshape(DEC_BATCH, PAST_LEN, DIFF_W)
            a_l, z_l = _lat_cd(proj_lat, ck, cv, diff_lambda[i], subln, dft_lat[0], dft_lat[1], bdc, bds, lam_init)
            mixed = [(a_c, z_c), (a_l, z_l)]
            outs.setdefault('diff_k', []).append(jnp.transpose(dk_t, (0, 1, 5, 2, 3, 4)))
            outs.setdefault('diff_v', []).append(jnp.transpose(dv_h, (0, 1, 3, 2, 4)))
        xs = [_post(xs[p], mixed[p][0], mixed[p][1], mods[p], ln_g, ln_b, w_out, wg_all, wu_all, wd_all,
                    rows_per_mod[p], l, i) for p in range(2)]

    y_prompt = xs[0].reshape(BATCH, SEQ, D_MODEL)
    y_sample = xs[1].reshape(DEC_BATCH, DEC_SEQ, D_MODEL)
    cat = lambda parts: parts[0] if len(parts) == 1 else jnp.concatenate(parts, 1)
    return (y_prompt, y_sample, cat(outs['state']), cat(outs['win_k']), cat(outs['win_v']),
            cat(outs['diff_k']), cat(outs['diff_v']))
```

```python
import functools
import math

import jax
import jax.numpy as jnp
import numpy as np
from jax import lax
from jax.experimental import pallas as pl
from jax.experimental.pallas import tpu as pltpu

D_MODEL = 1024
BATCH = 32
SEQ = 256
DEPTH = 2
DEC_BATCH = 2
DEC_SEQ = 1024
PAST_LEN = 512
GRID_W = 64
HEAD_DIM = 64
ROPE_BASE = 10000.0
H_RET = 8
H_WIN = 8
KV_WIN = 2
G_WIN = H_WIN // KV_WIN
WINDOW = 128
H_DIFF = 6
FNET_GROUPS = 4
FNET_DIM = 64
D_FF = 256 * math.ceil(8 * D_MODEL / 3 / 256)
RET_W = H_RET * HEAD_DIM
WIN_W = H_WIN * HEAD_DIM
KV_W = KV_WIN * HEAD_DIM
AB_IN = 4 * RET_W + WIN_W + 2 * KV_W
DIFF_W = H_DIFF * 2 * HEAD_DIM
FNET_W = FNET_GROUPS * FNET_DIM
CD_IN = 3 * DIFF_W + FNET_W
ALPHA = (2 * DEPTH) ** 0.25
LN_EPS = 1e-5
QK_SCALE = HEAD_DIM ** -0.5
LOG2_E = math.log2(math.e)

N_CTX = BATCH * SEQ
N_LAT = DEC_BATCH * DEC_SEQ
N_TOK = N_CTX + N_LAT

LANES = 128
PAIR_W = 2 * HEAD_DIM
TM = 512
CTX_BLOCKS = N_CTX // TM
TOK_BLOCKS = N_TOK // TM
ROW_GROUPS = 2
TQ = 256
N_CHUNK = 256
NEG_BIG = -1e30
VMEM_LIMIT = 56 * 1024 * 1024

f32 = jnp.float32
bf16 = jnp.bfloat16


def _params(n_axes):
    return pltpu.CompilerParams(dimension_semantics=("arbitrary",) * n_axes,
                                vmem_limit_bytes=VMEM_LIMIT)


def _dot(a, b):
    return jnp.dot(a, b, preferred_element_type=f32)


def _dot_nt(a, b):
    return lax.dot_general(a, b, (((1,), (1,)), ((), ())), preferred_element_type=f32)


def _ln(x):
    mu = jnp.mean(x, -1, keepdims=True)
    d = x - mu
    var = jnp.mean(d * d, -1, keepdims=True)
    return d * lax.rsqrt(var + LN_EPS)


def _silu(x):
    return x * jax.nn.sigmoid(x)


def _split_bf16(x):
    hi = x.astype(bf16)
    lo = (x - hi.astype(f32)).astype(bf16)
    return hi, lo


def _lane_half_mask(shape):
    return (lax.broadcasted_iota(jnp.int32, shape, len(shape) - 1) & HEAD_DIM) == 0


def _mod_kernel(c_ref, w_ref, b_ref, o_ref):
    a = _silu(c_ref[...])
    a_hi, a_lo = _split_bf16(a)
    w_hi, w_lo = _split_bf16(w_ref[0])
    acc = _dot(a_hi, w_hi) + _dot(a_lo, w_hi) + _dot(a_hi, w_lo)
    o_ref[0] = acc + b_ref[0]


def _modulation(cond, w_mod, b_mod):
    tn = 1536
    rows = cond.shape[0]
    return pl.pallas_call(
        _mod_kernel,
        grid=(DEPTH, 6 * D_MODEL // tn),
        in_specs=[pl.BlockSpec((rows, D_MODEL), lambda l, j: (0, 0)),
                  pl.BlockSpec((1, D_MODEL, tn), lambda l, j: (l, 0, j)),
                  pl.BlockSpec((1, 1, tn), lambda l, j: (l, 0, j))],
        out_specs=pl.BlockSpec((1, rows, tn), lambda l, j: (l, 0, j)),
        out_shape=jax.ShapeDtypeStruct((DEPTH, rows, 6 * D_MODEL), f32),
        compiler_params=_params(2),
        name="modulation",
    )(cond, w_mod, b_mod.reshape(DEPTH, 1, 6 * D_MODEL))


def _tok(i, n_w):
    return jnp.maximum(i - n_w, 0)


def _ctx_blk(t):
    return jnp.minimum(t, CTX_BLOCKS - 1)


def _lat_blk(t):
    return jnp.maximum(t - CTX_BLOCKS, 0)


def _mod_row(t):
    return jnp.where(t < CTX_BLOCKS, 0, 1 + _lat_blk(t) * TM // DEC_SEQ)


def _token_specs(parts, n_w):
    width = parts[0].shape[1]
    if len(parts) == 1:
        return [pl.BlockSpec((TM, width), lambda i: (_tok(i, n_w), 0))]
    return [pl.BlockSpec((TM, width), lambda i: (_ctx_blk(_tok(i, n_w)), 0)),
            pl.BlockSpec((TM, width), lambda i: (_lat_blk(_tok(i, n_w)), 0))]


def _pick(refs, is_ctx, rs):
    if len(refs) == 1:
        return refs[0][rs, :]
    return jnp.where(is_ctx, refs[0][rs, :], refs[1][rs, :])


def _rope_pair(y, cos, sin_signed):
    first_half = (lax.broadcasted_iota(jnp.int32, y.shape, 1) & (HEAD_DIM // 2)) == 0
    swapped = jnp.where(first_half, pltpu.roll(y, LANES - HEAD_DIM // 2, 1), pltpu.roll(y, HEAD_DIM // 2, 1))
    return y * cos + swapped * sin_signed


def _proj_kernel(*refs, n_x, n_cache, n_w, rope_tiles, scale_tiles, cache_plan):
    x_refs = refs[:n_x]
    mod_ref, w_ref, cos_ref, sin_ref, o_ref = refs[n_x:n_x + 5]
    cache_refs = refs[n_x + 5:n_x + 5 + n_cache]
    wbf_ref, u_ref = refs[n_x + 5 + n_cache:]
    i = pl.program_id(0)

    @pl.when(i < n_w)
    def _():
        wbf_ref[i] = w_ref[0].astype(bf16)

    def tokens(is_ctx):
        x_ref = x_refs[0] if is_ctx else x_refs[-1]
        shift = mod_ref[0, 0:1, :]
        scale = mod_ref[0, 1:2, :]
        groups = [slice(b * SEQ, (b + 1) * SEQ) for b in range(TM // SEQ)]
        for rs in groups:
            u_ref[rs, :] = (_ln(x_ref[rs, :]) * (1.0 + scale) + shift).astype(bf16)
        for c in range(n_w):
            for b, rs in enumerate(groups):
                y = _dot(u_ref[rs, :], wbf_ref[c])
                for t in range(N_CHUNK // LANES):
                    tile = c * (N_CHUNK // LANES) + t
                    piece = y[:, t * LANES:(t + 1) * LANES]
                    if tile in rope_tiles and not is_ctx:
                        piece = _rope_pair(piece, cos_ref[rs, :], sin_ref[rs, :])
                    if tile in scale_tiles:
                        piece = piece * (QK_SCALE * LOG2_E)
                    o_ref[rs, tile * LANES:(tile + 1) * LANES] = piece.astype(o_ref.dtype)
                    if tile in cache_plan and is_ctx:
                        kind, out_idx, slot = cache_plan[tile]
                        c_ref = cache_refs[out_idx]
                        if kind == "plain":
                            c_ref[b, 0, slot] = piece
                        else:
                            piece_t = piece.T
                            if kind == "heads":
                                c_ref[b, 0, 0] = piece_t[0:HEAD_DIM]
                                c_ref[b, 0, 1] = piece_t[HEAD_DIM:]
                            else:
                                c_ref[b, 0, slot, 0] = piece_t[0:HEAD_DIM]
                                c_ref[b, 0, slot, 1] = piece_t[HEAD_DIM:]

    t = i - n_w

    @pl.when(jnp.logical_and(t >= 0, t < CTX_BLOCKS))
    def _():
        tokens(True)

    @pl.when(t >= CTX_BLOCKS)
    def _():
        tokens(False)


def _proj(x_parts, mod, w_all, layer, scale_tiles, rope_tabs, rope_tiles, cache_shapes, cache_plan):
    n_out = w_all.shape[2]
    n_w = n_out // N_CHUNK
    nb = DEC_SEQ // TM
    tok = lambda i: _tok(i, n_w)
    in_specs = _token_specs(x_parts, n_w) + [
        pl.BlockSpec((1, 6, D_MODEL), lambda i: (_mod_row(tok(i)), 0, 0)),
        pl.BlockSpec((1, D_MODEL, N_CHUNK), lambda i: (layer, 0, jnp.minimum(i, n_w - 1))),
        pl.BlockSpec((TM, LANES), lambda i: (_lat_blk(tok(i)) % nb, 0)),
        pl.BlockSpec((TM, LANES), lambda i: (_lat_blk(tok(i)) % nb, 0))]
    out_specs = [pl.BlockSpec((TM, n_out), lambda i: (tok(i), 0))]
    out_shape = [jax.ShapeDtypeStruct((N_TOK, n_out), bf16)]
    for shp in cache_shapes:
        blk = (TM // SEQ,) + tuple(shp[1:])
        out_specs.append(pl.BlockSpec(blk, lambda i, nd=len(shp): (_ctx_blk(tok(i)),) + (0,) * (nd - 1)))
        out_shape.append(jax.ShapeDtypeStruct(tuple(shp), f32))
    return pl.pallas_call(
        functools.partial(_proj_kernel, n_x=len(x_parts), n_cache=len(cache_shapes), n_w=n_w,
                          rope_tiles=frozenset(rope_tiles), scale_tiles=frozenset(scale_tiles),
                          cache_plan=dict(cache_plan)),
        grid=(n_w + TOK_BLOCKS,),
        in_specs=in_specs,
        out_specs=out_specs,
        out_shape=out_shape,
        scratch_shapes=[pltpu.VMEM((n_w, D_MODEL, N_CHUNK), bf16), pltpu.VMEM((TM, D_MODEL), bf16)],
        compiler_params=_params(1),
        name="proj",
    )(*x_parts, mod, w_all, *rope_tabs)


def _post_kernel(*refs, n_x, n_y, ka, kb, n_w):
    x_refs = refs[:n_x]
    (ac_ref, al_ref, bc_ref, bl_ref, mod_ref, lng_ref, lnb_ref,
     wo_ref, wg_ref, wu_ref, wd_ref) = refs[n_x:n_x + 11]
    y_refs = refs[n_x + 11:n_x + 11 + n_y]
    wo_s, wg_s, wu_s, wd_s, x1_ref, u_ref, h_ref, y_ref = refs[n_x + 11 + n_y:]
    n_wo = (ka + kb) // N_CHUNK
    i = pl.program_id(0)

    @pl.when(i < n_w)
    def _():
        wg_s[i] = wg_ref[0].astype(bf16)
        wu_s[i] = wu_ref[0].astype(bf16)
        wd_s[i] = wd_ref[0].astype(bf16)

    @pl.when(i < n_wo)
    def _():
        wo_s[i] = wo_ref[0].astype(bf16)

    @pl.when(i >= n_w)
    def _():
        is_ctx = (i - n_w) < CTX_BLOCKS
        gate1 = mod_ref[0, 2:3, :]
        shift2 = mod_ref[0, 3:4, :]
        scale2 = mod_ref[0, 4:5, :]
        gate2 = mod_ref[0, 5:6, :]
        groups = [slice(r * TM // ROW_GROUPS, (r + 1) * TM // ROW_GROUPS) for r in range(ROW_GROUPS)]
        for rs in groups:
            a = _pick((ac_ref, al_ref), is_ctx, rs)
            b = _pick((bc_ref, bl_ref), is_ctx, rs)
            pieces = ([a[:, c:c + N_CHUNK] for c in range(0, ka, N_CHUNK)]
                      + [b[:, c:c + N_CHUNK] for c in range(0, kb, N_CHUNK)])
            h = functools.reduce(lambda s, p: s + p, [_dot(p, wo_s[c]) for c, p in enumerate(pieces)])
            x1 = _ln(ALPHA * _pick(x_refs, is_ctx, rs) + gate1 * h) * lng_ref[0, 0:1, :] + lnb_ref[0, 0:1, :]
            x1_ref[rs, :] = x1
            u_ref[rs, :] = (_ln(x1) * (1.0 + scale2) + shift2).astype(bf16)
        for c in range(n_w):
            cols = slice(c * N_CHUNK, (c + 1) * N_CHUNK)
            for rs in groups:
                g = _dot(u_ref[rs, :], wg_s[c])
                up = _dot(u_ref[rs, :], wu_s[c])
                h_ref[rs, cols] = (_silu(g) * up).astype(bf16)
        for rs in groups:
            ffn = functools.reduce(lambda s, p: s + p,
                                   [_dot(h_ref[rs, c * N_CHUNK:(c + 1) * N_CHUNK], wd_s[c]) for c in range(n_w)])
            y_ref[rs, :] = _ln(ALPHA * x1_ref[rs, :] + gate2 * ffn) * lng_ref[0, 1:2, :] + lnb_ref[0, 1:2, :]
        if n_y == 1:
            y_refs[0][...] = y_ref[...]
        else:
            @pl.when(is_ctx)
            def _():
                y_refs[0][...] = y_ref[...]

            @pl.when(jnp.logical_not(is_ctx))
            def _():
                y_refs[1][...] = y_ref[...]


def _post(x_parts, mix_a, mix_b, mod, ln_g, ln_b, w_out, w_gate, w_up, w_down, layer, mix_layer, split_out):
    ka, kb = mix_a[0].shape[1], mix_b[0].shape[1]
    n_w = D_FF // N_CHUNK
    n_wo = (ka + kb) // N_CHUNK
    tok = lambda i: _tok(i, n_w)
    lay = lambda i: (layer, 0, 0)
    in_specs = (_token_specs(x_parts, n_w) + _token_specs(mix_a, n_w) + _token_specs(mix_b, n_w) + [
        pl.BlockSpec((1, 6, D_MODEL), lambda i: (_mod_row(tok(i)), 0, 0)),
        pl.BlockSpec((1, 2, D_MODEL), lay),
        pl.BlockSpec((1, 2, D_MODEL), lay),
        pl.BlockSpec((1, N_CHUNK, D_MODEL), lambda i: (mix_layer, jnp.minimum(i, n_wo - 1), 0)),
        pl.BlockSpec((1, D_MODEL, N_CHUNK), lambda i: (layer, 0, jnp.minimum(i, n_w - 1))),
        pl.BlockSpec((1, D_MODEL, N_CHUNK), lambda i: (layer, 0, jnp.minimum(i, n_w - 1))),
        pl.BlockSpec((1, N_CHUNK, D_MODEL), lambda i: (layer, jnp.minimum(i, n_w - 1), 0))])
    if split_out:
        out_specs = [pl.BlockSpec((TM, D_MODEL), lambda i: (_ctx_blk(tok(i)), 0)),
                     pl.BlockSpec((TM, D_MODEL), lambda i: (_lat_blk(tok(i)), 0))]
        out_shape = [jax.ShapeDtypeStruct((N_CTX, D_MODEL), f32), jax.ShapeDtypeStruct((N_LAT, D_MODEL), f32)]
    else:
        out_specs = [pl.BlockSpec((TM, D_MODEL), lambda i: (tok(i), 0))]
        out_shape = [jax.ShapeDtypeStruct((N_TOK, D_MODEL), f32)]
    return pl.pallas_call(
        functools.partial(_post_kernel, n_x=len(x_parts), n_y=len(out_shape), ka=ka, kb=kb, n_w=n_w),
        grid=(n_w + TOK_BLOCKS,),
        in_specs=in_specs,
        out_specs=out_specs,
        out_shape=out_shape,
        scratch_shapes=[pltpu.VMEM((n_wo, N_CHUNK, D_MODEL), bf16), pltpu.VMEM((n_w, D_MODEL, N_CHUNK), bf16),
                        pltpu.VMEM((n_w, D_MODEL, N_CHUNK), bf16), pltpu.VMEM((n_w, N_CHUNK, D_MODEL), bf16),
                        pltpu.VMEM((TM, D_MODEL), f32), pltpu.VMEM((TM, D_MODEL), bf16),
                        pltpu.VMEM((TM, D_FF), bf16), pltpu.VMEM((TM, D_MODEL), f32)],
        compiler_params=_params(1),
        name="post",
    )(*x_parts, *mix_a, *mix_b, mod, ln_g, ln_b, w_out, w_gate, w_up, w_down)


def _group_norm_gate(ro, rg, gmat, gn_g, gn_b):
    def gmean(parts):
        cols = []
        for c in range(0, RET_W, N_CHUNK):
            cols.append(sum(_dot(p[:, c:c + N_CHUNK], gmat) for p in parts))
        return jnp.concatenate(cols, -1)

    d = ro - gmean(_split_bf16(ro))
    var = gmean([(d * d).astype(bf16)])
    y = d * lax.rsqrt(var + LN_EPS) * gn_g + gn_b
    return _silu(rg.astype(f32)) * y


def _dup_head(x, j):
    first = _lane_half_mask(x.shape)
    keep = first if j == 0 else jnp.logical_not(first)
    xm = jnp.where(keep, x.astype(f32), 0.0)
    return xm + pltpu.roll(xm, HEAD_DIM, 1)


def _softmax_parts(scores, sink):
    m = sink
    for s in scores:
        m = jnp.maximum(m, jnp.max(s, -1, keepdims=True))
    es = [jnp.exp2(s - m) for s in scores]
    denom = jnp.exp2(sink - m)
    for e in es:
        denom = denom + jnp.sum(e, -1, keepdims=True)
    return es, denom


def _ctx_ab_kernel(lg_ref, sink_ref, rq_ref, rk_ref, rv_ref, rg_ref, wq_ref, wk_ref, wv_ref,
                   lgf_ref, lgb_ref, gmat_ref, gng_ref, gnb_ref,
                   ro_ref, wo_ref, st_ref, dmask_ref, kdec_ref, ret_ref):
    t_len = SEQ

    @pl.when(pl.program_id(0) == 0)
    def _():
        row = lax.broadcasted_iota(jnp.int32, (t_len, t_len), 0)
        col = lax.broadcasted_iota(jnp.int32, (t_len, t_len), 1)
        diff = (row - col).astype(f32)
        diag = jnp.where(row == col, 2.0 * QK_SCALE, QK_SCALE)
        for h in range(H_RET):
            dmask_ref[h] = jnp.exp(jnp.where(diff >= 0, lg_ref[0, h] * diff, -lg_ref[1, h] * diff)) * diag
        t = lax.broadcasted_iota(jnp.int32, (t_len, RET_W), 0).astype(f32)
        kdec_ref[0] = jnp.exp(lgf_ref[...] * (t_len - 1.0 - t)) * QK_SCALE
        kdec_ref[1] = jnp.exp(lgb_ref[...] * t) * QK_SCALE

    first = _lane_half_mask((t_len, PAIR_W))
    for p in range(H_RET // 2):
        sl = slice(p * PAIR_W, (p + 1) * PAIR_W)
        q = rq_ref[:, sl]
        kb = rk_ref[:, sl]
        v = rv_ref[:, sl]
        outs = []
        for e in range(2):
            keep = first if e == 0 else jnp.logical_not(first)
            s = _dot_nt(q, jnp.where(keep, kb, jnp.zeros_like(kb))) * dmask_ref[2 * p + e]
            outs.append(_dot(s.astype(bf16), v))
        ret_ref[:, sl] = jnp.where(first, outs[0], outs[1])
        for d in range(2):
            kd_t = (kb * kdec_ref[d, :, sl]).T.astype(bf16)
            st = _dot(kd_t, v)
            st_ref[0, d, 2 * p] = st[0:HEAD_DIM, 0:HEAD_DIM]
            st_ref[0, d, 2 * p + 1] = pltpu.roll(st[HEAD_DIM:, :], HEAD_DIM, 1)[:, 0:HEAD_DIM]
    ro_ref[...] = _group_norm_gate(ret_ref[...], rg_ref[...], gmat_ref[...], gng_ref[...], gnb_ref[...]).astype(bf16)

    for j in range(KV_WIN):
        k_dup = _dup_head(wk_ref[...], j).astype(bf16)
        v_dup = _dup_head(wv_ref[...], j).astype(bf16)
        for pp in range(G_WIN // 2):
            col = (j * G_WIN + 2 * pp) * HEAD_DIM
            qb = wq_ref[:, col:col + PAIR_W]
            outs = []
            for e in range(2):
                keep = first if e == 0 else jnp.logical_not(first)
                s = _dot_nt(jnp.where(keep, qb, jnp.zeros_like(qb)), k_dup)
                (es,), denom = _softmax_parts([s], sink_ref[0, j * G_WIN + 2 * pp + e] * LOG2_E)
                outs.append(_dot(es.astype(bf16), v_dup) / denom)
            wo_ref[:, col:col + PAIR_W] = jnp.where(first, outs[0], outs[1]).astype(bf16)


def _ctx_ab(proj, log_gamma, sink, lgf_lanes, lgb_lanes, gmat, gn_g, gn_b):
    t = SEQ
    smem = pl.BlockSpec(memory_space=pltpu.SMEM)
    const = lambda b: (0, 0)
    col = lambda c: (lambda b: (b, c))
    return pl.pallas_call(
        _ctx_ab_kernel,
        grid=(BATCH,),
        in_specs=[smem, smem,
                  pl.BlockSpec((t, RET_W), col(0)), pl.BlockSpec((t, RET_W), col(1)),
                  pl.BlockSpec((t, RET_W), col(2)), pl.BlockSpec((t, RET_W), col(3)),
                  pl.BlockSpec((t, WIN_W), col(4)),
                  pl.BlockSpec((t, KV_W), col((4 * RET_W + WIN_W) // KV_W)),
                  pl.BlockSpec((t, KV_W), col((4 * RET_W + WIN_W) // KV_W + 1)),
                  pl.BlockSpec((1, RET_W), const), pl.BlockSpec((1, RET_W), const),
                  pl.BlockSpec((N_CHUNK, N_CHUNK), const),
                  pl.BlockSpec((1, RET_W), const), pl.BlockSpec((1, RET_W), const)],
        out_specs=[pl.BlockSpec((t, RET_W), lambda b: (b, 0)),
                   pl.BlockSpec((t, WIN_W), lambda b: (b, 0)),
                   pl.BlockSpec((1, 2, H_RET, HEAD_DIM, HEAD_DIM), lambda b: (b, 0, 0, 0, 0))],
        out_shape=[jax.ShapeDtypeStruct((BATCH * t, RET_W), bf16),
                   jax.ShapeDtypeStruct((BATCH * t, WIN_W), bf16),
                   jax.ShapeDtypeStruct((BATCH, 2, H_RET, HEAD_DIM, HEAD_DIM), f32)],
        scratch_shapes=[pltpu.VMEM((H_RET, t, t), f32), pltpu.VMEM((2, t, RET_W), f32),
                        pltpu.VMEM((t, RET_W), f32)],
        compiler_params=_params(1),
        name="ctx_ab",
    )(log_gamma, sink, proj, proj, proj, proj, proj, proj, proj, lgf_lanes, lgb_lanes, gmat, gn_g, gn_b)


def _lat_ab_kernel(lg_ref, sink_ref, rq_ref, rk_ref, rv_ref, rg_ref, wq_ref, wk_ref, wv_ref, ck_ref, cv_ref,
                   s0_ref, lgf_ref, lgb_ref, gmat_ref, gng_ref, gnb_ref,
                   ro_ref, wo_ref, ret_ref):
    t_len = DEC_SEQ
    q0 = pl.program_id(1) * TQ
    first = _lane_half_mask((TQ, PAIR_W))
    first_k = _lane_half_mask((t_len, PAIR_W))
    row = q0 + lax.broadcasted_iota(jnp.int32, (TQ, t_len), 0)
    col = lax.broadcasted_iota(jnp.int32, (TQ, t_len), 1)
    diff = (row - col).astype(f32)
    diag = jnp.where(row == col, 2.0 * QK_SCALE, QK_SCALE)
    t_q = (q0 + lax.broadcasted_iota(jnp.int32, (TQ, PAIR_W), 0)).astype(f32)
    for p in range(H_RET // 2):
        sl = slice(p * PAIR_W, (p + 1) * PAIR_W)
        q = rq_ref[:, sl]
        kb = rk_ref[:, sl]
        v = rv_ref[:, sl]
        outs = []
        for e in range(2):
            h = 2 * p + e
            keep = first_k if e == 0 else jnp.logical_not(first_k)
            dmask = jnp.exp(jnp.where(diff >= 0, lg_ref[0, h] * diff, -lg_ref[1, h] * diff)) * diag
            s = _dot_nt(q, jnp.where(keep, kb, jnp.zeros_like(kb))) * dmask
            outs.append(_dot(s.astype(bf16), v))
        o = jnp.where(first, outs[0], outs[1])
        o = o + _dot(q, s0_ref[0, 0, p].astype(bf16)) * jnp.exp(lgf_ref[:, sl] * (t_q + 1.0))
        o = o + _dot(q, s0_ref[0, 1, p].astype(bf16)) * jnp.exp(lgb_ref[:, sl] * (t_len - t_q))
        ret_ref[:, sl] = o
    ro_ref[...] = _group_norm_gate(ret_ref[...], rg_ref[...], gmat_ref[...], gng_ref[...], gnb_ref[...]).astype(bf16)

    band = TQ + 2 * WINDOW
    k_start = pl.multiple_of(jnp.clip(q0 - WINDOW, 0, t_len - band), LANES)
    qi = q0 + lax.broadcasted_iota(jnp.int32, (TQ, band), 0)
    kj = k_start + lax.broadcasted_iota(jnp.int32, (TQ, band), 1)
    in_band = jnp.abs(qi - kj) <= WINDOW
    for j in range(KV_WIN):
        k_dup = _dup_head(wk_ref[pl.ds(k_start, band), :], j).astype(bf16)
        v_dup = _dup_head(wv_ref[pl.ds(k_start, band), :], j).astype(bf16)
        ck_dup = _dup_head(ck_ref[0], j).astype(bf16)
        cv_dup = _dup_head(cv_ref[0], j).astype(bf16)
        for pp in range(G_WIN // 2):
            c0 = (j * G_WIN + 2 * pp) * HEAD_DIM
            qb = wq_ref[:, c0:c0 + PAIR_W]
            outs = []
            for e in range(2):
                keep = first if e == 0 else jnp.logical_not(first)
                qm = jnp.where(keep, qb, jnp.zeros_like(qb))
                s_band = jnp.where(in_band, _dot_nt(qm, k_dup), NEG_BIG)
                s_ctx = _dot_nt(qm, ck_dup)
                (e_band, e_ctx), denom = _softmax_parts([s_band, s_ctx],
                                                        sink_ref[0, j * G_WIN + 2 * pp + e] * LOG2_E)
                outs.append((_dot(e_band.astype(bf16), v_dup) + _dot(e_ctx.astype(bf16), cv_dup)) / denom)
            wo_ref[:, c0:c0 + PAIR_W] = jnp.where(first, outs[0], outs[1]).astype(bf16)


def _lat_ab(proj, log_gamma, sink, ck, cv, s0_pairs, lgf_lanes, lgb_lanes, gmat, gn_g, gn_b):
    t = DEC_SEQ
    nq = t // TQ
    smem = pl.BlockSpec(memory_space=pltpu.SMEM)
    const = lambda b, i: (0, 0)
    qcol = lambda c: (lambda b, i: (N_CTX // TQ + b * nq + i, c))
    bcol = lambda c: (lambda b, i: (N_CTX // t + b, c))
    kv_col = (4 * RET_W + WIN_W) // KV_W
    return pl.pallas_call(
        _lat_ab_kernel,
        grid=(DEC_BATCH, nq),
        in_specs=[smem, smem,
                  pl.BlockSpec((TQ, RET_W), qcol(0)), pl.BlockSpec((t, RET_W), bcol(1)),
                  pl.BlockSpec((t, RET_W), bcol(2)), pl.BlockSpec((TQ, RET_W), qcol(3)),
                  pl.BlockSpec((TQ, WIN_W), qcol(4)),
                  pl.BlockSpec((t, KV_W), bcol(kv_col)), pl.BlockSpec((t, KV_W), bcol(kv_col + 1)),
                  pl.BlockSpec((1, PAST_LEN, KV_W), lambda b, i: (b, 0, 0)),
                  pl.BlockSpec((1, PAST_LEN, KV_W), lambda b, i: (b, 0, 0)),
                  pl.BlockSpec((1, 2, H_RET // 2, PAIR_W, PAIR_W), lambda b, i: (b, 0, 0, 0, 0)),
                  pl.BlockSpec((1, RET_W), const), pl.BlockSpec((1, RET_W), const),
                  pl.BlockSpec((N_CHUNK, N_CHUNK), const),
                  pl.BlockSpec((1, RET_W), const), pl.BlockSpec((1, RET_W), const)],
        out_specs=[pl.BlockSpec((TQ, RET_W), lambda b, i: (b * nq + i, 0)),
                   pl.BlockSpec((TQ, WIN_W), lambda b, i: (b * nq + i, 0))],
        out_shape=[jax.ShapeDtypeStruct((DEC_BATCH * t, RET_W), bf16),
                   jax.ShapeDtypeStruct((DEC_BATCH * t, WIN_W), bf16)],
        scratch_shapes=[pltpu.VMEM((TQ, RET_W), f32)],
        compiler_params=_params(2),
        name="lat_ab",
    )(log_gamma, sink, proj, proj, proj, proj, proj, proj, proj, ck, cv, s0_pairs,
      lgf_lanes, lgb_lanes, gmat, gn_g, gn_b)


def _lambda_full(lam_ref, lam_init):
    lam = lam_ref[...]
    a = jnp.sum(lam[0:1, :] * lam[1:2, :], -1, keepdims=True)
    b = jnp.sum(lam[2:3, :] * lam[3:4, :], -1, keepdims=True)
    return jnp.exp(a) - jnp.exp(b) + lam_init


def _diff_head(q, k_parts, v_parts, lam, subln, lam_init):
    first = _lane_half_mask(k_parts[0].shape)
    outs = []
    for e in range(2):
        scores = []
        for k in k_parts:
            fm = _lane_half_mask(k.shape)
            keep = fm if e == 0 else jnp.logical_not(fm)
            scores.append(_dot_nt(q, jnp.where(keep, k, jnp.zeros_like(k))))
        m = scores[0].max(-1, keepdims=True)
        for s in scores[1:]:
            m = jnp.maximum(m, s.max(-1, keepdims=True))
        es = [jnp.exp2(s - m) for s in scores]
        denom = es[0].sum(-1, keepdims=True)
        for ex in es[1:]:
            denom = denom + ex.sum(-1, keepdims=True)
        pv = _dot(es[0].astype(bf16), v_parts[0])
        for ex, v in zip(es[1:], v_parts[1:]):
            pv = pv + _dot(ex.astype(bf16), v)
        outs.append(pv / denom)
    del first
    a = outs[0] - lam * outs[1]
    return a * lax.rsqrt(jnp.mean(a * a, -1, keepdims=True) + LN_EPS) * subln * (1.0 - lam_init)


def _fourier_rows(ct_ref, st_ref, z, bdc_ref, bds_ref):
    zb = z
    zc = _dot(zb, bdc_ref[...].astype(bf16)).astype(bf16)
    zs = _dot(zb, bds_ref[...].astype(bf16)).astype(bf16)
    return _dot(ct_ref[...].astype(bf16), zc) - _dot(st_ref[...].astype(bf16), zs)


def _ctx_cd_kernel(q_ref, k_ref, v_ref, z_ref, lam_ref, subln_ref, ct_ref, st_ref, bdc_ref, bds_ref,
                   a_ref, zf_ref, *, lam_init):
    lam = _lambda_full(lam_ref, lam_init)
    for h in range(H_DIFF):
        sl = slice(h * PAIR_W, (h + 1) * PAIR_W)
        a_ref[:, sl] = _diff_head(q_ref[:, sl], [k_ref[:, sl]],
                                  [v_ref[:, sl]], lam, subln_ref[...], lam_init).astype(bf16)
    zf_ref[...] = _fourier_rows(ct_ref, st_ref, z_ref[...], bdc_ref, bds_ref).astype(bf16)


def _ctx_cd(proj, lam, subln, ct, st, bdc, bds, lam_init):
    t = SEQ
    const = lambda b: (0, 0)
    col = lambda c: (lambda b: (b, c))
    return pl.pallas_call(
        functools.partial(_ctx_cd_kernel, lam_init=lam_init),
        grid=(BATCH,),
        in_specs=[pl.BlockSpec((t, DIFF_W), col(0)), pl.BlockSpec((t, DIFF_W), col(1)),
                  pl.BlockSpec((t, DIFF_W), col(2)), pl.BlockSpec((t, FNET_W), col(3 * DIFF_W // FNET_W)),
                  pl.BlockSpec((4, HEAD_DIM), const), pl.BlockSpec((1, PAIR_W), const),
                  pl.BlockSpec((t, t), const), pl.BlockSpec((t, t), const),
                  pl.BlockSpec((FNET_W, FNET_W), const), pl.BlockSpec((FNET_W, FNET_W), const)],
        out_specs=[pl.BlockSpec((t, DIFF_W), lambda b: (b, 0)), pl.BlockSpec((t, FNET_W), lambda b: (b, 0))],
        out_shape=[jax.ShapeDtypeStruct((BATCH * t, DIFF_W), bf16),
                   jax.ShapeDtypeStruct((BATCH * t, FNET_W), bf16)],
        compiler_params=_params(1),
        name="ctx_cd",
    )(proj, proj, proj, proj, lam, subln, ct, st, bdc, bds)


def _lat_cd_kernel(q_ref, k_ref, v_ref, z_ref, ck_ref, cv_ref, lam_ref, subln_ref, ct_ref, st_ref, bdc_ref, bds_ref,
                   a_ref, zf_ref, *, lam_init):
    lam = _lambda_full(lam_ref, lam_init)
    for h in range(H_DIFF):
        sl = slice(h * PAIR_W, (h + 1) * PAIR_W)
        a_ref[:, sl] = _diff_head(q_ref[:, sl],
                                  [k_ref[:, sl], ck_ref[0, :, sl].astype(bf16)],
                                  [v_ref[:, sl], cv_ref[0, :, sl].astype(bf16)],
                                  lam, subln_ref[...], lam_init).astype(bf16)
    zf_ref[...] = _fourier_rows(ct_ref, st_ref, z_ref[...], bdc_ref, bds_ref).astype(bf16)


def _lat_cd(proj, ck, cv, lam, subln, ct, st, bdc, bds, lam_init):
    t = DEC_SEQ
    nq = t // TQ
    const = lambda b, i: (0, 0)
    return pl.pallas_call(
        functools.partial(_lat_cd_kernel, lam_init=lam_init),
        grid=(DEC_BATCH, nq),
        in_specs=[pl.BlockSpec((TQ, DIFF_W), lambda b, i: (N_CTX // TQ + b * nq + i, 0)),
                  pl.BlockSpec((t, DIFF_W), lambda b, i: (N_CTX // t + b, 1)),
                  pl.BlockSpec((t, DIFF_W), lambda b, i: (N_CTX // t + b, 2)),
                  pl.BlockSpec((t, FNET_W), lambda b, i: (N_CTX // t + b, 3 * DIFF_W // FNET_W)),
                  pl.BlockSpec((1, PAST_LEN, DIFF_W), lambda b, i: (b, 0, 0)),
                  pl.BlockSpec((1, PAST_LEN, DIFF_W), lambda b, i: (b, 0, 0)),
                  pl.BlockSpec((4, HEAD_DIM), const), pl.BlockSpec((1, PAIR_W), const),
                  pl.BlockSpec((TQ, t), lambda b, i: (i, 0)), pl.BlockSpec((TQ, t), lambda b, i: (i, 0)),
                  pl.BlockSpec((FNET_W, FNET_W), const), pl.BlockSpec((FNET_W, FNET_W), const)],
        out_specs=[pl.BlockSpec((TQ, DIFF_W), lambda b, i: (b * nq + i, 0)),
                   pl.BlockSpec((TQ, FNET_W), lambda b, i: (b * nq + i, 0))],
        out_shape=[jax.ShapeDtypeStruct((DEC_BATCH * t, DIFF_W), bf16),
                   jax.ShapeDtypeStruct((DEC_BATCH * t, FNET_W), bf16)],
        compiler_params=_params(2),
        name="lat_cd",
    )(proj, proj, proj, proj, ck, cv, lam, subln, ct, st, bdc, bds)


def _rope_tables():
    t = np.arange(DEC_SEQ)
    quarter = HEAD_DIM // 4
    inv = ROPE_BASE ** (-np.arange(quarter, dtype=np.float64) / quarter)
    ang = np.concatenate([(t // GRID_W)[:, None] * inv, (t % GRID_W)[:, None] * inv], -1)
    cos, sin = np.cos(ang), np.sin(ang)
    reps = LANES // HEAD_DIM
    return (np.tile(np.concatenate([cos, cos], -1), (1, reps)).astype(np.float32),
            np.tile(np.concatenate([-sin, sin], -1), (1, reps)).astype(np.float32))


def _dft_tables(n):
    k = np.arange(n)
    ang = (2.0 * math.pi / n) * ((k[:, None] * k[None, :]) % n)
    return (np.cos(ang) / math.sqrt(n)).astype(np.float32), (np.sin(ang) / math.sqrt(n)).astype(np.float32)


def _block_diag(m, reps):
    return np.kron(np.eye(reps, dtype=m.dtype), m)


def kernel(x_prompt, x_sample, state_ret, cache_win_k, cache_win_v, cache_diff_k, cache_diff_v, c, c_ctx, w_mod, b_mod, ln_g, ln_b, w_in_ab, w_out_ab, ret_log_gamma, ret_gn_g, ret_gn_b, win_sink, w_in_cd, w_out_cd, diff_lambda, diff_subln_g, w_gate, w_up, w_down):
    cond = jnp.concatenate([c_ctx[None, :], c, jnp.zeros((8 - 1 - DEC_BATCH, D_MODEL), f32)], 0)
    mod = _modulation(cond, w_mod, b_mod).reshape(DEPTH, 8, 6, D_MODEL)

    rope_tabs = _rope_tables()
    gmat = jnp.asarray(_block_diag(np.full((HEAD_DIM, HEAD_DIM), 1.0 / HEAD_DIM, np.float32),
                                   N_CHUNK // HEAD_DIM), bf16)
    c64, s64 = _dft_tables(FNET_DIM)
    bdc = _block_diag(c64, FNET_GROUPS)
    bds = _block_diag(s64, FNET_GROUPS)
    dft_ctx = _dft_tables(SEQ)
    dft_lat = _dft_tables(DEC_SEQ)

    x_parts = [x_prompt.reshape(N_CTX, D_MODEL), x_sample.reshape(N_LAT, D_MODEL)]
    outs = {}
    for l in range(DEPTH):
        i = l // 2
        if l % 2 == 0:
            lgf = jnp.repeat(ret_log_gamma[i, 0], HEAD_DIM)[None, :]
            lgb = jnp.repeat(ret_log_gamma[i, 1], HEAD_DIM)[None, :]
            gn_g = ret_gn_g[i][None, :]
            gn_b = ret_gn_b[i][None, :]
            sink = win_sink[i][None, :]
            rope_tiles = tuple(range(0, 2 * RET_W // LANES)) + tuple(
                range(4 * RET_W // LANES, (4 * RET_W + WIN_W + KV_W) // LANES))
            kv_tile = (4 * RET_W + WIN_W) // LANES
            kv_shape = (BATCH, 1, KV_WIN, HEAD_DIM, SEQ)
            scale_tiles = tuple(range(4 * RET_W // LANES, (4 * RET_W + WIN_W) // LANES))
            proj, wk_t, wv_t = _proj(x_parts, mod[l], w_in_ab, i, scale_tiles, rope_tabs, rope_tiles,
                                     (kv_shape, kv_shape),
                                     {kv_tile: ("heads", 0, 0), kv_tile + 1: ("heads", 1, 0)})
            ro_c, wo_c, st_c = _ctx_ab(proj, ret_log_gamma[i], sink, lgf, lgb, gmat, gn_g, gn_b)
            s0 = state_ret[:, i]
            s0 = s0.reshape(DEC_BATCH, 2, H_RET // 2, 2, HEAD_DIM, HEAD_DIM)
            eye2 = jnp.eye(2, dtype=f32)
            s0_pairs = jnp.einsum('bdpeij,ef->bdpeifj', s0, eye2).reshape(
                DEC_BATCH, 2, H_RET // 2, PAIR_W, PAIR_W)
            ck = cache_win_k[:, i].reshape(DEC_BATCH, PAST_LEN, KV_W)
            cv = cache_win_v[:, i].reshape(DEC_BATCH, PAST_LEN, KV_W)
            ro_l, wo_l = _lat_ab(proj, ret_log_gamma[i], sink, ck, cv, s0_pairs, lgf, lgb, gmat, gn_g, gn_b)
            mix_a, mix_b, w_out = (ro_c, ro_l), (wo_c, wo_l), w_out_ab
            outs.setdefault('state', []).append(st_c[:, None])
            outs.setdefault('win_k', []).append(jnp.transpose(wk_t, (0, 1, 4, 2, 3)))
            outs.setdefault('win_v', []).append(jnp.transpose(wv_t, (0, 1, 4, 2, 3)))
        else:
            lam_init = 0.8 - 0.6 * math.exp(-0.3 * l)
            subln = diff_subln_g[i][None, :]
            rope_tiles = tuple(range(0, 2 * DIFF_W // LANES))
            plan = {}
            for h in range(H_DIFF):
                plan[DIFF_W // LANES + h] = ("pairs", 0, h)
                plan[2 * DIFF_W // LANES + h] = ("plain", 1, h)
            scale_tiles = tuple(range(0, DIFF_W // LANES))
            proj, dk_t, dv_h = _proj(
                x_parts, mod[l], w_in_cd, i, scale_tiles, rope_tabs, rope_tiles,
                ((BATCH, 1, H_DIFF, 2, HEAD_DIM, SEQ), (BATCH, 1, H_DIFF, SEQ, 2 * HEAD_DIM)), plan)
            a_c, z_c = _ctx_cd(proj, diff_lambda[i], subln, dft_ctx[0], dft_ctx[1], bdc, bds, lam_init)
            ck = cache_diff_k[:, i].reshape(DEC_BATCH, PAST_LEN, DIFF_W)
            cv = cache_diff_v[:, i].reshape(DEC_BATCH, PAST_LEN, DIFF_W)
            a_l, z_l = _lat_cd(proj, ck, cv, diff_lambda[i], subln, dft_lat[0], dft_lat[1], bdc, bds, lam_init)
            mix_a, mix_b, w_out = (a_c, a_l), (z_c, z_l), w_out_cd
            outs.setdefault('diff_k', []).append(jnp.transpose(dk_t, (0, 1, 5, 2, 3, 4)))
            outs.setdefault('diff_v', []).append(jnp.transpose(dv_h, (0, 1, 3, 2, 4)))
        x_parts = _post(x_parts, mix_a, mix_b, mod[l], ln_g, ln_b, w_out, w_gate, w_up, w_down, l, i,
                        split_out=(l == DEPTH - 1))

    y_prompt = x_parts[0].reshape(BATCH, SEQ, D_MODEL)
    y_sample = x_parts[1].reshape(DEC_BATCH, DEC_SEQ, D_MODEL)
    cat = lambda parts: parts[0] if len(parts) == 1 else jnp.concatenate(parts, 1)
    return (y_prompt, y_sample, cat(outs['state']), cat(outs['win_k']), cat(outs['win_v']),
            cat(outs['diff_k']), cat(outs['diff_v']))
```

```python
import functools
import math

import jax
import jax.numpy as jnp
import numpy as np
from jax import lax
from jax.experimental import pallas as pl
from jax.experimental.pallas import tpu as pltpu

D_MODEL = 1024
BATCH = 32
SEQ = 256
DEPTH = 2
DEC_BATCH = 2
DEC_SEQ = 1024
PAST_LEN = 512
GRID_W = 64
HEAD_DIM = 64
ROPE_BASE = 10000.0
H_RET = 8
H_WIN = 8
KV_WIN = 2
G_WIN = H_WIN // KV_WIN
WINDOW = 128
H_DIFF = 6
FNET_GROUPS = 4
FNET_DIM = 64
D_FF = 256 * math.ceil(8 * D_MODEL / 3 / 256)
RET_W = H_RET * HEAD_DIM
WIN_W = H_WIN * HEAD_DIM
KV_W = KV_WIN * HEAD_DIM
AB_IN = 4 * RET_W + WIN_W + 2 * KV_W
DIFF_W = H_DIFF * 2 * HEAD_DIM
FNET_W = FNET_GROUPS * FNET_DIM
CD_IN = 3 * DIFF_W + FNET_W
ALPHA = (2 * DEPTH) ** 0.25
LN_EPS = 1e-5
QK_SCALE = HEAD_DIM ** -0.5
LOG2_E = math.log2(math.e)

N_CTX = BATCH * SEQ
N_LAT = DEC_BATCH * DEC_SEQ
N_TOK = N_CTX + N_LAT

LANES = 128
PAIR_W = 2 * HEAD_DIM
TM = 512
CTX_BLOCKS = N_CTX // TM
TOK_BLOCKS = N_TOK // TM
ROW_GROUPS = 2
TQ = 256
CTX_SEQS = 2
CTX_SEQS_CD = 4
N_CHUNK = 256
NEG_BIG = -1e30
VMEM_LIMIT = 56 * 1024 * 1024

f32 = jnp.float32
bf16 = jnp.bfloat16


def _params(n_axes):
    return pltpu.CompilerParams(dimension_semantics=("arbitrary",) * n_axes,
                                vmem_limit_bytes=VMEM_LIMIT)


def _dot(a, b):
    return jnp.dot(a, b, preferred_element_type=f32)


def _dot_nt(a, b):
    return lax.dot_general(a, b, (((1,), (1,)), ((), ())), preferred_element_type=f32)


def _ln(x):
    mu = jnp.mean(x, -1, keepdims=True)
    d = x - mu
    var = jnp.mean(d * d, -1, keepdims=True)
    return d * lax.rsqrt(var + LN_EPS)


def _silu(x):
    return x * jax.nn.sigmoid(x)


def _split_bf16(x):
    hi = x.astype(bf16)
    lo = (x - hi.astype(f32)).astype(bf16)
    return hi, lo


def _lane_half_mask(shape):
    return (lax.broadcasted_iota(jnp.int32, shape, len(shape) - 1) & HEAD_DIM) == 0


def _mod_kernel(c_ref, w_ref, b_ref, o_ref):
    a = _silu(c_ref[...])
    a_hi, a_lo = _split_bf16(a)
    w_hi, w_lo = _split_bf16(w_ref[0])
    acc = _dot(a_hi, w_hi) + _dot(a_lo, w_hi) + _dot(a_hi, w_lo)
    o_ref[0] = acc + b_ref[0]


def _modulation(cond, w_mod, b_mod):
    tn = 3072
    rows = cond.shape[0]
    return pl.pallas_call(
        _mod_kernel,
        grid=(DEPTH, 6 * D_MODEL // tn),
        in_specs=[pl.BlockSpec((rows, D_MODEL), lambda l, j: (0, 0)),
                  pl.BlockSpec((1, D_MODEL, tn), lambda l, j: (l, 0, j)),
                  pl.BlockSpec((1, 1, tn), lambda l, j: (l, 0, j))],
        out_specs=pl.BlockSpec((1, rows, tn), lambda l, j: (l, 0, j)),
        out_shape=jax.ShapeDtypeStruct((DEPTH, rows, 6 * D_MODEL), f32),
        compiler_params=_params(2),
        name="modulation",
    )(cond, w_mod, b_mod.reshape(DEPTH, 1, 6 * D_MODEL))


def _tok(i, n_w):
    return jnp.maximum(i - n_w, 0)


def _ctx_blk(t):
    return jnp.minimum(t, CTX_BLOCKS - 1)


def _lat_blk(t):
    return jnp.maximum(t - CTX_BLOCKS, 0)


def _mod_row(t):
    return jnp.where(t < CTX_BLOCKS, 0, 1 + _lat_blk(t) * TM // DEC_SEQ)


def _token_specs(parts, n_w):
    width = parts[0].shape[1]
    if len(parts) == 1:
        return [pl.BlockSpec((TM, width), lambda i: (_tok(i, n_w), 0))]
    return [pl.BlockSpec((TM, width), lambda i: (_ctx_blk(_tok(i, n_w)), 0)),
            pl.BlockSpec((TM, width), lambda i: (_lat_blk(_tok(i, n_w)), 0))]


def _pick(refs, is_ctx, rs):
    if len(refs) == 1:
        return refs[0][rs, :]
    return jnp.where(is_ctx, refs[0][rs, :], refs[1][rs, :])


def _rope_pair(y, cos, sin_signed):
    first_half = (lax.broadcasted_iota(jnp.int32, y.shape, 1) & (HEAD_DIM // 2)) == 0
    swapped = jnp.where(first_half, pltpu.roll(y, LANES - HEAD_DIM // 2, 1), pltpu.roll(y, HEAD_DIM // 2, 1))
    return y * cos + swapped * sin_signed


def _proj_kernel(*refs, n_x, n_cache, n_w, rope_tiles, scale_tiles, cache_plan):
    x_refs = refs[:n_x]
    mod_ref, w_ref, cos_ref, sin_ref, o_ref = refs[n_x:n_x + 5]
    cache_refs = refs[n_x + 5:n_x + 5 + n_cache]
    wbf_ref, u_ref = refs[n_x + 5 + n_cache:]
    i = pl.program_id(0)

    @pl.when(i < n_w)
    def _():
        wbf_ref[i] = w_ref[0].astype(bf16)

    def tokens(is_ctx):
        x_ref = x_refs[0] if is_ctx else x_refs[-1]
        shift = mod_ref[0, 0:1, :]
        scale = mod_ref[0, 1:2, :]
        groups = [slice(b * SEQ, (b + 1) * SEQ) for b in range(TM // SEQ)]
        for rs in groups:
            u_ref[rs, :] = (_ln(x_ref[rs, :]) * (1.0 + scale) + shift).astype(bf16)
        for c in range(n_w):
            for b, rs in enumerate(groups):
                y = _dot(u_ref[rs, :], wbf_ref[c])
                for t in range(N_CHUNK // LANES):
                    tile = c * (N_CHUNK // LANES) + t
                    piece = y[:, t * LANES:(t + 1) * LANES]
                    if tile in rope_tiles and not is_ctx:
                        piece = _rope_pair(piece, cos_ref[rs, :], sin_ref[rs, :])
                    if tile in scale_tiles:
                        piece = piece * (QK_SCALE * LOG2_E)
                    o_ref[rs, tile * LANES:(tile + 1) * LANES] = piece.astype(o_ref.dtype)
                    if tile in cache_plan and is_ctx:
                        kind, out_idx, slot = cache_plan[tile]
                        c_ref = cache_refs[out_idx]
                        if kind == "plain":
                            c_ref[b, 0, slot] = piece
                        else:
                            piece_t = piece.T
                            if kind == "heads":
                                c_ref[b, 0, 0] = piece_t[0:HEAD_DIM]
                                c_ref[b, 0, 1] = piece_t[HEAD_DIM:]
                            else:
                                c_ref[b, 0, slot, 0] = piece_t[0:HEAD_DIM]
                                c_ref[b, 0, slot, 1] = piece_t[HEAD_DIM:]

    t = i - n_w

    @pl.when(jnp.logical_and(t >= 0, t < CTX_BLOCKS))
    def _():
        tokens(True)

    @pl.when(t >= CTX_BLOCKS)
    def _():
        tokens(False)


def _proj(x_parts, mod, w_all, layer, scale_tiles, rope_tabs, rope_tiles, cache_shapes, cache_plan):
    n_out = w_all.shape[2]
    n_w = n_out // N_CHUNK
    nb = DEC_SEQ // TM
    tok = lambda i: _tok(i, n_w)
    in_specs = _token_specs(x_parts, n_w) + [
        pl.BlockSpec((1, 6, D_MODEL), lambda i: (_mod_row(tok(i)), 0, 0)),
        pl.BlockSpec((1, D_MODEL, N_CHUNK), lambda i: (layer, 0, jnp.minimum(i, n_w - 1))),
        pl.BlockSpec((TM, LANES), lambda i: (_lat_blk(tok(i)) % nb, 0)),
        pl.BlockSpec((TM, LANES), lambda i: (_lat_blk(tok(i)) % nb, 0))]
    out_specs = [pl.BlockSpec((TM, n_out), lambda i: (tok(i), 0))]
    out_shape = [jax.ShapeDtypeStruct((N_TOK, n_out), bf16)]
    for shp in cache_shapes:
        blk = (TM // SEQ,) + tuple(shp[1:])
        out_specs.append(pl.BlockSpec(blk, lambda i, nd=len(shp): (_ctx_blk(tok(i)),) + (0,) * (nd - 1)))
        out_shape.append(jax.ShapeDtypeStruct(tuple(shp), f32))
    return pl.pallas_call(
        functools.partial(_proj_kernel, n_x=len(x_parts), n_cache=len(cache_shapes), n_w=n_w,
                          rope_tiles=frozenset(rope_tiles), scale_tiles=frozenset(scale_tiles),
                          cache_plan=dict(cache_plan)),
        grid=(n_w + TOK_BLOCKS,),
        in_specs=in_specs,
        out_specs=out_specs,
        out_shape=out_shape,
        scratch_shapes=[pltpu.VMEM((n_w, D_MODEL, N_CHUNK), bf16), pltpu.VMEM((TM, D_MODEL), bf16)],
        compiler_params=_params(1),
        name="proj",
    )(*x_parts, mod, w_all, *rope_tabs)


def _post_kernel(*refs, n_x, n_y, ka, kb, n_w):
    x_refs = refs[:n_x]
    (ac_ref, al_ref, bc_ref, bl_ref, mod_ref, lng_ref, lnb_ref,
     wo_ref, wg_ref, wu_ref, wd_ref) = refs[n_x:n_x + 11]
    y_refs = refs[n_x + 11:n_x + 11 + n_y]
    wo_s, wg_s, wu_s, wd_s, x1_ref, u_ref, h_ref, y_ref = refs[n_x + 11 + n_y:]
    n_wo = (ka + kb) // N_CHUNK
    i = pl.program_id(0)

    @pl.when(i < n_w)
    def _():
        wg_s[i] = wg_ref[0].astype(bf16)
        wu_s[i] = wu_ref[0].astype(bf16)
        wd_s[i] = wd_ref[0].astype(bf16)

    @pl.when(i < n_wo)
    def _():
        wo_s[i] = wo_ref[0].astype(bf16)

    @pl.when(i >= n_w)
    def _():
        is_ctx = (i - n_w) < CTX_BLOCKS
        gate1 = mod_ref[0, 2:3, :]
        shift2 = mod_ref[0, 3:4, :]
        scale2 = mod_ref[0, 4:5, :]
        gate2 = mod_ref[0, 5:6, :]
        groups = [slice(r * TM // ROW_GROUPS, (r + 1) * TM // ROW_GROUPS) for r in range(ROW_GROUPS)]
        for rs in groups:
            a = _pick((ac_ref, al_ref), is_ctx, rs)
            b = _pick((bc_ref, bl_ref), is_ctx, rs)
            pieces = ([a[:, c:c + N_CHUNK] for c in range(0, ka, N_CHUNK)]
                      + [b[:, c:c + N_CHUNK] for c in range(0, kb, N_CHUNK)])
            h = functools.reduce(lambda s, p: s + p, [_dot(p, wo_s[c]) for c, p in enumerate(pieces)])
            x1 = _ln(ALPHA * _pick(x_refs, is_ctx, rs) + gate1 * h) * lng_ref[0, 0:1, :] + lnb_ref[0, 0:1, :]
            x1_ref[rs, :] = x1
            u_ref[rs, :] = (_ln(x1) * (1.0 + scale2) + shift2).astype(bf16)
        for c in range(n_w):
            cols = slice(c * N_CHUNK, (c + 1) * N_CHUNK)
            for rs in groups:
                g = _dot(u_ref[rs, :], wg_s[c])
                up = _dot(u_ref[rs, :], wu_s[c])
                h_ref[rs, cols] = (_silu(g) * up).astype(bf16)
        for rs in groups:
            ffn = functools.reduce(lambda s, p: s + p,
                                   [_dot(h_ref[rs, c * N_CHUNK:(c + 1) * N_CHUNK], wd_s[c]) for c in range(n_w)])
            y_ref[rs, :] = _ln(ALPHA * x1_ref[rs, :] + gate2 * ffn) * lng_ref[0, 1:2, :] + lnb_ref[0, 1:2, :]
        if n_y == 1:
            y_refs[0][...] = y_ref[...]
        else:
            @pl.when(is_ctx)
            def _():
                y_refs[0][...] = y_ref[...]

            @pl.when(jnp.logical_not(is_ctx))
            def _():
                y_refs[1][...] = y_ref[...]


def _post(x_parts, mix_a, mix_b, mod, ln_g, ln_b, w_out, w_gate, w_up, w_down, layer, mix_layer, split_out):
    ka, kb = mix_a[0].shape[1], mix_b[0].shape[1]
    n_w = D_FF // N_CHUNK
    n_wo = (ka + kb) // N_CHUNK
    tok = lambda i: _tok(i, n_w)
    lay = lambda i: (layer, 0, 0)
    in_specs = (_token_specs(x_parts, n_w) + _token_specs(mix_a, n_w) + _token_specs(mix_b, n_w) + [
        pl.BlockSpec((1, 6, D_MODEL), lambda i: (_mod_row(tok(i)), 0, 0)),
        pl.BlockSpec((1, 2, D_MODEL), lay),
        pl.BlockSpec((1, 2, D_MODEL), lay),
        pl.BlockSpec((1, N_CHUNK, D_MODEL), lambda i: (mix_layer, jnp.minimum(i, n_wo - 1), 0)),
        pl.BlockSpec((1, D_MODEL, N_CHUNK), lambda i: (layer, 0, jnp.minimum(i, n_w - 1))),
        pl.BlockSpec((1, D_MODEL, N_CHUNK), lambda i: (layer, 0, jnp.minimum(i, n_w - 1))),
        pl.BlockSpec((1, N_CHUNK, D_MODEL), lambda i: (layer, jnp.minimum(i, n_w - 1), 0))])
    if split_out:
        out_specs = [pl.BlockSpec((TM, D_MODEL), lambda i: (_ctx_blk(tok(i)), 0)),
                     pl.BlockSpec((TM, D_MODEL), lambda i: (_lat_blk(tok(i)), 0))]
        out_shape = [jax.ShapeDtypeStruct((N_CTX, D_MODEL), f32), jax.ShapeDtypeStruct((N_LAT, D_MODEL), f32)]
    else:
        out_specs = [pl.BlockSpec((TM, D_MODEL), lambda i: (tok(i), 0))]
        out_shape = [jax.ShapeDtypeStruct((N_TOK, D_MODEL), f32)]
    return pl.pallas_call(
        functools.partial(_post_kernel, n_x=len(x_parts), n_y=len(out_shape), ka=ka, kb=kb, n_w=n_w),
        grid=(n_w + TOK_BLOCKS,),
        in_specs=in_specs,
        out_specs=out_specs,
        out_shape=out_shape,
        scratch_shapes=[pltpu.VMEM((n_wo, N_CHUNK, D_MODEL), bf16), pltpu.VMEM((n_w, D_MODEL, N_CHUNK), bf16),
                        pltpu.VMEM((n_w, D_MODEL, N_CHUNK), bf16), pltpu.VMEM((n_w, N_CHUNK, D_MODEL), bf16),
                        pltpu.VMEM((TM, D_MODEL), f32), pltpu.VMEM((TM, D_MODEL), bf16),
                        pltpu.VMEM((TM, D_FF), bf16), pltpu.VMEM((TM, D_MODEL), f32)],
        compiler_params=_params(1),
        name="post",
    )(*x_parts, *mix_a, *mix_b, mod, ln_g, ln_b, w_out, w_gate, w_up, w_down)


def _group_norm_gate(ro, rg, gmat, gn_g, gn_b):
    def gmean(parts):
        cols = []
        for c in range(0, RET_W, N_CHUNK):
            cols.append(sum(_dot(p[:, c:c + N_CHUNK], gmat) for p in parts))
        return jnp.concatenate(cols, -1)

    d = ro - gmean(_split_bf16(ro))
    var = gmean([(d * d).astype(bf16)])
    y = d * lax.rsqrt(var + LN_EPS) * gn_g + gn_b
    return _silu(rg.astype(f32)) * y


def _dup_head(x, j):
    first = _lane_half_mask(x.shape)
    keep = first if j == 0 else jnp.logical_not(first)
    xm = jnp.where(keep, x.astype(f32), 0.0)
    return xm + pltpu.roll(xm, HEAD_DIM, 1)


def _softmax_parts(scores, sink):
    m = sink
    for s in scores:
        m = jnp.maximum(m, jnp.max(s, -1, keepdims=True))
    es = [jnp.exp2(s - m) for s in scores]
    denom = jnp.exp2(sink - m)
    for e in es:
        denom = denom + jnp.sum(e, -1, keepdims=True)
    return es, denom


def _retention_tables(lg_ref, lgf_ref, lgb_ref, dmask_ref, kdec_ref, n):
    row = lax.broadcasted_iota(jnp.int32, (n, n), 0)
    col = lax.broadcasted_iota(jnp.int32, (n, n), 1)
    diff = (row - col).astype(f32)
    diag = jnp.where(row == col, 2.0 * QK_SCALE, QK_SCALE)
    for h in range(H_RET):
        dmask_ref[h] = jnp.exp(jnp.where(diff >= 0, lg_ref[0, h] * diff, -lg_ref[1, h] * diff)) * diag
    t = lax.broadcasted_iota(jnp.int32, (n, RET_W), 0).astype(f32)
    kdec_ref[0] = jnp.exp(lgf_ref[...] * (n - 1.0 - t)) * QK_SCALE
    kdec_ref[1] = jnp.exp(lgb_ref[...] * t) * QK_SCALE


def _ctx_ab_kernel(lg_ref, sink_ref, rq_ref, rk_ref, rv_ref, rg_ref, wq_ref, wk_ref, wv_ref,
                   lgf_ref, lgb_ref, gmat_ref, gng_ref, gnb_ref,
                   ro_ref, wo_ref, st_ref, dmask_ref, kdec_ref, ret_ref):
    t_len = SEQ

    @pl.when(pl.program_id(0) == 0)
    def _():
        _retention_tables(lg_ref, lgf_ref, lgb_ref, dmask_ref, kdec_ref, t_len)

    first = _lane_half_mask((t_len, PAIR_W))
    for sq in range(CTX_SEQS):
        rows = slice(sq * t_len, (sq + 1) * t_len)
        for p in range(H_RET // 2):
            sl = slice(p * PAIR_W, (p + 1) * PAIR_W)
            q = rq_ref[rows, sl]
            kb = rk_ref[rows, sl]
            v = rv_ref[rows, sl]
            outs = []
            for e in range(2):
                keep = first if e == 0 else jnp.logical_not(first)
                s = _dot_nt(q, jnp.where(keep, kb, jnp.zeros_like(kb))) * dmask_ref[2 * p + e]
                outs.append(_dot(s.astype(bf16), v))
            ret_ref[rows, sl] = jnp.where(first, outs[0], outs[1])
            for d in range(2):
                kd_t = (kb * kdec_ref[d, :, sl]).T.astype(bf16)
                st = _dot(kd_t, v)
                st_ref[sq, d, 2 * p] = st[0:HEAD_DIM, 0:HEAD_DIM]
                st_ref[sq, d, 2 * p + 1] = pltpu.roll(st[HEAD_DIM:, :], HEAD_DIM, 1)[:, 0:HEAD_DIM]
        ro_ref[rows, :] = _group_norm_gate(ret_ref[rows, :], rg_ref[rows, :], gmat_ref[...], gng_ref[...],
                                           gnb_ref[...]).astype(bf16)

        for j in range(KV_WIN):
            k_dup = _dup_head(wk_ref[rows, :], j).astype(bf16)
            v_dup = _dup_head(wv_ref[rows, :], j).astype(bf16)
            for pp in range(G_WIN // 2):
                col = (j * G_WIN + 2 * pp) * HEAD_DIM
                qb = wq_ref[rows, col:col + PAIR_W]
                outs = []
                for e in range(2):
                    keep = first if e == 0 else jnp.logical_not(first)
                    s = _dot_nt(jnp.where(keep, qb, jnp.zeros_like(qb)), k_dup)
                    (es,), denom = _softmax_parts([s], sink_ref[0, j * G_WIN + 2 * pp + e] * LOG2_E)
                    outs.append(_dot(es.astype(bf16), v_dup) / denom)
                wo_ref[rows, col:col + PAIR_W] = jnp.where(first, outs[0], outs[1]).astype(bf16)


def _ctx_ab(proj, log_gamma, sink, lgf_lanes, lgb_lanes, gmat, gn_g, gn_b):
    t = SEQ
    tb = CTX_SEQS * t
    smem = pl.BlockSpec(memory_space=pltpu.SMEM)
    const = lambda b: (0, 0)
    col = lambda c: (lambda b: (b, c))
    return pl.pallas_call(
        _ctx_ab_kernel,
        grid=(BATCH // CTX_SEQS,),
        in_specs=[smem, smem,
                  pl.BlockSpec((tb, RET_W), col(0)), pl.BlockSpec((tb, RET_W), col(1)),
                  pl.BlockSpec((tb, RET_W), col(2)), pl.BlockSpec((tb, RET_W), col(3)),
                  pl.BlockSpec((tb, WIN_W), col(4)),
                  pl.BlockSpec((tb, KV_W), col((4 * RET_W + WIN_W) // KV_W)),
                  pl.BlockSpec((tb, KV_W), col((4 * RET_W + WIN_W) // KV_W + 1)),
                  pl.BlockSpec((1, RET_W), const), pl.BlockSpec((1, RET_W), const),
                  pl.BlockSpec((N_CHUNK, N_CHUNK), const),
                  pl.BlockSpec((1, RET_W), const), pl.BlockSpec((1, RET_W), const)],
        out_specs=[pl.BlockSpec((tb, RET_W), lambda b: (b, 0)),
                   pl.BlockSpec((tb, WIN_W), lambda b: (b, 0)),
                   pl.BlockSpec((CTX_SEQS, 2, H_RET, HEAD_DIM, HEAD_DIM), lambda b: (b, 0, 0, 0, 0))],
        out_shape=[jax.ShapeDtypeStruct((BATCH * t, RET_W), bf16),
                   jax.ShapeDtypeStruct((BATCH * t, WIN_W), bf16),
                   jax.ShapeDtypeStruct((BATCH, 2, H_RET, HEAD_DIM, HEAD_DIM), f32)],
        scratch_shapes=[pltpu.VMEM((H_RET, t, t), f32), pltpu.VMEM((2, t, RET_W), f32),
                        pltpu.VMEM((tb, RET_W), f32)],
        compiler_params=_params(1),
        name="ctx_ab",
    )(log_gamma, sink, proj, proj, proj, proj, proj, proj, proj, lgf_lanes, lgb_lanes, gmat, gn_g, gn_b)


def _lat_ab_kernel(lg_ref, sink_ref, rq_ref, rk_ref, rv_ref, rg_ref, wq_ref, wk_ref, wv_ref, ck_ref, cv_ref,
                   s0_ref, lgf_ref, lgb_ref, gmat_ref, gng_ref, gnb_ref,
                   ro_ref, wo_ref, ret_ref, dmask_ref, kdec_ref, qdec_ref, sf_ref, sb_ref):
    t_len = DEC_SEQ
    n_chunks = t_len // TQ
    chunk = pl.program_id(1)
    q0 = pl.multiple_of(chunk * TQ, TQ)
    first = _lane_half_mask((TQ, PAIR_W))

    @pl.when(jnp.logical_and(pl.program_id(0) == 0, chunk == 0))
    def _():
        _retention_tables(lg_ref, lgf_ref, lgb_ref, dmask_ref, kdec_ref, TQ)
        t = lax.broadcasted_iota(jnp.int32, (TQ, RET_W), 0).astype(f32)
        qdec_ref[0] = jnp.exp(lgf_ref[...] * (t + 1.0))
        qdec_ref[1] = jnp.exp(lgb_ref[...] * (TQ - t))

    @pl.when(chunk == 0)
    def _():
        r = lax.broadcasted_iota(jnp.int32, (PAIR_W, PAIR_W), 0)
        c_ = lax.broadcasted_iota(jnp.int32, (PAIR_W, PAIR_W), 1)
        same_head = (r < HEAD_DIM) == (c_ < HEAD_DIM)
        for p in range(H_RET // 2):
            sl = slice(p * PAIR_W, (p + 1) * PAIR_W)
            kv = []
            for c in range(n_chunks):
                rows = slice(c * TQ, (c + 1) * TQ)
                kc = rk_ref[rows, sl]
                vc = rv_ref[rows, sl]
                kv.append([jnp.where(same_head, _dot((kc * kdec_ref[d, :, sl]).T.astype(bf16), vc), 0.0)
                           for d in range(2)])
            state = s0_ref[0, 0, p]
            for c in range(n_chunks):
                sf_ref[c, p] = state
                state = state * jnp.exp(lgf_ref[:, sl] * TQ) + kv[c][0]
            state = s0_ref[0, 1, p]
            for c in reversed(range(n_chunks)):
                sb_ref[c, p] = state
                state = state * jnp.exp(lgb_ref[:, sl] * TQ) + kv[c][1]

    for p in range(H_RET // 2):
        sl = slice(p * PAIR_W, (p + 1) * PAIR_W)
        q = rq_ref[:, sl]
        kb = rk_ref[pl.ds(q0, TQ), sl]
        v = rv_ref[pl.ds(q0, TQ), sl]
        outs = []
        for e in range(2):
            keep = first if e == 0 else jnp.logical_not(first)
            s = _dot_nt(q, jnp.where(keep, kb, jnp.zeros_like(kb))) * dmask_ref[2 * p + e]
            outs.append(_dot(s.astype(bf16), v))
        o = jnp.where(first, outs[0], outs[1])
        o = o + _dot(q, sf_ref[chunk, p].astype(bf16)) * qdec_ref[0, :, sl]
        o = o + _dot(q, sb_ref[chunk, p].astype(bf16)) * qdec_ref[1, :, sl]
        ret_ref[:, sl] = o
    ro_ref[...] = _group_norm_gate(ret_ref[...], rg_ref[...], gmat_ref[...], gng_ref[...], gnb_ref[...]).astype(bf16)

    band = TQ + 2 * WINDOW
    k_start = pl.multiple_of(jnp.clip(q0 - WINDOW, 0, t_len - band), LANES)
    qi = q0 + lax.broadcasted_iota(jnp.int32, (TQ, band), 0)
    kj = k_start + lax.broadcasted_iota(jnp.int32, (TQ, band), 1)
    in_band = jnp.abs(qi - kj) <= WINDOW
    for j in range(KV_WIN):
        k_dup = _dup_head(wk_ref[pl.ds(k_start, band), :], j).astype(bf16)
        v_dup = _dup_head(wv_ref[pl.ds(k_start, band), :], j).astype(bf16)
        ck_dup = _dup_head(ck_ref[0], j).astype(bf16)
        cv_dup = _dup_head(cv_ref[0], j).astype(bf16)
        for pp in range(G_WIN // 2):
            c0 = (j * G_WIN + 2 * pp) * HEAD_DIM
            qb = wq_ref[:, c0:c0 + PAIR_W]
            outs = []
            for e in range(2):
                keep = first if e == 0 else jnp.logical_not(first)
                qm = jnp.where(keep, qb, jnp.zeros_like(qb))
                s_band = jnp.where(in_band, _dot_nt(qm, k_dup), NEG_BIG)
                s_ctx = _dot_nt(qm, ck_dup)
                (e_band, e_ctx), denom = _softmax_parts([s_band, s_ctx],
                                                        sink_ref[0, j * G_WIN + 2 * pp + e] * LOG2_E)
                outs.append((_dot(e_band.astype(bf16), v_dup) + _dot(e_ctx.astype(bf16), cv_dup)) / denom)
            wo_ref[:, c0:c0 + PAIR_W] = jnp.where(first, outs[0], outs[1]).astype(bf16)


def _lat_ab(proj, log_gamma, sink, ck, cv, s0_pairs, lgf_lanes, lgb_lanes, gmat, gn_g, gn_b):
    t = DEC_SEQ
    nq = t // TQ
    smem = pl.BlockSpec(memory_space=pltpu.SMEM)
    const = lambda b, i: (0, 0)
    qcol = lambda c: (lambda b, i: (N_CTX // TQ + b * nq + i, c))
    bcol = lambda c: (lambda b, i: (N_CTX // t + b, c))
    kv_col = (4 * RET_W + WIN_W) // KV_W
    return pl.pallas_call(
        _lat_ab_kernel,
        grid=(DEC_BATCH, nq),
        in_specs=[smem, smem,
                  pl.BlockSpec((TQ, RET_W), qcol(0)), pl.BlockSpec((t, RET_W), bcol(1)),
                  pl.BlockSpec((t, RET_W), bcol(2)), pl.BlockSpec((TQ, RET_W), qcol(3)),
                  pl.BlockSpec((TQ, WIN_W), qcol(4)),
                  pl.BlockSpec((t, KV_W), bcol(kv_col)), pl.BlockSpec((t, KV_W), bcol(kv_col + 1)),
                  pl.BlockSpec((1, PAST_LEN, KV_W), lambda b, i: (b, 0, 0)),
                  pl.BlockSpec((1, PAST_LEN, KV_W), lambda b, i: (b, 0, 0)),
                  pl.BlockSpec((1, 2, H_RET // 2, PAIR_W, PAIR_W), lambda b, i: (b, 0, 0, 0, 0)),
                  pl.BlockSpec((1, RET_W), const), pl.BlockSpec((1, RET_W), const),
                  pl.BlockSpec((N_CHUNK, N_CHUNK), const),
                  pl.BlockSpec((1, RET_W), const), pl.BlockSpec((1, RET_W), const)],
        out_specs=[pl.BlockSpec((TQ, RET_W), lambda b, i: (b * nq + i, 0)),
                   pl.BlockSpec((TQ, WIN_W), lambda b, i: (b * nq + i, 0))],
        out_shape=[jax.ShapeDtypeStruct((DEC_BATCH * t, RET_W), bf16),
                   jax.ShapeDtypeStruct((DEC_BATCH * t, WIN_W), bf16)],
        scratch_shapes=[pltpu.VMEM((TQ, RET_W), f32), pltpu.VMEM((H_RET, TQ, TQ), f32),
                        pltpu.VMEM((2, TQ, RET_W), f32), pltpu.VMEM((2, TQ, RET_W), f32),
                        pltpu.VMEM((t // TQ, H_RET // 2, PAIR_W, PAIR_W), f32),
                        pltpu.VMEM((t // TQ, H_RET // 2, PAIR_W, PAIR_W), f32)],
        compiler_params=_params(2),
        name="lat_ab",
    )(log_gamma, sink, proj, proj, proj, proj, proj, proj, proj, ck, cv, s0_pairs,
      lgf_lanes, lgb_lanes, gmat, gn_g, gn_b)


def _lambda_full(lam_ref, lam_init):
    lam = lam_ref[...]
    a = jnp.sum(lam[0:1, :] * lam[1:2, :], -1, keepdims=True)
    b = jnp.sum(lam[2:3, :] * lam[3:4, :], -1, keepdims=True)
    return jnp.exp(a) - jnp.exp(b) + lam_init


def _diff_head(q, k_parts, v_parts, lam, subln, lam_init):
    first = _lane_half_mask(k_parts[0].shape)
    outs = []
    for e in range(2):
        scores = []
        for k in k_parts:
            fm = _lane_half_mask(k.shape)
            keep = fm if e == 0 else jnp.logical_not(fm)
            scores.append(_dot_nt(q, jnp.where(keep, k, jnp.zeros_like(k))))
        m = scores[0].max(-1, keepdims=True)
        for s in scores[1:]:
            m = jnp.maximum(m, s.max(-1, keepdims=True))
        es = [jnp.exp2(s - m) for s in scores]
        denom = es[0].sum(-1, keepdims=True)
        for ex in es[1:]:
            denom = denom + ex.sum(-1, keepdims=True)
        pv = _dot(es[0].astype(bf16), v_parts[0])
        for ex, v in zip(es[1:], v_parts[1:]):
            pv = pv + _dot(ex.astype(bf16), v)
        outs.append(pv / denom)
    del first
    a = outs[0] - lam * outs[1]
    return a * lax.rsqrt(jnp.mean(a * a, -1, keepdims=True) + LN_EPS) * subln * (1.0 - lam_init)


def _fourier_rows(ct_ref, st_ref, z, bdc_ref, bds_ref):
    zb = z
    zc = _dot(zb, bdc_ref[...].astype(bf16)).astype(bf16)
    zs = _dot(zb, bds_ref[...].astype(bf16)).astype(bf16)
    return _dot(ct_ref[...].astype(bf16), zc) - _dot(st_ref[...].astype(bf16), zs)


def _ctx_cd_kernel(q_ref, k_ref, v_ref, z_ref, lam_ref, subln_ref, ct_ref, st_ref, bdc_ref, bds_ref,
                   a_ref, zf_ref, *, lam_init):
    lam = _lambda_full(lam_ref, lam_init)
    for sq in range(CTX_SEQS_CD):
        rows = slice(sq * SEQ, (sq + 1) * SEQ)
        for h in range(H_DIFF):
            sl = slice(h * PAIR_W, (h + 1) * PAIR_W)
            a_ref[rows, sl] = _diff_head(q_ref[rows, sl], [k_ref[rows, sl]],
                                         [v_ref[rows, sl]], lam, subln_ref[...], lam_init).astype(bf16)
        zf_ref[rows, :] = _fourier_rows(ct_ref, st_ref, z_ref[rows, :], bdc_ref, bds_ref).astype(bf16)


def _ctx_cd(proj, lam, subln, ct, st, bdc, bds, lam_init):
    t = SEQ
    tb = CTX_SEQS_CD * t
    const = lambda b: (0, 0)
    col = lambda c: (lambda b: (b, c))
    return pl.pallas_call(
        functools.partial(_ctx_cd_kernel, lam_init=lam_init),
        grid=(BATCH // CTX_SEQS_CD,),
        in_specs=[pl.BlockSpec((tb, DIFF_W), col(0)), pl.BlockSpec((tb, DIFF_W), col(1)),
                  pl.BlockSpec((tb, DIFF_W), col(2)), pl.BlockSpec((tb, FNET_W), col(3 * DIFF_W // FNET_W)),
                  pl.BlockSpec((4, HEAD_DIM), const), pl.BlockSpec((1, PAIR_W), const),
                  pl.BlockSpec((t, t), const), pl.BlockSpec((t, t), const),
                  pl.BlockSpec((FNET_W, FNET_W), const), pl.BlockSpec((FNET_W, FNET_W), const)],
        out_specs=[pl.BlockSpec((tb, DIFF_W), lambda b: (b, 0)), pl.BlockSpec((tb, FNET_W), lambda b: (b, 0))],
        out_shape=[jax.ShapeDtypeStruct((BATCH * t, DIFF_W), bf16),
                   jax.ShapeDtypeStruct((BATCH * t, FNET_W), bf16)],
        compiler_params=_params(1),
        name="ctx_cd",
    )(proj, proj, proj, proj, lam, subln, ct, st, bdc, bds)


def _lat_cd_kernel(q_ref, k_ref, v_ref, z_ref, ck_ref, cv_ref, lam_ref, subln_ref, ct_ref, st_ref, bdc_ref, bds_ref,
                   a_ref, zf_ref, *, lam_init):
    lam = _lambda_full(lam_ref, lam_init)
    for h in range(H_DIFF):
        sl = slice(h * PAIR_W, (h + 1) * PAIR_W)
        a_ref[:, sl] = _diff_head(q_ref[:, sl],
                                  [k_ref[:, sl], ck_ref[0, :, sl].astype(bf16)],
                                  [v_ref[:, sl], cv_ref[0, :, sl].astype(bf16)],
                                  lam, subln_ref[...], lam_init).astype(bf16)
    zf_ref[...] = _fourier_rows(ct_ref, st_ref, z_ref[...], bdc_ref, bds_ref).astype(bf16)


def _lat_cd(proj, ck, cv, lam, subln, ct, st, bdc, bds, lam_init):
    t = DEC_SEQ
    nq = t // TQ
    const = lambda b, i: (0, 0)
    return pl.pallas_call(
        functools.partial(_lat_cd_kernel, lam_init=lam_init),
        grid=(DEC_BATCH, nq),
        in_specs=[pl.BlockSpec((TQ, DIFF_W), lambda b, i: (N_CTX // TQ + b * nq + i, 0)),
                  pl.BlockSpec((t, DIFF_W), lambda b, i: (N_CTX // t + b, 1)),
                  pl.BlockSpec((t, DIFF_W), lambda b, i: (N_CTX // t + b, 2)),
                  pl.BlockSpec((t, FNET_W), lambda b, i: (N_CTX // t + b, 3 * DIFF_W // FNET_W)),
                  pl.BlockSpec((1, PAST_LEN, DIFF_W), lambda b, i: (b, 0, 0)),
                  pl.BlockSpec((1, PAST_LEN, DIFF_W), lambda b, i: (b, 0, 0)),
                  pl.BlockSpec((4, HEAD_DIM), const), pl.BlockSpec((1, PAIR_W), const),
                  pl.BlockSpec((TQ, t), lambda b, i: (i, 0)), pl.BlockSpec((TQ, t), lambda b, i: (i, 0)),
                  pl.BlockSpec((FNET_W, FNET_W), const), pl.BlockSpec((FNET_W, FNET_W), const)],
        out_specs=[pl.BlockSpec((TQ, DIFF_W), lambda b, i: (b * nq + i, 0)),
                   pl.BlockSpec((TQ, FNET_W), lambda b, i: (b * nq + i, 0))],
        out_shape=[jax.ShapeDtypeStruct((DEC_BATCH * t, DIFF_W), bf16),
                   jax.ShapeDtypeStruct((DEC_BATCH * t, FNET_W), bf16)],
        compiler_params=_params(2),
        name="lat_cd",
    )(proj, proj, proj, proj, ck, cv, lam, subln, ct, st, bdc, bds)


def _rope_tables():
    t = np.arange(DEC_SEQ)
    quarter = HEAD_DIM // 4
    inv = ROPE_BASE ** (-np.arange(quarter, dtype=np.float64) / quarter)
    ang = np.concatenate([(t // GRID_W)[:, None] * inv, (t % GRID_W)[:, None] * inv], -1)
    cos, sin = np.cos(ang), np.sin(ang)
    reps = LANES // HEAD_DIM
    return (np.tile(np.concatenate([cos, cos], -1), (1, reps)).astype(np.float32),
            np.tile(np.concatenate([-sin, sin], -1), (1, reps)).astype(np.float32))


def _dft_tables(n):
    k = np.arange(n)
    ang = (2.0 * math.pi / n) * ((k[:, None] * k[None, :]) % n)
    return (np.cos(ang) / math.sqrt(n)).astype(np.float32), (np.sin(ang) / math.sqrt(n)).astype(np.float32)


def _block_diag(m, reps):
    return np.kron(np.eye(reps, dtype=m.dtype), m)


def kernel(x_prompt, x_sample, state_ret, cache_win_k, cache_win_v, cache_diff_k, cache_diff_v, c, c_ctx, w_mod, b_mod, ln_g, ln_b, w_in_ab, w_out_ab, ret_log_gamma, ret_gn_g, ret_gn_b, win_sink, w_in_cd, w_out_cd, diff_lambda, diff_subln_g, w_gate, w_up, w_down):
    cond = jnp.concatenate([c_ctx[None, :], c, jnp.zeros((8 - 1 - DEC_BATCH, D_MODEL), f32)], 0)
    mod = _modulation(cond, w_mod, b_mod).reshape(DEPTH, 8, 6, D_MODEL)

    rope_tabs = _rope_tables()
    gmat = jnp.asarray(_block_diag(np.full((HEAD_DIM, HEAD_DIM), 1.0 / HEAD_DIM, np.float32),
                                   N_CHUNK // HEAD_DIM), bf16)
    c64, s64 = _dft_tables(FNET_DIM)
    bdc = _block_diag(c64, FNET_GROUPS)
    bds = _block_diag(s64, FNET_GROUPS)
    dft_ctx = _dft_tables(SEQ)
    dft_lat = _dft_tables(DEC_SEQ)

    x_parts = [x_prompt.reshape(N_CTX, D_MODEL), x_sample.reshape(N_LAT, D_MODEL)]
    outs = {}
    for l in range(DEPTH):
        i = l // 2
        if l % 2 == 0:
            lgf = jnp.repeat(ret_log_gamma[i, 0], HEAD_DIM)[None, :]
            lgb = jnp.repeat(ret_log_gamma[i, 1], HEAD_DIM)[None, :]
            gn_g = ret_gn_g[i][None, :]
            gn_b = ret_gn_b[i][None, :]
            sink = win_sink[i][None, :]
            rope_tiles = tuple(range(0, 2 * RET_W // LANES)) + tuple(
                range(4 * RET_W // LANES, (4 * RET_W + WIN_W + KV_W) // LANES))
            kv_tile = (4 * RET_W + WIN_W) // LANES
            kv_shape = (BATCH, 1, KV_WIN, HEAD_DIM, SEQ)
            scale_tiles = tuple(range(4 * RET_W // LANES, (4 * RET_W + WIN_W) // LANES))
            proj, wk_t, wv_t = _proj(x_parts, mod[l], w_in_ab, i, scale_tiles, rope_tabs, rope_tiles,
                                     (kv_shape, kv_shape),
                                     {kv_tile: ("heads", 0, 0), kv_tile + 1: ("heads", 1, 0)})
            ro_c, wo_c, st_c = _ctx_ab(proj, ret_log_gamma[i], sink, lgf, lgb, gmat, gn_g, gn_b)
            s0 = state_ret[:, i]
            s0 = s0.reshape(DEC_BATCH, 2, H_RET // 2, 2, HEAD_DIM, HEAD_DIM)
            eye2 = jnp.eye(2, dtype=f32)
            s0_pairs = jnp.einsum('bdpeij,ef->bdpeifj', s0, eye2).reshape(
                DEC_BATCH, 2, H_RET // 2, PAIR_W, PAIR_W)
            ck = cache_win_k[:, i].reshape(DEC_BATCH, PAST_LEN, KV_W)
            cv = cache_win_v[:, i].reshape(DEC_BATCH, PAST_LEN, KV_W)
            ro_l, wo_l = _lat_ab(proj, ret_log_gamma[i], sink, ck, cv, s0_pairs, lgf, lgb, gmat, gn_g, gn_b)
            mix_a, mix_b, w_out = (ro_c, ro_l), (wo_c, wo_l), w_out_ab
            outs.setdefault('state', []).append(st_c[:, None])
            outs.setdefault('win_k', []).append(jnp.transpose(wk_t, (0, 1, 4, 2, 3)))
            outs.setdefault('win_v', []).append(jnp.transpose(wv_t, (0, 1, 4, 2, 3)))
        else:
            lam_init = 0.8 - 0.6 * math.exp(-0.3 * l)
            subln = diff_subln_g[i][None, :]
            rope_tiles = tuple(range(0, 2 * DIFF_W // LANES))
            plan = {}
            for h in range(H_DIFF):
                plan[DIFF_W // LANES + h] = ("pairs", 0, h)
                plan[2 * DIFF_W // LANES + h] = ("plain", 1, h)
            scale_tiles = tuple(range(0, DIFF_W // LANES))
            proj, dk_t, dv_h = _proj(
                x_parts, mod[l], w_in_cd, i, scale_tiles, rope_tabs, rope_tiles,
                ((BATCH, 1, H_DIFF, 2, HEAD_DIM, SEQ), (BATCH, 1, H_DIFF, SEQ, 2 * HEAD_DIM)), plan)
            a_c, z_c = _ctx_cd(proj, diff_lambda[i], subln, dft_ctx[0], dft_ctx[1], bdc, bds, lam_init)
            ck = cache_diff_k[:, i].reshape(DEC_BATCH, PAST_LEN, DIFF_W)
            cv = cache_diff_v[:, i].reshape(DEC_BATCH, PAST_LEN, DIFF_W)
            a_l, z_l = _lat_cd(proj, ck, cv, diff_lambda[i], subln, dft_lat[0], dft_lat[1], bdc, bds, lam_init)
            mix_a, mix_b, w_out = (a_c, a_l), (z_c, z_l), w_out_cd
            outs.setdefault('diff_k', []).append(jnp.transpose(dk_t, (0, 1, 5, 2, 3, 4)))
            outs.setdefault('diff_v', []).append(jnp.transpose(dv_h, (0, 1, 3, 2, 4)))
        x_parts = _post(x_parts, mix_a, mix_b, mod[l], ln_g, ln_b, w_out, w_gate, w_up, w_down, l, i,
                        split_out=(l == DEPTH - 1))

    y_prompt = x_parts[0].reshape(BATCH, SEQ, D_MODEL)
    y_sample = x_parts[1].reshape(DEC_BATCH, DEC_SEQ, D_MODEL)
    cat = lambda parts: parts[0] if len(parts) == 1 else jnp.concatenate(parts, 1)
    return (y_prompt, y_sample, cat(outs['state']), cat(outs['win_k']), cat(outs['win_v']),
            cat(outs['diff_k']), cat(outs['diff_v']))
```

```python
import functools
import math

import jax
import jax.numpy as jnp
import numpy as np
from jax import lax
from jax.experimental import pallas as pl
from jax.experimental.pallas import tpu as pltpu

D_MODEL = 1024
BATCH = 32
SEQ = 256
DEPTH = 2
DEC_BATCH = 2
DEC_SEQ = 1024
PAST_LEN = 512
GRID_W = 64
HEAD_DIM = 64
ROPE_BASE = 10000.0
H_RET = 8
H_WIN = 8
KV_WIN = 2
G_WIN = H_WIN // KV_WIN
WINDOW = 128
H_DIFF = 6
FNET_GROUPS = 4
FNET_DIM = 64
D_FF = 256 * math.ceil(8 * D_MODEL / 3 / 256)
RET_W = H_RET * HEAD_DIM
WIN_W = H_WIN * HEAD_DIM
KV_W = KV_WIN * HEAD_DIM
AB_IN = 4 * RET_W + WIN_W + 2 * KV_W
DIFF_W = H_DIFF * 2 * HEAD_DIM
FNET_W = FNET_GROUPS * FNET_DIM
CD_IN = 3 * DIFF_W + FNET_W
ALPHA = (2 * DEPTH) ** 0.25
LN_EPS = 1e-5
QK_SCALE = HEAD_DIM ** -0.5
LOG2_E = math.log2(math.e)

N_CTX = BATCH * SEQ
N_LAT = DEC_BATCH * DEC_SEQ
N_TOK = N_CTX + N_LAT

LANES = 128
PAIR_W = 2 * HEAD_DIM
TM = 512
CTX_BLOCKS = N_CTX // TM
TOK_BLOCKS = N_TOK // TM
ROW_GROUPS = 2
TQ = 256
CTX_SEQS = 2
CTX_SEQS_CD = 4
RET_GROUP = 4
WIN_GROUP = 4
WIN_GROUP_LAT = 4
DIFF_GROUP_LAT = 1
DIFF_GROUP = 3
N_CHUNK = 256
NEG_BIG = -1e30
VMEM_LIMIT = 56 * 1024 * 1024

f32 = jnp.float32
bf16 = jnp.bfloat16


def _params(n_axes):
    return pltpu.CompilerParams(dimension_semantics=("arbitrary",) * n_axes,
                                vmem_limit_bytes=VMEM_LIMIT)


def _dot(a, b):
    return jnp.dot(a, b, preferred_element_type=f32)


def _dot_nt(a, b):
    return lax.dot_general(a, b, (((1,), (1,)), ((), ())), preferred_element_type=f32)


def _ln(x):
    mu = jnp.mean(x, -1, keepdims=True)
    d = x - mu
    var = jnp.mean(d * d, -1, keepdims=True)
    return d * lax.rsqrt(var + LN_EPS)


def _silu(x):
    return x * jax.nn.sigmoid(x)


def _split_bf16(x):
    hi = x.astype(bf16)
    lo = (x - hi.astype(f32)).astype(bf16)
    return hi, lo


def _lane_half_mask(shape):
    return (lax.broadcasted_iota(jnp.int32, shape, len(shape) - 1) & HEAD_DIM) == 0


def _mod_kernel(c_ref, w_ref, b_ref, o_ref):
    a = _silu(c_ref[...])
    a_hi, a_lo = _split_bf16(a)
    w_hi, w_lo = _split_bf16(w_ref[0])
    acc = _dot(a_hi, w_hi) + _dot(a_lo, w_hi) + _dot(a_hi, w_lo)
    o_ref[0] = acc + b_ref[0]


def _modulation(cond, w_mod, b_mod):
    tn = 1536
    rows = cond.shape[0]
    return pl.pallas_call(
        _mod_kernel,
        grid=(DEPTH, 6 * D_MODEL // tn),
        in_specs=[pl.BlockSpec((rows, D_MODEL), lambda l, j: (0, 0)),
                  pl.BlockSpec((1, D_MODEL, tn), lambda l, j: (l, 0, j)),
                  pl.BlockSpec((1, 1, tn), lambda l, j: (l, 0, j))],
        out_specs=pl.BlockSpec((1, rows, tn), lambda l, j: (l, 0, j)),
        out_shape=jax.ShapeDtypeStruct((DEPTH, rows, 6 * D_MODEL), f32),
        compiler_params=_params(2),
        name="modulation",
    )(cond, w_mod, b_mod.reshape(DEPTH, 1, 6 * D_MODEL))


def _tok(i, n_w):
    return jnp.maximum(i - n_w, 0)


def _ctx_blk(t):
    return jnp.minimum(t, CTX_BLOCKS - 1)


def _lat_blk(t):
    return jnp.maximum(t - CTX_BLOCKS, 0)


def _mod_row(t):
    return jnp.where(t < CTX_BLOCKS, 0, 1 + _lat_blk(t) * TM // DEC_SEQ)


def _token_specs(parts, n_w):
    width = parts[0].shape[1]
    if len(parts) == 1:
        return [pl.BlockSpec((TM, width), lambda i: (_tok(i, n_w), 0))]
    return [pl.BlockSpec((TM, width), lambda i: (_ctx_blk(_tok(i, n_w)), 0)),
            pl.BlockSpec((TM, width), lambda i: (_lat_blk(_tok(i, n_w)), 0))]


def _pick(refs, is_ctx, rs):
    if len(refs) == 1:
        return refs[0][rs, :]
    return jnp.where(is_ctx, refs[0][rs, :], refs[1][rs, :])


def _rope_pair(y, cos, sin_signed):
    first_half = (lax.broadcasted_iota(jnp.int32, y.shape, 1) & (HEAD_DIM // 2)) == 0
    swapped = jnp.where(first_half, pltpu.roll(y, LANES - HEAD_DIM // 2, 1), pltpu.roll(y, HEAD_DIM // 2, 1))
    return y * cos + swapped * sin_signed


def _proj_kernel(*refs, n_x, n_cache, n_w, rope_tiles, scale_tiles, cache_plan):
    x_refs = refs[:n_x]
    mod_ref, w_ref, cos_ref, sin_ref, o_ref = refs[n_x:n_x + 5]
    cache_refs = refs[n_x + 5:n_x + 5 + n_cache]
    wbf_ref, u_ref = refs[n_x + 5 + n_cache:]
    i = pl.program_id(0)

    @pl.when(i < n_w)
    def _():
        wbf_ref[i] = w_ref[0].astype(bf16)

    def tokens(is_ctx):
        x_ref = x_refs[0] if is_ctx else x_refs[-1]
        shift = mod_ref[0, 0:1, :]
        scale = mod_ref[0, 1:2, :]
        groups = [slice(b * SEQ, (b + 1) * SEQ) for b in range(TM // SEQ)]
        for rs in groups:
            u_ref[rs, :] = (_ln(x_ref[rs, :]) * (1.0 + scale) + shift).astype(bf16)
        for c in range(n_w):
            for b, rs in enumerate(groups):
                y = _dot(u_ref[rs, :], wbf_ref[c])
                for t in range(N_CHUNK // LANES):
                    tile = c * (N_CHUNK // LANES) + t
                    piece = y[:, t * LANES:(t + 1) * LANES]
                    if tile in rope_tiles and not is_ctx:
                        piece = _rope_pair(piece, cos_ref[rs, :], sin_ref[rs, :])
                    if tile in scale_tiles:
                        piece = piece * (QK_SCALE * LOG2_E)
                    o_ref[rs, tile * LANES:(tile + 1) * LANES] = piece.astype(o_ref.dtype)
                    if tile in cache_plan and is_ctx:
                        kind, out_idx, slot = cache_plan[tile]
                        c_ref = cache_refs[out_idx]
                        if kind == "plain":
                            c_ref[b, 0, slot] = piece
                        else:
                            piece_t = piece.T
                            if kind == "heads":
                                c_ref[b, 0, 0] = piece_t[0:HEAD_DIM]
                                c_ref[b, 0, 1] = piece_t[HEAD_DIM:]
                            else:
                                c_ref[b, 0, slot, 0] = piece_t[0:HEAD_DIM]
                                c_ref[b, 0, slot, 1] = piece_t[HEAD_DIM:]

    t = i - n_w

    @pl.when(jnp.logical_and(t >= 0, t < CTX_BLOCKS))
    def _():
        tokens(True)

    @pl.when(t >= CTX_BLOCKS)
    def _():
        tokens(False)


def _proj(x_parts, mod, w_all, layer, scale_tiles, rope_tabs, rope_tiles, cache_shapes, cache_plan):
    n_out = w_all.shape[2]
    n_w = n_out // N_CHUNK
    nb = DEC_SEQ // TM
    tok = lambda i: _tok(i, n_w)
    in_specs = _token_specs(x_parts, n_w) + [
        pl.BlockSpec((1, 6, D_MODEL), lambda i: (_mod_row(tok(i)), 0, 0)),
        pl.BlockSpec((1, D_MODEL, N_CHUNK), lambda i: (layer, 0, jnp.minimum(i, n_w - 1))),
        pl.BlockSpec((TM, LANES), lambda i: (_lat_blk(tok(i)) % nb, 0)),
        pl.BlockSpec((TM, LANES), lambda i: (_lat_blk(tok(i)) % nb, 0))]
    out_specs = [pl.BlockSpec((TM, n_out), lambda i: (tok(i), 0))]
    out_shape = [jax.ShapeDtypeStruct((N_TOK, n_out), bf16)]
    for shp in cache_shapes:
        blk = (TM // SEQ,) + tuple(shp[1:])
        out_specs.append(pl.BlockSpec(blk, lambda i, nd=len(shp): (_ctx_blk(tok(i)),) + (0,) * (nd - 1)))
        out_shape.append(jax.ShapeDtypeStruct(tuple(shp), f32))
    return pl.pallas_call(
        functools.partial(_proj_kernel, n_x=len(x_parts), n_cache=len(cache_shapes), n_w=n_w,
                          rope_tiles=frozenset(rope_tiles), scale_tiles=frozenset(scale_tiles),
                          cache_plan=dict(cache_plan)),
        grid=(n_w + TOK_BLOCKS,),
        in_specs=in_specs,
        out_specs=out_specs,
        out_shape=out_shape,
        scratch_shapes=[pltpu.VMEM((n_w, D_MODEL, N_CHUNK), bf16), pltpu.VMEM((TM, D_MODEL), bf16)],
        compiler_params=_params(1),
        name="proj",
    )(*x_parts, mod, w_all, *rope_tabs)


def _post_kernel(*refs, n_x, n_y, ka, kb, n_w):
    x_refs = refs[:n_x]
    (ac_ref, al_ref, bc_ref, bl_ref, mod_ref, lng_ref, lnb_ref,
     wo_ref, wg_ref, wu_ref, wd_ref) = refs[n_x:n_x + 11]
    y_refs = refs[n_x + 11:n_x + 11 + n_y]
    wo_s, wg_s, wu_s, wd_s, x1_ref, u_ref, h_ref, y_ref = refs[n_x + 11 + n_y:]
    n_wo = (ka + kb) // N_CHUNK
    i = pl.program_id(0)

    @pl.when(i < n_w)
    def _():
        wg_s[i] = wg_ref[0].astype(bf16)
        wu_s[i] = wu_ref[0].astype(bf16)
        wd_s[i] = wd_ref[0].astype(bf16)

    @pl.when(i < n_wo)
    def _():
        wo_s[i] = wo_ref[0].astype(bf16)

    @pl.when(i >= n_w)
    def _():
        is_ctx = (i - n_w) < CTX_BLOCKS
        gate1 = mod_ref[0, 2:3, :]
        shift2 = mod_ref[0, 3:4, :]
        scale2 = mod_ref[0, 4:5, :]
        gate2 = mod_ref[0, 5:6, :]
        groups = [slice(r * TM // ROW_GROUPS, (r + 1) * TM // ROW_GROUPS) for r in range(ROW_GROUPS)]
        for rs in groups:
            a = _pick((ac_ref, al_ref), is_ctx, rs)
            b = _pick((bc_ref, bl_ref), is_ctx, rs)
            pieces = ([a[:, c:c + N_CHUNK] for c in range(0, ka, N_CHUNK)]
                      + [b[:, c:c + N_CHUNK] for c in range(0, kb, N_CHUNK)])
            h = functools.reduce(lambda s, p: s + p, [_dot(p, wo_s[c]) for c, p in enumerate(pieces)])
            x1 = _ln(ALPHA * _pick(x_refs, is_ctx, rs) + gate1 * h) * lng_ref[0, 0:1, :] + lnb_ref[0, 0:1, :]
            x1_ref[rs, :] = x1
            u_ref[rs, :] = (_ln(x1) * (1.0 + scale2) + shift2).astype(bf16)
        for c in range(n_w):
            cols = slice(c * N_CHUNK, (c + 1) * N_CHUNK)
            for rs in groups:
                g = _dot(u_ref[rs, :], wg_s[c])
                up = _dot(u_ref[rs, :], wu_s[c])
                h_ref[rs, cols] = (_silu(g) * up).astype(bf16)
        for rs in groups:
            ffn = functools.reduce(lambda s, p: s + p,
                                   [_dot(h_ref[rs, c * N_CHUNK:(c + 1) * N_CHUNK], wd_s[c]) for c in range(n_w)])
            y_ref[rs, :] = _ln(ALPHA * x1_ref[rs, :] + gate2 * ffn) * lng_ref[0, 1:2, :] + lnb_ref[0, 1:2, :]
        if n_y == 1:
            y_refs[0][...] = y_ref[...]
        else:
            @pl.when(is_ctx)
            def _():
                y_refs[0][...] = y_ref[...]

            @pl.when(jnp.logical_not(is_ctx))
            def _():
                y_refs[1][...] = y_ref[...]


def _post(x_parts, mix_a, mix_b, mod, ln_g, ln_b, w_out, w_gate, w_up, w_down, layer, mix_layer, split_out):
    ka, kb = mix_a[0].shape[1], mix_b[0].shape[1]
    n_w = D_FF // N_CHUNK
    n_wo = (ka + kb) // N_CHUNK
    tok = lambda i: _tok(i, n_w)
    lay = lambda i: (layer, 0, 0)
    in_specs = (_token_specs(x_parts, n_w) + _token_specs(mix_a, n_w) + _token_specs(mix_b, n_w) + [
        pl.BlockSpec((1, 6, D_MODEL), lambda i: (_mod_row(tok(i)), 0, 0)),
        pl.BlockSpec((1, 2, D_MODEL), lay),
        pl.BlockSpec((1, 2, D_MODEL), lay),
        pl.BlockSpec((1, N_CHUNK, D_MODEL), lambda i: (mix_layer, jnp.minimum(i, n_wo - 1), 0)),
        pl.BlockSpec((1, D_MODEL, N_CHUNK), lambda i: (layer, 0, jnp.minimum(i, n_w - 1))),
        pl.BlockSpec((1, D_MODEL, N_CHUNK), lambda i: (layer, 0, jnp.minimum(i, n_w - 1))),
        pl.BlockSpec((1, N_CHUNK, D_MODEL), lambda i: (layer, jnp.minimum(i, n_w - 1), 0))])
    if split_out:
        out_specs = [pl.BlockSpec((TM, D_MODEL), lambda i: (_ctx_blk(tok(i)), 0)),
                     pl.BlockSpec((TM, D_MODEL), lambda i: (_lat_blk(tok(i)), 0))]
        out_shape = [jax.ShapeDtypeStruct((N_CTX, D_MODEL), f32), jax.ShapeDtypeStruct((N_LAT, D_MODEL), f32)]
    else:
        out_specs = [pl.BlockSpec((TM, D_MODEL), lambda i: (tok(i), 0))]
        out_shape = [jax.ShapeDtypeStruct((N_TOK, D_MODEL), f32)]
    return pl.pallas_call(
        functools.partial(_post_kernel, n_x=len(x_parts), n_y=len(out_shape), ka=ka, kb=kb, n_w=n_w),
        grid=(n_w + TOK_BLOCKS,),
        in_specs=in_specs,
        out_specs=out_specs,
        out_shape=out_shape,
        scratch_shapes=[pltpu.VMEM((n_wo, N_CHUNK, D_MODEL), bf16), pltpu.VMEM((n_w, D_MODEL, N_CHUNK), bf16),
                        pltpu.VMEM((n_w, D_MODEL, N_CHUNK), bf16), pltpu.VMEM((n_w, N_CHUNK, D_MODEL), bf16),
                        pltpu.VMEM((TM, D_MODEL), f32), pltpu.VMEM((TM, D_MODEL), bf16),
                        pltpu.VMEM((TM, D_FF), bf16), pltpu.VMEM((TM, D_MODEL), f32)],
        compiler_params=_params(1),
        name="post",
    )(*x_parts, *mix_a, *mix_b, mod, ln_g, ln_b, w_out, w_gate, w_up, w_down)


def _group_norm_gate(ro, rg, gmat, gn_g, gn_b):
    def gmean(parts):
        cols = []
        for c in range(0, RET_W, N_CHUNK):
            cols.append(sum(_dot(p[:, c:c + N_CHUNK], gmat) for p in parts))
        return jnp.concatenate(cols, -1)

    d = ro - gmean(_split_bf16(ro))
    var = gmean([(d * d).astype(bf16)])
    y = d * lax.rsqrt(var + LN_EPS) * gn_g + gn_b
    return _silu(rg.astype(f32)) * y


def _dup_head(x, j):
    first = _lane_half_mask(x.shape)
    keep = first if j == 0 else jnp.logical_not(first)
    xm = jnp.where(keep, x.astype(f32), 0.0)
    return xm + pltpu.roll(xm, HEAD_DIM, 1)


def _softmax_parts(scores, sink):
    m = sink
    for s in scores:
        m = jnp.maximum(m, jnp.max(s, -1, keepdims=True))
    es = [jnp.exp2(s - m) for s in scores]
    denom = jnp.exp2(sink - m)
    for e in es:
        denom = denom + jnp.sum(e, -1, keepdims=True)
    return es, denom


def _retention_tables(lg_ref, lgf_ref, lgb_ref, dmask_ref, kdec_ref, n):
    row = lax.broadcasted_iota(jnp.int32, (n, n), 0)
    col = lax.broadcasted_iota(jnp.int32, (n, n), 1)
    diff = (row - col).astype(f32)
    diag = jnp.where(row == col, 2.0 * QK_SCALE, QK_SCALE)
    for h in range(H_RET):
        dmask_ref[h] = jnp.exp(jnp.where(diff >= 0, lg_ref[0, h] * diff, -lg_ref[1, h] * diff)) * diag
    t = lax.broadcasted_iota(jnp.int32, (n, RET_W), 0).astype(f32)
    kdec_ref[0] = jnp.exp(lgf_ref[...] * (n - 1.0 - t)) * QK_SCALE
    kdec_ref[1] = jnp.exp(lgb_ref[...] * t) * QK_SCALE


def _retention_intra(pairs, q_of, k_of, v_of, dmask_ref):
    first = _lane_half_mask(k_of(pairs[0]).shape)
    masked = {}
    for p in pairs:
        kb = k_of(p)
        for e in range(2):
            keep = first if e == 0 else jnp.logical_not(first)
            s = _dot_nt(q_of(p), jnp.where(keep, kb, jnp.zeros_like(kb))) * dmask_ref[2 * p + e]
            masked[p, e] = s.astype(bf16)
    outs = {}
    for p in pairs:
        pv = [_dot(masked[p, e], v_of(p)) for e in range(2)]
        outs[p] = jnp.where(_lane_half_mask(pv[0].shape), pv[0], pv[1])
    return outs


def _window_group(subs, q_of, k_parts_of, v_parts_of, masks, sink_of):
    scores = {}
    for key in subs:
        parts = [_dot_nt(q_of(key), k) for k in k_parts_of(key)]
        scores[key] = [sc if mk is None else jnp.where(mk, sc, NEG_BIG) for sc, mk in zip(parts, masks)]
    probs = {}
    for key in subs:
        es, denom = _softmax_parts(scores[key], sink_of(key))
        probs[key] = ([ex.astype(bf16) for ex in es], denom)
    outs = {}
    for key in subs:
        es, denom = probs[key]
        pv = functools.reduce(lambda x, y: x + y, [_dot(ex, v) for ex, v in zip(es, v_parts_of(key))])
        outs[key] = pv / denom
    return outs


def _ctx_ab_kernel(lg_ref, sink_ref, rq_ref, rk_ref, rv_ref, rg_ref, wq_ref, wk_ref, wv_ref,
                   lgf_ref, lgb_ref, gmat_ref, gng_ref, gnb_ref,
                   ro_ref, wo_ref, st_ref, dmask_ref, kdec_ref, ret_ref):
    t_len = SEQ

    @pl.when(pl.program_id(0) == 0)
    def _():
        _retention_tables(lg_ref, lgf_ref, lgb_ref, dmask_ref, kdec_ref, t_len)

    first = _lane_half_mask((t_len, PAIR_W))
    for sq in range(CTX_SEQS):
        rows = slice(sq * t_len, (sq + 1) * t_len)
        psl = lambda p: slice(p * PAIR_W, (p + 1) * PAIR_W)
        for p0 in range(0, H_RET // 2, RET_GROUP):
            pairs = list(range(p0, p0 + RET_GROUP))
            intra = _retention_intra(pairs, lambda p: rq_ref[rows, psl(p)], lambda p: rk_ref[rows, psl(p)],
                                     lambda p: rv_ref[rows, psl(p)], dmask_ref)
            for p in pairs:
                ret_ref[rows, psl(p)] = intra[p]
        for p in range(H_RET // 2):
            sl = psl(p)
            kb = rk_ref[rows, sl]
            v = rv_ref[rows, sl]
            for d in range(2):
                kd_t = (kb * kdec_ref[d, :, sl]).T.astype(bf16)
                st = _dot(kd_t, v)
                st_ref[sq, d, 2 * p] = st[0:HEAD_DIM, 0:HEAD_DIM]
                st_ref[sq, d, 2 * p + 1] = pltpu.roll(st[HEAD_DIM:, :], HEAD_DIM, 1)[:, 0:HEAD_DIM]
        ro_ref[rows, :] = _group_norm_gate(ret_ref[rows, :], rg_ref[rows, :], gmat_ref[...], gng_ref[...],
                                           gnb_ref[...]).astype(bf16)

        k_dup = [_dup_head(wk_ref[rows, :], j).astype(bf16) for j in range(KV_WIN)]
        v_dup = [_dup_head(wv_ref[rows, :], j).astype(bf16) for j in range(KV_WIN)]

        def q_masked(key):
            qp, e = key
            qb = wq_ref[rows, qp * PAIR_W:(qp + 1) * PAIR_W]
            return jnp.where(first if e == 0 else jnp.logical_not(first), qb, jnp.zeros_like(qb))

        kv_of = lambda key: key[0] * 2 // G_WIN
        for g0 in range(0, H_WIN // 2, WIN_GROUP):
            subs = [(qp, e) for qp in range(g0, g0 + WIN_GROUP) for e in range(2)]
            outs = _window_group(subs, q_masked, lambda key: [k_dup[kv_of(key)]], lambda key: [v_dup[kv_of(key)]],
                                 [None], lambda key: sink_ref[0, 2 * key[0] + key[1]] * LOG2_E)
            for qp in range(g0, g0 + WIN_GROUP):
                wo_ref[rows, qp * PAIR_W:(qp + 1) * PAIR_W] = jnp.where(first, outs[qp, 0], outs[qp, 1]).astype(bf16)


def _ctx_ab(proj, log_gamma, sink, lgf_lanes, lgb_lanes, gmat, gn_g, gn_b):
    t = SEQ
    tb = CTX_SEQS * t
    smem = pl.BlockSpec(memory_space=pltpu.SMEM)
    const = lambda b: (0, 0)
    col = lambda c: (lambda b: (b, c))
    return pl.pallas_call(
        _ctx_ab_kernel,
        grid=(BATCH // CTX_SEQS,),
        in_specs=[smem, smem,
                  pl.BlockSpec((tb, RET_W), col(0)), pl.BlockSpec((tb, RET_W), col(1)),
                  pl.BlockSpec((tb, RET_W), col(2)), pl.BlockSpec((tb, RET_W), col(3)),
                  pl.BlockSpec((tb, WIN_W), col(4)),
                  pl.BlockSpec((tb, KV_W), col((4 * RET_W + WIN_W) // KV_W)),
                  pl.BlockSpec((tb, KV_W), col((4 * RET_W + WIN_W) // KV_W + 1)),
                  pl.BlockSpec((1, RET_W), const), pl.BlockSpec((1, RET_W), const),
                  pl.BlockSpec((N_CHUNK, N_CHUNK), const),
                  pl.BlockSpec((1, RET_W), const), pl.BlockSpec((1, RET_W), const)],
        out_specs=[pl.BlockSpec((tb, RET_W), lambda b: (b, 0)),
                   pl.BlockSpec((tb, WIN_W), lambda b: (b, 0)),
                   pl.BlockSpec((CTX_SEQS, 2, H_RET, HEAD_DIM, HEAD_DIM), lambda b: (b, 0, 0, 0, 0))],
        out_shape=[jax.ShapeDtypeStruct((BATCH * t, RET_W), bf16),
                   jax.ShapeDtypeStruct((BATCH * t, WIN_W), bf16),
                   jax.ShapeDtypeStruct((BATCH, 2, H_RET, HEAD_DIM, HEAD_DIM), f32)],
        scratch_shapes=[pltpu.VMEM((H_RET, t, t), f32), pltpu.VMEM((2, t, RET_W), f32),
                        pltpu.VMEM((tb, RET_W), f32)],
        compiler_params=_params(1),
        name="ctx_ab",
    )(log_gamma, sink, proj, proj, proj, proj, proj, proj, proj, lgf_lanes, lgb_lanes, gmat, gn_g, gn_b)


def _lat_ab_kernel(lg_ref, sink_ref, rq_ref, rk_ref, rv_ref, rg_ref, wq_ref, wk_ref, wv_ref, ck_ref, cv_ref,
                   s0_ref, lgf_ref, lgb_ref, gmat_ref, gng_ref, gnb_ref,
                   ro_ref, wo_ref, ret_ref, dmask_ref, kdec_ref, qdec_ref, sf_ref, sb_ref):
    t_len = DEC_SEQ
    n_chunks = t_len // TQ
    chunk = pl.program_id(1)
    q0 = pl.multiple_of(chunk * TQ, TQ)
    first = _lane_half_mask((TQ, PAIR_W))

    @pl.when(jnp.logical_and(pl.program_id(0) == 0, chunk == 0))
    def _():
        _retention_tables(lg_ref, lgf_ref, lgb_ref, dmask_ref, kdec_ref, TQ)
        t = lax.broadcasted_iota(jnp.int32, (TQ, RET_W), 0).astype(f32)
        qdec_ref[0] = jnp.exp(lgf_ref[...] * (t + 1.0))
        qdec_ref[1] = jnp.exp(lgb_ref[...] * (TQ - t))

    @pl.when(chunk == 0)
    def _():
        r = lax.broadcasted_iota(jnp.int32, (PAIR_W, PAIR_W), 0)
        c_ = lax.broadcasted_iota(jnp.int32, (PAIR_W, PAIR_W), 1)
        same_head = (r < HEAD_DIM) == (c_ < HEAD_DIM)
        for p in range(H_RET // 2):
            sl = slice(p * PAIR_W, (p + 1) * PAIR_W)
            kv = []
            for c in range(n_chunks):
                rows = slice(c * TQ, (c + 1) * TQ)
                kc = rk_ref[rows, sl]
                vc = rv_ref[rows, sl]
                kv.append([jnp.where(same_head, _dot((kc * kdec_ref[d, :, sl]).T.astype(bf16), vc), 0.0)
                           for d in range(2)])
            state = s0_ref[0, 0, p]
            for c in range(n_chunks):
                sf_ref[c, p] = state
                state = state * jnp.exp(lgf_ref[:, sl] * TQ) + kv[c][0]
            state = s0_ref[0, 1, p]
            for c in reversed(range(n_chunks)):
                sb_ref[c, p] = state
                state = state * jnp.exp(lgb_ref[:, sl] * TQ) + kv[c][1]

    psl = lambda p: slice(p * PAIR_W, (p + 1) * PAIR_W)
    intra = {}
    for p0 in range(0, H_RET // 2, RET_GROUP):
        intra.update(_retention_intra(list(range(p0, p0 + RET_GROUP)), lambda p: rq_ref[:, psl(p)],
                                      lambda p: rk_ref[pl.ds(q0, TQ), psl(p)],
                                      lambda p: rv_ref[pl.ds(q0, TQ), psl(p)], dmask_ref))
    for p in range(H_RET // 2):
        sl = psl(p)
        q = rq_ref[:, sl]
        o = intra[p]
        o = o + _dot(q, sf_ref[chunk, p].astype(bf16)) * qdec_ref[0, :, sl]
        o = o + _dot(q, sb_ref[chunk, p].astype(bf16)) * qdec_ref[1, :, sl]
        ret_ref[:, sl] = o
    ro_ref[...] = _group_norm_gate(ret_ref[...], rg_ref[...], gmat_ref[...], gng_ref[...], gnb_ref[...]).astype(bf16)

    band = TQ + 2 * WINDOW
    k_start = pl.multiple_of(jnp.clip(q0 - WINDOW, 0, t_len - band), LANES)
    qi = q0 + lax.broadcasted_iota(jnp.int32, (TQ, band), 0)
    kj = k_start + lax.broadcasted_iota(jnp.int32, (TQ, band), 1)
    in_band = jnp.abs(qi - kj) <= WINDOW
    k_parts = [[_dup_head(wk_ref[pl.ds(k_start, band), :], j).astype(bf16), _dup_head(ck_ref[0], j).astype(bf16)]
               for j in range(KV_WIN)]
    v_parts = [[_dup_head(wv_ref[pl.ds(k_start, band), :], j).astype(bf16), _dup_head(cv_ref[0], j).astype(bf16)]
               for j in range(KV_WIN)]

    def q_masked(key):
        qp, e = key
        qb = wq_ref[:, qp * PAIR_W:(qp + 1) * PAIR_W]
        return jnp.where(first if e == 0 else jnp.logical_not(first), qb, jnp.zeros_like(qb))

    kv_of = lambda key: key[0] * 2 // G_WIN
    for g0 in range(0, H_WIN // 2, WIN_GROUP_LAT):
        subs = [(qp, e) for qp in range(g0, g0 + WIN_GROUP_LAT) for e in range(2)]
        outs = _window_group(subs, q_masked, lambda key: k_parts[kv_of(key)], lambda key: v_parts[kv_of(key)],
                             [in_band, None], lambda key: sink_ref[0, 2 * key[0] + key[1]] * LOG2_E)
        for qp in range(g0, g0 + WIN_GROUP_LAT):
            wo_ref[:, qp * PAIR_W:(qp + 1) * PAIR_W] = jnp.where(first, outs[qp, 0], outs[qp, 1]).astype(bf16)


def _lat_ab(proj, log_gamma, sink, ck, cv, s0_pairs, lgf_lanes, lgb_lanes, gmat, gn_g, gn_b):
    t = DEC_SEQ
    nq = t // TQ
    smem = pl.BlockSpec(memory_space=pltpu.SMEM)
    const = lambda b, i: (0, 0)
    qcol = lambda c: (lambda b, i: (N_CTX // TQ + b * nq + i, c))
    bcol = lambda c: (lambda b, i: (N_CTX // t + b, c))
    kv_col = (4 * RET_W + WIN_W) // KV_W
    return pl.pallas_call(
        _lat_ab_kernel,
        grid=(DEC_BATCH, nq),
        in_specs=[smem, smem,
                  pl.BlockSpec((TQ, RET_W), qcol(0)), pl.BlockSpec((t, RET_W), bcol(1)),
                  pl.BlockSpec((t, RET_W), bcol(2)), pl.BlockSpec((TQ, RET_W), qcol(3)),
                  pl.BlockSpec((TQ, WIN_W), qcol(4)),
                  pl.BlockSpec((t, KV_W), bcol(kv_col)), pl.BlockSpec((t, KV_W), bcol(kv_col + 1)),
                  pl.BlockSpec((1, PAST_LEN, KV_W), lambda b, i: (b, 0, 0)),
                  pl.BlockSpec((1, PAST_LEN, KV_W), lambda b, i: (b, 0, 0)),
                  pl.BlockSpec((1, 2, H_RET // 2, PAIR_W, PAIR_W), lambda b, i: (b, 0, 0, 0, 0)),
                  pl.BlockSpec((1, RET_W), const), pl.BlockSpec((1, RET_W), const),
                  pl.BlockSpec((N_CHUNK, N_CHUNK), const),
                  pl.BlockSpec((1, RET_W), const), pl.BlockSpec((1, RET_W), const)],
        out_specs=[pl.BlockSpec((TQ, RET_W), lambda b, i: (b * nq + i, 0)),
                   pl.BlockSpec((TQ, WIN_W), lambda b, i: (b * nq + i, 0))],
        out_shape=[jax.ShapeDtypeStruct((DEC_BATCH * t, RET_W), bf16),
                   jax.ShapeDtypeStruct((DEC_BATCH * t, WIN_W), bf16)],
        scratch_shapes=[pltpu.VMEM((TQ, RET_W), f32), pltpu.VMEM((H_RET, TQ, TQ), f32),
                        pltpu.VMEM((2, TQ, RET_W), f32), pltpu.VMEM((2, TQ, RET_W), f32),
                        pltpu.VMEM((t // TQ, H_RET // 2, PAIR_W, PAIR_W), f32),
                        pltpu.VMEM((t // TQ, H_RET // 2, PAIR_W, PAIR_W), f32)],
        compiler_params=_params(2),
        name="lat_ab",
    )(log_gamma, sink, proj, proj, proj, proj, proj, proj, proj, ck, cv, s0_pairs,
      lgf_lanes, lgb_lanes, gmat, gn_g, gn_b)


def _lambda_full(lam_ref, lam_init):
    lam = lam_ref[...]
    a = jnp.sum(lam[0:1, :] * lam[1:2, :], -1, keepdims=True)
    b = jnp.sum(lam[2:3, :] * lam[3:4, :], -1, keepdims=True)
    return jnp.exp(a) - jnp.exp(b) + lam_init


def _diff_heads(q_of, k_parts_of, v_parts_of, lam, subln, lam_init, group):
    res = []
    for h0 in range(0, H_DIFF, group):
        res += _diff_head_group(range(h0, h0 + group), q_of, k_parts_of, v_parts_of, lam, subln, lam_init)
    return res


def _diff_head_group(heads, q_of, k_parts_of, v_parts_of, lam, subln, lam_init):
    subs = [(h, e) for h in heads for e in range(2)]
    scores = {}
    for h, e in subs:
        parts = []
        for k in k_parts_of(h):
            fm = _lane_half_mask(k.shape)
            keep = fm if e == 0 else jnp.logical_not(fm)
            parts.append(_dot_nt(q_of(h), jnp.where(keep, k, jnp.zeros_like(k))))
        scores[h, e] = parts
    probs = {}
    for key in subs:
        m = scores[key][0].max(-1, keepdims=True)
        for sc in scores[key][1:]:
            m = jnp.maximum(m, sc.max(-1, keepdims=True))
        es = [jnp.exp2(sc - m) for sc in scores[key]]
        denom = es[0].sum(-1, keepdims=True)
        for ex in es[1:]:
            denom = denom + ex.sum(-1, keepdims=True)
        probs[key] = ([ex.astype(bf16) for ex in es], denom)
    outs = {}
    for h, e in subs:
        es, denom = probs[h, e]
        pv = functools.reduce(lambda x, y: x + y, [_dot(ex, v) for v, ex in zip(v_parts_of(h), es)])
        outs[h, e] = pv / denom
    res = []
    for h in heads:
        a = outs[h, 0] - lam * outs[h, 1]
        res.append(a * lax.rsqrt(jnp.mean(a * a, -1, keepdims=True) + LN_EPS) * subln * (1.0 - lam_init))
    return res


def _fourier_rows(ct_ref, st_ref, z, bdc_ref, bds_ref):
    zb = z
    zc = _dot(zb, bdc_ref[...].astype(bf16)).astype(bf16)
    zs = _dot(zb, bds_ref[...].astype(bf16)).astype(bf16)
    return _dot(ct_ref[...].astype(bf16), zc) - _dot(st_ref[...].astype(bf16), zs)


def _ctx_cd_kernel(q_ref, k_ref, v_ref, z_ref, lam_ref, subln_ref, ct_ref, st_ref, bdc_ref, bds_ref,
                   a_ref, zf_ref, *, lam_init):
    lam = _lambda_full(lam_ref, lam_init)
    for sq in range(CTX_SEQS_CD):
        rows = slice(sq * SEQ, (sq + 1) * SEQ)
        sl = lambda h: slice(h * PAIR_W, (h + 1) * PAIR_W)
        heads = _diff_heads(lambda h: q_ref[rows, sl(h)], lambda h: [k_ref[rows, sl(h)]],
                            lambda h: [v_ref[rows, sl(h)]], lam, subln_ref[...], lam_init, DIFF_GROUP)
        for h in range(H_DIFF):
            a_ref[rows, sl(h)] = heads[h].astype(bf16)
        zf_ref[rows, :] = _fourier_rows(ct_ref, st_ref, z_ref[rows, :], bdc_ref, bds_ref).astype(bf16)


def _ctx_cd(proj, lam, subln, ct, st, bdc, bds, lam_init):
    t = SEQ
    tb = CTX_SEQS_CD * t
    const = lambda b: (0, 0)
    col = lambda c: (lambda b: (b, c))
    return pl.pallas_call(
        functools.partial(_ctx_cd_kernel, lam_init=lam_init),
        grid=(BATCH // CTX_SEQS_CD,),
        in_specs=[pl.BlockSpec((tb, DIFF_W), col(0)), pl.BlockSpec((tb, DIFF_W), col(1)),
                  pl.BlockSpec((tb, DIFF_W), col(2)), pl.BlockSpec((tb, FNET_W), col(3 * DIFF_W // FNET_W)),
                  pl.BlockSpec((4, HEAD_DIM), const), pl.BlockSpec((1, PAIR_W), const),
                  pl.BlockSpec((t, t), const), pl.BlockSpec((t, t), const),
                  pl.BlockSpec((FNET_W, FNET_W), const), pl.BlockSpec((FNET_W, FNET_W), const)],
        out_specs=[pl.BlockSpec((tb, DIFF_W), lambda b: (b, 0)), pl.BlockSpec((tb, FNET_W), lambda b: (b, 0))],
        out_shape=[jax.ShapeDtypeStruct((BATCH * t, DIFF_W), bf16),
                   jax.ShapeDtypeStruct((BATCH * t, FNET_W), bf16)],
        compiler_params=_params(1),
        name="ctx_cd",
    )(proj, proj, proj, proj, lam, subln, ct, st, bdc, bds)


def _lat_cd_kernel(q_ref, k_ref, v_ref, z_ref, ck_ref, cv_ref, lam_ref, subln_ref, ct_ref, st_ref, bdc_ref, bds_ref,
                   a_ref, zf_ref, *, lam_init):
    lam = _lambda_full(lam_ref, lam_init)
    sl = lambda h: slice(h * PAIR_W, (h + 1) * PAIR_W)
    heads = _diff_heads(lambda h: q_ref[:, sl(h)], lambda h: [k_ref[:, sl(h)], ck_ref[0, :, sl(h)].astype(bf16)],
                        lambda h: [v_ref[:, sl(h)], cv_ref[0, :, sl(h)].astype(bf16)], lam, subln_ref[...], lam_init,
                        DIFF_GROUP_LAT)
    for h in range(H_DIFF):
        a_ref[:, sl(h)] = heads[h].astype(bf16)
    zf_ref[...] = _fourier_rows(ct_ref, st_ref, z_ref[...], bdc_ref, bds_ref).astype(bf16)


def _lat_cd(proj, ck, cv, lam, subln, ct, st, bdc, bds, lam_init):
    t = DEC_SEQ
    nq = t // TQ
    const = lambda b, i: (0, 0)
    return pl.pallas_call(
        functools.partial(_lat_cd_kernel, lam_init=lam_init),
        grid=(DEC_BATCH, nq),
        in_specs=[pl.BlockSpec((TQ, DIFF_W), lambda b, i: (N_CTX // TQ + b * nq + i, 0)),
                  pl.BlockSpec((t, DIFF_W), lambda b, i: (N_CTX // t + b, 1)),
                  pl.BlockSpec((t, DIFF_W), lambda b, i: (N_CTX // t + b, 2)),
                  pl.BlockSpec((t, FNET_W), lambda b, i: (N_CTX // t + b, 3 * DIFF_W // FNET_W)),
                  pl.BlockSpec((1, PAST_LEN, DIFF_W), lambda b, i: (b, 0, 0)),
                  pl.BlockSpec((1, PAST_LEN, DIFF_W), lambda b, i: (b, 0, 0)),
                  pl.BlockSpec((4, HEAD_DIM), const), pl.BlockSpec((1, PAIR_W), const),
                  pl.BlockSpec((TQ, t), lambda b, i: (i, 0)), pl.BlockSpec((TQ, t), lambda b, i: (i, 0)),
                  pl.BlockSpec((FNET_W, FNET_W), const), pl.BlockSpec((FNET_W, FNET_W), const)],
        out_specs=[pl.BlockSpec((TQ, DIFF_W), lambda b, i: (b * nq + i, 0)),
                   pl.BlockSpec((TQ, FNET_W), lambda b, i: (b * nq + i, 0))],
        out_shape=[jax.ShapeDtypeStruct((DEC_BATCH * t, DIFF_W), bf16),
                   jax.ShapeDtypeStruct((DEC_BATCH * t, FNET_W), bf16)],
        compiler_params=_params(2),
        name="lat_cd",
    )(proj, proj, proj, proj, ck, cv, lam, subln, ct, st, bdc, bds)


def _rope_tables():
    t = np.arange(DEC_SEQ)
    quarter = HEAD_DIM // 4
    inv = ROPE_BASE ** (-np.arange(quarter, dtype=np.float64) / quarter)
    ang = np.concatenate([(t // GRID_W)[:, None] * inv, (t % GRID_W)[:, None] * inv], -1)
    cos, sin = np.cos(ang), np.sin(ang)
    reps = LANES // HEAD_DIM
    return (np.tile(np.concatenate([cos, cos], -1), (1, reps)).astype(np.float32),
            np.tile(np.concatenate([-sin, sin], -1), (1, reps)).astype(np.float32))


def _dft_tables(n):
    k = np.arange(n)
    ang = (2.0 * math.pi / n) * ((k[:, None] * k[None, :]) % n)
    return (np.cos(ang) / math.sqrt(n)).astype(np.float32), (np.sin(ang) / math.sqrt(n)).astype(np.float32)


def _block_diag(m, reps):
    return np.kron(np.eye(reps, dtype=m.dtype), m)


def kernel(x_prompt, x_sample, state_ret, cache_win_k, cache_win_v, cache_diff_k, cache_diff_v, c, c_ctx, w_mod, b_mod, ln_g, ln_b, w_in_ab, w_out_ab, ret_log_gamma, ret_gn_g, ret_gn_b, win_sink, w_in_cd, w_out_cd, diff_lambda, diff_subln_g, w_gate, w_up, w_down):
    cond = jnp.concatenate([c_ctx[None, :], c, jnp.zeros((8 - 1 - DEC_BATCH, D_MODEL), f32)], 0)
    mod = _modulation(cond, w_mod, b_mod).reshape(DEPTH, 8, 6, D_MODEL)

    rope_tabs = _rope_tables()
    gmat = jnp.asarray(_block_diag(np.full((HEAD_DIM, HEAD_DIM), 1.0 / HEAD_DIM, np.float32),
                                   N_CHUNK // HEAD_DIM), bf16)
    c64, s64 = _dft_tables(FNET_DIM)
    bdc = _block_diag(c64, FNET_GROUPS)
    bds = _block_diag(s64, FNET_GROUPS)
    dft_ctx = _dft_tables(SEQ)
    dft_lat = _dft_tables(DEC_SEQ)

    x_parts = [x_prompt.reshape(N_CTX, D_MODEL), x_sample.reshape(N_LAT, D_MODEL)]
    outs = {}
    for l in range(DEPTH):
        i = l // 2
        if l % 2 == 0:
            lgf = jnp.repeat(ret_log_gamma[i, 0], HEAD_DIM)[None, :]
            lgb = jnp.repeat(ret_log_gamma[i, 1], HEAD_DIM)[None, :]
            gn_g = ret_gn_g[i][None, :]
            gn_b = ret_gn_b[i][None, :]
            sink = win_sink[i][None, :]
            rope_tiles = tuple(range(0, 2 * RET_W // LANES)) + tuple(
                range(4 * RET_W // LANES, (4 * RET_W + WIN_W + KV_W) // LANES))
            kv_tile = (4 * RET_W + WIN_W) // LANES
            kv_shape = (BATCH, 1, KV_WIN, HEAD_DIM, SEQ)
            scale_tiles = tuple(range(4 * RET_W // LANES, (4 * RET_W + WIN_W) // LANES))
            proj, wk_t, wv_t = _proj(x_parts, mod[l], w_in_ab, i, scale_tiles, rope_tabs, rope_tiles,
                                     (kv_shape, kv_shape),
                                     {kv_tile: ("heads", 0, 0), kv_tile + 1: ("heads", 1, 0)})
            ro_c, wo_c, st_c = _ctx_ab(proj, ret_log_gamma[i], sink, lgf, lgb, gmat, gn_g, gn_b)
            s0 = state_ret[:, i]
            s0 = s0.reshape(DEC_BATCH, 2, H_RET // 2, 2, HEAD_DIM, HEAD_DIM)
            eye2 = jnp.eye(2, dtype=f32)
            s0_pairs = jnp.einsum('bdpeij,ef->bdpeifj', s0, eye2).reshape(
                DEC_BATCH, 2, H_RET // 2, PAIR_W, PAIR_W)
            ck = cache_win_k[:, i].reshape(DEC_BATCH, PAST_LEN, KV_W)
            cv = cache_win_v[:, i].reshape(DEC_BATCH, PAST_LEN, KV_W)
            ro_l, wo_l = _lat_ab(proj, ret_log_gamma[i], sink, ck, cv, s0_pairs, lgf, lgb, gmat, gn_g, gn_b)
            mix_a, mix_b, w_out = (ro_c, ro_l), (wo_c, wo_l), w_out_ab
            outs.setdefault('state', []).append(st_c[:, None])
            outs.setdefault('win_k', []).append(jnp.transpose(wk_t, (0, 1, 4, 2, 3)))
            outs.setdefault('win_v', []).append(jnp.transpose(wv_t, (0, 1, 4, 2, 3)))
        else:
            lam_init = 0.8 - 0.6 * math.exp(-0.3 * l)
            subln = diff_subln_g[i][None, :]
            rope_tiles = tuple(range(0, 2 * DIFF_W // LANES))
            plan = {}
            for h in range(H_DIFF):
                plan[DIFF_W // LANES + h] = ("pairs", 0, h)
                plan[2 * DIFF_W // LANES + h] = ("plain", 1, h)
            scale_tiles = tuple(range(0, DIFF_W // LANES))
            proj, dk_t, dv_h = _proj(
                x_parts, mod[l], w_in_cd, i, scale_tiles, rope_tabs, rope_tiles,
                ((BATCH, 1, H_DIFF, 2, HEAD_DIM, SEQ), (BATCH, 1, H_DIFF, SEQ, 2 * HEAD_DIM)), plan)
            a_c, z_c = _ctx_cd(proj, diff_lambda[i], subln, dft_ctx[0], dft_ctx[1], bdc, bds, lam_init)
            ck = cache_diff_k[:, i].reshape(DEC_BATCH, PAST_LEN, DIFF_W)
            cv = cache_diff_v[:, i].reshape(DEC_BATCH, PAST_LEN, DIFF_W)
            a_l, z_l = _lat_cd(proj, ck, cv, diff_lambda[i], subln, dft_lat[0], dft_lat[1], bdc, bds, lam_init)
            mix_a, mix_b, w_out = (a_c, a_l), (z_c, z_l), w_out_cd
            outs.setdefault('diff_k', []).append(jnp.transpose(dk_t, (0, 1, 5, 2, 3, 4)))
            outs.setdefault('diff_v', []).append(jnp.transpose(dv_h, (0, 1, 3, 2, 4)))
        x_parts = _post(x_parts, mix_a, mix_b, mod[l], ln_g, ln_b, w_out, w_gate, w_up, w_down, l, i,
                        split_out=(l == DEPTH - 1))

    y_prompt = x_parts[0].reshape(BATCH, SEQ, D_MODEL)
    y_sample = x_parts[1].reshape(DEC_BATCH, DEC_SEQ, D_MODEL)
    cat = lambda parts: parts[0] if len(parts) == 1 else jnp.concatenate(parts, 1)
    return (y_prompt, y_sample, cat(outs['state']), cat(outs['win_k']), cat(outs['win_v']),
            cat(outs['diff_k']), cat(outs['diff_v']))
```

```python
import functools
import math

import jax
import jax.numpy as jnp
import numpy as np
from jax import lax
from jax.experimental import pallas as pl
from jax.experimental.pallas import tpu as pltpu

D_MODEL = 1024
BATCH = 32
SEQ = 256
DEPTH = 2
DEC_BATCH = 2
DEC_SEQ = 1024
PAST_LEN = 512
GRID_W = 64
HEAD_DIM = 64
ROPE_BASE = 10000.0
H_RET = 8
H_WIN = 8
KV_WIN = 2
G_WIN = H_WIN // KV_WIN
WINDOW = 128
H_DIFF = 6
FNET_GROUPS = 4
FNET_DIM = 64
D_FF = 256 * math.ceil(8 * D_MODEL / 3 / 256)
RET_W = H_RET * HEAD_DIM
WIN_W = H_WIN * HEAD_DIM
KV_W = KV_WIN * HEAD_DIM
AB_IN = 4 * RET_W + WIN_W + 2 * KV_W
DIFF_W = H_DIFF * 2 * HEAD_DIM
FNET_W = FNET_GROUPS * FNET_DIM
CD_IN = 3 * DIFF_W + FNET_W
ALPHA = (2 * DEPTH) ** 0.25
LN_EPS = 1e-5
QK_SCALE = HEAD_DIM ** -0.5
LOG2_E = math.log2(math.e)

N_CTX = BATCH * SEQ
N_LAT = DEC_BATCH * DEC_SEQ
N_TOK = N_CTX + N_LAT

LANES = 128
PAIR_W = 2 * HEAD_DIM
TM = 512
CTX_BLOCKS = N_CTX // TM
TOK_BLOCKS = N_TOK // TM
ROW_GROUPS = 2
FFN_SKEW = 2
TQ = 256
CTX_SEQS = 2
CTX_SEQS_CD = 4
RET_GROUP = 4
WIN_GROUP = 4
WIN_GROUP_LAT = 4
DIFF_GROUP_LAT = 1
DIFF_GROUP = 3
N_CHUNK = 256
NEG_BIG = -1e30
VMEM_LIMIT = 56 * 1024 * 1024

f32 = jnp.float32
bf16 = jnp.bfloat16


def _params(n_axes):
    return pltpu.CompilerParams(dimension_semantics=("arbitrary",) * n_axes,
                                vmem_limit_bytes=VMEM_LIMIT)


def _dot(a, b):
    return jnp.dot(a, b, preferred_element_type=f32)


def _dot_nt(a, b):
    return lax.dot_general(a, b, (((1,), (1,)), ((), ())), preferred_element_type=f32)


def _ln(x):
    mu = jnp.mean(x, -1, keepdims=True)
    d = x - mu
    var = jnp.mean(d * d, -1, keepdims=True)
    return d * lax.rsqrt(var + LN_EPS)


def _silu(x):
    return x * jax.nn.sigmoid(x)


def _split_bf16(x):
    hi = x.astype(bf16)
    lo = (x - hi.astype(f32)).astype(bf16)
    return hi, lo


def _lane_half_mask(shape):
    return (lax.broadcasted_iota(jnp.int32, shape, len(shape) - 1) & HEAD_DIM) == 0


def _mod_kernel(c_ref, w_ref, b_ref, o_ref):
    a = _silu(c_ref[...])
    a_hi, a_lo = _split_bf16(a)
    w_hi, w_lo = _split_bf16(w_ref[0])
    acc = _dot(a_hi, w_hi) + _dot(a_lo, w_hi) + _dot(a_hi, w_lo)
    o_ref[0] = acc + b_ref[0]


def _modulation(cond, w_mod, b_mod):
    tn = 1536
    rows = cond.shape[0]
    return pl.pallas_call(
        _mod_kernel,
        grid=(DEPTH, 6 * D_MODEL // tn),
        in_specs=[pl.BlockSpec((rows, D_MODEL), lambda l, j: (0, 0)),
                  pl.BlockSpec((1, D_MODEL, tn), lambda l, j: (l, 0, j)),
                  pl.BlockSpec((1, 1, tn), lambda l, j: (l, 0, j))],
        out_specs=pl.BlockSpec((1, rows, tn), lambda l, j: (l, 0, j)),
        out_shape=jax.ShapeDtypeStruct((DEPTH, rows, 6 * D_MODEL), f32),
        compiler_params=_params(2),
        name="modulation",
    )(cond, w_mod, b_mod.reshape(DEPTH, 1, 6 * D_MODEL))


def _tok(i, n_w):
    return jnp.maximum(i - n_w, 0)


def _ctx_blk(t):
    return jnp.minimum(t, CTX_BLOCKS - 1)


def _lat_blk(t):
    return jnp.maximum(t - CTX_BLOCKS, 0)


def _mod_row(t):
    return jnp.where(t < CTX_BLOCKS, 0, 1 + _lat_blk(t) * TM // DEC_SEQ)


def _token_specs(parts, n_w):
    width = parts[0].shape[1]
    if len(parts) == 1:
        return [pl.BlockSpec((TM, width), lambda i: (_tok(i, n_w), 0))]
    return [pl.BlockSpec((TM, width), lambda i: (_ctx_blk(_tok(i, n_w)), 0)),
            pl.BlockSpec((TM, width), lambda i: (_lat_blk(_tok(i, n_w)), 0))]


def _pick(refs, is_ctx, rs):
    if len(refs) == 1:
        return refs[0][rs, :]
    return jnp.where(is_ctx, refs[0][rs, :], refs[1][rs, :])


def _rope_pair(y, cos, sin_signed):
    first_half = (lax.broadcasted_iota(jnp.int32, y.shape, 1) & (HEAD_DIM // 2)) == 0
    swapped = jnp.where(first_half, pltpu.roll(y, LANES - HEAD_DIM // 2, 1), pltpu.roll(y, HEAD_DIM // 2, 1))
    return y * cos + swapped * sin_signed


def _proj_kernel(*refs, n_x, n_cache, n_w, rope_tiles, scale_tiles, cache_plan):
    x_refs = refs[:n_x]
    mod_ref, w_ref, cos_ref, sin_ref, o_ref = refs[n_x:n_x + 5]
    cache_refs = refs[n_x + 5:n_x + 5 + n_cache]
    wbf_ref, u_ref = refs[n_x + 5 + n_cache:]
    i = pl.program_id(0)

    @pl.when(i < n_w)
    def _():
        wbf_ref[i] = w_ref[0].astype(bf16)

    def tokens(is_ctx):
        x_ref = x_refs[0] if is_ctx else x_refs[-1]
        shift = mod_ref[0, 0:1, :]
        scale = mod_ref[0, 1:2, :]
        groups = [slice(b * SEQ, (b + 1) * SEQ) for b in range(TM // SEQ)]
        for rs in groups:
            u_ref[rs, :] = (_ln(x_ref[rs, :]) * (1.0 + scale) + shift).astype(bf16)
        for c in range(n_w):
            for b, rs in enumerate(groups):
                y = _dot(u_ref[rs, :], wbf_ref[c])
                for t in range(N_CHUNK // LANES):
                    tile = c * (N_CHUNK // LANES) + t
                    piece = y[:, t * LANES:(t + 1) * LANES]
                    if tile in rope_tiles and not is_ctx:
                        piece = _rope_pair(piece, cos_ref[rs, :], sin_ref[rs, :])
                    if tile in scale_tiles:
                        piece = piece * (QK_SCALE * LOG2_E)
                    o_ref[rs, tile * LANES:(tile + 1) * LANES] = piece.astype(o_ref.dtype)
                    if tile in cache_plan and is_ctx:
                        kind, out_idx, slot = cache_plan[tile]
                        c_ref = cache_refs[out_idx]
                        if kind == "plain":
                            c_ref[b, 0, slot] = piece
                        else:
                            piece_t = piece.T
                            if kind == "heads":
                                c_ref[b, 0, 0] = piece_t[0:HEAD_DIM]
                                c_ref[b, 0, 1] = piece_t[HEAD_DIM:]
                            else:
                                c_ref[b, 0, slot, 0] = piece_t[0:HEAD_DIM]
                                c_ref[b, 0, slot, 1] = piece_t[HEAD_DIM:]

    t = i - n_w

    @pl.when(jnp.logical_and(t >= 0, t < CTX_BLOCKS))
    def _():
        tokens(True)

    @pl.when(t >= CTX_BLOCKS)
    def _():
        tokens(False)


def _proj(x_parts, mod, w_all, layer, scale_tiles, rope_tabs, rope_tiles, cache_shapes, cache_plan):
    n_out = w_all.shape[2]
    n_w = n_out // N_CHUNK
    nb = DEC_SEQ // TM
    tok = lambda i: _tok(i, n_w)
    in_specs = _token_specs(x_parts, n_w) + [
        pl.BlockSpec((1, 6, D_MODEL), lambda i: (_mod_row(tok(i)), 0, 0)),
        pl.BlockSpec((1, D_MODEL, N_CHUNK), lambda i: (layer, 0, jnp.minimum(i, n_w - 1))),
        pl.BlockSpec((TM, LANES), lambda i: (_lat_blk(tok(i)) % nb, 0)),
        pl.BlockSpec((TM, LANES), lambda i: (_lat_blk(tok(i)) % nb, 0))]
    out_specs = [pl.BlockSpec((TM, n_out), lambda i: (tok(i), 0))]
    out_shape = [jax.ShapeDtypeStruct((N_TOK, n_out), bf16)]
    for shp in cache_shapes:
        blk = (TM // SEQ,) + tuple(shp[1:])
        out_specs.append(pl.BlockSpec(blk, lambda i, nd=len(shp): (_ctx_blk(tok(i)),) + (0,) * (nd - 1)))
        out_shape.append(jax.ShapeDtypeStruct(tuple(shp), f32))
    return pl.pallas_call(
        functools.partial(_proj_kernel, n_x=len(x_parts), n_cache=len(cache_shapes), n_w=n_w,
                          rope_tiles=frozenset(rope_tiles), scale_tiles=frozenset(scale_tiles),
                          cache_plan=dict(cache_plan)),
        grid=(n_w + TOK_BLOCKS,),
        in_specs=in_specs,
        out_specs=out_specs,
        out_shape=out_shape,
        scratch_shapes=[pltpu.VMEM((n_w, D_MODEL, N_CHUNK), bf16), pltpu.VMEM((TM, D_MODEL), bf16)],
        compiler_params=_params(1),
        name="proj",
    )(*x_parts, mod, w_all, *rope_tabs)


def _post_kernel(*refs, n_x, n_y, ka, kb, n_w):
    x_refs = refs[:n_x]
    (ac_ref, al_ref, bc_ref, bl_ref, mod_ref, lng_ref, lnb_ref,
     wo_ref, wg_ref, wu_ref, wd_ref) = refs[n_x:n_x + 11]
    y_refs = refs[n_x + 11:n_x + 11 + n_y]
    wo_s, wg_s, wu_s, wd_s, x1_ref, u_ref, h_ref, y_ref = refs[n_x + 11 + n_y:]
    n_wo = (ka + kb) // N_CHUNK
    i = pl.program_id(0)

    @pl.when(i < n_w)
    def _():
        wg_s[i] = wg_ref[0].astype(bf16)
        wu_s[i] = wu_ref[0].astype(bf16)
        wd_s[i] = wd_ref[0].astype(bf16)

    @pl.when(i < n_wo)
    def _():
        wo_s[i] = wo_ref[0].astype(bf16)

    @pl.when(i >= n_w)
    def _():
        is_ctx = (i - n_w) < CTX_BLOCKS
        gate1 = mod_ref[0, 2:3, :]
        shift2 = mod_ref[0, 3:4, :]
        scale2 = mod_ref[0, 4:5, :]
        gate2 = mod_ref[0, 5:6, :]
        groups = [slice(r * TM // ROW_GROUPS, (r + 1) * TM // ROW_GROUPS) for r in range(ROW_GROUPS)]
        for rs in groups:
            a = _pick((ac_ref, al_ref), is_ctx, rs)
            b = _pick((bc_ref, bl_ref), is_ctx, rs)
            pieces = ([a[:, c:c + N_CHUNK] for c in range(0, ka, N_CHUNK)]
                      + [b[:, c:c + N_CHUNK] for c in range(0, kb, N_CHUNK)])
            h = functools.reduce(lambda s, p: s + p, [_dot(p, wo_s[c]) for c, p in enumerate(pieces)])
            x1 = _ln(ALPHA * _pick(x_refs, is_ctx, rs) + gate1 * h) * lng_ref[0, 0:1, :] + lnb_ref[0, 0:1, :]
            x1_ref[rs, :] = x1
            u_ref[rs, :] = (_ln(x1) * (1.0 + scale2) + shift2).astype(bf16)
        def ffn_chunk(rs, c):
            g = _dot(u_ref[rs, :], wg_s[c])
            up = _dot(u_ref[rs, :], wu_s[c])
            h_ref[rs, c * N_CHUNK:(c + 1) * N_CHUNK] = (_silu(g) * up).astype(bf16)

        def ffn_down(rs):
            ffn = functools.reduce(lambda s, p: s + p,
                                   [_dot(h_ref[rs, c * N_CHUNK:(c + 1) * N_CHUNK], wd_s[c]) for c in range(n_w)])
            y_ref[rs, :] = _ln(ALPHA * x1_ref[rs, :] + gate2 * ffn) * lng_ref[0, 1:2, :] + lnb_ref[0, 1:2, :]

        for c in range(n_w + FFN_SKEW * (ROW_GROUPS - 1)):
            for r, rs in enumerate(groups):
                cc = c - FFN_SKEW * r
                if 0 <= cc < n_w:
                    ffn_chunk(rs, cc)
                if cc == n_w - 1:
                    ffn_down(rs)
        if n_y == 1:
            y_refs[0][...] = y_ref[...]
        else:
            @pl.when(is_ctx)
            def _():
                y_refs[0][...] = y_ref[...]

            @pl.when(jnp.logical_not(is_ctx))
            def _():
                y_refs[1][...] = y_ref[...]


def _post(x_parts, mix_a, mix_b, mod, ln_g, ln_b, w_out, w_gate, w_up, w_down, layer, mix_layer, split_out):
    ka, kb = mix_a[0].shape[1], mix_b[0].shape[1]
    n_w = D_FF // N_CHUNK
    n_wo = (ka + kb) // N_CHUNK
    tok = lambda i: _tok(i, n_w)
    lay = lambda i: (layer, 0, 0)
    in_specs = (_token_specs(x_parts, n_w) + _token_specs(mix_a, n_w) + _token_specs(mix_b, n_w) + [
        pl.BlockSpec((1, 6, D_MODEL), lambda i: (_mod_row(tok(i)), 0, 0)),
        pl.BlockSpec((1, 2, D_MODEL), lay),
        pl.BlockSpec((1, 2, D_MODEL), lay),
        pl.BlockSpec((1, N_CHUNK, D_MODEL), lambda i: (mix_layer, jnp.minimum(i, n_wo - 1), 0)),
        pl.BlockSpec((1, D_MODEL, N_CHUNK), lambda i: (layer, 0, jnp.minimum(i, n_w - 1))),
        pl.BlockSpec((1, D_MODEL, N_CHUNK), lambda i: (layer, 0, jnp.minimum(i, n_w - 1))),
        pl.BlockSpec((1, N_CHUNK, D_MODEL), lambda i: (layer, jnp.minimum(i, n_w - 1), 0))])
    if split_out:
        out_specs = [pl.BlockSpec((TM, D_MODEL), lambda i: (_ctx_blk(tok(i)), 0)),
                     pl.BlockSpec((TM, D_MODEL), lambda i: (_lat_blk(tok(i)), 0))]
        out_shape = [jax.ShapeDtypeStruct((N_CTX, D_MODEL), f32), jax.ShapeDtypeStruct((N_LAT, D_MODEL), f32)]
    else:
        out_specs = [pl.BlockSpec((TM, D_MODEL), lambda i: (tok(i), 0))]
        out_shape = [jax.ShapeDtypeStruct((N_TOK, D_MODEL), f32)]
    return pl.pallas_call(
        functools.partial(_post_kernel, n_x=len(x_parts), n_y=len(out_shape), ka=ka, kb=kb, n_w=n_w),
        grid=(n_w + TOK_BLOCKS,),
        in_specs=in_specs,
        out_specs=out_specs,
        out_shape=out_shape,
        scratch_shapes=[pltpu.VMEM((n_wo, N_CHUNK, D_MODEL), bf16), pltpu.VMEM((n_w, D_MODEL, N_CHUNK), bf16),
                        pltpu.VMEM((n_w, D_MODEL, N_CHUNK), bf16), pltpu.VMEM((n_w, N_CHUNK, D_MODEL), bf16),
                        pltpu.VMEM((TM, D_MODEL), f32), pltpu.VMEM((TM, D_MODEL), bf16),
                        pltpu.VMEM((TM, D_FF), bf16), pltpu.VMEM((TM, D_MODEL), f32)],
        compiler_params=_params(1),
        name="post",
    )(*x_parts, *mix_a, *mix_b, mod, ln_g, ln_b, w_out, w_gate, w_up, w_down)


def _group_norm_gate(ro, rg, gmat, gn_g, gn_b):
    def gmean(parts):
        cols = []
        for c in range(0, RET_W, N_CHUNK):
            cols.append(sum(_dot(p[:, c:c + N_CHUNK], gmat) for p in parts))
        return jnp.concatenate(cols, -1)

    d = ro - gmean(_split_bf16(ro))
    var = gmean([(d * d).astype(bf16)])
    y = d * lax.rsqrt(var + LN_EPS) * gn_g + gn_b
    return _silu(rg.astype(f32)) * y


def _dup_head(x, j):
    first = _lane_half_mask(x.shape)
    keep = first if j == 0 else jnp.logical_not(first)
    xm = jnp.where(keep, x.astype(f32), 0.0)
    return xm + pltpu.roll(xm, HEAD_DIM, 1)


def _softmax_parts(scores, sink):
    m = sink
    for s in scores:
        m = jnp.maximum(m, jnp.max(s, -1, keepdims=True))
    es = [jnp.exp2(s - m) for s in scores]
    denom = jnp.exp2(sink - m)
    for e in es:
        denom = denom + jnp.sum(e, -1, keepdims=True)
    return es, denom


def _retention_tables(lg_ref, lgf_ref, lgb_ref, dmask_ref, kdec_ref, n):
    row = lax.broadcasted_iota(jnp.int32, (n, n), 0)
    col = lax.broadcasted_iota(jnp.int32, (n, n), 1)
    diff = (row - col).astype(f32)
    diag = jnp.where(row == col, 2.0 * QK_SCALE, QK_SCALE)
    for h in range(H_RET):
        dmask_ref[h] = jnp.exp(jnp.where(diff >= 0, lg_ref[0, h] * diff, -lg_ref[1, h] * diff)) * diag
    t = lax.broadcasted_iota(jnp.int32, (n, RET_W), 0).astype(f32)
    kdec_ref[0] = jnp.exp(lgf_ref[...] * (n - 1.0 - t)) * QK_SCALE
    kdec_ref[1] = jnp.exp(lgb_ref[...] * t) * QK_SCALE


def _retention_intra(pairs, q_of, k_of, v_of, dmask_ref):
    first = _lane_half_mask(k_of(pairs[0]).shape)
    masked = {}
    for p in pairs:
        kb = k_of(p)
        for e in range(2):
            keep = first if e == 0 else jnp.logical_not(first)
            s = _dot_nt(q_of(p), jnp.where(keep, kb, jnp.zeros_like(kb))) * dmask_ref[2 * p + e]
            masked[p, e] = s.astype(bf16)
    outs = {}
    for p in pairs:
        pv = [_dot(masked[p, e], v_of(p)) for e in range(2)]
        outs[p] = jnp.where(_lane_half_mask(pv[0].shape), pv[0], pv[1])
    return outs


def _window_group(subs, q_of, k_parts_of, v_parts_of, masks, sink_of):
    scores = {}
    for key in subs:
        parts = [_dot_nt(q_of(key), k) for k in k_parts_of(key)]
        scores[key] = [sc if mk is None else jnp.where(mk, sc, NEG_BIG) for sc, mk in zip(parts, masks)]
    probs = {}
    for key in subs:
        es, denom = _softmax_parts(scores[key], sink_of(key))
        probs[key] = ([ex.astype(bf16) for ex in es], denom)
    outs = {}
    for key in subs:
        es, denom = probs[key]
        pv = functools.reduce(lambda x, y: x + y, [_dot(ex, v) for ex, v in zip(es, v_parts_of(key))])
        outs[key] = pv / denom
    return outs


def _ctx_ab_kernel(lg_ref, sink_ref, rq_ref, rk_ref, rv_ref, rg_ref, wq_ref, wk_ref, wv_ref,
                   lgf_ref, lgb_ref, gmat_ref, gng_ref, gnb_ref,
                   ro_ref, wo_ref, st_ref, dmask_ref, kdec_ref, ret_ref):
    t_len = SEQ

    @pl.when(pl.program_id(0) == 0)
    def _():
        _retention_tables(lg_ref, lgf_ref, lgb_ref, dmask_ref, kdec_ref, t_len)

    first = _lane_half_mask((t_len, PAIR_W))
    for sq in range(CTX_SEQS):
        rows = slice(sq * t_len, (sq + 1) * t_len)
        psl = lambda p: slice(p * PAIR_W, (p + 1) * PAIR_W)
        for p0 in range(0, H_RET // 2, RET_GROUP):
            pairs = list(range(p0, p0 + RET_GROUP))
            intra = _retention_intra(pairs, lambda p: rq_ref[rows, psl(p)], lambda p: rk_ref[rows, psl(p)],
                                     lambda p: rv_ref[rows, psl(p)], dmask_ref)
            for p in pairs:
                ret_ref[rows, psl(p)] = intra[p]
        for p in range(H_RET // 2):
            sl = psl(p)
            kb = rk_ref[rows, sl]
            v = rv_ref[rows, sl]
            for d in range(2):
                kd_t = (kb * kdec_ref[d, :, sl]).T.astype(bf16)
                st = _dot(kd_t, v)
                st_ref[sq, d, 2 * p] = st[0:HEAD_DIM, 0:HEAD_DIM]
                st_ref[sq, d, 2 * p + 1] = pltpu.roll(st[HEAD_DIM:, :], HEAD_DIM, 1)[:, 0:HEAD_DIM]
        ro_ref[rows, :] = _group_norm_gate(ret_ref[rows, :], rg_ref[rows, :], gmat_ref[...], gng_ref[...],
                                           gnb_ref[...]).astype(bf16)

        k_dup = [_dup_head(wk_ref[rows, :], j).astype(bf16) for j in range(KV_WIN)]
        v_dup = [_dup_head(wv_ref[rows, :], j).astype(bf16) for j in range(KV_WIN)]

        def q_masked(key):
            qp, e = key
            qb = wq_ref[rows, qp * PAIR_W:(qp + 1) * PAIR_W]
            return jnp.where(first if e == 0 else jnp.logical_not(first), qb, jnp.zeros_like(qb))

        kv_of = lambda key: key[0] * 2 // G_WIN
        for g0 in range(0, H_WIN // 2, WIN_GROUP):
            subs = [(qp, e) for qp in range(g0, g0 + WIN_GROUP) for e in range(2)]
            outs = _window_group(subs, q_masked, lambda key: [k_dup[kv_of(key)]], lambda key: [v_dup[kv_of(key)]],
                                 [None], lambda key: sink_ref[0, 2 * key[0] + key[1]] * LOG2_E)
            for qp in range(g0, g0 + WIN_GROUP):
                wo_ref[rows, qp * PAIR_W:(qp + 1) * PAIR_W] = jnp.where(first, outs[qp, 0], outs[qp, 1]).astype(bf16)


def _ctx_ab(proj, log_gamma, sink, lgf_lanes, lgb_lanes, gmat, gn_g, gn_b):
    t = SEQ
    tb = CTX_SEQS * t
    smem = pl.BlockSpec(memory_space=pltpu.SMEM)
    const = lambda b: (0, 0)
    col = lambda c: (lambda b: (b, c))
    return pl.pallas_call(
        _ctx_ab_kernel,
        grid=(BATCH // CTX_SEQS,),
        in_specs=[smem, smem,
                  pl.BlockSpec((tb, RET_W), col(0)), pl.BlockSpec((tb, RET_W), col(1)),
                  pl.BlockSpec((tb, RET_W), col(2)), pl.BlockSpec((tb, RET_W), col(3)),
                  pl.BlockSpec((tb, WIN_W), col(4)),
                  pl.BlockSpec((tb, KV_W), col((4 * RET_W + WIN_W) // KV_W)),
                  pl.BlockSpec((tb, KV_W), col((4 * RET_W + WIN_W) // KV_W + 1)),
                  pl.BlockSpec((1, RET_W), const), pl.BlockSpec((1, RET_W), const),
                  pl.BlockSpec((N_CHUNK, N_CHUNK), const),
                  pl.BlockSpec((1, RET_W), const), pl.BlockSpec((1, RET_W), const)],
        out_specs=[pl.BlockSpec((tb, RET_W), lambda b: (b, 0)),
                   pl.BlockSpec((tb, WIN_W), lambda b: (b, 0)),
                   pl.BlockSpec((CTX_SEQS, 2, H_RET, HEAD_DIM, HEAD_DIM), lambda b: (b, 0, 0, 0, 0))],
        out_shape=[jax.ShapeDtypeStruct((BATCH * t, RET_W), bf16),
                   jax.ShapeDtypeStruct((BATCH * t, WIN_W), bf16),
                   jax.ShapeDtypeStruct((BATCH, 2, H_RET, HEAD_DIM, HEAD_DIM), f32)],
        scratch_shapes=[pltpu.VMEM((H_RET, t, t), f32), pltpu.VMEM((2, t, RET_W), f32),
                        pltpu.VMEM((tb, RET_W), f32)],
        compiler_params=_params(1),
        name="ctx_ab",
    )(log_gamma, sink, proj, proj, proj, proj, proj, proj, proj, lgf_lanes, lgb_lanes, gmat, gn_g, gn_b)


def _lat_ab_kernel(lg_ref, sink_ref, rq_ref, rk_ref, rv_ref, rg_ref, wq_ref, wk_ref, wv_ref, ck_ref, cv_ref,
                   s0_ref, lgf_ref, lgb_ref, gmat_ref, gng_ref, gnb_ref,
                   ro_ref, wo_ref, ret_ref, dmask_ref, kdec_ref, qdec_ref, sf_ref, sb_ref):
    t_len = DEC_SEQ
    n_chunks = t_len // TQ
    chunk = pl.program_id(1)
    q0 = pl.multiple_of(chunk * TQ, TQ)
    first = _lane_half_mask((TQ, PAIR_W))

    @pl.when(jnp.logical_and(pl.program_id(0) == 0, chunk == 0))
    def _():
        _retention_tables(lg_ref, lgf_ref, lgb_ref, dmask_ref, kdec_ref, TQ)
        t = lax.broadcasted_iota(jnp.int32, (TQ, RET_W), 0).astype(f32)
        qdec_ref[0] = jnp.exp(lgf_ref[...] * (t + 1.0))
        qdec_ref[1] = jnp.exp(lgb_ref[...] * (TQ - t))

    @pl.when(chunk == 0)
    def _():
        r = lax.broadcasted_iota(jnp.int32, (PAIR_W, PAIR_W), 0)
        c_ = lax.broadcasted_iota(jnp.int32, (PAIR_W, PAIR_W), 1)
        same_head = (r < HEAD_DIM) == (c_ < HEAD_DIM)
        for p in range(H_RET // 2):
            sl = slice(p * PAIR_W, (p + 1) * PAIR_W)
            kv = []
            for c in range(n_chunks):
                rows = slice(c * TQ, (c + 1) * TQ)
                kc = rk_ref[rows, sl]
                vc = rv_ref[rows, sl]
                kv.append([jnp.where(same_head, _dot((kc * kdec_ref[d, :, sl]).T.astype(bf16), vc), 0.0)
                           for d in range(2)])
            state = s0_ref[0, 0, p]
            for c in range(n_chunks):
                sf_ref[c, p] = state
                state = state * jnp.exp(lgf_ref[:, sl] * TQ) + kv[c][0]
            state = s0_ref[0, 1, p]
            for c in reversed(range(n_chunks)):
                sb_ref[c, p] = state
                state = state * jnp.exp(lgb_ref[:, sl] * TQ) + kv[c][1]

    psl = lambda p: slice(p * PAIR_W, (p + 1) * PAIR_W)
    intra = {}
    for p0 in range(0, H_RET // 2, RET_GROUP):
        intra.update(_retention_intra(list(range(p0, p0 + RET_GROUP)), lambda p: rq_ref[:, psl(p)],
                                      lambda p: rk_ref[pl.ds(q0, TQ), psl(p)],
                                      lambda p: rv_ref[pl.ds(q0, TQ), psl(p)], dmask_ref))
    for p in range(H_RET // 2):
        sl = psl(p)
        q = rq_ref[:, sl]
        o = intra[p]
        o = o + _dot(q, sf_ref[chunk, p].astype(bf16)) * qdec_ref[0, :, sl]
        o = o + _dot(q, sb_ref[chunk, p].astype(bf16)) * qdec_ref[1, :, sl]
        ret_ref[:, sl] = o
    ro_ref[...] = _group_norm_gate(ret_ref[...], rg_ref[...], gmat_ref[...], gng_ref[...], gnb_ref[...]).astype(bf16)

    band = TQ + 2 * WINDOW
    k_start = pl.multiple_of(jnp.clip(q0 - WINDOW, 0, t_len - band), LANES)
    qi = q0 + lax.broadcasted_iota(jnp.int32, (TQ, band), 0)
    kj = k_start + lax.broadcasted_iota(jnp.int32, (TQ, band), 1)
    in_band = jnp.abs(qi - kj) <= WINDOW
    k_parts = [[_dup_head(wk_ref[pl.ds(k_start, band), :], j).astype(bf16), _dup_head(ck_ref[0], j).astype(bf16)]
               for j in range(KV_WIN)]
    v_parts = [[_dup_head(wv_ref[pl.ds(k_start, band), :], j).astype(bf16), _dup_head(cv_ref[0], j).astype(bf16)]
               for j in range(KV_WIN)]

    def q_masked(key):
        qp, e = key
        qb = wq_ref[:, qp * PAIR_W:(qp + 1) * PAIR_W]
        return jnp.where(first if e == 0 else jnp.logical_not(first), qb, jnp.zeros_like(qb))

    kv_of = lambda key: key[0] * 2 // G_WIN
    for g0 in range(0, H_WIN // 2, WIN_GROUP_LAT):
        subs = [(qp, e) for qp in range(g0, g0 + WIN_GROUP_LAT) for e in range(2)]
        outs = _window_group(subs, q_masked, lambda key: k_parts[kv_of(key)], lambda key: v_parts[kv_of(key)],
                             [in_band, None], lambda key: sink_ref[0, 2 * key[0] + key[1]] * LOG2_E)
        for qp in range(g0, g0 + WIN_GROUP_LAT):
            wo_ref[:, qp * PAIR_W:(qp + 1) * PAIR_W] = jnp.where(first, outs[qp, 0], outs[qp, 1]).astype(bf16)


def _lat_ab(proj, log_gamma, sink, ck, cv, s0_pairs, lgf_lanes, lgb_lanes, gmat, gn_g, gn_b):
    t = DEC_SEQ
    nq = t // TQ
    smem = pl.BlockSpec(memory_space=pltpu.SMEM)
    const = lambda b, i: (0, 0)
    qcol = lambda c: (lambda b, i: (N_CTX // TQ + b * nq + i, c))
    bcol = lambda c: (lambda b, i: (N_CTX // t + b, c))
    kv_col = (4 * RET_W + WIN_W) // KV_W
    return pl.pallas_call(
        _lat_ab_kernel,
        grid=(DEC_BATCH, nq),
        in_specs=[smem, smem,
                  pl.BlockSpec((TQ, RET_W), qcol(0)), pl.BlockSpec((t, RET_W), bcol(1)),
                  pl.BlockSpec((t, RET_W), bcol(2)), pl.BlockSpec((TQ, RET_W), qcol(3)),
                  pl.BlockSpec((TQ, WIN_W), qcol(4)),
                  pl.BlockSpec((t, KV_W), bcol(kv_col)), pl.BlockSpec((t, KV_W), bcol(kv_col + 1)),
                  pl.BlockSpec((1, PAST_LEN, KV_W), lambda b, i: (b, 0, 0)),
                  pl.BlockSpec((1, PAST_LEN, KV_W), lambda b, i: (b, 0, 0)),
                  pl.BlockSpec((1, 2, H_RET // 2, PAIR_W, PAIR_W), lambda b, i: (b, 0, 0, 0, 0)),
                  pl.BlockSpec((1, RET_W), const), pl.BlockSpec((1, RET_W), const),
                  pl.BlockSpec((N_CHUNK, N_CHUNK), const),
                  pl.BlockSpec((1, RET_W), const), pl.BlockSpec((1, RET_W), const)],
        out_specs=[pl.BlockSpec((TQ, RET_W), lambda b, i: (b * nq + i, 0)),
                   pl.BlockSpec((TQ, WIN_W), lambda b, i: (b * nq + i, 0))],
        out_shape=[jax.ShapeDtypeStruct((DEC_BATCH * t, RET_W), bf16),
                   jax.ShapeDtypeStruct((DEC_BATCH * t, WIN_W), bf16)],
        scratch_shapes=[pltpu.VMEM((TQ, RET_W), f32), pltpu.VMEM((H_RET, TQ, TQ), f32),
                        pltpu.VMEM((2, TQ, RET_W), f32), pltpu.VMEM((2, TQ, RET_W), f32),
                        pltpu.VMEM((t // TQ, H_RET // 2, PAIR_W, PAIR_W), f32),
                        pltpu.VMEM((t // TQ, H_RET // 2, PAIR_W, PAIR_W), f32)],
        compiler_params=_params(2),
        name="lat_ab",
    )(log_gamma, sink, proj, proj, proj, proj, proj, proj, proj, ck, cv, s0_pairs,
      lgf_lanes, lgb_lanes, gmat, gn_g, gn_b)


def _lambda_full(lam_ref, lam_init):
    lam = lam_ref[...]
    a = jnp.sum(lam[0:1, :] * lam[1:2, :], -1, keepdims=True)
    b = jnp.sum(lam[2:3, :] * lam[3:4, :], -1, keepdims=True)
    return jnp.exp(a) - jnp.exp(b) + lam_init


def _diff_heads(q_of, k_parts_of, v_parts_of, lam, subln, lam_init, group):
    res = []
    for h0 in range(0, H_DIFF, group):
        res += _diff_head_group(range(h0, h0 + group), q_of, k_parts_of, v_parts_of, lam, subln, lam_init)
    return res


def _diff_head_group(heads, q_of, k_parts_of, v_parts_of, lam, subln, lam_init):
    subs = [(h, e) for h in heads for e in range(2)]
    scores = {}
    for h, e in subs:
        q = q_of(h)
        fm = _lane_half_mask(q.shape)
        q_sub = jnp.where(fm if e == 0 else jnp.logical_not(fm), q, jnp.zeros_like(q))
        scores[h, e] = [_dot(q_sub, k) if transposed else _dot_nt(q_sub, k) for k, transposed in k_parts_of(h)]
    probs = {}
    for key in subs:
        m = scores[key][0].max(-1, keepdims=True)
        for sc in scores[key][1:]:
            m = jnp.maximum(m, sc.max(-1, keepdims=True))
        es = [jnp.exp2(sc - m) for sc in scores[key]]
        denom = es[0].sum(-1, keepdims=True)
        for ex in es[1:]:
            denom = denom + ex.sum(-1, keepdims=True)
        probs[key] = ([ex.astype(bf16) for ex in es], denom)
    outs = {}
    for h, e in subs:
        es, denom = probs[h, e]
        pv = functools.reduce(lambda x, y: x + y, [_dot(ex, v) for v, ex in zip(v_parts_of(h), es)])
        outs[h, e] = pv / denom
    res = []
    for h in heads:
        a = outs[h, 0] - lam * outs[h, 1]
        res.append(a * lax.rsqrt(jnp.mean(a * a, -1, keepdims=True) + LN_EPS) * subln * (1.0 - lam_init))
    return res


def _fourier_rows(ct_ref, st_ref, z, bdc_ref, bds_ref):
    zb = z
    zc = _dot(zb, bdc_ref[...].astype(bf16)).astype(bf16)
    zs = _dot(zb, bds_ref[...].astype(bf16)).astype(bf16)
    return _dot(ct_ref[...].astype(bf16), zc) - _dot(st_ref[...].astype(bf16), zs)


def _ctx_cd_kernel(q_ref, k_ref, v_ref, z_ref, lam_ref, subln_ref, ct_ref, st_ref, bdc_ref, bds_ref,
                   a_ref, zf_ref, *, lam_init):
    lam = _lambda_full(lam_ref, lam_init)
    for sq in range(CTX_SEQS_CD):
        rows = slice(sq * SEQ, (sq + 1) * SEQ)
        sl = lambda h: slice(h * PAIR_W, (h + 1) * PAIR_W)
        heads = _diff_heads(lambda h: q_ref[rows, sl(h)], lambda h: [(k_ref[rows, sl(h)], False)],
                            lambda h: [v_ref[rows, sl(h)]], lam, subln_ref[...], lam_init, DIFF_GROUP)
        for h in range(H_DIFF):
            a_ref[rows, sl(h)] = heads[h].astype(bf16)
        zf_ref[rows, :] = _fourier_rows(ct_ref, st_ref, z_ref[rows, :], bdc_ref, bds_ref).astype(bf16)


def _ctx_cd(proj, lam, subln, ct, st, bdc, bds, lam_init):
    t = SEQ
    tb = CTX_SEQS_CD * t
    const = lambda b: (0, 0)
    col = lambda c: (lambda b: (b, c))
    return pl.pallas_call(
        functools.partial(_ctx_cd_kernel, lam_init=lam_init),
        grid=(BATCH // CTX_SEQS_CD,),
        in_specs=[pl.BlockSpec((tb, DIFF_W), col(0)), pl.BlockSpec((tb, DIFF_W), col(1)),
                  pl.BlockSpec((tb, DIFF_W), col(2)), pl.BlockSpec((tb, FNET_W), col(3 * DIFF_W // FNET_W)),
                  pl.BlockSpec((4, HEAD_DIM), const), pl.BlockSpec((1, PAIR_W), const),
                  pl.BlockSpec((t, t), const), pl.BlockSpec((t, t), const),
                  pl.BlockSpec((FNET_W, FNET_W), const), pl.BlockSpec((FNET_W, FNET_W), const)],
        out_specs=[pl.BlockSpec((tb, DIFF_W), lambda b: (b, 0)), pl.BlockSpec((tb, FNET_W), lambda b: (b, 0))],
        out_shape=[jax.ShapeDtypeStruct((BATCH * t, DIFF_W), bf16),
                   jax.ShapeDtypeStruct((BATCH * t, FNET_W), bf16)],
        compiler_params=_params(1),
        name="ctx_cd",
    )(proj, proj, proj, proj, lam, subln, ct, st, bdc, bds)


def _lat_cd_kernel(q_ref, k_ref, v_ref, z_ref, ckt_ref, cv_ref, lam_ref, subln_ref, ct_ref, st_ref, bdc_ref, bds_ref,
                   a_ref, zf_ref, *, lam_init):
    lam = _lambda_full(lam_ref, lam_init)
    sl = lambda h: slice(h * PAIR_W, (h + 1) * PAIR_W)
    heads = _diff_heads(lambda h: q_ref[:, sl(h)],
                        lambda h: [(k_ref[:, sl(h)], False), (ckt_ref[0, h].astype(bf16), True)],
                        lambda h: [v_ref[:, sl(h)], cv_ref[0, h].astype(bf16)], lam, subln_ref[...], lam_init,
                        DIFF_GROUP_LAT)
    for h in range(H_DIFF):
        a_ref[:, sl(h)] = heads[h].astype(bf16)
    zf_ref[...] = _fourier_rows(ct_ref, st_ref, z_ref[...], bdc_ref, bds_ref).astype(bf16)


def _lat_cd(proj, ck, cv, lam, subln, ct, st, bdc, bds, lam_init):
    t = DEC_SEQ
    nq = t // TQ
    const = lambda b, i: (0, 0)
    return pl.pallas_call(
        functools.partial(_lat_cd_kernel, lam_init=lam_init),
        grid=(DEC_BATCH, nq),
        in_specs=[pl.BlockSpec((TQ, DIFF_W), lambda b, i: (N_CTX // TQ + b * nq + i, 0)),
                  pl.BlockSpec((t, DIFF_W), lambda b, i: (N_CTX // t + b, 1)),
                  pl.BlockSpec((t, DIFF_W), lambda b, i: (N_CTX // t + b, 2)),
                  pl.BlockSpec((t, FNET_W), lambda b, i: (N_CTX // t + b, 3 * DIFF_W // FNET_W)),
                  pl.BlockSpec((1, H_DIFF, PAIR_W, PAST_LEN), lambda b, i: (b, 0, 0, 0)),
                  pl.BlockSpec((1, H_DIFF, PAST_LEN, PAIR_W), lambda b, i: (b, 0, 0, 0)),
                  pl.BlockSpec((4, HEAD_DIM), const), pl.BlockSpec((1, PAIR_W), const),
                  pl.BlockSpec((TQ, t), lambda b, i: (i, 0)), pl.BlockSpec((TQ, t), lambda b, i: (i, 0)),
                  pl.BlockSpec((FNET_W, FNET_W), const), pl.BlockSpec((FNET_W, FNET_W), const)],
        out_specs=[pl.BlockSpec((TQ, DIFF_W), lambda b, i: (b * nq + i, 0)),
                   pl.BlockSpec((TQ, FNET_W), lambda b, i: (b * nq + i, 0))],
        out_shape=[jax.ShapeDtypeStruct((DEC_BATCH * t, DIFF_W), bf16),
                   jax.ShapeDtypeStruct((DEC_BATCH * t, FNET_W), bf16)],
        compiler_params=_params(2),
        name="lat_cd",
    )(proj, proj, proj, proj, ck, cv, lam, subln, ct, st, bdc, bds)


def _rope_tables():
    t = np.arange(DEC_SEQ)
    quarter = HEAD_DIM // 4
    inv = ROPE_BASE ** (-np.arange(quarter, dtype=np.float64) / quarter)
    ang = np.concatenate([(t // GRID_W)[:, None] * inv, (t % GRID_W)[:, None] * inv], -1)
    cos, sin = np.cos(ang), np.sin(ang)
    reps = LANES // HEAD_DIM
    return (np.tile(np.concatenate([cos, cos], -1), (1, reps)).astype(np.float32),
            np.tile(np.concatenate([-sin, sin], -1), (1, reps)).astype(np.float32))


def _dft_tables(n):
    k = np.arange(n)
    ang = (2.0 * math.pi / n) * ((k[:, None] * k[None, :]) % n)
    return (np.cos(ang) / math.sqrt(n)).astype(np.float32), (np.sin(ang) / math.sqrt(n)).astype(np.float32)


def _block_diag(m, reps):
    return np.kron(np.eye(reps, dtype=m.dtype), m)


def kernel(x_prompt, x_sample, state_ret, cache_win_k, cache_win_v, cache_diff_k, cache_diff_v, c, c_ctx, w_mod, b_mod, ln_g, ln_b, w_in_ab, w_out_ab, ret_log_gamma, ret_gn_g, ret_gn_b, win_sink, w_in_cd, w_out_cd, diff_lambda, diff_subln_g, w_gate, w_up, w_down):
    cond = jnp.concatenate([c_ctx[None, :], c, jnp.zeros((8 - 1 - DEC_BATCH, D_MODEL), f32)], 0)
    mod = _modulation(cond, w_mod, b_mod).reshape(DEPTH, 8, 6, D_MODEL)

    rope_tabs = _rope_tables()
    gmat = jnp.asarray(_block_diag(np.full((HEAD_DIM, HEAD_DIM), 1.0 / HEAD_DIM, np.float32),
                                   N_CHUNK // HEAD_DIM), bf16)
    c64, s64 = _dft_tables(FNET_DIM)
    bdc = _block_diag(c64, FNET_GROUPS)
    bds = _block_diag(s64, FNET_GROUPS)
    dft_ctx = _dft_tables(SEQ)
    dft_lat = _dft_tables(DEC_SEQ)

    x_parts = [x_prompt.reshape(N_CTX, D_MODEL), x_sample.reshape(N_LAT, D_MODEL)]
    outs = {}
    for l in range(DEPTH):
        i = l // 2
        if l % 2 == 0:
            lgf = jnp.repeat(ret_log_gamma[i, 0], HEAD_DIM)[None, :]
            lgb = jnp.repeat(ret_log_gamma[i, 1], HEAD_DIM)[None, :]
            gn_g = ret_gn_g[i][None, :]
            gn_b = ret_gn_b[i][None, :]
            sink = win_sink[i][None, :]
            rope_tiles = tuple(range(0, 2 * RET_W // LANES)) + tuple(
                range(4 * RET_W // LANES, (4 * RET_W + WIN_W + KV_W) // LANES))
            kv_tile = (4 * RET_W + WIN_W) // LANES
            kv_shape = (BATCH, 1, KV_WIN, HEAD_DIM, SEQ)
            scale_tiles = tuple(range(4 * RET_W // LANES, (4 * RET_W + WIN_W) // LANES))
            proj, wk_t, wv_t = _proj(x_parts, mod[l], w_in_ab, i, scale_tiles, rope_tabs, rope_tiles,
                                     (kv_shape, kv_shape),
                                     {kv_tile: ("heads", 0, 0), kv_tile + 1: ("heads", 1, 0)})
            ro_c, wo_c, st_c = _ctx_ab(proj, ret_log_gamma[i], sink, lgf, lgb, gmat, gn_g, gn_b)
            s0 = state_ret[:, i]
            s0 = s0.reshape(DEC_BATCH, 2, H_RET // 2, 2, HEAD_DIM, HEAD_DIM)
            eye2 = jnp.eye(2, dtype=f32)
            s0_pairs = jnp.einsum('bdpeij,ef->bdpeifj', s0, eye2).reshape(
                DEC_BATCH, 2, H_RET // 2, PAIR_W, PAIR_W)
            ck = cache_win_k[:, i].reshape(DEC_BATCH, PAST_LEN, KV_W)
            cv = cache_win_v[:, i].reshape(DEC_BATCH, PAST_LEN, KV_W)
            ro_l, wo_l = _lat_ab(proj, ret_log_gamma[i], sink, ck, cv, s0_pairs, lgf, lgb, gmat, gn_g, gn_b)
            mix_a, mix_b, w_out = (ro_c, ro_l), (wo_c, wo_l), w_out_ab
            outs.setdefault('state', []).append(st_c[:, None])
            outs.setdefault('win_k', []).append(jnp.transpose(wk_t, (0, 1, 4, 2, 3)))
            outs.setdefault('win_v', []).append(jnp.transpose(wv_t, (0, 1, 4, 2, 3)))
        else:
            lam_init = 0.8 - 0.6 * math.exp(-0.3 * l)
            subln = diff_subln_g[i][None, :]
            rope_tiles = tuple(range(0, 2 * DIFF_W // LANES))
            plan = {}
            for h in range(H_DIFF):
                plan[DIFF_W // LANES + h] = ("pairs", 0, h)
                plan[2 * DIFF_W // LANES + h] = ("plain", 1, h)
            scale_tiles = tuple(range(0, DIFF_W // LANES))
            proj, dk_t, dv_h = _proj(
                x_parts, mod[l], w_in_cd, i, scale_tiles, rope_tabs, rope_tiles,
                ((BATCH, 1, H_DIFF, 2, HEAD_DIM, SEQ), (BATCH, 1, H_DIFF, SEQ, 2 * HEAD_DIM)), plan)
            a_c, z_c = _ctx_cd(proj, diff_lambda[i], subln, dft_ctx[0], dft_ctx[1], bdc, bds, lam_init)
            ck = jnp.transpose(cache_diff_k[:, i], (0, 2, 3, 4, 1)).reshape(DEC_BATCH, H_DIFF, PAIR_W, PAST_LEN)
            cv = jnp.transpose(cache_diff_v[:, i], (0, 2, 1, 3))
            a_l, z_l = _lat_cd(proj, ck, cv, diff_lambda[i], subln, dft_lat[0], dft_lat[1], bdc, bds, lam_init)
            mix_a, mix_b, w_out = (a_c, a_l), (z_c, z_l), w_out_cd
            outs.setdefault('diff_k', []).append(jnp.transpose(dk_t, (0, 1, 5, 2, 3, 4)))
            outs.setdefault('diff_v', []).append(jnp.transpose(dv_h, (0, 1, 3, 2, 4)))
        x_parts = _post(x_parts, mix_a, mix_b, mod[l], ln_g, ln_b, w_out, w_gate, w_up, w_down, l, i,
                        split_out=(l == DEPTH - 1))

    y_prompt = x_parts[0].reshape(BATCH, SEQ, D_MODEL)
    y_sample = x_parts[1].reshape(DEC_BATCH, DEC_SEQ, D_MODEL)
    cat = lambda parts: parts[0] if len(parts) == 1 else jnp.concatenate(parts, 1)
    return (y_prompt, y_sample, cat(outs['state']), cat(outs['win_k']), cat(outs['win_v']),
            cat(outs['diff_k']), cat(outs['diff_v']))
```

```python
import functools
import math

import jax
import jax.numpy as jnp
import numpy as np
from jax import lax
from jax.experimental import pallas as pl
from jax.experimental.pallas import tpu as pltpu

D_MODEL = 1024
BATCH = 32
SEQ = 256
DEPTH = 2
DEC_BATCH = 2
DEC_SEQ = 1024
PAST_LEN = 512
GRID_W = 64
HEAD_DIM = 64
ROPE_BASE = 10000.0
H_RET = 8
H_WIN = 8
KV_WIN = 2
G_WIN = H_WIN // KV_WIN
WINDOW = 128
H_DIFF = 6
FNET_GROUPS = 4
FNET_DIM = 64
D_FF = 256 * math.ceil(8 * D_MODEL / 3 / 256)
RET_W = H_RET * HEAD_DIM
WIN_W = H_WIN * HEAD_DIM
KV_W = KV_WIN * HEAD_DIM
AB_IN = 4 * RET_W + WIN_W + 2 * KV_W
DIFF_W = H_DIFF * 2 * HEAD_DIM
FNET_W = FNET_GROUPS * FNET_DIM
CD_IN = 3 * DIFF_W + FNET_W
ALPHA = (2 * DEPTH) ** 0.25
LN_EPS = 1e-5
QK_SCALE = HEAD_DIM ** -0.5
LOG2_E = math.log2(math.e)

N_CTX = BATCH * SEQ
N_LAT = DEC_BATCH * DEC_SEQ
N_TOK = N_CTX + N_LAT

LANES = 128
PAIR_W = 2 * HEAD_DIM
TM = 512
CTX_BLOCKS = N_CTX // TM
TOK_BLOCKS = N_TOK // TM
ROW_GROUPS = 2
FFN_SKEW = 2
TQ = 256
CTX_SEQS = 2
CTX_SEQS_CD = 4
RET_GROUP = 4
WIN_GROUP = 4
WIN_GROUP_LAT = 4
DIFF_GROUP_LAT = 1
DIFF_GROUP = 3
N_CHUNK = 256
NEG_BIG = -1e30
VMEM_LIMIT = 56 * 1024 * 1024

f32 = jnp.float32
bf16 = jnp.bfloat16


def _params(n_axes):
    return pltpu.CompilerParams(dimension_semantics=("arbitrary",) * n_axes,
                                vmem_limit_bytes=VMEM_LIMIT)


def _dot(a, b):
    return jnp.dot(a, b, preferred_element_type=f32)


def _dot_nt(a, b):
    return lax.dot_general(a, b, (((1,), (1,)), ((), ())), preferred_element_type=f32)


def _ln(x):
    mu = jnp.mean(x, -1, keepdims=True)
    d = x - mu
    var = jnp.mean(d * d, -1, keepdims=True)
    return d * lax.rsqrt(var + LN_EPS)


def _silu(x):
    return x * jax.nn.sigmoid(x)


def _split_bf16(x):
    hi = x.astype(bf16)
    lo = (x - hi.astype(f32)).astype(bf16)
    return hi, lo


def _lane_half_mask(shape):
    return (lax.broadcasted_iota(jnp.int32, shape, len(shape) - 1) & HEAD_DIM) == 0


def _mod_kernel(c_ref, w_ref, b_ref, o_ref):
    layer = pl.program_id(0)
    a = _silu(c_ref[...])
    a_hi, a_lo = _split_bf16(a)
    w_hi, w_lo = _split_bf16(w_ref[0])
    acc = _dot(a_hi, w_hi) + _dot(a_lo, w_hi) + _dot(a_hi, w_lo)
    o_ref[0] = acc + b_ref[pl.ds(layer, 1), :]


def _modulation(cond, w_mod, b_mod):
    tn = 1536
    rows = cond.shape[0]
    return pl.pallas_call(
        _mod_kernel,
        grid=(DEPTH, 6 * D_MODEL // tn),
        in_specs=[pl.BlockSpec((rows, D_MODEL), lambda l, j: (0, 0)),
                  pl.BlockSpec((1, D_MODEL, tn), lambda l, j: (l, 0, j)),
                  pl.BlockSpec((DEPTH, tn), lambda l, j: (0, j))],
        out_specs=pl.BlockSpec((1, rows, tn), lambda l, j: (l, 0, j)),
        out_shape=jax.ShapeDtypeStruct((DEPTH, rows, 6 * D_MODEL), f32),
        compiler_params=_params(2),
        name="modulation",
    )(cond, w_mod, b_mod)


def _tok(i, n_w):
    return jnp.maximum(i - n_w, 0)


def _ctx_blk(t):
    return jnp.minimum(t, CTX_BLOCKS - 1)


def _lat_blk(t):
    return jnp.maximum(t - CTX_BLOCKS, 0)


def _mod_row(t):
    return jnp.where(t < CTX_BLOCKS, 0, 1 + _lat_blk(t) * TM // DEC_SEQ)


def _token_specs(parts, n_w):
    width = parts[0].shape[1]
    if len(parts) == 1:
        return [pl.BlockSpec((TM, width), lambda i: (_tok(i, n_w), 0))]
    return [pl.BlockSpec((TM, width), lambda i: (_ctx_blk(_tok(i, n_w)), 0)),
            pl.BlockSpec((TM, width), lambda i: (_lat_blk(_tok(i, n_w)), 0))]


def _pick(refs, is_ctx, rs):
    if len(refs) == 1:
        return refs[0][rs, :]
    return jnp.where(is_ctx, refs[0][rs, :], refs[1][rs, :])


def _rope_pair(y, cos, sin_signed):
    first_half = (lax.broadcasted_iota(jnp.int32, y.shape, 1) & (HEAD_DIM // 2)) == 0
    swapped = jnp.where(first_half, pltpu.roll(y, LANES - HEAD_DIM // 2, 1), pltpu.roll(y, HEAD_DIM // 2, 1))
    return y * cos + swapped * sin_signed


def _proj_kernel(*refs, n_x, n_cache, n_w, rope_tiles, scale_tiles, cache_plan):
    x_refs = refs[:n_x]
    mod_ref, w_ref, cos_ref, sin_ref, o_ref = refs[n_x:n_x + 5]
    cache_refs = refs[n_x + 5:n_x + 5 + n_cache]
    wbf_ref, u_ref = refs[n_x + 5 + n_cache:]
    i = pl.program_id(0)

    @pl.when(i < n_w)
    def _():
        wbf_ref[i] = w_ref[0].astype(bf16)

    def tokens(is_ctx):
        x_ref = x_refs[0] if is_ctx else x_refs[-1]
        shift = mod_ref[0, 0:1, :]
        scale = mod_ref[0, 1:2, :]
        groups = [slice(b * SEQ, (b + 1) * SEQ) for b in range(TM // SEQ)]
        for rs in groups:
            u_ref[rs, :] = (_ln(x_ref[rs, :]) * (1.0 + scale) + shift).astype(bf16)
        for c in range(n_w):
            y_all = _dot(u_ref[...], wbf_ref[c])
            for b, rs in enumerate(groups):
                y = y_all[rs, :]
                for t in range(N_CHUNK // LANES):
                    tile = c * (N_CHUNK // LANES) + t
                    piece = y[:, t * LANES:(t + 1) * LANES]
                    if tile in rope_tiles and not is_ctx:
                        piece = _rope_pair(piece, cos_ref[rs, :], sin_ref[rs, :])
                    if tile in scale_tiles:
                        piece = piece * (QK_SCALE * LOG2_E)
                    o_ref[rs, tile * LANES:(tile + 1) * LANES] = piece.astype(o_ref.dtype)
                    if tile in cache_plan and is_ctx:
                        kind, out_idx, slot = cache_plan[tile]
                        c_ref = cache_refs[out_idx]
                        if kind == "plain":
                            c_ref[b, 0, slot] = piece
                        else:
                            piece_t = piece.T
                            if kind == "heads":
                                c_ref[b, 0, 0] = piece_t[0:HEAD_DIM]
                                c_ref[b, 0, 1] = piece_t[HEAD_DIM:]
                            else:
                                c_ref[b, 0, slot, 0] = piece_t[0:HEAD_DIM]
                                c_ref[b, 0, slot, 1] = piece_t[HEAD_DIM:]

    t = i - n_w

    @pl.when(jnp.logical_and(t >= 0, t < CTX_BLOCKS))
    def _():
        tokens(True)

    @pl.when(t >= CTX_BLOCKS)
    def _():
        tokens(False)


def _proj(x_parts, mod, mod_layer, w_all, layer, scale_tiles, rope_tabs, rope_tiles, cache_shapes, cache_plan):
    n_out = w_all.shape[2]
    n_w = n_out // N_CHUNK
    nb = DEC_SEQ // TM
    tok = lambda i: _tok(i, n_w)
    in_specs = _token_specs(x_parts, n_w) + [
        pl.BlockSpec((None, 1, 6, D_MODEL), lambda i: (mod_layer, _mod_row(tok(i)), 0, 0)),
        pl.BlockSpec((1, D_MODEL, N_CHUNK), lambda i: (layer, 0, jnp.minimum(i, n_w - 1))),
        pl.BlockSpec((TM, LANES), lambda i: (_lat_blk(tok(i)) % nb, 0)),
        pl.BlockSpec((TM, LANES), lambda i: (_lat_blk(tok(i)) % nb, 0))]
    out_specs = [pl.BlockSpec((TM, n_out), lambda i: (tok(i), 0))]
    out_shape = [jax.ShapeDtypeStruct((N_TOK, n_out), bf16)]
    for shp in cache_shapes:
        blk = (TM // SEQ,) + tuple(shp[1:])
        out_specs.append(pl.BlockSpec(blk, lambda i, nd=len(shp): (_ctx_blk(tok(i)),) + (0,) * (nd - 1)))
        out_shape.append(jax.ShapeDtypeStruct(tuple(shp), f32))
    return pl.pallas_call(
        functools.partial(_proj_kernel, n_x=len(x_parts), n_cache=len(cache_shapes), n_w=n_w,
                          rope_tiles=frozenset(rope_tiles), scale_tiles=frozenset(scale_tiles),
                          cache_plan=dict(cache_plan)),
        grid=(n_w + TOK_BLOCKS,),
        in_specs=in_specs,
        out_specs=out_specs,
        out_shape=out_shape,
        scratch_shapes=[pltpu.VMEM((n_w, D_MODEL, N_CHUNK), bf16), pltpu.VMEM((TM, D_MODEL), bf16)],
        compiler_params=_params(1),
        name="proj",
    )(*x_parts, mod, w_all, *rope_tabs)


def _post_kernel(*refs, n_x, n_y, ka, kb, n_w):
    x_refs = refs[:n_x]
    (ac_ref, al_ref, bc_ref, bl_ref, mod_ref, lng_ref, lnb_ref,
     wo_ref, wg_ref, wu_ref, wd_ref) = refs[n_x:n_x + 11]
    y_refs = refs[n_x + 11:n_x + 11 + n_y]
    wo_s, wg_s, wu_s, wd_s, x1_ref, u_ref, h_ref, y_ref = refs[n_x + 11 + n_y:]
    n_wo = (ka + kb) // N_CHUNK
    i = pl.program_id(0)

    @pl.when(i < n_w)
    def _():
        wg_s[i] = wg_ref[0].astype(bf16)
        wu_s[i] = wu_ref[0].astype(bf16)
        wd_s[i] = wd_ref[0].astype(bf16)

    @pl.when(i < n_wo)
    def _():
        wo_s[i] = wo_ref[0].astype(bf16)

    @pl.when(i >= n_w)
    def _():
        is_ctx = (i - n_w) < CTX_BLOCKS
        gate1 = mod_ref[0, 2:3, :]
        shift2 = mod_ref[0, 3:4, :]
        scale2 = mod_ref[0, 4:5, :]
        gate2 = mod_ref[0, 5:6, :]
        groups = [slice(r * TM // ROW_GROUPS, (r + 1) * TM // ROW_GROUPS) for r in range(ROW_GROUPS)]
        for rs in groups:
            a = _pick((ac_ref, al_ref), is_ctx, rs)
            b = _pick((bc_ref, bl_ref), is_ctx, rs)
            pieces = ([a[:, c:c + N_CHUNK] for c in range(0, ka, N_CHUNK)]
                      + [b[:, c:c + N_CHUNK] for c in range(0, kb, N_CHUNK)])
            h = functools.reduce(lambda s, p: s + p, [_dot(p, wo_s[c]) for c, p in enumerate(pieces)])
            x1 = _ln(ALPHA * _pick(x_refs, is_ctx, rs) + gate1 * h) * lng_ref[0, 0:1, :] + lnb_ref[0, 0:1, :]
            x1_ref[rs, :] = x1
            u_ref[rs, :] = (_ln(x1) * (1.0 + scale2) + shift2).astype(bf16)
        def ffn_chunk(rs, c):
            g = _dot(u_ref[rs, :], wg_s[c])
            up = _dot(u_ref[rs, :], wu_s[c])
            h_ref[rs, c * N_CHUNK:(c + 1) * N_CHUNK] = (_silu(g) * up).astype(bf16)

        def ffn_down(rs):
            ffn = functools.reduce(lambda s, p: s + p,
                                   [_dot(h_ref[rs, c * N_CHUNK:(c + 1) * N_CHUNK], wd_s[c]) for c in range(n_w)])
            y_ref[rs, :] = _ln(ALPHA * x1_ref[rs, :] + gate2 * ffn) * lng_ref[0, 1:2, :] + lnb_ref[0, 1:2, :]

        for c in range(n_w + FFN_SKEW * (ROW_GROUPS - 1)):
            for r, rs in enumerate(groups):
                cc = c - FFN_SKEW * r
                if 0 <= cc < n_w:
                    ffn_chunk(rs, cc)
                if cc == n_w - 1:
                    ffn_down(rs)
        if n_y == 1:
            y_refs[0][...] = y_ref[...]
        else:
            @pl.when(is_ctx)
            def _():
                y_refs[0][...] = y_ref[...]

            @pl.when(jnp.logical_not(is_ctx))
            def _():
                y_refs[1][...] = y_ref[...]


def _post(x_parts, mix_a, mix_b, mod, ln_g, ln_b, w_out, w_gate, w_up, w_down, layer, mix_layer, split_out):
    ka, kb = mix_a[0].shape[1], mix_b[0].shape[1]
    n_w = D_FF // N_CHUNK
    n_wo = (ka + kb) // N_CHUNK
    tok = lambda i: _tok(i, n_w)
    lay = lambda i: (layer, 0, 0)
    in_specs = (_token_specs(x_parts, n_w) + _token_specs(mix_a, n_w) + _token_specs(mix_b, n_w) + [
        pl.BlockSpec((None, 1, 6, D_MODEL), lambda i: (layer, _mod_row(tok(i)), 0, 0)),
        pl.BlockSpec((1, 2, D_MODEL), lay),
        pl.BlockSpec((1, 2, D_MODEL), lay),
        pl.BlockSpec((1, N_CHUNK, D_MODEL), lambda i: (mix_layer, jnp.minimum(i, n_wo - 1), 0)),
        pl.BlockSpec((1, D_MODEL, N_CHUNK), lambda i: (layer, 0, jnp.minimum(i, n_w - 1))),
        pl.BlockSpec((1, D_MODEL, N_CHUNK), lambda i: (layer, 0, jnp.minimum(i, n_w - 1))),
        pl.BlockSpec((1, N_CHUNK, D_MODEL), lambda i: (layer, jnp.minimum(i, n_w - 1), 0))])
    if split_out:
        out_specs = [pl.BlockSpec((TM, D_MODEL), lambda i: (_ctx_blk(tok(i)), 0)),
                     pl.BlockSpec((TM, D_MODEL), lambda i: (_lat_blk(tok(i)), 0))]
        out_shape = [jax.ShapeDtypeStruct((N_CTX, D_MODEL), f32), jax.ShapeDtypeStruct((N_LAT, D_MODEL), f32)]
    else:
        out_specs = [pl.BlockSpec((TM, D_MODEL), lambda i: (tok(i), 0))]
        out_shape = [jax.ShapeDtypeStruct((N_TOK, D_MODEL), f32)]
    return pl.pallas_call(
        functools.partial(_post_kernel, n_x=len(x_parts), n_y=len(out_shape), ka=ka, kb=kb, n_w=n_w),
        grid=(n_w + TOK_BLOCKS,),
        in_specs=in_specs,
        out_specs=out_specs,
        out_shape=out_shape,
        scratch_shapes=[pltpu.VMEM((n_wo, N_CHUNK, D_MODEL), bf16), pltpu.VMEM((n_w, D_MODEL, N_CHUNK), bf16),
                        pltpu.VMEM((n_w, D_MODEL, N_CHUNK), bf16), pltpu.VMEM((n_w, N_CHUNK, D_MODEL), bf16),
                        pltpu.VMEM((TM, D_MODEL), f32), pltpu.VMEM((TM, D_MODEL), bf16),
                        pltpu.VMEM((TM, D_FF), bf16), pltpu.VMEM((TM, D_MODEL), f32)],
        compiler_params=_params(1),
        name="post",
    )(*x_parts, *mix_a, *mix_b, mod, ln_g, ln_b, w_out, w_gate, w_up, w_down)


def _group_norm_gate(ro, rg, gmat, gn_g, gn_b):
    def gmean(parts):
        cols = []
        for c in range(0, RET_W, N_CHUNK):
            cols.append(sum(_dot(p[:, c:c + N_CHUNK], gmat) for p in parts))
        return jnp.concatenate(cols, -1)

    d = ro - gmean(_split_bf16(ro))
    var = gmean([(d * d).astype(bf16)])
    y = d * lax.rsqrt(var + LN_EPS) * gn_g + gn_b
    return _silu(rg.astype(f32)) * y


def _dup_head(x, j):
    first = _lane_half_mask(x.shape)
    keep = first if j == 0 else jnp.logical_not(first)
    xm = jnp.where(keep, x.astype(f32), 0.0)
    return xm + pltpu.roll(xm, HEAD_DIM, 1)


def _softmax_parts(scores, sink):
    m = sink
    for s in scores:
        m = jnp.maximum(m, jnp.max(s, -1, keepdims=True))
    es = [jnp.exp2(s - m) for s in scores]
    denom = jnp.exp2(sink - m)
    for e in es:
        denom = denom + jnp.sum(e, -1, keepdims=True)
    return es, denom


def _retention_tables(lg_ref, lgf_ref, lgb_ref, dmask_ref, kdec_ref, n):
    row = lax.broadcasted_iota(jnp.int32, (n, n), 0)
    col = lax.broadcasted_iota(jnp.int32, (n, n), 1)
    diff = (row - col).astype(f32)
    diag = jnp.where(row == col, 2.0 * QK_SCALE, QK_SCALE)
    for h in range(H_RET):
        dmask_ref[h] = jnp.exp(jnp.where(diff >= 0, lg_ref[0, h] * diff, -lg_ref[1, h] * diff)) * diag
    t = lax.broadcasted_iota(jnp.int32, (n, RET_W), 0).astype(f32)
    kdec_ref[0] = jnp.exp(lgf_ref[...] * (n - 1.0 - t)) * QK_SCALE
    kdec_ref[1] = jnp.exp(lgb_ref[...] * t) * QK_SCALE


def _retention_intra(pairs, q_of, k_of, v_of, dmask_ref):
    first = _lane_half_mask(k_of(pairs[0]).shape)
    masked = {}
    for p in pairs:
        kb = k_of(p)
        for e in range(2):
            keep = first if e == 0 else jnp.logical_not(first)
            s = _dot_nt(q_of(p), jnp.where(keep, kb, jnp.zeros_like(kb))) * dmask_ref[2 * p + e]
            masked[p, e] = s.astype(bf16)
    outs = {}
    for p in pairs:
        pv = [_dot(masked[p, e], v_of(p)) for e in range(2)]
        outs[p] = jnp.where(_lane_half_mask(pv[0].shape), pv[0], pv[1])
    return outs


def _window_group(subs, q_of, k_parts_of, v_parts_of, masks, sink_of):
    scores = {}
    for key in subs:
        parts = [_dot_nt(q_of(key), k) for k in k_parts_of(key)]
        scores[key] = [sc if mk is None else jnp.where(mk, sc, NEG_BIG) for sc, mk in zip(parts, masks)]
    probs = {}
    for key in subs:
        es, denom = _softmax_parts(scores[key], sink_of(key))
        probs[key] = ([ex.astype(bf16) for ex in es], denom)
    outs = {}
    for key in subs:
        es, denom = probs[key]
        pv = functools.reduce(lambda x, y: x + y, [_dot(ex, v) for ex, v in zip(es, v_parts_of(key))])
        outs[key] = pv / denom
    return outs


def _ctx_ab_kernel(lg_ref, sink_ref, rq_ref, rk_ref, rv_ref, rg_ref, wq_ref, wk_ref, wv_ref,
                   lgf_ref, lgb_ref, gmat_ref, gng_ref, gnb_ref,
                   ro_ref, wo_ref, st_ref, dmask_ref, kdec_ref, ret_ref):
    t_len = SEQ

    @pl.when(pl.program_id(0) == 0)
    def _():
        _retention_tables(lg_ref, lgf_ref, lgb_ref, dmask_ref, kdec_ref, t_len)

    first = _lane_half_mask((t_len, PAIR_W))
    for sq in range(CTX_SEQS):
        rows = slice(sq * t_len, (sq + 1) * t_len)
        psl = lambda p: slice(p * PAIR_W, (p + 1) * PAIR_W)
        for p0 in range(0, H_RET // 2, RET_GROUP):
            pairs = list(range(p0, p0 + RET_GROUP))
            intra = _retention_intra(pairs, lambda p: rq_ref[rows, psl(p)], lambda p: rk_ref[rows, psl(p)],
                                     lambda p: rv_ref[rows, psl(p)], dmask_ref)
            for p in pairs:
                ret_ref[rows, psl(p)] = intra[p]
        for p in range(H_RET // 2):
            sl = psl(p)
            kb = rk_ref[rows, sl]
            v = rv_ref[rows, sl]
            for d in range(2):
                kd_t = (kb * kdec_ref[d, :, sl]).T.astype(bf16)
                st = _dot(kd_t, v)
                st_ref[sq, d, 2 * p] = st[0:HEAD_DIM, 0:HEAD_DIM]
                st_ref[sq, d, 2 * p + 1] = pltpu.roll(st[HEAD_DIM:, :], HEAD_DIM, 1)[:, 0:HEAD_DIM]
        ro_ref[rows, :] = _group_norm_gate(ret_ref[rows, :], rg_ref[rows, :], gmat_ref[...], gng_ref[...],
                                           gnb_ref[...]).astype(bf16)

        k_dup = [_dup_head(wk_ref[rows, :], j).astype(bf16) for j in range(KV_WIN)]
        v_dup = [_dup_head(wv_ref[rows, :], j).astype(bf16) for j in range(KV_WIN)]

        def q_masked(key):
            qp, e = key
            qb = wq_ref[rows, qp * PAIR_W:(qp + 1) * PAIR_W]
            return jnp.where(first if e == 0 else jnp.logical_not(first), qb, jnp.zeros_like(qb))

        kv_of = lambda key: key[0] * 2 // G_WIN
        for g0 in range(0, H_WIN // 2, WIN_GROUP):
            subs = [(qp, e) for qp in range(g0, g0 + WIN_GROUP) for e in range(2)]
            outs = _window_group(subs, q_masked, lambda key: [k_dup[kv_of(key)]], lambda key: [v_dup[kv_of(key)]],
                                 [None], lambda key: sink_ref[0, 2 * key[0] + key[1]] * LOG2_E)
            for qp in range(g0, g0 + WIN_GROUP):
                wo_ref[rows, qp * PAIR_W:(qp + 1) * PAIR_W] = jnp.where(first, outs[qp, 0], outs[qp, 1]).astype(bf16)


def _ctx_ab(proj, log_gamma, sink, lgf_lanes, lgb_lanes, gmat, gn_g, gn_b):
    t = SEQ
    tb = CTX_SEQS * t
    smem = pl.BlockSpec(memory_space=pltpu.SMEM)
    const = lambda b: (0, 0)
    col = lambda c: (lambda b: (b, c))
    return pl.pallas_call(
        _ctx_ab_kernel,
        grid=(BATCH // CTX_SEQS,),
        in_specs=[smem, smem,
                  pl.BlockSpec((tb, RET_W), col(0)), pl.BlockSpec((tb, RET_W), col(1)),
                  pl.BlockSpec((tb, RET_W), col(2)), pl.BlockSpec((tb, RET_W), col(3)),
                  pl.BlockSpec((tb, WIN_W), col(4)),
                  pl.BlockSpec((tb, KV_W), col((4 * RET_W + WIN_W) // KV_W)),
                  pl.BlockSpec((tb, KV_W), col((4 * RET_W + WIN_W) // KV_W + 1)),
                  pl.BlockSpec((1, RET_W), const), pl.BlockSpec((1, RET_W), const),
                  pl.BlockSpec((N_CHUNK, N_CHUNK), const),
                  pl.BlockSpec((1, RET_W), const), pl.BlockSpec((1, RET_W), const)],
        out_specs=[pl.BlockSpec((tb, RET_W), lambda b: (b, 0)),
                   pl.BlockSpec((tb, WIN_W), lambda b: (b, 0)),
                   pl.BlockSpec((CTX_SEQS, 2, H_RET, HEAD_DIM, HEAD_DIM), lambda b: (b, 0, 0, 0, 0))],
        out_shape=[jax.ShapeDtypeStruct((BATCH * t, RET_W), bf16),
                   jax.ShapeDtypeStruct((BATCH * t, WIN_W), bf16),
                   jax.ShapeDtypeStruct((BATCH, 2, H_RET, HEAD_DIM, HEAD_DIM), f32)],
        scratch_shapes=[pltpu.VMEM((H_RET, t, t), f32), pltpu.VMEM((2, t, RET_W), f32),
                        pltpu.VMEM((tb, RET_W), f32)],
        compiler_params=_params(1),
        name="ctx_ab",
    )(log_gamma, sink, proj, proj, proj, proj, proj, proj, proj, lgf_lanes, lgb_lanes, gmat, gn_g, gn_b)


def _pair_state(s0_ref, d, p):
    zero = jnp.zeros((HEAD_DIM, HEAD_DIM), f32)
    top = jnp.concatenate([s0_ref[0, 0, d, 2 * p], zero], 1)
    bottom = jnp.concatenate([zero, s0_ref[0, 0, d, 2 * p + 1]], 1)
    return jnp.concatenate([top, bottom], 0)


def _lat_ab_kernel(lg_ref, sink_ref, rq_ref, rk_ref, rv_ref, rg_ref, wq_ref, wk_ref, wv_ref, ck_ref, cv_ref,
                   s0_ref, lgf_ref, lgb_ref, gmat_ref, gng_ref, gnb_ref,
                   ro_ref, wo_ref, ret_ref, dmask_ref, kdec_ref, qdec_ref, sf_ref, sb_ref):
    t_len = DEC_SEQ
    n_chunks = t_len // TQ
    chunk = pl.program_id(1)
    q0 = pl.multiple_of(chunk * TQ, TQ)
    first = _lane_half_mask((TQ, PAIR_W))

    @pl.when(jnp.logical_and(pl.program_id(0) == 0, chunk == 0))
    def _():
        _retention_tables(lg_ref, lgf_ref, lgb_ref, dmask_ref, kdec_ref, TQ)
        t = lax.broadcasted_iota(jnp.int32, (TQ, RET_W), 0).astype(f32)
        qdec_ref[0] = jnp.exp(lgf_ref[...] * (t + 1.0))
        qdec_ref[1] = jnp.exp(lgb_ref[...] * (TQ - t))

    @pl.when(chunk == 0)
    def _():
        r = lax.broadcasted_iota(jnp.int32, (PAIR_W, PAIR_W), 0)
        c_ = lax.broadcasted_iota(jnp.int32, (PAIR_W, PAIR_W), 1)
        same_head = (r < HEAD_DIM) == (c_ < HEAD_DIM)
        for p in range(H_RET // 2):
            sl = slice(p * PAIR_W, (p + 1) * PAIR_W)
            kv = []
            for c in range(n_chunks):
                rows = slice(c * TQ, (c + 1) * TQ)
                kc = rk_ref[rows, sl]
                vc = rv_ref[rows, sl]
                kv.append([jnp.where(same_head, _dot((kc * kdec_ref[d, :, sl]).T.astype(bf16), vc), 0.0)
                           for d in range(2)])
            state = _pair_state(s0_ref, 0, p)
            for c in range(n_chunks):
                sf_ref[c, p] = state
                state = state * jnp.exp(lgf_ref[:, sl] * TQ) + kv[c][0]
            state = _pair_state(s0_ref, 1, p)
            for c in reversed(range(n_chunks)):
                sb_ref[c, p] = state
                state = state * jnp.exp(lgb_ref[:, sl] * TQ) + kv[c][1]

    psl = lambda p: slice(p * PAIR_W, (p + 1) * PAIR_W)
    intra = {}
    for p0 in range(0, H_RET // 2, RET_GROUP):
        intra.update(_retention_intra(list(range(p0, p0 + RET_GROUP)), lambda p: rq_ref[:, psl(p)],
                                      lambda p: rk_ref[pl.ds(q0, TQ), psl(p)],
                                      lambda p: rv_ref[pl.ds(q0, TQ), psl(p)], dmask_ref))
    for p in range(H_RET // 2):
        sl = psl(p)
        q = rq_ref[:, sl]
        o = intra[p]
        o = o + _dot(q, sf_ref[chunk, p].astype(bf16)) * qdec_ref[0, :, sl]
        o = o + _dot(q, sb_ref[chunk, p].astype(bf16)) * qdec_ref[1, :, sl]
        ret_ref[:, sl] = o
    ro_ref[...] = _group_norm_gate(ret_ref[...], rg_ref[...], gmat_ref[...], gng_ref[...], gnb_ref[...]).astype(bf16)

    band = TQ + 2 * WINDOW
    k_start = pl.multiple_of(jnp.clip(q0 - WINDOW, 0, t_len - band), LANES)
    qi = q0 + lax.broadcasted_iota(jnp.int32, (TQ, band), 0)
    kj = k_start + lax.broadcasted_iota(jnp.int32, (TQ, band), 1)
    in_band = jnp.abs(qi - kj) <= WINDOW
    k_parts = [[_dup_head(wk_ref[pl.ds(k_start, band), :], j).astype(bf16), _dup_head(ck_ref[0], j).astype(bf16)]
               for j in range(KV_WIN)]
    v_parts = [[_dup_head(wv_ref[pl.ds(k_start, band), :], j).astype(bf16), _dup_head(cv_ref[0], j).astype(bf16)]
               for j in range(KV_WIN)]

    def q_masked(key):
        qp, e = key
        qb = wq_ref[:, qp * PAIR_W:(qp + 1) * PAIR_W]
        return jnp.where(first if e == 0 else jnp.logical_not(first), qb, jnp.zeros_like(qb))

    kv_of = lambda key: key[0] * 2 // G_WIN
    for g0 in range(0, H_WIN // 2, WIN_GROUP_LAT):
        subs = [(qp, e) for qp in range(g0, g0 + WIN_GROUP_LAT) for e in range(2)]
        outs = _window_group(subs, q_masked, lambda key: k_parts[kv_of(key)], lambda key: v_parts[kv_of(key)],
                             [in_band, None], lambda key: sink_ref[0, 2 * key[0] + key[1]] * LOG2_E)
        for qp in range(g0, g0 + WIN_GROUP_LAT):
            wo_ref[:, qp * PAIR_W:(qp + 1) * PAIR_W] = jnp.where(first, outs[qp, 0], outs[qp, 1]).astype(bf16)


def _lat_ab(proj, log_gamma, sink, ck, cv, state, layer, lgf_lanes, lgb_lanes, gmat, gn_g, gn_b):
    t = DEC_SEQ
    nq = t // TQ
    smem = pl.BlockSpec(memory_space=pltpu.SMEM)
    const = lambda b, i: (0, 0)
    qcol = lambda c: (lambda b, i: (N_CTX // TQ + b * nq + i, c))
    bcol = lambda c: (lambda b, i: (N_CTX // t + b, c))
    kv_col = (4 * RET_W + WIN_W) // KV_W
    return pl.pallas_call(
        _lat_ab_kernel,
        grid=(DEC_BATCH, nq),
        in_specs=[smem, smem,
                  pl.BlockSpec((TQ, RET_W), qcol(0)), pl.BlockSpec((t, RET_W), bcol(1)),
                  pl.BlockSpec((t, RET_W), bcol(2)), pl.BlockSpec((TQ, RET_W), qcol(3)),
                  pl.BlockSpec((TQ, WIN_W), qcol(4)),
                  pl.BlockSpec((t, KV_W), bcol(kv_col)), pl.BlockSpec((t, KV_W), bcol(kv_col + 1)),
                  pl.BlockSpec((1, PAST_LEN, KV_W), lambda b, i: (b, 0, 0)),
                  pl.BlockSpec((1, PAST_LEN, KV_W), lambda b, i: (b, 0, 0)),
                  pl.BlockSpec((1, 1, 2, H_RET, HEAD_DIM, HEAD_DIM), lambda b, i: (b, layer, 0, 0, 0, 0)),
                  pl.BlockSpec((1, RET_W), const), pl.BlockSpec((1, RET_W), const),
                  pl.BlockSpec((N_CHUNK, N_CHUNK), const),
                  pl.BlockSpec((1, RET_W), const), pl.BlockSpec((1, RET_W), const)],
        out_specs=[pl.BlockSpec((TQ, RET_W), lambda b, i: (b * nq + i, 0)),
                   pl.BlockSpec((TQ, WIN_W), lambda b, i: (b * nq + i, 0))],
        out_shape=[jax.ShapeDtypeStruct((DEC_BATCH * t, RET_W), bf16),
                   jax.ShapeDtypeStruct((DEC_BATCH * t, WIN_W), bf16)],
        scratch_shapes=[pltpu.VMEM((TQ, RET_W), f32), pltpu.VMEM((H_RET, TQ, TQ), f32),
                        pltpu.VMEM((2, TQ, RET_W), f32), pltpu.VMEM((2, TQ, RET_W), f32),
                        pltpu.VMEM((t // TQ, H_RET // 2, PAIR_W, PAIR_W), f32),
                        pltpu.VMEM((t // TQ, H_RET // 2, PAIR_W, PAIR_W), f32)],
        compiler_params=_params(2),
        name="lat_ab",
    )(log_gamma, sink, proj, proj, proj, proj, proj, proj, proj, ck, cv, state,
      lgf_lanes, lgb_lanes, gmat, gn_g, gn_b)


def _lambda_full(lam_ref, lam_init):
    lam = lam_ref[...]
    a = jnp.sum(lam[0:1, :] * lam[1:2, :], -1, keepdims=True)
    b = jnp.sum(lam[2:3, :] * lam[3:4, :], -1, keepdims=True)
    return jnp.exp(a) - jnp.exp(b) + lam_init


def _diff_heads(q_of, k_parts_of, v_parts_of, lam, subln, lam_init, group):
    res = []
    for h0 in range(0, H_DIFF, group):
        res += _diff_head_group(range(h0, h0 + group), q_of, k_parts_of, v_parts_of, lam, subln, lam_init)
    return res


def _diff_head_group(heads, q_of, k_parts_of, v_parts_of, lam, subln, lam_init):
    subs = [(h, e) for h in heads for e in range(2)]
    scores = {}
    for h, e in subs:
        q = q_of(h)
        fm = _lane_half_mask(q.shape)
        q_sub = jnp.where(fm if e == 0 else jnp.logical_not(fm), q, jnp.zeros_like(q))
        scores[h, e] = [_dot(q_sub, k) if transposed else _dot_nt(q_sub, k) for k, transposed in k_parts_of(h)]
    probs = {}
    for key in subs:
        m = scores[key][0].max(-1, keepdims=True)
        for sc in scores[key][1:]:
            m = jnp.maximum(m, sc.max(-1, keepdims=True))
        es = [jnp.exp2(sc - m) for sc in scores[key]]
        denom = es[0].sum(-1, keepdims=True)
        for ex in es[1:]:
            denom = denom + ex.sum(-1, keepdims=True)
        probs[key] = ([ex.astype(bf16) for ex in es], denom)
    outs = {}
    for h, e in subs:
        es, denom = probs[h, e]
        pv = functools.reduce(lambda x, y: x + y, [_dot(ex, v) for v, ex in zip(v_parts_of(h), es)])
        outs[h, e] = pv / denom
    res = []
    for h in heads:
        a = outs[h, 0] - lam * outs[h, 1]
        res.append(a * lax.rsqrt(jnp.mean(a * a, -1, keepdims=True) + LN_EPS) * subln * (1.0 - lam_init))
    return res


def _fourier_rows(ct_ref, st_ref, z, bdc_ref, bds_ref):
    zb = z
    zc = _dot(zb, bdc_ref[...].astype(bf16)).astype(bf16)
    zs = _dot(zb, bds_ref[...].astype(bf16)).astype(bf16)
    return _dot(ct_ref[...].astype(bf16), zc) - _dot(st_ref[...].astype(bf16), zs)


def _ctx_cd_kernel(q_ref, k_ref, v_ref, z_ref, lam_ref, subln_ref, ct_ref, st_ref, bdc_ref, bds_ref,
                   a_ref, zf_ref, *, lam_init):
    lam = _lambda_full(lam_ref, lam_init)
    for sq in range(CTX_SEQS_CD):
        rows = slice(sq * SEQ, (sq + 1) * SEQ)
        sl = lambda h: slice(h * PAIR_W, (h + 1) * PAIR_W)
        heads = _diff_heads(lambda h: q_ref[rows, sl(h)], lambda h: [(k_ref[rows, sl(h)], False)],
                            lambda h: [v_ref[rows, sl(h)]], lam, subln_ref[...], lam_init, DIFF_GROUP)
        for h in range(H_DIFF):
            a_ref[rows, sl(h)] = heads[h].astype(bf16)
        zf_ref[rows, :] = _fourier_rows(ct_ref, st_ref, z_ref[rows, :], bdc_ref, bds_ref).astype(bf16)


def _ctx_cd(proj, lam, subln, ct, st, bdc, bds, lam_init):
    t = SEQ
    tb = CTX_SEQS_CD * t
    const = lambda b: (0, 0)
    col = lambda c: (lambda b: (b, c))
    return pl.pallas_call(
        functools.partial(_ctx_cd_kernel, lam_init=lam_init),
        grid=(BATCH // CTX_SEQS_CD,),
        in_specs=[pl.BlockSpec((tb, DIFF_W), col(0)), pl.BlockSpec((tb, DIFF_W), col(1)),
                  pl.BlockSpec((tb, DIFF_W), col(2)), pl.BlockSpec((tb, FNET_W), col(3 * DIFF_W // FNET_W)),
                  pl.BlockSpec((4, HEAD_DIM), const), pl.BlockSpec((1, PAIR_W), const),
                  pl.BlockSpec((t, t), const), pl.BlockSpec((t, t), const),
                  pl.BlockSpec((FNET_W, FNET_W), const), pl.BlockSpec((FNET_W, FNET_W), const)],
        out_specs=[pl.BlockSpec((tb, DIFF_W), lambda b: (b, 0)), pl.BlockSpec((tb, FNET_W), lambda b: (b, 0))],
        out_shape=[jax.ShapeDtypeStruct((BATCH * t, DIFF_W), bf16),
                   jax.ShapeDtypeStruct((BATCH * t, FNET_W), bf16)],
        compiler_params=_params(1),
        name="ctx_cd",
    )(proj, proj, proj, proj, lam, subln, ct, st, bdc, bds)


def _lat_cd_kernel(q_ref, k_ref, v_ref, z_ref, ckt_ref, cv_ref, lam_ref, subln_ref, ct_ref, st_ref, bdc_ref, bds_ref,
                   a_ref, zf_ref, *, lam_init):
    lam = _lambda_full(lam_ref, lam_init)
    sl = lambda h: slice(h * PAIR_W, (h + 1) * PAIR_W)
    heads = _diff_heads(lambda h: q_ref[:, sl(h)],
                        lambda h: [(k_ref[:, sl(h)], False), (ckt_ref[0, h].astype(bf16), True)],
                        lambda h: [v_ref[:, sl(h)], cv_ref[0, h].astype(bf16)], lam, subln_ref[...], lam_init,
                        DIFF_GROUP_LAT)
    for h in range(H_DIFF):
        a_ref[:, sl(h)] = heads[h].astype(bf16)
    zf_ref[...] = _fourier_rows(ct_ref, st_ref, z_ref[...], bdc_ref, bds_ref).astype(bf16)


def _lat_cd(proj, ck, cv, lam, subln, ct, st, bdc, bds, lam_init):
    t = DEC_SEQ
    nq = t // TQ
    const = lambda b, i: (0, 0)
    return pl.pallas_call(
        functools.partial(_lat_cd_kernel, lam_init=lam_init),
        grid=(DEC_BATCH, nq),
        in_specs=[pl.BlockSpec((TQ, DIFF_W), lambda b, i: (N_CTX // TQ + b * nq + i, 0)),
                  pl.BlockSpec((t, DIFF_W), lambda b, i: (N_CTX // t + b, 1)),
                  pl.BlockSpec((t, DIFF_W), lambda b, i: (N_CTX // t + b, 2)),
                  pl.BlockSpec((t, FNET_W), lambda b, i: (N_CTX // t + b, 3 * DIFF_W // FNET_W)),
                  pl.BlockSpec((1, H_DIFF, PAIR_W, PAST_LEN), lambda b, i: (b, 0, 0, 0)),
                  pl.BlockSpec((1, H_DIFF, PAST_LEN, PAIR_W), lambda b, i: (b, 0, 0, 0)),
                  pl.BlockSpec((4, HEAD_DIM), const), pl.BlockSpec((1, PAIR_W), const),
                  pl.BlockSpec((TQ, t), lambda b, i: (i, 0)), pl.BlockSpec((TQ, t), lambda b, i: (i, 0)),
                  pl.BlockSpec((FNET_W, FNET_W), const), pl.BlockSpec((FNET_W, FNET_W), const)],
        out_specs=[pl.BlockSpec((TQ, DIFF_W), lambda b, i: (b * nq + i, 0)),
                   pl.BlockSpec((TQ, FNET_W), lambda b, i: (b * nq + i, 0))],
        out_shape=[jax.ShapeDtypeStruct((DEC_BATCH * t, DIFF_W), bf16),
                   jax.ShapeDtypeStruct((DEC_BATCH * t, FNET_W), bf16)],
        compiler_params=_params(2),
        name="lat_cd",
    )(proj, proj, proj, proj, ck, cv, lam, subln, ct, st, bdc, bds)


def _rope_tables():
    t = np.arange(DEC_SEQ)
    quarter = HEAD_DIM // 4
    inv = ROPE_BASE ** (-np.arange(quarter, dtype=np.float64) / quarter)
    ang = np.concatenate([(t // GRID_W)[:, None] * inv, (t % GRID_W)[:, None] * inv], -1)
    cos, sin = np.cos(ang), np.sin(ang)
    reps = LANES // HEAD_DIM
    return (np.tile(np.concatenate([cos, cos], -1), (1, reps)).astype(np.float32),
            np.tile(np.concatenate([-sin, sin], -1), (1, reps)).astype(np.float32))


def _dft_tables(n):
    k = np.arange(n)
    ang = (2.0 * math.pi / n) * ((k[:, None] * k[None, :]) % n)
    return (np.cos(ang) / math.sqrt(n)).astype(np.float32), (np.sin(ang) / math.sqrt(n)).astype(np.float32)


def _block_diag(m, reps):
    return np.kron(np.eye(reps, dtype=m.dtype), m)


def kernel(x_prompt, x_sample, state_ret, cache_win_k, cache_win_v, cache_diff_k, cache_diff_v, c, c_ctx, w_mod, b_mod, ln_g, ln_b, w_in_ab, w_out_ab, ret_log_gamma, ret_gn_g, ret_gn_b, win_sink, w_in_cd, w_out_cd, diff_lambda, diff_subln_g, w_gate, w_up, w_down):
    cond = jnp.concatenate([c_ctx[None, :], c, jnp.zeros((8 - 1 - DEC_BATCH, D_MODEL), f32)], 0)
    mod = _modulation(cond, w_mod, b_mod).reshape(DEPTH, 8, 6, D_MODEL)

    rope_tabs = _rope_tables()
    gmat = jnp.asarray(_block_diag(np.full((HEAD_DIM, HEAD_DIM), 1.0 / HEAD_DIM, np.float32),
                                   N_CHUNK // HEAD_DIM), bf16)
    c64, s64 = _dft_tables(FNET_DIM)
    bdc = _block_diag(c64, FNET_GROUPS)
    bds = _block_diag(s64, FNET_GROUPS)
    dft_ctx = _dft_tables(SEQ)
    dft_lat = _dft_tables(DEC_SEQ)

    x_parts = [x_prompt.reshape(N_CTX, D_MODEL), x_sample.reshape(N_LAT, D_MODEL)]
    outs = {}
    for l in range(DEPTH):
        i = l // 2
        if l % 2 == 0:
            lgf = jnp.repeat(ret_log_gamma[i, 0], HEAD_DIM)[None, :]
            lgb = jnp.repeat(ret_log_gamma[i, 1], HEAD_DIM)[None, :]
            gn_g = ret_gn_g[i][None, :]
            gn_b = ret_gn_b[i][None, :]
            sink = win_sink[i][None, :]
            rope_tiles = tuple(range(0, 2 * RET_W // LANES)) + tuple(
                range(4 * RET_W // LANES, (4 * RET_W + WIN_W + KV_W) // LANES))
            kv_tile = (4 * RET_W + WIN_W) // LANES
            kv_shape = (BATCH, 1, KV_WIN, HEAD_DIM, SEQ)
            scale_tiles = tuple(range(4 * RET_W // LANES, (4 * RET_W + WIN_W) // LANES))
            proj, wk_t, wv_t = _proj(x_parts, mod, l, w_in_ab, i, scale_tiles, rope_tabs, rope_tiles,
                                     (kv_shape, kv_shape),
                                     {kv_tile: ("heads", 0, 0), kv_tile + 1: ("heads", 1, 0)})
            ro_c, wo_c, st_c = _ctx_ab(proj, ret_log_gamma[i], sink, lgf, lgb, gmat, gn_g, gn_b)
            ck = cache_win_k[:, i].reshape(DEC_BATCH, PAST_LEN, KV_W)
            cv = cache_win_v[:, i].reshape(DEC_BATCH, PAST_LEN, KV_W)
            ro_l, wo_l = _lat_ab(proj, ret_log_gamma[i], sink, ck, cv, state_ret, i, lgf, lgb, gmat, gn_g, gn_b)
            mix_a, mix_b, w_out = (ro_c, ro_l), (wo_c, wo_l), w_out_ab
            outs.setdefault('state', []).append(st_c[:, None])
            outs.setdefault('win_k', []).append(jnp.transpose(wk_t, (0, 1, 4, 2, 3)))
            outs.setdefault('win_v', []).append(jnp.transpose(wv_t, (0, 1, 4, 2, 3)))
        else:
            lam_init = 0.8 - 0.6 * math.exp(-0.3 * l)
            subln = diff_subln_g[i][None, :]
            rope_tiles = tuple(range(0, 2 * DIFF_W // LANES))
            plan = {}
            for h in range(H_DIFF):
                plan[DIFF_W // LANES + h] = ("pairs", 0, h)
                plan[2 * DIFF_W // LANES + h] = ("plain", 1, h)
            scale_tiles = tuple(range(0, DIFF_W // LANES))
            proj, dk_t, dv_h = _proj(
                x_parts, mod, l, w_in_cd, i, scale_tiles, rope_tabs, rope_tiles,
                ((BATCH, 1, H_DIFF, 2, HEAD_DIM, SEQ), (BATCH, 1, H_DIFF, SEQ, 2 * HEAD_DIM)), plan)
            a_c, z_c = _ctx_cd(proj, diff_lambda[i], subln, dft_ctx[0], dft_ctx[1], bdc, bds, lam_init)
            ck = jnp.transpose(cache_diff_k[:, i], (0, 2, 3, 4, 1)).reshape(DEC_BATCH, H_DIFF, PAIR_W, PAST_LEN)
            cv = jnp.transpose(cache_diff_v[:, i], (0, 2, 1, 3))
            a_l, z_l = _lat_cd(proj, ck, cv, diff_lambda[i], subln, dft_lat[0], dft_lat[1], bdc, bds, lam_init)
            mix_a, mix_b, w_out = (a_c, a_l), (z_c, z_l), w_out_cd
            outs.setdefault('diff_k', []).append(jnp.transpose(dk_t, (0, 1, 5, 2, 3, 4)))
            outs.setdefault('diff_v', []).append(jnp.transpose(dv_h, (0, 1, 3, 2, 4)))
        x_parts = _post(x_parts, mix_a, mix_b, mod, ln_g, ln_b, w_out, w_gate, w_up, w_down, l, i,
                        split_out=(l == DEPTH - 1))

    y_prompt = x_parts[0].reshape(BATCH, SEQ, D_MODEL)
    y_sample = x_parts[1].reshape(DEC_BATCH, DEC_SEQ, D_MODEL)
    cat = lambda parts: parts[0] if len(parts) == 1 else jnp.concatenate(parts, 1)
    return (y_prompt, y_sample, cat(outs['state']), cat(outs['win_k']), cat(outs['win_v']),
            cat(outs['diff_k']), cat(outs['diff_v']))
```

```python
import functools
import math

import jax
import jax.numpy as jnp
import numpy as np
from jax import lax
from jax.experimental import pallas as pl
from jax.experimental.pallas import tpu as pltpu

D_MODEL = 1024
BATCH = 32
SEQ = 256
DEPTH = 2
DEC_BATCH = 2
DEC_SEQ = 1024
PAST_LEN = 512
GRID_W = 64
HEAD_DIM = 64
ROPE_BASE = 10000.0
H_RET = 8
H_WIN = 8
KV_WIN = 2
G_WIN = H_WIN // KV_WIN
WINDOW = 128
H_DIFF = 6
FNET_GROUPS = 4
FNET_DIM = 64
D_FF = 256 * math.ceil(8 * D_MODEL / 3 / 256)
RET_W = H_RET * HEAD_DIM
WIN_W = H_WIN * HEAD_DIM
KV_W = KV_WIN * HEAD_DIM
AB_IN = 4 * RET_W + WIN_W + 2 * KV_W
DIFF_W = H_DIFF * 2 * HEAD_DIM
FNET_W = FNET_GROUPS * FNET_DIM
CD_IN = 3 * DIFF_W + FNET_W
ALPHA = (2 * DEPTH) ** 0.25
LN_EPS = 1e-5
QK_SCALE = HEAD_DIM ** -0.5
LOG2_E = math.log2(math.e)

N_CTX = BATCH * SEQ
N_LAT = DEC_BATCH * DEC_SEQ
N_TOK = N_CTX + N_LAT

LANES = 128
SUBLANES = 8
PAIR_W = 2 * HEAD_DIM
TM = 512
CTX_BLOCKS = N_CTX // TM
TOK_BLOCKS = N_TOK // TM
ROW_GROUPS = 2
FFN_SKEW = 2
TQ = 256
TQ_CD = 512
CTX_SEQS = 2
CTX_SEQS_CD = 4
RET_GROUP = 4
WIN_GROUP = 4
WIN_GROUP_LAT = 4
DIFF_GROUP_LAT = 1
DIFF_GROUP = 3
N_CHUNK = 256
MOD_TN = 1536
NEG_BIG = -1e30
VMEM_LIMIT = 56 * 1024 * 1024

f32 = jnp.float32
bf16 = jnp.bfloat16


def _params(n_axes):
    return pltpu.CompilerParams(dimension_semantics=("arbitrary",) * n_axes,
                                vmem_limit_bytes=VMEM_LIMIT)


def _dot(a, b):
    return jnp.dot(a, b, preferred_element_type=f32)


def _dot_nt(a, b):
    return lax.dot_general(a, b, (((1,), (1,)), ((), ())), preferred_element_type=f32)


def _ln(x):
    mu = jnp.mean(x, -1, keepdims=True)
    d = x - mu
    var = jnp.mean(d * d, -1, keepdims=True)
    return d * lax.rsqrt(var + LN_EPS)


def _silu(x):
    return x * jax.nn.sigmoid(x)


def _split_bf16(x):
    hi = x.astype(bf16)
    lo = (x - hi.astype(f32)).astype(bf16)
    return hi, lo


def _lane_half_mask(shape):
    return (lax.broadcasted_iota(jnp.int32, shape, len(shape) - 1) & HEAD_DIM) == 0


def _mod_kernel(c_ref, w_ref, b_ref, o_ref):
    layer = pl.program_id(0)
    a = _silu(c_ref[...])
    a_hi, a_lo = _split_bf16(a)
    w_hi, w_lo = _split_bf16(w_ref[0])
    acc = _dot(a_hi, w_hi) + _dot(a_lo, w_hi) + _dot(a_hi, w_lo)
    o_ref[0] = acc + b_ref[pl.ds(layer, 1), :]


def _modulation(cond, w_mod, b_mod):
    tn = MOD_TN
    rows = cond.shape[0]
    return pl.pallas_call(
        _mod_kernel,
        grid=(DEPTH, 6 * D_MODEL // tn),
        in_specs=[pl.BlockSpec((rows, D_MODEL), lambda l, j: (0, 0)),
                  pl.BlockSpec((1, D_MODEL, tn), lambda l, j: (l, 0, j)),
                  pl.BlockSpec((DEPTH, tn), lambda l, j: (0, j))],
        out_specs=pl.BlockSpec((1, rows, tn), lambda l, j: (l, 0, j)),
        out_shape=jax.ShapeDtypeStruct((DEPTH, rows, 6 * D_MODEL), f32),
        compiler_params=_params(2),
        name="modulation",
    )(cond, w_mod, b_mod)


def _tok(i, n_w):
    return jnp.maximum(i - n_w, 0)


def _ctx_blk(t):
    return jnp.minimum(t, CTX_BLOCKS - 1)


def _lat_blk(t):
    return jnp.maximum(t - CTX_BLOCKS, 0)


def _mod_row(t):
    return jnp.where(t < CTX_BLOCKS, 0, 1 + _lat_blk(t) * TM // DEC_SEQ)


def _token_specs(parts, n_w):
    width = parts[0].shape[1]
    if len(parts) == 1:
        return [pl.BlockSpec((TM, width), lambda i: (_tok(i, n_w), 0))]
    return [pl.BlockSpec((TM, width), lambda i: (_ctx_blk(_tok(i, n_w)), 0)),
            pl.BlockSpec((TM, width), lambda i: (_lat_blk(_tok(i, n_w)), 0))]


def _pick(refs, is_ctx, rs):
    if len(refs) == 1:
        return refs[0][rs, :]
    return jnp.where(is_ctx, refs[0][rs, :], refs[1][rs, :])


def _rope_pair(y, cos, sin_signed):
    first_half = (lax.broadcasted_iota(jnp.int32, y.shape, 1) & (HEAD_DIM // 2)) == 0
    swapped = jnp.where(first_half, pltpu.roll(y, LANES - HEAD_DIM // 2, 1), pltpu.roll(y, HEAD_DIM // 2, 1))
    return y * cos + swapped * sin_signed


def _proj_kernel(*refs, n_x, n_cache, n_w, rope_tiles, scale_tiles, cache_plan):
    x_refs = refs[:n_x]
    mod_ref, w_ref, cos_ref, sin_ref, o_ref = refs[n_x:n_x + 5]
    cache_refs = refs[n_x + 5:n_x + 5 + n_cache]
    wbf_ref, u_ref = refs[n_x + 5 + n_cache:]
    i = pl.program_id(0)

    @pl.when(i < n_w)
    def _():
        wbf_ref[i] = w_ref[0].astype(bf16)

    def tokens(is_ctx):
        x_ref = x_refs[0] if is_ctx else x_refs[-1]
        shift = mod_ref[0, 0:1, :]
        scale = mod_ref[0, 1:2, :]
        groups = [slice(b * SEQ, (b + 1) * SEQ) for b in range(TM // SEQ)]
        for rs in groups:
            u_ref[rs, :] = (_ln(x_ref[rs, :]) * (1.0 + scale) + shift).astype(bf16)
        for c in range(n_w):
            y_all = _dot(u_ref[...], wbf_ref[c])
            for b, rs in enumerate(groups):
                y = y_all[rs, :]
                for t in range(N_CHUNK // LANES):
                    tile = c * (N_CHUNK // LANES) + t
                    piece = y[:, t * LANES:(t + 1) * LANES]
                    if tile in rope_tiles and not is_ctx:
                        piece = _rope_pair(piece, cos_ref[rs, :], sin_ref[rs, :])
                    if tile in scale_tiles:
                        piece = piece * (QK_SCALE * LOG2_E)
                    o_ref[rs, tile * LANES:(tile + 1) * LANES] = piece.astype(o_ref.dtype)
                    if tile in cache_plan and is_ctx:
                        kind, out_idx, slot = cache_plan[tile]
                        c_ref = cache_refs[out_idx]
                        if kind == "plain":
                            c_ref[b, 0, slot] = piece
                        else:
                            piece_t = piece.T
                            if kind == "heads":
                                c_ref[b, 0, 0] = piece_t[0:HEAD_DIM]
                                c_ref[b, 0, 1] = piece_t[HEAD_DIM:]
                            else:
                                c_ref[b, 0, slot, 0] = piece_t[0:HEAD_DIM]
                                c_ref[b, 0, slot, 1] = piece_t[HEAD_DIM:]

    t = i - n_w

    @pl.when(jnp.logical_and(t >= 0, t < CTX_BLOCKS))
    def _():
        tokens(True)

    @pl.when(t >= CTX_BLOCKS)
    def _():
        tokens(False)


def _proj(x_parts, mod, mod_layer, w_all, layer, scale_tiles, rope_tabs, rope_tiles, cache_shapes, cache_plan):
    n_out = w_all.shape[2]
    n_w = n_out // N_CHUNK
    nb = DEC_SEQ // TM
    tok = lambda i: _tok(i, n_w)
    in_specs = _token_specs(x_parts, n_w) + [
        pl.BlockSpec((None, 1, 6, D_MODEL), lambda i: (mod_layer, _mod_row(tok(i)), 0, 0)),
        pl.BlockSpec((1, D_MODEL, N_CHUNK), lambda i: (layer, 0, jnp.minimum(i, n_w - 1))),
        pl.BlockSpec((TM, LANES), lambda i: (_lat_blk(tok(i)) % nb, 0)),
        pl.BlockSpec((TM, LANES), lambda i: (_lat_blk(tok(i)) % nb, 0))]
    out_specs = [pl.BlockSpec((TM, n_out), lambda i: (tok(i), 0))]
    out_shape = [jax.ShapeDtypeStruct((N_TOK, n_out), bf16)]
    for shp in cache_shapes:
        blk = (TM // SEQ,) + tuple(shp[1:])
        out_specs.append(pl.BlockSpec(blk, lambda i, nd=len(shp): (_ctx_blk(tok(i)),) + (0,) * (nd - 1)))
        out_shape.append(jax.ShapeDtypeStruct(tuple(shp), f32))
    return pl.pallas_call(
        functools.partial(_proj_kernel, n_x=len(x_parts), n_cache=len(cache_shapes), n_w=n_w,
                          rope_tiles=frozenset(rope_tiles), scale_tiles=frozenset(scale_tiles),
                          cache_plan=dict(cache_plan)),
        grid=(n_w + TOK_BLOCKS,),
        in_specs=in_specs,
        out_specs=out_specs,
        out_shape=out_shape,
        scratch_shapes=[pltpu.VMEM((n_w, D_MODEL, N_CHUNK), bf16), pltpu.VMEM((TM, D_MODEL), bf16)],
        compiler_params=_params(1),
        name="proj",
    )(*x_parts, mod, w_all, *rope_tabs)


def _post_kernel(*refs, n_x, n_y, ka, kb, n_w):
    x_refs = refs[:n_x]
    (ac_ref, al_ref, bc_ref, bl_ref, mod_ref, lng_ref, lnb_ref,
     wo_ref, wg_ref, wu_ref, wd_ref) = refs[n_x:n_x + 11]
    y_refs = refs[n_x + 11:n_x + 11 + n_y]
    wo_s, wg_s, wu_s, wd_s, x1_ref, u_ref, h_ref, y_ref = refs[n_x + 11 + n_y:]
    n_wo = (ka + kb) // N_CHUNK
    i = pl.program_id(0)

    @pl.when(i < n_w)
    def _():
        wg_s[i] = wg_ref[0].astype(bf16)
        wu_s[i] = wu_ref[0].astype(bf16)
        wd_s[i] = wd_ref[0].astype(bf16)

    @pl.when(i < n_wo)
    def _():
        wo_s[i] = wo_ref[0].astype(bf16)

    @pl.when(i >= n_w)
    def _():
        is_ctx = (i - n_w) < CTX_BLOCKS
        gate1 = mod_ref[0, 2:3, :]
        shift2 = mod_ref[0, 3:4, :]
        scale2 = mod_ref[0, 4:5, :]
        gate2 = mod_ref[0, 5:6, :]
        groups = [slice(r * TM // ROW_GROUPS, (r + 1) * TM // ROW_GROUPS) for r in range(ROW_GROUPS)]
        for rs in groups:
            a = _pick((ac_ref, al_ref), is_ctx, rs)
            b = _pick((bc_ref, bl_ref), is_ctx, rs)
            pieces = ([a[:, c:c + N_CHUNK] for c in range(0, ka, N_CHUNK)]
                      + [b[:, c:c + N_CHUNK] for c in range(0, kb, N_CHUNK)])
            h = functools.reduce(lambda s, p: s + p, [_dot(p, wo_s[c]) for c, p in enumerate(pieces)])
            x1 = _ln(ALPHA * _pick(x_refs, is_ctx, rs) + gate1 * h) * lng_ref[0, 0:1, :] + lnb_ref[0, 0:1, :]
            x1_ref[rs, :] = x1
            u_ref[rs, :] = (_ln(x1) * (1.0 + scale2) + shift2).astype(bf16)
        def ffn_chunk(rs, c):
            g = _dot(u_ref[rs, :], wg_s[c])
            up = _dot(u_ref[rs, :], wu_s[c])
            h_ref[rs, c * N_CHUNK:(c + 1) * N_CHUNK] = (_silu(g) * up).astype(bf16)

        def ffn_down(rs):
            ffn = functools.reduce(lambda s, p: s + p,
                                   [_dot(h_ref[rs, c * N_CHUNK:(c + 1) * N_CHUNK], wd_s[c]) for c in range(n_w)])
            y_ref[rs, :] = _ln(ALPHA * x1_ref[rs, :] + gate2 * ffn) * lng_ref[0, 1:2, :] + lnb_ref[0, 1:2, :]

        for c in range(n_w + FFN_SKEW * (ROW_GROUPS - 1)):
            for r, rs in enumerate(groups):
                cc = c - FFN_SKEW * r
                if 0 <= cc < n_w:
                    ffn_chunk(rs, cc)
                if cc == n_w - 1:
                    ffn_down(rs)
        if n_y == 1:
            y_refs[0][...] = y_ref[...]
        else:
            @pl.when(is_ctx)
            def _():
                y_refs[0][...] = y_ref[...]

            @pl.when(jnp.logical_not(is_ctx))
            def _():
                y_refs[1][...] = y_ref[...]


def _post(x_parts, mix_a, mix_b, mod, ln_g, ln_b, w_out, w_gate, w_up, w_down, layer, mix_layer, split_out):
    ka, kb = mix_a[0].shape[1], mix_b[0].shape[1]
    n_w = D_FF // N_CHUNK
    n_wo = (ka + kb) // N_CHUNK
    tok = lambda i: _tok(i, n_w)
    lay = lambda i: (layer, 0, 0)
    in_specs = (_token_specs(x_parts, n_w) + _token_specs(mix_a, n_w) + _token_specs(mix_b, n_w) + [
        pl.BlockSpec((None, 1, 6, D_MODEL), lambda i: (layer, _mod_row(tok(i)), 0, 0)),
        pl.BlockSpec((1, 2, D_MODEL), lay),
        pl.BlockSpec((1, 2, D_MODEL), lay),
        pl.BlockSpec((1, N_CHUNK, D_MODEL), lambda i: (mix_layer, jnp.minimum(i, n_wo - 1), 0)),
        pl.BlockSpec((1, D_MODEL, N_CHUNK), lambda i: (layer, 0, jnp.minimum(i, n_w - 1))),
        pl.BlockSpec((1, D_MODEL, N_CHUNK), lambda i: (layer, 0, jnp.minimum(i, n_w - 1))),
        pl.BlockSpec((1, N_CHUNK, D_MODEL), lambda i: (layer, jnp.minimum(i, n_w - 1), 0))])
    if split_out:
        out_specs = [pl.BlockSpec((TM, D_MODEL), lambda i: (_ctx_blk(tok(i)), 0)),
                     pl.BlockSpec((TM, D_MODEL), lambda i: (_lat_blk(tok(i)), 0))]
        out_shape = [jax.ShapeDtypeStruct((N_CTX, D_MODEL), f32), jax.ShapeDtypeStruct((N_LAT, D_MODEL), f32)]
    else:
        out_specs = [pl.BlockSpec((TM, D_MODEL), lambda i: (tok(i), 0))]
        out_shape = [jax.ShapeDtypeStruct((N_TOK, D_MODEL), f32)]
    return pl.pallas_call(
        functools.partial(_post_kernel, n_x=len(x_parts), n_y=len(out_shape), ka=ka, kb=kb, n_w=n_w),
        grid=(n_w + TOK_BLOCKS,),
        in_specs=in_specs,
        out_specs=out_specs,
        out_shape=out_shape,
        scratch_shapes=[pltpu.VMEM((n_wo, N_CHUNK, D_MODEL), bf16), pltpu.VMEM((n_w, D_MODEL, N_CHUNK), bf16),
                        pltpu.VMEM((n_w, D_MODEL, N_CHUNK), bf16), pltpu.VMEM((n_w, N_CHUNK, D_MODEL), bf16),
                        pltpu.VMEM((TM, D_MODEL), f32), pltpu.VMEM((TM, D_MODEL), bf16),
                        pltpu.VMEM((TM, D_FF), bf16), pltpu.VMEM((TM, D_MODEL), f32)],
        compiler_params=_params(1),
        name="post",
    )(*x_parts, *mix_a, *mix_b, mod, ln_g, ln_b, w_out, w_gate, w_up, w_down)


def _group_norm_gate(ro, rg, gmat, gn_g, gn_b):
    def gmean(parts):
        cols = []
        for c in range(0, RET_W, N_CHUNK):
            cols.append(sum(_dot(p[:, c:c + N_CHUNK], gmat) for p in parts))
        return jnp.concatenate(cols, -1)

    d = ro - gmean(_split_bf16(ro))
    var = gmean([(d * d).astype(bf16)])
    y = d * lax.rsqrt(var + LN_EPS) * gn_g + gn_b
    return _silu(rg.astype(f32)) * y


def _dup_head(x, j):
    first = _lane_half_mask(x.shape)
    keep = first if j == 0 else jnp.logical_not(first)
    xm = jnp.where(keep, x.astype(f32), 0.0)
    return xm + pltpu.roll(xm, HEAD_DIM, 1)


def _softmax_parts(scores, sink):
    m = sink
    for s in scores:
        m = jnp.maximum(m, jnp.max(s, -1, keepdims=True))
    es = [jnp.exp2(s - m) for s in scores]
    denom = jnp.exp2(sink - m)
    for e in es:
        denom = denom + jnp.sum(e, -1, keepdims=True)
    return es, denom


def _retention_tables(lg_ref, lgf_ref, lgb_ref, dmask_ref, kdec_ref, n):
    row = lax.broadcasted_iota(jnp.int32, (n, n), 0)
    col = lax.broadcasted_iota(jnp.int32, (n, n), 1)
    diff = (row - col).astype(f32)
    diag = jnp.where(row == col, 2.0 * QK_SCALE, QK_SCALE)
    for h in range(H_RET):
        dmask_ref[h] = jnp.exp(jnp.where(diff >= 0, lg_ref[0, h] * diff, -lg_ref[1, h] * diff)) * diag
    t = lax.broadcasted_iota(jnp.int32, (n, RET_W), 0).astype(f32)
    kdec_ref[0] = jnp.exp(lgf_ref[...] * (n - 1.0 - t)) * QK_SCALE
    kdec_ref[1] = jnp.exp(lgb_ref[...] * t) * QK_SCALE


def _retention_intra(pairs, q_of, k_of, v_of, dmask_ref):
    first = _lane_half_mask(k_of(pairs[0]).shape)
    masked = {}
    for p in pairs:
        kb = k_of(p)
        for e in range(2):
            keep = first if e == 0 else jnp.logical_not(first)
            s = _dot_nt(q_of(p), jnp.where(keep, kb, jnp.zeros_like(kb))) * dmask_ref[2 * p + e]
            masked[p, e] = s.astype(bf16)
    outs = {}
    for p in pairs:
        pv = [_dot(masked[p, e], v_of(p)) for e in range(2)]
        outs[p] = jnp.where(_lane_half_mask(pv[0].shape), pv[0], pv[1])
    return outs


def _window_group(subs, q_of, k_parts_of, v_parts_of, masks, sink_of):
    scores = {}
    for key in subs:
        parts = [_dot_nt(q_of(key), k) for k in k_parts_of(key)]
        scores[key] = [sc if mk is None else jnp.where(mk, sc, NEG_BIG) for sc, mk in zip(parts, masks)]
    probs = {}
    for key in subs:
        es, denom = _softmax_parts(scores[key], sink_of(key))
        probs[key] = ([ex.astype(bf16) for ex in es], denom)
    outs = {}
    for key in subs:
        es, denom = probs[key]
        pv = functools.reduce(lambda x, y: x + y, [_dot(ex, v) for ex, v in zip(es, v_parts_of(key))])
        outs[key] = pv / denom
    return outs


def _ctx_ab_kernel(lg_ref, sink_ref, rq_ref, rk_ref, rv_ref, rg_ref, wq_ref, wk_ref, wv_ref,
                   lgf_ref, lgb_ref, gmat_ref, gng_ref, gnb_ref,
                   ro_ref, wo_ref, st_ref, dmask_ref, kdec_ref, ret_ref):
    t_len = SEQ

    @pl.when(pl.program_id(0) == 0)
    def _():
        _retention_tables(lg_ref, lgf_ref, lgb_ref, dmask_ref, kdec_ref, t_len)

    first = _lane_half_mask((t_len, PAIR_W))
    for sq in range(CTX_SEQS):
        rows = slice(sq * t_len, (sq + 1) * t_len)
        psl = lambda p: slice(p * PAIR_W, (p + 1) * PAIR_W)
        for p0 in range(0, H_RET // 2, RET_GROUP):
            pairs = list(range(p0, p0 + RET_GROUP))
            intra = _retention_intra(pairs, lambda p: rq_ref[rows, psl(p)], lambda p: rk_ref[rows, psl(p)],
                                     lambda p: rv_ref[rows, psl(p)], dmask_ref)
            for p in pairs:
                ret_ref[rows, psl(p)] = intra[p]
        for p in range(H_RET // 2):
            sl = psl(p)
            kb = rk_ref[rows, sl]
            v = rv_ref[rows, sl]
            for d in range(2):
                kd_t = (kb * kdec_ref[d, :, sl]).T.astype(bf16)
                st = _dot(kd_t, v)
                st_ref[sq, d, 2 * p] = st[0:HEAD_DIM, 0:HEAD_DIM]
                st_ref[sq, d, 2 * p + 1] = pltpu.roll(st[HEAD_DIM:, :], HEAD_DIM, 1)[:, 0:HEAD_DIM]
        ro_ref[rows, :] = _group_norm_gate(ret_ref[rows, :], rg_ref[rows, :], gmat_ref[...], gng_ref[...],
                                           gnb_ref[...]).astype(bf16)

        k_dup = [_dup_head(wk_ref[rows, :], j).astype(bf16) for j in range(KV_WIN)]
        v_dup = [_dup_head(wv_ref[rows, :], j).astype(bf16) for j in range(KV_WIN)]

        def q_masked(key):
            qp, e = key
            qb = wq_ref[rows, qp * PAIR_W:(qp + 1) * PAIR_W]
            return jnp.where(first if e == 0 else jnp.logical_not(first), qb, jnp.zeros_like(qb))

        kv_of = lambda key: key[0] * 2 // G_WIN
        for g0 in range(0, H_WIN // 2, WIN_GROUP):
            subs = [(qp, e) for qp in range(g0, g0 + WIN_GROUP) for e in range(2)]
            outs = _window_group(subs, q_masked, lambda key: [k_dup[kv_of(key)]], lambda key: [v_dup[kv_of(key)]],
                                 [None], lambda key: sink_ref[0, 2 * key[0] + key[1]] * LOG2_E)
            for qp in range(g0, g0 + WIN_GROUP):
                wo_ref[rows, qp * PAIR_W:(qp + 1) * PAIR_W] = jnp.where(first, outs[qp, 0], outs[qp, 1]).astype(bf16)


def _ctx_ab(proj, log_gamma, sink, lgf_lanes, lgb_lanes, gmat, gn_g, gn_b):
    t = SEQ
    tb = CTX_SEQS * t
    smem = pl.BlockSpec(memory_space=pltpu.SMEM)
    const = lambda b: (0, 0)
    col = lambda c: (lambda b: (b, c))
    return pl.pallas_call(
        _ctx_ab_kernel,
        grid=(BATCH // CTX_SEQS,),
        in_specs=[smem, smem,
                  pl.BlockSpec((tb, RET_W), col(0)), pl.BlockSpec((tb, RET_W), col(1)),
                  pl.BlockSpec((tb, RET_W), col(2)), pl.BlockSpec((tb, RET_W), col(3)),
                  pl.BlockSpec((tb, WIN_W), col(4)),
                  pl.BlockSpec((tb, KV_W), col((4 * RET_W + WIN_W) // KV_W)),
                  pl.BlockSpec((tb, KV_W), col((4 * RET_W + WIN_W) // KV_W + 1)),
                  pl.BlockSpec((1, RET_W), const), pl.BlockSpec((1, RET_W), const),
                  pl.BlockSpec((N_CHUNK, N_CHUNK), const),
                  pl.BlockSpec((1, RET_W), const), pl.BlockSpec((1, RET_W), const)],
        out_specs=[pl.BlockSpec((tb, RET_W), lambda b: (b, 0)),
                   pl.BlockSpec((tb, WIN_W), lambda b: (b, 0)),
                   pl.BlockSpec((CTX_SEQS, 2, H_RET, HEAD_DIM, HEAD_DIM), lambda b: (b, 0, 0, 0, 0))],
        out_shape=[jax.ShapeDtypeStruct((BATCH * t, RET_W), bf16),
                   jax.ShapeDtypeStruct((BATCH * t, WIN_W), bf16),
                   jax.ShapeDtypeStruct((BATCH, 2, H_RET, HEAD_DIM, HEAD_DIM), f32)],
        scratch_shapes=[pltpu.VMEM((H_RET, t, t), f32), pltpu.VMEM((2, t, RET_W), f32),
                        pltpu.VMEM((tb, RET_W), f32)],
        compiler_params=_params(1),
        name="ctx_ab",
    )(log_gamma, sink, proj, proj, proj, proj, proj, proj, proj, lgf_lanes, lgb_lanes, gmat, gn_g, gn_b)


def _pair_state(s0_ref, d, p):
    zero = jnp.zeros((HEAD_DIM, HEAD_DIM), f32)
    top = jnp.concatenate([s0_ref[0, 0, d, 2 * p], zero], 1)
    bottom = jnp.concatenate([zero, s0_ref[0, 0, d, 2 * p + 1]], 1)
    return jnp.concatenate([top, bottom], 0)


def _lat_ab_kernel(lg_ref, sink_ref, rq_ref, rk_ref, rv_ref, rg_ref, wq_ref, wk_ref, wv_ref, ck_ref, cv_ref,
                   s0_ref, lgf_ref, lgb_ref, gmat_ref, gng_ref, gnb_ref,
                   ro_ref, wo_ref, ret_ref, dmask_ref, kdec_ref, qdec_ref, sf_ref, sb_ref):
    t_len = DEC_SEQ
    n_chunks = t_len // TQ
    chunk = pl.program_id(1)
    q0 = pl.multiple_of(chunk * TQ, TQ)
    first = _lane_half_mask((TQ, PAIR_W))

    @pl.when(jnp.logical_and(pl.program_id(0) == 0, chunk == 0))
    def _():
        _retention_tables(lg_ref, lgf_ref, lgb_ref, dmask_ref, kdec_ref, TQ)
        t = lax.broadcasted_iota(jnp.int32, (TQ, RET_W), 0).astype(f32)
        qdec_ref[0] = jnp.exp(lgf_ref[...] * (t + 1.0))
        qdec_ref[1] = jnp.exp(lgb_ref[...] * (TQ - t))

    @pl.when(chunk == 0)
    def _():
        r = lax.broadcasted_iota(jnp.int32, (PAIR_W, PAIR_W), 0)
        c_ = lax.broadcasted_iota(jnp.int32, (PAIR_W, PAIR_W), 1)
        same_head = (r < HEAD_DIM) == (c_ < HEAD_DIM)
        for p in range(H_RET // 2):
            sl = slice(p * PAIR_W, (p + 1) * PAIR_W)
            kv = []
            for c in range(n_chunks):
                rows = slice(c * TQ, (c + 1) * TQ)
                kc = rk_ref[rows, sl]
                vc = rv_ref[rows, sl]
                kv.append([jnp.where(same_head, _dot((kc * kdec_ref[d, :, sl]).T.astype(bf16), vc), 0.0)
                           for d in range(2)])
            state = _pair_state(s0_ref, 0, p)
            for c in range(n_chunks):
                sf_ref[c, p] = state
                state = state * jnp.exp(lgf_ref[:, sl] * TQ) + kv[c][0]
            state = _pair_state(s0_ref, 1, p)
            for c in reversed(range(n_chunks)):
                sb_ref[c, p] = state
                state = state * jnp.exp(lgb_ref[:, sl] * TQ) + kv[c][1]

    psl = lambda p: slice(p * PAIR_W, (p + 1) * PAIR_W)
    intra = {}
    for p0 in range(0, H_RET // 2, RET_GROUP):
        intra.update(_retention_intra(list(range(p0, p0 + RET_GROUP)), lambda p: rq_ref[:, psl(p)],
                                      lambda p: rk_ref[pl.ds(q0, TQ), psl(p)],
                                      lambda p: rv_ref[pl.ds(q0, TQ), psl(p)], dmask_ref))
    for p in range(H_RET // 2):
        sl = psl(p)
        q = rq_ref[:, sl]
        o = intra[p]
        o = o + _dot(q, sf_ref[chunk, p].astype(bf16)) * qdec_ref[0, :, sl]
        o = o + _dot(q, sb_ref[chunk, p].astype(bf16)) * qdec_ref[1, :, sl]
        ret_ref[:, sl] = o
    ro_ref[...] = _group_norm_gate(ret_ref[...], rg_ref[...], gmat_ref[...], gng_ref[...], gnb_ref[...]).astype(bf16)

    band = TQ + 2 * WINDOW
    k_start = pl.multiple_of(jnp.clip(q0 - WINDOW, 0, t_len - band), LANES)
    qi = q0 + lax.broadcasted_iota(jnp.int32, (TQ, band), 0)
    kj = k_start + lax.broadcasted_iota(jnp.int32, (TQ, band), 1)
    in_band = jnp.abs(qi - kj) <= WINDOW
    k_parts = [[_dup_head(wk_ref[pl.ds(k_start, band), :], j).astype(bf16), _dup_head(ck_ref[0], j).astype(bf16)]
               for j in range(KV_WIN)]
    v_parts = [[_dup_head(wv_ref[pl.ds(k_start, band), :], j).astype(bf16), _dup_head(cv_ref[0], j).astype(bf16)]
               for j in range(KV_WIN)]

    def q_masked(key):
        qp, e = key
        qb = wq_ref[:, qp * PAIR_W:(qp + 1) * PAIR_W]
        return jnp.where(first if e == 0 else jnp.logical_not(first), qb, jnp.zeros_like(qb))

    kv_of = lambda key: key[0] * 2 // G_WIN
    for g0 in range(0, H_WIN // 2, WIN_GROUP_LAT):
        subs = [(qp, e) for qp in range(g0, g0 + WIN_GROUP_LAT) for e in range(2)]
        outs = _window_group(subs, q_masked, lambda key: k_parts[kv_of(key)], lambda key: v_parts[kv_of(key)],
                             [in_band, None], lambda key: sink_ref[0, 2 * key[0] + key[1]] * LOG2_E)
        for qp in range(g0, g0 + WIN_GROUP_LAT):
            wo_ref[:, qp * PAIR_W:(qp + 1) * PAIR_W] = jnp.where(first, outs[qp, 0], outs[qp, 1]).astype(bf16)


def _lat_ab(proj, log_gamma, sink, ck, cv, state, layer, lgf_lanes, lgb_lanes, gmat, gn_g, gn_b):
    t = DEC_SEQ
    nq = t // TQ
    smem = pl.BlockSpec(memory_space=pltpu.SMEM)
    const = lambda b, i: (0, 0)
    qcol = lambda c: (lambda b, i: (N_CTX // TQ + b * nq + i, c))
    bcol = lambda c: (lambda b, i: (N_CTX // t + b, c))
    kv_col = (4 * RET_W + WIN_W) // KV_W
    return pl.pallas_call(
        _lat_ab_kernel,
        grid=(DEC_BATCH, nq),
        in_specs=[smem, smem,
                  pl.BlockSpec((TQ, RET_W), qcol(0)), pl.BlockSpec((t, RET_W), bcol(1)),
                  pl.BlockSpec((t, RET_W), bcol(2)), pl.BlockSpec((TQ, RET_W), qcol(3)),
                  pl.BlockSpec((TQ, WIN_W), qcol(4)),
                  pl.BlockSpec((t, KV_W), bcol(kv_col)), pl.BlockSpec((t, KV_W), bcol(kv_col + 1)),
                  pl.BlockSpec((1, PAST_LEN, KV_W), lambda b, i: (b, 0, 0)),
                  pl.BlockSpec((1, PAST_LEN, KV_W), lambda b, i: (b, 0, 0)),
                  pl.BlockSpec((1, 1, 2, H_RET, HEAD_DIM, HEAD_DIM), lambda b, i: (b, layer, 0, 0, 0, 0)),
                  pl.BlockSpec((1, RET_W), const), pl.BlockSpec((1, RET_W), const),
                  pl.BlockSpec((N_CHUNK, N_CHUNK), const),
                  pl.BlockSpec((1, RET_W), const), pl.BlockSpec((1, RET_W), const)],
        out_specs=[pl.BlockSpec((TQ, RET_W), lambda b, i: (b * nq + i, 0)),
                   pl.BlockSpec((TQ, WIN_W), lambda b, i: (b * nq + i, 0))],
        out_shape=[jax.ShapeDtypeStruct((DEC_BATCH * t, RET_W), bf16),
                   jax.ShapeDtypeStruct((DEC_BATCH * t, WIN_W), bf16)],
        scratch_shapes=[pltpu.VMEM((TQ, RET_W), f32), pltpu.VMEM((H_RET, TQ, TQ), f32),
                        pltpu.VMEM((2, TQ, RET_W), f32), pltpu.VMEM((2, TQ, RET_W), f32),
                        pltpu.VMEM((t // TQ, H_RET // 2, PAIR_W, PAIR_W), f32),
                        pltpu.VMEM((t // TQ, H_RET // 2, PAIR_W, PAIR_W), f32)],
        compiler_params=_params(2),
        name="lat_ab",
    )(log_gamma, sink, proj, proj, proj, proj, proj, proj, proj, ck, cv, state,
      lgf_lanes, lgb_lanes, gmat, gn_g, gn_b)


def _lambda_full(lam_ref, lam_init):
    lam = lam_ref[...]
    a = jnp.sum(lam[0:1, :] * lam[1:2, :], -1, keepdims=True)
    b = jnp.sum(lam[2:3, :] * lam[3:4, :], -1, keepdims=True)
    return jnp.exp(a) - jnp.exp(b) + lam_init


def _diff_heads(q_of, k_parts_of, v_parts_of, lam, subln, lam_init, group):
    res = []
    for h0 in range(0, H_DIFF, group):
        res += _diff_head_group(range(h0, h0 + group), q_of, k_parts_of, v_parts_of, lam, subln, lam_init)
    return res


def _diff_head_group(heads, q_of, k_parts_of, v_parts_of, lam, subln, lam_init):
    subs = [(h, e) for h in heads for e in range(2)]
    scores = {}
    for h, e in subs:
        q = q_of(h)
        fm = _lane_half_mask(q.shape)
        q_sub = jnp.where(fm if e == 0 else jnp.logical_not(fm), q, jnp.zeros_like(q))
        scores[h, e] = [_dot(q_sub, k) if transposed else _dot_nt(q_sub, k) for k, transposed in k_parts_of(h)]
    probs = {}
    for key in subs:
        m = scores[key][0].max(-1, keepdims=True)
        for sc in scores[key][1:]:
            m = jnp.maximum(m, sc.max(-1, keepdims=True))
        es = [jnp.exp2(sc - m) for sc in scores[key]]
        denom = es[0].sum(-1, keepdims=True)
        for ex in es[1:]:
            denom = denom + ex.sum(-1, keepdims=True)
        probs[key] = ([ex.astype(bf16) for ex in es], denom)
    outs = {}
    for h, e in subs:
        es, denom = probs[h, e]
        pv = functools.reduce(lambda x, y: x + y, [_dot(ex, v) for v, ex in zip(v_parts_of(h), es)])
        outs[h, e] = pv / denom
    res = []
    for h in heads:
        a = outs[h, 0] - lam * outs[h, 1]
        res.append(a * lax.rsqrt(jnp.mean(a * a, -1, keepdims=True) + LN_EPS) * subln * (1.0 - lam_init))
    return res


def _fourier_rows(ct_ref, st_ref, z, bdc_ref, bds_ref):
    zc = _dot(z, bdc_ref[...].astype(bf16)).astype(bf16)
    zs = _dot(z, bds_ref[...].astype(bf16)).astype(bf16)
    return _dot(ct_ref[...].astype(bf16), zc) - _dot(st_ref[...].astype(bf16), zs)


def _ctx_cd_kernel(q_ref, k_ref, v_ref, z_ref, lam_ref, subln_ref, ct_ref, st_ref, bdc_ref, bds_ref,
                   a_ref, zf_ref, *, lam_init):
    lam = _lambda_full(lam_ref, lam_init)
    for sq in range(CTX_SEQS_CD):
        rows = slice(sq * SEQ, (sq + 1) * SEQ)
        sl = lambda h: slice(h * PAIR_W, (h + 1) * PAIR_W)
        heads = _diff_heads(lambda h: q_ref[rows, sl(h)], lambda h: [(k_ref[rows, sl(h)], False)],
                            lambda h: [v_ref[rows, sl(h)]], lam, subln_ref[...], lam_init, DIFF_GROUP)
        for h in range(H_DIFF):
            a_ref[rows, sl(h)] = heads[h].astype(bf16)
        zf_ref[rows, :] = _fourier_rows(ct_ref, st_ref, z_ref[rows, :], bdc_ref, bds_ref).astype(bf16)


def _ctx_cd(proj, lam, subln, ct, st, bdc, bds, lam_init):
    t = SEQ
    tb = CTX_SEQS_CD * t
    const = lambda b: (0, 0)
    col = lambda c: (lambda b: (b, c))
    return pl.pallas_call(
        functools.partial(_ctx_cd_kernel, lam_init=lam_init),
        grid=(BATCH // CTX_SEQS_CD,),
        in_specs=[pl.BlockSpec((tb, DIFF_W), col(0)), pl.BlockSpec((tb, DIFF_W), col(1)),
                  pl.BlockSpec((tb, DIFF_W), col(2)), pl.BlockSpec((tb, FNET_W), col(3 * DIFF_W // FNET_W)),
                  pl.BlockSpec((4, HEAD_DIM), const), pl.BlockSpec((1, PAIR_W), const),
                  pl.BlockSpec((t, t), const), pl.BlockSpec((t, t), const),
                  pl.BlockSpec((FNET_W, FNET_W), const), pl.BlockSpec((FNET_W, FNET_W), const)],
        out_specs=[pl.BlockSpec((tb, DIFF_W), lambda b: (b, 0)), pl.BlockSpec((tb, FNET_W), lambda b: (b, 0))],
        out_shape=[jax.ShapeDtypeStruct((BATCH * t, DIFF_W), bf16),
                   jax.ShapeDtypeStruct((BATCH * t, FNET_W), bf16)],
        compiler_params=_params(1),
        name="ctx_cd",
    )(proj, proj, proj, proj, lam, subln, ct, st, bdc, bds)


def _lat_cd_kernel(q_ref, k_ref, v_ref, z_ref, ckt_ref, cv_ref, lam_ref, subln_ref, ct_ref, st_ref, bdc_ref, bds_ref,
                   a_ref, zf_ref, *, lam_init):
    lam = _lambda_full(lam_ref, lam_init)
    sl = lambda h: slice(h * PAIR_W, (h + 1) * PAIR_W)
    heads = _diff_heads(lambda h: q_ref[:, sl(h)],
                        lambda h: [(k_ref[:, sl(h)], False), (ckt_ref[0, h].astype(bf16), True)],
                        lambda h: [v_ref[:, sl(h)], cv_ref[0, h].astype(bf16)], lam, subln_ref[...], lam_init,
                        DIFF_GROUP_LAT)
    for h in range(H_DIFF):
        a_ref[:, sl(h)] = heads[h].astype(bf16)
    zf_ref[...] = _fourier_rows(ct_ref, st_ref, z_ref[...], bdc_ref, bds_ref).astype(bf16)


def _lat_cd(proj, ck, cv, lam, subln, ct, st, bdc, bds, lam_init):
    t = DEC_SEQ
    nq = t // TQ_CD
    const = lambda b, i: (0, 0)
    return pl.pallas_call(
        functools.partial(_lat_cd_kernel, lam_init=lam_init),
        grid=(DEC_BATCH, nq),
        in_specs=[pl.BlockSpec((TQ_CD, DIFF_W), lambda b, i: (N_CTX // TQ_CD + b * nq + i, 0)),
                  pl.BlockSpec((t, DIFF_W), lambda b, i: (N_CTX // t + b, 1)),
                  pl.BlockSpec((t, DIFF_W), lambda b, i: (N_CTX // t + b, 2)),
                  pl.BlockSpec((t, FNET_W), lambda b, i: (N_CTX // t + b, 3 * DIFF_W // FNET_W)),
                  pl.BlockSpec((1, H_DIFF, PAIR_W, PAST_LEN), lambda b, i: (b, 0, 0, 0)),
                  pl.BlockSpec((1, H_DIFF, PAST_LEN, PAIR_W), lambda b, i: (b, 0, 0, 0)),
                  pl.BlockSpec((4, HEAD_DIM), const), pl.BlockSpec((1, PAIR_W), const),
                  pl.BlockSpec((TQ_CD, t), lambda b, i: (i, 0)), pl.BlockSpec((TQ_CD, t), lambda b, i: (i, 0)),
                  pl.BlockSpec((FNET_W, FNET_W), const), pl.BlockSpec((FNET_W, FNET_W), const)],
        out_specs=[pl.BlockSpec((TQ_CD, DIFF_W), lambda b, i: (b * nq + i, 0)),
                   pl.BlockSpec((TQ_CD, FNET_W), lambda b, i: (b * nq + i, 0))],
        out_shape=[jax.ShapeDtypeStruct((DEC_BATCH * t, DIFF_W), bf16),
                   jax.ShapeDtypeStruct((DEC_BATCH * t, FNET_W), bf16)],
        compiler_params=_params(2),
        name="lat_cd",
    )(proj, proj, proj, proj, ck, cv, lam, subln, ct, st, bdc, bds)


def _rope_tables():
    t = np.arange(DEC_SEQ)
    quarter = HEAD_DIM // 4
    inv = ROPE_BASE ** (-np.arange(quarter, dtype=np.float64) / quarter)
    ang = np.concatenate([(t // GRID_W)[:, None] * inv, (t % GRID_W)[:, None] * inv], -1)
    cos, sin = np.cos(ang), np.sin(ang)
    reps = LANES // HEAD_DIM
    return (np.tile(np.concatenate([cos, cos], -1), (1, reps)).astype(np.float32),
            np.tile(np.concatenate([-sin, sin], -1), (1, reps)).astype(np.float32))


def _dft_tables(n):
    k = np.arange(n)
    ang = (2.0 * math.pi / n) * ((k[:, None] * k[None, :]) % n)
    return (np.cos(ang) / math.sqrt(n)).astype(np.float32), (np.sin(ang) / math.sqrt(n)).astype(np.float32)


def _block_diag(m, reps):
    return np.kron(np.eye(reps, dtype=m.dtype), m)


def kernel(x_prompt, x_sample, state_ret, cache_win_k, cache_win_v, cache_diff_k, cache_diff_v, c, c_ctx, w_mod, b_mod, ln_g, ln_b, w_in_ab, w_out_ab, ret_log_gamma, ret_gn_g, ret_gn_b, win_sink, w_in_cd, w_out_cd, diff_lambda, diff_subln_g, w_gate, w_up, w_down):
    cond = jnp.concatenate([c_ctx[None, :], c, jnp.zeros((SUBLANES - 1 - DEC_BATCH, D_MODEL), f32)], 0)
    mod = _modulation(cond, w_mod, b_mod).reshape(DEPTH, SUBLANES, 6, D_MODEL)

    rope_tabs = _rope_tables()
    gmat = jnp.asarray(_block_diag(np.full((HEAD_DIM, HEAD_DIM), 1.0 / HEAD_DIM, np.float32),
                                   N_CHUNK // HEAD_DIM), bf16)
    c64, s64 = _dft_tables(FNET_DIM)
    bdc = _block_diag(c64, FNET_GROUPS)
    bds = _block_diag(s64, FNET_GROUPS)
    dft_ctx = _dft_tables(SEQ)
    dft_lat = _dft_tables(DEC_SEQ)

    x_parts = [x_prompt.reshape(N_CTX, D_MODEL), x_sample.reshape(N_LAT, D_MODEL)]
    outs = {}
    for l in range(DEPTH):
        i = l // 2
        if l % 2 == 0:
            lgf = jnp.repeat(ret_log_gamma[i, 0], HEAD_DIM)[None, :]
            lgb = jnp.repeat(ret_log_gamma[i, 1], HEAD_DIM)[None, :]
            gn_g = ret_gn_g[i][None, :]
            gn_b = ret_gn_b[i][None, :]
            sink = win_sink[i][None, :]
            rope_tiles = tuple(range(0, 2 * RET_W // LANES)) + tuple(
                range(4 * RET_W // LANES, (4 * RET_W + WIN_W + KV_W) // LANES))
            kv_tile = (4 * RET_W + WIN_W) // LANES
            kv_shape = (BATCH, 1, KV_WIN, HEAD_DIM, SEQ)
            scale_tiles = tuple(range(4 * RET_W // LANES, (4 * RET_W + WIN_W) // LANES))
            proj, wk_t, wv_t = _proj(x_parts, mod, l, w_in_ab, i, scale_tiles, rope_tabs, rope_tiles,
                                     (kv_shape, kv_shape),
                                     {kv_tile: ("heads", 0, 0), kv_tile + 1: ("heads", 1, 0)})
            ro_c, wo_c, st_c = _ctx_ab(proj, ret_log_gamma[i], sink, lgf, lgb, gmat, gn_g, gn_b)
            ck = cache_win_k[:, i].reshape(DEC_BATCH, PAST_LEN, KV_W)
            cv = cache_win_v[:, i].reshape(DEC_BATCH, PAST_LEN, KV_W)
            ro_l, wo_l = _lat_ab(proj, ret_log_gamma[i], sink, ck, cv, state_ret, i, lgf, lgb, gmat, gn_g, gn_b)
            mix_a, mix_b, w_out = (ro_c, ro_l), (wo_c, wo_l), w_out_ab
            outs.setdefault('state', []).append(st_c[:, None])
            outs.setdefault('win_k', []).append(jnp.transpose(wk_t, (0, 1, 4, 2, 3)))
            outs.setdefault('win_v', []).append(jnp.transpose(wv_t, (0, 1, 4, 2, 3)))
        else:
            lam_init = 0.8 - 0.6 * math.exp(-0.3 * l)
            subln = diff_subln_g[i][None, :]
            rope_tiles = tuple(range(0, 2 * DIFF_W // LANES))
            plan = {}
            for h in range(H_DIFF):
                plan[DIFF_W // LANES + h] = ("pairs", 0, h)
                plan[2 * DIFF_W // LANES + h] = ("plain", 1, h)
            scale_tiles = tuple(range(0, DIFF_W // LANES))
            proj, dk_t, dv_h = _proj(
                x_parts, mod, l, w_in_cd, i, scale_tiles, rope_tabs, rope_tiles,
                ((BATCH, 1, H_DIFF, 2, HEAD_DIM, SEQ), (BATCH, 1, H_DIFF, SEQ, 2 * HEAD_DIM)), plan)
            a_c, z_c = _ctx_cd(proj, diff_lambda[i], subln, dft_ctx[0], dft_ctx[1], bdc, bds, lam_init)
            ck = jnp.transpose(cache_diff_k[:, i], (0, 2, 3, 4, 1)).reshape(DEC_BATCH, H_DIFF, PAIR_W, PAST_LEN)
            cv = jnp.transpose(cache_diff_v[:, i], (0, 2, 1, 3))
            a_l, z_l = _lat_cd(proj, ck, cv, diff_lambda[i], subln, dft_lat[0], dft_lat[1], bdc, bds, lam_init)
            mix_a, mix_b, w_out = (a_c, a_l), (z_c, z_l), w_out_cd
            outs.setdefault('diff_k', []).append(jnp.transpose(dk_t, (0, 1, 5, 2, 3, 4)))
            outs.setdefault('diff_v', []).append(jnp.transpose(dv_h, (0, 1, 3, 2, 4)))
        x_parts = _post(x_parts, mix_a, mix_b, mod, ln_g, ln_b, w_out, w_gate, w_up, w_down, l, i,
                        split_out=(l == DEPTH - 1))

    y_prompt = x_parts[0].reshape(BATCH, SEQ, D_MODEL)
    y_sample = x_parts[1].reshape(DEC_BATCH, DEC_SEQ, D_MODEL)
    cat = lambda parts: parts[0] if len(parts) == 1 else jnp.concatenate(parts, 1)
    return (y_prompt, y_sample, cat(outs['state']), cat(outs['win_k']), cat(outs['win_v']),
            cat(outs['diff_k']), cat(outs['diff_v']))
```

```python
import functools
import math

import jax
import jax.numpy as jnp
import numpy as np
from jax import lax
from jax.experimental import pallas as pl
from jax.experimental.pallas import tpu as pltpu

D_MODEL = 1024
BATCH = 32
SEQ = 256
DEPTH = 2
DEC_BATCH = 2
DEC_SEQ = 1024
PAST_LEN = 512
GRID_W = 64
HEAD_DIM = 64
ROPE_BASE = 10000.0
H_RET = 8
H_WIN = 8
KV_WIN = 2
G_WIN = H_WIN // KV_WIN
WINDOW = 128
H_DIFF = 6
FNET_GROUPS = 4
FNET_DIM = 64
D_FF = 256 * math.ceil(8 * D_MODEL / 3 / 256)
RET_W = H_RET * HEAD_DIM
WIN_W = H_WIN * HEAD_DIM
KV_W = KV_WIN * HEAD_DIM
AB_IN = 4 * RET_W + WIN_W + 2 * KV_W
DIFF_W = H_DIFF * 2 * HEAD_DIM
FNET_W = FNET_GROUPS * FNET_DIM
CD_IN = 3 * DIFF_W + FNET_W
ALPHA = (2 * DEPTH) ** 0.25
LN_EPS = 1e-5
QK_SCALE = HEAD_DIM ** -0.5
LOG2_E = math.log2(math.e)

N_CTX = BATCH * SEQ
N_LAT = DEC_BATCH * DEC_SEQ
N_TOK = N_CTX + N_LAT

LANES = 128
SUBLANES = 8
PAIR_W = 2 * HEAD_DIM
TM = 512
CTX_BLOCKS = N_CTX // TM
TOK_BLOCKS = N_TOK // TM
ROW_GROUPS = 2
FFN_SKEW = 2
TQ = 256
TQ_CD = 512
CTX_SEQS = 2
CTX_SEQS_CD = 4
RET_GROUP = 4
WIN_GROUP = 4
WIN_GROUP_LAT = 4
DIFF_GROUP_LAT = 1
DIFF_GROUP = 3
N_CHUNK = 256
MOD_TN = 1536
NEG_BIG = -1e30
VMEM_LIMIT = 56 * 1024 * 1024

f32 = jnp.float32
bf16 = jnp.bfloat16


def _params(n_axes):
    return pltpu.CompilerParams(dimension_semantics=("arbitrary",) * n_axes,
                                vmem_limit_bytes=VMEM_LIMIT)


def _dot(a, b):
    return jnp.dot(a, b, preferred_element_type=f32)


def _dot_nt(a, b):
    return lax.dot_general(a, b, (((1,), (1,)), ((), ())), preferred_element_type=f32)


def _ln(x):
    mu = jnp.mean(x, -1, keepdims=True)
    d = x - mu
    var = jnp.mean(d * d, -1, keepdims=True)
    return d * lax.rsqrt(var + LN_EPS)


def _silu(x):
    return x * jax.nn.sigmoid(x)


def _split_bf16(x):
    hi = x.astype(bf16)
    lo = (x - hi.astype(f32)).astype(bf16)
    return hi, lo


def _lane_half_mask(shape):
    return (lax.broadcasted_iota(jnp.int32, shape, len(shape) - 1) & HEAD_DIM) == 0


def _mod_kernel(c_ref, w_ref, b_ref, o_ref):
    layer = pl.program_id(0)
    a = _silu(c_ref[...])
    a_hi, a_lo = _split_bf16(a)
    w_hi, w_lo = _split_bf16(w_ref[0])
    acc = _dot(a_hi, w_hi) + _dot(a_lo, w_hi) + _dot(a_hi, w_lo)
    o_ref[0] = acc + b_ref[pl.ds(layer, 1), :]


def _modulation(cond, w_mod, b_mod):
    tn = MOD_TN
    rows = cond.shape[0]
    return pl.pallas_call(
        _mod_kernel,
        grid=(DEPTH, 6 * D_MODEL // tn),
        in_specs=[pl.BlockSpec((rows, D_MODEL), lambda l, j: (0, 0)),
                  pl.BlockSpec((1, D_MODEL, tn), lambda l, j: (l, 0, j)),
                  pl.BlockSpec((DEPTH, tn), lambda l, j: (0, j))],
        out_specs=pl.BlockSpec((1, rows, tn), lambda l, j: (l, 0, j)),
        out_shape=jax.ShapeDtypeStruct((DEPTH, rows, 6 * D_MODEL), f32),
        compiler_params=_params(2),
        name="modulation",
    )(cond, w_mod, b_mod)


def _tok(i, lead):
    return jnp.maximum(i - lead, 0)


def _ctx_blk(t):
    return jnp.minimum(t, CTX_BLOCKS - 1)


def _lat_blk(t):
    return jnp.maximum(t - CTX_BLOCKS, 0)


def _mod_row(t):
    return jnp.where(t < CTX_BLOCKS, 0, 1 + _lat_blk(t) * TM // DEC_SEQ)


def _token_specs(parts, lead):
    width = parts[0].shape[1]
    if len(parts) == 1:
        return [pl.BlockSpec((TM, width), lambda i: (_tok(i, lead), 0))]
    return [pl.BlockSpec((TM, width), lambda i: (_ctx_blk(_tok(i, lead)), 0)),
            pl.BlockSpec((TM, width), lambda i: (_lat_blk(_tok(i, lead)), 0))]


def _pick(refs, is_ctx, rs):
    if len(refs) == 1:
        return refs[0][rs, :]
    if isinstance(is_ctx, bool):
        return refs[0 if is_ctx else 1][rs, :]
    return jnp.where(is_ctx, refs[0][rs, :], refs[1][rs, :])


def _rope_pair(y, cos, sin_signed):
    first_half = (lax.broadcasted_iota(jnp.int32, y.shape, 1) & (HEAD_DIM // 2)) == 0
    swapped = jnp.where(first_half, pltpu.roll(y, LANES - HEAD_DIM // 2, 1), pltpu.roll(y, HEAD_DIM // 2, 1))
    return y * cos + swapped * sin_signed


def _proj_kernel(*refs, n_x, n_cache, n_w, rope_tiles, scale_tiles, cache_plan):
    x_refs = refs[:n_x]
    mod_ref, w_ref, cos_ref, sin_ref, o_ref = refs[n_x:n_x + 5]
    cache_refs = refs[n_x + 5:n_x + 5 + n_cache]
    wbf_ref, u_ref = refs[n_x + 5 + n_cache:]
    i = pl.program_id(0)
    lead = n_w - 1
    groups = [slice(b * SEQ, (b + 1) * SEQ) for b in range(TM // SEQ)]

    def modulated_norm(is_ctx):
        x_ref = x_refs[0] if is_ctx else x_refs[-1]
        shift = mod_ref[0, 0:1, :]
        scale = mod_ref[0, 1:2, :]
        for rs in groups:
            u_ref[rs, :] = (_ln(x_ref[rs, :]) * (1.0 + scale) + shift).astype(bf16)

    def project(is_ctx, c):
        y_all = _dot(u_ref[...], wbf_ref[c])
        for b, rs in enumerate(groups):
            y = y_all[rs, :]
            for t in range(N_CHUNK // LANES):
                tile = c * (N_CHUNK // LANES) + t
                piece = y[:, t * LANES:(t + 1) * LANES]
                if tile in rope_tiles and not is_ctx:
                    piece = _rope_pair(piece, cos_ref[rs, :], sin_ref[rs, :])
                if tile in scale_tiles:
                    piece = piece * (QK_SCALE * LOG2_E)
                o_ref[rs, tile * LANES:(tile + 1) * LANES] = piece.astype(o_ref.dtype)
                if tile in cache_plan and is_ctx:
                    kind, out_idx, slot = cache_plan[tile]
                    c_ref = cache_refs[out_idx]
                    if kind == "plain":
                        c_ref[b, 0, slot] = piece
                    else:
                        piece_t = piece.T
                        if kind == "heads":
                            c_ref[b, 0, 0] = piece_t[0:HEAD_DIM]
                            c_ref[b, 0, 1] = piece_t[HEAD_DIM:]
                        else:
                            c_ref[b, 0, slot, 0] = piece_t[0:HEAD_DIM]
                            c_ref[b, 0, slot, 1] = piece_t[HEAD_DIM:]

    for c in range(n_w):
        @pl.when(i == c)
        def _(c=c):
            wbf_ref[c] = w_ref[0].astype(bf16)
            if c == 0:
                modulated_norm(True)
            project(True, c)

    t = i - lead

    @pl.when(jnp.logical_and(i >= n_w, t < CTX_BLOCKS))
    def _():
        modulated_norm(True)
        for c in range(n_w):
            project(True, c)

    @pl.when(t >= CTX_BLOCKS)
    def _():
        modulated_norm(False)
        for c in range(n_w):
            project(False, c)


def _proj(x_parts, mod, mod_layer, w_all, layer, scale_tiles, rope_tabs, rope_tiles, cache_shapes, cache_plan):
    n_out = w_all.shape[2]
    n_w = n_out // N_CHUNK
    lead = n_w - 1
    nb = DEC_SEQ // TM
    tok = lambda i: _tok(i, lead)
    in_specs = _token_specs(x_parts, lead) + [
        pl.BlockSpec((None, 1, 6, D_MODEL), lambda i: (mod_layer, _mod_row(tok(i)), 0, 0)),
        pl.BlockSpec((1, D_MODEL, N_CHUNK), lambda i: (layer, 0, jnp.minimum(i, n_w - 1))),
        pl.BlockSpec((TM, LANES), lambda i: (_lat_blk(tok(i)) % nb, 0)),
        pl.BlockSpec((TM, LANES), lambda i: (_lat_blk(tok(i)) % nb, 0))]
    out_specs = [pl.BlockSpec((TM, n_out), lambda i: (tok(i), 0))]
    out_shape = [jax.ShapeDtypeStruct((N_TOK, n_out), bf16)]
    for shp in cache_shapes:
        blk = (TM // SEQ,) + tuple(shp[1:])
        out_specs.append(pl.BlockSpec(blk, lambda i, nd=len(shp): (_ctx_blk(tok(i)),) + (0,) * (nd - 1)))
        out_shape.append(jax.ShapeDtypeStruct(tuple(shp), f32))
    return pl.pallas_call(
        functools.partial(_proj_kernel, n_x=len(x_parts), n_cache=len(cache_shapes), n_w=n_w,
                          rope_tiles=frozenset(rope_tiles), scale_tiles=frozenset(scale_tiles),
                          cache_plan=dict(cache_plan)),
        grid=(lead + TOK_BLOCKS,),
        in_specs=in_specs,
        out_specs=out_specs,
        out_shape=out_shape,
        scratch_shapes=[pltpu.VMEM((n_w, D_MODEL, N_CHUNK), bf16), pltpu.VMEM((TM, D_MODEL), bf16)],
        compiler_params=_params(1),
        name="proj",
    )(*x_parts, mod, w_all, *rope_tabs)


def _post_kernel(*refs, n_x, n_y, ka, kb, n_w):
    x_refs = refs[:n_x]
    (ac_ref, al_ref, bc_ref, bl_ref, mod_ref, lng_ref, lnb_ref,
     wo_ref, wg_ref, wu_ref, wd_ref) = refs[n_x:n_x + 11]
    y_refs = refs[n_x + 11:n_x + 11 + n_y]
    wo_s, wg_s, wu_s, wd_s, x1_ref, u_ref, h_ref, y_ref = refs[n_x + 11 + n_y:]
    i = pl.program_id(0)
    lead = n_w - 1
    gate1 = mod_ref[0, 2:3, :]
    shift2 = mod_ref[0, 3:4, :]
    scale2 = mod_ref[0, 4:5, :]
    gate2 = mod_ref[0, 5:6, :]
    groups = [slice(r * TM // ROW_GROUPS, (r + 1) * TM // ROW_GROUPS) for r in range(ROW_GROUPS)]

    def mix_in(rs, is_ctx):
        a = _pick((ac_ref, al_ref), is_ctx, rs)
        b = _pick((bc_ref, bl_ref), is_ctx, rs)
        pieces = ([a[:, c:c + N_CHUNK] for c in range(0, ka, N_CHUNK)]
                  + [b[:, c:c + N_CHUNK] for c in range(0, kb, N_CHUNK)])
        h = functools.reduce(lambda s, p: s + p, [_dot(p, wo_s[c]) for c, p in enumerate(pieces)])
        x1 = _ln(ALPHA * _pick(x_refs, is_ctx, rs) + gate1 * h) * lng_ref[0, 0:1, :] + lnb_ref[0, 0:1, :]
        x1_ref[rs, :] = x1
        u_ref[rs, :] = (_ln(x1) * (1.0 + scale2) + shift2).astype(bf16)

    def finish(rs, ffn):
        y_ref[rs, :] = _ln(ALPHA * x1_ref[rs, :] + gate2 * ffn) * lng_ref[0, 1:2, :] + lnb_ref[0, 1:2, :]

    def emit(is_ctx):
        if n_y == 1:
            y_refs[0][...] = y_ref[...]
        elif isinstance(is_ctx, bool):
            y_refs[0 if is_ctx else 1][...] = y_ref[...]
        else:
            @pl.when(is_ctx)
            def _():
                y_refs[0][...] = y_ref[...]

            @pl.when(jnp.logical_not(is_ctx))
            def _():
                y_refs[1][...] = y_ref[...]

    @pl.when(i < n_w)
    def _():
        wg_s[i] = wg_ref[0].astype(bf16)
        wu_s[i] = wu_ref[0].astype(bf16)
        wd_s[i] = wd_ref[0].astype(bf16)

        @pl.when(i == 0)
        def _():
            for c in range((ka + kb) // N_CHUNK):
                wo_s[c] = wo_ref[0, c * N_CHUNK:(c + 1) * N_CHUNK, :].astype(bf16)
            for rs in groups:
                mix_in(rs, True)
                y_ref[rs, :] = jnp.zeros((TM // ROW_GROUPS, D_MODEL), f32)

        for rs in groups:
            g = _dot(u_ref[rs, :], wg_s[i])
            up = _dot(u_ref[rs, :], wu_s[i])
            y_ref[rs, :] += _dot((_silu(g) * up).astype(bf16), wd_s[i])

        @pl.when(i == lead)
        def _():
            for rs in groups:
                finish(rs, y_ref[rs, :])
            emit(True)

    @pl.when(i >= n_w)
    def _():
        is_ctx = (i - lead) < CTX_BLOCKS
        for rs in groups:
            mix_in(rs, is_ctx)

        def ffn_chunk(rs, c):
            g = _dot(u_ref[rs, :], wg_s[c])
            up = _dot(u_ref[rs, :], wu_s[c])
            h_ref[rs, c * N_CHUNK:(c + 1) * N_CHUNK] = (_silu(g) * up).astype(bf16)

        def ffn_down(rs):
            finish(rs, functools.reduce(
                lambda s, p: s + p,
                [_dot(h_ref[rs, c * N_CHUNK:(c + 1) * N_CHUNK], wd_s[c]) for c in range(n_w)]))

        for c in range(n_w + FFN_SKEW * (ROW_GROUPS - 1)):
            for r, rs in enumerate(groups):
                cc = c - FFN_SKEW * r
                if 0 <= cc < n_w:
                    ffn_chunk(rs, cc)
                if cc == n_w - 1:
                    ffn_down(rs)
        emit(is_ctx)


def _post(x_parts, mix_a, mix_b, mod, ln_g, ln_b, w_out, w_gate, w_up, w_down, layer, mix_layer, split_out):
    ka, kb = mix_a[0].shape[1], mix_b[0].shape[1]
    n_w = D_FF // N_CHUNK
    lead = n_w - 1
    tok = lambda i: _tok(i, lead)
    lay = lambda i: (layer, 0, 0)
    in_specs = (_token_specs(x_parts, lead) + _token_specs(mix_a, lead) + _token_specs(mix_b, lead) + [
        pl.BlockSpec((None, 1, 6, D_MODEL), lambda i: (layer, _mod_row(tok(i)), 0, 0)),
        pl.BlockSpec((1, 2, D_MODEL), lay),
        pl.BlockSpec((1, 2, D_MODEL), lay),
        pl.BlockSpec((1, ka + kb, D_MODEL), lambda i: (mix_layer, 0, 0), pipeline_mode=pl.Buffered(1)),
        pl.BlockSpec((1, D_MODEL, N_CHUNK), lambda i: (layer, 0, jnp.minimum(i, n_w - 1))),
        pl.BlockSpec((1, D_MODEL, N_CHUNK), lambda i: (layer, 0, jnp.minimum(i, n_w - 1))),
        pl.BlockSpec((1, N_CHUNK, D_MODEL), lambda i: (layer, jnp.minimum(i, n_w - 1), 0))])
    if split_out:
        out_specs = [pl.BlockSpec((TM, D_MODEL), lambda i: (_ctx_blk(tok(i)), 0)),
                     pl.BlockSpec((TM, D_MODEL), lambda i: (_lat_blk(tok(i)), 0))]
        out_shape = [jax.ShapeDtypeStruct((N_CTX, D_MODEL), f32), jax.ShapeDtypeStruct((N_LAT, D_MODEL), f32)]
    else:
        out_specs = [pl.BlockSpec((TM, D_MODEL), lambda i: (tok(i), 0))]
        out_shape = [jax.ShapeDtypeStruct((N_TOK, D_MODEL), f32)]
    return pl.pallas_call(
        functools.partial(_post_kernel, n_x=len(x_parts), n_y=len(out_shape), ka=ka, kb=kb, n_w=n_w),
        grid=(lead + TOK_BLOCKS,),
        in_specs=in_specs,
        out_specs=out_specs,
        out_shape=out_shape,
        scratch_shapes=[pltpu.VMEM(((ka + kb) // N_CHUNK, N_CHUNK, D_MODEL), bf16),
                        pltpu.VMEM((n_w, D_MODEL, N_CHUNK), bf16),
                        pltpu.VMEM((n_w, D_MODEL, N_CHUNK), bf16), pltpu.VMEM((n_w, N_CHUNK, D_MODEL), bf16),
                        pltpu.VMEM((TM, D_MODEL), f32), pltpu.VMEM((TM, D_MODEL), bf16),
                        pltpu.VMEM((TM, D_FF), bf16), pltpu.VMEM((TM, D_MODEL), f32)],
        compiler_params=_params(1),
        name="post",
    )(*x_parts, *mix_a, *mix_b, mod, ln_g, ln_b, w_out, w_gate, w_up, w_down)


def _group_norm_gate(ro, rg, gmat, gn_g, gn_b):
    def gmean(parts):
        cols = []
        for c in range(0, RET_W, N_CHUNK):
            cols.append(sum(_dot(p[:, c:c + N_CHUNK], gmat) for p in parts))
        return jnp.concatenate(cols, -1)

    d = ro - gmean(_split_bf16(ro))
    var = gmean([(d * d).astype(bf16)])
    y = d * lax.rsqrt(var + LN_EPS) * gn_g + gn_b
    return _silu(rg.astype(f32)) * y


def _dup_head(x, j):
    first = _lane_half_mask(x.shape)
    keep = first if j == 0 else jnp.logical_not(first)
    xm = jnp.where(keep, x.astype(f32), 0.0)
    return xm + pltpu.roll(xm, HEAD_DIM, 1)


def _softmax_parts(scores, sink):
    m = sink
    for s in scores:
        m = jnp.maximum(m, jnp.max(s, -1, keepdims=True))
    es = [jnp.exp2(s - m) for s in scores]
    denom = jnp.exp2(sink - m)
    for e in es:
        denom = denom + jnp.sum(e, -1, keepdims=True)
    return es, denom


def _retention_tables(lg_ref, lgf_ref, lgb_ref, dmask_ref, kdec_ref, n):
    row = lax.broadcasted_iota(jnp.int32, (n, n), 0)
    col = lax.broadcasted_iota(jnp.int32, (n, n), 1)
    diff = (row - col).astype(f32)
    diag = jnp.where(row == col, 2.0 * QK_SCALE, QK_SCALE)
    for h in range(H_RET):
        dmask_ref[h] = jnp.exp(jnp.where(diff >= 0, lg_ref[0, h] * diff, -lg_ref[1, h] * diff)) * diag
    t = lax.broadcasted_iota(jnp.int32, (n, RET_W), 0).astype(f32)
    kdec_ref[0] = jnp.exp(lgf_ref[...] * (n - 1.0 - t)) * QK_SCALE
    kdec_ref[1] = jnp.exp(lgb_ref[...] * t) * QK_SCALE


def _retention_intra(pairs, q_of, k_of, v_of, dmask_ref):
    first = _lane_half_mask(k_of(pairs[0]).shape)
    masked = {}
    for p in pairs:
        kb = k_of(p)
        for e in range(2):
            keep = first if e == 0 else jnp.logical_not(first)
            s = _dot_nt(q_of(p), jnp.where(keep, kb, jnp.zeros_like(kb))) * dmask_ref[2 * p + e]
            masked[p, e] = s.astype(bf16)
    outs = {}
    for p in pairs:
        pv = [_dot(masked[p, e], v_of(p)) for e in range(2)]
        outs[p] = jnp.where(_lane_half_mask(pv[0].shape), pv[0], pv[1])
    return outs


def _window_group(subs, q_of, k_parts_of, v_parts_of, masks, sink_of):
    scores = {}
    for key in subs:
        parts = [_dot_nt(q_of(key), k) for k in k_parts_of(key)]
        scores[key] = [sc if mk is None else jnp.where(mk, sc, NEG_BIG) for sc, mk in zip(parts, masks)]
    probs = {}
    for key in subs:
        es, denom = _softmax_parts(scores[key], sink_of(key))
        probs[key] = ([ex.astype(bf16) for ex in es], denom)
    outs = {}
    for key in subs:
        es, denom = probs[key]
        pv = functools.reduce(lambda x, y: x + y, [_dot(ex, v) for ex, v in zip(es, v_parts_of(key))])
        outs[key] = pv / denom
    return outs


def _ctx_ab_kernel(lg_ref, sink_ref, rq_ref, rk_ref, rv_ref, rg_ref, wq_ref, wk_ref, wv_ref,
                   lgf_ref, lgb_ref, gmat_ref, gng_ref, gnb_ref,
                   ro_ref, wo_ref, st_ref, dmask_ref, kdec_ref, ret_ref):
    t_len = SEQ

    @pl.when(pl.program_id(0) == 0)
    def _():
        _retention_tables(lg_ref, lgf_ref, lgb_ref, dmask_ref, kdec_ref, t_len)

    first = _lane_half_mask((t_len, PAIR_W))
    for sq in range(CTX_SEQS):
        rows = slice(sq * t_len, (sq + 1) * t_len)
        psl = lambda p: slice(p * PAIR_W, (p + 1) * PAIR_W)
        for p0 in range(0, H_RET // 2, RET_GROUP):
            pairs = list(range(p0, p0 + RET_GROUP))
            intra = _retention_intra(pairs, lambda p: rq_ref[rows, psl(p)], lambda p: rk_ref[rows, psl(p)],
                                     lambda p: rv_ref[rows, psl(p)], dmask_ref)
            for p in pairs:
                ret_ref[rows, psl(p)] = intra[p]
        for p in range(H_RET // 2):
            sl = psl(p)
            kb = rk_ref[rows, sl]
            v = rv_ref[rows, sl]
            for d in range(2):
                kd_t = (kb * kdec_ref[d, :, sl]).T.astype(bf16)
                st = _dot(kd_t, v)
                st_ref[sq, d, 2 * p] = st[0:HEAD_DIM, 0:HEAD_DIM]
                st_ref[sq, d, 2 * p + 1] = pltpu.roll(st[HEAD_DIM:, :], HEAD_DIM, 1)[:, 0:HEAD_DIM]
        ro_ref[rows, :] = _group_norm_gate(ret_ref[rows, :], rg_ref[rows, :], gmat_ref[...], gng_ref[...],
                                           gnb_ref[...]).astype(bf16)

        k_dup = [_dup_head(wk_ref[rows, :], j).astype(bf16) for j in range(KV_WIN)]
        v_dup = [_dup_head(wv_ref[rows, :], j).astype(bf16) for j in range(KV_WIN)]

        def q_masked(key):
            qp, e = key
            qb = wq_ref[rows, qp * PAIR_W:(qp + 1) * PAIR_W]
            return jnp.where(first if e == 0 else jnp.logical_not(first), qb, jnp.zeros_like(qb))

        kv_of = lambda key: key[0] * 2 // G_WIN
        for g0 in range(0, H_WIN // 2, WIN_GROUP):
            subs = [(qp, e) for qp in range(g0, g0 + WIN_GROUP) for e in range(2)]
            outs = _window_group(subs, q_masked, lambda key: [k_dup[kv_of(key)]], lambda key: [v_dup[kv_of(key)]],
                                 [None], lambda key: sink_ref[0, 2 * key[0] + key[1]] * LOG2_E)
            for qp in range(g0, g0 + WIN_GROUP):
                wo_ref[rows, qp * PAIR_W:(qp + 1) * PAIR_W] = jnp.where(first, outs[qp, 0], outs[qp, 1]).astype(bf16)


def _ctx_ab(proj, log_gamma, sink, lgf_lanes, lgb_lanes, gmat, gn_g, gn_b):
    t = SEQ
    tb = CTX_SEQS * t
    smem = pl.BlockSpec(memory_space=pltpu.SMEM)
    const = lambda b: (0, 0)
    col = lambda c: (lambda b: (b, c))
    return pl.pallas_call(
        _ctx_ab_kernel,
        grid=(BATCH // CTX_SEQS,),
        in_specs=[smem, smem,
                  pl.BlockSpec((tb, RET_W), col(0)), pl.BlockSpec((tb, RET_W), col(1)),
                  pl.BlockSpec((tb, RET_W), col(2)), pl.BlockSpec((tb, RET_W), col(3)),
                  pl.BlockSpec((tb, WIN_W), col(4)),
                  pl.BlockSpec((tb, KV_W), col((4 * RET_W + WIN_W) // KV_W)),
                  pl.BlockSpec((tb, KV_W), col((4 * RET_W + WIN_W) // KV_W + 1)),
                  pl.BlockSpec((1, RET_W), const), pl.BlockSpec((1, RET_W), const),
                  pl.BlockSpec((N_CHUNK, N_CHUNK), const),
                  pl.BlockSpec((1, RET_W), const), pl.BlockSpec((1, RET_W), const)],
        out_specs=[pl.BlockSpec((tb, RET_W), lambda b: (b, 0)),
                   pl.BlockSpec((tb, WIN_W), lambda b: (b, 0)),
                   pl.BlockSpec((CTX_SEQS, 2, H_RET, HEAD_DIM, HEAD_DIM), lambda b: (b, 0, 0, 0, 0))],
        out_shape=[jax.ShapeDtypeStruct((BATCH * t, RET_W), bf16),
                   jax.ShapeDtypeStruct((BATCH * t, WIN_W), bf16),
                   jax.ShapeDtypeStruct((BATCH, 2, H_RET, HEAD_DIM, HEAD_DIM), f32)],
        scratch_shapes=[pltpu.VMEM((H_RET, t, t), f32), pltpu.VMEM((2, t, RET_W), f32),
                        pltpu.VMEM((tb, RET_W), f32)],
        compiler_params=_params(1),
        name="ctx_ab",
    )(log_gamma, sink, proj, proj, proj, proj, proj, proj, proj, lgf_lanes, lgb_lanes, gmat, gn_g, gn_b)


def _pair_state(s0_ref, d, p):
    zero = jnp.zeros((HEAD_DIM, HEAD_DIM), f32)
    top = jnp.concatenate([s0_ref[0, 0, d, 2 * p], zero], 1)
    bottom = jnp.concatenate([zero, s0_ref[0, 0, d, 2 * p + 1]], 1)
    return jnp.concatenate([top, bottom], 0)


def _lat_ab_kernel(lg_ref, sink_ref, rq_ref, rk_ref, rv_ref, rg_ref, wq_ref, wk_ref, wv_ref, ck_ref, cv_ref,
                   s0_ref, lgf_ref, lgb_ref, gmat_ref, gng_ref, gnb_ref,
                   ro_ref, wo_ref, ret_ref, dmask_ref, kdec_ref, qdec_ref, sf_ref, sb_ref):
    t_len = DEC_SEQ
    n_chunks = t_len // TQ
    chunk = pl.program_id(1)
    q0 = pl.multiple_of(chunk * TQ, TQ)
    first = _lane_half_mask((TQ, PAIR_W))

    @pl.when(jnp.logical_and(pl.program_id(0) == 0, chunk == 0))
    def _():
        _retention_tables(lg_ref, lgf_ref, lgb_ref, dmask_ref, kdec_ref, TQ)
        t = lax.broadcasted_iota(jnp.int32, (TQ, RET_W), 0).astype(f32)
        qdec_ref[0] = jnp.exp(lgf_ref[...] * (t + 1.0))
        qdec_ref[1] = jnp.exp(lgb_ref[...] * (TQ - t))

    @pl.when(chunk == 0)
    def _():
        r = lax.broadcasted_iota(jnp.int32, (PAIR_W, PAIR_W), 0)
        c_ = lax.broadcasted_iota(jnp.int32, (PAIR_W, PAIR_W), 1)
        same_head = (r < HEAD_DIM) == (c_ < HEAD_DIM)
        for p in range(H_RET // 2):
            sl = slice(p * PAIR_W, (p + 1) * PAIR_W)
            kv = []
            for c in range(n_chunks):
                rows = slice(c * TQ, (c + 1) * TQ)
                kc = rk_ref[rows, sl]
                vc = rv_ref[rows, sl]
                kv.append([jnp.where(same_head, _dot((kc * kdec_ref[d, :, sl]).T.astype(bf16), vc), 0.0)
                           for d in range(2)])
            state = _pair_state(s0_ref, 0, p)
            for c in range(n_chunks):
                sf_ref[c, p] = state
                state = state * jnp.exp(lgf_ref[:, sl] * TQ) + kv[c][0]
            state = _pair_state(s0_ref, 1, p)
            for c in reversed(range(n_chunks)):
                sb_ref[c, p] = state
                state = state * jnp.exp(lgb_ref[:, sl] * TQ) + kv[c][1]

    psl = lambda p: slice(p * PAIR_W, (p + 1) * PAIR_W)
    intra = {}
    for p0 in range(0, H_RET // 2, RET_GROUP):
        intra.update(_retention_intra(list(range(p0, p0 + RET_GROUP)), lambda p: rq_ref[:, psl(p)],
                                      lambda p: rk_ref[pl.ds(q0, TQ), psl(p)],
                                      lambda p: rv_ref[pl.ds(q0, TQ), psl(p)], dmask_ref))
    for p in range(H_RET // 2):
        sl = psl(p)
        q = rq_ref[:, sl]
        o = intra[p]
        o = o + _dot(q, sf_ref[chunk, p].astype(bf16)) * qdec_ref[0, :, sl]
        o = o + _dot(q, sb_ref[chunk, p].astype(bf16)) * qdec_ref[1, :, sl]
        ret_ref[:, sl] = o
    ro_ref[...] = _group_norm_gate(ret_ref[...], rg_ref[...], gmat_ref[...], gng_ref[...], gnb_ref[...]).astype(bf16)

    band = TQ + 2 * WINDOW
    k_start = pl.multiple_of(jnp.clip(q0 - WINDOW, 0, t_len - band), LANES)
    qi = q0 + lax.broadcasted_iota(jnp.int32, (TQ, band), 0)
    kj = k_start + lax.broadcasted_iota(jnp.int32, (TQ, band), 1)
    in_band = jnp.abs(qi - kj) <= WINDOW
    k_parts = [[_dup_head(wk_ref[pl.ds(k_start, band), :], j).astype(bf16), _dup_head(ck_ref[0], j).astype(bf16)]
               for j in range(KV_WIN)]
    v_parts = [[_dup_head(wv_ref[pl.ds(k_start, band), :], j).astype(bf16), _dup_head(cv_ref[0], j).astype(bf16)]
               for j in range(KV_WIN)]

    def q_masked(key):
        qp, e = key
        qb = wq_ref[:, qp * PAIR_W:(qp + 1) * PAIR_W]
        return jnp.where(first if e == 0 else jnp.logical_not(first), qb, jnp.zeros_like(qb))

    kv_of = lambda key: key[0] * 2 // G_WIN
    for g0 in range(0, H_WIN // 2, WIN_GROUP_LAT):
        subs = [(qp, e) for qp in range(g0, g0 + WIN_GROUP_LAT) for e in range(2)]
        outs = _window_group(subs, q_masked, lambda key: k_parts[kv_of(key)], lambda key: v_parts[kv_of(key)],
                             [in_band, None], lambda key: sink_ref[0, 2 * key[0] + key[1]] * LOG2_E)
        for qp in range(g0, g0 + WIN_GROUP_LAT):
            wo_ref[:, qp * PAIR_W:(qp + 1) * PAIR_W] = jnp.where(first, outs[qp, 0], outs[qp, 1]).astype(bf16)


def _lat_ab(proj, log_gamma, sink, ck, cv, state, layer, lgf_lanes, lgb_lanes, gmat, gn_g, gn_b):
    t = DEC_SEQ
    nq = t // TQ
    smem = pl.BlockSpec(memory_space=pltpu.SMEM)
    const = lambda b, i: (0, 0)
    qcol = lambda c: (lambda b, i: (N_CTX // TQ + b * nq + i, c))
    bcol = lambda c: (lambda b, i: (N_CTX // t + b, c))
    kv_col = (4 * RET_W + WIN_W) // KV_W
    return pl.pallas_call(
        _lat_ab_kernel,
        grid=(DEC_BATCH, nq),
        in_specs=[smem, smem,
                  pl.BlockSpec((TQ, RET_W), qcol(0)), pl.BlockSpec((t, RET_W), bcol(1)),
                  pl.BlockSpec((t, RET_W), bcol(2)), pl.BlockSpec((TQ, RET_W), qcol(3)),
                  pl.BlockSpec((TQ, WIN_W), qcol(4)),
                  pl.BlockSpec((t, KV_W), bcol(kv_col)), pl.BlockSpec((t, KV_W), bcol(kv_col + 1)),
                  pl.BlockSpec((1, PAST_LEN, KV_W), lambda b, i: (b, 0, 0)),
                  pl.BlockSpec((1, PAST_LEN, KV_W), lambda b, i: (b, 0, 0)),
                  pl.BlockSpec((1, 1, 2, H_RET, HEAD_DIM, HEAD_DIM), lambda b, i: (b, layer, 0, 0, 0, 0)),
                  pl.BlockSpec((1, RET_W), const), pl.BlockSpec((1, RET_W), const),
                  pl.BlockSpec((N_CHUNK, N_CHUNK), const),
                  pl.BlockSpec((1, RET_W), const), pl.BlockSpec((1, RET_W), const)],
        out_specs=[pl.BlockSpec((TQ, RET_W), lambda b, i: (b * nq + i, 0)),
                   pl.BlockSpec((TQ, WIN_W), lambda b, i: (b * nq + i, 0))],
        out_shape=[jax.ShapeDtypeStruct((DEC_BATCH * t, RET_W), bf16),
                   jax.ShapeDtypeStruct((DEC_BATCH * t, WIN_W), bf16)],
        scratch_shapes=[pltpu.VMEM((TQ, RET_W), f32), pltpu.VMEM((H_RET, TQ, TQ), f32),
                        pltpu.VMEM((2, TQ, RET_W), f32), pltpu.VMEM((2, TQ, RET_W), f32),
                        pltpu.VMEM((t // TQ, H_RET // 2, PAIR_W, PAIR_W), f32),
                        pltpu.VMEM((t // TQ, H_RET // 2, PAIR_W, PAIR_W), f32)],
        compiler_params=_params(2),
        name="lat_ab",
    )(log_gamma, sink, proj, proj, proj, proj, proj, proj, proj, ck, cv, state,
      lgf_lanes, lgb_lanes, gmat, gn_g, gn_b)


def _lambda_full(lam_ref, lam_init):
    lam = lam_ref[...]
    a = jnp.sum(lam[0:1, :] * lam[1:2, :], -1, keepdims=True)
    b = jnp.sum(lam[2:3, :] * lam[3:4, :], -1, keepdims=True)
    return jnp.exp(a) - jnp.exp(b) + lam_init


def _diff_heads(q_of, k_parts_of, v_parts_of, lam, subln, lam_init, group):
    res = []
    for h0 in range(0, H_DIFF, group):
        res += _diff_head_group(range(h0, h0 + group), q_of, k_parts_of, v_parts_of, lam, subln, lam_init)
    return res


def _diff_head_group(heads, q_of, k_parts_of, v_parts_of, lam, subln, lam_init):
    subs = [(h, e) for h in heads for e in range(2)]
    scores = {}
    for h, e in subs:
        q = q_of(h)
        fm = _lane_half_mask(q.shape)
        q_sub = jnp.where(fm if e == 0 else jnp.logical_not(fm), q, jnp.zeros_like(q))
        scores[h, e] = [_dot(q_sub, k) if transposed else _dot_nt(q_sub, k) for k, transposed in k_parts_of(h)]
    probs = {}
    for key in subs:
        m = scores[key][0].max(-1, keepdims=True)
        for sc in scores[key][1:]:
            m = jnp.maximum(m, sc.max(-1, keepdims=True))
        es = [jnp.exp2(sc - m) for sc in scores[key]]
        denom = es[0].sum(-1, keepdims=True)
        for ex in es[1:]:
            denom = denom + ex.sum(-1, keepdims=True)
        probs[key] = ([ex.astype(bf16) for ex in es], denom)
    outs = {}
    for h, e in subs:
        es, denom = probs[h, e]
        pv = functools.reduce(lambda x, y: x + y, [_dot(ex, v) for v, ex in zip(v_parts_of(h), es)])
        outs[h, e] = pv / denom
    res = []
    for h in heads:
        a = outs[h, 0] - lam * outs[h, 1]
        res.append(a * lax.rsqrt(jnp.mean(a * a, -1, keepdims=True) + LN_EPS) * subln * (1.0 - lam_init))
    return res


def _fourier_rows(ct_ref, st_ref, z, bdc_ref, bds_ref):
    zc = _dot(z, bdc_ref[...].astype(bf16)).astype(bf16)
    zs = _dot(z, bds_ref[...].astype(bf16)).astype(bf16)
    return _dot(ct_ref[...].astype(bf16), zc) - _dot(st_ref[...].astype(bf16), zs)


def _ctx_cd_kernel(q_ref, k_ref, v_ref, z_ref, lam_ref, subln_ref, ct_ref, st_ref, bdc_ref, bds_ref,
                   a_ref, zf_ref, *, lam_init):
    lam = _lambda_full(lam_ref, lam_init)
    for sq in range(CTX_SEQS_CD):
        rows = slice(sq * SEQ, (sq + 1) * SEQ)
        sl = lambda h: slice(h * PAIR_W, (h + 1) * PAIR_W)
        heads = _diff_heads(lambda h: q_ref[rows, sl(h)], lambda h: [(k_ref[rows, sl(h)], False)],
                            lambda h: [v_ref[rows, sl(h)]], lam, subln_ref[...], lam_init, DIFF_GROUP)
        for h in range(H_DIFF):
            a_ref[rows, sl(h)] = heads[h].astype(bf16)
        zf_ref[rows, :] = _fourier_rows(ct_ref, st_ref, z_ref[rows, :], bdc_ref, bds_ref).astype(bf16)


def _ctx_cd(proj, lam, subln, ct, st, bdc, bds, lam_init):
    t = SEQ
    tb = CTX_SEQS_CD * t
    const = lambda b: (0, 0)
    col = lambda c: (lambda b: (b, c))
    return pl.pallas_call(
        functools.partial(_ctx_cd_kernel, lam_init=lam_init),
        grid=(BATCH // CTX_SEQS_CD,),
        in_specs=[pl.BlockSpec((tb, DIFF_W), col(0)), pl.BlockSpec((tb, DIFF_W), col(1)),
                  pl.BlockSpec((tb, DIFF_W), col(2)), pl.BlockSpec((tb, FNET_W), col(3 * DIFF_W // FNET_W)),
                  pl.BlockSpec((4, HEAD_DIM), const), pl.BlockSpec((1, PAIR_W), const),
                  pl.BlockSpec((t, t), const), pl.BlockSpec((t, t), const),
                  pl.BlockSpec((FNET_W, FNET_W), const), pl.BlockSpec((FNET_W, FNET_W), const)],
        out_specs=[pl.BlockSpec((tb, DIFF_W), lambda b: (b, 0)), pl.BlockSpec((tb, FNET_W), lambda b: (b, 0))],
        out_shape=[jax.ShapeDtypeStruct((BATCH * t, DIFF_W), bf16),
                   jax.ShapeDtypeStruct((BATCH * t, FNET_W), bf16)],
        compiler_params=_params(1),
        name="ctx_cd",
    )(proj, proj, proj, proj, lam, subln, ct, st, bdc, bds)


def _lat_cd_kernel(q_ref, k_ref, v_ref, z_ref, ckt_ref, cv_ref, lam_ref, subln_ref, ct_ref, st_ref, bdc_ref, bds_ref,
                   a_ref, zf_ref, *, lam_init):
    lam = _lambda_full(lam_ref, lam_init)
    sl = lambda h: slice(h * PAIR_W, (h + 1) * PAIR_W)
    heads = _diff_heads(lambda h: q_ref[:, sl(h)],
                        lambda h: [(k_ref[:, sl(h)], False), (ckt_ref[0, h].astype(bf16), True)],
                        lambda h: [v_ref[:, sl(h)], cv_ref[0, h].astype(bf16)], lam, subln_ref[...], lam_init,
                        DIFF_GROUP_LAT)
    for h in range(H_DIFF):
        a_ref[:, sl(h)] = heads[h].astype(bf16)
    zf_ref[...] = _fourier_rows(ct_ref, st_ref, z_ref[...], bdc_ref, bds_ref).astype(bf16)


def _lat_cd(proj, ck, cv, lam, subln, ct, st, bdc, bds, lam_init):
    t = DEC_SEQ
    nq = t // TQ_CD
    const = lambda b, i: (0, 0)
    return pl.pallas_call(
        functools.partial(_lat_cd_kernel, lam_init=lam_init),
        grid=(DEC_BATCH, nq),
        in_specs=[pl.BlockSpec((TQ_CD, DIFF_W), lambda b, i: (N_CTX // TQ_CD + b * nq + i, 0)),
                  pl.BlockSpec((t, DIFF_W), lambda b, i: (N_CTX // t + b, 1)),
                  pl.BlockSpec((t, DIFF_W), lambda b, i: (N_CTX // t + b, 2)),
                  pl.BlockSpec((t, FNET_W), lambda b, i: (N_CTX // t + b, 3 * DIFF_W // FNET_W)),
                  pl.BlockSpec((1, H_DIFF, PAIR_W, PAST_LEN), lambda b, i: (b, 0, 0, 0)),
                  pl.BlockSpec((1, H_DIFF, PAST_LEN, PAIR_W), lambda b, i: (b, 0, 0, 0)),
                  pl.BlockSpec((4, HEAD_DIM), const), pl.BlockSpec((1, PAIR_W), const),
                  pl.BlockSpec((TQ_CD, t), lambda b, i: (i, 0)), pl.BlockSpec((TQ_CD, t), lambda b, i: (i, 0)),
                  pl.BlockSpec((FNET_W, FNET_W), const), pl.BlockSpec((FNET_W, FNET_W), const)],
        out_specs=[pl.BlockSpec((TQ_CD, DIFF_W), lambda b, i: (b * nq + i, 0)),
                   pl.BlockSpec((TQ_CD, FNET_W), lambda b, i: (b * nq + i, 0))],
        out_shape=[jax.ShapeDtypeStruct((DEC_BATCH * t, DIFF_W), bf16),
                   jax.ShapeDtypeStruct((DEC_BATCH * t, FNET_W), bf16)],
        compiler_params=_params(2),
        name="lat_cd",
    )(proj, proj, proj, proj, ck, cv, lam, subln, ct, st, bdc, bds)


def _rope_tables():
    t = np.arange(DEC_SEQ)
    quarter = HEAD_DIM // 4
    inv = ROPE_BASE ** (-np.arange(quarter, dtype=np.float64) / quarter)
    ang = np.concatenate([(t // GRID_W)[:, None] * inv, (t % GRID_W)[:, None] * inv], -1)
    cos, sin = np.cos(ang), np.sin(ang)
    reps = LANES // HEAD_DIM
    return (np.tile(np.concatenate([cos, cos], -1), (1, reps)).astype(np.float32),
            np.tile(np.concatenate([-sin, sin], -1), (1, reps)).astype(np.float32))


def _dft_tables(n):
    k = np.arange(n)
    ang = (2.0 * math.pi / n) * ((k[:, None] * k[None, :]) % n)
    return (np.cos(ang) / math.sqrt(n)).astype(np.float32), (np.sin(ang) / math.sqrt(n)).astype(np.float32)


def _block_diag(m, reps):
    return np.kron(np.eye(reps, dtype=m.dtype), m)


def kernel(x_prompt, x_sample, state_ret, cache_win_k, cache_win_v, cache_diff_k, cache_diff_v, c, c_ctx, w_mod, b_mod, ln_g, ln_b, w_in_ab, w_out_ab, ret_log_gamma, ret_gn_g, ret_gn_b, win_sink, w_in_cd, w_out_cd, diff_lambda, diff_subln_g, w_gate, w_up, w_down):
    cond = jnp.concatenate([c_ctx[None, :], c, jnp.zeros((SUBLANES - 1 - DEC_BATCH, D_MODEL), f32)], 0)
    mod = _modulation(cond, w_mod, b_mod).reshape(DEPTH, SUBLANES, 6, D_MODEL)

    rope_tabs = _rope_tables()
    gmat = jnp.asarray(_block_diag(np.full((HEAD_DIM, HEAD_DIM), 1.0 / HEAD_DIM, np.float32),
                                   N_CHUNK // HEAD_DIM), bf16)
    c64, s64 = _dft_tables(FNET_DIM)
    bdc = _block_diag(c64, FNET_GROUPS)
    bds = _block_diag(s64, FNET_GROUPS)
    dft_ctx = _dft_tables(SEQ)
    dft_lat = _dft_tables(DEC_SEQ)

    x_parts = [x_prompt.reshape(N_CTX, D_MODEL), x_sample.reshape(N_LAT, D_MODEL)]
    outs = {}
    for l in range(DEPTH):
        i = l // 2
        if l % 2 == 0:
            lgf = jnp.repeat(ret_log_gamma[i, 0], HEAD_DIM)[None, :]
            lgb = jnp.repeat(ret_log_gamma[i, 1], HEAD_DIM)[None, :]
            gn_g = ret_gn_g[i][None, :]
            gn_b = ret_gn_b[i][None, :]
            sink = win_sink[i][None, :]
            rope_tiles = tuple(range(0, 2 * RET_W // LANES)) + tuple(
                range(4 * RET_W // LANES, (4 * RET_W + WIN_W + KV_W) // LANES))
            kv_tile = (4 * RET_W + WIN_W) // LANES
            kv_shape = (BATCH, 1, KV_WIN, HEAD_DIM, SEQ)
            scale_tiles = tuple(range(4 * RET_W // LANES, (4 * RET_W + WIN_W) // LANES))
            proj, wk_t, wv_t = _proj(x_parts, mod, l, w_in_ab, i, scale_tiles, rope_tabs, rope_tiles,
                                     (kv_shape, kv_shape),
                                     {kv_tile: ("heads", 0, 0), kv_tile + 1: ("heads", 1, 0)})
            ro_c, wo_c, st_c = _ctx_ab(proj, ret_log_gamma[i], sink, lgf, lgb, gmat, gn_g, gn_b)
            ck = cache_win_k[:, i].reshape(DEC_BATCH, PAST_LEN, KV_W)
            cv = cache_win_v[:, i].reshape(DEC_BATCH, PAST_LEN, KV_W)
            ro_l, wo_l = _lat_ab(proj, ret_log_gamma[i], sink, ck, cv, state_ret, i, lgf, lgb, gmat, gn_g, gn_b)
            mix_a, mix_b, w_out = (ro_c, ro_l), (wo_c, wo_l), w_out_ab
            outs.setdefault('state', []).append(st_c[:, None])
            outs.setdefault('win_k', []).append(jnp.transpose(wk_t, (0, 1, 4, 2, 3)))
            outs.setdefault('win_v', []).append(jnp.transpose(wv_t, (0, 1, 4, 2, 3)))
        else:
            lam_init = 0.8 - 0.6 * math.exp(-0.3 * l)
            subln = diff_subln_g[i][None, :]
            rope_tiles = tuple(range(0, 2 * DIFF_W // LANES))
            plan = {}
            for h in range(H_DIFF):
                plan[DIFF_W // LANES + h] = ("pairs", 0, h)
                plan[2 * DIFF_W // LANES + h] = ("plain", 1, h)
            scale_tiles = tuple(range(0, DIFF_W // LANES))
            proj, dk_t, dv_h = _proj(
                x_parts, mod, l, w_in_cd, i, scale_tiles, rope_tabs, rope_tiles,
                ((BATCH, 1, H_DIFF, 2, HEAD_DIM, SEQ), (BATCH, 1, H_DIFF, SEQ, 2 * HEAD_DIM)), plan)
            a_c, z_c = _ctx_cd(proj, diff_lambda[i], subln, dft_ctx[0], dft_ctx[1], bdc, bds, lam_init)
            ck = jnp.transpose(cache_diff_k[:, i], (0, 2, 3, 4, 1)).reshape(DEC_BATCH, H_DIFF, PAIR_W, PAST_LEN)
            cv = jnp.transpose(cache_diff_v[:, i], (0, 2, 1, 3))
            a_l, z_l = _lat_cd(proj, ck, cv, diff_lambda[i], subln, dft_lat[0], dft_lat[1], bdc, bds, lam_init)
            mix_a, mix_b, w_out = (a_c, a_l), (z_c, z_l), w_out_cd
            outs.setdefault('diff_k', []).append(jnp.transpose(dk_t, (0, 1, 5, 2, 3, 4)))
            outs.setdefault('diff_v', []).append(jnp.transpose(dv_h, (0, 1, 3, 2, 4)))
        x_parts = _post(x_parts, mix_a, mix_b, mod, ln_g, ln_b, w_out, w_gate, w_up, w_down, l, i,
                        split_out=(l == DEPTH - 1))

    y_prompt = x_parts[0].reshape(BATCH, SEQ, D_MODEL)
    y_sample = x_parts[1].reshape(DEC_BATCH, DEC_SEQ, D_MODEL)
    cat = lambda parts: parts[0] if len(parts) == 1 else jnp.concatenate(parts, 1)
    return (y_prompt, y_sample, cat(outs['state']), cat(outs['win_k']), cat(outs['win_v']),
            cat(outs['diff_k']), cat(outs['diff_v']))
```

```python
import functools
import math

import jax
import jax.numpy as jnp
import numpy as np
from jax import lax
from jax.experimental import pallas as pl
from jax.experimental.pallas import tpu as pltpu

D_MODEL = 1024
BATCH = 32
SEQ = 256
DEPTH = 2
DEC_BATCH = 2
DEC_SEQ = 1024
PAST_LEN = 512
GRID_W = 64
HEAD_DIM = 64
ROPE_BASE = 10000.0
H_RET = 8
H_WIN = 8
KV_WIN = 2
G_WIN = H_WIN // KV_WIN
WINDOW = 128
H_DIFF = 6
FNET_GROUPS = 4
FNET_DIM = 64
D_FF = 256 * math.ceil(8 * D_MODEL / 3 / 256)
RET_W = H_RET * HEAD_DIM
WIN_W = H_WIN * HEAD_DIM
KV_W = KV_WIN * HEAD_DIM
AB_IN = 4 * RET_W + WIN_W + 2 * KV_W
DIFF_W = H_DIFF * 2 * HEAD_DIM
FNET_W = FNET_GROUPS * FNET_DIM
CD_IN = 3 * DIFF_W + FNET_W
ALPHA = (2 * DEPTH) ** 0.25
LN_EPS = 1e-5
QK_SCALE = HEAD_DIM ** -0.5
LOG2_E = math.log2(math.e)

N_CTX = BATCH * SEQ
N_LAT = DEC_BATCH * DEC_SEQ
N_TOK = N_CTX + N_LAT

LANES = 128
SUBLANES = 8
PAIR_W = 2 * HEAD_DIM
TM = 512
CTX_BLOCKS = N_CTX // TM
TOK_BLOCKS = N_TOK // TM
ROW_GROUPS = 2
FFN_SKEW = 2
TQ = 256
TQ_CD = 512
CTX_SEQS = 2
CTX_SEQS_CD = 4
RET_GROUP = 2
WIN_GROUP = 2
WIN_GROUP_LAT = 2
DIFF_GROUP_LAT = 2
DIFF_GROUP = 2
N_CHUNK = 256
MOD_TN = 1536
NEG_BIG = -1e30
VMEM_LIMIT = 56 * 1024 * 1024

f32 = jnp.float32
bf16 = jnp.bfloat16


def _params(n_axes):
    return pltpu.CompilerParams(dimension_semantics=("arbitrary",) * n_axes,
                                vmem_limit_bytes=VMEM_LIMIT)


def _dot(a, b):
    return jnp.dot(a, b, preferred_element_type=f32)


def _dot_nt(a, b):
    return lax.dot_general(a, b, (((1,), (1,)), ((), ())), preferred_element_type=f32)


def _ln(x):
    mu = jnp.mean(x, -1, keepdims=True)
    d = x - mu
    var = jnp.mean(d * d, -1, keepdims=True)
    return d * lax.rsqrt(var + LN_EPS)


def _silu(x):
    return x * jax.nn.sigmoid(x)


def _split_bf16(x):
    hi = x.astype(bf16)
    lo = (x - hi.astype(f32)).astype(bf16)
    return hi, lo


def _lane_half_mask(shape):
    return (lax.broadcasted_iota(jnp.int32, shape, len(shape) - 1) & HEAD_DIM) == 0


def _mod_kernel(c_ref, w_ref, b_ref, o_ref):
    layer = pl.program_id(0)
    a = _silu(c_ref[...])
    a_hi, a_lo = _split_bf16(a)
    w_hi, w_lo = _split_bf16(w_ref[0])
    acc = _dot(a_hi, w_hi) + _dot(a_lo, w_hi) + _dot(a_hi, w_lo)
    o_ref[0] = acc + b_ref[pl.ds(layer, 1), :]


def _modulation(cond, w_mod, b_mod):
    tn = MOD_TN
    rows = cond.shape[0]
    return pl.pallas_call(
        _mod_kernel,
        grid=(DEPTH, 6 * D_MODEL // tn),
        in_specs=[pl.BlockSpec((rows, D_MODEL), lambda l, j: (0, 0)),
                  pl.BlockSpec((1, D_MODEL, tn), lambda l, j: (l, 0, j)),
                  pl.BlockSpec((DEPTH, tn), lambda l, j: (0, j))],
        out_specs=pl.BlockSpec((1, rows, tn), lambda l, j: (l, 0, j)),
        out_shape=jax.ShapeDtypeStruct((DEPTH, rows, 6 * D_MODEL), f32),
        compiler_params=_params(2),
        name="modulation",
    )(cond, w_mod, b_mod)


def _tok(i, n_w):
    return jnp.maximum(i - n_w, 0)


def _ctx_blk(t):
    return jnp.minimum(t, CTX_BLOCKS - 1)


def _lat_blk(t):
    return jnp.maximum(t - CTX_BLOCKS, 0)


def _mod_row(t):
    return jnp.where(t < CTX_BLOCKS, 0, 1 + _lat_blk(t) * TM // DEC_SEQ)


def _token_specs(parts, n_w):
    width = parts[0].shape[1]
    if len(parts) == 1:
        return [pl.BlockSpec((TM, width), lambda i: (_tok(i, n_w), 0))]
    return [pl.BlockSpec((TM, width), lambda i: (_ctx_blk(_tok(i, n_w)), 0)),
            pl.BlockSpec((TM, width), lambda i: (_lat_blk(_tok(i, n_w)), 0))]


def _pick(refs, is_ctx, rs):
    if len(refs) == 1:
        return refs[0][rs, :]
    if isinstance(is_ctx, bool):
        return refs[0 if is_ctx else 1][rs, :]
    return jnp.where(is_ctx, refs[0][rs, :], refs[1][rs, :])


def _rope_pair(y, cos, sin_signed):
    first_half = (lax.broadcasted_iota(jnp.int32, y.shape, 1) & (HEAD_DIM // 2)) == 0
    swapped = jnp.where(first_half, pltpu.roll(y, LANES - HEAD_DIM // 2, 1), pltpu.roll(y, HEAD_DIM // 2, 1))
    return y * cos + swapped * sin_signed


def _proj_kernel(*refs, n_x, n_cache, n_w, rope_tiles, scale_tiles, cache_plan):
    x_refs = refs[:n_x]
    mod_ref, w_ref, cos_ref, sin_ref, o_ref = refs[n_x:n_x + 5]
    cache_refs = refs[n_x + 5:n_x + 5 + n_cache]
    wbf_ref, u_ref = refs[n_x + 5 + n_cache:]
    i = pl.program_id(0)

    @pl.when(i < n_w)
    def _():
        wbf_ref[i] = w_ref[0].astype(bf16)

    def tokens(is_ctx):
        x_ref = x_refs[0] if is_ctx else x_refs[-1]
        shift = mod_ref[0, 0:1, :]
        scale = mod_ref[0, 1:2, :]
        groups = [slice(b * SEQ, (b + 1) * SEQ) for b in range(TM // SEQ)]
        for rs in groups:
            u_ref[rs, :] = (_ln(x_ref[rs, :]) * (1.0 + scale) + shift).astype(bf16)
        for c in range(n_w):
            y_all = _dot(u_ref[...], wbf_ref[c])
            for b, rs in enumerate(groups):
                y = y_all[rs, :]
                for t in range(N_CHUNK // LANES):
                    tile = c * (N_CHUNK // LANES) + t
                    piece = y[:, t * LANES:(t + 1) * LANES]
                    if tile in rope_tiles and not is_ctx:
                        piece = _rope_pair(piece, cos_ref[rs, :], sin_ref[rs, :])
                    if tile in scale_tiles:
                        piece = piece * (QK_SCALE * LOG2_E)
                    o_ref[rs, tile * LANES:(tile + 1) * LANES] = piece.astype(o_ref.dtype)
                    if tile in cache_plan and is_ctx:
                        kind, out_idx, slot = cache_plan[tile]
                        c_ref = cache_refs[out_idx]
                        if kind == "plain":
                            c_ref[b, 0, slot] = piece
                        else:
                            piece_t = piece.T
                            if kind == "heads":
                                c_ref[b, 0, 0] = piece_t[0:HEAD_DIM]
                                c_ref[b, 0, 1] = piece_t[HEAD_DIM:]
                            else:
                                c_ref[b, 0, slot, 0] = piece_t[0:HEAD_DIM]
                                c_ref[b, 0, slot, 1] = piece_t[HEAD_DIM:]

    t = i - n_w

    @pl.when(jnp.logical_and(t >= 0, t < CTX_BLOCKS))
    def _():
        tokens(True)

    @pl.when(t >= CTX_BLOCKS)
    def _():
        tokens(False)


def _proj(x_parts, mod, mod_layer, w_all, layer, scale_tiles, rope_tabs, rope_tiles, cache_shapes, cache_plan):
    n_out = w_all.shape[2]
    n_w = n_out // N_CHUNK
    nb = DEC_SEQ // TM
    tok = lambda i: _tok(i, n_w)
    in_specs = _token_specs(x_parts, n_w) + [
        pl.BlockSpec((None, 1, 6, D_MODEL), lambda i: (mod_layer, _mod_row(tok(i)), 0, 0)),
        pl.BlockSpec((1, D_MODEL, N_CHUNK), lambda i: (layer, 0, jnp.minimum(i, n_w - 1))),
        pl.BlockSpec((TM, LANES), lambda i: (_lat_blk(tok(i)) % nb, 0)),
        pl.BlockSpec((TM, LANES), lambda i: (_lat_blk(tok(i)) % nb, 0))]
    out_specs = [pl.BlockSpec((TM, n_out), lambda i: (tok(i), 0))]
    out_shape = [jax.ShapeDtypeStruct((N_TOK, n_out), bf16)]
    for shp in cache_shapes:
        blk = (TM // SEQ,) + tuple(shp[1:])
        out_specs.append(pl.BlockSpec(blk, lambda i, nd=len(shp): (_ctx_blk(tok(i)),) + (0,) * (nd - 1)))
        out_shape.append(jax.ShapeDtypeStruct(tuple(shp), f32))
    return pl.pallas_call(
        functools.partial(_proj_kernel, n_x=len(x_parts), n_cache=len(cache_shapes), n_w=n_w,
                          rope_tiles=frozenset(rope_tiles), scale_tiles=frozenset(scale_tiles),
                          cache_plan=dict(cache_plan)),
        grid=(n_w + TOK_BLOCKS,),
        in_specs=in_specs,
        out_specs=out_specs,
        out_shape=out_shape,
        scratch_shapes=[pltpu.VMEM((n_w, D_MODEL, N_CHUNK), bf16), pltpu.VMEM((TM, D_MODEL), bf16)],
        compiler_params=_params(1),
        name="proj",
    )(*x_parts, mod, w_all, *rope_tabs)


def _post_kernel(*refs, n_x, n_y, ka, kb, n_w):
    x_refs = refs[:n_x]
    (ac_ref, al_ref, bc_ref, bl_ref, mod_ref, lng_ref, lnb_ref,
     wo_ref, wg_ref, wu_ref, wd_ref) = refs[n_x:n_x + 11]
    y_refs = refs[n_x + 11:n_x + 11 + n_y]
    wo_s, wg_s, wu_s, wd_s, x1_ref, u_ref, h_ref, y_ref = refs[n_x + 11 + n_y:]
    i = pl.program_id(0)
    lead = n_w - 1
    gate1 = mod_ref[0, 2:3, :]
    shift2 = mod_ref[0, 3:4, :]
    scale2 = mod_ref[0, 4:5, :]
    gate2 = mod_ref[0, 5:6, :]
    groups = [slice(r * TM // ROW_GROUPS, (r + 1) * TM // ROW_GROUPS) for r in range(ROW_GROUPS)]

    def mix_in(rs, is_ctx):
        a = _pick((ac_ref, al_ref), is_ctx, rs)
        b = _pick((bc_ref, bl_ref), is_ctx, rs)
        pieces = ([a[:, c:c + N_CHUNK] for c in range(0, ka, N_CHUNK)]
                  + [b[:, c:c + N_CHUNK] for c in range(0, kb, N_CHUNK)])
        h = functools.reduce(lambda s, p: s + p, [_dot(p, wo_s[c]) for c, p in enumerate(pieces)])
        x1 = _ln(ALPHA * _pick(x_refs, is_ctx, rs) + gate1 * h) * lng_ref[0, 0:1, :] + lnb_ref[0, 0:1, :]
        x1_ref[rs, :] = x1
        u_ref[rs, :] = (_ln(x1) * (1.0 + scale2) + shift2).astype(bf16)

    def finish(rs, ffn):
        y_ref[rs, :] = _ln(ALPHA * x1_ref[rs, :] + gate2 * ffn) * lng_ref[0, 1:2, :] + lnb_ref[0, 1:2, :]

    def emit(is_ctx):
        if n_y == 1:
            y_refs[0][...] = y_ref[...]
        elif isinstance(is_ctx, bool):
            y_refs[0 if is_ctx else 1][...] = y_ref[...]
        else:
            @pl.when(is_ctx)
            def _():
                y_refs[0][...] = y_ref[...]

            @pl.when(jnp.logical_not(is_ctx))
            def _():
                y_refs[1][...] = y_ref[...]

    @pl.when(i < n_w)
    def _():
        wg_s[i] = wg_ref[0].astype(bf16)
        wu_s[i] = wu_ref[0].astype(bf16)
        wd_s[i] = wd_ref[0].astype(bf16)

        @pl.when(i == 0)
        def _():
            for c in range((ka + kb) // N_CHUNK):
                wo_s[c] = wo_ref[0, c * N_CHUNK:(c + 1) * N_CHUNK, :].astype(bf16)
            for rs in groups:
                mix_in(rs, True)
                y_ref[rs, :] = jnp.zeros((TM // ROW_GROUPS, D_MODEL), f32)

        for rs in groups:
            g = _dot(u_ref[rs, :], wg_s[i])
            up = _dot(u_ref[rs, :], wu_s[i])
            y_ref[rs, :] += _dot((_silu(g) * up).astype(bf16), wd_s[i])

        @pl.when(i == lead)
        def _():
            for rs in groups:
                finish(rs, y_ref[rs, :])
            emit(True)

    @pl.when(i >= n_w)
    def _():
        is_ctx = (i - lead) < CTX_BLOCKS
        for rs in groups:
            mix_in(rs, is_ctx)

        def ffn_chunk(rs, c):
            g = _dot(u_ref[rs, :], wg_s[c])
            up = _dot(u_ref[rs, :], wu_s[c])
            h_ref[rs, c * N_CHUNK:(c + 1) * N_CHUNK] = (_silu(g) * up).astype(bf16)

        def ffn_down(rs):
            finish(rs, functools.reduce(
                lambda s, p: s + p,
                [_dot(h_ref[rs, c * N_CHUNK:(c + 1) * N_CHUNK], wd_s[c]) for c in range(n_w)]))

        for c in range(n_w + FFN_SKEW * (ROW_GROUPS - 1)):
            for r, rs in enumerate(groups):
                cc = c - FFN_SKEW * r
                if 0 <= cc < n_w:
                    ffn_chunk(rs, cc)
                if cc == n_w - 1:
                    ffn_down(rs)
        emit(is_ctx)


def _post(x_parts, mix_a, mix_b, mod, ln_g, ln_b, w_out, w_gate, w_up, w_down, layer, mix_layer, split_out):
    ka, kb = mix_a[0].shape[1], mix_b[0].shape[1]
    n_w = D_FF // N_CHUNK
    lead = n_w - 1
    tok = lambda i: _tok(i, lead)
    lay = lambda i: (layer, 0, 0)
    in_specs = (_token_specs(x_parts, lead) + _token_specs(mix_a, lead) + _token_specs(mix_b, lead) + [
        pl.BlockSpec((None, 1, 6, D_MODEL), lambda i: (layer, _mod_row(tok(i)), 0, 0)),
        pl.BlockSpec((1, 2, D_MODEL), lay),
        pl.BlockSpec((1, 2, D_MODEL), lay),
        pl.BlockSpec((1, ka + kb, D_MODEL), lambda i: (mix_layer, 0, 0), pipeline_mode=pl.Buffered(1)),
        pl.BlockSpec((1, D_MODEL, N_CHUNK), lambda i: (layer, 0, jnp.minimum(i, n_w - 1))),
        pl.BlockSpec((1, D_MODEL, N_CHUNK), lambda i: (layer, 0, jnp.minimum(i, n_w - 1))),
        pl.BlockSpec((1, N_CHUNK, D_MODEL), lambda i: (layer, jnp.minimum(i, n_w - 1), 0))])
    if split_out:
        out_specs = [pl.BlockSpec((TM, D_MODEL), lambda i: (_ctx_blk(tok(i)), 0)),
                     pl.BlockSpec((TM, D_MODEL), lambda i: (_lat_blk(tok(i)), 0))]
        out_shape = [jax.ShapeDtypeStruct((N_CTX, D_MODEL), f32), jax.ShapeDtypeStruct((N_LAT, D_MODEL), f32)]
    else:
        out_specs = [pl.BlockSpec((TM, D_MODEL), lambda i: (tok(i), 0))]
        out_shape = [jax.ShapeDtypeStruct((N_TOK, D_MODEL), f32)]
    return pl.pallas_call(
        functools.partial(_post_kernel, n_x=len(x_parts), n_y=len(out_shape), ka=ka, kb=kb, n_w=n_w),
        grid=(lead + TOK_BLOCKS,),
        in_specs=in_specs,
        out_specs=out_specs,
        out_shape=out_shape,
        scratch_shapes=[pltpu.VMEM(((ka + kb) // N_CHUNK, N_CHUNK, D_MODEL), bf16),
                        pltpu.VMEM((n_w, D_MODEL, N_CHUNK), bf16),
                        pltpu.VMEM((n_w, D_MODEL, N_CHUNK), bf16), pltpu.VMEM((n_w, N_CHUNK, D_MODEL), bf16),
                        pltpu.VMEM((TM, D_MODEL), f32), pltpu.VMEM((TM, D_MODEL), bf16),
                        pltpu.VMEM((TM, D_FF), bf16), pltpu.VMEM((TM, D_MODEL), f32)],
        compiler_params=_params(1),
        name="post",
    )(*x_parts, *mix_a, *mix_b, mod, ln_g, ln_b, w_out, w_gate, w_up, w_down)


def _group_norm_gate(ro, rg, gmat, gn_g, gn_b):
    def gmean(parts):
        cols = []
        for c in range(0, RET_W, N_CHUNK):
            cols.append(sum(_dot(p[:, c:c + N_CHUNK], gmat) for p in parts))
        return jnp.concatenate(cols, -1)

    d = ro - gmean(_split_bf16(ro))
    var = gmean([(d * d).astype(bf16)])
    y = d * lax.rsqrt(var + LN_EPS) * gn_g + gn_b
    return _silu(rg.astype(f32)) * y


def _dup_head(x, j):
    first = _lane_half_mask(x.shape)
    keep = first if j == 0 else jnp.logical_not(first)
    xm = jnp.where(keep, x.astype(f32), 0.0)
    return xm + pltpu.roll(xm, HEAD_DIM, 1)


def _softmax_parts(scores, sink):
    m = sink
    for s in scores:
        m = jnp.maximum(m, jnp.max(s, -1, keepdims=True))
    es = [jnp.exp2(s - m) for s in scores]
    denom = jnp.exp2(sink - m)
    for e in es:
        denom = denom + jnp.sum(e, -1, keepdims=True)
    return es, denom


def _retention_tables(lg_ref, lgf_ref, lgb_ref, dmask_ref, kdec_ref, n):
    row = lax.broadcasted_iota(jnp.int32, (n, n), 0)
    col = lax.broadcasted_iota(jnp.int32, (n, n), 1)
    diff = (row - col).astype(f32)
    diag = jnp.where(row == col, 2.0 * QK_SCALE, QK_SCALE)
    for h in range(H_RET):
        dmask_ref[h] = jnp.exp(jnp.where(diff >= 0, lg_ref[0, h] * diff, -lg_ref[1, h] * diff)) * diag
    t = lax.broadcasted_iota(jnp.int32, (n, RET_W), 0).astype(f32)
    kdec_ref[0] = jnp.exp(lgf_ref[...] * (n - 1.0 - t)) * QK_SCALE
    kdec_ref[1] = jnp.exp(lgb_ref[...] * t) * QK_SCALE


def _retention_intra(pairs, q_of, k_of, v_of, dmask_ref):
    first = _lane_half_mask(k_of(pairs[0]).shape)
    masked = {}
    for p in pairs:
        kb = k_of(p)
        for e in range(2):
            keep = first if e == 0 else jnp.logical_not(first)
            s = _dot_nt(q_of(p), jnp.where(keep, kb, jnp.zeros_like(kb))) * dmask_ref[2 * p + e]
            masked[p, e] = s.astype(bf16)
    outs = {}
    for p in pairs:
        pv = [_dot(masked[p, e], v_of(p)) for e in range(2)]
        outs[p] = jnp.where(_lane_half_mask(pv[0].shape), pv[0], pv[1])
    return outs


def _window_group(subs, q_of, k_parts_of, v_parts_of, masks, sink_of):
    scores = {}
    for key in subs:
        parts = [_dot_nt(q_of(key), k) for k in k_parts_of(key)]
        scores[key] = [sc if mk is None else jnp.where(mk, sc, NEG_BIG) for sc, mk in zip(parts, masks)]
    probs = {}
    for key in subs:
        es, denom = _softmax_parts(scores[key], sink_of(key))
        probs[key] = ([ex.astype(bf16) for ex in es], denom)
    outs = {}
    for key in subs:
        es, denom = probs[key]
        pv = functools.reduce(lambda x, y: x + y, [_dot(ex, v) for ex, v in zip(es, v_parts_of(key))])
        outs[key] = pv / denom
    return outs


def _ctx_ab_kernel(lg_ref, sink_ref, rq_ref, rk_ref, rv_ref, rg_ref, wq_ref, wk_ref, wv_ref,
                   lgf_ref, lgb_ref, gmat_ref, gng_ref, gnb_ref,
                   ro_ref, wo_ref, st_ref, dmask_ref, kdec_ref, ret_ref):
    t_len = SEQ

    @pl.when(pl.program_id(0) == 0)
    def _():
        _retention_tables(lg_ref, lgf_ref, lgb_ref, dmask_ref, kdec_ref, t_len)

    first = _lane_half_mask((t_len, PAIR_W))
    for sq in range(CTX_SEQS):
        rows = slice(sq * t_len, (sq + 1) * t_len)
        psl = lambda p: slice(p * PAIR_W, (p + 1) * PAIR_W)
        for p0 in range(0, H_RET // 2, RET_GROUP):
            pairs = list(range(p0, p0 + RET_GROUP))
            intra = _retention_intra(pairs, lambda p: rq_ref[rows, psl(p)], lambda p: rk_ref[rows, psl(p)],
                                     lambda p: rv_ref[rows, psl(p)], dmask_ref)
            for p in pairs:
                ret_ref[rows, psl(p)] = intra[p]
        for p in range(H_RET // 2):
            sl = psl(p)
            kb = rk_ref[rows, sl]
            v = rv_ref[rows, sl]
            for d in range(2):
                kd_t = (kb * kdec_ref[d, :, sl]).T.astype(bf16)
                st = _dot(kd_t, v)
                st_ref[sq, d, 2 * p] = st[0:HEAD_DIM, 0:HEAD_DIM]
                st_ref[sq, d, 2 * p + 1] = pltpu.roll(st[HEAD_DIM:, :], HEAD_DIM, 1)[:, 0:HEAD_DIM]
        ro_ref[rows, :] = _group_norm_gate(ret_ref[rows, :], rg_ref[rows, :], gmat_ref[...], gng_ref[...],
                                           gnb_ref[...]).astype(bf16)

        k_dup = [_dup_head(wk_ref[rows, :], j).astype(bf16) for j in range(KV_WIN)]
        v_dup = [_dup_head(wv_ref[rows, :], j).astype(bf16) for j in range(KV_WIN)]

        def q_masked(key):
            qp, e = key
            qb = wq_ref[rows, qp * PAIR_W:(qp + 1) * PAIR_W]
            return jnp.where(first if e == 0 else jnp.logical_not(first), qb, jnp.zeros_like(qb))

        kv_of = lambda key: key[0] * 2 // G_WIN
        for g0 in range(0, H_WIN // 2, WIN_GROUP):
            subs = [(qp, e) for qp in range(g0, g0 + WIN_GROUP) for e in range(2)]
            outs = _window_group(subs, q_masked, lambda key: [k_dup[kv_of(key)]], lambda key: [v_dup[kv_of(key)]],
                                 [None], lambda key: sink_ref[0, 2 * key[0] + key[1]] * LOG2_E)
            for qp in range(g0, g0 + WIN_GROUP):
                wo_ref[rows, qp * PAIR_W:(qp + 1) * PAIR_W] = jnp.where(first, outs[qp, 0], outs[qp, 1]).astype(bf16)


def _ctx_ab(proj, log_gamma, sink, lgf_lanes, lgb_lanes, gmat, gn_g, gn_b):
    t = SEQ
    tb = CTX_SEQS * t
    smem = pl.BlockSpec(memory_space=pltpu.SMEM)
    const = lambda b: (0, 0)
    col = lambda c: (lambda b: (b, c))
    return pl.pallas_call(
        _ctx_ab_kernel,
        grid=(BATCH // CTX_SEQS,),
        in_specs=[smem, smem,
                  pl.BlockSpec((tb, RET_W), col(0)), pl.BlockSpec((tb, RET_W), col(1)),
                  pl.BlockSpec((tb, RET_W), col(2)), pl.BlockSpec((tb, RET_W), col(3)),
                  pl.BlockSpec((tb, WIN_W), col(4)),
                  pl.BlockSpec((tb, KV_W), col((4 * RET_W + WIN_W) // KV_W)),
                  pl.BlockSpec((tb, KV_W), col((4 * RET_W + WIN_W) // KV_W + 1)),
                  pl.BlockSpec((1, RET_W), const), pl.BlockSpec((1, RET_W), const),
                  pl.BlockSpec((N_CHUNK, N_CHUNK), const),
                  pl.BlockSpec((1, RET_W), const), pl.BlockSpec((1, RET_W), const)],
        out_specs=[pl.BlockSpec((tb, RET_W), lambda b: (b, 0)),
                   pl.BlockSpec((tb, WIN_W), lambda b: (b, 0)),
                   pl.BlockSpec((CTX_SEQS, 2, H_RET, HEAD_DIM, HEAD_DIM), lambda b: (b, 0, 0, 0, 0))],
        out_shape=[jax.ShapeDtypeStruct((BATCH * t, RET_W), bf16),
                   jax.ShapeDtypeStruct((BATCH * t, WIN_W), bf16),
                   jax.ShapeDtypeStruct((BATCH, 2, H_RET, HEAD_DIM, HEAD_DIM), f32)],
        scratch_shapes=[pltpu.VMEM((H_RET, t, t), f32), pltpu.VMEM((2, t, RET_W), f32),
                        pltpu.VMEM((tb, RET_W), f32)],
        compiler_params=_params(1),
        name="ctx_ab",
    )(log_gamma, sink, proj, proj, proj, proj, proj, proj, proj, lgf_lanes, lgb_lanes, gmat, gn_g, gn_b)


def _pair_state(s0_ref, d, p):
    zero = jnp.zeros((HEAD_DIM, HEAD_DIM), f32)
    top = jnp.concatenate([s0_ref[0, 0, d, 2 * p], zero], 1)
    bottom = jnp.concatenate([zero, s0_ref[0, 0, d, 2 * p + 1]], 1)
    return jnp.concatenate([top, bottom], 0)


def _lat_ab_kernel(lg_ref, sink_ref, rq_ref, rk_ref, rv_ref, rg_ref, wq_ref, wk_ref, wv_ref, ck_ref, cv_ref,
                   s0_ref, lgf_ref, lgb_ref, gmat_ref, gng_ref, gnb_ref,
                   ro_ref, wo_ref, ret_ref, dmask_ref, kdec_ref, qdec_ref, sf_ref, sb_ref):
    t_len = DEC_SEQ
    n_chunks = t_len // TQ
    chunk = pl.program_id(1)
    q0 = pl.multiple_of(chunk * TQ, TQ)
    first = _lane_half_mask((TQ, PAIR_W))

    @pl.when(jnp.logical_and(pl.program_id(0) == 0, chunk == 0))
    def _():
        _retention_tables(lg_ref, lgf_ref, lgb_ref, dmask_ref, kdec_ref, TQ)
        t = lax.broadcasted_iota(jnp.int32, (TQ, RET_W), 0).astype(f32)
        qdec_ref[0] = jnp.exp(lgf_ref[...] * (t + 1.0))
        qdec_ref[1] = jnp.exp(lgb_ref[...] * (TQ - t))

    @pl.when(chunk == 0)
    def _():
        r = lax.broadcasted_iota(jnp.int32, (PAIR_W, PAIR_W), 0)
        c_ = lax.broadcasted_iota(jnp.int32, (PAIR_W, PAIR_W), 1)
        same_head = (r < HEAD_DIM) == (c_ < HEAD_DIM)
        for p in range(H_RET // 2):
            sl = slice(p * PAIR_W, (p + 1) * PAIR_W)
            kv = []
            for c in range(n_chunks):
                rows = slice(c * TQ, (c + 1) * TQ)
                kc = rk_ref[rows, sl]
                vc = rv_ref[rows, sl]
                kv.append([jnp.where(same_head, _dot((kc * kdec_ref[d, :, sl]).T.astype(bf16), vc), 0.0)
                           for d in range(2)])
            state = _pair_state(s0_ref, 0, p)
            for c in range(n_chunks):
                sf_ref[c, p] = state
                state = state * jnp.exp(lgf_ref[:, sl] * TQ) + kv[c][0]
            state = _pair_state(s0_ref, 1, p)
            for c in reversed(range(n_chunks)):
                sb_ref[c, p] = state
                state = state * jnp.exp(lgb_ref[:, sl] * TQ) + kv[c][1]

    psl = lambda p: slice(p * PAIR_W, (p + 1) * PAIR_W)
    intra = {}
    for p0 in range(0, H_RET // 2, RET_GROUP):
        intra.update(_retention_intra(list(range(p0, p0 + RET_GROUP)), lambda p: rq_ref[:, psl(p)],
                                      lambda p: rk_ref[pl.ds(q0, TQ), psl(p)],
                                      lambda p: rv_ref[pl.ds(q0, TQ), psl(p)], dmask_ref))
    for p in range(H_RET // 2):
        sl = psl(p)
        q = rq_ref[:, sl]
        o = intra[p]
        o = o + _dot(q, sf_ref[chunk, p].astype(bf16)) * qdec_ref[0, :, sl]
        o = o + _dot(q, sb_ref[chunk, p].astype(bf16)) * qdec_ref[1, :, sl]
        ret_ref[:, sl] = o
    ro_ref[...] = _group_norm_gate(ret_ref[...], rg_ref[...], gmat_ref[...], gng_ref[...], gnb_ref[...]).astype(bf16)

    band = TQ + 2 * WINDOW
    k_start = pl.multiple_of(jnp.clip(q0 - WINDOW, 0, t_len - band), LANES)
    qi = q0 + lax.broadcasted_iota(jnp.int32, (TQ, band), 0)
    kj = k_start + lax.broadcasted_iota(jnp.int32, (TQ, band), 1)
    in_band = jnp.abs(qi - kj) <= WINDOW
    k_parts = [[_dup_head(wk_ref[pl.ds(k_start, band), :], j).astype(bf16), _dup_head(ck_ref[0], j).astype(bf16)]
               for j in range(KV_WIN)]
    v_parts = [[_dup_head(wv_ref[pl.ds(k_start, band), :], j).astype(bf16), _dup_head(cv_ref[0], j).astype(bf16)]
               for j in range(KV_WIN)]

    def q_masked(key):
        qp, e = key
        qb = wq_ref[:, qp * PAIR_W:(qp + 1) * PAIR_W]
        return jnp.where(first if e == 0 else jnp.logical_not(first), qb, jnp.zeros_like(qb))

    kv_of = lambda key: key[0] * 2 // G_WIN
    for g0 in range(0, H_WIN // 2, WIN_GROUP_LAT):
        subs = [(qp, e) for qp in range(g0, g0 + WIN_GROUP_LAT) for e in range(2)]
        outs = _window_group(subs, q_masked, lambda key: k_parts[kv_of(key)], lambda key: v_parts[kv_of(key)],
                             [in_band, None], lambda key: sink_ref[0, 2 * key[0] + key[1]] * LOG2_E)
        for qp in range(g0, g0 + WIN_GROUP_LAT):
            wo_ref[:, qp * PAIR_W:(qp + 1) * PAIR_W] = jnp.where(first, outs[qp, 0], outs[qp, 1]).astype(bf16)


def _lat_ab(proj, log_gamma, sink, ck, cv, state, layer, lgf_lanes, lgb_lanes, gmat, gn_g, gn_b):
    t = DEC_SEQ
    nq = t // TQ
    smem = pl.BlockSpec(memory_space=pltpu.SMEM)
    const = lambda b, i: (0, 0)
    qcol = lambda c: (lambda b, i: (N_CTX // TQ + b * nq + i, c))
    bcol = lambda c: (lambda b, i: (N_CTX // t + b, c))
    kv_col = (4 * RET_W + WIN_W) // KV_W
    return pl.pallas_call(
        _lat_ab_kernel,
        grid=(DEC_BATCH, nq),
        in_specs=[smem, smem,
                  pl.BlockSpec((TQ, RET_W), qcol(0)), pl.BlockSpec((t, RET_W), bcol(1)),
                  pl.BlockSpec((t, RET_W), bcol(2)), pl.BlockSpec((TQ, RET_W), qcol(3)),
                  pl.BlockSpec((TQ, WIN_W), qcol(4)),
                  pl.BlockSpec((t, KV_W), bcol(kv_col)), pl.BlockSpec((t, KV_W), bcol(kv_col + 1)),
                  pl.BlockSpec((1, PAST_LEN, KV_W), lambda b, i: (b, 0, 0)),
                  pl.BlockSpec((1, PAST_LEN, KV_W), lambda b, i: (b, 0, 0)),
                  pl.BlockSpec((1, 1, 2, H_RET, HEAD_DIM, HEAD_DIM), lambda b, i: (b, layer, 0, 0, 0, 0)),
                  pl.BlockSpec((1, RET_W), const), pl.BlockSpec((1, RET_W), const),
                  pl.BlockSpec((N_CHUNK, N_CHUNK), const),
                  pl.BlockSpec((1, RET_W), const), pl.BlockSpec((1, RET_W), const)],
        out_specs=[pl.BlockSpec((TQ, RET_W), lambda b, i: (b * nq + i, 0)),
                   pl.BlockSpec((TQ, WIN_W), lambda b, i: (b * nq + i, 0))],
        out_shape=[jax.ShapeDtypeStruct((DEC_BATCH * t, RET_W), bf16),
                   jax.ShapeDtypeStruct((DEC_BATCH * t, WIN_W), bf16)],
        scratch_shapes=[pltpu.VMEM((TQ, RET_W), f32), pltpu.VMEM((H_RET, TQ, TQ), f32),
                        pltpu.VMEM((2, TQ, RET_W), f32), pltpu.VMEM((2, TQ, RET_W), f32),
                        pltpu.VMEM((t // TQ, H_RET // 2, PAIR_W, PAIR_W), f32),
                        pltpu.VMEM((t // TQ, H_RET // 2, PAIR_W, PAIR_W), f32)],
        compiler_params=_params(2),
        name="lat_ab",
    )(log_gamma, sink, proj, proj, proj, proj, proj, proj, proj, ck, cv, state,
      lgf_lanes, lgb_lanes, gmat, gn_g, gn_b)


def _lambda_full(lam_ref, lam_init):
    lam = lam_ref[...]
    a = jnp.sum(lam[0:1, :] * lam[1:2, :], -1, keepdims=True)
    b = jnp.sum(lam[2:3, :] * lam[3:4, :], -1, keepdims=True)
    return jnp.exp(a) - jnp.exp(b) + lam_init


def _diff_heads(q_of, k_parts_of, v_parts_of, lam, subln, lam_init, group):
    res = []
    for h0 in range(0, H_DIFF, group):
        res += _diff_head_group(range(h0, h0 + group), q_of, k_parts_of, v_parts_of, lam, subln, lam_init)
    return res


def _diff_head_group(heads, q_of, k_parts_of, v_parts_of, lam, subln, lam_init):
    subs = [(h, e) for h in heads for e in range(2)]
    scores = {}
    for h, e in subs:
        q = q_of(h)
        fm = _lane_half_mask(q.shape)
        q_sub = jnp.where(fm if e == 0 else jnp.logical_not(fm), q, jnp.zeros_like(q))
        scores[h, e] = [_dot(q_sub, k) if transposed else _dot_nt(q_sub, k) for k, transposed in k_parts_of(h)]
    probs = {}
    for key in subs:
        m = scores[key][0].max(-1, keepdims=True)
        for sc in scores[key][1:]:
            m = jnp.maximum(m, sc.max(-1, keepdims=True))
        es = [jnp.exp2(sc - m) for sc in scores[key]]
        denom = es[0].sum(-1, keepdims=True)
        for ex in es[1:]:
            denom = denom + ex.sum(-1, keepdims=True)
        probs[key] = ([ex.astype(bf16) for ex in es], denom)
    outs = {}
    for h, e in subs:
        es, denom = probs[h, e]
        pv = functools.reduce(lambda x, y: x + y, [_dot(ex, v) for v, ex in zip(v_parts_of(h), es)])
        outs[h, e] = pv / denom
    res = []
    for h in heads:
        a = outs[h, 0] - lam * outs[h, 1]
        res.append(a * lax.rsqrt(jnp.mean(a * a, -1, keepdims=True) + LN_EPS) * subln * (1.0 - lam_init))
    return res


def _fourier_rows(ct_ref, st_ref, z, bdc_ref, bds_ref):
    zc = _dot(z, bdc_ref[...].astype(bf16)).astype(bf16)
    zs = _dot(z, bds_ref[...].astype(bf16)).astype(bf16)
    return _dot(ct_ref[...].astype(bf16), zc) - _dot(st_ref[...].astype(bf16), zs)


def _ctx_cd_kernel(q_ref, k_ref, v_ref, z_ref, lam_ref, subln_ref, ct_ref, st_ref, bdc_ref, bds_ref,
                   a_ref, zf_ref, *, lam_init):
    lam = _lambda_full(lam_ref, lam_init)
    for sq in range(CTX_SEQS_CD):
        rows = slice(sq * SEQ, (sq + 1) * SEQ)
        sl = lambda h: slice(h * PAIR_W, (h + 1) * PAIR_W)
        heads = _diff_heads(lambda h: q_ref[rows, sl(h)], lambda h: [(k_ref[rows, sl(h)], False)],
                            lambda h: [v_ref[rows, sl(h)]], lam, subln_ref[...], lam_init, DIFF_GROUP)
        for h in range(H_DIFF):
            a_ref[rows, sl(h)] = heads[h].astype(bf16)
        zf_ref[rows, :] = _fourier_rows(ct_ref, st_ref, z_ref[rows, :], bdc_ref, bds_ref).astype(bf16)


def _ctx_cd(proj, lam, subln, ct, st, bdc, bds, lam_init):
    t = SEQ
    tb = CTX_SEQS_CD * t
    const = lambda b: (0, 0)
    col = lambda c: (lambda b: (b, c))
    return pl.pallas_call(
        functools.partial(_ctx_cd_kernel, lam_init=lam_init),
        grid=(BATCH // CTX_SEQS_CD,),
        in_specs=[pl.BlockSpec((tb, DIFF_W), col(0)), pl.BlockSpec((tb, DIFF_W), col(1)),
                  pl.BlockSpec((tb, DIFF_W), col(2)), pl.BlockSpec((tb, FNET_W), col(3 * DIFF_W // FNET_W)),
                  pl.BlockSpec((4, HEAD_DIM), const), pl.BlockSpec((1, PAIR_W), const),
                  pl.BlockSpec((t, t), const), pl.BlockSpec((t, t), const),
                  pl.BlockSpec((FNET_W, FNET_W), const), pl.BlockSpec((FNET_W, FNET_W), const)],
        out_specs=[pl.BlockSpec((tb, DIFF_W), lambda b: (b, 0)), pl.BlockSpec((tb, FNET_W), lambda b: (b, 0))],
        out_shape=[jax.ShapeDtypeStruct((BATCH * t, DIFF_W), bf16),
                   jax.ShapeDtypeStruct((BATCH * t, FNET_W), bf16)],
        compiler_params=_params(1),
        name="ctx_cd",
    )(proj, proj, proj, proj, lam, subln, ct, st, bdc, bds)


def _lat_cd_kernel(q_ref, k_ref, v_ref, z_ref, ckt_ref, cv_ref, lam_ref, subln_ref, ct_ref, st_ref, bdc_ref, bds_ref,
                   a_ref, zf_ref, *, lam_init):
    lam = _lambda_full(lam_ref, lam_init)
    sl = lambda h: slice(h * PAIR_W, (h + 1) * PAIR_W)
    heads = _diff_heads(lambda h: q_ref[:, sl(h)],
                        lambda h: [(k_ref[:, sl(h)], False), (ckt_ref[0, h].astype(bf16), True)],
                        lambda h: [v_ref[:, sl(h)], cv_ref[0, h].astype(bf16)], lam, subln_ref[...], lam_init,
                        DIFF_GROUP_LAT)
    for h in range(H_DIFF):
        a_ref[:, sl(h)] = heads[h].astype(bf16)
    zf_ref[...] = _fourier_rows(ct_ref, st_ref, z_ref[...], bdc_ref, bds_ref).astype(bf16)


def _lat_cd(proj, ck, cv, lam, subln, ct, st, bdc, bds, lam_init):
    t = DEC_SEQ
    nq = t // TQ_CD
    const = lambda b, i: (0, 0)
    return pl.pallas_call(
        functools.partial(_lat_cd_kernel, lam_init=lam_init),
        grid=(DEC_BATCH, nq),
        in_specs=[pl.BlockSpec((TQ_CD, DIFF_W), lambda b, i: (N_CTX // TQ_CD + b * nq + i, 0)),
                  pl.BlockSpec((t, DIFF_W), lambda b, i: (N_CTX // t + b, 1)),
                  pl.BlockSpec((t, DIFF_W), lambda b, i: (N_CTX // t + b, 2)),
                  pl.BlockSpec((t, FNET_W), lambda b, i: (N_CTX // t + b, 3 * DIFF_W // FNET_W)),
                  pl.BlockSpec((1, H_DIFF, PAIR_W, PAST_LEN), lambda b, i: (b, 0, 0, 0)),
                  pl.BlockSpec((1, H_DIFF, PAST_LEN, PAIR_W), lambda b, i: (b, 0, 0, 0)),
                  pl.BlockSpec((4, HEAD_DIM), const), pl.BlockSpec((1, PAIR_W), const),
                  pl.BlockSpec((TQ_CD, t), lambda b, i: (i, 0)), pl.BlockSpec((TQ_CD, t), lambda b, i: (i, 0)),
                  pl.BlockSpec((FNET_W, FNET_W), const), pl.BlockSpec((FNET_W, FNET_W), const)],
        out_specs=[pl.BlockSpec((TQ_CD, DIFF_W), lambda b, i: (b * nq + i, 0)),
                   pl.BlockSpec((TQ_CD, FNET_W), lambda b, i: (b * nq + i, 0))],
        out_shape=[jax.ShapeDtypeStruct((DEC_BATCH * t, DIFF_W), bf16),
                   jax.ShapeDtypeStruct((DEC_BATCH * t, FNET_W), bf16)],
        compiler_params=_params(2),
        name="lat_cd",
    )(proj, proj, proj, proj, ck, cv, lam, subln, ct, st, bdc, bds)


def _rope_tables():
    t = np.arange(DEC_SEQ)
    quarter = HEAD_DIM // 4
    inv = ROPE_BASE ** (-np.arange(quarter, dtype=np.float64) / quarter)
    ang = np.concatenate([(t // GRID_W)[:, None] * inv, (t % GRID_W)[:, None] * inv], -1)
    cos, sin = np.cos(ang), np.sin(ang)
    reps = LANES // HEAD_DIM
    return (np.tile(np.concatenate([cos, cos], -1), (1, reps)).astype(np.float32),
            np.tile(np.concatenate([-sin, sin], -1), (1, reps)).astype(np.float32))


def _dft_tables(n):
    k = np.arange(n)
    ang = (2.0 * math.pi / n) * ((k[:, None] * k[None, :]) % n)
    return (np.cos(ang) / math.sqrt(n)).astype(np.float32), (np.sin(ang) / math.sqrt(n)).astype(np.float32)


def _block_diag(m, reps):
    return np.kron(np.eye(reps, dtype=m.dtype), m)


def kernel(x_prompt, x_sample, state_ret, cache_win_k, cache_win_v, cache_diff_k, cache_diff_v, c, c_ctx, w_mod, b_mod, ln_g, ln_b, w_in_ab, w_out_ab, ret_log_gamma, ret_gn_g, ret_gn_b, win_sink, w_in_cd, w_out_cd, diff_lambda, diff_subln_g, w_gate, w_up, w_down):
    cond = jnp.concatenate([c_ctx[None, :], c, jnp.zeros((SUBLANES - 1 - DEC_BATCH, D_MODEL), f32)], 0)
    mod = _modulation(cond, w_mod, b_mod).reshape(DEPTH, SUBLANES, 6, D_MODEL)

    rope_tabs = _rope_tables()
    gmat = jnp.asarray(_block_diag(np.full((HEAD_DIM, HEAD_DIM), 1.0 / HEAD_DIM, np.float32),
                                   N_CHUNK // HEAD_DIM), bf16)
    c64, s64 = _dft_tables(FNET_DIM)
    bdc = _block_diag(c64, FNET_GROUPS)
    bds = _block_diag(s64, FNET_GROUPS)
    dft_ctx = _dft_tables(SEQ)
    dft_lat = _dft_tables(DEC_SEQ)

    x_parts = [x_prompt.reshape(N_CTX, D_MODEL), x_sample.reshape(N_LAT, D_MODEL)]
    outs = {}
    for l in range(DEPTH):
        i = l // 2
        if l % 2 == 0:
            lgf = jnp.repeat(ret_log_gamma[i, 0], HEAD_DIM)[None, :]
            lgb = jnp.repeat(ret_log_gamma[i, 1], HEAD_DIM)[None, :]
            gn_g = ret_gn_g[i][None, :]
            gn_b = ret_gn_b[i][None, :]
            sink = win_sink[i][None, :]
            rope_tiles = tuple(range(0, 2 * RET_W // LANES)) + tuple(
                range(4 * RET_W // LANES, (4 * RET_W + WIN_W + KV_W) // LANES))
            kv_tile = (4 * RET_W + WIN_W) // LANES
            kv_shape = (BATCH, 1, KV_WIN, HEAD_DIM, SEQ)
            scale_tiles = tuple(range(4 * RET_W // LANES, (4 * RET_W + WIN_W) // LANES))
            proj, wk_t, wv_t = _proj(x_parts, mod, l, w_in_ab, i, scale_tiles, rope_tabs, rope_tiles,
                                     (kv_shape, kv_shape),
                                     {kv_tile: ("heads", 0, 0), kv_tile + 1: ("heads", 1, 0)})
            ro_c, wo_c, st_c = _ctx_ab(proj, ret_log_gamma[i], sink, lgf, lgb, gmat, gn_g, gn_b)
            ck = cache_win_k[:, i].reshape(DEC_BATCH, PAST_LEN, KV_W)
            cv = cache_win_v[:, i].reshape(DEC_BATCH, PAST_LEN, KV_W)
            ro_l, wo_l = _lat_ab(proj, ret_log_gamma[i], sink, ck, cv, state_ret, i, lgf, lgb, gmat, gn_g, gn_b)
            mix_a, mix_b, w_out = (ro_c, ro_l), (wo_c, wo_l), w_out_ab
            outs.setdefault('state', []).append(st_c[:, None])
            outs.setdefault('win_k', []).append(jnp.transpose(wk_t, (0, 1, 4, 2, 3)))
            outs.setdefault('win_v', []).append(jnp.transpose(wv_t, (0, 1, 4, 2, 3)))
        else:
            lam_init = 0.8 - 0.6 * math.exp(-0.3 * l)
            subln = diff_subln_g[i][None, :]
            rope_tiles = tuple(range(0, 2 * DIFF_W // LANES))
            plan = {}
            for h in range(H_DIFF):
                plan[DIFF_W // LANES + h] = ("pairs", 0, h)
                plan[2 * DIFF_W // LANES + h] = ("plain", 1, h)
            scale_tiles = tuple(range(0, DIFF_W // LANES))
            proj, dk_t, dv_h = _proj(
                x_parts, mod, l, w_in_cd, i, scale_tiles, rope_tabs, rope_tiles,
                ((BATCH, 1, H_DIFF, 2, HEAD_DIM, SEQ), (BATCH, 1, H_DIFF, SEQ, 2 * HEAD_DIM)), plan)
            a_c, z_c = _ctx_cd(proj, diff_lambda[i], subln, dft_ctx[0], dft_ctx[1], bdc, bds, lam_init)
            ck = jnp.transpose(cache_diff_k[:, i], (0, 2, 3, 4, 1)).reshape(DEC_BATCH, H_DIFF, PAIR_W, PAST_LEN)
            cv = jnp.transpose(cache_diff_v[:, i], (0, 2, 1, 3))
            a_l, z_l = _lat_cd(proj, ck, cv, diff_lambda[i], subln, dft_lat[0], dft_lat[1], bdc, bds, lam_init)
            mix_a, mix_b, w_out = (a_c, a_l), (z_c, z_l), w_out_cd
            outs.setdefault('diff_k', []).append(jnp.transpose(dk_t, (0, 1, 5, 2, 3, 4)))
            outs.setdefault('diff_v', []).append(jnp.transpose(dv_h, (0, 1, 3, 2, 4)))
        x_parts = _post(x_parts, mix_a, mix_b, mod, ln_g, ln_b, w_out, w_gate, w_up, w_down, l, i,
                        split_out=(l == DEPTH - 1))

    y_prompt = x_parts[0].reshape(BATCH, SEQ, D_MODEL)
    y_sample = x_parts[1].reshape(DEC_BATCH, DEC_SEQ, D_MODEL)
    cat = lambda parts: parts[0] if len(parts) == 1 else jnp.concatenate(parts, 1)
    return (y_prompt, y_sample, cat(outs['state']), cat(outs['win_k']), cat(outs['win_v']),
            cat(outs['diff_k']), cat(outs['diff_v']))
```

```python
import functools
import math

import jax
import jax.numpy as jnp
import numpy as np
from jax import lax
from jax.experimental import pallas as pl
from jax.experimental.pallas import tpu as pltpu

D_MODEL = 1024
BATCH = 32
SEQ = 256
DEPTH = 2
DEC_BATCH = 2
DEC_SEQ = 1024
PAST_LEN = 512
GRID_W = 64
HEAD_DIM = 64
ROPE_BASE = 10000.0
H_RET = 8
H_WIN = 8
KV_WIN = 2
G_WIN = H_WIN // KV_WIN
WINDOW = 128
H_DIFF = 6
FNET_GROUPS = 4
FNET_DIM = 64
D_FF = 256 * math.ceil(8 * D_MODEL / 3 / 256)
RET_W = H_RET * HEAD_DIM
WIN_W = H_WIN * HEAD_DIM
KV_W = KV_WIN * HEAD_DIM
AB_IN = 4 * RET_W + WIN_W + 2 * KV_W
DIFF_W = H_DIFF * 2 * HEAD_DIM
FNET_W = FNET_GROUPS * FNET_DIM
CD_IN = 3 * DIFF_W + FNET_W
ALPHA = (2 * DEPTH) ** 0.25
LN_EPS = 1e-5
QK_SCALE = HEAD_DIM ** -0.5
LOG2_E = math.log2(math.e)

N_CTX = BATCH * SEQ
N_LAT = DEC_BATCH * DEC_SEQ
N_TOK = N_CTX + N_LAT

LANES = 128
SUBLANES = 8
PAIR_W = 2 * HEAD_DIM
TM = 512
TM_PROJ = 1024
CTX_BLOCKS = N_CTX // TM
TOK_BLOCKS = N_TOK // TM
ROW_GROUPS = 2
FFN_SKEW = 2
TQ = 256
TQ_CD = 512
CTX_SEQS = 4
CTX_SEQS_CD = 4
RET_GROUP = 4
WIN_GROUP = 4
WIN_GROUP_LAT = 4
DIFF_GROUP_LAT = 1
DIFF_GROUP = 3
N_CHUNK = 256
MOD_TN = 1536
NEG_BIG = -1e30
VMEM_LIMIT = 56 * 1024 * 1024

f32 = jnp.float32
bf16 = jnp.bfloat16


def _params(n_axes):
    return pltpu.CompilerParams(dimension_semantics=("arbitrary",) * n_axes,
                                vmem_limit_bytes=VMEM_LIMIT)


def _dot(a, b):
    return jnp.dot(a, b, preferred_element_type=f32)


def _dot_nt(a, b):
    return lax.dot_general(a, b, (((1,), (1,)), ((), ())), preferred_element_type=f32)


def _ln(x):
    mu = jnp.mean(x, -1, keepdims=True)
    d = x - mu
    var = jnp.mean(d * d, -1, keepdims=True)
    return d * lax.rsqrt(var + LN_EPS)


def _silu(x):
    return x * jax.nn.sigmoid(x)


def _split_bf16(x):
    hi = x.astype(bf16)
    lo = (x - hi.astype(f32)).astype(bf16)
    return hi, lo


def _lane_half_mask(shape):
    return (lax.broadcasted_iota(jnp.int32, shape, len(shape) - 1) & HEAD_DIM) == 0


def _mod_kernel(c_ref, w_ref, b_ref, o_ref):
    layer = pl.program_id(0)
    a = _silu(c_ref[...])
    a_hi, a_lo = _split_bf16(a)
    w_hi, w_lo = _split_bf16(w_ref[0])
    acc = _dot(a_hi, w_hi) + _dot(a_lo, w_hi) + _dot(a_hi, w_lo)
    o_ref[0] = acc + b_ref[pl.ds(layer, 1), :]


def _modulation(cond, w_mod, b_mod):
    tn = MOD_TN
    rows = cond.shape[0]
    return pl.pallas_call(
        _mod_kernel,
        grid=(DEPTH, 6 * D_MODEL // tn),
        in_specs=[pl.BlockSpec((rows, D_MODEL), lambda l, j: (0, 0)),
                  pl.BlockSpec((1, D_MODEL, tn), lambda l, j: (l, 0, j)),
                  pl.BlockSpec((DEPTH, tn), lambda l, j: (0, j))],
        out_specs=pl.BlockSpec((1, rows, tn), lambda l, j: (l, 0, j)),
        out_shape=jax.ShapeDtypeStruct((DEPTH, rows, 6 * D_MODEL), f32),
        compiler_params=_params(2),
        name="modulation",
    )(cond, w_mod, b_mod)


def _tok(i, n_w):
    return jnp.maximum(i - n_w, 0)


def _ctx_blk(t, tm=TM):
    return jnp.minimum(t, N_CTX // tm - 1)


def _lat_blk(t, tm=TM):
    return jnp.maximum(t - N_CTX // tm, 0)


def _mod_row(t, tm=TM):
    return jnp.where(t < N_CTX // tm, 0, 1 + _lat_blk(t, tm) * tm // DEC_SEQ)


def _token_specs(parts, n_w, tm=TM):
    width = parts[0].shape[1]
    if len(parts) == 1:
        return [pl.BlockSpec((tm, width), lambda i: (_tok(i, n_w), 0))]
    return [pl.BlockSpec((tm, width), lambda i: (_ctx_blk(_tok(i, n_w), tm), 0)),
            pl.BlockSpec((tm, width), lambda i: (_lat_blk(_tok(i, n_w), tm), 0))]


def _pick(refs, is_ctx, rs):
    if len(refs) == 1:
        return refs[0][rs, :]
    if isinstance(is_ctx, bool):
        return refs[0 if is_ctx else 1][rs, :]
    return jnp.where(is_ctx, refs[0][rs, :], refs[1][rs, :])


def _rope_pair(y, cos, sin_signed):
    first_half = (lax.broadcasted_iota(jnp.int32, y.shape, 1) & (HEAD_DIM // 2)) == 0
    swapped = jnp.where(first_half, pltpu.roll(y, LANES - HEAD_DIM // 2, 1), pltpu.roll(y, HEAD_DIM // 2, 1))
    return y * cos + swapped * sin_signed


def _proj_kernel(*refs, n_x, n_cache, n_w, rope_tiles, scale_tiles, cache_plan):
    x_refs = refs[:n_x]
    mod_ref, w_ref, cos_ref, sin_ref, o_ref = refs[n_x:n_x + 5]
    cache_refs = refs[n_x + 5:n_x + 5 + n_cache]
    wbf_ref, u_ref = refs[n_x + 5 + n_cache:]
    i = pl.program_id(0)

    @pl.when(i < n_w)
    def _():
        wbf_ref[i] = w_ref[0].astype(bf16)

    def tokens(is_ctx):
        x_ref = x_refs[0] if is_ctx else x_refs[-1]
        shift = mod_ref[0, 0:1, :]
        scale = mod_ref[0, 1:2, :]
        groups = [slice(b * SEQ, (b + 1) * SEQ) for b in range(TM_PROJ // SEQ)]
        for rs in groups:
            u_ref[rs, :] = (_ln(x_ref[rs, :]) * (1.0 + scale) + shift).astype(bf16)
        for c in range(n_w):
            y_all = _dot(u_ref[...], wbf_ref[c])
            for b, rs in enumerate(groups):
                y = y_all[rs, :]
                for t in range(N_CHUNK // LANES):
                    tile = c * (N_CHUNK // LANES) + t
                    piece = y[:, t * LANES:(t + 1) * LANES]
                    if tile in rope_tiles and not is_ctx:
                        piece = _rope_pair(piece, cos_ref[rs, :], sin_ref[rs, :])
                    if tile in scale_tiles:
                        piece = piece * (QK_SCALE * LOG2_E)
                    o_ref[rs, tile * LANES:(tile + 1) * LANES] = piece.astype(o_ref.dtype)
                    if tile in cache_plan and is_ctx:
                        kind, out_idx, slot = cache_plan[tile]
                        c_ref = cache_refs[out_idx]
                        if kind == "plain":
                            c_ref[b, 0, slot] = piece
                        else:
                            piece_t = piece.T
                            if kind == "heads":
                                c_ref[b, 0, 0] = piece_t[0:HEAD_DIM]
                                c_ref[b, 0, 1] = piece_t[HEAD_DIM:]
                            else:
                                c_ref[b, 0, slot, 0] = piece_t[0:HEAD_DIM]
                                c_ref[b, 0, slot, 1] = piece_t[HEAD_DIM:]

    t = i - n_w

    @pl.when(jnp.logical_and(t >= 0, t < N_CTX // TM_PROJ))
    def _():
        tokens(True)

    @pl.when(t >= N_CTX // TM_PROJ)
    def _():
        tokens(False)


def _proj(x_parts, mod, mod_layer, w_all, layer, scale_tiles, rope_tabs, rope_tiles, cache_shapes, cache_plan):
    n_out = w_all.shape[2]
    n_w = n_out // N_CHUNK
    tm = TM_PROJ
    nb = DEC_SEQ // tm
    tok = lambda i: _tok(i, n_w)
    in_specs = _token_specs(x_parts, n_w, tm) + [
        pl.BlockSpec((None, 1, 6, D_MODEL), lambda i: (mod_layer, _mod_row(tok(i), tm), 0, 0)),
        pl.BlockSpec((1, D_MODEL, N_CHUNK), lambda i: (layer, 0, jnp.minimum(i, n_w - 1))),
        pl.BlockSpec((tm, LANES), lambda i: (_lat_blk(tok(i), tm) % nb, 0)),
        pl.BlockSpec((tm, LANES), lambda i: (_lat_blk(tok(i), tm) % nb, 0))]
    out_specs = [pl.BlockSpec((tm, n_out), lambda i: (tok(i), 0))]
    out_shape = [jax.ShapeDtypeStruct((N_TOK, n_out), bf16)]
    for shp in cache_shapes:
        blk = (tm // SEQ,) + tuple(shp[1:])
        out_specs.append(pl.BlockSpec(blk, lambda i, nd=len(shp): (_ctx_blk(tok(i), tm),) + (0,) * (nd - 1)))
        out_shape.append(jax.ShapeDtypeStruct(tuple(shp), f32))
    return pl.pallas_call(
        functools.partial(_proj_kernel, n_x=len(x_parts), n_cache=len(cache_shapes), n_w=n_w,
                          rope_tiles=frozenset(rope_tiles), scale_tiles=frozenset(scale_tiles),
                          cache_plan=dict(cache_plan)),
        grid=(n_w + N_TOK // tm,),
        in_specs=in_specs,
        out_specs=out_specs,
        out_shape=out_shape,
        scratch_shapes=[pltpu.VMEM((n_w, D_MODEL, N_CHUNK), bf16), pltpu.VMEM((tm, D_MODEL), bf16)],
        compiler_params=_params(1),
        name="proj",
    )(*x_parts, mod, w_all, *rope_tabs)


def _post_kernel(*refs, n_x, n_y, ka, kb, n_w):
    x_refs = refs[:n_x]
    (ac_ref, al_ref, bc_ref, bl_ref, mod_ref, lng_ref, lnb_ref,
     wo_ref, wg_ref, wu_ref, wd_ref) = refs[n_x:n_x + 11]
    y_refs = refs[n_x + 11:n_x + 11 + n_y]
    wo_s, wg_s, wu_s, wd_s, x1_ref, u_ref, h_ref, y_ref = refs[n_x + 11 + n_y:]
    i = pl.program_id(0)
    lead = n_w - 1
    gate1 = mod_ref[0, 2:3, :]
    shift2 = mod_ref[0, 3:4, :]
    scale2 = mod_ref[0, 4:5, :]
    gate2 = mod_ref[0, 5:6, :]
    groups = [slice(r * TM // ROW_GROUPS, (r + 1) * TM // ROW_GROUPS) for r in range(ROW_GROUPS)]

    def mix_in(rs, is_ctx):
        a = _pick((ac_ref, al_ref), is_ctx, rs)
        b = _pick((bc_ref, bl_ref), is_ctx, rs)
        pieces = ([a[:, c:c + N_CHUNK] for c in range(0, ka, N_CHUNK)]
                  + [b[:, c:c + N_CHUNK] for c in range(0, kb, N_CHUNK)])
        h = functools.reduce(lambda s, p: s + p, [_dot(p, wo_s[c]) for c, p in enumerate(pieces)])
        x1 = _ln(ALPHA * _pick(x_refs, is_ctx, rs) + gate1 * h) * lng_ref[0, 0:1, :] + lnb_ref[0, 0:1, :]
        x1_ref[rs, :] = x1
        u_ref[rs, :] = (_ln(x1) * (1.0 + scale2) + shift2).astype(bf16)

    def finish(rs, ffn, is_ctx):
        y = _ln(ALPHA * x1_ref[rs, :] + gate2 * ffn) * lng_ref[0, 1:2, :] + lnb_ref[0, 1:2, :]
        if n_y == 1:
            y_refs[0][rs, :] = y
        elif isinstance(is_ctx, bool):
            y_refs[0 if is_ctx else 1][rs, :] = y
        else:
            y_ref[rs, :] = y

    def emit(is_ctx):
        if n_y == 2 and not isinstance(is_ctx, bool):
            @pl.when(is_ctx)
            def _():
                y_refs[0][...] = y_ref[...]

            @pl.when(jnp.logical_not(is_ctx))
            def _():
                y_refs[1][...] = y_ref[...]

    @pl.when(i < n_w)
    def _():
        wg_s[i] = wg_ref[0].astype(bf16)
        wu_s[i] = wu_ref[0].astype(bf16)
        wd_s[i] = wd_ref[0].astype(bf16)

        @pl.when(i == 0)
        def _():
            for c in range((ka + kb) // N_CHUNK):
                wo_s[c] = wo_ref[0, c * N_CHUNK:(c + 1) * N_CHUNK, :].astype(bf16)
            for rs in groups:
                mix_in(rs, True)
                y_ref[rs, :] = jnp.zeros((TM // ROW_GROUPS, D_MODEL), f32)

        for rs in groups:
            g = _dot(u_ref[rs, :], wg_s[i])
            up = _dot(u_ref[rs, :], wu_s[i])
            y_ref[rs, :] += _dot((_silu(g) * up).astype(bf16), wd_s[i])

        @pl.when(i == lead)
        def _():
            for rs in groups:
                finish(rs, y_ref[rs, :], True)

    @pl.when(i >= n_w)
    def _():
        is_ctx = (i - lead) < CTX_BLOCKS
        for rs in groups:
            mix_in(rs, is_ctx)

        def ffn_chunk(rs, c):
            g = _dot(u_ref[rs, :], wg_s[c])
            up = _dot(u_ref[rs, :], wu_s[c])
            h_ref[rs, c * N_CHUNK:(c + 1) * N_CHUNK] = (_silu(g) * up).astype(bf16)

        def ffn_down(rs):
            finish(rs, functools.reduce(
                lambda s, p: s + p,
                [_dot(h_ref[rs, c * N_CHUNK:(c + 1) * N_CHUNK], wd_s[c]) for c in range(n_w)]), is_ctx)

        for c in range(n_w + FFN_SKEW * (ROW_GROUPS - 1)):
            for r, rs in enumerate(groups):
                cc = c - FFN_SKEW * r
                if 0 <= cc < n_w:
                    ffn_chunk(rs, cc)
                if cc == n_w - 1:
                    ffn_down(rs)
        emit(is_ctx)


def _post(x_parts, mix_a, mix_b, mod, ln_g, ln_b, w_out, w_gate, w_up, w_down, layer, mix_layer, split_out):
    ka, kb = mix_a[0].shape[1], mix_b[0].shape[1]
    n_w = D_FF // N_CHUNK
    lead = n_w - 1
    tok = lambda i: _tok(i, lead)
    lay = lambda i: (layer, 0, 0)
    in_specs = (_token_specs(x_parts, lead) + _token_specs(mix_a, lead) + _token_specs(mix_b, lead) + [
        pl.BlockSpec((None, 1, 6, D_MODEL), lambda i: (layer, _mod_row(tok(i)), 0, 0)),
        pl.BlockSpec((1, 2, D_MODEL), lay),
        pl.BlockSpec((1, 2, D_MODEL), lay),
        pl.BlockSpec((1, ka + kb, D_MODEL), lambda i: (mix_layer, 0, 0), pipeline_mode=pl.Buffered(1)),
        pl.BlockSpec((1, D_MODEL, N_CHUNK), lambda i: (layer, 0, jnp.minimum(i, n_w - 1))),
        pl.BlockSpec((1, D_MODEL, N_CHUNK), lambda i: (layer, 0, jnp.minimum(i, n_w - 1))),
        pl.BlockSpec((1, N_CHUNK, D_MODEL), lambda i: (layer, jnp.minimum(i, n_w - 1), 0))])
    if split_out:
        out_specs = [pl.BlockSpec((TM, D_MODEL), lambda i: (_ctx_blk(tok(i)), 0)),
                     pl.BlockSpec((TM, D_MODEL), lambda i: (_lat_blk(tok(i)), 0))]
        out_shape = [jax.ShapeDtypeStruct((N_CTX, D_MODEL), f32), jax.ShapeDtypeStruct((N_LAT, D_MODEL), f32)]
    else:
        out_specs = [pl.BlockSpec((TM, D_MODEL), lambda i: (tok(i), 0))]
        out_shape = [jax.ShapeDtypeStruct((N_TOK, D_MODEL), f32)]
    return pl.pallas_call(
        functools.partial(_post_kernel, n_x=len(x_parts), n_y=len(out_shape), ka=ka, kb=kb, n_w=n_w),
        grid=(lead + TOK_BLOCKS,),
        in_specs=in_specs,
        out_specs=out_specs,
        out_shape=out_shape,
        scratch_shapes=[pltpu.VMEM(((ka + kb) // N_CHUNK, N_CHUNK, D_MODEL), bf16),
                        pltpu.VMEM((n_w, D_MODEL, N_CHUNK), bf16),
                        pltpu.VMEM((n_w, D_MODEL, N_CHUNK), bf16), pltpu.VMEM((n_w, N_CHUNK, D_MODEL), bf16),
                        pltpu.VMEM((TM, D_MODEL), f32), pltpu.VMEM((TM, D_MODEL), bf16),
                        pltpu.VMEM((TM, D_FF), bf16), pltpu.VMEM((TM, D_MODEL), f32)],
        compiler_params=_params(1),
        name="post",
    )(*x_parts, *mix_a, *mix_b, mod, ln_g, ln_b, w_out, w_gate, w_up, w_down)


def _group_norm_gate(ro, rg, gmat, gn_g, gn_b):
    def gmean(parts):
        cols = []
        for c in range(0, RET_W, N_CHUNK):
            cols.append(sum(_dot(p[:, c:c + N_CHUNK], gmat) for p in parts))
        return jnp.concatenate(cols, -1)

    d = ro - gmean(_split_bf16(ro))
    var = gmean([(d * d).astype(bf16)])
    y = d * lax.rsqrt(var + LN_EPS) * gn_g + gn_b
    return _silu(rg.astype(f32)) * y


def _dup_head(x, j):
    first = _lane_half_mask(x.shape)
    keep = first if j == 0 else jnp.logical_not(first)
    xm = jnp.where(keep, x.astype(f32), 0.0)
    return xm + pltpu.roll(xm, HEAD_DIM, 1)


def _softmax_parts(scores, sink):
    m = sink
    for s in scores:
        m = jnp.maximum(m, jnp.max(s, -1, keepdims=True))
    es = [jnp.exp2(s - m) for s in scores]
    denom = jnp.exp2(sink - m)
    for e in es:
        denom = denom + jnp.sum(e, -1, keepdims=True)
    return es, denom


def _retention_tables(lg_ref, lgf_ref, lgb_ref, dmask_ref, kdec_ref, n):
    row = lax.broadcasted_iota(jnp.int32, (n, n), 0)
    col = lax.broadcasted_iota(jnp.int32, (n, n), 1)
    diff = (row - col).astype(f32)
    diag = jnp.where(row == col, 2.0 * QK_SCALE, QK_SCALE)
    for h in range(H_RET):
        dmask_ref[h] = jnp.exp(jnp.where(diff >= 0, lg_ref[0, h] * diff, -lg_ref[1, h] * diff)) * diag
    t = lax.broadcasted_iota(jnp.int32, (n, RET_W), 0).astype(f32)
    kdec_ref[0] = jnp.exp(lgf_ref[...] * (n - 1.0 - t)) * QK_SCALE
    kdec_ref[1] = jnp.exp(lgb_ref[...] * t) * QK_SCALE


def _retention_intra(pairs, q_of, k_of, v_of, dmask_ref):
    first = _lane_half_mask(k_of(pairs[0]).shape)
    masked = {}
    for p in pairs:
        kb = k_of(p)
        for e in range(2):
            keep = first if e == 0 else jnp.logical_not(first)
            s = _dot_nt(q_of(p), jnp.where(keep, kb, jnp.zeros_like(kb))) * dmask_ref[2 * p + e]
            masked[p, e] = s.astype(bf16)
    outs = {}
    for p in pairs:
        pv = [_dot(masked[p, e], v_of(p)) for e in range(2)]
        outs[p] = jnp.where(_lane_half_mask(pv[0].shape), pv[0], pv[1])
    return outs


def _window_group(subs, q_of, k_parts_of, v_parts_of, masks, sink_of):
    scores = {}
    for key in subs:
        parts = [_dot_nt(q_of(key), k) for k in k_parts_of(key)]
        scores[key] = [sc if mk is None else jnp.where(mk, sc, NEG_BIG) for sc, mk in zip(parts, masks)]
    probs = {}
    for key in subs:
        es, denom = _softmax_parts(scores[key], sink_of(key))
        probs[key] = ([ex.astype(bf16) for ex in es], denom)
    outs = {}
    for key in subs:
        es, denom = probs[key]
        pv = functools.reduce(lambda x, y: x + y, [_dot(ex, v) for ex, v in zip(es, v_parts_of(key))])
        outs[key] = pv / denom
    return outs


def _ctx_ab_kernel(lg_ref, sink_ref, rq_ref, rk_ref, rv_ref, rg_ref, wq_ref, wk_ref, wv_ref,
                   lgf_ref, lgb_ref, gmat_ref, gng_ref, gnb_ref,
                   ro_ref, wo_ref, st_ref, dmask_ref, kdec_ref, ret_ref):
    t_len = SEQ

    @pl.when(pl.program_id(0) == 0)
    def _():
        _retention_tables(lg_ref, lgf_ref, lgb_ref, dmask_ref, kdec_ref, t_len)

    first = _lane_half_mask((t_len, PAIR_W))
    for sq in range(CTX_SEQS):
        rows = slice(sq * t_len, (sq + 1) * t_len)
        psl = lambda p: slice(p * PAIR_W, (p + 1) * PAIR_W)
        for p0 in range(0, H_RET // 2, RET_GROUP):
            pairs = list(range(p0, p0 + RET_GROUP))
            intra = _retention_intra(pairs, lambda p: rq_ref[rows, psl(p)], lambda p: rk_ref[rows, psl(p)],
                                     lambda p: rv_ref[rows, psl(p)], dmask_ref)
            for p in pairs:
                ret_ref[rows, psl(p)] = intra[p]
        for p in range(H_RET // 2):
            sl = psl(p)
            kb = rk_ref[rows, sl]
            v = rv_ref[rows, sl]
            for d in range(2):
                kd_t = (kb * kdec_ref[d, :, sl]).T.astype(bf16)
                st = _dot(kd_t, v)
                st_ref[sq, d, 2 * p] = st[0:HEAD_DIM, 0:HEAD_DIM]
                st_ref[sq, d, 2 * p + 1] = pltpu.roll(st[HEAD_DIM:, :], HEAD_DIM, 1)[:, 0:HEAD_DIM]
        ro_ref[rows, :] = _group_norm_gate(ret_ref[rows, :], rg_ref[rows, :], gmat_ref[...], gng_ref[...],
                                           gnb_ref[...]).astype(bf16)

        k_dup = [_dup_head(wk_ref[rows, :], j).astype(bf16) for j in range(KV_WIN)]
        v_dup = [_dup_head(wv_ref[rows, :], j).astype(bf16) for j in range(KV_WIN)]

        def q_masked(key):
            qp, e = key
            qb = wq_ref[rows, qp * PAIR_W:(qp + 1) * PAIR_W]
            return jnp.where(first if e == 0 else jnp.logical_not(first), qb, jnp.zeros_like(qb))

        kv_of = lambda key: key[0] * 2 // G_WIN
        for g0 in range(0, H_WIN // 2, WIN_GROUP):
            subs = [(qp, e) for qp in range(g0, g0 + WIN_GROUP) for e in range(2)]
            outs = _window_group(subs, q_masked, lambda key: [k_dup[kv_of(key)]], lambda key: [v_dup[kv_of(key)]],
                                 [None], lambda key: sink_ref[0, 2 * key[0] + key[1]] * LOG2_E)
            for qp in range(g0, g0 + WIN_GROUP):
                wo_ref[rows, qp * PAIR_W:(qp + 1) * PAIR_W] = jnp.where(first, outs[qp, 0], outs[qp, 1]).astype(bf16)


def _ctx_ab(proj, log_gamma, sink, lgf_lanes, lgb_lanes, gmat, gn_g, gn_b):
    t = SEQ
    tb = CTX_SEQS * t
    smem = pl.BlockSpec(memory_space=pltpu.SMEM)
    const = lambda b: (0, 0)
    col = lambda c: (lambda b: (b, c))
    return pl.pallas_call(
        _ctx_ab_kernel,
        grid=(BATCH // CTX_SEQS,),
        in_specs=[smem, smem,
                  pl.BlockSpec((tb, RET_W), col(0)), pl.BlockSpec((tb, RET_W), col(1)),
                  pl.BlockSpec((tb, RET_W), col(2)), pl.BlockSpec((tb, RET_W), col(3)),
                  pl.BlockSpec((tb, WIN_W), col(4)),
                  pl.BlockSpec((tb, KV_W), col((4 * RET_W + WIN_W) // KV_W)),
                  pl.BlockSpec((tb, KV_W), col((4 * RET_W + WIN_W) // KV_W + 1)),
                  pl.BlockSpec((1, RET_W), const), pl.BlockSpec((1, RET_W), const),
                  pl.BlockSpec((N_CHUNK, N_CHUNK), const),
                  pl.BlockSpec((1, RET_W), const), pl.BlockSpec((1, RET_W), const)],
        out_specs=[pl.BlockSpec((tb, RET_W), lambda b: (b, 0)),
                   pl.BlockSpec((tb, WIN_W), lambda b: (b, 0)),
                   pl.BlockSpec((CTX_SEQS, 2, H_RET, HEAD_DIM, HEAD_DIM), lambda b: (b, 0, 0, 0, 0))],
        out_shape=[jax.ShapeDtypeStruct((BATCH * t, RET_W), bf16),
                   jax.ShapeDtypeStruct((BATCH * t, WIN_W), bf16),
                   jax.ShapeDtypeStruct((BATCH, 2, H_RET, HEAD_DIM, HEAD_DIM), f32)],
        scratch_shapes=[pltpu.VMEM((H_RET, t, t), f32), pltpu.VMEM((2, t, RET_W), f32),
                        pltpu.VMEM((tb, RET_W), f32)],
        compiler_params=_params(1),
        name="ctx_ab",
    )(log_gamma, sink, proj, proj, proj, proj, proj, proj, proj, lgf_lanes, lgb_lanes, gmat, gn_g, gn_b)


def _pair_state(s0_ref, d, p):
    zero = jnp.zeros((HEAD_DIM, HEAD_DIM), f32)
    top = jnp.concatenate([s0_ref[0, 0, d, 2 * p], zero], 1)
    bottom = jnp.concatenate([zero, s0_ref[0, 0, d, 2 * p + 1]], 1)
    return jnp.concatenate([top, bottom], 0)


def _lat_ab_kernel(lg_ref, sink_ref, rq_ref, rk_ref, rv_ref, rg_ref, wq_ref, wk_ref, wv_ref, ck_ref, cv_ref,
                   s0_ref, lgf_ref, lgb_ref, gmat_ref, gng_ref, gnb_ref,
                   ro_ref, wo_ref, ret_ref, dmask_ref, kdec_ref, qdec_ref, sf_ref, sb_ref):
    t_len = DEC_SEQ
    n_chunks = t_len // TQ
    chunk = pl.program_id(1)
    q0 = pl.multiple_of(chunk * TQ, TQ)
    first = _lane_half_mask((TQ, PAIR_W))

    @pl.when(jnp.logical_and(pl.program_id(0) == 0, chunk == 0))
    def _():
        _retention_tables(lg_ref, lgf_ref, lgb_ref, dmask_ref, kdec_ref, TQ)
        t = lax.broadcasted_iota(jnp.int32, (TQ, RET_W), 0).astype(f32)
        qdec_ref[0] = jnp.exp(lgf_ref[...] * (t + 1.0))
        qdec_ref[1] = jnp.exp(lgb_ref[...] * (TQ - t))

    @pl.when(chunk == 0)
    def _():
        r = lax.broadcasted_iota(jnp.int32, (PAIR_W, PAIR_W), 0)
        c_ = lax.broadcasted_iota(jnp.int32, (PAIR_W, PAIR_W), 1)
        same_head = (r < HEAD_DIM) == (c_ < HEAD_DIM)
        for p in range(H_RET // 2):
            sl = slice(p * PAIR_W, (p + 1) * PAIR_W)
            kv = []
            for c in range(n_chunks):
                rows = slice(c * TQ, (c + 1) * TQ)
                kc = rk_ref[rows, sl]
                vc = rv_ref[rows, sl]
                kv.append([jnp.where(same_head, _dot((kc * kdec_ref[d, :, sl]).T.astype(bf16), vc), 0.0)
                           for d in range(2)])
            state = _pair_state(s0_ref, 0, p)
            for c in range(n_chunks):
                sf_ref[c, p] = state
                state = state * jnp.exp(lgf_ref[:, sl] * TQ) + kv[c][0]
            state = _pair_state(s0_ref, 1, p)
            for c in reversed(range(n_chunks)):
                sb_ref[c, p] = state
                state = state * jnp.exp(lgb_ref[:, sl] * TQ) + kv[c][1]

    psl = lambda p: slice(p * PAIR_W, (p + 1) * PAIR_W)
    intra = {}
    for p0 in range(0, H_RET // 2, RET_GROUP):
        intra.update(_retention_intra(list(range(p0, p0 + RET_GROUP)), lambda p: rq_ref[:, psl(p)],
                                      lambda p: rk_ref[pl.ds(q0, TQ), psl(p)],
                                      lambda p: rv_ref[pl.ds(q0, TQ), psl(p)], dmask_ref))
    for p in range(H_RET // 2):
        sl = psl(p)
        q = rq_ref[:, sl]
        o = intra[p]
        o = o + _dot(q, sf_ref[chunk, p].astype(bf16)) * qdec_ref[0, :, sl]
        o = o + _dot(q, sb_ref[chunk, p].astype(bf16)) * qdec_ref[1, :, sl]
        ret_ref[:, sl] = o
    ro_ref[...] = _group_norm_gate(ret_ref[...], rg_ref[...], gmat_ref[...], gng_ref[...], gnb_ref[...]).astype(bf16)

    band = TQ + 2 * WINDOW
    k_start = pl.multiple_of(jnp.clip(q0 - WINDOW, 0, t_len - band), LANES)
    qi = q0 + lax.broadcasted_iota(jnp.int32, (TQ, band), 0)
    kj = k_start + lax.broadcasted_iota(jnp.int32, (TQ, band), 1)
    in_band = jnp.abs(qi - kj) <= WINDOW
    k_parts = [[_dup_head(wk_ref[pl.ds(k_start, band), :], j).astype(bf16), _dup_head(ck_ref[0], j).astype(bf16)]
               for j in range(KV_WIN)]
    v_parts = [[_dup_head(wv_ref[pl.ds(k_start, band), :], j).astype(bf16), _dup_head(cv_ref[0], j).astype(bf16)]
               for j in range(KV_WIN)]

    def q_masked(key):
        qp, e = key
        qb = wq_ref[:, qp * PAIR_W:(qp + 1) * PAIR_W]
        return jnp.where(first if e == 0 else jnp.logical_not(first), qb, jnp.zeros_like(qb))

    kv_of = lambda key: key[0] * 2 // G_WIN
    for g0 in range(0, H_WIN // 2, WIN_GROUP_LAT):
        subs = [(qp, e) for qp in range(g0, g0 + WIN_GROUP_LAT) for e in range(2)]
        outs = _window_group(subs, q_masked, lambda key: k_parts[kv_of(key)], lambda key: v_parts[kv_of(key)],
                             [in_band, None], lambda key: sink_ref[0, 2 * key[0] + key[1]] * LOG2_E)
        for qp in range(g0, g0 + WIN_GROUP_LAT):
            wo_ref[:, qp * PAIR_W:(qp + 1) * PAIR_W] = jnp.where(first, outs[qp, 0], outs[qp, 1]).astype(bf16)


def _lat_ab(proj, log_gamma, sink, ck, cv, state, layer, lgf_lanes, lgb_lanes, gmat, gn_g, gn_b):
    t = DEC_SEQ
    nq = t // TQ
    smem = pl.BlockSpec(memory_space=pltpu.SMEM)
    const = lambda b, i: (0, 0)
    qcol = lambda c: (lambda b, i: (N_CTX // TQ + b * nq + i, c))
    bcol = lambda c: (lambda b, i: (N_CTX // t + b, c))
    kv_col = (4 * RET_W + WIN_W) // KV_W
    return pl.pallas_call(
        _lat_ab_kernel,
        grid=(DEC_BATCH, nq),
        in_specs=[smem, smem,
                  pl.BlockSpec((TQ, RET_W), qcol(0)), pl.BlockSpec((t, RET_W), bcol(1)),
                  pl.BlockSpec((t, RET_W), bcol(2)), pl.BlockSpec((TQ, RET_W), qcol(3)),
                  pl.BlockSpec((TQ, WIN_W), qcol(4)),
                  pl.BlockSpec((t, KV_W), bcol(kv_col)), pl.BlockSpec((t, KV_W), bcol(kv_col + 1)),
                  pl.BlockSpec((1, PAST_LEN, KV_W), lambda b, i: (b, 0, 0)),
                  pl.BlockSpec((1, PAST_LEN, KV_W), lambda b, i: (b, 0, 0)),
                  pl.BlockSpec((1, 1, 2, H_RET, HEAD_DIM, HEAD_DIM), lambda b, i: (b, layer, 0, 0, 0, 0)),
                  pl.BlockSpec((1, RET_W), const), pl.BlockSpec((1, RET_W), const),
                  pl.BlockSpec((N_CHUNK, N_CHUNK), const),
                  pl.BlockSpec((1, RET_W), const), pl.BlockSpec((1, RET_W), const)],
        out_specs=[pl.BlockSpec((TQ, RET_W), lambda b, i: (b * nq + i, 0)),
                   pl.BlockSpec((TQ, WIN_W), lambda b, i: (b * nq + i, 0))],
        out_shape=[jax.ShapeDtypeStruct((DEC_BATCH * t, RET_W), bf16),
                   jax.ShapeDtypeStruct((DEC_BATCH * t, WIN_W), bf16)],
        scratch_shapes=[pltpu.VMEM((TQ, RET_W), f32), pltpu.VMEM((H_RET, TQ, TQ), f32),
                        pltpu.VMEM((2, TQ, RET_W), f32), pltpu.VMEM((2, TQ, RET_W), f32),
                        pltpu.VMEM((t // TQ, H_RET // 2, PAIR_W, PAIR_W), f32),
                        pltpu.VMEM((t // TQ, H_RET // 2, PAIR_W, PAIR_W), f32)],
        compiler_params=_params(2),
        name="lat_ab",
    )(log_gamma, sink, proj, proj, proj, proj, proj, proj, proj, ck, cv, state,
      lgf_lanes, lgb_lanes, gmat, gn_g, gn_b)


def _lambda_full(lam_ref, lam_init):
    lam = lam_ref[...]
    a = jnp.sum(lam[0:1, :] * lam[1:2, :], -1, keepdims=True)
    b = jnp.sum(lam[2:3, :] * lam[3:4, :], -1, keepdims=True)
    return jnp.exp(a) - jnp.exp(b) + lam_init


def _diff_heads(q_of, k_parts_of, v_parts_of, lam, subln, lam_init, group):
    res = []
    for h0 in range(0, H_DIFF, group):
        res += _diff_head_group(range(h0, h0 + group), q_of, k_parts_of, v_parts_of, lam, subln, lam_init)
    return res


def _diff_head_group(heads, q_of, k_parts_of, v_parts_of, lam, subln, lam_init):
    subs = [(h, e) for h in heads for e in range(2)]
    scores = {}
    for h, e in subs:
        q = q_of(h)
        fm = _lane_half_mask(q.shape)
        q_sub = jnp.where(fm if e == 0 else jnp.logical_not(fm), q, jnp.zeros_like(q))
        scores[h, e] = [_dot(q_sub, k) if transposed else _dot_nt(q_sub, k) for k, transposed in k_parts_of(h)]
    probs = {}
    for key in subs:
        m = scores[key][0].max(-1, keepdims=True)
        for sc in scores[key][1:]:
            m = jnp.maximum(m, sc.max(-1, keepdims=True))
        es = [jnp.exp2(sc - m) for sc in scores[key]]
        denom = es[0].sum(-1, keepdims=True)
        for ex in es[1:]:
            denom = denom + ex.sum(-1, keepdims=True)
        probs[key] = ([ex.astype(bf16) for ex in es], denom)
    outs = {}
    for h, e in subs:
        es, denom = probs[h, e]
        pv = functools.reduce(lambda x, y: x + y, [_dot(ex, v) for v, ex in zip(v_parts_of(h), es)])
        outs[h, e] = pv / denom
    res = []
    for h in heads:
        a = outs[h, 0] - lam * outs[h, 1]
        res.append(a * lax.rsqrt(jnp.mean(a * a, -1, keepdims=True) + LN_EPS) * subln * (1.0 - lam_init))
    return res


def _fourier_rows(ct_ref, st_ref, z, bdc_ref, bds_ref):
    zc = _dot(z, bdc_ref[...].astype(bf16)).astype(bf16)
    zs = _dot(z, bds_ref[...].astype(bf16)).astype(bf16)
    return _dot(ct_ref[...].astype(bf16), zc) - _dot(st_ref[...].astype(bf16), zs)


def _ctx_cd_kernel(q_ref, k_ref, v_ref, z_ref, lam_ref, subln_ref, ct_ref, st_ref, bdc_ref, bds_ref,
                   a_ref, zf_ref, *, lam_init):
    lam = _lambda_full(lam_ref, lam_init)
    for sq in range(CTX_SEQS_CD):
        rows = slice(sq * SEQ, (sq + 1) * SEQ)
        sl = lambda h: slice(h * PAIR_W, (h + 1) * PAIR_W)
        heads = _diff_heads(lambda h: q_ref[rows, sl(h)], lambda h: [(k_ref[rows, sl(h)], False)],
                            lambda h: [v_ref[rows, sl(h)]], lam, subln_ref[...], lam_init, DIFF_GROUP)
        for h in range(H_DIFF):
            a_ref[rows, sl(h)] = heads[h].astype(bf16)
        zf_ref[rows, :] = _fourier_rows(ct_ref, st_ref, z_ref[rows, :], bdc_ref, bds_ref).astype(bf16)


def _ctx_cd(proj, lam, subln, ct, st, bdc, bds, lam_init):
    t = SEQ
    tb = CTX_SEQS_CD * t
    const = lambda b: (0, 0)
    col = lambda c: (lambda b: (b, c))
    return pl.pallas_call(
        functools.partial(_ctx_cd_kernel, lam_init=lam_init),
        grid=(BATCH // CTX_SEQS_CD,),
        in_specs=[pl.BlockSpec((tb, DIFF_W), col(0)), pl.BlockSpec((tb, DIFF_W), col(1)),
                  pl.BlockSpec((tb, DIFF_W), col(2)), pl.BlockSpec((tb, FNET_W), col(3 * DIFF_W // FNET_W)),
                  pl.BlockSpec((4, HEAD_DIM), const), pl.BlockSpec((1, PAIR_W), const),
                  pl.BlockSpec((t, t), const), pl.BlockSpec((t, t), const),
                  pl.BlockSpec((FNET_W, FNET_W), const), pl.BlockSpec((FNET_W, FNET_W), const)],
        out_specs=[pl.BlockSpec((tb, DIFF_W), lambda b: (b, 0)), pl.BlockSpec((tb, FNET_W), lambda b: (b, 0))],
        out_shape=[jax.ShapeDtypeStruct((BATCH * t, DIFF_W), bf16),
                   jax.ShapeDtypeStruct((BATCH * t, FNET_W), bf16)],
        compiler_params=_params(1),
        name="ctx_cd",
    )(proj, proj, proj, proj, lam, subln, ct, st, bdc, bds)


def _lat_cd_kernel(q_ref, k_ref, v_ref, z_ref, ckt_ref, cv_ref, lam_ref, subln_ref, ct_ref, st_ref, bdc_ref, bds_ref,
                   a_ref, zf_ref, *, lam_init):
    lam = _lambda_full(lam_ref, lam_init)
    sl = lambda h: slice(h * PAIR_W, (h + 1) * PAIR_W)
    heads = _diff_heads(lambda h: q_ref[:, sl(h)],
                        lambda h: [(k_ref[:, sl(h)], False), (ckt_ref[0, h].astype(bf16), True)],
                        lambda h: [v_ref[:, sl(h)], cv_ref[0, h].astype(bf16)], lam, subln_ref[...], lam_init,
                        DIFF_GROUP_LAT)
    for h in range(H_DIFF):
        a_ref[:, sl(h)] = heads[h].astype(bf16)
    zf_ref[...] = _fourier_rows(ct_ref, st_ref, z_ref[...], bdc_ref, bds_ref).astype(bf16)


def _lat_cd(proj, ck, cv, lam, subln, ct, st, bdc, bds, lam_init):
    t = DEC_SEQ
    nq = t // TQ_CD
    const = lambda b, i: (0, 0)
    return pl.pallas_call(
        functools.partial(_lat_cd_kernel, lam_init=lam_init),
        grid=(DEC_BATCH, nq),
        in_specs=[pl.BlockSpec((TQ_CD, DIFF_W), lambda b, i: (N_CTX // TQ_CD + b * nq + i, 0)),
                  pl.BlockSpec((t, DIFF_W), lambda b, i: (N_CTX // t + b, 1)),
                  pl.BlockSpec((t, DIFF_W), lambda b, i: (N_CTX // t + b, 2)),
                  pl.BlockSpec((t, FNET_W), lambda b, i: (N_CTX // t + b, 3 * DIFF_W // FNET_W)),
                  pl.BlockSpec((1, H_DIFF, PAIR_W, PAST_LEN), lambda b, i: (b, 0, 0, 0)),
                  pl.BlockSpec((1, H_DIFF, PAST_LEN, PAIR_W), lambda b, i: (b, 0, 0, 0)),
                  pl.BlockSpec((4, HEAD_DIM), const), pl.BlockSpec((1, PAIR_W), const),
                  pl.BlockSpec((TQ_CD, t), lambda b, i: (i, 0)), pl.BlockSpec((TQ_CD, t), lambda b, i: (i, 0)),
                  pl.BlockSpec((FNET_W, FNET_W), const), pl.BlockSpec((FNET_W, FNET_W), const)],
        out_specs=[pl.BlockSpec((TQ_CD, DIFF_W), lambda b, i: (b * nq + i, 0)),
                   pl.BlockSpec((TQ_CD, FNET_W), lambda b, i: (b * nq + i, 0))],
        out_shape=[jax.ShapeDtypeStruct((DEC_BATCH * t, DIFF_W), bf16),
                   jax.ShapeDtypeStruct((DEC_BATCH * t, FNET_W), bf16)],
        compiler_params=_params(2),
        name="lat_cd",
    )(proj, proj, proj, proj, ck, cv, lam, subln, ct, st, bdc, bds)


def _rope_tables():
    t = np.arange(DEC_SEQ)
    quarter = HEAD_DIM // 4
    inv = ROPE_BASE ** (-np.arange(quarter, dtype=np.float64) / quarter)
    ang = np.concatenate([(t // GRID_W)[:, None] * inv, (t % GRID_W)[:, None] * inv], -1)
    cos, sin = np.cos(ang), np.sin(ang)
    reps = LANES // HEAD_DIM
    return (np.tile(np.concatenate([cos, cos], -1), (1, reps)).astype(np.float32),
            np.tile(np.concatenate([-sin, sin], -1), (1, reps)).astype(np.float32))


def _dft_tables(n):
    k = np.arange(n)
    ang = (2.0 * math.pi / n) * ((k[:, None] * k[None, :]) % n)
    return (np.cos(ang) / math.sqrt(n)).astype(np.float32), (np.sin(ang) / math.sqrt(n)).astype(np.float32)


def _block_diag(m, reps):
    return np.kron(np.eye(reps, dtype=m.dtype), m)


def kernel(x_prompt, x_sample, state_ret, cache_win_k, cache_win_v, cache_diff_k, cache_diff_v, c, c_ctx, w_mod, b_mod, ln_g, ln_b, w_in_ab, w_out_ab, ret_log_gamma, ret_gn_g, ret_gn_b, win_sink, w_in_cd, w_out_cd, diff_lambda, diff_subln_g, w_gate, w_up, w_down):
    cond = jnp.concatenate([c_ctx[None, :], c, jnp.zeros((SUBLANES - 1 - DEC_BATCH, D_MODEL), f32)], 0)
    mod = _modulation(cond, w_mod, b_mod).reshape(DEPTH, SUBLANES, 6, D_MODEL)

    rope_tabs = _rope_tables()
    gmat = jnp.asarray(_block_diag(np.full((HEAD_DIM, HEAD_DIM), 1.0 / HEAD_DIM, np.float32),
                                   N_CHUNK // HEAD_DIM), bf16)
    c64, s64 = _dft_tables(FNET_DIM)
    bdc = _block_diag(c64, FNET_GROUPS)
    bds = _block_diag(s64, FNET_GROUPS)
    dft_ctx = _dft_tables(SEQ)
    dft_lat = _dft_tables(DEC_SEQ)

    x_parts = [x_prompt.reshape(N_CTX, D_MODEL), x_sample.reshape(N_LAT, D_MODEL)]
    outs = {}
    for l in range(DEPTH):
        i = l // 2
        if l % 2 == 0:
            lgf = jnp.repeat(ret_log_gamma[i, 0], HEAD_DIM)[None, :]
            lgb = jnp.repeat(ret_log_gamma[i, 1], HEAD_DIM)[None, :]
            gn_g = ret_gn_g[i][None, :]
            gn_b = ret_gn_b[i][None, :]
            sink = win_sink[i][None, :]
            rope_tiles = tuple(range(0, 2 * RET_W // LANES)) + tuple(
                range(4 * RET_W // LANES, (4 * RET_W + WIN_W + KV_W) // LANES))
            kv_tile = (4 * RET_W + WIN_W) // LANES
            kv_shape = (BATCH, 1, KV_WIN, HEAD_DIM, SEQ)
            scale_tiles = tuple(range(4 * RET_W // LANES, (4 * RET_W + WIN_W) // LANES))
            proj, wk_t, wv_t = _proj(x_parts, mod, l, w_in_ab, i, scale_tiles, rope_tabs, rope_tiles,
                                     (kv_shape, kv_shape),
                                     {kv_tile: ("heads", 0, 0), kv_tile + 1: ("heads", 1, 0)})
            ro_c, wo_c, st_c = _ctx_ab(proj, ret_log_gamma[i], sink, lgf, lgb, gmat, gn_g, gn_b)
            ck = cache_win_k[:, i].reshape(DEC_BATCH, PAST_LEN, KV_W)
            cv = cache_win_v[:, i].reshape(DEC_BATCH, PAST_LEN, KV_W)
            ro_l, wo_l = _lat_ab(proj, ret_log_gamma[i], sink, ck, cv, state_ret, i, lgf, lgb, gmat, gn_g, gn_b)
            mix_a, mix_b, w_out = (ro_c, ro_l), (wo_c, wo_l), w_out_ab
            outs.setdefault('state', []).append(st_c[:, None])
            outs.setdefault('win_k', []).append(jnp.transpose(wk_t, (0, 1, 4, 2, 3)))
            outs.setdefault('win_v', []).append(jnp.transpose(wv_t, (0, 1, 4, 2, 3)))
        else:
            lam_init = 0.8 - 0.6 * math.exp(-0.3 * l)
            subln = diff_subln_g[i][None, :]
            rope_tiles = tuple(range(0, 2 * DIFF_W // LANES))
            plan = {}
            for h in range(H_DIFF):
                plan[DIFF_W // LANES + h] = ("pairs", 0, h)
                plan[2 * DIFF_W // LANES + h] = ("plain", 1, h)
            scale_tiles = tuple(range(0, DIFF_W // LANES))
            proj, dk_t, dv_h = _proj(
                x_parts, mod, l, w_in_cd, i, scale_tiles, rope_tabs, rope_tiles,
                ((BATCH, 1, H_DIFF, 2, HEAD_DIM, SEQ), (BATCH, 1, H_DIFF, SEQ, 2 * HEAD_DIM)), plan)
            a_c, z_c = _ctx_cd(proj, diff_lambda[i], subln, dft_ctx[0], dft_ctx[1], bdc, bds, lam_init)
            ck = jnp.transpose(cache_diff_k[:, i], (0, 2, 3, 4, 1)).reshape(DEC_BATCH, H_DIFF, PAIR_W, PAST_LEN)
            cv = jnp.transpose(cache_diff_v[:, i], (0, 2, 1, 3))
            a_l, z_l = _lat_cd(proj, ck, cv, diff_lambda[i], subln, dft_lat[0], dft_lat[1], bdc, bds, lam_init)
            mix_a, mix_b, w_out = (a_c, a_l), (z_c, z_l), w_out_cd
            outs.setdefault('diff_k', []).append(jnp.transpose(dk_t, (0, 1, 5, 2, 3, 4)))
            outs.setdefault('diff_v', []).append(jnp.transpose(dv_h, (0, 1, 3, 2, 4)))
        x_parts = _post(x_parts, mix_a, mix_b, mod, ln_g, ln_b, w_out, w_gate, w_up, w_down, l, i,
                        split_out=(l == DEPTH - 1))

    y_prompt = x_parts[0].reshape(BATCH, SEQ, D_MODEL)
    y_sample = x_parts[1].reshape(DEC_BATCH, DEC_SEQ, D_MODEL)
    cat = lambda parts: parts[0] if len(parts) == 1 else jnp.concatenate(parts, 1)
    return (y_prompt, y_sample, cat(outs['state']), cat(outs['win_k']), cat(outs['win_v']),
            cat(outs['diff_k']), cat(outs['diff_v']))
```

```python
import functools
import math

import jax
import jax.numpy as jnp
import numpy as np
from jax import lax
from jax.experimental import pallas as pl
from jax.experimental.pallas import tpu as pltpu

D_MODEL = 1024
BATCH = 32
SEQ = 256
DEPTH = 2
DEC_BATCH = 2
DEC_SEQ = 1024
PAST_LEN = 512
GRID_W = 64
HEAD_DIM = 64
ROPE_BASE = 10000.0
H_RET = 8
H_WIN = 8
KV_WIN = 2
G_WIN = H_WIN // KV_WIN
WINDOW = 128
H_DIFF = 6
FNET_GROUPS = 4
FNET_DIM = 64
D_FF = 256 * math.ceil(8 * D_MODEL / 3 / 256)
RET_W = H_RET * HEAD_DIM
WIN_W = H_WIN * HEAD_DIM
KV_W = KV_WIN * HEAD_DIM
AB_IN = 4 * RET_W + WIN_W + 2 * KV_W
DIFF_W = H_DIFF * 2 * HEAD_DIM
FNET_W = FNET_GROUPS * FNET_DIM
CD_IN = 3 * DIFF_W + FNET_W
ALPHA = (2 * DEPTH) ** 0.25
LN_EPS = 1e-5
QK_SCALE = HEAD_DIM ** -0.5
LOG2_E = math.log2(math.e)

N_CTX = BATCH * SEQ
N_LAT = DEC_BATCH * DEC_SEQ
N_TOK = N_CTX + N_LAT

LANES = 128
SUBLANES = 8
PAIR_W = 2 * HEAD_DIM
TM = 512
TM_PROJ = 1024
CTX_BLOCKS = N_CTX // TM
TOK_BLOCKS = N_TOK // TM
ROW_GROUPS = 2
FFN_SKEW = 2
TQ = 256
TQ_CD = 512
CTX_SEQS = 4
CTX_SEQS_CD = 4
RET_GROUP = 4
WIN_GROUP = 4
WIN_GROUP_LAT = 4
DIFF_GROUP_LAT = 1
DIFF_GROUP = 3
N_CHUNK = 256
MOD_TN = 1536
NEG_BIG = -1e30
VMEM_LIMIT = 56 * 1024 * 1024

f32 = jnp.float32
bf16 = jnp.bfloat16


def _params(n_axes):
    return pltpu.CompilerParams(dimension_semantics=("arbitrary",) * n_axes,
                                vmem_limit_bytes=VMEM_LIMIT)


def _dot(a, b):
    return jnp.dot(a, b, preferred_element_type=f32)


def _dot_nt(a, b):
    return lax.dot_general(a, b, (((1,), (1,)), ((), ())), preferred_element_type=f32)


def _ln(x):
    mu = jnp.mean(x, -1, keepdims=True)
    d = x - mu
    var = jnp.mean(d * d, -1, keepdims=True)
    return d * lax.rsqrt(var + LN_EPS)


def _silu(x):
    return x * jax.nn.sigmoid(x)


def _split_bf16(x):
    hi = x.astype(bf16)
    lo = (x - hi.astype(f32)).astype(bf16)
    return hi, lo


def _lane_half_mask(shape):
    return (lax.broadcasted_iota(jnp.int32, shape, len(shape) - 1) & HEAD_DIM) == 0


def _mod_kernel(c_ref, w_ref, b_ref, o_ref):
    layer = pl.program_id(0)
    a = _silu(c_ref[...])
    o_ref[0] = _dot(a, w_ref[0]) + b_ref[pl.ds(layer, 1), :]


def _modulation(cond, w_mod, b_mod):
    tn = MOD_TN
    rows = cond.shape[0]
    return pl.pallas_call(
        _mod_kernel,
        grid=(DEPTH, 6 * D_MODEL // tn),
        in_specs=[pl.BlockSpec((rows, D_MODEL), lambda l, j: (0, 0)),
                  pl.BlockSpec((1, D_MODEL, tn), lambda l, j: (l, 0, j)),
                  pl.BlockSpec((DEPTH, tn), lambda l, j: (0, j))],
        out_specs=pl.BlockSpec((1, rows, tn), lambda l, j: (l, 0, j)),
        out_shape=jax.ShapeDtypeStruct((DEPTH, rows, 6 * D_MODEL), f32),
        compiler_params=_params(2),
        name="modulation",
    )(cond, w_mod, b_mod)


def _tok(i, n_w):
    return jnp.maximum(i - n_w, 0)


def _ctx_blk(t, tm=TM):
    return jnp.minimum(t, N_CTX // tm - 1)


def _lat_blk(t, tm=TM):
    return jnp.maximum(t - N_CTX // tm, 0)


def _mod_row(t, tm=TM):
    return jnp.where(t < N_CTX // tm, 0, 1 + _lat_blk(t, tm) * tm // DEC_SEQ)


def _token_specs(parts, n_w, tm=TM):
    width = parts[0].shape[1]
    if len(parts) == 1:
        return [pl.BlockSpec((tm, width), lambda i: (_tok(i, n_w), 0))]
    return [pl.BlockSpec((tm, width), lambda i: (_ctx_blk(_tok(i, n_w), tm), 0)),
            pl.BlockSpec((tm, width), lambda i: (_lat_blk(_tok(i, n_w), tm), 0))]


def _pick(refs, is_ctx, rs):
    if len(refs) == 1:
        return refs[0][rs, :]
    if isinstance(is_ctx, bool):
        return refs[0 if is_ctx else 1][rs, :]
    return jnp.where(is_ctx, refs[0][rs, :], refs[1][rs, :])


def _rope_pair(y, cos, sin_signed):
    first_half = (lax.broadcasted_iota(jnp.int32, y.shape, 1) & (HEAD_DIM // 2)) == 0
    swapped = jnp.where(first_half, pltpu.roll(y, LANES - HEAD_DIM // 2, 1), pltpu.roll(y, HEAD_DIM // 2, 1))
    return y * cos + swapped * sin_signed


def _proj_kernel(*refs, n_x, n_cache, n_w, rope_tiles, scale_tiles, cache_plan):
    x_refs = refs[:n_x]
    mod_ref, w_ref, cos_ref, sin_ref, o_ref = refs[n_x:n_x + 5]
    cache_refs = refs[n_x + 5:n_x + 5 + n_cache]
    wbf_ref, u_ref = refs[n_x + 5 + n_cache:]
    i = pl.program_id(0)

    @pl.when(i < n_w)
    def _():
        wbf_ref[i] = w_ref[0].astype(bf16)

    def tokens(is_ctx):
        x_ref = x_refs[0] if is_ctx else x_refs[-1]
        shift = mod_ref[0, 0:1, :]
        scale = mod_ref[0, 1:2, :]
        groups = [slice(b * SEQ, (b + 1) * SEQ) for b in range(TM_PROJ // SEQ)]
        for rs in groups:
            u_ref[rs, :] = (_ln(x_ref[rs, :]) * (1.0 + scale) + shift).astype(bf16)
        for c in range(n_w):
            y_all = _dot(u_ref[...], wbf_ref[c])
            for b, rs in enumerate(groups):
                y = y_all[rs, :]
                for t in range(N_CHUNK // LANES):
                    tile = c * (N_CHUNK // LANES) + t
                    piece = y[:, t * LANES:(t + 1) * LANES]
                    if tile in rope_tiles and not is_ctx:
                        piece = _rope_pair(piece, cos_ref[rs, :], sin_ref[rs, :])
                    if tile in scale_tiles:
                        piece = piece * (QK_SCALE * LOG2_E)
                    o_ref[rs, tile * LANES:(tile + 1) * LANES] = piece.astype(o_ref.dtype)
                    if tile in cache_plan and is_ctx:
                        kind, out_idx, slot = cache_plan[tile]
                        c_ref = cache_refs[out_idx]
                        if kind == "plain":
                            c_ref[b, 0, slot] = piece
                        else:
                            piece_t = piece.T
                            if kind == "heads":
                                c_ref[b, 0, 0] = piece_t[0:HEAD_DIM]
                                c_ref[b, 0, 1] = piece_t[HEAD_DIM:]
                            else:
                                c_ref[b, 0, slot, 0] = piece_t[0:HEAD_DIM]
                                c_ref[b, 0, slot, 1] = piece_t[HEAD_DIM:]

    t = i - n_w

    @pl.when(jnp.logical_and(t >= 0, t < N_CTX // TM_PROJ))
    def _():
        tokens(True)

    @pl.when(t >= N_CTX // TM_PROJ)
    def _():
        tokens(False)


def _proj(x_parts, mod, mod_layer, w_all, layer, scale_tiles, rope_tabs, rope_tiles, cache_shapes, cache_plan):
    n_out = w_all.shape[2]
    n_w = n_out // N_CHUNK
    tm = TM_PROJ
    nb = DEC_SEQ // tm
    tok = lambda i: _tok(i, n_w)
    in_specs = _token_specs(x_parts, n_w, tm) + [
        pl.BlockSpec((None, 1, 6, D_MODEL), lambda i: (mod_layer, _mod_row(tok(i), tm), 0, 0)),
        pl.BlockSpec((1, D_MODEL, N_CHUNK), lambda i: (layer, 0, jnp.minimum(i, n_w - 1))),
        pl.BlockSpec((tm, LANES), lambda i: (_lat_blk(tok(i), tm) % nb, 0)),
        pl.BlockSpec((tm, LANES), lambda i: (_lat_blk(tok(i), tm) % nb, 0))]
    out_specs = [pl.BlockSpec((tm, n_out), lambda i: (tok(i), 0))]
    out_shape = [jax.ShapeDtypeStruct((N_TOK, n_out), bf16)]
    for shp in cache_shapes:
        blk = (tm // SEQ,) + tuple(shp[1:])
        out_specs.append(pl.BlockSpec(blk, lambda i, nd=len(shp): (_ctx_blk(tok(i), tm),) + (0,) * (nd - 1)))
        out_shape.append(jax.ShapeDtypeStruct(tuple(shp), f32))
    return pl.pallas_call(
        functools.partial(_proj_kernel, n_x=len(x_parts), n_cache=len(cache_shapes), n_w=n_w,
                          rope_tiles=frozenset(rope_tiles), scale_tiles=frozenset(scale_tiles),
                          cache_plan=dict(cache_plan)),
        grid=(n_w + N_TOK // tm,),
        in_specs=in_specs,
        out_specs=out_specs,
        out_shape=out_shape,
        scratch_shapes=[pltpu.VMEM((n_w, D_MODEL, N_CHUNK), bf16), pltpu.VMEM((tm, D_MODEL), bf16)],
        compiler_params=_params(1),
        name="proj",
    )(*x_parts, mod, w_all, *rope_tabs)


def _post_kernel(*refs, n_x, n_y, ka, kb, n_w):
    x_refs = refs[:n_x]
    (ac_ref, al_ref, bc_ref, bl_ref, mod_ref, lng_ref, lnb_ref,
     wo_ref, wg_ref, wu_ref, wd_ref) = refs[n_x:n_x + 11]
    y_refs = refs[n_x + 11:n_x + 11 + n_y]
    wo_s, wg_s, wu_s, wd_s, x1_ref, u_ref, h_ref, y_ref = refs[n_x + 11 + n_y:]
    i = pl.program_id(0)
    lead = n_w - 1
    gate1 = mod_ref[0, 2:3, :]
    shift2 = mod_ref[0, 3:4, :]
    scale2 = mod_ref[0, 4:5, :]
    gate2 = mod_ref[0, 5:6, :]
    groups = [slice(r * TM // ROW_GROUPS, (r + 1) * TM // ROW_GROUPS) for r in range(ROW_GROUPS)]

    def mix_in(rs, is_ctx):
        a = _pick((ac_ref, al_ref), is_ctx, rs)
        b = _pick((bc_ref, bl_ref), is_ctx, rs)
        pieces = ([a[:, c:c + N_CHUNK] for c in range(0, ka, N_CHUNK)]
                  + [b[:, c:c + N_CHUNK] for c in range(0, kb, N_CHUNK)])
        h = functools.reduce(lambda s, p: s + p, [_dot(p, wo_s[c]) for c, p in enumerate(pieces)])
        x1 = _ln(ALPHA * _pick(x_refs, is_ctx, rs) + gate1 * h) * lng_ref[0, 0:1, :] + lnb_ref[0, 0:1, :]
        x1_ref[rs, :] = x1
        u_ref[rs, :] = (_ln(x1) * (1.0 + scale2) + shift2).astype(bf16)

    def finish(rs, ffn, is_ctx):
        y = _ln(ALPHA * x1_ref[rs, :] + gate2 * ffn) * lng_ref[0, 1:2, :] + lnb_ref[0, 1:2, :]
        if n_y == 1:
            y_refs[0][rs, :] = y
        elif isinstance(is_ctx, bool):
            y_refs[0 if is_ctx else 1][rs, :] = y
        else:
            y_ref[rs, :] = y

    def emit(is_ctx):
        if n_y == 2 and not isinstance(is_ctx, bool):
            @pl.when(is_ctx)
            def _():
                y_refs[0][...] = y_ref[...]

            @pl.when(jnp.logical_not(is_ctx))
            def _():
                y_refs[1][...] = y_ref[...]

    @pl.when(i < n_w)
    def _():
        wg_s[i] = wg_ref[0].astype(bf16)
        wu_s[i] = wu_ref[0].astype(bf16)
        wd_s[i] = wd_ref[0].astype(bf16)

        @pl.when(i == 0)
        def _():
            for c in range((ka + kb) // N_CHUNK):
                wo_s[c] = wo_ref[0, c * N_CHUNK:(c + 1) * N_CHUNK, :].astype(bf16)
            for rs in groups:
                mix_in(rs, True)
                y_ref[rs, :] = jnp.zeros((TM // ROW_GROUPS, D_MODEL), f32)

        for rs in groups:
            g = _dot(u_ref[rs, :], wg_s[i])
            up = _dot(u_ref[rs, :], wu_s[i])
            y_ref[rs, :] += _dot((_silu(g) * up).astype(bf16), wd_s[i])

        @pl.when(i == lead)
        def _():
            for rs in groups:
                finish(rs, y_ref[rs, :], True)

    @pl.when(i >= n_w)
    def _():
        is_ctx = (i - lead) < CTX_BLOCKS
        for rs in groups:
            mix_in(rs, is_ctx)

        def ffn_chunk(rs, c):
            g = _dot(u_ref[rs, :], wg_s[c])
            up = _dot(u_ref[rs, :], wu_s[c])
            h_ref[rs, c * N_CHUNK:(c + 1) * N_CHUNK] = (_silu(g) * up).astype(bf16)

        def ffn_down(rs):
            finish(rs, functools.reduce(
                lambda s, p: s + p,
                [_dot(h_ref[rs, c * N_CHUNK:(c + 1) * N_CHUNK], wd_s[c]) for c in range(n_w)]), is_ctx)

        for c in range(n_w + FFN_SKEW * (ROW_GROUPS - 1)):
            for r, rs in enumerate(groups):
                cc = c - FFN_SKEW * r
                if 0 <= cc < n_w:
                    ffn_chunk(rs, cc)
                if cc == n_w - 1:
                    ffn_down(rs)
        emit(is_ctx)


def _post(x_parts, mix_a, mix_b, mod, ln_g, ln_b, w_out, w_gate, w_up, w_down, layer, mix_layer, split_out):
    ka, kb = mix_a[0].shape[1], mix_b[0].shape[1]
    n_w = D_FF // N_CHUNK
    lead = n_w - 1
    tok = lambda i: _tok(i, lead)
    lay = lambda i: (layer, 0, 0)
    in_specs = (_token_specs(x_parts, lead) + _token_specs(mix_a, lead) + _token_specs(mix_b, lead) + [
        pl.BlockSpec((None, 1, 6, D_MODEL), lambda i: (layer, _mod_row(tok(i)), 0, 0)),
        pl.BlockSpec((1, 2, D_MODEL), lay),
        pl.BlockSpec((1, 2, D_MODEL), lay),
        pl.BlockSpec((1, ka + kb, D_MODEL), lambda i: (mix_layer, 0, 0), pipeline_mode=pl.Buffered(1)),
        pl.BlockSpec((1, D_MODEL, N_CHUNK), lambda i: (layer, 0, jnp.minimum(i, n_w - 1))),
        pl.BlockSpec((1, D_MODEL, N_CHUNK), lambda i: (layer, 0, jnp.minimum(i, n_w - 1))),
        pl.BlockSpec((1, N_CHUNK, D_MODEL), lambda i: (layer, jnp.minimum(i, n_w - 1), 0))])
    if split_out:
        out_specs = [pl.BlockSpec((TM, D_MODEL), lambda i: (_ctx_blk(tok(i)), 0)),
                     pl.BlockSpec((TM, D_MODEL), lambda i: (_lat_blk(tok(i)), 0))]
        out_shape = [jax.ShapeDtypeStruct((N_CTX, D_MODEL), f32), jax.ShapeDtypeStruct((N_LAT, D_MODEL), f32)]
    else:
        out_specs = [pl.BlockSpec((TM, D_MODEL), lambda i: (tok(i), 0))]
        out_shape = [jax.ShapeDtypeStruct((N_TOK, D_MODEL), f32)]
    return pl.pallas_call(
        functools.partial(_post_kernel, n_x=len(x_parts), n_y=len(out_shape), ka=ka, kb=kb, n_w=n_w),
        grid=(lead + TOK_BLOCKS,),
        in_specs=in_specs,
        out_specs=out_specs,
        out_shape=out_shape,
        scratch_shapes=[pltpu.VMEM(((ka + kb) // N_CHUNK, N_CHUNK, D_MODEL), bf16),
                        pltpu.VMEM((n_w, D_MODEL, N_CHUNK), bf16),
                        pltpu.VMEM((n_w, D_MODEL, N_CHUNK), bf16), pltpu.VMEM((n_w, N_CHUNK, D_MODEL), bf16),
                        pltpu.VMEM((TM, D_MODEL), f32), pltpu.VMEM((TM, D_MODEL), bf16),
                        pltpu.VMEM((TM, D_FF), bf16), pltpu.VMEM((TM, D_MODEL), f32)],
        compiler_params=_params(1),
        name="post",
    )(*x_parts, *mix_a, *mix_b, mod, ln_g, ln_b, w_out, w_gate, w_up, w_down)


def _group_norm_gate(ro, rg, gmat, gn_g, gn_b):
    def gmean(parts):
        cols = []
        for c in range(0, RET_W, N_CHUNK):
            cols.append(sum(_dot(p[:, c:c + N_CHUNK], gmat) for p in parts))
        return jnp.concatenate(cols, -1)

    d = ro - gmean(_split_bf16(ro))
    var = gmean([(d * d).astype(bf16)])
    y = d * lax.rsqrt(var + LN_EPS) * gn_g + gn_b
    return _silu(rg.astype(f32)) * y


def _dup_head(x, j):
    first = _lane_half_mask(x.shape)
    keep = first if j == 0 else jnp.logical_not(first)
    xm = jnp.where(keep, x.astype(f32), 0.0)
    return xm + pltpu.roll(xm, HEAD_DIM, 1)


def _softmax_parts(scores, sink):
    m = sink
    for s in scores:
        m = jnp.maximum(m, jnp.max(s, -1, keepdims=True))
    es = [jnp.exp2(s - m) for s in scores]
    denom = jnp.exp2(sink - m)
    for e in es:
        denom = denom + jnp.sum(e, -1, keepdims=True)
    return es, denom


def _retention_tables(lg_ref, lgf_ref, lgb_ref, dmask_ref, kdec_ref, n):
    row = lax.broadcasted_iota(jnp.int32, (n, n), 0)
    col = lax.broadcasted_iota(jnp.int32, (n, n), 1)
    diff = (row - col).astype(f32)
    diag = jnp.where(row == col, 2.0 * QK_SCALE, QK_SCALE)
    for h in range(H_RET):
        dmask_ref[h] = jnp.exp(jnp.where(diff >= 0, lg_ref[0, h] * diff, -lg_ref[1, h] * diff)) * diag
    t = lax.broadcasted_iota(jnp.int32, (n, RET_W), 0).astype(f32)
    kdec_ref[0] = jnp.exp(lgf_ref[...] * (n - 1.0 - t)) * QK_SCALE
    kdec_ref[1] = jnp.exp(lgb_ref[...] * t) * QK_SCALE


def _retention_intra(pairs, q_of, k_of, v_of, dmask_ref):
    first = _lane_half_mask(k_of(pairs[0]).shape)
    masked = {}
    for p in pairs:
        kb = k_of(p)
        for e in range(2):
            keep = first if e == 0 else jnp.logical_not(first)
            s = _dot_nt(q_of(p), jnp.where(keep, kb, jnp.zeros_like(kb))) * dmask_ref[2 * p + e]
            masked[p, e] = s.astype(bf16)
    outs = {}
    for p in pairs:
        pv = [_dot(masked[p, e], v_of(p)) for e in range(2)]
        outs[p] = jnp.where(_lane_half_mask(pv[0].shape), pv[0], pv[1])
    return outs


def _window_group(subs, q_of, k_parts_of, v_parts_of, masks, sink_of):
    scores = {}
    for key in subs:
        parts = [_dot_nt(q_of(key), k) for k in k_parts_of(key)]
        scores[key] = [sc if mk is None else jnp.where(mk, sc, NEG_BIG) for sc, mk in zip(parts, masks)]
    probs = {}
    for key in subs:
        es, denom = _softmax_parts(scores[key], sink_of(key))
        probs[key] = ([ex.astype(bf16) for ex in es], denom)
    outs = {}
    for key in subs:
        es, denom = probs[key]
        pv = functools.reduce(lambda x, y: x + y, [_dot(ex, v) for ex, v in zip(es, v_parts_of(key))])
        outs[key] = pv / denom
    return outs


def _ctx_ab_kernel(lg_ref, sink_ref, rq_ref, rk_ref, rv_ref, rg_ref, wq_ref, wk_ref, wv_ref,
                   lgf_ref, lgb_ref, gmat_ref, gng_ref, gnb_ref,
                   ro_ref, wo_ref, st_ref, dmask_ref, kdec_ref, ret_ref):
    t_len = SEQ

    @pl.when(pl.program_id(0) == 0)
    def _():
        _retention_tables(lg_ref, lgf_ref, lgb_ref, dmask_ref, kdec_ref, t_len)

    first = _lane_half_mask((t_len, PAIR_W))
    for sq in range(CTX_SEQS):
        rows = slice(sq * t_len, (sq + 1) * t_len)
        psl = lambda p: slice(p * PAIR_W, (p + 1) * PAIR_W)
        for p0 in range(0, H_RET // 2, RET_GROUP):
            pairs = list(range(p0, p0 + RET_GROUP))
            intra = _retention_intra(pairs, lambda p: rq_ref[rows, psl(p)], lambda p: rk_ref[rows, psl(p)],
                                     lambda p: rv_ref[rows, psl(p)], dmask_ref)
            for p in pairs:
                ret_ref[rows, psl(p)] = intra[p]
        for p in range(H_RET // 2):
            sl = psl(p)
            kb = rk_ref[rows, sl]
            v = rv_ref[rows, sl]
            for d in range(2):
                kd_t = (kb * kdec_ref[d, :, sl]).T.astype(bf16)
                st = _dot(kd_t, v)
                st_ref[sq, d, 2 * p] = st[0:HEAD_DIM, 0:HEAD_DIM]
                st_ref[sq, d, 2 * p + 1] = pltpu.roll(st[HEAD_DIM:, :], HEAD_DIM, 1)[:, 0:HEAD_DIM]
        ro_ref[rows, :] = _group_norm_gate(ret_ref[rows, :], rg_ref[rows, :], gmat_ref[...], gng_ref[...],
                                           gnb_ref[...]).astype(bf16)

        k_dup = [_dup_head(wk_ref[rows, :], j).astype(bf16) for j in range(KV_WIN)]
        v_dup = [_dup_head(wv_ref[rows, :], j).astype(bf16) for j in range(KV_WIN)]

        def q_masked(key):
            qp, e = key
            qb = wq_ref[rows, qp * PAIR_W:(qp + 1) * PAIR_W]
            return jnp.where(first if e == 0 else jnp.logical_not(first), qb, jnp.zeros_like(qb))

        kv_of = lambda key: key[0] * 2 // G_WIN
        for g0 in range(0, H_WIN // 2, WIN_GROUP):
            subs = [(qp, e) for qp in range(g0, g0 + WIN_GROUP) for e in range(2)]
            outs = _window_group(subs, q_masked, lambda key: [k_dup[kv_of(key)]], lambda key: [v_dup[kv_of(key)]],
                                 [None], lambda key: sink_ref[0, 2 * key[0] + key[1]] * LOG2_E)
            for qp in range(g0, g0 + WIN_GROUP):
                wo_ref[rows, qp * PAIR_W:(qp + 1) * PAIR_W] = jnp.where(first, outs[qp, 0], outs[qp, 1]).astype(bf16)


def _ctx_ab(proj, log_gamma, sink, lgf_lanes, lgb_lanes, gmat, gn_g, gn_b):
    t = SEQ
    tb = CTX_SEQS * t
    smem = pl.BlockSpec(memory_space=pltpu.SMEM)
    const = lambda b: (0, 0)
    col = lambda c: (lambda b: (b, c))
    return pl.pallas_call(
        _ctx_ab_kernel,
        grid=(BATCH // CTX_SEQS,),
        in_specs=[smem, smem,
                  pl.BlockSpec((tb, RET_W), col(0)), pl.BlockSpec((tb, RET_W), col(1)),
                  pl.BlockSpec((tb, RET_W), col(2)), pl.BlockSpec((tb, RET_W), col(3)),
                  pl.BlockSpec((tb, WIN_W), col(4)),
                  pl.BlockSpec((tb, KV_W), col((4 * RET_W + WIN_W) // KV_W)),
                  pl.BlockSpec((tb, KV_W), col((4 * RET_W + WIN_W) // KV_W + 1)),
                  pl.BlockSpec((1, RET_W), const), pl.BlockSpec((1, RET_W), const),
                  pl.BlockSpec((N_CHUNK, N_CHUNK), const),
                  pl.BlockSpec((1, RET_W), const), pl.BlockSpec((1, RET_W), const)],
        out_specs=[pl.BlockSpec((tb, RET_W), lambda b: (b, 0)),
                   pl.BlockSpec((tb, WIN_W), lambda b: (b, 0)),
                   pl.BlockSpec((CTX_SEQS, 2, H_RET, HEAD_DIM, HEAD_DIM), lambda b: (b, 0, 0, 0, 0))],
        out_shape=[jax.ShapeDtypeStruct((BATCH * t, RET_W), bf16),
                   jax.ShapeDtypeStruct((BATCH * t, WIN_W), bf16),
                   jax.ShapeDtypeStruct((BATCH, 2, H_RET, HEAD_DIM, HEAD_DIM), f32)],
        scratch_shapes=[pltpu.VMEM((H_RET, t, t), f32), pltpu.VMEM((2, t, RET_W), f32),
                        pltpu.VMEM((tb, RET_W), f32)],
        compiler_params=_params(1),
        name="ctx_ab",
    )(log_gamma, sink, proj, proj, proj, proj, proj, proj, proj, lgf_lanes, lgb_lanes, gmat, gn_g, gn_b)


def _pair_state(s0_ref, d, p):
    zero = jnp.zeros((HEAD_DIM, HEAD_DIM), f32)
    top = jnp.concatenate([s0_ref[0, 0, d, 2 * p], zero], 1)
    bottom = jnp.concatenate([zero, s0_ref[0, 0, d, 2 * p + 1]], 1)
    return jnp.concatenate([top, bottom], 0)


def _lat_ab_kernel(lg_ref, sink_ref, rq_ref, rk_ref, rv_ref, rg_ref, wq_ref, wk_ref, wv_ref, ck_ref, cv_ref,
                   s0_ref, lgf_ref, lgb_ref, gmat_ref, gng_ref, gnb_ref,
                   ro_ref, wo_ref, ret_ref, dmask_ref, kdec_ref, qdec_ref, sf_ref, sb_ref):
    t_len = DEC_SEQ
    n_chunks = t_len // TQ
    chunk = pl.program_id(1)
    q0 = pl.multiple_of(chunk * TQ, TQ)
    first = _lane_half_mask((TQ, PAIR_W))

    @pl.when(jnp.logical_and(pl.program_id(0) == 0, chunk == 0))
    def _():
        _retention_tables(lg_ref, lgf_ref, lgb_ref, dmask_ref, kdec_ref, TQ)
        t = lax.broadcasted_iota(jnp.int32, (TQ, RET_W), 0).astype(f32)
        qdec_ref[0] = jnp.exp(lgf_ref[...] * (t + 1.0))
        qdec_ref[1] = jnp.exp(lgb_ref[...] * (TQ - t))

    @pl.when(chunk == 0)
    def _():
        r = lax.broadcasted_iota(jnp.int32, (PAIR_W, PAIR_W), 0)
        c_ = lax.broadcasted_iota(jnp.int32, (PAIR_W, PAIR_W), 1)
        same_head = (r < HEAD_DIM) == (c_ < HEAD_DIM)
        for p in range(H_RET // 2):
            sl = slice(p * PAIR_W, (p + 1) * PAIR_W)
            kv = []
            for c in range(n_chunks):
                rows = slice(c * TQ, (c + 1) * TQ)
                kc = rk_ref[rows, sl]
                vc = rv_ref[rows, sl]
                kv.append([jnp.where(same_head, _dot((kc * kdec_ref[d, :, sl]).T.astype(bf16), vc), 0.0)
                           for d in range(2)])
            state = _pair_state(s0_ref, 0, p)
            for c in range(n_chunks):
                sf_ref[c, p] = state
                state = state * jnp.exp(lgf_ref[:, sl] * TQ) + kv[c][0]
            state = _pair_state(s0_ref, 1, p)
            for c in reversed(range(n_chunks)):
                sb_ref[c, p] = state
                state = state * jnp.exp(lgb_ref[:, sl] * TQ) + kv[c][1]

    psl = lambda p: slice(p * PAIR_W, (p + 1) * PAIR_W)
    intra = {}
    for p0 in range(0, H_RET // 2, RET_GROUP):
        intra.update(_retention_intra(list(range(p0, p0 + RET_GROUP)), lambda p: rq_ref[:, psl(p)],
                                      lambda p: rk_ref[pl.ds(q0, TQ), psl(p)],
                                      lambda p: rv_ref[pl.ds(q0, TQ), psl(p)], dmask_ref))
    for p in range(H_RET // 2):
        sl = psl(p)
        q = rq_ref[:, sl]
        o = intra[p]
        o = o + _dot(q, sf_ref[chunk, p].astype(bf16)) * qdec_ref[0, :, sl]
        o = o + _dot(q, sb_ref[chunk, p].astype(bf16)) * qdec_ref[1, :, sl]
        ret_ref[:, sl] = o
    ro_ref[...] = _group_norm_gate(ret_ref[...], rg_ref[...], gmat_ref[...], gng_ref[...], gnb_ref[...]).astype(bf16)

    band = TQ + 2 * WINDOW
    k_start = pl.multiple_of(jnp.clip(q0 - WINDOW, 0, t_len - band), LANES)
    qi = q0 + lax.broadcasted_iota(jnp.int32, (TQ, band), 0)
    kj = k_start + lax.broadcasted_iota(jnp.int32, (TQ, band), 1)
    in_band = jnp.abs(qi - kj) <= WINDOW
    k_parts = [[_dup_head(wk_ref[pl.ds(k_start, band), :], j).astype(bf16), _dup_head(ck_ref[0], j).astype(bf16)]
               for j in range(KV_WIN)]
    v_parts = [[_dup_head(wv_ref[pl.ds(k_start, band), :], j).astype(bf16), _dup_head(cv_ref[0], j).astype(bf16)]
               for j in range(KV_WIN)]

    def q_masked(key):
        qp, e = key
        qb = wq_ref[:, qp * PAIR_W:(qp + 1) * PAIR_W]
        return jnp.where(first if e == 0 else jnp.logical_not(first), qb, jnp.zeros_like(qb))

    kv_of = lambda key: key[0] * 2 // G_WIN
    for g0 in range(0, H_WIN // 2, WIN_GROUP_LAT):
        subs = [(qp, e) for qp in range(g0, g0 + WIN_GROUP_LAT) for e in range(2)]
        outs = _window_group(subs, q_masked, lambda key: k_parts[kv_of(key)], lambda key: v_parts[kv_of(key)],
                             [in_band, None], lambda key: sink_ref[0, 2 * key[0] + key[1]] * LOG2_E)
        for qp in range(g0, g0 + WIN_GROUP_LAT):
            wo_ref[:, qp * PAIR_W:(qp + 1) * PAIR_W] = jnp.where(first, outs[qp, 0], outs[qp, 1]).astype(bf16)


def _lat_ab(proj, log_gamma, sink, ck, cv, state, layer, lgf_lanes, lgb_lanes, gmat, gn_g, gn_b):
    t = DEC_SEQ
    nq = t // TQ
    smem = pl.BlockSpec(memory_space=pltpu.SMEM)
    const = lambda b, i: (0, 0)
    qcol = lambda c: (lambda b, i: (N_CTX // TQ + b * nq + i, c))
    bcol = lambda c: (lambda b, i: (N_CTX // t + b, c))
    kv_col = (4 * RET_W + WIN_W) // KV_W
    return pl.pallas_call(
        _lat_ab_kernel,
        grid=(DEC_BATCH, nq),
        in_specs=[smem, smem,
                  pl.BlockSpec((TQ, RET_W), qcol(0)), pl.BlockSpec((t, RET_W), bcol(1)),
                  pl.BlockSpec((t, RET_W), bcol(2)), pl.BlockSpec((TQ, RET_W), qcol(3)),
                  pl.BlockSpec((TQ, WIN_W), qcol(4)),
                  pl.BlockSpec((t, KV_W), bcol(kv_col)), pl.BlockSpec((t, KV_W), bcol(kv_col + 1)),
                  pl.BlockSpec((1, PAST_LEN, KV_W), lambda b, i: (b, 0, 0)),
                  pl.BlockSpec((1, PAST_LEN, KV_W), lambda b, i: (b, 0, 0)),
                  pl.BlockSpec((1, 1, 2, H_RET, HEAD_DIM, HEAD_DIM), lambda b, i: (b, layer, 0, 0, 0, 0)),
                  pl.BlockSpec((1, RET_W), const), pl.BlockSpec((1, RET_W), const),
                  pl.BlockSpec((N_CHUNK, N_CHUNK), const),
                  pl.BlockSpec((1, RET_W), const), pl.BlockSpec((1, RET_W), const)],
        out_specs=[pl.BlockSpec((TQ, RET_W), lambda b, i: (b * nq + i, 0)),
                   pl.BlockSpec((TQ, WIN_W), lambda b, i: (b * nq + i, 0))],
        out_shape=[jax.ShapeDtypeStruct((DEC_BATCH * t, RET_W), bf16),
                   jax.ShapeDtypeStruct((DEC_BATCH * t, WIN_W), bf16)],
        scratch_shapes=[pltpu.VMEM((TQ, RET_W), f32), pltpu.VMEM((H_RET, TQ, TQ), f32),
                        pltpu.VMEM((2, TQ, RET_W), f32), pltpu.VMEM((2, TQ, RET_W), f32),
                        pltpu.VMEM((t // TQ, H_RET // 2, PAIR_W, PAIR_W), f32),
                        pltpu.VMEM((t // TQ, H_RET // 2, PAIR_W, PAIR_W), f32)],
        compiler_params=_params(2),
        name="lat_ab",
    )(log_gamma, sink, proj, proj, proj, proj, proj, proj, proj, ck, cv, state,
      lgf_lanes, lgb_lanes, gmat, gn_g, gn_b)


def _lambda_full(lam_ref, lam_init):
    lam = lam_ref[...]
    a = jnp.sum(lam[0:1, :] * lam[1:2, :], -1, keepdims=True)
    b = jnp.sum(lam[2:3, :] * lam[3:4, :], -1, keepdims=True)
    return jnp.exp(a) - jnp.exp(b) + lam_init


def _diff_heads(q_of, k_parts_of, v_parts_of, lam, subln, lam_init, group):
    res = []
    for h0 in range(0, H_DIFF, group):
        res += _diff_head_group(range(h0, h0 + group), q_of, k_parts_of, v_parts_of, lam, subln, lam_init)
    return res


def _diff_head_group(heads, q_of, k_parts_of, v_parts_of, lam, subln, lam_init):
    subs = [(h, e) for h in heads for e in range(2)]
    scores = {}
    for h, e in subs:
        q = q_of(h)
        fm = _lane_half_mask(q.shape)
        q_sub = jnp.where(fm if e == 0 else jnp.logical_not(fm), q, jnp.zeros_like(q))
        scores[h, e] = [_dot(q_sub, k) if transposed else _dot_nt(q_sub, k) for k, transposed in k_parts_of(h)]
    probs = {}
    for key in subs:
        m = scores[key][0].max(-1, keepdims=True)
        for sc in scores[key][1:]:
            m = jnp.maximum(m, sc.max(-1, keepdims=True))
        es = [jnp.exp2(sc - m) for sc in scores[key]]
        denom = es[0].sum(-1, keepdims=True)
        for ex in es[1:]:
            denom = denom + ex.sum(-1, keepdims=True)
        probs[key] = ([ex.astype(bf16) for ex in es], denom)
    outs = {}
    for h, e in subs:
        es, denom = probs[h, e]
        pv = functools.reduce(lambda x, y: x + y, [_dot(ex, v) for v, ex in zip(v_parts_of(h), es)])
        outs[h, e] = pv / denom
    res = []
    for h in heads:
        a = outs[h, 0] - lam * outs[h, 1]
        res.append(a * lax.rsqrt(jnp.mean(a * a, -1, keepdims=True) + LN_EPS) * subln * (1.0 - lam_init))
    return res


def _fourier_rows(ct_ref, st_ref, z, bdc_ref, bds_ref):
    zc = _dot(z, bdc_ref[...].astype(bf16)).astype(bf16)
    zs = _dot(z, bds_ref[...].astype(bf16)).astype(bf16)
    return _dot(ct_ref[...].astype(bf16), zc) - _dot(st_ref[...].astype(bf16), zs)


def _ctx_cd_kernel(q_ref, k_ref, v_ref, z_ref, lam_ref, subln_ref, ct_ref, st_ref, bdc_ref, bds_ref,
                   a_ref, zf_ref, *, lam_init):
    lam = _lambda_full(lam_ref, lam_init)
    for sq in range(CTX_SEQS_CD):
        rows = slice(sq * SEQ, (sq + 1) * SEQ)
        sl = lambda h: slice(h * PAIR_W, (h + 1) * PAIR_W)
        heads = _diff_heads(lambda h: q_ref[rows, sl(h)], lambda h: [(k_ref[rows, sl(h)], False)],
                            lambda h: [v_ref[rows, sl(h)]], lam, subln_ref[...], lam_init, DIFF_GROUP)
        for h in range(H_DIFF):
            a_ref[rows, sl(h)] = heads[h].astype(bf16)
        zf_ref[rows, :] = _fourier_rows(ct_ref, st_ref, z_ref[rows, :], bdc_ref, bds_ref).astype(bf16)


def _ctx_cd(proj, lam, subln, ct, st, bdc, bds, lam_init):
    t = SEQ
    tb = CTX_SEQS_CD * t
    const = lambda b: (0, 0)
    col = lambda c: (lambda b: (b, c))
    return pl.pallas_call(
        functools.partial(_ctx_cd_kernel, lam_init=lam_init),
        grid=(BATCH // CTX_SEQS_CD,),
        in_specs=[pl.BlockSpec((tb, DIFF_W), col(0)), pl.BlockSpec((tb, DIFF_W), col(1)),
                  pl.BlockSpec((tb, DIFF_W), col(2)), pl.BlockSpec((tb, FNET_W), col(3 * DIFF_W // FNET_W)),
                  pl.BlockSpec((4, HEAD_DIM), const), pl.BlockSpec((1, PAIR_W), const),
                  pl.BlockSpec((t, t), const), pl.BlockSpec((t, t), const),
                  pl.BlockSpec((FNET_W, FNET_W), const), pl.BlockSpec((FNET_W, FNET_W), const)],
        out_specs=[pl.BlockSpec((tb, DIFF_W), lambda b: (b, 0)), pl.BlockSpec((tb, FNET_W), lambda b: (b, 0))],
        out_shape=[jax.ShapeDtypeStruct((BATCH * t, DIFF_W), bf16),
                   jax.ShapeDtypeStruct((BATCH * t, FNET_W), bf16)],
        compiler_params=_params(1),
        name="ctx_cd",
    )(proj, proj, proj, proj, lam, subln, ct, st, bdc, bds)


def _lat_cd_kernel(q_ref, k_ref, v_ref, z_ref, ckt_ref, cv_ref, lam_ref, subln_ref, ct_ref, st_ref, bdc_ref, bds_ref,
                   a_ref, zf_ref, *, lam_init):
    lam = _lambda_full(lam_ref, lam_init)
    sl = lambda h: slice(h * PAIR_W, (h + 1) * PAIR_W)
    heads = _diff_heads(lambda h: q_ref[:, sl(h)],
                        lambda h: [(k_ref[:, sl(h)], False), (ckt_ref[0, h].astype(bf16), True)],
                        lambda h: [v_ref[:, sl(h)], cv_ref[0, h].astype(bf16)], lam, subln_ref[...], lam_init,
                        DIFF_GROUP_LAT)
    for h in range(H_DIFF):
        a_ref[:, sl(h)] = heads[h].astype(bf16)
    zf_ref[...] = _fourier_rows(ct_ref, st_ref, z_ref[...], bdc_ref, bds_ref).astype(bf16)


def _lat_cd(proj, ck, cv, lam, subln, ct, st, bdc, bds, lam_init):
    t = DEC_SEQ
    nq = t // TQ_CD
    const = lambda b, i: (0, 0)
    return pl.pallas_call(
        functools.partial(_lat_cd_kernel, lam_init=lam_init),
        grid=(DEC_BATCH, nq),
        in_specs=[pl.BlockSpec((TQ_CD, DIFF_W), lambda b, i: (N_CTX // TQ_CD + b * nq + i, 0)),
                  pl.BlockSpec((t, DIFF_W), lambda b, i: (N_CTX // t + b, 1)),
                  pl.BlockSpec((t, DIFF_W), lambda b, i: (N_CTX // t + b, 2)),
                  pl.BlockSpec((t, FNET_W), lambda b, i: (N_CTX // t + b, 3 * DIFF_W // FNET_W)),
                  pl.BlockSpec((1, H_DIFF, PAIR_W, PAST_LEN), lambda b, i: (b, 0, 0, 0)),
                  pl.BlockSpec((1, H_DIFF, PAST_LEN, PAIR_W), lambda b, i: (b, 0, 0, 0)),
                  pl.BlockSpec((4, HEAD_DIM), const), pl.BlockSpec((1, PAIR_W), const),
                  pl.BlockSpec((TQ_CD, t), lambda b, i: (i, 0)), pl.BlockSpec((TQ_CD, t), lambda b, i: (i, 0)),
                  pl.BlockSpec((FNET_W, FNET_W), const), pl.BlockSpec((FNET_W, FNET_W), const)],
        out_specs=[pl.BlockSpec((TQ_CD, DIFF_W), lambda b, i: (b * nq + i, 0)),
                   pl.BlockSpec((TQ_CD, FNET_W), lambda b, i: (b * nq + i, 0))],
        out_shape=[jax.ShapeDtypeStruct((DEC_BATCH * t, DIFF_W), bf16),
                   jax.ShapeDtypeStruct((DEC_BATCH * t, FNET_W), bf16)],
        compiler_params=_params(2),
        name="lat_cd",
    )(proj, proj, proj, proj, ck, cv, lam, subln, ct, st, bdc, bds)


def _rope_tables():
    t = np.arange(DEC_SEQ)
    quarter = HEAD_DIM // 4
    inv = ROPE_BASE ** (-np.arange(quarter, dtype=np.float64) / quarter)
    ang = np.concatenate([(t // GRID_W)[:, None] * inv, (t % GRID_W)[:, None] * inv], -1)
    cos, sin = np.cos(ang), np.sin(ang)
    reps = LANES // HEAD_DIM
    return (np.tile(np.concatenate([cos, cos], -1), (1, reps)).astype(np.float32),
            np.tile(np.concatenate([-sin, sin], -1), (1, reps)).astype(np.float32))


def _dft_tables(n):
    k = np.arange(n)
    ang = (2.0 * math.pi / n) * ((k[:, None] * k[None, :]) % n)
    return (np.cos(ang) / math.sqrt(n)).astype(np.float32), (np.sin(ang) / math.sqrt(n)).astype(np.float32)


def _block_diag(m, reps):
    return np.kron(np.eye(reps, dtype=m.dtype), m)


def kernel(x_prompt, x_sample, state_ret, cache_win_k, cache_win_v, cache_diff_k, cache_diff_v, c, c_ctx, w_mod, b_mod, ln_g, ln_b, w_in_ab, w_out_ab, ret_log_gamma, ret_gn_g, ret_gn_b, win_sink, w_in_cd, w_out_cd, diff_lambda, diff_subln_g, w_gate, w_up, w_down):
    cond = jnp.concatenate([c_ctx[None, :], c, jnp.zeros((SUBLANES - 1 - DEC_BATCH, D_MODEL), f32)], 0)
    mod = _modulation(cond, w_mod, b_mod).reshape(DEPTH, SUBLANES, 6, D_MODEL)

    rope_tabs = _rope_tables()
    gmat = jnp.asarray(_block_diag(np.full((HEAD_DIM, HEAD_DIM), 1.0 / HEAD_DIM, np.float32),
                                   N_CHUNK // HEAD_DIM), bf16)
    c64, s64 = _dft_tables(FNET_DIM)
    bdc = _block_diag(c64, FNET_GROUPS)
    bds = _block_diag(s64, FNET_GROUPS)
    dft_ctx = _dft_tables(SEQ)
    dft_lat = _dft_tables(DEC_SEQ)

    x_parts = [x_prompt.reshape(N_CTX, D_MODEL), x_sample.reshape(N_LAT, D_MODEL)]
    outs = {}
    for l in range(DEPTH):
        i = l // 2
        if l % 2 == 0:
            lgf = jnp.repeat(ret_log_gamma[i, 0], HEAD_DIM)[None, :]
            lgb = jnp.repeat(ret_log_gamma[i, 1], HEAD_DIM)[None, :]
            gn_g = ret_gn_g[i][None, :]
            gn_b = ret_gn_b[i][None, :]
            sink = win_sink[i][None, :]
            rope_tiles = tuple(range(0, 2 * RET_W // LANES)) + tuple(
                range(4 * RET_W // LANES, (4 * RET_W + WIN_W + KV_W) // LANES))
            kv_tile = (4 * RET_W + WIN_W) // LANES
            kv_shape = (BATCH, 1, KV_WIN, HEAD_DIM, SEQ)
            scale_tiles = tuple(range(4 * RET_W // LANES, (4 * RET_W + WIN_W) // LANES))
            proj, wk_t, wv_t = _proj(x_parts, mod, l, w_in_ab, i, scale_tiles, rope_tabs, rope_tiles,
                                     (kv_shape, kv_shape),
                                     {kv_tile: ("heads", 0, 0), kv_tile + 1: ("heads", 1, 0)})
            ro_c, wo_c, st_c = _ctx_ab(proj, ret_log_gamma[i], sink, lgf, lgb, gmat, gn_g, gn_b)
            ck = cache_win_k[:, i].reshape(DEC_BATCH, PAST_LEN, KV_W)
            cv = cache_win_v[:, i].reshape(DEC_BATCH, PAST_LEN, KV_W)
            ro_l, wo_l = _lat_ab(proj, ret_log_gamma[i], sink, ck, cv, state_ret, i, lgf, lgb, gmat, gn_g, gn_b)
            mix_a, mix_b, w_out = (ro_c, ro_l), (wo_c, wo_l), w_out_ab
            outs.setdefault('state', []).append(st_c[:, None])
            outs.setdefault('win_k', []).append(jnp.transpose(wk_t, (0, 1, 4, 2, 3)))
            outs.setdefault('win_v', []).append(jnp.transpose(wv_t, (0, 1, 4, 2, 3)))
        else:
            lam_init = 0.8 - 0.6 * math.exp(-0.3 * l)
            subln = diff_subln_g[i][None, :]
            rope_tiles = tuple(range(0, 2 * DIFF_W // LANES))
            plan = {}
            for h in range(H_DIFF):
                plan[DIFF_W // LANES + h] = ("pairs", 0, h)
                plan[2 * DIFF_W // LANES + h] = ("plain", 1, h)
            scale_tiles = tuple(range(0, DIFF_W // LANES))
            proj, dk_t, dv_h = _proj(
                x_parts, mod, l, w_in_cd, i, scale_tiles, rope_tabs, rope_tiles,
                ((BATCH, 1, H_DIFF, 2, HEAD_DIM, SEQ), (BATCH, 1, H_DIFF, SEQ, 2 * HEAD_DIM)), plan)
            a_c, z_c = _ctx_cd(proj, diff_lambda[i], subln, dft_ctx[0], dft_ctx[1], bdc, bds, lam_init)
            ck = jnp.transpose(cache_diff_k[:, i], (0, 2, 3, 4, 1)).reshape(DEC_BATCH, H_DIFF, PAIR_W, PAST_LEN)
            cv = jnp.transpose(cache_diff_v[:, i], (0, 2, 1, 3))
            a_l, z_l = _lat_cd(proj, ck, cv, diff_lambda[i], subln, dft_lat[0], dft_lat[1], bdc, bds, lam_init)
            mix_a, mix_b, w_out = (a_c, a_l), (z_c, z_l), w_out_cd
            outs.setdefault('diff_k', []).append(jnp.transpose(dk_t, (0, 1, 5, 2, 3, 4)))
            outs.setdefault('diff_v', []).append(jnp.transpose(dv_h, (0, 1, 3, 2, 4)))
        x_parts = _post(x_parts, mix_a, mix_b, mod, ln_g, ln_b, w_out, w_gate, w_up, w_down, l, i,
                        split_out=(l == DEPTH - 1))

    y_prompt = x_parts[0].reshape(BATCH, SEQ, D_MODEL)
    y_sample = x_parts[1].reshape(DEC_BATCH, DEC_SEQ, D_MODEL)
    cat = lambda parts: parts[0] if len(parts) == 1 else jnp.concatenate(parts, 1)
    return (y_prompt, y_sample, cat(outs['state']), cat(outs['win_k']), cat(outs['win_v']),
            cat(outs['diff_k']), cat(outs['diff_v']))
```

```python
import functools
import math

import jax
import jax.numpy as jnp
import numpy as np
from jax import lax
from jax.experimental import pallas as pl
from jax.experimental.pallas import tpu as pltpu

D_MODEL = 1024
BATCH = 32
SEQ = 256
DEPTH = 2
DEC_BATCH = 2
DEC_SEQ = 1024
PAST_LEN = 512
GRID_W = 64
HEAD_DIM = 64
ROPE_BASE = 10000.0
H_RET = 8
H_WIN = 8
KV_WIN = 2
G_WIN = H_WIN // KV_WIN
WINDOW = 128
H_DIFF = 6
FNET_GROUPS = 4
FNET_DIM = 64
D_FF = 256 * math.ceil(8 * D_MODEL / 3 / 256)
RET_W = H_RET * HEAD_DIM
WIN_W = H_WIN * HEAD_DIM
KV_W = KV_WIN * HEAD_DIM
AB_IN = 4 * RET_W + WIN_W + 2 * KV_W
DIFF_W = H_DIFF * 2 * HEAD_DIM
FNET_W = FNET_GROUPS * FNET_DIM
CD_IN = 3 * DIFF_W + FNET_W
ALPHA = (2 * DEPTH) ** 0.25
LN_EPS = 1e-5
QK_SCALE = HEAD_DIM ** -0.5
LOG2_E = math.log2(math.e)

N_CTX = BATCH * SEQ
N_LAT = DEC_BATCH * DEC_SEQ
N_TOK = N_CTX + N_LAT

LANES = 128
SUBLANES = 8
PAIR_W = 2 * HEAD_DIM
TM = 512
TM_PROJ = 1024
CTX_BLOCKS = N_CTX // TM
TOK_BLOCKS = N_TOK // TM
ROW_GROUPS = 2
FFN_SKEW = 2
TQ = 256
TQ_CD = 512
CTX_SEQS = 4
CTX_SEQS_CD = 4
RET_GROUP = 4
WIN_GROUP = 4
WIN_GROUP_LAT = 4
DIFF_GROUP_LAT = 1
DIFF_GROUP = 3
N_CHUNK = 256
MOD_TN = 1536
NEG_BIG = -1e30
VMEM_LIMIT = 56 * 1024 * 1024

f32 = jnp.float32
bf16 = jnp.bfloat16


def _params(n_axes):
    return pltpu.CompilerParams(dimension_semantics=("arbitrary",) * n_axes,
                                vmem_limit_bytes=VMEM_LIMIT)


def _dot(a, b):
    return jnp.dot(a, b, preferred_element_type=f32)


def _dot_nt(a, b):
    return lax.dot_general(a, b, (((1,), (1,)), ((), ())), preferred_element_type=f32)


def _ln(x):
    mu = jnp.mean(x, -1, keepdims=True)
    d = x - mu
    var = jnp.mean(d * d, -1, keepdims=True)
    return d * lax.rsqrt(var + LN_EPS)


def _silu(x):
    return x * jax.nn.sigmoid(x)


def _split_bf16(x):
    hi = x.astype(bf16)
    lo = (x - hi.astype(f32)).astype(bf16)
    return hi, lo


def _lane_half_mask(shape):
    return (lax.broadcasted_iota(jnp.int32, shape, len(shape) - 1) & HEAD_DIM) == 0


def _mod_kernel(c_ref, w_ref, b_ref, o_ref):
    layer = pl.program_id(0)
    a = _silu(c_ref[...])
    a_hi, a_lo = _split_bf16(a)
    w_hi, w_lo = _split_bf16(w_ref[0])
    acc = _dot(a_hi, w_hi) + _dot(a_lo, w_hi) + _dot(a_hi, w_lo)
    o_ref[0] = acc + b_ref[pl.ds(layer, 1), :]


def _modulation(cond, w_mod, b_mod):
    tn = MOD_TN
    rows = cond.shape[0]
    return pl.pallas_call(
        _mod_kernel,
        grid=(DEPTH, 6 * D_MODEL // tn),
        in_specs=[pl.BlockSpec((rows, D_MODEL), lambda l, j: (0, 0)),
                  pl.BlockSpec((1, D_MODEL, tn), lambda l, j: (l, 0, j)),
                  pl.BlockSpec((DEPTH, tn), lambda l, j: (0, j))],
        out_specs=pl.BlockSpec((1, rows, tn), lambda l, j: (l, 0, j)),
        out_shape=jax.ShapeDtypeStruct((DEPTH, rows, 6 * D_MODEL), f32),
        compiler_params=_params(2),
        name="modulation",
    )(cond, w_mod, b_mod)


def _tok(i, n_w):
    return jnp.maximum(i - n_w, 0)


def _ctx_blk(t, tm=TM):
    return jnp.minimum(t, N_CTX // tm - 1)


def _lat_blk(t, tm=TM):
    return jnp.maximum(t - N_CTX // tm, 0)


def _mod_row(t, tm=TM):
    return jnp.where(t < N_CTX // tm, 0, 1 + _lat_blk(t, tm) * tm // DEC_SEQ)


def _token_specs(parts, n_w, tm=TM):
    width = parts[0].shape[1]
    if len(parts) == 1:
        return [pl.BlockSpec((tm, width), lambda i: (_tok(i, n_w), 0))]
    return [pl.BlockSpec((tm, width), lambda i: (_ctx_blk(_tok(i, n_w), tm), 0)),
            pl.BlockSpec((tm, width), lambda i: (_lat_blk(_tok(i, n_w), tm), 0))]


def _pick(refs, is_ctx, rs):
    if len(refs) == 1:
        return refs[0][rs, :]
    if isinstance(is_ctx, bool):
        return refs[0 if is_ctx else 1][rs, :]
    return jnp.where(is_ctx, refs[0][rs, :], refs[1][rs, :])


def _rope_pair(y, cos, sin_signed):
    first_half = (lax.broadcasted_iota(jnp.int32, y.shape, 1) & (HEAD_DIM // 2)) == 0
    swapped = jnp.where(first_half, pltpu.roll(y, LANES - HEAD_DIM // 2, 1), pltpu.roll(y, HEAD_DIM // 2, 1))
    return y * cos + swapped * sin_signed


def _proj_kernel(*refs, n_x, n_cache, n_w, rope_tiles, scale_tiles, cache_plan):
    x_refs = refs[:n_x]
    mod_ref, w_ref, cos_ref, sin_ref, o_ref = refs[n_x:n_x + 5]
    cache_refs = refs[n_x + 5:n_x + 5 + n_cache]
    wbf_ref, u_ref = refs[n_x + 5 + n_cache:]
    i = pl.program_id(0)

    @pl.when(i == 0)
    def _():
        for c in range(n_w):
            wbf_ref[c] = w_ref[0, :, c * N_CHUNK:(c + 1) * N_CHUNK].astype(bf16)

    def tokens(is_ctx):
        x_ref = x_refs[0] if is_ctx else x_refs[-1]
        shift = mod_ref[0, 0:1, :]
        scale = mod_ref[0, 1:2, :]
        groups = [slice(b * SEQ, (b + 1) * SEQ) for b in range(TM_PROJ // SEQ)]
        for rs in groups:
            u_ref[rs, :] = (_ln(x_ref[rs, :]) * (1.0 + scale) + shift).astype(bf16)
        for c in range(n_w):
            y_all = _dot(u_ref[...], wbf_ref[c])
            for b, rs in enumerate(groups):
                y = y_all[rs, :]
                for t in range(N_CHUNK // LANES):
                    tile = c * (N_CHUNK // LANES) + t
                    piece = y[:, t * LANES:(t + 1) * LANES]
                    if tile in rope_tiles and not is_ctx:
                        piece = _rope_pair(piece, cos_ref[rs, :], sin_ref[rs, :])
                    if tile in scale_tiles:
                        piece = piece * (QK_SCALE * LOG2_E)
                    o_ref[rs, tile * LANES:(tile + 1) * LANES] = piece.astype(o_ref.dtype)
                    if tile in cache_plan and is_ctx:
                        kind, out_idx, slot = cache_plan[tile]
                        c_ref = cache_refs[out_idx]
                        if kind == "plain":
                            c_ref[b, 0, slot] = piece
                        else:
                            piece_t = piece.T
                            if kind == "heads":
                                c_ref[b, 0, 0] = piece_t[0:HEAD_DIM]
                                c_ref[b, 0, 1] = piece_t[HEAD_DIM:]
                            else:
                                c_ref[b, 0, slot, 0] = piece_t[0:HEAD_DIM]
                                c_ref[b, 0, slot, 1] = piece_t[HEAD_DIM:]

    t = i - 1

    @pl.when(jnp.logical_and(t >= 0, t < N_CTX // TM_PROJ))
    def _():
        tokens(True)

    @pl.when(t >= N_CTX // TM_PROJ)
    def _():
        tokens(False)


def _proj(x_parts, mod, mod_layer, w_all, layer, scale_tiles, rope_tabs, rope_tiles, cache_shapes, cache_plan):
    n_out = w_all.shape[2]
    n_w = n_out // N_CHUNK
    tm = TM_PROJ
    nb = DEC_SEQ // tm
    tok = lambda i: _tok(i, 1)
    in_specs = _token_specs(x_parts, 1, tm) + [
        pl.BlockSpec((None, 1, 6, D_MODEL), lambda i: (mod_layer, _mod_row(tok(i), tm), 0, 0)),
        pl.BlockSpec((1, D_MODEL, n_out), lambda i: (layer, 0, 0), pipeline_mode=pl.Buffered(1)),
        pl.BlockSpec((tm, LANES), lambda i: (_lat_blk(tok(i), tm) % nb, 0)),
        pl.BlockSpec((tm, LANES), lambda i: (_lat_blk(tok(i), tm) % nb, 0))]
    out_specs = [pl.BlockSpec((tm, n_out), lambda i: (tok(i), 0))]
    out_shape = [jax.ShapeDtypeStruct((N_TOK, n_out), bf16)]
    for shp in cache_shapes:
        blk = (tm // SEQ,) + tuple(shp[1:])
        out_specs.append(pl.BlockSpec(blk, lambda i, nd=len(shp): (_ctx_blk(tok(i), tm),) + (0,) * (nd - 1)))
        out_shape.append(jax.ShapeDtypeStruct(tuple(shp), f32))
    return pl.pallas_call(
        functools.partial(_proj_kernel, n_x=len(x_parts), n_cache=len(cache_shapes), n_w=n_w,
                          rope_tiles=frozenset(rope_tiles), scale_tiles=frozenset(scale_tiles),
                          cache_plan=dict(cache_plan)),
        grid=(1 + N_TOK // tm,),
        in_specs=in_specs,
        out_specs=out_specs,
        out_shape=out_shape,
        scratch_shapes=[pltpu.VMEM((n_w, D_MODEL, N_CHUNK), bf16), pltpu.VMEM((tm, D_MODEL), bf16)],
        compiler_params=_params(1),
        name="proj",
    )(*x_parts, mod, w_all, *rope_tabs)


def _post_kernel(*refs, n_x, n_y, ka, kb, n_w):
    x_refs = refs[:n_x]
    (ac_ref, al_ref, bc_ref, bl_ref, mod_ref, lng_ref, lnb_ref,
     wo_ref, wg_ref, wu_ref, wd_ref) = refs[n_x:n_x + 11]
    y_refs = refs[n_x + 11:n_x + 11 + n_y]
    wo_s, wg_s, wu_s, wd_s, x1_ref, u_ref, h_ref, y_ref = refs[n_x + 11 + n_y:]
    i = pl.program_id(0)
    lead = n_w - 1
    gate1 = mod_ref[0, 2:3, :]
    shift2 = mod_ref[0, 3:4, :]
    scale2 = mod_ref[0, 4:5, :]
    gate2 = mod_ref[0, 5:6, :]
    groups = [slice(r * TM // ROW_GROUPS, (r + 1) * TM // ROW_GROUPS) for r in range(ROW_GROUPS)]

    def mix_in(rs, is_ctx):
        a = _pick((ac_ref, al_ref), is_ctx, rs)
        b = _pick((bc_ref, bl_ref), is_ctx, rs)
        pieces = ([a[:, c:c + N_CHUNK] for c in range(0, ka, N_CHUNK)]
                  + [b[:, c:c + N_CHUNK] for c in range(0, kb, N_CHUNK)])
        h = functools.reduce(lambda s, p: s + p, [_dot(p, wo_s[c]) for c, p in enumerate(pieces)])
        x1 = _ln(ALPHA * _pick(x_refs, is_ctx, rs) + gate1 * h) * lng_ref[0, 0:1, :] + lnb_ref[0, 0:1, :]
        x1_ref[rs, :] = x1
        u_ref[rs, :] = (_ln(x1) * (1.0 + scale2) + shift2).astype(bf16)

    def finish(rs, ffn, is_ctx):
        y = _ln(ALPHA * x1_ref[rs, :] + gate2 * ffn) * lng_ref[0, 1:2, :] + lnb_ref[0, 1:2, :]
        if n_y == 1:
            y_refs[0][rs, :] = y
        elif isinstance(is_ctx, bool):
            y_refs[0 if is_ctx else 1][rs, :] = y
        else:
            y_ref[rs, :] = y

    def emit(is_ctx):
        if n_y == 2 and not isinstance(is_ctx, bool):
            @pl.when(is_ctx)
            def _():
                y_refs[0][...] = y_ref[...]

            @pl.when(jnp.logical_not(is_ctx))
            def _():
                y_refs[1][...] = y_ref[...]

    @pl.when(i < n_w)
    def _():
        wg_s[i] = wg_ref[0].astype(bf16)
        wu_s[i] = wu_ref[0].astype(bf16)
        wd_s[i] = wd_ref[0].astype(bf16)

        @pl.when(i == 0)
        def _():
            for c in range((ka + kb) // N_CHUNK):
                wo_s[c] = wo_ref[0, c * N_CHUNK:(c + 1) * N_CHUNK, :].astype(bf16)
            for rs in groups:
                mix_in(rs, True)
                y_ref[rs, :] = jnp.zeros((TM // ROW_GROUPS, D_MODEL), f32)

        for rs in groups:
            g = _dot(u_ref[rs, :], wg_s[i])
            up = _dot(u_ref[rs, :], wu_s[i])
            y_ref[rs, :] += _dot((_silu(g) * up).astype(bf16), wd_s[i])

        @pl.when(i == lead)
        def _():
            for rs in groups:
                finish(rs, y_ref[rs, :], True)

    @pl.when(i >= n_w)
    def _():
        is_ctx = (i - lead) < CTX_BLOCKS
        for rs in groups:
            mix_in(rs, is_ctx)

        def ffn_chunk(rs, c):
            g = _dot(u_ref[rs, :], wg_s[c])
            up = _dot(u_ref[rs, :], wu_s[c])
            h_ref[rs, c * N_CHUNK:(c + 1) * N_CHUNK] = (_silu(g) * up).astype(bf16)

        def ffn_down(rs):
            finish(rs, functools.reduce(
                lambda s, p: s + p,
                [_dot(h_ref[rs, c * N_CHUNK:(c + 1) * N_CHUNK], wd_s[c]) for c in range(n_w)]), is_ctx)

        for c in range(n_w + FFN_SKEW * (ROW_GROUPS - 1)):
            for r, rs in enumerate(groups):
                cc = c - FFN_SKEW * r
                if 0 <= cc < n_w:
                    ffn_chunk(rs, cc)
                if cc == n_w - 1:
                    ffn_down(rs)
        emit(is_ctx)


def _post(x_parts, mix_a, mix_b, mod, ln_g, ln_b, w_out, w_gate, w_up, w_down, layer, mix_layer, split_out):
    ka, kb = mix_a[0].shape[1], mix_b[0].shape[1]
    n_w = D_FF // N_CHUNK
    lead = n_w - 1
    tok = lambda i: _tok(i, lead)
    lay = lambda i: (layer, 0, 0)
    in_specs = (_token_specs(x_parts, lead) + _token_specs(mix_a, lead) + _token_specs(mix_b, lead) + [
        pl.BlockSpec((None, 1, 6, D_MODEL), lambda i: (layer, _mod_row(tok(i)), 0, 0)),
        pl.BlockSpec((1, 2, D_MODEL), lay),
        pl.BlockSpec((1, 2, D_MODEL), lay),
        pl.BlockSpec((1, ka + kb, D_MODEL), lambda i: (mix_layer, 0, 0), pipeline_mode=pl.Buffered(1)),
        pl.BlockSpec((1, D_MODEL, N_CHUNK), lambda i: (layer, 0, jnp.minimum(i, n_w - 1))),
        pl.BlockSpec((1, D_MODEL, N_CHUNK), lambda i: (layer, 0, jnp.minimum(i, n_w - 1))),
        pl.BlockSpec((1, N_CHUNK, D_MODEL), lambda i: (layer, jnp.minimum(i, n_w - 1), 0))])
    if split_out:
        out_specs = [pl.BlockSpec((TM, D_MODEL), lambda i: (_ctx_blk(tok(i)), 0)),
                     pl.BlockSpec((TM, D_MODEL), lambda i: (_lat_blk(tok(i)), 0))]
        out_shape = [jax.ShapeDtypeStruct((N_CTX, D_MODEL), f32), jax.ShapeDtypeStruct((N_LAT, D_MODEL), f32)]
    else:
        out_specs = [pl.BlockSpec((TM, D_MODEL), lambda i: (tok(i), 0))]
        out_shape = [jax.ShapeDtypeStruct((N_TOK, D_MODEL), f32)]
    return pl.pallas_call(
        functools.partial(_post_kernel, n_x=len(x_parts), n_y=len(out_shape), ka=ka, kb=kb, n_w=n_w),
        grid=(lead + TOK_BLOCKS,),
        in_specs=in_specs,
        out_specs=out_specs,
        out_shape=out_shape,
        scratch_shapes=[pltpu.VMEM(((ka + kb) // N_CHUNK, N_CHUNK, D_MODEL), bf16),
                        pltpu.VMEM((n_w, D_MODEL, N_CHUNK), bf16),
                        pltpu.VMEM((n_w, D_MODEL, N_CHUNK), bf16), pltpu.VMEM((n_w, N_CHUNK, D_MODEL), bf16),
                        pltpu.VMEM((TM, D_MODEL), f32), pltpu.VMEM((TM, D_MODEL), bf16),
                        pltpu.VMEM((TM, D_FF), bf16), pltpu.VMEM((TM, D_MODEL), f32)],
        compiler_params=_params(1),
        name="post",
    )(*x_parts, *mix_a, *mix_b, mod, ln_g, ln_b, w_out, w_gate, w_up, w_down)


def _group_norm_gate(ro, rg, gmat, gn_g, gn_b):
    def gmean(parts):
        cols = []
        for c in range(0, RET_W, N_CHUNK):
            cols.append(sum(_dot(p[:, c:c + N_CHUNK], gmat) for p in parts))
        return jnp.concatenate(cols, -1)

    d = ro - gmean(_split_bf16(ro))
    var = gmean([(d * d).astype(bf16)])
    y = d * lax.rsqrt(var + LN_EPS) * gn_g + gn_b
    return _silu(rg.astype(f32)) * y


def _dup_head(x, j):
    first = _lane_half_mask(x.shape)
    keep = first if j == 0 else jnp.logical_not(first)
    xm = jnp.where(keep, x.astype(f32), 0.0)
    return xm + pltpu.roll(xm, HEAD_DIM, 1)


def _softmax_parts(scores, sink):
    m = sink
    for s in scores:
        m = jnp.maximum(m, jnp.max(s, -1, keepdims=True))
    es = [jnp.exp2(s - m) for s in scores]
    denom = jnp.exp2(sink - m)
    for e in es:
        denom = denom + jnp.sum(e, -1, keepdims=True)
    return es, denom


def _retention_tables(lg_ref, lgf_ref, lgb_ref, dmask_ref, kdec_ref, n):
    row = lax.broadcasted_iota(jnp.int32, (n, n), 0)
    col = lax.broadcasted_iota(jnp.int32, (n, n), 1)
    diff = (row - col).astype(f32)
    diag = jnp.where(row == col, 2.0 * QK_SCALE, QK_SCALE)
    for h in range(H_RET):
        dmask_ref[h] = jnp.exp(jnp.where(diff >= 0, lg_ref[0, h] * diff, -lg_ref[1, h] * diff)) * diag
    t = lax.broadcasted_iota(jnp.int32, (n, RET_W), 0).astype(f32)
    kdec_ref[0] = jnp.exp(lgf_ref[...] * (n - 1.0 - t)) * QK_SCALE
    kdec_ref[1] = jnp.exp(lgb_ref[...] * t) * QK_SCALE


def _retention_intra(pairs, q_of, k_of, v_of, dmask_ref):
    first = _lane_half_mask(k_of(pairs[0]).shape)
    masked = {}
    for p in pairs:
        kb = k_of(p)
        for e in range(2):
            keep = first if e == 0 else jnp.logical_not(first)
            s = _dot_nt(q_of(p), jnp.where(keep, kb, jnp.zeros_like(kb))) * dmask_ref[2 * p + e]
            masked[p, e] = s.astype(bf16)
    outs = {}
    for p in pairs:
        pv = [_dot(masked[p, e], v_of(p)) for e in range(2)]
        outs[p] = jnp.where(_lane_half_mask(pv[0].shape), pv[0], pv[1])
    return outs


def _window_group(subs, q_of, k_parts_of, v_parts_of, masks, sink_of):
    scores = {}
    for key in subs:
        parts = [_dot_nt(q_of(key), k) for k in k_parts_of(key)]
        scores[key] = [sc if mk is None else jnp.where(mk, sc, NEG_BIG) for sc, mk in zip(parts, masks)]
    probs = {}
    for key in subs:
        es, denom = _softmax_parts(scores[key], sink_of(key))
        probs[key] = ([ex.astype(bf16) for ex in es], denom)
    outs = {}
    for key in subs:
        es, denom = probs[key]
        pv = functools.reduce(lambda x, y: x + y, [_dot(ex, v) for ex, v in zip(es, v_parts_of(key))])
        outs[key] = pv / denom
    return outs


def _ctx_ab_kernel(lg_ref, sink_ref, rq_ref, rk_ref, rv_ref, rg_ref, wq_ref, wk_ref, wv_ref,
                   lgf_ref, lgb_ref, gmat_ref, gng_ref, gnb_ref,
                   ro_ref, wo_ref, st_ref, dmask_ref, kdec_ref, ret_ref):
    t_len = SEQ

    @pl.when(pl.program_id(0) == 0)
    def _():
        _retention_tables(lg_ref, lgf_ref, lgb_ref, dmask_ref, kdec_ref, t_len)

    first = _lane_half_mask((t_len, PAIR_W))
    for sq in range(CTX_SEQS):
        rows = slice(sq * t_len, (sq + 1) * t_len)
        psl = lambda p: slice(p * PAIR_W, (p + 1) * PAIR_W)
        for p0 in range(0, H_RET // 2, RET_GROUP):
            pairs = list(range(p0, p0 + RET_GROUP))
            intra = _retention_intra(pairs, lambda p: rq_ref[rows, psl(p)], lambda p: rk_ref[rows, psl(p)],
                                     lambda p: rv_ref[rows, psl(p)], dmask_ref)
            for p in pairs:
                ret_ref[rows, psl(p)] = intra[p]
        for p in range(H_RET // 2):
            sl = psl(p)
            kb = rk_ref[rows, sl]
            v = rv_ref[rows, sl]
            for d in range(2):
                kd_t = (kb * kdec_ref[d, :, sl]).T.astype(bf16)
                st = _dot(kd_t, v)
                st_ref[sq, d, 2 * p] = st[0:HEAD_DIM, 0:HEAD_DIM]
                st_ref[sq, d, 2 * p + 1] = pltpu.roll(st[HEAD_DIM:, :], HEAD_DIM, 1)[:, 0:HEAD_DIM]
        ro_ref[rows, :] = _group_norm_gate(ret_ref[rows, :], rg_ref[rows, :], gmat_ref[...], gng_ref[...],
                                           gnb_ref[...]).astype(bf16)

        k_dup = [_dup_head(wk_ref[rows, :], j).astype(bf16) for j in range(KV_WIN)]
        v_dup = [_dup_head(wv_ref[rows, :], j).astype(bf16) for j in range(KV_WIN)]

        def q_masked(key):
            qp, e = key
            qb = wq_ref[rows, qp * PAIR_W:(qp + 1) * PAIR_W]
            return jnp.where(first if e == 0 else jnp.logical_not(first), qb, jnp.zeros_like(qb))

        kv_of = lambda key: key[0] * 2 // G_WIN
        for g0 in range(0, H_WIN // 2, WIN_GROUP):
            subs = [(qp, e) for qp in range(g0, g0 + WIN_GROUP) for e in range(2)]
            outs = _window_group(subs, q_masked, lambda key: [k_dup[kv_of(key)]], lambda key: [v_dup[kv_of(key)]],
                                 [None], lambda key: sink_ref[0, 2 * key[0] + key[1]] * LOG2_E)
            for qp in range(g0, g0 + WIN_GROUP):
                wo_ref[rows, qp * PAIR_W:(qp + 1) * PAIR_W] = jnp.where(first, outs[qp, 0], outs[qp, 1]).astype(bf16)


def _ctx_ab(proj, log_gamma, sink, lgf_lanes, lgb_lanes, gmat, gn_g, gn_b):
    t = SEQ
    tb = CTX_SEQS * t
    smem = pl.BlockSpec(memory_space=pltpu.SMEM)
    const = lambda b: (0, 0)
    col = lambda c: (lambda b: (b, c))
    return pl.pallas_call(
        _ctx_ab_kernel,
        grid=(BATCH // CTX_SEQS,),
        in_specs=[smem, smem,
                  pl.BlockSpec((tb, RET_W), col(0)), pl.BlockSpec((tb, RET_W), col(1)),
                  pl.BlockSpec((tb, RET_W), col(2)), pl.BlockSpec((tb, RET_W), col(3)),
                  pl.BlockSpec((tb, WIN_W), col(4)),
                  pl.BlockSpec((tb, KV_W), col((4 * RET_W + WIN_W) // KV_W)),
                  pl.BlockSpec((tb, KV_W), col((4 * RET_W + WIN_W) // KV_W + 1)),
                  pl.BlockSpec((1, RET_W), const), pl.BlockSpec((1, RET_W), const),
                  pl.BlockSpec((N_CHUNK, N_CHUNK), const),
                  pl.BlockSpec((1, RET_W), const), pl.BlockSpec((1, RET_W), const)],
        out_specs=[pl.BlockSpec((tb, RET_W), lambda b: (b, 0)),
                   pl.BlockSpec((tb, WIN_W), lambda b: (b, 0)),
                   pl.BlockSpec((CTX_SEQS, 2, H_RET, HEAD_DIM, HEAD_DIM), lambda b: (b, 0, 0, 0, 0))],
        out_shape=[jax.ShapeDtypeStruct((BATCH * t, RET_W), bf16),
                   jax.ShapeDtypeStruct((BATCH * t, WIN_W), bf16),
                   jax.ShapeDtypeStruct((BATCH, 2, H_RET, HEAD_DIM, HEAD_DIM), f32)],
        scratch_shapes=[pltpu.VMEM((H_RET, t, t), f32), pltpu.VMEM((2, t, RET_W), f32),
                        pltpu.VMEM((tb, RET_W), f32)],
        compiler_params=_params(1),
        name="ctx_ab",
    )(log_gamma, sink, proj, proj, proj, proj, proj, proj, proj, lgf_lanes, lgb_lanes, gmat, gn_g, gn_b)


def _pair_state(s0_ref, d, p):
    zero = jnp.zeros((HEAD_DIM, HEAD_DIM), f32)
    top = jnp.concatenate([s0_ref[0, 0, d, 2 * p], zero], 1)
    bottom = jnp.concatenate([zero, s0_ref[0, 0, d, 2 * p + 1]], 1)
    return jnp.concatenate([top, bottom], 0)


def _lat_ab_kernel(lg_ref, sink_ref, rq_ref, rk_ref, rv_ref, rg_ref, wq_ref, wk_ref, wv_ref, ck_ref, cv_ref,
                   s0_ref, lgf_ref, lgb_ref, gmat_ref, gng_ref, gnb_ref,
                   ro_ref, wo_ref, ret_ref, dmask_ref, kdec_ref, qdec_ref, sf_ref, sb_ref):
    t_len = DEC_SEQ
    n_chunks = t_len // TQ
    chunk = pl.program_id(1)
    q0 = pl.multiple_of(chunk * TQ, TQ)
    first = _lane_half_mask((TQ, PAIR_W))

    @pl.when(jnp.logical_and(pl.program_id(0) == 0, chunk == 0))
    def _():
        _retention_tables(lg_ref, lgf_ref, lgb_ref, dmask_ref, kdec_ref, TQ)
        t = lax.broadcasted_iota(jnp.int32, (TQ, RET_W), 0).astype(f32)
        qdec_ref[0] = jnp.exp(lgf_ref[...] * (t + 1.0))
        qdec_ref[1] = jnp.exp(lgb_ref[...] * (TQ - t))

    @pl.when(chunk == 0)
    def _():
        r = lax.broadcasted_iota(jnp.int32, (PAIR_W, PAIR_W), 0)
        c_ = lax.broadcasted_iota(jnp.int32, (PAIR_W, PAIR_W), 1)
        same_head = (r < HEAD_DIM) == (c_ < HEAD_DIM)
        for p in range(H_RET // 2):
            sl = slice(p * PAIR_W, (p + 1) * PAIR_W)
            kv = []
            for c in range(n_chunks):
                rows = slice(c * TQ, (c + 1) * TQ)
                kc = rk_ref[rows, sl]
                vc = rv_ref[rows, sl]
                kv.append([jnp.where(same_head, _dot((kc * kdec_ref[d, :, sl]).T.astype(bf16), vc), 0.0)
                           for d in range(2)])
            state = _pair_state(s0_ref, 0, p)
            for c in range(n_chunks):
                sf_ref[c, p] = state
                state = state * jnp.exp(lgf_ref[:, sl] * TQ) + kv[c][0]
            state = _pair_state(s0_ref, 1, p)
            for c in reversed(range(n_chunks)):
                sb_ref[c, p] = state
                state = state * jnp.exp(lgb_ref[:, sl] * TQ) + kv[c][1]

    psl = lambda p: slice(p * PAIR_W, (p + 1) * PAIR_W)
    intra = {}
    for p0 in range(0, H_RET // 2, RET_GROUP):
        intra.update(_retention_intra(list(range(p0, p0 + RET_GROUP)), lambda p: rq_ref[:, psl(p)],
                                      lambda p: rk_ref[pl.ds(q0, TQ), psl(p)],
                                      lambda p: rv_ref[pl.ds(q0, TQ), psl(p)], dmask_ref))
    for p in range(H_RET // 2):
        sl = psl(p)
        q = rq_ref[:, sl]
        o = intra[p]
        o = o + _dot(q, sf_ref[chunk, p].astype(bf16)) * qdec_ref[0, :, sl]
        o = o + _dot(q, sb_ref[chunk, p].astype(bf16)) * qdec_ref[1, :, sl]
        ret_ref[:, sl] = o
    ro_ref[...] = _group_norm_gate(ret_ref[...], rg_ref[...], gmat_ref[...], gng_ref[...], gnb_ref[...]).astype(bf16)

    band = TQ + 2 * WINDOW
    k_start = pl.multiple_of(jnp.clip(q0 - WINDOW, 0, t_len - band), LANES)
    qi = q0 + lax.broadcasted_iota(jnp.int32, (TQ, band), 0)
    kj = k_start + lax.broadcasted_iota(jnp.int32, (TQ, band), 1)
    in_band = jnp.abs(qi - kj) <= WINDOW
    k_parts = [[_dup_head(wk_ref[pl.ds(k_start, band), :], j).astype(bf16), _dup_head(ck_ref[0], j).astype(bf16)]
               for j in range(KV_WIN)]
    v_parts = [[_dup_head(wv_ref[pl.ds(k_start, band), :], j).astype(bf16), _dup_head(cv_ref[0], j).astype(bf16)]
               for j in range(KV_WIN)]

    def q_masked(key):
        qp, e = key
        qb = wq_ref[:, qp * PAIR_W:(qp + 1) * PAIR_W]
        return jnp.where(first if e == 0 else jnp.logical_not(first), qb, jnp.zeros_like(qb))

    kv_of = lambda key: key[0] * 2 // G_WIN
    for g0 in range(0, H_WIN // 2, WIN_GROUP_LAT):
        subs = [(qp, e) for qp in range(g0, g0 + WIN_GROUP_LAT) for e in range(2)]
        outs = _window_group(subs, q_masked, lambda key: k_parts[kv_of(key)], lambda key: v_parts[kv_of(key)],
                             [in_band, None], lambda key: sink_ref[0, 2 * key[0] + key[1]] * LOG2_E)
        for qp in range(g0, g0 + WIN_GROUP_LAT):
            wo_ref[:, qp * PAIR_W:(qp + 1) * PAIR_W] = jnp.where(first, outs[qp, 0], outs[qp, 1]).astype(bf16)


def _lat_ab(proj, log_gamma, sink, ck, cv, state, layer, lgf_lanes, lgb_lanes, gmat, gn_g, gn_b):
    t = DEC_SEQ
    nq = t // TQ
    smem = pl.BlockSpec(memory_space=pltpu.SMEM)
    const = lambda b, i: (0, 0)
    qcol = lambda c: (lambda b, i: (N_CTX // TQ + b * nq + i, c))
    bcol = lambda c: (lambda b, i: (N_CTX // t + b, c))
    kv_col = (4 * RET_W + WIN_W) // KV_W
    return pl.pallas_call(
        _lat_ab_kernel,
        grid=(DEC_BATCH, nq),
        in_specs=[smem, smem,
                  pl.BlockSpec((TQ, RET_W), qcol(0)), pl.BlockSpec((t, RET_W), bcol(1)),
                  pl.BlockSpec((t, RET_W), bcol(2)), pl.BlockSpec((TQ, RET_W), qcol(3)),
                  pl.BlockSpec((TQ, WIN_W), qcol(4)),
                  pl.BlockSpec((t, KV_W), bcol(kv_col)), pl.BlockSpec((t, KV_W), bcol(kv_col + 1)),
                  pl.BlockSpec((1, PAST_LEN, KV_W), lambda b, i: (b, 0, 0)),
                  pl.BlockSpec((1, PAST_LEN, KV_W), lambda b, i: (b, 0, 0)),
                  pl.BlockSpec((1, 1, 2, H_RET, HEAD_DIM, HEAD_DIM), lambda b, i: (b, layer, 0, 0, 0, 0)),
                  pl.BlockSpec((1, RET_W), const), pl.BlockSpec((1, RET_W), const),
                  pl.BlockSpec((N_CHUNK, N_CHUNK), const),
                  pl.BlockSpec((1, RET_W), const), pl.BlockSpec((1, RET_W), const)],
        out_specs=[pl.BlockSpec((TQ, RET_W), lambda b, i: (b * nq + i, 0)),
                   pl.BlockSpec((TQ, WIN_W), lambda b, i: (b * nq + i, 0))],
        out_shape=[jax.ShapeDtypeStruct((DEC_BATCH * t, RET_W), bf16),
                   jax.ShapeDtypeStruct((DEC_BATCH * t, WIN_W), bf16)],
        scratch_shapes=[pltpu.VMEM((TQ, RET_W), f32), pltpu.VMEM((H_RET, TQ, TQ), f32),
                        pltpu.VMEM((2, TQ, RET_W), f32), pltpu.VMEM((2, TQ, RET_W), f32),
                        pltpu.VMEM((t // TQ, H_RET // 2, PAIR_W, PAIR_W), f32),
                        pltpu.VMEM((t // TQ, H_RET // 2, PAIR_W, PAIR_W), f32)],
        compiler_params=_params(2),
        name="lat_ab",
    )(log_gamma, sink, proj, proj, proj, proj, proj, proj, proj, ck, cv, state,
      lgf_lanes, lgb_lanes, gmat, gn_g, gn_b)


def _lambda_full(lam_ref, lam_init):
    lam = lam_ref[...]
    a = jnp.sum(lam[0:1, :] * lam[1:2, :], -1, keepdims=True)
    b = jnp.sum(lam[2:3, :] * lam[3:4, :], -1, keepdims=True)
    return jnp.exp(a) - jnp.exp(b) + lam_init


def _diff_heads(q_of, k_parts_of, v_parts_of, lam, subln, lam_init, group):
    res = []
    for h0 in range(0, H_DIFF, group):
        res += _diff_head_group(range(h0, h0 + group), q_of, k_parts_of, v_parts_of, lam, subln, lam_init)
    return res


def _diff_head_group(heads, q_of, k_parts_of, v_parts_of, lam, subln, lam_init):
    subs = [(h, e) for h in heads for e in range(2)]
    scores = {}
    for h, e in subs:
        q = q_of(h)
        fm = _lane_half_mask(q.shape)
        q_sub = jnp.where(fm if e == 0 else jnp.logical_not(fm), q, jnp.zeros_like(q))
        scores[h, e] = [_dot(q_sub, k) if transposed else _dot_nt(q_sub, k) for k, transposed in k_parts_of(h)]
    probs = {}
    for key in subs:
        m = scores[key][0].max(-1, keepdims=True)
        for sc in scores[key][1:]:
            m = jnp.maximum(m, sc.max(-1, keepdims=True))
        es = [jnp.exp2(sc - m) for sc in scores[key]]
        denom = es[0].sum(-1, keepdims=True)
        for ex in es[1:]:
            denom = denom + ex.sum(-1, keepdims=True)
        probs[key] = ([ex.astype(bf16) for ex in es], denom)
    outs = {}
    for h, e in subs:
        es, denom = probs[h, e]
        pv = functools.reduce(lambda x, y: x + y, [_dot(ex, v) for v, ex in zip(v_parts_of(h), es)])
        outs[h, e] = pv / denom
    res = []
    for h in heads:
        a = outs[h, 0] - lam * outs[h, 1]
        res.append(a * lax.rsqrt(jnp.mean(a * a, -1, keepdims=True) + LN_EPS) * subln * (1.0 - lam_init))
    return res


def _fourier_rows(ct_ref, st_ref, z, bdc_ref, bds_ref):
    zc = _dot(z, bdc_ref[...].astype(bf16)).astype(bf16)
    zs = _dot(z, bds_ref[...].astype(bf16)).astype(bf16)
    return _dot(ct_ref[...].astype(bf16), zc) - _dot(st_ref[...].astype(bf16), zs)


def _ctx_cd_kernel(q_ref, k_ref, v_ref, z_ref, lam_ref, subln_ref, ct_ref, st_ref, bdc_ref, bds_ref,
                   a_ref, zf_ref, *, lam_init):
    lam = _lambda_full(lam_ref, lam_init)
    for sq in range(CTX_SEQS_CD):
        rows = slice(sq * SEQ, (sq + 1) * SEQ)
        sl = lambda h: slice(h * PAIR_W, (h + 1) * PAIR_W)
        heads = _diff_heads(lambda h: q_ref[rows, sl(h)], lambda h: [(k_ref[rows, sl(h)], False)],
                            lambda h: [v_ref[rows, sl(h)]], lam, subln_ref[...], lam_init, DIFF_GROUP)
        for h in range(H_DIFF):
            a_ref[rows, sl(h)] = heads[h].astype(bf16)
        zf_ref[rows, :] = _fourier_rows(ct_ref, st_ref, z_ref[rows, :], bdc_ref, bds_ref).astype(bf16)


def _ctx_cd(proj, lam, subln, ct, st, bdc, bds, lam_init):
    t = SEQ
    tb = CTX_SEQS_CD * t
    const = lambda b: (0, 0)
    col = lambda c: (lambda b: (b, c))
    return pl.pallas_call(
        functools.partial(_ctx_cd_kernel, lam_init=lam_init),
        grid=(BATCH // CTX_SEQS_CD,),
        in_specs=[pl.BlockSpec((tb, DIFF_W), col(0)), pl.BlockSpec((tb, DIFF_W), col(1)),
                  pl.BlockSpec((tb, DIFF_W), col(2)), pl.BlockSpec((tb, FNET_W), col(3 * DIFF_W // FNET_W)),
                  pl.BlockSpec((4, HEAD_DIM), const), pl.BlockSpec((1, PAIR_W), const),
                  pl.BlockSpec((t, t), const), pl.BlockSpec((t, t), const),
                  pl.BlockSpec((FNET_W, FNET_W), const), pl.BlockSpec((FNET_W, FNET_W), const)],
        out_specs=[pl.BlockSpec((tb, DIFF_W), lambda b: (b, 0)), pl.BlockSpec((tb, FNET_W), lambda b: (b, 0))],
        out_shape=[jax.ShapeDtypeStruct((BATCH * t, DIFF_W), bf16),
                   jax.ShapeDtypeStruct((BATCH * t, FNET_W), bf16)],
        compiler_params=_params(1),
        name="ctx_cd",
    )(proj, proj, proj, proj, lam, subln, ct, st, bdc, bds)


def _lat_cd_kernel(q_ref, k_ref, v_ref, z_ref, ckt_ref, cv_ref, lam_ref, subln_ref, ct_ref, st_ref, bdc_ref, bds_ref,
                   a_ref, zf_ref, *, lam_init):
    lam = _lambda_full(lam_ref, lam_init)
    sl = lambda h: slice(h * PAIR_W, (h + 1) * PAIR_W)
    heads = _diff_heads(lambda h: q_ref[:, sl(h)],
                        lambda h: [(k_ref[:, sl(h)], False), (ckt_ref[0, h].astype(bf16), True)],
                        lambda h: [v_ref[:, sl(h)], cv_ref[0, h].astype(bf16)], lam, subln_ref[...], lam_init,
                        DIFF_GROUP_LAT)
    for h in range(H_DIFF):
        a_ref[:, sl(h)] = heads[h].astype(bf16)
    zf_ref[...] = _fourier_rows(ct_ref, st_ref, z_ref[...], bdc_ref, bds_ref).astype(bf16)


def _lat_cd(proj, ck, cv, lam, subln, ct, st, bdc, bds, lam_init):
    t = DEC_SEQ
    nq = t // TQ_CD
    const = lambda b, i: (0, 0)
    return pl.pallas_call(
        functools.partial(_lat_cd_kernel, lam_init=lam_init),
        grid=(DEC_BATCH, nq),
        in_specs=[pl.BlockSpec((TQ_CD, DIFF_W), lambda b, i: (N_CTX // TQ_CD + b * nq + i, 0)),
                  pl.BlockSpec((t, DIFF_W), lambda b, i: (N_CTX // t + b, 1)),
                  pl.BlockSpec((t, DIFF_W), lambda b, i: (N_CTX // t + b, 2)),
                  pl.BlockSpec((t, FNET_W), lambda b, i: (N_CTX // t + b, 3 * DIFF_W // FNET_W)),
                  pl.BlockSpec((1, H_DIFF, PAIR_W, PAST_LEN), lambda b, i: (b, 0, 0, 0)),
                  pl.BlockSpec((1, H_DIFF, PAST_LEN, PAIR_W), lambda b, i: (b, 0, 0, 0)),
                  pl.BlockSpec((4, HEAD_DIM), const), pl.BlockSpec((1, PAIR_W), const),
                  pl.BlockSpec((TQ_CD, t), lambda b, i: (i, 0)), pl.BlockSpec((TQ_CD, t), lambda b, i: (i, 0)),
                  pl.BlockSpec((FNET_W, FNET_W), const), pl.BlockSpec((FNET_W, FNET_W), const)],
        out_specs=[pl.BlockSpec((TQ_CD, DIFF_W), lambda b, i: (b * nq + i, 0)),
                   pl.BlockSpec((TQ_CD, FNET_W), lambda b, i: (b * nq + i, 0))],
        out_shape=[jax.ShapeDtypeStruct((DEC_BATCH * t, DIFF_W), bf16),
                   jax.ShapeDtypeStruct((DEC_BATCH * t, FNET_W), bf16)],
        compiler_params=_params(2),
        name="lat_cd",
    )(proj, proj, proj, proj, ck, cv, lam, subln, ct, st, bdc, bds)


def _rope_tables():
    t = np.arange(DEC_SEQ)
    quarter = HEAD_DIM // 4
    inv = ROPE_BASE ** (-np.arange(quarter, dtype=np.float64) / quarter)
    ang = np.concatenate([(t // GRID_W)[:, None] * inv, (t % GRID_W)[:, None] * inv], -1)
    cos, sin = np.cos(ang), np.sin(ang)
    reps = LANES // HEAD_DIM
    return (np.tile(np.concatenate([cos, cos], -1), (1, reps)).astype(np.float32),
            np.tile(np.concatenate([-sin, sin], -1), (1, reps)).astype(np.float32))


def _dft_tables(n):
    k = np.arange(n)
    ang = (2.0 * math.pi / n) * ((k[:, None] * k[None, :]) % n)
    return (np.cos(ang) / math.sqrt(n)).astype(np.float32), (np.sin(ang) / math.sqrt(n)).astype(np.float32)


def _block_diag(m, reps):
    return np.kron(np.eye(reps, dtype=m.dtype), m)


def kernel(x_prompt, x_sample, state_ret, cache_win_k, cache_win_v, cache_diff_k, cache_diff_v, c, c_ctx, w_mod, b_mod, ln_g, ln_b, w_in_ab, w_out_ab, ret_log_gamma, ret_gn_g, ret_gn_b, win_sink, w_in_cd, w_out_cd, diff_lambda, diff_subln_g, w_gate, w_up, w_down):
    cond = jnp.concatenate([c_ctx[None, :], c, jnp.zeros((SUBLANES - 1 - DEC_BATCH, D_MODEL), f32)], 0)
    mod = _modulation(cond, w_mod, b_mod).reshape(DEPTH, SUBLANES, 6, D_MODEL)

    rope_tabs = _rope_tables()
    gmat = jnp.asarray(_block_diag(np.full((HEAD_DIM, HEAD_DIM), 1.0 / HEAD_DIM, np.float32),
                                   N_CHUNK // HEAD_DIM), bf16)
    c64, s64 = _dft_tables(FNET_DIM)
    bdc = _block_diag(c64, FNET_GROUPS)
    bds = _block_diag(s64, FNET_GROUPS)
    dft_ctx = _dft_tables(SEQ)
    dft_lat = _dft_tables(DEC_SEQ)

    x_parts = [x_prompt.reshape(N_CTX, D_MODEL), x_sample.reshape(N_LAT, D_MODEL)]
    outs = {}
    for l in range(DEPTH):
        i = l // 2
        if l % 2 == 0:
            lgf = jnp.repeat(ret_log_gamma[i, 0], HEAD_DIM)[None, :]
            lgb = jnp.repeat(ret_log_gamma[i, 1], HEAD_DIM)[None, :]
            gn_g = ret_gn_g[i][None, :]
            gn_b = ret_gn_b[i][None, :]
            sink = win_sink[i][None, :]
            rope_tiles = tuple(range(0, 2 * RET_W // LANES)) + tuple(
                range(4 * RET_W // LANES, (4 * RET_W + WIN_W + KV_W) // LANES))
            kv_tile = (4 * RET_W + WIN_W) // LANES
            kv_shape = (BATCH, 1, KV_WIN, HEAD_DIM, SEQ)
            scale_tiles = tuple(range(4 * RET_W // LANES, (4 * RET_W + WIN_W) // LANES))
            proj, wk_t, wv_t = _proj(x_parts, mod, l, w_in_ab, i, scale_tiles, rope_tabs, rope_tiles,
                                     (kv_shape, kv_shape),
                                     {kv_tile: ("heads", 0, 0), kv_tile + 1: ("heads", 1, 0)})
            ro_c, wo_c, st_c = _ctx_ab(proj, ret_log_gamma[i], sink, lgf, lgb, gmat, gn_g, gn_b)
            ck = cache_win_k[:, i].reshape(DEC_BATCH, PAST_LEN, KV_W)
            cv = cache_win_v[:, i].reshape(DEC_BATCH, PAST_LEN, KV_W)
            ro_l, wo_l = _lat_ab(proj, ret_log_gamma[i], sink, ck, cv, state_ret, i, lgf, lgb, gmat, gn_g, gn_b)
            mix_a, mix_b, w_out = (ro_c, ro_l), (wo_c, wo_l), w_out_ab
            outs.setdefault('state', []).append(st_c[:, None])
            outs.setdefault('win_k', []).append(jnp.transpose(wk_t, (0, 1, 4, 2, 3)))
            outs.setdefault('win_v', []).append(jnp.transpose(wv_t, (0, 1, 4, 2, 3)))
        else:
            lam_init = 0.8 - 0.6 * math.exp(-0.3 * l)
            subln = diff_subln_g[i][None, :]
            rope_tiles = tuple(range(0, 2 * DIFF_W // LANES))
            plan = {}
            for h in range(H_DIFF):
                plan[DIFF_W // LANES + h] = ("pairs", 0, h)
                plan[2 * DIFF_W // LANES + h] = ("plain", 1, h)
            scale_tiles = tuple(range(0, DIFF_W // LANES))
            proj, dk_t, dv_h = _proj(
                x_parts, mod, l, w_in_cd, i, scale_tiles, rope_tabs, rope_tiles,
                ((BATCH, 1, H_DIFF, 2, HEAD_DIM, SEQ), (BATCH, 1, H_DIFF, SEQ, 2 * HEAD_DIM)), plan)
            a_c, z_c = _ctx_cd(proj, diff_lambda[i], subln, dft_ctx[0], dft_ctx[1], bdc, bds, lam_init)
            ck = jnp.transpose(cache_diff_k[:, i], (0, 2, 3, 4, 1)).reshape(DEC_BATCH, H_DIFF, PAIR_W, PAST_LEN)
            cv = jnp.transpose(cache_diff_v[:, i], (0, 2, 1, 3))
            a_l, z_l = _lat_cd(proj, ck, cv, diff_lambda[i], subln, dft_lat[0], dft_lat[1], bdc, bds, lam_init)
            mix_a, mix_b, w_out = (a_c, a_l), (z_c, z_l), w_out_cd
            outs.setdefault('diff_k', []).append(jnp.transpose(dk_t, (0, 1, 5, 2, 3, 4)))
            outs.setdefault('diff_v', []).append(jnp.transpose(dv_h, (0, 1, 3, 2, 4)))
        x_parts = _post(x_parts, mix_a, mix_b, mod, ln_g, ln_b, w_out, w_gate, w_up, w_down, l, i,
                        split_out=(l == DEPTH - 1))

    y_prompt = x_parts[0].reshape(BATCH, SEQ, D_MODEL)
    y_sample = x_parts[1].reshape(DEC_BATCH, DEC_SEQ, D_MODEL)
    cat = lambda parts: parts[0] if len(parts) == 1 else jnp.concatenate(parts, 1)
    return (y_prompt, y_sample, cat(outs['state']), cat(outs['win_k']), cat(outs['win_v']),
            cat(outs['diff_k']), cat(outs['diff_v']))
```

```python
import functools
import math

import jax
import jax.numpy as jnp
import numpy as np
from jax import lax
from jax.experimental import pallas as pl
from jax.experimental.pallas import tpu as pltpu

D_MODEL = 1024
BATCH = 32
SEQ = 256
DEPTH = 2
DEC_BATCH = 2
DEC_SEQ = 1024
PAST_LEN = 512
GRID_W = 64
HEAD_DIM = 64
ROPE_BASE = 10000.0
H_RET = 8
H_WIN = 8
KV_WIN = 2
G_WIN = H_WIN // KV_WIN
WINDOW = 128
H_DIFF = 6
FNET_GROUPS = 4
FNET_DIM = 64
D_FF = 256 * math.ceil(8 * D_MODEL / 3 / 256)
RET_W = H_RET * HEAD_DIM
WIN_W = H_WIN * HEAD_DIM
KV_W = KV_WIN * HEAD_DIM
AB_IN = 4 * RET_W + WIN_W + 2 * KV_W
DIFF_W = H_DIFF * 2 * HEAD_DIM
FNET_W = FNET_GROUPS * FNET_DIM
CD_IN = 3 * DIFF_W + FNET_W
ALPHA = (2 * DEPTH) ** 0.25
LN_EPS = 1e-5
QK_SCALE = HEAD_DIM ** -0.5
LOG2_E = math.log2(math.e)

N_CTX = BATCH * SEQ
N_LAT = DEC_BATCH * DEC_SEQ
N_TOK = N_CTX + N_LAT

LANES = 128
SUBLANES = 8
PAIR_W = 2 * HEAD_DIM
TM = 512
TM_PROJ = 1024
CTX_BLOCKS = N_CTX // TM
TOK_BLOCKS = N_TOK // TM
ROW_GROUPS = 2
FFN_SKEW = 2
TQ = 256
TQ_CD = 512
CTX_SEQS = 4
CTX_SEQS_CD = 4
RET_GROUP = 4
WIN_GROUP = 4
WIN_GROUP_LAT = 4
DIFF_GROUP_LAT = 1
DIFF_GROUP = 3
N_CHUNK = 256
MOD_TN = 1536
NEG_BIG = -1e30
VMEM_LIMIT = 56 * 1024 * 1024

f32 = jnp.float32
bf16 = jnp.bfloat16


def _params(n_axes):
    return pltpu.CompilerParams(dimension_semantics=("arbitrary",) * n_axes,
                                vmem_limit_bytes=VMEM_LIMIT)


def _dot(a, b):
    return jnp.dot(a, b, preferred_element_type=f32)


def _dot_nt(a, b):
    return lax.dot_general(a, b, (((1,), (1,)), ((), ())), preferred_element_type=f32)


def _ln(x):
    mu = jnp.mean(x, -1, keepdims=True)
    d = x - mu
    var = jnp.mean(d * d, -1, keepdims=True)
    return d * lax.rsqrt(var + LN_EPS)


def _silu(x):
    return x * jax.nn.sigmoid(x)


def _split_bf16(x):
    hi = x.astype(bf16)
    lo = (x - hi.astype(f32)).astype(bf16)
    return hi, lo


def _lane_half_mask(shape):
    return (lax.broadcasted_iota(jnp.int32, shape, len(shape) - 1) & HEAD_DIM) == 0


def _mod_kernel(c_ref, w_ref, b_ref, o_ref):
    layer = pl.program_id(0)
    a = _silu(c_ref[...])
    a_hi, a_lo = _split_bf16(a)
    w_hi, w_lo = _split_bf16(w_ref[0])
    acc = _dot(a_hi, w_hi) + _dot(a_lo, w_hi) + _dot(a_hi, w_lo)
    o_ref[0] = acc + b_ref[pl.ds(layer, 1), :]


def _modulation(cond, w_mod, b_mod):
    tn = MOD_TN
    rows = cond.shape[0]
    return pl.pallas_call(
        _mod_kernel,
        grid=(DEPTH, 6 * D_MODEL // tn),
        in_specs=[pl.BlockSpec((rows, D_MODEL), lambda l, j: (0, 0)),
                  pl.BlockSpec((1, D_MODEL, tn), lambda l, j: (l, 0, j)),
                  pl.BlockSpec((DEPTH, tn), lambda l, j: (0, j))],
        out_specs=pl.BlockSpec((1, rows, tn), lambda l, j: (l, 0, j)),
        out_shape=jax.ShapeDtypeStruct((DEPTH, rows, 6 * D_MODEL), f32),
        compiler_params=_params(2),
        name="modulation",
    )(cond, w_mod, b_mod)


def _tok(i, n_w):
    return jnp.maximum(i - n_w, 0)


def _ctx_blk(t, tm=TM):
    return jnp.minimum(t, N_CTX // tm - 1)


def _lat_blk(t, tm=TM):
    return jnp.maximum(t - N_CTX // tm, 0)


def _mod_row(t, tm=TM):
    return jnp.where(t < N_CTX // tm, 0, 1 + _lat_blk(t, tm) * tm // DEC_SEQ)


def _token_specs(parts, n_w, tm=TM):
    width = parts[0].shape[1]
    if len(parts) == 1:
        return [pl.BlockSpec((tm, width), lambda i: (_tok(i, n_w), 0))]
    return [pl.BlockSpec((tm, width), lambda i: (_ctx_blk(_tok(i, n_w), tm), 0)),
            pl.BlockSpec((tm, width), lambda i: (_lat_blk(_tok(i, n_w), tm), 0))]


def _pick(refs, is_ctx, rs):
    return refs[0 if (len(refs) == 1 or is_ctx) else 1][rs, :]


def _rope_pair(y, cos, sin_signed):
    first_half = (lax.broadcasted_iota(jnp.int32, y.shape, 1) & (HEAD_DIM // 2)) == 0
    swapped = jnp.where(first_half, pltpu.roll(y, LANES - HEAD_DIM // 2, 1), pltpu.roll(y, HEAD_DIM // 2, 1))
    return y * cos + swapped * sin_signed


def _proj_kernel(*refs, n_x, n_cache, n_w, rope_tiles, scale_tiles, cache_plan):
    x_refs = refs[:n_x]
    mod_ref, w_ref, cos_ref, sin_ref, o_ref = refs[n_x:n_x + 5]
    cache_refs = refs[n_x + 5:n_x + 5 + n_cache]
    wbf_ref, u_ref = refs[n_x + 5 + n_cache:]
    i = pl.program_id(0)

    @pl.when(i == 0)
    def _():
        for c in range(n_w):
            wbf_ref[c] = w_ref[0, :, c * N_CHUNK:(c + 1) * N_CHUNK].astype(bf16)

    def tokens(is_ctx):
        x_ref = x_refs[0] if is_ctx else x_refs[-1]
        shift = mod_ref[0, 0:1, :]
        scale = mod_ref[0, 1:2, :]
        groups = [slice(b * SEQ, (b + 1) * SEQ) for b in range(TM_PROJ // SEQ)]
        for rs in groups:
            u_ref[rs, :] = (_ln(x_ref[rs, :]) * (1.0 + scale) + shift).astype(bf16)
        for c in range(n_w):
            y_all = _dot(u_ref[...], wbf_ref[c])
            for b, rs in enumerate(groups):
                y = y_all[rs, :]
                for t in range(N_CHUNK // LANES):
                    tile = c * (N_CHUNK // LANES) + t
                    piece = y[:, t * LANES:(t + 1) * LANES]
                    if tile in rope_tiles and not is_ctx:
                        piece = _rope_pair(piece, cos_ref[rs, :], sin_ref[rs, :])
                    if tile in scale_tiles:
                        piece = piece * (QK_SCALE * LOG2_E)
                    o_ref[rs, tile * LANES:(tile + 1) * LANES] = piece.astype(o_ref.dtype)
                    if tile in cache_plan and is_ctx:
                        kind, out_idx, slot = cache_plan[tile]
                        c_ref = cache_refs[out_idx]
                        if kind == "plain":
                            c_ref[b, 0, slot] = piece
                        else:
                            piece_t = piece.T
                            if kind == "heads":
                                c_ref[b, 0, 0] = piece_t[0:HEAD_DIM]
                                c_ref[b, 0, 1] = piece_t[HEAD_DIM:]
                            else:
                                c_ref[b, 0, slot, 0] = piece_t[0:HEAD_DIM]
                                c_ref[b, 0, slot, 1] = piece_t[HEAD_DIM:]

    t = i - 1

    @pl.when(jnp.logical_and(t >= 0, t < N_CTX // TM_PROJ))
    def _():
        tokens(True)

    @pl.when(t >= N_CTX // TM_PROJ)
    def _():
        tokens(False)


def _proj(x_parts, mod, mod_layer, w_all, layer, scale_tiles, rope_tabs, rope_tiles, cache_shapes, cache_plan):
    n_out = w_all.shape[2]
    n_w = n_out // N_CHUNK
    tm = TM_PROJ
    nb = DEC_SEQ // tm
    tok = lambda i: _tok(i, 1)
    in_specs = _token_specs(x_parts, 1, tm) + [
        pl.BlockSpec((None, 1, 6, D_MODEL), lambda i: (mod_layer, _mod_row(tok(i), tm), 0, 0)),
        pl.BlockSpec((1, D_MODEL, n_out), lambda i: (layer, 0, 0), pipeline_mode=pl.Buffered(1)),
        pl.BlockSpec((tm, LANES), lambda i: (_lat_blk(tok(i), tm) % nb, 0)),
        pl.BlockSpec((tm, LANES), lambda i: (_lat_blk(tok(i), tm) % nb, 0))]
    out_specs = [pl.BlockSpec((tm, n_out), lambda i: (tok(i), 0))]
    out_shape = [jax.ShapeDtypeStruct((N_TOK, n_out), bf16)]
    for shp in cache_shapes:
        blk = (tm // SEQ,) + tuple(shp[1:])
        out_specs.append(pl.BlockSpec(blk, lambda i, nd=len(shp): (_ctx_blk(tok(i), tm),) + (0,) * (nd - 1)))
        out_shape.append(jax.ShapeDtypeStruct(tuple(shp), f32))
    return pl.pallas_call(
        functools.partial(_proj_kernel, n_x=len(x_parts), n_cache=len(cache_shapes), n_w=n_w,
                          rope_tiles=frozenset(rope_tiles), scale_tiles=frozenset(scale_tiles),
                          cache_plan=dict(cache_plan)),
        grid=(1 + N_TOK // tm,),
        in_specs=in_specs,
        out_specs=out_specs,
        out_shape=out_shape,
        scratch_shapes=[pltpu.VMEM((n_w, D_MODEL, N_CHUNK), bf16), pltpu.VMEM((tm, D_MODEL), bf16)],
        compiler_params=_params(1),
        name="proj",
    )(*x_parts, mod, w_all, *rope_tabs)


def _post_kernel(*refs, n_x, n_y, ka, kb, n_w):
    x_refs = refs[:n_x]
    (ac_ref, al_ref, bc_ref, bl_ref, mod_ref, lng_ref, lnb_ref,
     wo_ref, wg_ref, wu_ref, wd_ref) = refs[n_x:n_x + 11]
    y_refs = refs[n_x + 11:n_x + 11 + n_y]
    wo_s, wg_s, wu_s, wd_s, x1_ref, u_ref, h_ref, y_ref = refs[n_x + 11 + n_y:]
    i = pl.program_id(0)
    lead = n_w - 1
    gate1 = mod_ref[0, 2:3, :]
    shift2 = mod_ref[0, 3:4, :]
    scale2 = mod_ref[0, 4:5, :]
    gate2 = mod_ref[0, 5:6, :]
    groups = [slice(r * TM // ROW_GROUPS, (r + 1) * TM // ROW_GROUPS) for r in range(ROW_GROUPS)]

    def mix_in(rs, is_ctx):
        a = _pick((ac_ref, al_ref), is_ctx, rs)
        b = _pick((bc_ref, bl_ref), is_ctx, rs)
        pieces = ([a[:, c:c + N_CHUNK] for c in range(0, ka, N_CHUNK)]
                  + [b[:, c:c + N_CHUNK] for c in range(0, kb, N_CHUNK)])
        h = functools.reduce(lambda s, p: s + p, [_dot(p, wo_s[c]) for c, p in enumerate(pieces)])
        x1 = _ln(ALPHA * _pick(x_refs, is_ctx, rs) + gate1 * h) * lng_ref[0, 0:1, :] + lnb_ref[0, 0:1, :]
        x1_ref[rs, :] = x1
        u_ref[rs, :] = (_ln(x1) * (1.0 + scale2) + shift2).astype(bf16)

    def finish(rs, ffn, is_ctx):
        y = _ln(ALPHA * x1_ref[rs, :] + gate2 * ffn) * lng_ref[0, 1:2, :] + lnb_ref[0, 1:2, :]
        y_refs[0 if (n_y == 1 or is_ctx) else 1][rs, :] = y

    @pl.when(i < n_w)
    def _():
        wg_s[i] = wg_ref[0].astype(bf16)
        wu_s[i] = wu_ref[0].astype(bf16)
        wd_s[i] = wd_ref[0].astype(bf16)

        @pl.when(i == 0)
        def _():
            for c in range((ka + kb) // N_CHUNK):
                wo_s[c] = wo_ref[0, c * N_CHUNK:(c + 1) * N_CHUNK, :].astype(bf16)
            for rs in groups:
                mix_in(rs, True)
                y_ref[rs, :] = jnp.zeros((TM // ROW_GROUPS, D_MODEL), f32)

        for rs in groups:
            g = _dot(u_ref[rs, :], wg_s[i])
            up = _dot(u_ref[rs, :], wu_s[i])
            y_ref[rs, :] += _dot((_silu(g) * up).astype(bf16), wd_s[i])

        @pl.when(i == lead)
        def _():
            for rs in groups:
                finish(rs, y_ref[rs, :], True)

    def token_block(is_ctx):
        for rs in groups:
            mix_in(rs, is_ctx)

        def ffn_chunk(rs, c):
            g = _dot(u_ref[rs, :], wg_s[c])
            up = _dot(u_ref[rs, :], wu_s[c])
            h_ref[rs, c * N_CHUNK:(c + 1) * N_CHUNK] = (_silu(g) * up).astype(bf16)

        def ffn_down(rs):
            finish(rs, functools.reduce(
                lambda s, p: s + p,
                [_dot(h_ref[rs, c * N_CHUNK:(c + 1) * N_CHUNK], wd_s[c]) for c in range(n_w)]), is_ctx)

        for c in range(n_w + FFN_SKEW * (ROW_GROUPS - 1)):
            for r, rs in enumerate(groups):
                cc = c - FFN_SKEW * r
                if 0 <= cc < n_w:
                    ffn_chunk(rs, cc)
                if cc == n_w - 1:
                    ffn_down(rs)

    blk = i - lead

    @pl.when(jnp.logical_and(i >= n_w, blk < CTX_BLOCKS))
    def _():
        token_block(True)

    @pl.when(blk >= CTX_BLOCKS)
    def _():
        token_block(False)


def _post(x_parts, mix_a, mix_b, mod, ln_g, ln_b, w_out, w_gate, w_up, w_down, layer, mix_layer, split_out):
    ka, kb = mix_a[0].shape[1], mix_b[0].shape[1]
    n_w = D_FF // N_CHUNK
    lead = n_w - 1
    tok = lambda i: _tok(i, lead)
    lay = lambda i: (layer, 0, 0)
    in_specs = (_token_specs(x_parts, lead) + _token_specs(mix_a, lead) + _token_specs(mix_b, lead) + [
        pl.BlockSpec((None, 1, 6, D_MODEL), lambda i: (layer, _mod_row(tok(i)), 0, 0)),
        pl.BlockSpec((1, 2, D_MODEL), lay),
        pl.BlockSpec((1, 2, D_MODEL), lay),
        pl.BlockSpec((1, ka + kb, D_MODEL), lambda i: (mix_layer, 0, 0), pipeline_mode=pl.Buffered(1)),
        pl.BlockSpec((1, D_MODEL, N_CHUNK), lambda i: (layer, 0, jnp.minimum(i, n_w - 1))),
        pl.BlockSpec((1, D_MODEL, N_CHUNK), lambda i: (layer, 0, jnp.minimum(i, n_w - 1))),
        pl.BlockSpec((1, N_CHUNK, D_MODEL), lambda i: (layer, jnp.minimum(i, n_w - 1), 0))])
    if split_out:
        out_specs = [pl.BlockSpec((TM, D_MODEL), lambda i: (_ctx_blk(tok(i)), 0)),
                     pl.BlockSpec((TM, D_MODEL), lambda i: (_lat_blk(tok(i)), 0))]
        out_shape = [jax.ShapeDtypeStruct((N_CTX, D_MODEL), f32), jax.ShapeDtypeStruct((N_LAT, D_MODEL), f32)]
    else:
        out_specs = [pl.BlockSpec((TM, D_MODEL), lambda i: (tok(i), 0))]
        out_shape = [jax.ShapeDtypeStruct((N_TOK, D_MODEL), f32)]
    return pl.pallas_call(
        functools.partial(_post_kernel, n_x=len(x_parts), n_y=len(out_shape), ka=ka, kb=kb, n_w=n_w),
        grid=(lead + TOK_BLOCKS,),
        in_specs=in_specs,
        out_specs=out_specs,
        out_shape=out_shape,
        scratch_shapes=[pltpu.VMEM(((ka + kb) // N_CHUNK, N_CHUNK, D_MODEL), bf16),
                        pltpu.VMEM((n_w, D_MODEL, N_CHUNK), bf16),
                        pltpu.VMEM((n_w, D_MODEL, N_CHUNK), bf16), pltpu.VMEM((n_w, N_CHUNK, D_MODEL), bf16),
                        pltpu.VMEM((TM, D_MODEL), f32), pltpu.VMEM((TM, D_MODEL), bf16),
                        pltpu.VMEM((TM, D_FF), bf16), pltpu.VMEM((TM, D_MODEL), f32)],
        compiler_params=_params(1),
        name="post",
    )(*x_parts, *mix_a, *mix_b, mod, ln_g, ln_b, w_out, w_gate, w_up, w_down)


def _group_norm_gate(ro, rg, gmat, gn_g, gn_b):
    def gmean(parts):
        cols = []
        for c in range(0, RET_W, N_CHUNK):
            cols.append(sum(_dot(p[:, c:c + N_CHUNK], gmat) for p in parts))
        return jnp.concatenate(cols, -1)

    d = ro - gmean(_split_bf16(ro))
    var = gmean([(d * d).astype(bf16)])
    y = d * lax.rsqrt(var + LN_EPS) * gn_g + gn_b
    return _silu(rg.astype(f32)) * y


def _dup_head(x, j):
    first = _lane_half_mask(x.shape)
    keep = first if j == 0 else jnp.logical_not(first)
    xm = jnp.where(keep, x.astype(f32), 0.0)
    return xm + pltpu.roll(xm, HEAD_DIM, 1)


def _softmax_parts(scores, sink):
    m = sink
    for s in scores:
        m = jnp.maximum(m, jnp.max(s, -1, keepdims=True))
    es = [jnp.exp2(s - m) for s in scores]
    denom = jnp.exp2(sink - m)
    for e in es:
        denom = denom + jnp.sum(e, -1, keepdims=True)
    return es, denom


def _retention_tables(lg_ref, lgf_ref, lgb_ref, dmask_ref, kdec_ref, n):
    row = lax.broadcasted_iota(jnp.int32, (n, n), 0)
    col = lax.broadcasted_iota(jnp.int32, (n, n), 1)
    diff = (row - col).astype(f32)
    diag = jnp.where(row == col, 2.0 * QK_SCALE, QK_SCALE)
    for h in range(H_RET):
        dmask_ref[h] = jnp.exp(jnp.where(diff >= 0, lg_ref[0, h] * diff, -lg_ref[1, h] * diff)) * diag
    t = lax.broadcasted_iota(jnp.int32, (n, RET_W), 0).astype(f32)
    kdec_ref[0] = jnp.exp(lgf_ref[...] * (n - 1.0 - t)) * QK_SCALE
    kdec_ref[1] = jnp.exp(lgb_ref[...] * t) * QK_SCALE


def _retention_intra(pairs, q_of, k_of, v_of, dmask_ref):
    first = _lane_half_mask(k_of(pairs[0]).shape)
    masked = {}
    for p in pairs:
        kb = k_of(p)
        for e in range(2):
            keep = first if e == 0 else jnp.logical_not(first)
            s = _dot_nt(q_of(p), jnp.where(keep, kb, jnp.zeros_like(kb))) * dmask_ref[2 * p + e]
            masked[p, e] = s.astype(bf16)
    outs = {}
    for p in pairs:
        pv = [_dot(masked[p, e], v_of(p)) for e in range(2)]
        outs[p] = jnp.where(_lane_half_mask(pv[0].shape), pv[0], pv[1])
    return outs


def _window_group(subs, q_of, k_parts_of, v_parts_of, masks, sink_of):
    scores = {}
    for key in subs:
        parts = [_dot_nt(q_of(key), k) for k in k_parts_of(key)]
        scores[key] = [sc if mk is None else jnp.where(mk, sc, NEG_BIG) for sc, mk in zip(parts, masks)]
    probs = {}
    for key in subs:
        es, denom = _softmax_parts(scores[key], sink_of(key))
        probs[key] = ([ex.astype(bf16) for ex in es], denom)
    outs = {}
    for key in subs:
        es, denom = probs[key]
        pv = functools.reduce(lambda x, y: x + y, [_dot(ex, v) for ex, v in zip(es, v_parts_of(key))])
        outs[key] = pv / denom
    return outs


def _ctx_ab_kernel(lg_ref, sink_ref, rq_ref, rk_ref, rv_ref, rg_ref, wq_ref, wk_ref, wv_ref,
                   lgf_ref, lgb_ref, gmat_ref, gng_ref, gnb_ref,
                   ro_ref, wo_ref, st_ref, dmask_ref, kdec_ref, ret_ref):
    t_len = SEQ

    @pl.when(pl.program_id(0) == 0)
    def _():
        _retention_tables(lg_ref, lgf_ref, lgb_ref, dmask_ref, kdec_ref, t_len)

    first = _lane_half_mask((t_len, PAIR_W))
    for sq in range(CTX_SEQS):
        rows = slice(sq * t_len, (sq + 1) * t_len)
        psl = lambda p: slice(p * PAIR_W, (p + 1) * PAIR_W)
        for p0 in range(0, H_RET // 2, RET_GROUP):
            pairs = list(range(p0, p0 + RET_GROUP))
            intra = _retention_intra(pairs, lambda p: rq_ref[rows, psl(p)], lambda p: rk_ref[rows, psl(p)],
                                     lambda p: rv_ref[rows, psl(p)], dmask_ref)
            for p in pairs:
                ret_ref[rows, psl(p)] = intra[p]
        for p in range(H_RET // 2):
            sl = psl(p)
            kb = rk_ref[rows, sl]
            v = rv_ref[rows, sl]
            for d in range(2):
                kd_t = (kb * kdec_ref[d, :, sl]).T.astype(bf16)
                st = _dot(kd_t, v)
                st_ref[sq, d, 2 * p] = st[0:HEAD_DIM, 0:HEAD_DIM]
                st_ref[sq, d, 2 * p + 1] = pltpu.roll(st[HEAD_DIM:, :], HEAD_DIM, 1)[:, 0:HEAD_DIM]
        ro_ref[rows, :] = _group_norm_gate(ret_ref[rows, :], rg_ref[rows, :], gmat_ref[...], gng_ref[...],
                                           gnb_ref[...]).astype(bf16)

        k_dup = [_dup_head(wk_ref[rows, :], j).astype(bf16) for j in range(KV_WIN)]
        v_dup = [_dup_head(wv_ref[rows, :], j).astype(bf16) for j in range(KV_WIN)]

        def q_masked(key):
            qp, e = key
            qb = wq_ref[rows, qp * PAIR_W:(qp + 1) * PAIR_W]
            return jnp.where(first if e == 0 else jnp.logical_not(first), qb, jnp.zeros_like(qb))

        kv_of = lambda key: key[0] * 2 // G_WIN
        for g0 in range(0, H_WIN // 2, WIN_GROUP):
            subs = [(qp, e) for qp in range(g0, g0 + WIN_GROUP) for e in range(2)]
            outs = _window_group(subs, q_masked, lambda key: [k_dup[kv_of(key)]], lambda key: [v_dup[kv_of(key)]],
                                 [None], lambda key: sink_ref[0, 2 * key[0] + key[1]] * LOG2_E)
            for qp in range(g0, g0 + WIN_GROUP):
                wo_ref[rows, qp * PAIR_W:(qp + 1) * PAIR_W] = jnp.where(first, outs[qp, 0], outs[qp, 1]).astype(bf16)


def _ctx_ab(proj, log_gamma, sink, lgf_lanes, lgb_lanes, gmat, gn_g, gn_b):
    t = SEQ
    tb = CTX_SEQS * t
    smem = pl.BlockSpec(memory_space=pltpu.SMEM)
    const = lambda b: (0, 0)
    col = lambda c: (lambda b: (b, c))
    return pl.pallas_call(
        _ctx_ab_kernel,
        grid=(BATCH // CTX_SEQS,),
        in_specs=[smem, smem,
                  pl.BlockSpec((tb, RET_W), col(0)), pl.BlockSpec((tb, RET_W), col(1)),
                  pl.BlockSpec((tb, RET_W), col(2)), pl.BlockSpec((tb, RET_W), col(3)),
                  pl.BlockSpec((tb, WIN_W), col(4)),
                  pl.BlockSpec((tb, KV_W), col((4 * RET_W + WIN_W) // KV_W)),
                  pl.BlockSpec((tb, KV_W), col((4 * RET_W + WIN_W) // KV_W + 1)),
                  pl.BlockSpec((1, RET_W), const), pl.BlockSpec((1, RET_W), const),
                  pl.BlockSpec((N_CHUNK, N_CHUNK), const),
                  pl.BlockSpec((1, RET_W), const), pl.BlockSpec((1, RET_W), const)],
        out_specs=[pl.BlockSpec((tb, RET_W), lambda b: (b, 0)),
                   pl.BlockSpec((tb, WIN_W), lambda b: (b, 0)),
                   pl.BlockSpec((CTX_SEQS, 2, H_RET, HEAD_DIM, HEAD_DIM), lambda b: (b, 0, 0, 0, 0))],
        out_shape=[jax.ShapeDtypeStruct((BATCH * t, RET_W), bf16),
                   jax.ShapeDtypeStruct((BATCH * t, WIN_W), bf16),
                   jax.ShapeDtypeStruct((BATCH, 2, H_RET, HEAD_DIM, HEAD_DIM), f32)],
        scratch_shapes=[pltpu.VMEM((H_RET, t, t), f32), pltpu.VMEM((2, t, RET_W), f32),
                        pltpu.VMEM((tb, RET_W), f32)],
        compiler_params=_params(1),
        name="ctx_ab",
    )(log_gamma, sink, proj, proj, proj, proj, proj, proj, proj, lgf_lanes, lgb_lanes, gmat, gn_g, gn_b)


def _pair_state(s0_ref, d, p):
    zero = jnp.zeros((HEAD_DIM, HEAD_DIM), f32)
    top = jnp.concatenate([s0_ref[0, 0, d, 2 * p], zero], 1)
    bottom = jnp.concatenate([zero, s0_ref[0, 0, d, 2 * p + 1]], 1)
    return jnp.concatenate([top, bottom], 0)


def _lat_ab_kernel(lg_ref, sink_ref, rq_ref, rk_ref, rv_ref, rg_ref, wq_ref, wk_ref, wv_ref, ck_ref, cv_ref,
                   s0_ref, lgf_ref, lgb_ref, gmat_ref, gng_ref, gnb_ref,
                   ro_ref, wo_ref, ret_ref, dmask_ref, kdec_ref, qdec_ref, sf_ref, sb_ref):
    t_len = DEC_SEQ
    n_chunks = t_len // TQ
    chunk = pl.program_id(1)
    q0 = pl.multiple_of(chunk * TQ, TQ)
    first = _lane_half_mask((TQ, PAIR_W))

    @pl.when(jnp.logical_and(pl.program_id(0) == 0, chunk == 0))
    def _():
        _retention_tables(lg_ref, lgf_ref, lgb_ref, dmask_ref, kdec_ref, TQ)
        t = lax.broadcasted_iota(jnp.int32, (TQ, RET_W), 0).astype(f32)
        qdec_ref[0] = jnp.exp(lgf_ref[...] * (t + 1.0))
        qdec_ref[1] = jnp.exp(lgb_ref[...] * (TQ - t))

    @pl.when(chunk == 0)
    def _():
        r = lax.broadcasted_iota(jnp.int32, (PAIR_W, PAIR_W), 0)
        c_ = lax.broadcasted_iota(jnp.int32, (PAIR_W, PAIR_W), 1)
        same_head = (r < HEAD_DIM) == (c_ < HEAD_DIM)
        for p in range(H_RET // 2):
            sl = slice(p * PAIR_W, (p + 1) * PAIR_W)
            kv = []
            for c in range(n_chunks):
                rows = slice(c * TQ, (c + 1) * TQ)
                kc = rk_ref[rows, sl]
                vc = rv_ref[rows, sl]
                kv.append([jnp.where(same_head, _dot((kc * kdec_ref[d, :, sl]).T.astype(bf16), vc), 0.0)
                           for d in range(2)])
            state = _pair_state(s0_ref, 0, p)
            for c in range(n_chunks):
                sf_ref[c, p] = state
                state = state * jnp.exp(lgf_ref[:, sl] * TQ) + kv[c][0]
            state = _pair_state(s0_ref, 1, p)
            for c in reversed(range(n_chunks)):
                sb_ref[c, p] = state
                state = state * jnp.exp(lgb_ref[:, sl] * TQ) + kv[c][1]

    psl = lambda p: slice(p * PAIR_W, (p + 1) * PAIR_W)
    intra = {}
    for p0 in range(0, H_RET // 2, RET_GROUP):
        intra.update(_retention_intra(list(range(p0, p0 + RET_GROUP)), lambda p: rq_ref[:, psl(p)],
                                      lambda p: rk_ref[pl.ds(q0, TQ), psl(p)],
                                      lambda p: rv_ref[pl.ds(q0, TQ), psl(p)], dmask_ref))
    for p in range(H_RET // 2):
        sl = psl(p)
        q = rq_ref[:, sl]
        o = intra[p]
        o = o + _dot(q, sf_ref[chunk, p].astype(bf16)) * qdec_ref[0, :, sl]
        o = o + _dot(q, sb_ref[chunk, p].astype(bf16)) * qdec_ref[1, :, sl]
        ret_ref[:, sl] = o
    ro_ref[...] = _group_norm_gate(ret_ref[...], rg_ref[...], gmat_ref[...], gng_ref[...], gnb_ref[...]).astype(bf16)

    band = TQ + 2 * WINDOW
    k_start = pl.multiple_of(jnp.clip(q0 - WINDOW, 0, t_len - band), LANES)
    qi = q0 + lax.broadcasted_iota(jnp.int32, (TQ, band), 0)
    kj = k_start + lax.broadcasted_iota(jnp.int32, (TQ, band), 1)
    in_band = jnp.abs(qi - kj) <= WINDOW
    k_parts = [[_dup_head(wk_ref[pl.ds(k_start, band), :], j).astype(bf16), _dup_head(ck_ref[0], j).astype(bf16)]
               for j in range(KV_WIN)]
    v_parts = [[_dup_head(wv_ref[pl.ds(k_start, band), :], j).astype(bf16), _dup_head(cv_ref[0], j).astype(bf16)]
               for j in range(KV_WIN)]

    def q_masked(key):
        qp, e = key
        qb = wq_ref[:, qp * PAIR_W:(qp + 1) * PAIR_W]
        return jnp.where(first if e == 0 else jnp.logical_not(first), qb, jnp.zeros_like(qb))

    kv_of = lambda key: key[0] * 2 // G_WIN
    for g0 in range(0, H_WIN // 2, WIN_GROUP_LAT):
        subs = [(qp, e) for qp in range(g0, g0 + WIN_GROUP_LAT) for e in range(2)]
        outs = _window_group(subs, q_masked, lambda key: k_parts[kv_of(key)], lambda key: v_parts[kv_of(key)],
                             [in_band, None], lambda key: sink_ref[0, 2 * key[0] + key[1]] * LOG2_E)
        for qp in range(g0, g0 + WIN_GROUP_LAT):
            wo_ref[:, qp * PAIR_W:(qp + 1) * PAIR_W] = jnp.where(first, outs[qp, 0], outs[qp, 1]).astype(bf16)


def _lat_ab(proj, log_gamma, sink, ck, cv, state, layer, lgf_lanes, lgb_lanes, gmat, gn_g, gn_b):
    t = DEC_SEQ
    nq = t // TQ
    smem = pl.BlockSpec(memory_space=pltpu.SMEM)
    const = lambda b, i: (0, 0)
    qcol = lambda c: (lambda b, i: (N_CTX // TQ + b * nq + i, c))
    bcol = lambda c: (lambda b, i: (N_CTX // t + b, c))
    kv_col = (4 * RET_W + WIN_W) // KV_W
    return pl.pallas_call(
        _lat_ab_kernel,
        grid=(DEC_BATCH, nq),
        in_specs=[smem, smem,
                  pl.BlockSpec((TQ, RET_W), qcol(0)), pl.BlockSpec((t, RET_W), bcol(1)),
                  pl.BlockSpec((t, RET_W), bcol(2)), pl.BlockSpec((TQ, RET_W), qcol(3)),
                  pl.BlockSpec((TQ, WIN_W), qcol(4)),
                  pl.BlockSpec((t, KV_W), bcol(kv_col)), pl.BlockSpec((t, KV_W), bcol(kv_col + 1)),
                  pl.BlockSpec((1, PAST_LEN, KV_W), lambda b, i: (b, 0, 0)),
                  pl.BlockSpec((1, PAST_LEN, KV_W), lambda b, i: (b, 0, 0)),
                  pl.BlockSpec((1, 1, 2, H_RET, HEAD_DIM, HEAD_DIM), lambda b, i: (b, layer, 0, 0, 0, 0)),
                  pl.BlockSpec((1, RET_W), const), pl.BlockSpec((1, RET_W), const),
                  pl.BlockSpec((N_CHUNK, N_CHUNK), const),
                  pl.BlockSpec((1, RET_W), const), pl.BlockSpec((1, RET_W), const)],
        out_specs=[pl.BlockSpec((TQ, RET_W), lambda b, i: (b * nq + i, 0)),
                   pl.BlockSpec((TQ, WIN_W), lambda b, i: (b * nq + i, 0))],
        out_shape=[jax.ShapeDtypeStruct((DEC_BATCH * t, RET_W), bf16),
                   jax.ShapeDtypeStruct((DEC_BATCH * t, WIN_W), bf16)],
        scratch_shapes=[pltpu.VMEM((TQ, RET_W), f32), pltpu.VMEM((H_RET, TQ, TQ), f32),
                        pltpu.VMEM((2, TQ, RET_W), f32), pltpu.VMEM((2, TQ, RET_W), f32),
                        pltpu.VMEM((t // TQ, H_RET // 2, PAIR_W, PAIR_W), f32),
                        pltpu.VMEM((t // TQ, H_RET // 2, PAIR_W, PAIR_W), f32)],
        compiler_params=_params(2),
        name="lat_ab",
    )(log_gamma, sink, proj, proj, proj, proj, proj, proj, proj, ck, cv, state,
      lgf_lanes, lgb_lanes, gmat, gn_g, gn_b)


def _lambda_full(lam_ref, lam_init):
    lam = lam_ref[...]
    a = jnp.sum(lam[0:1, :] * lam[1:2, :], -1, keepdims=True)
    b = jnp.sum(lam[2:3, :] * lam[3:4, :], -1, keepdims=True)
    return jnp.exp(a) - jnp.exp(b) + lam_init


def _diff_heads(q_of, k_parts_of, v_parts_of, lam, subln, lam_init, group):
    res = []
    for h0 in range(0, H_DIFF, group):
        res += _diff_head_group(range(h0, h0 + group), q_of, k_parts_of, v_parts_of, lam, subln, lam_init)
    return res


def _diff_head_group(heads, q_of, k_parts_of, v_parts_of, lam, subln, lam_init):
    subs = [(h, e) for h in heads for e in range(2)]
    scores = {}
    for h, e in subs:
        q = q_of(h)
        fm = _lane_half_mask(q.shape)
        q_sub = jnp.where(fm if e == 0 else jnp.logical_not(fm), q, jnp.zeros_like(q))
        scores[h, e] = [_dot(q_sub, k) if transposed else _dot_nt(q_sub, k) for k, transposed in k_parts_of(h)]
    probs = {}
    for key in subs:
        m = scores[key][0].max(-1, keepdims=True)
        for sc in scores[key][1:]:
            m = jnp.maximum(m, sc.max(-1, keepdims=True))
        es = [jnp.exp2(sc - m) for sc in scores[key]]
        denom = es[0].sum(-1, keepdims=True)
        for ex in es[1:]:
            denom = denom + ex.sum(-1, keepdims=True)
        probs[key] = ([ex.astype(bf16) for ex in es], denom)
    outs = {}
    for h, e in subs:
        es, denom = probs[h, e]
        pv = functools.reduce(lambda x, y: x + y, [_dot(ex, v) for v, ex in zip(v_parts_of(h), es)])
        outs[h, e] = pv / denom
    res = []
    for h in heads:
        a = outs[h, 0] - lam * outs[h, 1]
        res.append(a * lax.rsqrt(jnp.mean(a * a, -1, keepdims=True) + LN_EPS) * subln * (1.0 - lam_init))
    return res


def _fourier_rows(ct_ref, st_ref, z, bdc_ref, bds_ref):
    zc = _dot(z, bdc_ref[...].astype(bf16)).astype(bf16)
    zs = _dot(z, bds_ref[...].astype(bf16)).astype(bf16)
    return _dot(ct_ref[...].astype(bf16), zc) - _dot(st_ref[...].astype(bf16), zs)


def _ctx_cd_kernel(q_ref, k_ref, v_ref, z_ref, lam_ref, subln_ref, ct_ref, st_ref, bdc_ref, bds_ref,
                   a_ref, zf_ref, *, lam_init):
    lam = _lambda_full(lam_ref, lam_init)
    for sq in range(CTX_SEQS_CD):
        rows = slice(sq * SEQ, (sq + 1) * SEQ)
        sl = lambda h: slice(h * PAIR_W, (h + 1) * PAIR_W)
        heads = _diff_heads(lambda h: q_ref[rows, sl(h)], lambda h: [(k_ref[rows, sl(h)], False)],
                            lambda h: [v_ref[rows, sl(h)]], lam, subln_ref[...], lam_init, DIFF_GROUP)
        for h in range(H_DIFF):
            a_ref[rows, sl(h)] = heads[h].astype(bf16)
        zf_ref[rows, :] = _fourier_rows(ct_ref, st_ref, z_ref[rows, :], bdc_ref, bds_ref).astype(bf16)


def _ctx_cd(proj, lam, subln, ct, st, bdc, bds, lam_init):
    t = SEQ
    tb = CTX_SEQS_CD * t
    const = lambda b: (0, 0)
    col = lambda c: (lambda b: (b, c))
    return pl.pallas_call(
        functools.partial(_ctx_cd_kernel, lam_init=lam_init),
        grid=(BATCH // CTX_SEQS_CD,),
        in_specs=[pl.BlockSpec((tb, DIFF_W), col(0)), pl.BlockSpec((tb, DIFF_W), col(1)),
                  pl.BlockSpec((tb, DIFF_W), col(2)), pl.BlockSpec((tb, FNET_W), col(3 * DIFF_W // FNET_W)),
                  pl.BlockSpec((4, HEAD_DIM), const), pl.BlockSpec((1, PAIR_W), const),
                  pl.BlockSpec((t, t), const), pl.BlockSpec((t, t), const),
                  pl.BlockSpec((FNET_W, FNET_W), const), pl.BlockSpec((FNET_W, FNET_W), const)],
        out_specs=[pl.BlockSpec((tb, DIFF_W), lambda b: (b, 0)), pl.BlockSpec((tb, FNET_W), lambda b: (b, 0))],
        out_shape=[jax.ShapeDtypeStruct((BATCH * t, DIFF_W), bf16),
                   jax.ShapeDtypeStruct((BATCH * t, FNET_W), bf16)],
        compiler_params=_params(1),
        name="ctx_cd",
    )(proj, proj, proj, proj, lam, subln, ct, st, bdc, bds)


def _lat_cd_kernel(q_ref, k_ref, v_ref, z_ref, ckt_ref, cv_ref, lam_ref, subln_ref, ct_ref, st_ref, bdc_ref, bds_ref,
                   a_ref, zf_ref, *, lam_init):
    lam = _lambda_full(lam_ref, lam_init)
    sl = lambda h: slice(h * PAIR_W, (h + 1) * PAIR_W)
    heads = _diff_heads(lambda h: q_ref[:, sl(h)],
                        lambda h: [(k_ref[:, sl(h)], False), (ckt_ref[0, h].astype(bf16), True)],
                        lambda h: [v_ref[:, sl(h)], cv_ref[0, h].astype(bf16)], lam, subln_ref[...], lam_init,
                        DIFF_GROUP_LAT)
    for h in range(H_DIFF):
        a_ref[:, sl(h)] = heads[h].astype(bf16)
    zf_ref[...] = _fourier_rows(ct_ref, st_ref, z_ref[...], bdc_ref, bds_ref).astype(bf16)


def _lat_cd(proj, ck, cv, lam, subln, ct, st, bdc, bds, lam_init):
    t = DEC_SEQ
    nq = t // TQ_CD
    const = lambda b, i: (0, 0)
    return pl.pallas_call(
        functools.partial(_lat_cd_kernel, lam_init=lam_init),
        grid=(DEC_BATCH, nq),
        in_specs=[pl.BlockSpec((TQ_CD, DIFF_W), lambda b, i: (N_CTX // TQ_CD + b * nq + i, 0)),
                  pl.BlockSpec((t, DIFF_W), lambda b, i: (N_CTX // t + b, 1)),
                  pl.BlockSpec((t, DIFF_W), lambda b, i: (N_CTX // t + b, 2)),
                  pl.BlockSpec((t, FNET_W), lambda b, i: (N_CTX // t + b, 3 * DIFF_W // FNET_W)),
                  pl.BlockSpec((1, H_DIFF, PAIR_W, PAST_LEN), lambda b, i: (b, 0, 0, 0)),
                  pl.BlockSpec((1, H_DIFF, PAST_LEN, PAIR_W), lambda b, i: (b, 0, 0, 0)),
                  pl.BlockSpec((4, HEAD_DIM), const), pl.BlockSpec((1, PAIR_W), const),
                  pl.BlockSpec((TQ_CD, t), lambda b, i: (i, 0)), pl.BlockSpec((TQ_CD, t), lambda b, i: (i, 0)),
                  pl.BlockSpec((FNET_W, FNET_W), const), pl.BlockSpec((FNET_W, FNET_W), const)],
        out_specs=[pl.BlockSpec((TQ_CD, DIFF_W), lambda b, i: (b * nq + i, 0)),
                   pl.BlockSpec((TQ_CD, FNET_W), lambda b, i: (b * nq + i, 0))],
        out_shape=[jax.ShapeDtypeStruct((DEC_BATCH * t, DIFF_W), bf16),
                   jax.ShapeDtypeStruct((DEC_BATCH * t, FNET_W), bf16)],
        compiler_params=_params(2),
        name="lat_cd",
    )(proj, proj, proj, proj, ck, cv, lam, subln, ct, st, bdc, bds)


def _rope_tables():
    t = np.arange(DEC_SEQ)
    quarter = HEAD_DIM // 4
    inv = ROPE_BASE ** (-np.arange(quarter, dtype=np.float64) / quarter)
    ang = np.concatenate([(t // GRID_W)[:, None] * inv, (t % GRID_W)[:, None] * inv], -1)
    cos, sin = np.cos(ang), np.sin(ang)
    reps = LANES // HEAD_DIM
    return (np.tile(np.concatenate([cos, cos], -1), (1, reps)).astype(np.float32),
            np.tile(np.concatenate([-sin, sin], -1), (1, reps)).astype(np.float32))


def _dft_tables(n):
    k = np.arange(n)
    ang = (2.0 * math.pi / n) * ((k[:, None] * k[None, :]) % n)
    return (np.cos(ang) / math.sqrt(n)).astype(np.float32), (np.sin(ang) / math.sqrt(n)).astype(np.float32)


def _block_diag(m, reps):
    return np.kron(np.eye(reps, dtype=m.dtype), m)


def kernel(x_prompt, x_sample, state_ret, cache_win_k, cache_win_v, cache_diff_k, cache_diff_v, c, c_ctx, w_mod, b_mod, ln_g, ln_b, w_in_ab, w_out_ab, ret_log_gamma, ret_gn_g, ret_gn_b, win_sink, w_in_cd, w_out_cd, diff_lambda, diff_subln_g, w_gate, w_up, w_down):
    cond = jnp.concatenate([c_ctx[None, :], c, jnp.zeros((SUBLANES - 1 - DEC_BATCH, D_MODEL), f32)], 0)
    mod = _modulation(cond, w_mod, b_mod).reshape(DEPTH, SUBLANES, 6, D_MODEL)

    rope_tabs = _rope_tables()
    gmat = jnp.asarray(_block_diag(np.full((HEAD_DIM, HEAD_DIM), 1.0 / HEAD_DIM, np.float32),
                                   N_CHUNK // HEAD_DIM), bf16)
    c64, s64 = _dft_tables(FNET_DIM)
    bdc = _block_diag(c64, FNET_GROUPS)
    bds = _block_diag(s64, FNET_GROUPS)
    dft_ctx = _dft_tables(SEQ)
    dft_lat = _dft_tables(DEC_SEQ)

    x_parts = [x_prompt.reshape(N_CTX, D_MODEL), x_sample.reshape(N_LAT, D_MODEL)]
    outs = {}
    for l in range(DEPTH):
        i = l // 2
        if l % 2 == 0:
            lgf = jnp.repeat(ret_log_gamma[i, 0], HEAD_DIM)[None, :]
            lgb = jnp.repeat(ret_log_gamma[i, 1], HEAD_DIM)[None, :]
            gn_g = ret_gn_g[i][None, :]
            gn_b = ret_gn_b[i][None, :]
            sink = win_sink[i][None, :]
            rope_tiles = tuple(range(0, 2 * RET_W // LANES)) + tuple(
                range(4 * RET_W // LANES, (4 * RET_W + WIN_W + KV_W) // LANES))
            kv_tile = (4 * RET_W + WIN_W) // LANES
            kv_shape = (BATCH, 1, KV_WIN, HEAD_DIM, SEQ)
            scale_tiles = tuple(range(4 * RET_W // LANES, (4 * RET_W + WIN_W) // LANES))
            proj, wk_t, wv_t = _proj(x_parts, mod, l, w_in_ab, i, scale_tiles, rope_tabs, rope_tiles,
                                     (kv_shape, kv_shape),
                                     {kv_tile: ("heads", 0, 0), kv_tile + 1: ("heads", 1, 0)})
            ro_c, wo_c, st_c = _ctx_ab(proj, ret_log_gamma[i], sink, lgf, lgb, gmat, gn_g, gn_b)
            ck = cache_win_k[:, i].reshape(DEC_BATCH, PAST_LEN, KV_W)
            cv = cache_win_v[:, i].reshape(DEC_BATCH, PAST_LEN, KV_W)
            ro_l, wo_l = _lat_ab(proj, ret_log_gamma[i], sink, ck, cv, state_ret, i, lgf, lgb, gmat, gn_g, gn_b)
            mix_a, mix_b, w_out = (ro_c, ro_l), (wo_c, wo_l), w_out_ab
            outs.setdefault('state', []).append(st_c[:, None])
            outs.setdefault('win_k', []).append(jnp.transpose(wk_t, (0, 1, 4, 2, 3)))
            outs.setdefault('win_v', []).append(jnp.transpose(wv_t, (0, 1, 4, 2, 3)))
        else:
            lam_init = 0.8 - 0.6 * math.exp(-0.3 * l)
            subln = diff_subln_g[i][None, :]
            rope_tiles = tuple(range(0, 2 * DIFF_W // LANES))
            plan = {}
            for h in range(H_DIFF):
                plan[DIFF_W // LANES + h] = ("pairs", 0, h)
                plan[2 * DIFF_W // LANES + h] = ("plain", 1, h)
            scale_tiles = tuple(range(0, DIFF_W // LANES))
            proj, dk_t, dv_h = _proj(
                x_parts, mod, l, w_in_cd, i, scale_tiles, rope_tabs, rope_tiles,
                ((BATCH, 1, H_DIFF, 2, HEAD_DIM, SEQ), (BATCH, 1, H_DIFF, SEQ, 2 * HEAD_DIM)), plan)
            a_c, z_c = _ctx_cd(proj, diff_lambda[i], subln, dft_ctx[0], dft_ctx[1], bdc, bds, lam_init)
            ck = jnp.transpose(cache_diff_k[:, i], (0, 2, 3, 4, 1)).reshape(DEC_BATCH, H_DIFF, PAIR_W, PAST_LEN)
            cv = jnp.transpose(cache_diff_v[:, i], (0, 2, 1, 3))
            a_l, z_l = _lat_cd(proj, ck, cv, diff_lambda[i], subln, dft_lat[0], dft_lat[1], bdc, bds, lam_init)
            mix_a, mix_b, w_out = (a_c, a_l), (z_c, z_l), w_out_cd
            outs.setdefault('diff_k', []).append(jnp.transpose(dk_t, (0, 1, 5, 2, 3, 4)))
            outs.setdefault('diff_v', []).append(jnp.transpose(dv_h, (0, 1, 3, 2, 4)))
        x_parts = _post(x_parts, mix_a, mix_b, mod, ln_g, ln_b, w_out, w_gate, w_up, w_down, l, i,
                        split_out=(l == DEPTH - 1))

    y_prompt = x_parts[0].reshape(BATCH, SEQ, D_MODEL)
    y_sample = x_parts[1].reshape(DEC_BATCH, DEC_SEQ, D_MODEL)
    cat = lambda parts: parts[0] if len(parts) == 1 else jnp.concatenate(parts, 1)
    return (y_prompt, y_sample, cat(outs['state']), cat(outs['win_k']), cat(outs['win_v']),
            cat(outs['diff_k']), cat(outs['diff_v']))
```

```python
import functools
import math

import jax
import jax.numpy as jnp
import numpy as np
from jax import lax
from jax.experimental import pallas as pl
from jax.experimental.pallas import tpu as pltpu

D_MODEL = 1024
BATCH = 32
SEQ = 256
DEPTH = 2
DEC_BATCH = 2
DEC_SEQ = 1024
PAST_LEN = 512
GRID_W = 64
HEAD_DIM = 64
ROPE_BASE = 10000.0
H_RET = 8
H_WIN = 8
KV_WIN = 2
G_WIN = H_WIN // KV_WIN
WINDOW = 128
H_DIFF = 6
FNET_GROUPS = 4
FNET_DIM = 64
D_FF = 256 * math.ceil(8 * D_MODEL / 3 / 256)
RET_W = H_RET * HEAD_DIM
WIN_W = H_WIN * HEAD_DIM
KV_W = KV_WIN * HEAD_DIM
AB_IN = 4 * RET_W + WIN_W + 2 * KV_W
DIFF_W = H_DIFF * 2 * HEAD_DIM
FNET_W = FNET_GROUPS * FNET_DIM
CD_IN = 3 * DIFF_W + FNET_W
ALPHA = (2 * DEPTH) ** 0.25
LN_EPS = 1e-5
QK_SCALE = HEAD_DIM ** -0.5
LOG2_E = math.log2(math.e)

N_CTX = BATCH * SEQ
N_LAT = DEC_BATCH * DEC_SEQ
N_TOK = N_CTX + N_LAT

LANES = 128
SUBLANES = 8
PAIR_W = 2 * HEAD_DIM
TM = 512
TM_PROJ = 1024
CTX_BLOCKS = N_CTX // TM
TOK_BLOCKS = N_TOK // TM
ROW_GROUPS = 2
FFN_SKEW = 2
TQ = 256
LAT_SUB = 2
TQ_CD = 512
CTX_SEQS = 4
CTX_SEQS_CD = 4
RET_GROUP = 4
WIN_GROUP = 4
WIN_GROUP_LAT = 4
DIFF_GROUP_LAT = 1
DIFF_GROUP = 3
N_CHUNK = 256
MOD_TN = 1536
NEG_BIG = -1e30
VMEM_LIMIT = 56 * 1024 * 1024

f32 = jnp.float32
bf16 = jnp.bfloat16


def _params(n_axes):
    return pltpu.CompilerParams(dimension_semantics=("arbitrary",) * n_axes,
                                vmem_limit_bytes=VMEM_LIMIT)


def _dot(a, b):
    return jnp.dot(a, b, preferred_element_type=f32)


def _dot_nt(a, b):
    return lax.dot_general(a, b, (((1,), (1,)), ((), ())), preferred_element_type=f32)


def _ln(x):
    mu = jnp.mean(x, -1, keepdims=True)
    d = x - mu
    var = jnp.mean(d * d, -1, keepdims=True)
    return d * lax.rsqrt(var + LN_EPS)


def _silu(x):
    return x * jax.nn.sigmoid(x)


def _split_bf16(x):
    hi = x.astype(bf16)
    lo = (x - hi.astype(f32)).astype(bf16)
    return hi, lo


def _lane_half_mask(shape):
    return (lax.broadcasted_iota(jnp.int32, shape, len(shape) - 1) & HEAD_DIM) == 0


def _mod_kernel(c_ref, w_ref, b_ref, o_ref):
    layer = pl.program_id(0)
    a = _silu(c_ref[...])
    rows = a.shape[0]
    a_hi, a_lo = _split_bf16(a)
    w_hi, w_lo = _split_bf16(w_ref[0])
    both = _dot(jnp.concatenate([a_hi, a_lo], 0), w_hi)
    o_ref[0] = both[:rows] + both[rows:] + _dot(a_hi, w_lo) + b_ref[pl.ds(layer, 1), :]


def _modulation(cond, w_mod, b_mod):
    tn = MOD_TN
    rows = cond.shape[0]
    return pl.pallas_call(
        _mod_kernel,
        grid=(DEPTH, 6 * D_MODEL // tn),
        in_specs=[pl.BlockSpec((rows, D_MODEL), lambda l, j: (0, 0)),
                  pl.BlockSpec((1, D_MODEL, tn), lambda l, j: (l, 0, j)),
                  pl.BlockSpec((DEPTH, tn), lambda l, j: (0, j))],
        out_specs=pl.BlockSpec((1, rows, tn), lambda l, j: (l, 0, j)),
        out_shape=jax.ShapeDtypeStruct((DEPTH, rows, 6 * D_MODEL), f32),
        compiler_params=_params(2),
        name="modulation",
    )(cond, w_mod, b_mod)


def _tok(i, n_w):
    return jnp.maximum(i - n_w, 0)


def _ctx_blk(t, tm=TM):
    return jnp.minimum(t, N_CTX // tm - 1)


def _lat_blk(t, tm=TM):
    return jnp.maximum(t - N_CTX // tm, 0)


def _mod_row(t, tm=TM):
    return jnp.where(t < N_CTX // tm, 0, 1 + _lat_blk(t, tm) * tm // DEC_SEQ)


def _token_specs(parts, n_w, tm=TM):
    width = parts[0].shape[1]
    if len(parts) == 1:
        return [pl.BlockSpec((tm, width), lambda i: (_tok(i, n_w), 0))]
    return [pl.BlockSpec((tm, width), lambda i: (_ctx_blk(_tok(i, n_w), tm), 0)),
            pl.BlockSpec((tm, width), lambda i: (_lat_blk(_tok(i, n_w), tm), 0))]


def _pick(refs, is_ctx, rs):
    return refs[0 if (len(refs) == 1 or is_ctx) else 1][rs, :]


def _rope_pair(y, cos, sin_signed):
    first_half = (lax.broadcasted_iota(jnp.int32, y.shape, 1) & (HEAD_DIM // 2)) == 0
    swapped = jnp.where(first_half, pltpu.roll(y, LANES - HEAD_DIM // 2, 1), pltpu.roll(y, HEAD_DIM // 2, 1))
    return y * cos + swapped * sin_signed


def _proj_kernel(*refs, n_x, n_cache, n_w, rope_tiles, scale_tiles, cache_plan):
    x_refs = refs[:n_x]
    mod_ref, w_ref, cos_ref, sin_ref, o_ref = refs[n_x:n_x + 5]
    cache_refs = refs[n_x + 5:n_x + 5 + n_cache]
    wbf_ref, u_ref = refs[n_x + 5 + n_cache:]
    i = pl.program_id(0)

    @pl.when(i == 0)
    def _():
        for c in range(n_w):
            wbf_ref[c] = w_ref[0, :, c * N_CHUNK:(c + 1) * N_CHUNK].astype(bf16)

    def tokens(is_ctx):
        x_ref = x_refs[0] if is_ctx else x_refs[-1]
        shift = mod_ref[0, 0:1, :]
        scale = mod_ref[0, 1:2, :]
        groups = [slice(b * SEQ, (b + 1) * SEQ) for b in range(TM_PROJ // SEQ)]
        for rs in groups:
            u_ref[rs, :] = (_ln(x_ref[rs, :]) * (1.0 + scale) + shift).astype(bf16)
        for c in range(n_w):
            y_all = _dot(u_ref[...], wbf_ref[c])
            for b, rs in enumerate(groups):
                y = y_all[rs, :]
                for t in range(N_CHUNK // LANES):
                    tile = c * (N_CHUNK // LANES) + t
                    piece = y[:, t * LANES:(t + 1) * LANES]
                    if tile in rope_tiles and not is_ctx:
                        piece = _rope_pair(piece, cos_ref[rs, :], sin_ref[rs, :])
                    if tile in scale_tiles:
                        piece = piece * (QK_SCALE * LOG2_E)
                    o_ref[rs, tile * LANES:(tile + 1) * LANES] = piece.astype(o_ref.dtype)
                    if tile in cache_plan and is_ctx:
                        kind, out_idx, slot = cache_plan[tile]
                        c_ref = cache_refs[out_idx]
                        if kind == "plain":
                            c_ref[b, 0, slot] = piece
                        else:
                            piece_t = piece.T
                            if kind == "heads":
                                c_ref[b, 0, 0] = piece_t[0:HEAD_DIM]
                                c_ref[b, 0, 1] = piece_t[HEAD_DIM:]
                            else:
                                c_ref[b, 0, slot, 0] = piece_t[0:HEAD_DIM]
                                c_ref[b, 0, slot, 1] = piece_t[HEAD_DIM:]

    t = i - 1

    @pl.when(jnp.logical_and(t >= 0, t < N_CTX // TM_PROJ))
    def _():
        tokens(True)

    @pl.when(t >= N_CTX // TM_PROJ)
    def _():
        tokens(False)


def _proj(x_parts, mod, mod_layer, w_all, layer, scale_tiles, rope_tabs, rope_tiles, cache_shapes, cache_plan):
    n_out = w_all.shape[2]
    n_w = n_out // N_CHUNK
    tm = TM_PROJ
    nb = DEC_SEQ // tm
    tok = lambda i: _tok(i, 1)
    in_specs = _token_specs(x_parts, 1, tm) + [
        pl.BlockSpec((None, 1, 6, D_MODEL), lambda i: (mod_layer, _mod_row(tok(i), tm), 0, 0)),
        pl.BlockSpec((1, D_MODEL, n_out), lambda i: (layer, 0, 0), pipeline_mode=pl.Buffered(1)),
        pl.BlockSpec((tm, LANES), lambda i: (_lat_blk(tok(i), tm) % nb, 0)),
        pl.BlockSpec((tm, LANES), lambda i: (_lat_blk(tok(i), tm) % nb, 0))]
    out_specs = [pl.BlockSpec((tm, n_out), lambda i: (tok(i), 0))]
    out_shape = [jax.ShapeDtypeStruct((N_TOK, n_out), bf16)]
    for shp in cache_shapes:
        blk = (tm // SEQ,) + tuple(shp[1:])
        out_specs.append(pl.BlockSpec(blk, lambda i, nd=len(shp): (_ctx_blk(tok(i), tm),) + (0,) * (nd - 1)))
        out_shape.append(jax.ShapeDtypeStruct(tuple(shp), f32))
    return pl.pallas_call(
        functools.partial(_proj_kernel, n_x=len(x_parts), n_cache=len(cache_shapes), n_w=n_w,
                          rope_tiles=frozenset(rope_tiles), scale_tiles=frozenset(scale_tiles),
                          cache_plan=dict(cache_plan)),
        grid=(1 + N_TOK // tm,),
        in_specs=in_specs,
        out_specs=out_specs,
        out_shape=out_shape,
        scratch_shapes=[pltpu.VMEM((n_w, D_MODEL, N_CHUNK), bf16), pltpu.VMEM((tm, D_MODEL), bf16)],
        compiler_params=_params(1),
        name="proj",
    )(*x_parts, mod, w_all, *rope_tabs)


def _post_kernel(*refs, n_x, n_y, ka, kb, n_w):
    x_refs = refs[:n_x]
    (ac_ref, al_ref, bc_ref, bl_ref, mod_ref, lng_ref, lnb_ref,
     wo_ref, wg_ref, wu_ref, wd_ref) = refs[n_x:n_x + 11]
    y_refs = refs[n_x + 11:n_x + 11 + n_y]
    wo_s, wg_s, wu_s, wd_s, x1_ref, u_ref, h_ref, y_ref = refs[n_x + 11 + n_y:]
    i = pl.program_id(0)
    lead = n_w - 1
    gate1 = mod_ref[0, 2:3, :]
    shift2 = mod_ref[0, 3:4, :]
    scale2 = mod_ref[0, 4:5, :]
    gate2 = mod_ref[0, 5:6, :]
    groups = [slice(r * TM // ROW_GROUPS, (r + 1) * TM // ROW_GROUPS) for r in range(ROW_GROUPS)]

    def mix_in(rs, is_ctx):
        a = _pick((ac_ref, al_ref), is_ctx, rs)
        b = _pick((bc_ref, bl_ref), is_ctx, rs)
        pieces = ([a[:, c:c + N_CHUNK] for c in range(0, ka, N_CHUNK)]
                  + [b[:, c:c + N_CHUNK] for c in range(0, kb, N_CHUNK)])
        h = functools.reduce(lambda s, p: s + p, [_dot(p, wo_s[c]) for c, p in enumerate(pieces)])
        x1 = _ln(ALPHA * _pick(x_refs, is_ctx, rs) + gate1 * h) * lng_ref[0, 0:1, :] + lnb_ref[0, 0:1, :]
        x1_ref[rs, :] = x1
        u_ref[rs, :] = (_ln(x1) * (1.0 + scale2) + shift2).astype(bf16)

    def finish(rs, ffn, is_ctx):
        y = _ln(ALPHA * x1_ref[rs, :] + gate2 * ffn) * lng_ref[0, 1:2, :] + lnb_ref[0, 1:2, :]
        y_refs[0 if (n_y == 1 or is_ctx) else 1][rs, :] = y

    @pl.when(i < n_w)
    def _():
        wg_s[i] = wg_ref[0].astype(bf16)
        wu_s[i] = wu_ref[0].astype(bf16)
        wd_s[i] = wd_ref[0].astype(bf16)

        @pl.when(i == 0)
        def _():
            for c in range((ka + kb) // N_CHUNK):
                wo_s[c] = wo_ref[0, c * N_CHUNK:(c + 1) * N_CHUNK, :].astype(bf16)
            for rs in groups:
                mix_in(rs, True)
                y_ref[rs, :] = jnp.zeros((TM // ROW_GROUPS, D_MODEL), f32)

        for rs in groups:
            g = _dot(u_ref[rs, :], wg_s[i])
            up = _dot(u_ref[rs, :], wu_s[i])
            y_ref[rs, :] += _dot((_silu(g) * up).astype(bf16), wd_s[i])

        @pl.when(i == lead)
        def _():
            for rs in groups:
                finish(rs, y_ref[rs, :], True)

    def token_block(is_ctx):
        for rs in groups:
            mix_in(rs, is_ctx)

        def ffn_chunk(rs, c):
            g = _dot(u_ref[rs, :], wg_s[c])
            up = _dot(u_ref[rs, :], wu_s[c])
            h_ref[rs, c * N_CHUNK:(c + 1) * N_CHUNK] = (_silu(g) * up).astype(bf16)

        def ffn_down(rs):
            finish(rs, functools.reduce(
                lambda s, p: s + p,
                [_dot(h_ref[rs, c * N_CHUNK:(c + 1) * N_CHUNK], wd_s[c]) for c in range(n_w)]), is_ctx)

        for c in range(n_w + FFN_SKEW * (ROW_GROUPS - 1)):
            for r, rs in enumerate(groups):
                cc = c - FFN_SKEW * r
                if 0 <= cc < n_w:
                    ffn_chunk(rs, cc)
                if cc == n_w - 1:
                    ffn_down(rs)

    blk = i - lead

    @pl.when(jnp.logical_and(i >= n_w, blk < CTX_BLOCKS))
    def _():
        token_block(True)

    @pl.when(blk >= CTX_BLOCKS)
    def _():
        token_block(False)


def _post(x_parts, mix_a, mix_b, mod, ln_g, ln_b, w_out, w_gate, w_up, w_down, layer, mix_layer, split_out):
    ka, kb = mix_a[0].shape[1], mix_b[0].shape[1]
    n_w = D_FF // N_CHUNK
    lead = n_w - 1
    tok = lambda i: _tok(i, lead)
    lay = lambda i: (layer, 0, 0)
    in_specs = (_token_specs(x_parts, lead) + _token_specs(mix_a, lead) + _token_specs(mix_b, lead) + [
        pl.BlockSpec((None, 1, 6, D_MODEL), lambda i: (layer, _mod_row(tok(i)), 0, 0)),
        pl.BlockSpec((1, 2, D_MODEL), lay),
        pl.BlockSpec((1, 2, D_MODEL), lay),
        pl.BlockSpec((1, ka + kb, D_MODEL), lambda i: (mix_layer, 0, 0), pipeline_mode=pl.Buffered(1)),
        pl.BlockSpec((1, D_MODEL, N_CHUNK), lambda i: (layer, 0, jnp.minimum(i, n_w - 1))),
        pl.BlockSpec((1, D_MODEL, N_CHUNK), lambda i: (layer, 0, jnp.minimum(i, n_w - 1))),
        pl.BlockSpec((1, N_CHUNK, D_MODEL), lambda i: (layer, jnp.minimum(i, n_w - 1), 0))])
    if split_out:
        out_specs = [pl.BlockSpec((TM, D_MODEL), lambda i: (_ctx_blk(tok(i)), 0)),
                     pl.BlockSpec((TM, D_MODEL), lambda i: (_lat_blk(tok(i)), 0))]
        out_shape = [jax.ShapeDtypeStruct((N_CTX, D_MODEL), f32), jax.ShapeDtypeStruct((N_LAT, D_MODEL), f32)]
    else:
        out_specs = [pl.BlockSpec((TM, D_MODEL), lambda i: (tok(i), 0))]
        out_shape = [jax.ShapeDtypeStruct((N_TOK, D_MODEL), f32)]
    return pl.pallas_call(
        functools.partial(_post_kernel, n_x=len(x_parts), n_y=len(out_shape), ka=ka, kb=kb, n_w=n_w),
        grid=(lead + TOK_BLOCKS,),
        in_specs=in_specs,
        out_specs=out_specs,
        out_shape=out_shape,
        scratch_shapes=[pltpu.VMEM(((ka + kb) // N_CHUNK, N_CHUNK, D_MODEL), bf16),
                        pltpu.VMEM((n_w, D_MODEL, N_CHUNK), bf16),
                        pltpu.VMEM((n_w, D_MODEL, N_CHUNK), bf16), pltpu.VMEM((n_w, N_CHUNK, D_MODEL), bf16),
                        pltpu.VMEM((TM, D_MODEL), f32), pltpu.VMEM((TM, D_MODEL), bf16),
                        pltpu.VMEM((TM, D_FF), bf16), pltpu.VMEM((TM, D_MODEL), f32)],
        compiler_params=_params(1),
        name="post",
    )(*x_parts, *mix_a, *mix_b, mod, ln_g, ln_b, w_out, w_gate, w_up, w_down)


def _group_norm_gate(ro, rg, gmat, gn_g, gn_b):
    def gmean(parts):
        cols = []
        for c in range(0, RET_W, N_CHUNK):
            cols.append(sum(_dot(p[:, c:c + N_CHUNK], gmat) for p in parts))
        return jnp.concatenate(cols, -1)

    d = ro - gmean(_split_bf16(ro))
    var = gmean([(d * d).astype(bf16)])
    y = d * lax.rsqrt(var + LN_EPS) * gn_g + gn_b
    return _silu(rg.astype(f32)) * y


def _dup_head(x, j):
    first = _lane_half_mask(x.shape)
    keep = first if j == 0 else jnp.logical_not(first)
    xm = jnp.where(keep, x.astype(f32), 0.0)
    return xm + pltpu.roll(xm, HEAD_DIM, 1)


def _softmax_parts(scores, sink):
    m = sink
    for s in scores:
        m = jnp.maximum(m, jnp.max(s, -1, keepdims=True))
    es = [jnp.exp2(s - m) for s in scores]
    denom = jnp.exp2(sink - m)
    for e in es:
        denom = denom + jnp.sum(e, -1, keepdims=True)
    return es, denom


def _retention_tables(lg_ref, lgf_ref, lgb_ref, dmask_ref, kdec_ref, n):
    row = lax.broadcasted_iota(jnp.int32, (n, n), 0)
    col = lax.broadcasted_iota(jnp.int32, (n, n), 1)
    diff = (row - col).astype(f32)
    diag = jnp.where(row == col, 2.0 * QK_SCALE, QK_SCALE)
    for h in range(H_RET):
        dmask_ref[h] = jnp.exp(jnp.where(diff >= 0, lg_ref[0, h] * diff, -lg_ref[1, h] * diff)) * diag
    t = lax.broadcasted_iota(jnp.int32, (n, RET_W), 0).astype(f32)
    kdec_ref[0] = jnp.exp(lgf_ref[...] * (n - 1.0 - t)) * QK_SCALE
    kdec_ref[1] = jnp.exp(lgb_ref[...] * t) * QK_SCALE


def _retention_intra(pairs, q_of, k_of, v_of, dmask_ref):
    first = _lane_half_mask(k_of(pairs[0]).shape)
    masked = {}
    for p in pairs:
        kb = k_of(p)
        for e in range(2):
            keep = first if e == 0 else jnp.logical_not(first)
            s = _dot_nt(q_of(p), jnp.where(keep, kb, jnp.zeros_like(kb))) * dmask_ref[2 * p + e]
            masked[p, e] = s.astype(bf16)
    outs = {}
    for p in pairs:
        pv = [_dot(masked[p, e], v_of(p)) for e in range(2)]
        outs[p] = jnp.where(_lane_half_mask(pv[0].shape), pv[0], pv[1])
    return outs


def _window_group(subs, q_of, k_parts_of, v_parts_of, masks, sink_of):
    scores = {}
    for key in subs:
        parts = [_dot_nt(q_of(key), k) for k in k_parts_of(key)]
        scores[key] = [sc if mk is None else jnp.where(mk, sc, NEG_BIG) for sc, mk in zip(parts, masks)]
    probs = {}
    for key in subs:
        es, denom = _softmax_parts(scores[key], sink_of(key))
        probs[key] = ([ex.astype(bf16) for ex in es], denom)
    outs = {}
    for key in subs:
        es, denom = probs[key]
        pv = functools.reduce(lambda x, y: x + y, [_dot(ex, v) for ex, v in zip(es, v_parts_of(key))])
        outs[key] = pv / denom
    return outs


def _ctx_ab_kernel(lg_ref, sink_ref, rq_ref, rk_ref, rv_ref, rg_ref, wq_ref, wk_ref, wv_ref,
                   lgf_ref, lgb_ref, gmat_ref, gng_ref, gnb_ref,
                   ro_ref, wo_ref, st_ref, dmask_ref, kdec_ref, ret_ref):
    t_len = SEQ

    @pl.when(pl.program_id(0) == 0)
    def _():
        _retention_tables(lg_ref, lgf_ref, lgb_ref, dmask_ref, kdec_ref, t_len)

    first = _lane_half_mask((t_len, PAIR_W))
    for sq in range(CTX_SEQS):
        rows = slice(sq * t_len, (sq + 1) * t_len)
        psl = lambda p: slice(p * PAIR_W, (p + 1) * PAIR_W)
        for p0 in range(0, H_RET // 2, RET_GROUP):
            pairs = list(range(p0, p0 + RET_GROUP))
            intra = _retention_intra(pairs, lambda p: rq_ref[rows, psl(p)], lambda p: rk_ref[rows, psl(p)],
                                     lambda p: rv_ref[rows, psl(p)], dmask_ref)
            for p in pairs:
                ret_ref[rows, psl(p)] = intra[p]
        for p in range(H_RET // 2):
            sl = psl(p)
            kb = rk_ref[rows, sl]
            v = rv_ref[rows, sl]
            for d in range(2):
                kd_t = (kb * kdec_ref[d, :, sl]).T.astype(bf16)
                st = _dot(kd_t, v)
                st_ref[sq, d, 2 * p] = st[0:HEAD_DIM, 0:HEAD_DIM]
                st_ref[sq, d, 2 * p + 1] = pltpu.roll(st[HEAD_DIM:, :], HEAD_DIM, 1)[:, 0:HEAD_DIM]
        ro_ref[rows, :] = _group_norm_gate(ret_ref[rows, :], rg_ref[rows, :], gmat_ref[...], gng_ref[...],
                                           gnb_ref[...]).astype(bf16)

        k_dup = [_dup_head(wk_ref[rows, :], j).astype(bf16) for j in range(KV_WIN)]
        v_dup = [_dup_head(wv_ref[rows, :], j).astype(bf16) for j in range(KV_WIN)]

        def q_masked(key):
            qp, e = key
            qb = wq_ref[rows, qp * PAIR_W:(qp + 1) * PAIR_W]
            return jnp.where(first if e == 0 else jnp.logical_not(first), qb, jnp.zeros_like(qb))

        kv_of = lambda key: key[0] * 2 // G_WIN
        for g0 in range(0, H_WIN // 2, WIN_GROUP):
            subs = [(qp, e) for qp in range(g0, g0 + WIN_GROUP) for e in range(2)]
            outs = _window_group(subs, q_masked, lambda key: [k_dup[kv_of(key)]], lambda key: [v_dup[kv_of(key)]],
                                 [None], lambda key: sink_ref[0, 2 * key[0] + key[1]] * LOG2_E)
            for qp in range(g0, g0 + WIN_GROUP):
                wo_ref[rows, qp * PAIR_W:(qp + 1) * PAIR_W] = jnp.where(first, outs[qp, 0], outs[qp, 1]).astype(bf16)


def _ctx_ab(proj, log_gamma, sink, lgf_lanes, lgb_lanes, gmat, gn_g, gn_b):
    t = SEQ
    tb = CTX_SEQS * t
    smem = pl.BlockSpec(memory_space=pltpu.SMEM)
    const = lambda b: (0, 0)
    col = lambda c: (lambda b: (b, c))
    return pl.pallas_call(
        _ctx_ab_kernel,
        grid=(BATCH // CTX_SEQS,),
        in_specs=[smem, smem,
                  pl.BlockSpec((tb, RET_W), col(0)), pl.BlockSpec((tb, RET_W), col(1)),
                  pl.BlockSpec((tb, RET_W), col(2)), pl.BlockSpec((tb, RET_W), col(3)),
                  pl.BlockSpec((tb, WIN_W), col(4)),
                  pl.BlockSpec((tb, KV_W), col((4 * RET_W + WIN_W) // KV_W)),
                  pl.BlockSpec((tb, KV_W), col((4 * RET_W + WIN_W) // KV_W + 1)),
                  pl.BlockSpec((1, RET_W), const), pl.BlockSpec((1, RET_W), const),
                  pl.BlockSpec((N_CHUNK, N_CHUNK), const),
                  pl.BlockSpec((1, RET_W), const), pl.BlockSpec((1, RET_W), const)],
        out_specs=[pl.BlockSpec((tb, RET_W), lambda b: (b, 0)),
                   pl.BlockSpec((tb, WIN_W), lambda b: (b, 0)),
                   pl.BlockSpec((CTX_SEQS, 2, H_RET, HEAD_DIM, HEAD_DIM), lambda b: (b, 0, 0, 0, 0))],
        out_shape=[jax.ShapeDtypeStruct((BATCH * t, RET_W), bf16),
                   jax.ShapeDtypeStruct((BATCH * t, WIN_W), bf16),
                   jax.ShapeDtypeStruct((BATCH, 2, H_RET, HEAD_DIM, HEAD_DIM), f32)],
        scratch_shapes=[pltpu.VMEM((H_RET, t, t), f32), pltpu.VMEM((2, t, RET_W), f32),
                        pltpu.VMEM((tb, RET_W), f32)],
        compiler_params=_params(1),
        name="ctx_ab",
    )(log_gamma, sink, proj, proj, proj, proj, proj, proj, proj, lgf_lanes, lgb_lanes, gmat, gn_g, gn_b)


def _pair_state(s0_ref, d, p):
    zero = jnp.zeros((HEAD_DIM, HEAD_DIM), f32)
    top = jnp.concatenate([s0_ref[0, 0, d, 2 * p], zero], 1)
    bottom = jnp.concatenate([zero, s0_ref[0, 0, d, 2 * p + 1]], 1)
    return jnp.concatenate([top, bottom], 0)


def _lat_ab_kernel(lg_ref, sink_ref, rq_ref, rk_ref, rv_ref, rg_ref, wq_ref, wk_ref, wv_ref, ck_ref, cv_ref,
                   s0_ref, lgf_ref, lgb_ref, gmat_ref, gng_ref, gnb_ref,
                   ro_ref, wo_ref, ret_ref, dmask_ref, kdec_ref, qdec_ref, sf_ref, sb_ref):
    t_len = DEC_SEQ
    n_chunks = t_len // TQ
    step = pl.program_id(1)
    first = _lane_half_mask((TQ, PAIR_W))

    @pl.when(jnp.logical_and(pl.program_id(0) == 0, step == 0))
    def _():
        _retention_tables(lg_ref, lgf_ref, lgb_ref, dmask_ref, kdec_ref, TQ)
        t = lax.broadcasted_iota(jnp.int32, (TQ, RET_W), 0).astype(f32)
        qdec_ref[0] = jnp.exp(lgf_ref[...] * (t + 1.0))
        qdec_ref[1] = jnp.exp(lgb_ref[...] * (TQ - t))

    @pl.when(step == 0)
    def _():
        r = lax.broadcasted_iota(jnp.int32, (PAIR_W, PAIR_W), 0)
        c_ = lax.broadcasted_iota(jnp.int32, (PAIR_W, PAIR_W), 1)
        same_head = (r < HEAD_DIM) == (c_ < HEAD_DIM)
        for p in range(H_RET // 2):
            sl = slice(p * PAIR_W, (p + 1) * PAIR_W)
            kv = []
            for c in range(n_chunks):
                rows = slice(c * TQ, (c + 1) * TQ)
                kc = rk_ref[rows, sl]
                vc = rv_ref[rows, sl]
                kv.append([jnp.where(same_head, _dot((kc * kdec_ref[d, :, sl]).T.astype(bf16), vc), 0.0)
                           for d in range(2)])
            state = _pair_state(s0_ref, 0, p)
            for c in range(n_chunks):
                sf_ref[c, p] = state
                state = state * jnp.exp(lgf_ref[:, sl] * TQ) + kv[c][0]
            state = _pair_state(s0_ref, 1, p)
            for c in reversed(range(n_chunks)):
                sb_ref[c, p] = state
                state = state * jnp.exp(lgb_ref[:, sl] * TQ) + kv[c][1]

    for sub in range(LAT_SUB):
        chunk = step * LAT_SUB + sub
        q0 = pl.multiple_of(chunk * TQ, TQ)
        rows = slice(sub * TQ, (sub + 1) * TQ)
        psl = lambda p: slice(p * PAIR_W, (p + 1) * PAIR_W)
        intra = {}
        for p0 in range(0, H_RET // 2, RET_GROUP):
            intra.update(_retention_intra(list(range(p0, p0 + RET_GROUP)), lambda p: rq_ref[rows, psl(p)],
                                          lambda p: rk_ref[pl.ds(q0, TQ), psl(p)],
                                          lambda p: rv_ref[pl.ds(q0, TQ), psl(p)], dmask_ref))
        for p in range(H_RET // 2):
            sl = psl(p)
            q = rq_ref[rows, sl]
            o = intra[p]
            o = o + _dot(q, sf_ref[chunk, p].astype(bf16)) * qdec_ref[0, :, sl]
            o = o + _dot(q, sb_ref[chunk, p].astype(bf16)) * qdec_ref[1, :, sl]
            ret_ref[rows, sl] = o
        ro_ref[rows, :] = _group_norm_gate(ret_ref[rows, :], rg_ref[rows, :], gmat_ref[...], gng_ref[...],
                                           gnb_ref[...]).astype(bf16)

        band = TQ + 2 * WINDOW
        k_start = pl.multiple_of(jnp.clip(q0 - WINDOW, 0, t_len - band), LANES)
        qi = q0 + lax.broadcasted_iota(jnp.int32, (TQ, band), 0)
        kj = k_start + lax.broadcasted_iota(jnp.int32, (TQ, band), 1)
        in_band = jnp.abs(qi - kj) <= WINDOW
        k_parts = [[_dup_head(wk_ref[pl.ds(k_start, band), :], j).astype(bf16), _dup_head(ck_ref[0], j).astype(bf16)]
                   for j in range(KV_WIN)]
        v_parts = [[_dup_head(wv_ref[pl.ds(k_start, band), :], j).astype(bf16), _dup_head(cv_ref[0], j).astype(bf16)]
                   for j in range(KV_WIN)]

        def q_masked(key):
            qp, e = key
            qb = wq_ref[rows, qp * PAIR_W:(qp + 1) * PAIR_W]
            return jnp.where(first if e == 0 else jnp.logical_not(first), qb, jnp.zeros_like(qb))

        kv_of = lambda key: key[0] * 2 // G_WIN
        for g0 in range(0, H_WIN // 2, WIN_GROUP_LAT):
            subs = [(qp, e) for qp in range(g0, g0 + WIN_GROUP_LAT) for e in range(2)]
            outs = _window_group(subs, q_masked, lambda key: k_parts[kv_of(key)], lambda key: v_parts[kv_of(key)],
                                 [in_band, None], lambda key: sink_ref[0, 2 * key[0] + key[1]] * LOG2_E)
            for qp in range(g0, g0 + WIN_GROUP_LAT):
                wo_ref[rows, qp * PAIR_W:(qp + 1) * PAIR_W] = jnp.where(first, outs[qp, 0], outs[qp, 1]).astype(bf16)


def _lat_ab(proj, log_gamma, sink, ck, cv, state, layer, lgf_lanes, lgb_lanes, gmat, gn_g, gn_b):
    t = DEC_SEQ
    tb = LAT_SUB * TQ
    nq = t // tb
    smem = pl.BlockSpec(memory_space=pltpu.SMEM)
    const = lambda b, i: (0, 0)
    qcol = lambda c: (lambda b, i: (N_CTX // tb + b * nq + i, c))
    bcol = lambda c: (lambda b, i: (N_CTX // t + b, c))
    kv_col = (4 * RET_W + WIN_W) // KV_W
    return pl.pallas_call(
        _lat_ab_kernel,
        grid=(DEC_BATCH, nq),
        in_specs=[smem, smem,
                  pl.BlockSpec((tb, RET_W), qcol(0)), pl.BlockSpec((t, RET_W), bcol(1)),
                  pl.BlockSpec((t, RET_W), bcol(2)), pl.BlockSpec((tb, RET_W), qcol(3)),
                  pl.BlockSpec((tb, WIN_W), qcol(4)),
                  pl.BlockSpec((t, KV_W), bcol(kv_col)), pl.BlockSpec((t, KV_W), bcol(kv_col + 1)),
                  pl.BlockSpec((1, PAST_LEN, KV_W), lambda b, i: (b, 0, 0)),
                  pl.BlockSpec((1, PAST_LEN, KV_W), lambda b, i: (b, 0, 0)),
                  pl.BlockSpec((1, 1, 2, H_RET, HEAD_DIM, HEAD_DIM), lambda b, i: (b, layer, 0, 0, 0, 0)),
                  pl.BlockSpec((1, RET_W), const), pl.BlockSpec((1, RET_W), const),
                  pl.BlockSpec((N_CHUNK, N_CHUNK), const),
                  pl.BlockSpec((1, RET_W), const), pl.BlockSpec((1, RET_W), const)],
        out_specs=[pl.BlockSpec((tb, RET_W), lambda b, i: (b * nq + i, 0)),
                   pl.BlockSpec((tb, WIN_W), lambda b, i: (b * nq + i, 0))],
        out_shape=[jax.ShapeDtypeStruct((DEC_BATCH * t, RET_W), bf16),
                   jax.ShapeDtypeStruct((DEC_BATCH * t, WIN_W), bf16)],
        scratch_shapes=[pltpu.VMEM((tb, RET_W), f32), pltpu.VMEM((H_RET, TQ, TQ), f32),
                        pltpu.VMEM((2, TQ, RET_W), f32), pltpu.VMEM((2, TQ, RET_W), f32),
                        pltpu.VMEM((t // TQ, H_RET // 2, PAIR_W, PAIR_W), f32),
                        pltpu.VMEM((t // TQ, H_RET // 2, PAIR_W, PAIR_W), f32)],
        compiler_params=_params(2),
        name="lat_ab",
    )(log_gamma, sink, proj, proj, proj, proj, proj, proj, proj, ck, cv, state,
      lgf_lanes, lgb_lanes, gmat, gn_g, gn_b)


def _lambda_full(lam_ref, lam_init):
    lam = lam_ref[...]
    a = jnp.sum(lam[0:1, :] * lam[1:2, :], -1, keepdims=True)
    b = jnp.sum(lam[2:3, :] * lam[3:4, :], -1, keepdims=True)
    return jnp.exp(a) - jnp.exp(b) + lam_init


def _diff_heads(q_of, k_parts_of, v_parts_of, lam, subln, lam_init, group):
    res = []
    for h0 in range(0, H_DIFF, group):
        res += _diff_head_group(range(h0, h0 + group), q_of, k_parts_of, v_parts_of, lam, subln, lam_init)
    return res


def _diff_head_group(heads, q_of, k_parts_of, v_parts_of, lam, subln, lam_init):
    subs = [(h, e) for h in heads for e in range(2)]
    scores = {}
    for h, e in subs:
        q = q_of(h)
        fm = _lane_half_mask(q.shape)
        q_sub = jnp.where(fm if e == 0 else jnp.logical_not(fm), q, jnp.zeros_like(q))
        scores[h, e] = [_dot(q_sub, k) if transposed else _dot_nt(q_sub, k) for k, transposed in k_parts_of(h)]
    probs = {}
    for key in subs:
        m = scores[key][0].max(-1, keepdims=True)
        for sc in scores[key][1:]:
            m = jnp.maximum(m, sc.max(-1, keepdims=True))
        es = [jnp.exp2(sc - m) for sc in scores[key]]
        denom = es[0].sum(-1, keepdims=True)
        for ex in es[1:]:
            denom = denom + ex.sum(-1, keepdims=True)
        probs[key] = ([ex.astype(bf16) for ex in es], denom)
    outs = {}
    for h, e in subs:
        es, denom = probs[h, e]
        pv = functools.reduce(lambda x, y: x + y, [_dot(ex, v) for v, ex in zip(v_parts_of(h), es)])
        outs[h, e] = pv / denom
    res = []
    for h in heads:
        a = outs[h, 0] - lam * outs[h, 1]
        res.append(a * lax.rsqrt(jnp.mean(a * a, -1, keepdims=True) + LN_EPS) * subln * (1.0 - lam_init))
    return res


def _fourier_rows(ct_ref, st_ref, z, bdc_ref, bds_ref):
    zc = _dot(z, bdc_ref[...].astype(bf16)).astype(bf16)
    zs = _dot(z, bds_ref[...].astype(bf16)).astype(bf16)
    return _dot(ct_ref[...].astype(bf16), zc) - _dot(st_ref[...].astype(bf16), zs)


def _ctx_cd_kernel(q_ref, k_ref, v_ref, z_ref, lam_ref, subln_ref, ct_ref, st_ref, bdc_ref, bds_ref,
                   a_ref, zf_ref, *, lam_init):
    lam = _lambda_full(lam_ref, lam_init)
    for sq in range(CTX_SEQS_CD):
        rows = slice(sq * SEQ, (sq + 1) * SEQ)
        sl = lambda h: slice(h * PAIR_W, (h + 1) * PAIR_W)
        heads = _diff_heads(lambda h: q_ref[rows, sl(h)], lambda h: [(k_ref[rows, sl(h)], False)],
                            lambda h: [v_ref[rows, sl(h)]], lam, subln_ref[...], lam_init, DIFF_GROUP)
        for h in range(H_DIFF):
            a_ref[rows, sl(h)] = heads[h].astype(bf16)
        zf_ref[rows, :] = _fourier_rows(ct_ref, st_ref, z_ref[rows, :], bdc_ref, bds_ref).astype(bf16)


def _ctx_cd(proj, lam, subln, ct, st, bdc, bds, lam_init):
    t = SEQ
    tb = CTX_SEQS_CD * t
    const = lambda b: (0, 0)
    col = lambda c: (lambda b: (b, c))
    return pl.pallas_call(
        functools.partial(_ctx_cd_kernel, lam_init=lam_init),
        grid=(BATCH // CTX_SEQS_CD,),
        in_specs=[pl.BlockSpec((tb, DIFF_W), col(0)), pl.BlockSpec((tb, DIFF_W), col(1)),
                  pl.BlockSpec((tb, DIFF_W), col(2)), pl.BlockSpec((tb, FNET_W), col(3 * DIFF_W // FNET_W)),
                  pl.BlockSpec((4, HEAD_DIM), const), pl.BlockSpec((1, PAIR_W), const),
                  pl.BlockSpec((t, t), const), pl.BlockSpec((t, t), const),
                  pl.BlockSpec((FNET_W, FNET_W), const), pl.BlockSpec((FNET_W, FNET_W), const)],
        out_specs=[pl.BlockSpec((tb, DIFF_W), lambda b: (b, 0)), pl.BlockSpec((tb, FNET_W), lambda b: (b, 0))],
        out_shape=[jax.ShapeDtypeStruct((BATCH * t, DIFF_W), bf16),
                   jax.ShapeDtypeStruct((BATCH * t, FNET_W), bf16)],
        compiler_params=_params(1),
        name="ctx_cd",
    )(proj, proj, proj, proj, lam, subln, ct, st, bdc, bds)


def _lat_cd_kernel(q_ref, k_ref, v_ref, z_ref, ckt_ref, cv_ref, lam_ref, subln_ref, ct_ref, st_ref, bdc_ref, bds_ref,
                   a_ref, zf_ref, *, lam_init):
    lam = _lambda_full(lam_ref, lam_init)
    sl = lambda h: slice(h * PAIR_W, (h + 1) * PAIR_W)
    heads = _diff_heads(lambda h: q_ref[:, sl(h)],
                        lambda h: [(k_ref[:, sl(h)], False), (ckt_ref[0, h].astype(bf16), True)],
                        lambda h: [v_ref[:, sl(h)], cv_ref[0, h].astype(bf16)], lam, subln_ref[...], lam_init,
                        DIFF_GROUP_LAT)
    for h in range(H_DIFF):
        a_ref[:, sl(h)] = heads[h].astype(bf16)
    zf_ref[...] = _fourier_rows(ct_ref, st_ref, z_ref[...], bdc_ref, bds_ref).astype(bf16)


def _lat_cd(proj, ck, cv, lam, subln, ct, st, bdc, bds, lam_init):
    t = DEC_SEQ
    nq = t // TQ_CD
    const = lambda b, i: (0, 0)
    return pl.pallas_call(
        functools.partial(_lat_cd_kernel, lam_init=lam_init),
        grid=(DEC_BATCH, nq),
        in_specs=[pl.BlockSpec((TQ_CD, DIFF_W), lambda b, i: (N_CTX // TQ_CD + b * nq + i, 0)),
                  pl.BlockSpec((t, DIFF_W), lambda b, i: (N_CTX // t + b, 1)),
                  pl.BlockSpec((t, DIFF_W), lambda b, i: (N_CTX // t + b, 2)),
                  pl.BlockSpec((t, FNET_W), lambda b, i: (N_CTX // t + b, 3 * DIFF_W // FNET_W)),
                  pl.BlockSpec((1, H_DIFF, PAIR_W, PAST_LEN), lambda b, i: (b, 0, 0, 0)),
                  pl.BlockSpec((1, H_DIFF, PAST_LEN, PAIR_W), lambda b, i: (b, 0, 0, 0)),
                  pl.BlockSpec((4, HEAD_DIM), const), pl.BlockSpec((1, PAIR_W), const),
                  pl.BlockSpec((TQ_CD, t), lambda b, i: (i, 0)), pl.BlockSpec((TQ_CD, t), lambda b, i: (i, 0)),
                  pl.BlockSpec((FNET_W, FNET_W), const), pl.BlockSpec((FNET_W, FNET_W), const)],
        out_specs=[pl.BlockSpec((TQ_CD, DIFF_W), lambda b, i: (b * nq + i, 0)),
                   pl.BlockSpec((TQ_CD, FNET_W), lambda b, i: (b * nq + i, 0))],
        out_shape=[jax.ShapeDtypeStruct((DEC_BATCH * t, DIFF_W), bf16),
                   jax.ShapeDtypeStruct((DEC_BATCH * t, FNET_W), bf16)],
        compiler_params=_params(2),
        name="lat_cd",
    )(proj, proj, proj, proj, ck, cv, lam, subln, ct, st, bdc, bds)


def _rope_tables():
    t = np.arange(DEC_SEQ)
    quarter = HEAD_DIM // 4
    inv = ROPE_BASE ** (-np.arange(quarter, dtype=np.float64) / quarter)
    ang = np.concatenate([(t // GRID_W)[:, None] * inv, (t % GRID_W)[:, None] * inv], -1)
    cos, sin = np.cos(ang), np.sin(ang)
    reps = LANES // HEAD_DIM
    return (np.tile(np.concatenate([cos, cos], -1), (1, reps)).astype(np.float32),
            np.tile(np.concatenate([-sin, sin], -1), (1, reps)).astype(np.float32))


def _dft_tables(n):
    k = np.arange(n)
    ang = (2.0 * math.pi / n) * ((k[:, None] * k[None, :]) % n)
    return (np.cos(ang) / math.sqrt(n)).astype(np.float32), (np.sin(ang) / math.sqrt(n)).astype(np.float32)


def _block_diag(m, reps):
    return np.kron(np.eye(reps, dtype=m.dtype), m)


def kernel(x_prompt, x_sample, state_ret, cache_win_k, cache_win_v, cache_diff_k, cache_diff_v, c, c_ctx, w_mod, b_mod, ln_g, ln_b, w_in_ab, w_out_ab, ret_log_gamma, ret_gn_g, ret_gn_b, win_sink, w_in_cd, w_out_cd, diff_lambda, diff_subln_g, w_gate, w_up, w_down):
    cond = jnp.concatenate([c_ctx[None, :], c, jnp.zeros((SUBLANES - 1 - DEC_BATCH, D_MODEL), f32)], 0)
    mod = _modulation(cond, w_mod, b_mod).reshape(DEPTH, SUBLANES, 6, D_MODEL)

    rope_tabs = _rope_tables()
    gmat = jnp.asarray(_block_diag(np.full((HEAD_DIM, HEAD_DIM), 1.0 / HEAD_DIM, np.float32),
                                   N_CHUNK // HEAD_DIM), bf16)
    c64, s64 = _dft_tables(FNET_DIM)
    bdc = _block_diag(c64, FNET_GROUPS)
    bds = _block_diag(s64, FNET_GROUPS)
    dft_ctx = _dft_tables(SEQ)
    dft_lat = _dft_tables(DEC_SEQ)

    x_parts = [x_prompt.reshape(N_CTX, D_MODEL), x_sample.reshape(N_LAT, D_MODEL)]
    outs = {}
    for l in range(DEPTH):
        i = l // 2
        if l % 2 == 0:
            lgf = jnp.repeat(ret_log_gamma[i, 0], HEAD_DIM)[None, :]
            lgb = jnp.repeat(ret_log_gamma[i, 1], HEAD_DIM)[None, :]
            gn_g = ret_gn_g[i][None, :]
            gn_b = ret_gn_b[i][None, :]
            sink = win_sink[i][None, :]
            rope_tiles = tuple(range(0, 2 * RET_W // LANES)) + tuple(
                range(4 * RET_W // LANES, (4 * RET_W + WIN_W + KV_W) // LANES))
            kv_tile = (4 * RET_W + WIN_W) // LANES
            kv_shape = (BATCH, 1, KV_WIN, HEAD_DIM, SEQ)
            scale_tiles = tuple(range(4 * RET_W // LANES, (4 * RET_W + WIN_W) // LANES))
            proj, wk_t, wv_t = _proj(x_parts, mod, l, w_in_ab, i, scale_tiles, rope_tabs, rope_tiles,
                                     (kv_shape, kv_shape),
                                     {kv_tile: ("heads", 0, 0), kv_tile + 1: ("heads", 1, 0)})
            ro_c, wo_c, st_c = _ctx_ab(proj, ret_log_gamma[i], sink, lgf, lgb, gmat, gn_g, gn_b)
            ck = cache_win_k[:, i].reshape(DEC_BATCH, PAST_LEN, KV_W)
            cv = cache_win_v[:, i].reshape(DEC_BATCH, PAST_LEN, KV_W)
            ro_l, wo_l = _lat_ab(proj, ret_log_gamma[i], sink, ck, cv, state_ret, i, lgf, lgb, gmat, gn_g, gn_b)
            mix_a, mix_b, w_out = (ro_c, ro_l), (wo_c, wo_l), w_out_ab
            outs.setdefault('state', []).append(st_c[:, None])
            outs.setdefault('win_k', []).append(jnp.transpose(wk_t, (0, 1, 4, 2, 3)))
            outs.setdefault('win_v', []).append(jnp.transpose(wv_t, (0, 1, 4, 2, 3)))
        else:
            lam_init = 0.8 - 0.6 * math.exp(-0.3 * l)
            subln = diff_subln_g[i][None, :]
            rope_tiles = tuple(range(0, 2 * DIFF_W // LANES))
            plan = {}
            for h in range(H_DIFF):
                plan[DIFF_W // LANES + h] = ("pairs", 0, h)
                plan[2 * DIFF_W // LANES + h] = ("plain", 1, h)
            scale_tiles = tuple(range(0, DIFF_W // LANES))
            proj, dk_t, dv_h = _proj(
                x_parts, mod, l, w_in_cd, i, scale_tiles, rope_tabs, rope_tiles,
                ((BATCH, 1, H_DIFF, 2, HEAD_DIM, SEQ), (BATCH, 1, H_DIFF, SEQ, 2 * HEAD_DIM)), plan)
            a_c, z_c = _ctx_cd(proj, diff_lambda[i], subln, dft_ctx[0], dft_ctx[1], bdc, bds, lam_init)
            ck = jnp.transpose(cache_diff_k[:, i], (0, 2, 3, 4, 1)).reshape(DEC_BATCH, H_DIFF, PAIR_W, PAST_LEN)
            cv = jnp.transpose(cache_diff_v[:, i], (0, 2, 1, 3))
            a_l, z_l = _lat_cd(proj, ck, cv, diff_lambda[i], subln, dft_lat[0], dft_lat[1], bdc, bds, lam_init)
            mix_a, mix_b, w_out = (a_c, a_l), (z_c, z_l), w_out_cd
            outs.setdefault('diff_k', []).append(jnp.transpose(dk_t, (0, 1, 5, 2, 3, 4)))
            outs.setdefault('diff_v', []).append(jnp.transpose(dv_h, (0, 1, 3, 2, 4)))
        x_parts = _post(x_parts, mix_a, mix_b, mod, ln_g, ln_b, w_out, w_gate, w_up, w_down, l, i,
                        split_out=(l == DEPTH - 1))

    y_prompt = x_parts[0].reshape(BATCH, SEQ, D_MODEL)
    y_sample = x_parts[1].reshape(DEC_BATCH, DEC_SEQ, D_MODEL)
    cat = lambda parts: parts[0] if len(parts) == 1 else jnp.concatenate(parts, 1)
    return (y_prompt, y_sample, cat(outs['state']), cat(outs['win_k']), cat(outs['win_v']),
            cat(outs['diff_k']), cat(outs['diff_v']))
```

```python
import functools
import math

import jax
import jax.numpy as jnp
import numpy as np
from jax import lax
from jax.experimental import pallas as pl
from jax.experimental.pallas import tpu as pltpu

D_MODEL = 1024
BATCH = 32
SEQ = 256
DEPTH = 2
DEC_BATCH = 2
DEC_SEQ = 1024
PAST_LEN = 512
GRID_W = 64
HEAD_DIM = 64
ROPE_BASE = 10000.0
H_RET = 8
H_WIN = 8
KV_WIN = 2
G_WIN = H_WIN // KV_WIN
WINDOW = 128
H_DIFF = 6
FNET_GROUPS = 4
FNET_DIM = 64
D_FF = 256 * math.ceil(8 * D_MODEL / 3 / 256)
RET_W = H_RET * HEAD_DIM
WIN_W = H_WIN * HEAD_DIM
KV_W = KV_WIN * HEAD_DIM
AB_IN = 4 * RET_W + WIN_W + 2 * KV_W
DIFF_W = H_DIFF * 2 * HEAD_DIM
FNET_W = FNET_GROUPS * FNET_DIM
CD_IN = 3 * DIFF_W + FNET_W
ALPHA = (2 * DEPTH) ** 0.25
LN_EPS = 1e-5
QK_SCALE = HEAD_DIM ** -0.5
LOG2_E = math.log2(math.e)

N_CTX = BATCH * SEQ
N_LAT = DEC_BATCH * DEC_SEQ
N_TOK = N_CTX + N_LAT

LANES = 128
SUBLANES = 8
PAIR_W = 2 * HEAD_DIM
TM = 512
TM_PROJ = 1024
CTX_BLOCKS = N_CTX // TM
TOK_BLOCKS = N_TOK // TM
ROW_GROUPS = 2
FFN_SKEW = 2
TQ = 256
LAT_SUB = 4
TQ_CD = 512
CTX_SEQS = 4
CTX_SEQS_CD = 8
RET_GROUP = 4
WIN_GROUP = 4
WIN_GROUP_LAT = 4
DIFF_GROUP_LAT = 1
DIFF_GROUP = 3
N_CHUNK = 256
MOD_TN = 1536
NEG_BIG = -1e30
VMEM_LIMIT = 56 * 1024 * 1024

f32 = jnp.float32
bf16 = jnp.bfloat16


def _params(n_axes):
    return pltpu.CompilerParams(dimension_semantics=("arbitrary",) * n_axes,
                                vmem_limit_bytes=VMEM_LIMIT)


def _dot(a, b):
    return jnp.dot(a, b, preferred_element_type=f32)


def _dot_nt(a, b):
    return lax.dot_general(a, b, (((1,), (1,)), ((), ())), preferred_element_type=f32)


def _ln(x):
    mu = jnp.mean(x, -1, keepdims=True)
    d = x - mu
    var = jnp.mean(d * d, -1, keepdims=True)
    return d * lax.rsqrt(var + LN_EPS)


def _silu(x):
    return x * jax.nn.sigmoid(x)


def _split_bf16(x):
    hi = x.astype(bf16)
    lo = (x - hi.astype(f32)).astype(bf16)
    return hi, lo


def _lane_half_mask(shape):
    return (lax.broadcasted_iota(jnp.int32, shape, len(shape) - 1) & HEAD_DIM) == 0


def _mod_kernel(c_ref, w_ref, b_ref, o_ref):
    layer = pl.program_id(0)
    a = _silu(c_ref[...])
    rows = a.shape[0]
    a_hi, a_lo = _split_bf16(a)
    w_hi, w_lo = _split_bf16(w_ref[0])
    both = _dot(jnp.concatenate([a_hi, a_lo], 0), w_hi)
    o_ref[0] = both[:rows] + both[rows:] + _dot(a_hi, w_lo) + b_ref[pl.ds(layer, 1), :]


def _modulation(cond, w_mod, b_mod):
    tn = MOD_TN
    rows = cond.shape[0]
    return pl.pallas_call(
        _mod_kernel,
        grid=(DEPTH, 6 * D_MODEL // tn),
        in_specs=[pl.BlockSpec((rows, D_MODEL), lambda l, j: (0, 0)),
                  pl.BlockSpec((1, D_MODEL, tn), lambda l, j: (l, 0, j)),
                  pl.BlockSpec((DEPTH, tn), lambda l, j: (0, j))],
        out_specs=pl.BlockSpec((1, rows, tn), lambda l, j: (l, 0, j)),
        out_shape=jax.ShapeDtypeStruct((DEPTH, rows, 6 * D_MODEL), f32),
        compiler_params=_params(2),
        name="modulation",
    )(cond, w_mod, b_mod)


def _tok(i, n_w):
    return jnp.maximum(i - n_w, 0)


def _ctx_blk(t, tm=TM):
    return jnp.minimum(t, N_CTX // tm - 1)


def _lat_blk(t, tm=TM):
    return jnp.maximum(t - N_CTX // tm, 0)


def _mod_row(t, tm=TM):
    return jnp.where(t < N_CTX // tm, 0, 1 + _lat_blk(t, tm) * tm // DEC_SEQ)


def _token_specs(parts, n_w, tm=TM):
    width = parts[0].shape[1]
    if len(parts) == 1:
        return [pl.BlockSpec((tm, width), lambda i: (_tok(i, n_w), 0))]
    return [pl.BlockSpec((tm, width), lambda i: (_ctx_blk(_tok(i, n_w), tm), 0)),
            pl.BlockSpec((tm, width), lambda i: (_lat_blk(_tok(i, n_w), tm), 0))]


def _pick(refs, is_ctx, rs):
    return refs[0 if (len(refs) == 1 or is_ctx) else 1][rs, :]


def _rope_pair(y, cos, sin_signed):
    first_half = (lax.broadcasted_iota(jnp.int32, y.shape, 1) & (HEAD_DIM // 2)) == 0
    swapped = jnp.where(first_half, pltpu.roll(y, LANES - HEAD_DIM // 2, 1), pltpu.roll(y, HEAD_DIM // 2, 1))
    return y * cos + swapped * sin_signed


def _proj_kernel(*refs, n_x, n_cache, n_w, rope_tiles, scale_tiles, cache_plan):
    x_refs = refs[:n_x]
    mod_ref, w_ref, cos_ref, sin_ref, o_ref = refs[n_x:n_x + 5]
    cache_refs = refs[n_x + 5:n_x + 5 + n_cache]
    wbf_ref, u_ref = refs[n_x + 5 + n_cache:]
    i = pl.program_id(0)

    @pl.when(i == 0)
    def _():
        for c in range(n_w):
            wbf_ref[c] = w_ref[0, :, c * N_CHUNK:(c + 1) * N_CHUNK].astype(bf16)

    def tokens(is_ctx):
        x_ref = x_refs[0] if is_ctx else x_refs[-1]
        shift = mod_ref[0, 0:1, :]
        scale = mod_ref[0, 1:2, :]
        groups = [slice(b * SEQ, (b + 1) * SEQ) for b in range(TM_PROJ // SEQ)]
        for rs in groups:
            u_ref[rs, :] = (_ln(x_ref[rs, :]) * (1.0 + scale) + shift).astype(bf16)
        for c in range(n_w):
            y_all = _dot(u_ref[...], wbf_ref[c])
            for b, rs in enumerate(groups):
                y = y_all[rs, :]
                for t in range(N_CHUNK // LANES):
                    tile = c * (N_CHUNK // LANES) + t
                    piece = y[:, t * LANES:(t + 1) * LANES]
                    if tile in rope_tiles and not is_ctx:
                        piece = _rope_pair(piece, cos_ref[rs, :], sin_ref[rs, :])
                    if tile in scale_tiles:
                        piece = piece * (QK_SCALE * LOG2_E)
                    o_ref[rs, tile * LANES:(tile + 1) * LANES] = piece.astype(o_ref.dtype)
                    if tile in cache_plan and is_ctx:
                        kind, out_idx, slot = cache_plan[tile]
                        c_ref = cache_refs[out_idx]
                        if kind == "plain":
                            c_ref[b, 0, slot] = piece
                        else:
                            piece_t = piece.T
                            if kind == "heads":
                                c_ref[b, 0, 0] = piece_t[0:HEAD_DIM]
                                c_ref[b, 0, 1] = piece_t[HEAD_DIM:]
                            else:
                                c_ref[b, 0, slot, 0] = piece_t[0:HEAD_DIM]
                                c_ref[b, 0, slot, 1] = piece_t[HEAD_DIM:]

    t = i - 1

    @pl.when(jnp.logical_and(t >= 0, t < N_CTX // TM_PROJ))
    def _():
        tokens(True)

    @pl.when(t >= N_CTX // TM_PROJ)
    def _():
        tokens(False)


def _proj(x_parts, mod, mod_layer, w_all, layer, scale_tiles, rope_tabs, rope_tiles, cache_shapes, cache_plan):
    n_out = w_all.shape[2]
    n_w = n_out // N_CHUNK
    tm = TM_PROJ
    nb = DEC_SEQ // tm
    tok = lambda i: _tok(i, 1)
    in_specs = _token_specs(x_parts, 1, tm) + [
        pl.BlockSpec((None, 1, 6, D_MODEL), lambda i: (mod_layer, _mod_row(tok(i), tm), 0, 0)),
        pl.BlockSpec((1, D_MODEL, n_out), lambda i: (layer, 0, 0), pipeline_mode=pl.Buffered(1)),
        pl.BlockSpec((tm, LANES), lambda i: (_lat_blk(tok(i), tm) % nb, 0)),
        pl.BlockSpec((tm, LANES), lambda i: (_lat_blk(tok(i), tm) % nb, 0))]
    out_specs = [pl.BlockSpec((tm, n_out), lambda i: (tok(i), 0))]
    out_shape = [jax.ShapeDtypeStruct((N_TOK, n_out), bf16)]
    for shp in cache_shapes:
        blk = (tm // SEQ,) + tuple(shp[1:])
        out_specs.append(pl.BlockSpec(blk, lambda i, nd=len(shp): (_ctx_blk(tok(i), tm),) + (0,) * (nd - 1)))
        out_shape.append(jax.ShapeDtypeStruct(tuple(shp), f32))
    return pl.pallas_call(
        functools.partial(_proj_kernel, n_x=len(x_parts), n_cache=len(cache_shapes), n_w=n_w,
                          rope_tiles=frozenset(rope_tiles), scale_tiles=frozenset(scale_tiles),
                          cache_plan=dict(cache_plan)),
        grid=(1 + N_TOK // tm,),
        in_specs=in_specs,
        out_specs=out_specs,
        out_shape=out_shape,
        scratch_shapes=[pltpu.VMEM((n_w, D_MODEL, N_CHUNK), bf16), pltpu.VMEM((tm, D_MODEL), bf16)],
        compiler_params=_params(1),
        name="proj",
    )(*x_parts, mod, w_all, *rope_tabs)


def _post_kernel(*refs, n_x, n_y, ka, kb, n_w):
    x_refs = refs[:n_x]
    (ac_ref, al_ref, bc_ref, bl_ref, mod_ref, lng_ref, lnb_ref,
     wo_ref, wg_ref, wu_ref, wd_ref) = refs[n_x:n_x + 11]
    y_refs = refs[n_x + 11:n_x + 11 + n_y]
    wo_s, wg_s, wu_s, wd_s, x1_ref, u_ref, h_ref, y_ref = refs[n_x + 11 + n_y:]
    i = pl.program_id(0)
    lead = n_w - 1
    gate1 = mod_ref[0, 2:3, :]
    shift2 = mod_ref[0, 3:4, :]
    scale2 = mod_ref[0, 4:5, :]
    gate2 = mod_ref[0, 5:6, :]
    groups = [slice(r * TM // ROW_GROUPS, (r + 1) * TM // ROW_GROUPS) for r in range(ROW_GROUPS)]

    def mix_in(rs, is_ctx):
        a = _pick((ac_ref, al_ref), is_ctx, rs)
        b = _pick((bc_ref, bl_ref), is_ctx, rs)
        pieces = ([a[:, c:c + N_CHUNK] for c in range(0, ka, N_CHUNK)]
                  + [b[:, c:c + N_CHUNK] for c in range(0, kb, N_CHUNK)])
        h = functools.reduce(lambda s, p: s + p, [_dot(p, wo_s[c]) for c, p in enumerate(pieces)])
        x1 = _ln(ALPHA * _pick(x_refs, is_ctx, rs) + gate1 * h) * lng_ref[0, 0:1, :] + lnb_ref[0, 0:1, :]
        x1_ref[rs, :] = x1
        u_ref[rs, :] = (_ln(x1) * (1.0 + scale2) + shift2).astype(bf16)

    def finish(rs, ffn, is_ctx):
        y = _ln(ALPHA * x1_ref[rs, :] + gate2 * ffn) * lng_ref[0, 1:2, :] + lnb_ref[0, 1:2, :]
        y_refs[0 if (n_y == 1 or is_ctx) else 1][rs, :] = y

    @pl.when(i < n_w)
    def _():
        wg_s[i] = wg_ref[0].astype(bf16)
        wu_s[i] = wu_ref[0].astype(bf16)
        wd_s[i] = wd_ref[0].astype(bf16)

        @pl.when(i == 0)
        def _():
            for c in range((ka + kb) // N_CHUNK):
                wo_s[c] = wo_ref[0, c * N_CHUNK:(c + 1) * N_CHUNK, :].astype(bf16)
            for rs in groups:
                mix_in(rs, True)
                y_ref[rs, :] = jnp.zeros((TM // ROW_GROUPS, D_MODEL), f32)

        for rs in groups:
            g = _dot(u_ref[rs, :], wg_s[i])
            up = _dot(u_ref[rs, :], wu_s[i])
            y_ref[rs, :] += _dot((_silu(g) * up).astype(bf16), wd_s[i])

        @pl.when(i == lead)
        def _():
            for rs in groups:
                finish(rs, y_ref[rs, :], True)

    def token_block(is_ctx):
        for rs in groups:
            mix_in(rs, is_ctx)

        def ffn_chunk(rs, c):
            g = _dot(u_ref[rs, :], wg_s[c])
            up = _dot(u_ref[rs, :], wu_s[c])
            h_ref[rs, c * N_CHUNK:(c + 1) * N_CHUNK] = (_silu(g) * up).astype(bf16)

        def ffn_down(rs):
            finish(rs, functools.reduce(
                lambda s, p: s + p,
                [_dot(h_ref[rs, c * N_CHUNK:(c + 1) * N_CHUNK], wd_s[c]) for c in range(n_w)]), is_ctx)

        for c in range(n_w + FFN_SKEW * (ROW_GROUPS - 1)):
            for r, rs in enumerate(groups):
                cc = c - FFN_SKEW * r
                if 0 <= cc < n_w:
                    ffn_chunk(rs, cc)
                if cc == n_w - 1:
                    ffn_down(rs)

    blk = i - lead

    @pl.when(jnp.logical_and(i >= n_w, blk < CTX_BLOCKS))
    def _():
        token_block(True)

    @pl.when(blk >= CTX_BLOCKS)
    def _():
        token_block(False)


def _post(x_parts, mix_a, mix_b, mod, ln_g, ln_b, w_out, w_gate, w_up, w_down, layer, mix_layer, split_out):
    ka, kb = mix_a[0].shape[1], mix_b[0].shape[1]
    n_w = D_FF // N_CHUNK
    lead = n_w - 1
    tok = lambda i: _tok(i, lead)
    lay = lambda i: (layer, 0, 0)
    in_specs = (_token_specs(x_parts, lead) + _token_specs(mix_a, lead) + _token_specs(mix_b, lead) + [
        pl.BlockSpec((None, 1, 6, D_MODEL), lambda i: (layer, _mod_row(tok(i)), 0, 0)),
        pl.BlockSpec((1, 2, D_MODEL), lay),
        pl.BlockSpec((1, 2, D_MODEL), lay),
        pl.BlockSpec((1, ka + kb, D_MODEL), lambda i: (mix_layer, 0, 0), pipeline_mode=pl.Buffered(1)),
        pl.BlockSpec((1, D_MODEL, N_CHUNK), lambda i: (layer, 0, jnp.minimum(i, n_w - 1))),
        pl.BlockSpec((1, D_MODEL, N_CHUNK), lambda i: (layer, 0, jnp.minimum(i, n_w - 1))),
        pl.BlockSpec((1, N_CHUNK, D_MODEL), lambda i: (layer, jnp.minimum(i, n_w - 1), 0))])
    if split_out:
        out_specs = [pl.BlockSpec((TM, D_MODEL), lambda i: (_ctx_blk(tok(i)), 0)),
                     pl.BlockSpec((TM, D_MODEL), lambda i: (_lat_blk(tok(i)), 0))]
        out_shape = [jax.ShapeDtypeStruct((N_CTX, D_MODEL), f32), jax.ShapeDtypeStruct((N_LAT, D_MODEL), f32)]
    else:
        out_specs = [pl.BlockSpec((TM, D_MODEL), lambda i: (tok(i), 0))]
        out_shape = [jax.ShapeDtypeStruct((N_TOK, D_MODEL), f32)]
    return pl.pallas_call(
        functools.partial(_post_kernel, n_x=len(x_parts), n_y=len(out_shape), ka=ka, kb=kb, n_w=n_w),
        grid=(lead + TOK_BLOCKS,),
        in_specs=in_specs,
        out_specs=out_specs,
        out_shape=out_shape,
        scratch_shapes=[pltpu.VMEM(((ka + kb) // N_CHUNK, N_CHUNK, D_MODEL), bf16),
                        pltpu.VMEM((n_w, D_MODEL, N_CHUNK), bf16),
                        pltpu.VMEM((n_w, D_MODEL, N_CHUNK), bf16), pltpu.VMEM((n_w, N_CHUNK, D_MODEL), bf16),
                        pltpu.VMEM((TM, D_MODEL), f32), pltpu.VMEM((TM, D_MODEL), bf16),
                        pltpu.VMEM((TM, D_FF), bf16), pltpu.VMEM((TM, D_MODEL), f32)],
        compiler_params=_params(1),
        name="post",
    )(*x_parts, *mix_a, *mix_b, mod, ln_g, ln_b, w_out, w_gate, w_up, w_down)


def _group_norm_gate(ro, rg, gmat, gn_g, gn_b):
    def gmean(parts):
        cols = []
        for c in range(0, RET_W, N_CHUNK):
            cols.append(sum(_dot(p[:, c:c + N_CHUNK], gmat) for p in parts))
        return jnp.concatenate(cols, -1)

    d = ro - gmean(_split_bf16(ro))
    var = gmean([(d * d).astype(bf16)])
    y = d * lax.rsqrt(var + LN_EPS) * gn_g + gn_b
    return _silu(rg.astype(f32)) * y


def _dup_head(x, j):
    first = _lane_half_mask(x.shape)
    keep = first if j == 0 else jnp.logical_not(first)
    xm = jnp.where(keep, x.astype(f32), 0.0)
    return xm + pltpu.roll(xm, HEAD_DIM, 1)


def _softmax_parts(scores, sink):
    m = sink
    for s in scores:
        m = jnp.maximum(m, jnp.max(s, -1, keepdims=True))
    es = [jnp.exp2(s - m) for s in scores]
    denom = jnp.exp2(sink - m)
    for e in es:
        denom = denom + jnp.sum(e, -1, keepdims=True)
    return es, denom


def _retention_tables(lg_ref, lgf_ref, lgb_ref, dmask_ref, kdec_ref, n):
    row = lax.broadcasted_iota(jnp.int32, (n, n), 0)
    col = lax.broadcasted_iota(jnp.int32, (n, n), 1)
    diff = (row - col).astype(f32)
    diag = jnp.where(row == col, 2.0 * QK_SCALE, QK_SCALE)
    for h in range(H_RET):
        dmask_ref[h] = jnp.exp(jnp.where(diff >= 0, lg_ref[0, h] * diff, -lg_ref[1, h] * diff)) * diag
    t = lax.broadcasted_iota(jnp.int32, (n, RET_W), 0).astype(f32)
    kdec_ref[0] = jnp.exp(lgf_ref[...] * (n - 1.0 - t)) * QK_SCALE
    kdec_ref[1] = jnp.exp(lgb_ref[...] * t) * QK_SCALE


def _retention_intra(pairs, q_of, k_of, v_of, dmask_ref):
    first = _lane_half_mask(k_of(pairs[0]).shape)
    masked = {}
    for p in pairs:
        kb = k_of(p)
        for e in range(2):
            keep = first if e == 0 else jnp.logical_not(first)
            s = _dot_nt(q_of(p), jnp.where(keep, kb, jnp.zeros_like(kb))) * dmask_ref[2 * p + e]
            masked[p, e] = s.astype(bf16)
    outs = {}
    for p in pairs:
        pv = [_dot(masked[p, e], v_of(p)) for e in range(2)]
        outs[p] = jnp.where(_lane_half_mask(pv[0].shape), pv[0], pv[1])
    return outs


def _window_group(subs, q_of, k_parts_of, v_parts_of, masks, sink_of):
    scores = {}
    for key in subs:
        parts = [_dot_nt(q_of(key), k) for k in k_parts_of(key)]
        scores[key] = [sc if mk is None else jnp.where(mk, sc, NEG_BIG) for sc, mk in zip(parts, masks)]
    probs = {}
    for key in subs:
        es, denom = _softmax_parts(scores[key], sink_of(key))
        probs[key] = ([ex.astype(bf16) for ex in es], denom)
    outs = {}
    for key in subs:
        es, denom = probs[key]
        pv = functools.reduce(lambda x, y: x + y, [_dot(ex, v) for ex, v in zip(es, v_parts_of(key))])
        outs[key] = pv / denom
    return outs


def _ctx_ab_kernel(lg_ref, sink_ref, rq_ref, rk_ref, rv_ref, rg_ref, wq_ref, wk_ref, wv_ref,
                   lgf_ref, lgb_ref, gmat_ref, gng_ref, gnb_ref,
                   ro_ref, wo_ref, st_ref, dmask_ref, kdec_ref, ret_ref):
    t_len = SEQ

    @pl.when(pl.program_id(0) == 0)
    def _():
        _retention_tables(lg_ref, lgf_ref, lgb_ref, dmask_ref, kdec_ref, t_len)

    first = _lane_half_mask((t_len, PAIR_W))
    for sq in range(CTX_SEQS):
        rows = slice(sq * t_len, (sq + 1) * t_len)
        psl = lambda p: slice(p * PAIR_W, (p + 1) * PAIR_W)
        for p0 in range(0, H_RET // 2, RET_GROUP):
            pairs = list(range(p0, p0 + RET_GROUP))
            intra = _retention_intra(pairs, lambda p: rq_ref[rows, psl(p)], lambda p: rk_ref[rows, psl(p)],
                                     lambda p: rv_ref[rows, psl(p)], dmask_ref)
            for p in pairs:
                ret_ref[rows, psl(p)] = intra[p]
        for p in range(H_RET // 2):
            sl = psl(p)
            kb = rk_ref[rows, sl]
            v = rv_ref[rows, sl]
            for d in range(2):
                kd_t = (kb * kdec_ref[d, :, sl]).T.astype(bf16)
                st = _dot(kd_t, v)
                st_ref[sq, d, 2 * p] = st[0:HEAD_DIM, 0:HEAD_DIM]
                st_ref[sq, d, 2 * p + 1] = pltpu.roll(st[HEAD_DIM:, :], HEAD_DIM, 1)[:, 0:HEAD_DIM]
        ro_ref[rows, :] = _group_norm_gate(ret_ref[rows, :], rg_ref[rows, :], gmat_ref[...], gng_ref[...],
                                           gnb_ref[...]).astype(bf16)

        k_dup = [_dup_head(wk_ref[rows, :], j).astype(bf16) for j in range(KV_WIN)]
        v_dup = [_dup_head(wv_ref[rows, :], j).astype(bf16) for j in range(KV_WIN)]

        def q_masked(key):
            qp, e = key
            qb = wq_ref[rows, qp * PAIR_W:(qp + 1) * PAIR_W]
            return jnp.where(first if e == 0 else jnp.logical_not(first), qb, jnp.zeros_like(qb))

        kv_of = lambda key: key[0] * 2 // G_WIN
        for g0 in range(0, H_WIN // 2, WIN_GROUP):
            subs = [(qp, e) for qp in range(g0, g0 + WIN_GROUP) for e in range(2)]
            outs = _window_group(subs, q_masked, lambda key: [k_dup[kv_of(key)]], lambda key: [v_dup[kv_of(key)]],
                                 [None], lambda key: sink_ref[0, 2 * key[0] + key[1]] * LOG2_E)
            for qp in range(g0, g0 + WIN_GROUP):
                wo_ref[rows, qp * PAIR_W:(qp + 1) * PAIR_W] = jnp.where(first, outs[qp, 0], outs[qp, 1]).astype(bf16)


def _ctx_ab(proj, log_gamma, sink, lgf_lanes, lgb_lanes, gmat, gn_g, gn_b):
    t = SEQ
    tb = CTX_SEQS * t
    smem = pl.BlockSpec(memory_space=pltpu.SMEM)
    const = lambda b: (0, 0)
    col = lambda c: (lambda b: (b, c))
    return pl.pallas_call(
        _ctx_ab_kernel,
        grid=(BATCH // CTX_SEQS,),
        in_specs=[smem, smem,
                  pl.BlockSpec((tb, RET_W), col(0)), pl.BlockSpec((tb, RET_W), col(1)),
                  pl.BlockSpec((tb, RET_W), col(2)), pl.BlockSpec((tb, RET_W), col(3)),
                  pl.BlockSpec((tb, WIN_W), col(4)),
                  pl.BlockSpec((tb, KV_W), col((4 * RET_W + WIN_W) // KV_W)),
                  pl.BlockSpec((tb, KV_W), col((4 * RET_W + WIN_W) // KV_W + 1)),
                  pl.BlockSpec((1, RET_W), const), pl.BlockSpec((1, RET_W), const),
                  pl.BlockSpec((N_CHUNK, N_CHUNK), const),
                  pl.BlockSpec((1, RET_W), const), pl.BlockSpec((1, RET_W), const)],
        out_specs=[pl.BlockSpec((tb, RET_W), lambda b: (b, 0)),
                   pl.BlockSpec((tb, WIN_W), lambda b: (b, 0)),
                   pl.BlockSpec((CTX_SEQS, 2, H_RET, HEAD_DIM, HEAD_DIM), lambda b: (b, 0, 0, 0, 0))],
        out_shape=[jax.ShapeDtypeStruct((BATCH * t, RET_W), bf16),
                   jax.ShapeDtypeStruct((BATCH * t, WIN_W), bf16),
                   jax.ShapeDtypeStruct((BATCH, 2, H_RET, HEAD_DIM, HEAD_DIM), f32)],
        scratch_shapes=[pltpu.VMEM((H_RET, t, t), f32), pltpu.VMEM((2, t, RET_W), f32),
                        pltpu.VMEM((tb, RET_W), f32)],
        compiler_params=_params(1),
        name="ctx_ab",
    )(log_gamma, sink, proj, proj, proj, proj, proj, proj, proj, lgf_lanes, lgb_lanes, gmat, gn_g, gn_b)


def _pair_state(s0_ref, d, p):
    zero = jnp.zeros((HEAD_DIM, HEAD_DIM), f32)
    top = jnp.concatenate([s0_ref[0, 0, d, 2 * p], zero], 1)
    bottom = jnp.concatenate([zero, s0_ref[0, 0, d, 2 * p + 1]], 1)
    return jnp.concatenate([top, bottom], 0)


def _lat_ab_kernel(lg_ref, sink_ref, rq_ref, rk_ref, rv_ref, rg_ref, wq_ref, wk_ref, wv_ref, ck_ref, cv_ref,
                   s0_ref, lgf_ref, lgb_ref, gmat_ref, gng_ref, gnb_ref,
                   ro_ref, wo_ref, ret_ref, dmask_ref, kdec_ref, qdec_ref, sf_ref, sb_ref):
    t_len = DEC_SEQ
    n_chunks = t_len // TQ
    step = pl.program_id(1)
    first = _lane_half_mask((TQ, PAIR_W))

    @pl.when(jnp.logical_and(pl.program_id(0) == 0, step == 0))
    def _():
        _retention_tables(lg_ref, lgf_ref, lgb_ref, dmask_ref, kdec_ref, TQ)
        t = lax.broadcasted_iota(jnp.int32, (TQ, RET_W), 0).astype(f32)
        qdec_ref[0] = jnp.exp(lgf_ref[...] * (t + 1.0))
        qdec_ref[1] = jnp.exp(lgb_ref[...] * (TQ - t))

    @pl.when(step == 0)
    def _():
        r = lax.broadcasted_iota(jnp.int32, (PAIR_W, PAIR_W), 0)
        c_ = lax.broadcasted_iota(jnp.int32, (PAIR_W, PAIR_W), 1)
        same_head = (r < HEAD_DIM) == (c_ < HEAD_DIM)
        for p in range(H_RET // 2):
            sl = slice(p * PAIR_W, (p + 1) * PAIR_W)
            kv = []
            for c in range(n_chunks):
                rows = slice(c * TQ, (c + 1) * TQ)
                kc = rk_ref[rows, sl]
                vc = rv_ref[rows, sl]
                kv.append([jnp.where(same_head, _dot((kc * kdec_ref[d, :, sl]).T.astype(bf16), vc), 0.0)
                           for d in range(2)])
            state = _pair_state(s0_ref, 0, p)
            for c in range(n_chunks):
                sf_ref[c, p] = state
                state = state * jnp.exp(lgf_ref[:, sl] * TQ) + kv[c][0]
            state = _pair_state(s0_ref, 1, p)
            for c in reversed(range(n_chunks)):
                sb_ref[c, p] = state
                state = state * jnp.exp(lgb_ref[:, sl] * TQ) + kv[c][1]

    for sub in range(LAT_SUB):
        chunk = step * LAT_SUB + sub
        q0 = pl.multiple_of(chunk * TQ, TQ)
        rows = slice(sub * TQ, (sub + 1) * TQ)
        psl = lambda p: slice(p * PAIR_W, (p + 1) * PAIR_W)
        intra = {}
        for p0 in range(0, H_RET // 2, RET_GROUP):
            intra.update(_retention_intra(list(range(p0, p0 + RET_GROUP)), lambda p: rq_ref[rows, psl(p)],
                                          lambda p: rk_ref[pl.ds(q0, TQ), psl(p)],
                                          lambda p: rv_ref[pl.ds(q0, TQ), psl(p)], dmask_ref))
        for p in range(H_RET // 2):
            sl = psl(p)
            q = rq_ref[rows, sl]
            o = intra[p]
            o = o + _dot(q, sf_ref[chunk, p].astype(bf16)) * qdec_ref[0, :, sl]
            o = o + _dot(q, sb_ref[chunk, p].astype(bf16)) * qdec_ref[1, :, sl]
            ret_ref[rows, sl] = o
        ro_ref[rows, :] = _group_norm_gate(ret_ref[rows, :], rg_ref[rows, :], gmat_ref[...], gng_ref[...],
                                           gnb_ref[...]).astype(bf16)

        band = TQ + 2 * WINDOW
        k_start = pl.multiple_of(jnp.clip(q0 - WINDOW, 0, t_len - band), LANES)
        qi = q0 + lax.broadcasted_iota(jnp.int32, (TQ, band), 0)
        kj = k_start + lax.broadcasted_iota(jnp.int32, (TQ, band), 1)
        in_band = jnp.abs(qi - kj) <= WINDOW
        k_parts = [[_dup_head(wk_ref[pl.ds(k_start, band), :], j).astype(bf16), _dup_head(ck_ref[0], j).astype(bf16)]
                   for j in range(KV_WIN)]
        v_parts = [[_dup_head(wv_ref[pl.ds(k_start, band), :], j).astype(bf16), _dup_head(cv_ref[0], j).astype(bf16)]
                   for j in range(KV_WIN)]

        def q_masked(key):
            qp, e = key
            qb = wq_ref[rows, qp * PAIR_W:(qp + 1) * PAIR_W]
            return jnp.where(first if e == 0 else jnp.logical_not(first), qb, jnp.zeros_like(qb))

        kv_of = lambda key: key[0] * 2 // G_WIN
        for g0 in range(0, H_WIN // 2, WIN_GROUP_LAT):
            subs = [(qp, e) for qp in range(g0, g0 + WIN_GROUP_LAT) for e in range(2)]
            outs = _window_group(subs, q_masked, lambda key: k_parts[kv_of(key)], lambda key: v_parts[kv_of(key)],
                                 [in_band, None], lambda key: sink_ref[0, 2 * key[0] + key[1]] * LOG2_E)
            for qp in range(g0, g0 + WIN_GROUP_LAT):
                wo_ref[rows, qp * PAIR_W:(qp + 1) * PAIR_W] = jnp.where(first, outs[qp, 0], outs[qp, 1]).astype(bf16)


def _lat_ab(proj, log_gamma, sink, ck, cv, state, layer, lgf_lanes, lgb_lanes, gmat, gn_g, gn_b):
    t = DEC_SEQ
    tb = LAT_SUB * TQ
    nq = t // tb
    smem = pl.BlockSpec(memory_space=pltpu.SMEM)
    const = lambda b, i: (0, 0)
    qcol = lambda c: (lambda b, i: (N_CTX // tb + b * nq + i, c))
    bcol = lambda c: (lambda b, i: (N_CTX // t + b, c))
    kv_col = (4 * RET_W + WIN_W) // KV_W
    return pl.pallas_call(
        _lat_ab_kernel,
        grid=(DEC_BATCH, nq),
        in_specs=[smem, smem,
                  pl.BlockSpec((tb, RET_W), qcol(0)), pl.BlockSpec((t, RET_W), bcol(1)),
                  pl.BlockSpec((t, RET_W), bcol(2)), pl.BlockSpec((tb, RET_W), qcol(3)),
                  pl.BlockSpec((tb, WIN_W), qcol(4)),
                  pl.BlockSpec((t, KV_W), bcol(kv_col)), pl.BlockSpec((t, KV_W), bcol(kv_col + 1)),
                  pl.BlockSpec((1, PAST_LEN, KV_W), lambda b, i: (b, 0, 0)),
                  pl.BlockSpec((1, PAST_LEN, KV_W), lambda b, i: (b, 0, 0)),
                  pl.BlockSpec((1, 1, 2, H_RET, HEAD_DIM, HEAD_DIM), lambda b, i: (b, layer, 0, 0, 0, 0)),
                  pl.BlockSpec((1, RET_W), const), pl.BlockSpec((1, RET_W), const),
                  pl.BlockSpec((N_CHUNK, N_CHUNK), const),
                  pl.BlockSpec((1, RET_W), const), pl.BlockSpec((1, RET_W), const)],
        out_specs=[pl.BlockSpec((tb, RET_W), lambda b, i: (b * nq + i, 0)),
                   pl.BlockSpec((tb, WIN_W), lambda b, i: (b * nq + i, 0))],
        out_shape=[jax.ShapeDtypeStruct((DEC_BATCH * t, RET_W), bf16),
                   jax.ShapeDtypeStruct((DEC_BATCH * t, WIN_W), bf16)],
        scratch_shapes=[pltpu.VMEM((tb, RET_W), f32), pltpu.VMEM((H_RET, TQ, TQ), f32),
                        pltpu.VMEM((2, TQ, RET_W), f32), pltpu.VMEM((2, TQ, RET_W), f32),
                        pltpu.VMEM((t // TQ, H_RET // 2, PAIR_W, PAIR_W), f32),
                        pltpu.VMEM((t // TQ, H_RET // 2, PAIR_W, PAIR_W), f32)],
        compiler_params=_params(2),
        name="lat_ab",
    )(log_gamma, sink, proj, proj, proj, proj, proj, proj, proj, ck, cv, state,
      lgf_lanes, lgb_lanes, gmat, gn_g, gn_b)


def _lambda_full(lam_ref, lam_init):
    lam = lam_ref[...]
    a = jnp.sum(lam[0:1, :] * lam[1:2, :], -1, keepdims=True)
    b = jnp.sum(lam[2:3, :] * lam[3:4, :], -1, keepdims=True)
    return jnp.exp(a) - jnp.exp(b) + lam_init


def _diff_heads(q_of, k_parts_of, v_parts_of, lam, subln, lam_init, group):
    res = []
    for h0 in range(0, H_DIFF, group):
        res += _diff_head_group(range(h0, h0 + group), q_of, k_parts_of, v_parts_of, lam, subln, lam_init)
    return res


def _diff_head_group(heads, q_of, k_parts_of, v_parts_of, lam, subln, lam_init):
    subs = [(h, e) for h in heads for e in range(2)]
    scores = {}
    for h, e in subs:
        q = q_of(h)
        fm = _lane_half_mask(q.shape)
        q_sub = jnp.where(fm if e == 0 else jnp.logical_not(fm), q, jnp.zeros_like(q))
        scores[h, e] = [_dot(q_sub, k) if transposed else _dot_nt(q_sub, k) for k, transposed in k_parts_of(h)]
    probs = {}
    for key in subs:
        m = scores[key][0].max(-1, keepdims=True)
        for sc in scores[key][1:]:
            m = jnp.maximum(m, sc.max(-1, keepdims=True))
        es = [jnp.exp2(sc - m) for sc in scores[key]]
        denom = es[0].sum(-1, keepdims=True)
        for ex in es[1:]:
            denom = denom + ex.sum(-1, keepdims=True)
        probs[key] = ([ex.astype(bf16) for ex in es], denom)
    outs = {}
    for h, e in subs:
        es, denom = probs[h, e]
        pv = functools.reduce(lambda x, y: x + y, [_dot(ex, v) for v, ex in zip(v_parts_of(h), es)])
        outs[h, e] = pv / denom
    res = []
    for h in heads:
        a = outs[h, 0] - lam * outs[h, 1]
        res.append(a * lax.rsqrt(jnp.mean(a * a, -1, keepdims=True) + LN_EPS) * subln * (1.0 - lam_init))
    return res


def _fourier_rows(ct_ref, st_ref, z, bdc_ref, bds_ref):
    zc = _dot(z, bdc_ref[...].astype(bf16)).astype(bf16)
    zs = _dot(z, bds_ref[...].astype(bf16)).astype(bf16)
    return _dot(ct_ref[...].astype(bf16), zc) - _dot(st_ref[...].astype(bf16), zs)


def _ctx_cd_kernel(q_ref, k_ref, v_ref, z_ref, lam_ref, subln_ref, ct_ref, st_ref, bdc_ref, bds_ref,
                   a_ref, zf_ref, *, lam_init):
    lam = _lambda_full(lam_ref, lam_init)
    for sq in range(CTX_SEQS_CD):
        rows = slice(sq * SEQ, (sq + 1) * SEQ)
        sl = lambda h: slice(h * PAIR_W, (h + 1) * PAIR_W)
        heads = _diff_heads(lambda h: q_ref[rows, sl(h)], lambda h: [(k_ref[rows, sl(h)], False)],
                            lambda h: [v_ref[rows, sl(h)]], lam, subln_ref[...], lam_init, DIFF_GROUP)
        for h in range(H_DIFF):
            a_ref[rows, sl(h)] = heads[h].astype(bf16)
        zf_ref[rows, :] = _fourier_rows(ct_ref, st_ref, z_ref[rows, :], bdc_ref, bds_ref).astype(bf16)


def _ctx_cd(proj, lam, subln, ct, st, bdc, bds, lam_init):
    t = SEQ
    tb = CTX_SEQS_CD * t
    const = lambda b: (0, 0)
    col = lambda c: (lambda b: (b, c))
    return pl.pallas_call(
        functools.partial(_ctx_cd_kernel, lam_init=lam_init),
        grid=(BATCH // CTX_SEQS_CD,),
        in_specs=[pl.BlockSpec((tb, DIFF_W), col(0)), pl.BlockSpec((tb, DIFF_W), col(1)),
                  pl.BlockSpec((tb, DIFF_W), col(2)), pl.BlockSpec((tb, FNET_W), col(3 * DIFF_W // FNET_W)),
                  pl.BlockSpec((4, HEAD_DIM), const), pl.BlockSpec((1, PAIR_W), const),
                  pl.BlockSpec((t, t), const), pl.BlockSpec((t, t), const),
                  pl.BlockSpec((FNET_W, FNET_W), const), pl.BlockSpec((FNET_W, FNET_W), const)],
        out_specs=[pl.BlockSpec((tb, DIFF_W), lambda b: (b, 0)), pl.BlockSpec((tb, FNET_W), lambda b: (b, 0))],
        out_shape=[jax.ShapeDtypeStruct((BATCH * t, DIFF_W), bf16),
                   jax.ShapeDtypeStruct((BATCH * t, FNET_W), bf16)],
        compiler_params=_params(1),
        name="ctx_cd",
    )(proj, proj, proj, proj, lam, subln, ct, st, bdc, bds)


def _lat_cd_kernel(q_ref, k_ref, v_ref, z_ref, ckt_ref, cv_ref, lam_ref, subln_ref, ct_ref, st_ref, bdc_ref, bds_ref,
                   a_ref, zf_ref, *, lam_init):
    lam = _lambda_full(lam_ref, lam_init)
    sl = lambda h: slice(h * PAIR_W, (h + 1) * PAIR_W)
    heads = _diff_heads(lambda h: q_ref[:, sl(h)],
                        lambda h: [(k_ref[:, sl(h)], False), (ckt_ref[0, h].astype(bf16), True)],
                        lambda h: [v_ref[:, sl(h)], cv_ref[0, h].astype(bf16)], lam, subln_ref[...], lam_init,
                        DIFF_GROUP_LAT)
    for h in range(H_DIFF):
        a_ref[:, sl(h)] = heads[h].astype(bf16)
    zf_ref[...] = _fourier_rows(ct_ref, st_ref, z_ref[...], bdc_ref, bds_ref).astype(bf16)


def _lat_cd(proj, ck, cv, lam, subln, ct, st, bdc, bds, lam_init):
    t = DEC_SEQ
    nq = t // TQ_CD
    const = lambda b, i: (0, 0)
    return pl.pallas_call(
        functools.partial(_lat_cd_kernel, lam_init=lam_init),
        grid=(DEC_BATCH, nq),
        in_specs=[pl.BlockSpec((TQ_CD, DIFF_W), lambda b, i: (N_CTX // TQ_CD + b * nq + i, 0)),
                  pl.BlockSpec((t, DIFF_W), lambda b, i: (N_CTX // t + b, 1)),
                  pl.BlockSpec((t, DIFF_W), lambda b, i: (N_CTX // t + b, 2)),
                  pl.BlockSpec((t, FNET_W), lambda b, i: (N_CTX // t + b, 3 * DIFF_W // FNET_W)),
                  pl.BlockSpec((1, H_DIFF, PAIR_W, PAST_LEN), lambda b, i: (b, 0, 0, 0)),
                  pl.BlockSpec((1, H_DIFF, PAST_LEN, PAIR_W), lambda b, i: (b, 0, 0, 0)),
                  pl.BlockSpec((4, HEAD_DIM), const), pl.BlockSpec((1, PAIR_W), const),
                  pl.BlockSpec((TQ_CD, t), lambda b, i: (i, 0)), pl.BlockSpec((TQ_CD, t), lambda b, i: (i, 0)),
                  pl.BlockSpec((FNET_W, FNET_W), const), pl.BlockSpec((FNET_W, FNET_W), const)],
        out_specs=[pl.BlockSpec((TQ_CD, DIFF_W), lambda b, i: (b * nq + i, 0)),
                   pl.BlockSpec((TQ_CD, FNET_W), lambda b, i: (b * nq + i, 0))],
        out_shape=[jax.ShapeDtypeStruct((DEC_BATCH * t, DIFF_W), bf16),
                   jax.ShapeDtypeStruct((DEC_BATCH * t, FNET_W), bf16)],
        compiler_params=_params(2),
        name="lat_cd",
    )(proj, proj, proj, proj, ck, cv, lam, subln, ct, st, bdc, bds)


def _rope_tables():
    t = np.arange(DEC_SEQ)
    quarter = HEAD_DIM // 4
    inv = ROPE_BASE ** (-np.arange(quarter, dtype=np.float64) / quarter)
    ang = np.concatenate([(t // GRID_W)[:, None] * inv, (t % GRID_W)[:, None] * inv], -1)
    cos, sin = np.cos(ang), np.sin(ang)
    reps = LANES // HEAD_DIM
    return (np.tile(np.concatenate([cos, cos], -1), (1, reps)).astype(np.float32),
            np.tile(np.concatenate([-sin, sin], -1), (1, reps)).astype(np.float32))


def _dft_tables(n):
    k = np.arange(n)
    ang = (2.0 * math.pi / n) * ((k[:, None] * k[None, :]) % n)
    return (np.cos(ang) / math.sqrt(n)).astype(np.float32), (np.sin(ang) / math.sqrt(n)).astype(np.float32)


def _block_diag(m, reps):
    return np.kron(np.eye(reps, dtype=m.dtype), m)


def kernel(x_prompt, x_sample, state_ret, cache_win_k, cache_win_v, cache_diff_k, cache_diff_v, c, c_ctx, w_mod, b_mod, ln_g, ln_b, w_in_ab, w_out_ab, ret_log_gamma, ret_gn_g, ret_gn_b, win_sink, w_in_cd, w_out_cd, diff_lambda, diff_subln_g, w_gate, w_up, w_down):
    cond = jnp.concatenate([c_ctx[None, :], c, jnp.zeros((SUBLANES - 1 - DEC_BATCH, D_MODEL), f32)], 0)
    mod = _modulation(cond, w_mod, b_mod).reshape(DEPTH, SUBLANES, 6, D_MODEL)

    rope_tabs = _rope_tables()
    gmat = jnp.asarray(_block_diag(np.full((HEAD_DIM, HEAD_DIM), 1.0 / HEAD_DIM, np.float32),
                                   N_CHUNK // HEAD_DIM), bf16)
    c64, s64 = _dft_tables(FNET_DIM)
    bdc = _block_diag(c64, FNET_GROUPS)
    bds = _block_diag(s64, FNET_GROUPS)
    dft_ctx = _dft_tables(SEQ)
    dft_lat = _dft_tables(DEC_SEQ)

    x_parts = [x_prompt.reshape(N_CTX, D_MODEL), x_sample.reshape(N_LAT, D_MODEL)]
    outs = {}
    for l in range(DEPTH):
        i = l // 2
        if l % 2 == 0:
            lgf = jnp.repeat(ret_log_gamma[i, 0], HEAD_DIM)[None, :]
            lgb = jnp.repeat(ret_log_gamma[i, 1], HEAD_DIM)[None, :]
            gn_g = ret_gn_g[i][None, :]
            gn_b = ret_gn_b[i][None, :]
            sink = win_sink[i][None, :]
            rope_tiles = tuple(range(0, 2 * RET_W // LANES)) + tuple(
                range(4 * RET_W // LANES, (4 * RET_W + WIN_W + KV_W) // LANES))
            kv_tile = (4 * RET_W + WIN_W) // LANES
            kv_shape = (BATCH, 1, KV_WIN, HEAD_DIM, SEQ)
            scale_tiles = tuple(range(4 * RET_W // LANES, (4 * RET_W + WIN_W) // LANES))
            proj, wk_t, wv_t = _proj(x_parts, mod, l, w_in_ab, i, scale_tiles, rope_tabs, rope_tiles,
                                     (kv_shape, kv_shape),
                                     {kv_tile: ("heads", 0, 0), kv_tile + 1: ("heads", 1, 0)})
            ro_c, wo_c, st_c = _ctx_ab(proj, ret_log_gamma[i], sink, lgf, lgb, gmat, gn_g, gn_b)
            ck = cache_win_k[:, i].reshape(DEC_BATCH, PAST_LEN, KV_W)
            cv = cache_win_v[:, i].reshape(DEC_BATCH, PAST_LEN, KV_W)
            ro_l, wo_l = _lat_ab(proj, ret_log_gamma[i], sink, ck, cv, state_ret, i, lgf, lgb, gmat, gn_g, gn_b)
            mix_a, mix_b, w_out = (ro_c, ro_l), (wo_c, wo_l), w_out_ab
            outs.setdefault('state', []).append(st_c[:, None])
            outs.setdefault('win_k', []).append(jnp.transpose(wk_t, (0, 1, 4, 2, 3)))
            outs.setdefault('win_v', []).append(jnp.transpose(wv_t, (0, 1, 4, 2, 3)))
        else:
            lam_init = 0.8 - 0.6 * math.exp(-0.3 * l)
            subln = diff_subln_g[i][None, :]
            rope_tiles = tuple(range(0, 2 * DIFF_W // LANES))
            plan = {}
            for h in range(H_DIFF):
                plan[DIFF_W // LANES + h] = ("pairs", 0, h)
                plan[2 * DIFF_W // LANES + h] = ("plain", 1, h)
            scale_tiles = tuple(range(0, DIFF_W // LANES))
            proj, dk_t, dv_h = _proj(
                x_parts, mod, l, w_in_cd, i, scale_tiles, rope_tabs, rope_tiles,
                ((BATCH, 1, H_DIFF, 2, HEAD_DIM, SEQ), (BATCH, 1, H_DIFF, SEQ, 2 * HEAD_DIM)), plan)
            a_c, z_c = _ctx_cd(proj, diff_lambda[i], subln, dft_ctx[0], dft_ctx[1], bdc, bds, lam_init)
            ck = jnp.transpose(cache_diff_k[:, i], (0, 2, 3, 4, 1)).reshape(DEC_BATCH, H_DIFF, PAIR_W, PAST_LEN)
            cv = jnp.transpose(cache_diff_v[:, i], (0, 2, 1, 3))
            a_l, z_l = _lat_cd(proj, ck, cv, diff_lambda[i], subln, dft_lat[0], dft_lat[1], bdc, bds, lam_init)
            mix_a, mix_b, w_out = (a_c, a_l), (z_c, z_l), w_out_cd
            outs.setdefault('diff_k', []).append(jnp.transpose(dk_t, (0, 1, 5, 2, 3, 4)))
            outs.setdefault('diff_v', []).append(jnp.transpose(dv_h, (0, 1, 3, 2, 4)))
        x_parts = _post(x_parts, mix_a, mix_b, mod, ln_g, ln_b, w_out, w_gate, w_up, w_down, l, i,
                        split_out=(l == DEPTH - 1))

    y_prompt = x_parts[0].reshape(BATCH, SEQ, D_MODEL)
    y_sample = x_parts[1].reshape(DEC_BATCH, DEC_SEQ, D_MODEL)
    cat = lambda parts: parts[0] if len(parts) == 1 else jnp.concatenate(parts, 1)
    return (y_prompt, y_sample, cat(outs['state']), cat(outs['win_k']), cat(outs['win_v']),
            cat(outs['diff_k']), cat(outs['diff_v']))
```

```python
import functools
import math

import jax
import jax.numpy as jnp
import numpy as np
from jax import lax
from jax.experimental import pallas as pl
from jax.experimental.pallas import tpu as pltpu

D_MODEL = 1024
BATCH = 32
SEQ = 256
DEPTH = 2
DEC_BATCH = 2
DEC_SEQ = 1024
PAST_LEN = 512
GRID_W = 64
HEAD_DIM = 64
ROPE_BASE = 10000.0
H_RET = 8
H_WIN = 8
KV_WIN = 2
G_WIN = H_WIN // KV_WIN
WINDOW = 128
H_DIFF = 6
FNET_GROUPS = 4
FNET_DIM = 64
D_FF = 256 * math.ceil(8 * D_MODEL / 3 / 256)
RET_W = H_RET * HEAD_DIM
WIN_W = H_WIN * HEAD_DIM
KV_W = KV_WIN * HEAD_DIM
AB_IN = 4 * RET_W + WIN_W + 2 * KV_W
DIFF_W = H_DIFF * 2 * HEAD_DIM
FNET_W = FNET_GROUPS * FNET_DIM
CD_IN = 3 * DIFF_W + FNET_W
ALPHA = (2 * DEPTH) ** 0.25
LN_EPS = 1e-5
QK_SCALE = HEAD_DIM ** -0.5
LOG2_E = math.log2(math.e)

N_CTX = BATCH * SEQ
N_LAT = DEC_BATCH * DEC_SEQ
N_TOK = N_CTX + N_LAT

LANES = 128
SUBLANES = 8
PAIR_W = 2 * HEAD_DIM
TM = 512
TM_PROJ = 1024
CTX_BLOCKS = N_CTX // TM
TOK_BLOCKS = N_TOK // TM
ROW_GROUPS = 2
FFN_SKEW = 2
TQ = 256
LAT_SUB = 4
TQ_CD = 512
CTX_SEQS = 4
CTX_SEQS_CD = 4
RET_GROUP = 4
WIN_GROUP = 4
WIN_GROUP_LAT = 4
DIFF_GROUP_LAT = 1
DIFF_GROUP = 3
N_CHUNK = 256
MOD_TN = 1536
NEG_BIG = -1e30
VMEM_LIMIT = 56 * 1024 * 1024

f32 = jnp.float32
bf16 = jnp.bfloat16


def _params(n_axes):
    return pltpu.CompilerParams(dimension_semantics=("arbitrary",) * n_axes,
                                vmem_limit_bytes=VMEM_LIMIT)


def _dot(a, b):
    return jnp.dot(a, b, preferred_element_type=f32)


def _dot_nt(a, b):
    return lax.dot_general(a, b, (((1,), (1,)), ((), ())), preferred_element_type=f32)


def _ln(x):
    mu = jnp.mean(x, -1, keepdims=True)
    d = x - mu
    var = jnp.mean(d * d, -1, keepdims=True)
    return d * lax.rsqrt(var + LN_EPS)


def _silu(x):
    return x * jax.nn.sigmoid(x)


def _split_bf16(x):
    hi = x.astype(bf16)
    lo = (x - hi.astype(f32)).astype(bf16)
    return hi, lo


def _lane_half_mask(shape):
    return (lax.broadcasted_iota(jnp.int32, shape, len(shape) - 1) & HEAD_DIM) == 0


def _mod_kernel(c_ref, w_ref, b_ref, o_ref):
    layer = pl.program_id(0)
    a = _silu(c_ref[...])
    rows = a.shape[0]
    a_hi, a_lo = _split_bf16(a)
    w_hi, w_lo = _split_bf16(w_ref[0])
    both = _dot(jnp.concatenate([a_hi, a_lo], 0), w_hi)
    o_ref[0] = both[:rows] + both[rows:] + _dot(a_hi, w_lo) + b_ref[pl.ds(layer, 1), :]


def _modulation(cond, w_mod, b_mod):
    tn = MOD_TN
    rows = cond.shape[0]
    return pl.pallas_call(
        _mod_kernel,
        grid=(DEPTH, 6 * D_MODEL // tn),
        in_specs=[pl.BlockSpec((rows, D_MODEL), lambda l, j: (0, 0)),
                  pl.BlockSpec((1, D_MODEL, tn), lambda l, j: (l, 0, j)),
                  pl.BlockSpec((DEPTH, tn), lambda l, j: (0, j))],
        out_specs=pl.BlockSpec((1, rows, tn), lambda l, j: (l, 0, j)),
        out_shape=jax.ShapeDtypeStruct((DEPTH, rows, 6 * D_MODEL), f32),
        compiler_params=_params(2),
        name="modulation",
    )(cond, w_mod, b_mod)


def _tok(i, n_w):
    return jnp.maximum(i - n_w, 0)


def _ctx_blk(t, tm=TM):
    return jnp.minimum(t, N_CTX // tm - 1)


def _lat_blk(t, tm=TM):
    return jnp.maximum(t - N_CTX // tm, 0)


def _mod_row(t, tm=TM):
    return jnp.where(t < N_CTX // tm, 0, 1 + _lat_blk(t, tm) * tm // DEC_SEQ)


def _token_specs(parts, n_w, tm=TM):
    width = parts[0].shape[1]
    if len(parts) == 1:
        return [pl.BlockSpec((tm, width), lambda i: (_tok(i, n_w), 0))]
    return [pl.BlockSpec((tm, width), lambda i: (_ctx_blk(_tok(i, n_w), tm), 0)),
            pl.BlockSpec((tm, width), lambda i: (_lat_blk(_tok(i, n_w), tm), 0))]


def _pick(refs, is_ctx, rs):
    return refs[0 if (len(refs) == 1 or is_ctx) else 1][rs, :]


def _rope_pair(y, cos, sin_signed):
    first_half = (lax.broadcasted_iota(jnp.int32, y.shape, 1) & (HEAD_DIM // 2)) == 0
    swapped = jnp.where(first_half, pltpu.roll(y, LANES - HEAD_DIM // 2, 1), pltpu.roll(y, HEAD_DIM // 2, 1))
    return y * cos + swapped * sin_signed


def _proj_kernel(*refs, n_x, n_cache, n_w, rope_tiles, scale_tiles, cache_plan):
    x_refs = refs[:n_x]
    mod_ref, w_ref, cos_ref, sin_ref, o_ref = refs[n_x:n_x + 5]
    cache_refs = refs[n_x + 5:n_x + 5 + n_cache]
    wbf_ref, u_ref = refs[n_x + 5 + n_cache:]
    i = pl.program_id(0)

    @pl.when(i == 0)
    def _():
        for c in range(n_w):
            wbf_ref[c] = w_ref[0, :, c * N_CHUNK:(c + 1) * N_CHUNK].astype(bf16)

    def tokens(is_ctx):
        x_ref = x_refs[0] if is_ctx else x_refs[-1]
        shift = mod_ref[0, 0:1, :]
        scale = mod_ref[0, 1:2, :]
        groups = [slice(b * SEQ, (b + 1) * SEQ) for b in range(TM_PROJ // SEQ)]
        for rs in groups:
            u_ref[rs, :] = (_ln(x_ref[rs, :]) * (1.0 + scale) + shift).astype(bf16)
        for c in range(n_w):
            y_all = _dot(u_ref[...], wbf_ref[c])
            for b, rs in enumerate(groups):
                y = y_all[rs, :]
                for t in range(N_CHUNK // LANES):
                    tile = c * (N_CHUNK // LANES) + t
                    piece = y[:, t * LANES:(t + 1) * LANES]
                    if tile in rope_tiles and not is_ctx:
                        piece = _rope_pair(piece, cos_ref[rs, :], sin_ref[rs, :])
                    if tile in scale_tiles:
                        piece = piece * (QK_SCALE * LOG2_E)
                    o_ref[rs, tile * LANES:(tile + 1) * LANES] = piece.astype(o_ref.dtype)
                    if tile in cache_plan and is_ctx:
                        kind, out_idx, slot = cache_plan[tile]
                        c_ref = cache_refs[out_idx]
                        if kind == "plain":
                            c_ref[b, 0, slot] = piece
                        else:
                            piece_t = piece.T
                            if kind == "heads":
                                c_ref[b, 0, 0] = piece_t[0:HEAD_DIM]
                                c_ref[b, 0, 1] = piece_t[HEAD_DIM:]
                            else:
                                c_ref[b, 0, slot, 0] = piece_t[0:HEAD_DIM]
                                c_ref[b, 0, slot, 1] = piece_t[HEAD_DIM:]

    t = i - 1

    @pl.when(jnp.logical_and(t >= 0, t < N_CTX // TM_PROJ))
    def _():
        tokens(True)

    @pl.when(t >= N_CTX // TM_PROJ)
    def _():
        tokens(False)


def _proj(x_parts, mod, mod_layer, w_all, layer, scale_tiles, rope_tabs, rope_tiles, cache_shapes, cache_plan):
    n_out = w_all.shape[2]
    n_w = n_out // N_CHUNK
    tm = TM_PROJ
    nb = DEC_SEQ // tm
    tok = lambda i: _tok(i, 1)
    in_specs = _token_specs(x_parts, 1, tm) + [
        pl.BlockSpec((None, 1, 6, D_MODEL), lambda i: (mod_layer, _mod_row(tok(i), tm), 0, 0)),
        pl.BlockSpec((1, D_MODEL, n_out), lambda i: (layer, 0, 0), pipeline_mode=pl.Buffered(1)),
        pl.BlockSpec((tm, LANES), lambda i: (_lat_blk(tok(i), tm) % nb, 0)),
        pl.BlockSpec((tm, LANES), lambda i: (_lat_blk(tok(i), tm) % nb, 0))]
    out_specs = [pl.BlockSpec((tm, n_out), lambda i: (tok(i), 0))]
    out_shape = [jax.ShapeDtypeStruct((N_TOK, n_out), bf16)]
    for shp in cache_shapes:
        blk = (tm // SEQ,) + tuple(shp[1:])
        out_specs.append(pl.BlockSpec(blk, lambda i, nd=len(shp): (_ctx_blk(tok(i), tm),) + (0,) * (nd - 1)))
        out_shape.append(jax.ShapeDtypeStruct(tuple(shp), f32))
    return pl.pallas_call(
        functools.partial(_proj_kernel, n_x=len(x_parts), n_cache=len(cache_shapes), n_w=n_w,
                          rope_tiles=frozenset(rope_tiles), scale_tiles=frozenset(scale_tiles),
                          cache_plan=dict(cache_plan)),
        grid=(1 + N_TOK // tm,),
        in_specs=in_specs,
        out_specs=out_specs,
        out_shape=out_shape,
        scratch_shapes=[pltpu.VMEM((n_w, D_MODEL, N_CHUNK), bf16), pltpu.VMEM((tm, D_MODEL), bf16)],
        compiler_params=_params(1),
        name="proj",
    )(*x_parts, mod, w_all, *rope_tabs)


def _post_kernel(*refs, n_x, n_y, ka, kb, n_w):
    x_refs = refs[:n_x]
    (ac_ref, al_ref, bc_ref, bl_ref, mod_ref, lng_ref, lnb_ref,
     wo_ref, wg_ref, wu_ref, wd_ref) = refs[n_x:n_x + 11]
    y_refs = refs[n_x + 11:n_x + 11 + n_y]
    wo_s, wg_s, wu_s, wd_s, x1_ref, u_ref, h_ref, y_ref = refs[n_x + 11 + n_y:]
    i = pl.program_id(0)
    lead = n_w - 1
    gate1 = mod_ref[0, 2:3, :]
    shift2 = mod_ref[0, 3:4, :]
    scale2 = mod_ref[0, 4:5, :]
    gate2 = mod_ref[0, 5:6, :]
    groups = [slice(r * TM // ROW_GROUPS, (r + 1) * TM // ROW_GROUPS) for r in range(ROW_GROUPS)]

    def mix_in(rs, is_ctx):
        a = _pick((ac_ref, al_ref), is_ctx, rs)
        b = _pick((bc_ref, bl_ref), is_ctx, rs)
        pieces = ([a[:, c:c + N_CHUNK] for c in range(0, ka, N_CHUNK)]
                  + [b[:, c:c + N_CHUNK] for c in range(0, kb, N_CHUNK)])
        h = functools.reduce(lambda s, p: s + p, [_dot(p, wo_s[c]) for c, p in enumerate(pieces)])
        x1 = _ln(ALPHA * _pick(x_refs, is_ctx, rs) + gate1 * h) * lng_ref[0, 0:1, :] + lnb_ref[0, 0:1, :]
        x1_ref[rs, :] = x1
        u_ref[rs, :] = (_ln(x1) * (1.0 + scale2) + shift2).astype(bf16)

    def finish(rs, ffn, is_ctx):
        y = _ln(ALPHA * x1_ref[rs, :] + gate2 * ffn) * lng_ref[0, 1:2, :] + lnb_ref[0, 1:2, :]
        y_refs[0 if (n_y == 1 or is_ctx) else 1][rs, :] = y

    @pl.when(i < n_w)
    def _():
        wg_s[i] = wg_ref[0].astype(bf16)
        wu_s[i] = wu_ref[0].astype(bf16)
        wd_s[i] = wd_ref[0].astype(bf16)

        @pl.when(i == 0)
        def _():
            for c in range((ka + kb) // N_CHUNK):
                wo_s[c] = wo_ref[0, c * N_CHUNK:(c + 1) * N_CHUNK, :].astype(bf16)
            for rs in groups:
                mix_in(rs, True)
                y_ref[rs, :] = jnp.zeros((TM // ROW_GROUPS, D_MODEL), f32)

        for rs in groups:
            g = _dot(u_ref[rs, :], wg_s[i])
            up = _dot(u_ref[rs, :], wu_s[i])
            y_ref[rs, :] += _dot((_silu(g) * up).astype(bf16), wd_s[i])

        @pl.when(i == lead)
        def _():
            for rs in groups:
                finish(rs, y_ref[rs, :], True)

    def token_block(is_ctx):
        for rs in groups:
            mix_in(rs, is_ctx)

        def ffn_chunk(rs, c):
            g = _dot(u_ref[rs, :], wg_s[c])
            up = _dot(u_ref[rs, :], wu_s[c])
            h_ref[rs, c * N_CHUNK:(c + 1) * N_CHUNK] = (_silu(g) * up).astype(bf16)

        def ffn_down(rs):
            finish(rs, functools.reduce(
                lambda s, p: s + p,
                [_dot(h_ref[rs, c * N_CHUNK:(c + 1) * N_CHUNK], wd_s[c]) for c in range(n_w)]), is_ctx)

        for c in range(n_w + FFN_SKEW * (ROW_GROUPS - 1)):
            for r, rs in enumerate(groups):
                cc = c - FFN_SKEW * r
                if 0 <= cc < n_w:
                    ffn_chunk(rs, cc)
                if cc == n_w - 1:
                    ffn_down(rs)

    blk = i - lead

    @pl.when(jnp.logical_and(i >= n_w, blk < CTX_BLOCKS))
    def _():
        token_block(True)

    @pl.when(blk >= CTX_BLOCKS)
    def _():
        token_block(False)


def _post(x_parts, mix_a, mix_b, mod, ln_g, ln_b, w_out, w_gate, w_up, w_down, layer, mix_layer, split_out):
    ka, kb = mix_a[0].shape[1], mix_b[0].shape[1]
    n_w = D_FF // N_CHUNK
    lead = n_w - 1
    tok = lambda i: _tok(i, lead)
    lay = lambda i: (layer, 0, 0)
    in_specs = (_token_specs(x_parts, lead) + _token_specs(mix_a, lead) + _token_specs(mix_b, lead) + [
        pl.BlockSpec((None, 1, 6, D_MODEL), lambda i: (layer, _mod_row(tok(i)), 0, 0)),
        pl.BlockSpec((1, 2, D_MODEL), lay),
        pl.BlockSpec((1, 2, D_MODEL), lay),
        pl.BlockSpec((1, ka + kb, D_MODEL), lambda i: (mix_layer, 0, 0), pipeline_mode=pl.Buffered(1)),
        pl.BlockSpec((1, D_MODEL, N_CHUNK), lambda i: (layer, 0, jnp.minimum(i, n_w - 1))),
        pl.BlockSpec((1, D_MODEL, N_CHUNK), lambda i: (layer, 0, jnp.minimum(i, n_w - 1))),
        pl.BlockSpec((1, N_CHUNK, D_MODEL), lambda i: (layer, jnp.minimum(i, n_w - 1), 0))])
    if split_out:
        out_specs = [pl.BlockSpec((TM, D_MODEL), lambda i: (_ctx_blk(tok(i)), 0)),
                     pl.BlockSpec((TM, D_MODEL), lambda i: (_lat_blk(tok(i)), 0))]
        out_shape = [jax.ShapeDtypeStruct((N_CTX, D_MODEL), f32), jax.ShapeDtypeStruct((N_LAT, D_MODEL), f32)]
    else:
        out_specs = [pl.BlockSpec((TM, D_MODEL), lambda i: (tok(i), 0))]
        out_shape = [jax.ShapeDtypeStruct((N_TOK, D_MODEL), f32)]
    return pl.pallas_call(
        functools.partial(_post_kernel, n_x=len(x_parts), n_y=len(out_shape), ka=ka, kb=kb, n_w=n_w),
        grid=(lead + TOK_BLOCKS,),
        in_specs=in_specs,
        out_specs=out_specs,
        out_shape=out_shape,
        scratch_shapes=[pltpu.VMEM(((ka + kb) // N_CHUNK, N_CHUNK, D_MODEL), bf16),
                        pltpu.VMEM((n_w, D_MODEL, N_CHUNK), bf16),
                        pltpu.VMEM((n_w, D_MODEL, N_CHUNK), bf16), pltpu.VMEM((n_w, N_CHUNK, D_MODEL), bf16),
                        pltpu.VMEM((TM, D_MODEL), f32), pltpu.VMEM((TM, D_MODEL), bf16),
                        pltpu.VMEM((TM, D_FF), bf16), pltpu.VMEM((TM, D_MODEL), f32)],
        compiler_params=_params(1),
        name="post",
    )(*x_parts, *mix_a, *mix_b, mod, ln_g, ln_b, w_out, w_gate, w_up, w_down)


def _group_norm_gate(ro, rg, gmat, gn_g, gn_b):
    def gmean(parts):
        cols = []
        for c in range(0, RET_W, N_CHUNK):
            cols.append(sum(_dot(p[:, c:c + N_CHUNK], gmat) for p in parts))
        return jnp.concatenate(cols, -1)

    d = ro - gmean(_split_bf16(ro))
    var = gmean([(d * d).astype(bf16)])
    y = d * lax.rsqrt(var + LN_EPS) * gn_g + gn_b
    return _silu(rg.astype(f32)) * y


def _dup_head(x, j):
    first = _lane_half_mask(x.shape)
    keep = first if j == 0 else jnp.logical_not(first)
    xm = jnp.where(keep, x.astype(f32), 0.0)
    return xm + pltpu.roll(xm, HEAD_DIM, 1)


def _softmax_parts(scores, sink):
    m = sink
    for s in scores:
        m = jnp.maximum(m, jnp.max(s, -1, keepdims=True))
    es = [jnp.exp2(s - m) for s in scores]
    denom = jnp.exp2(sink - m)
    for e in es:
        denom = denom + jnp.sum(e, -1, keepdims=True)
    return es, denom


def _retention_tables(lg_ref, lgf_ref, lgb_ref, dmask_ref, kdec_ref, n):
    row = lax.broadcasted_iota(jnp.int32, (n, n), 0)
    col = lax.broadcasted_iota(jnp.int32, (n, n), 1)
    diff = (row - col).astype(f32)
    diag = jnp.where(row == col, 2.0 * QK_SCALE, QK_SCALE)
    for h in range(H_RET):
        dmask_ref[h] = jnp.exp(jnp.where(diff >= 0, lg_ref[0, h] * diff, -lg_ref[1, h] * diff)) * diag
    t = lax.broadcasted_iota(jnp.int32, (n, RET_W), 0).astype(f32)
    kdec_ref[0] = jnp.exp(lgf_ref[...] * (n - 1.0 - t)) * QK_SCALE
    kdec_ref[1] = jnp.exp(lgb_ref[...] * t) * QK_SCALE


def _retention_intra(pairs, q_of, k_of, v_of, dmask_ref):
    first = _lane_half_mask(k_of(pairs[0]).shape)
    masked = {}
    for p in pairs:
        kb = k_of(p)
        for e in range(2):
            keep = first if e == 0 else jnp.logical_not(first)
            s = _dot_nt(q_of(p), jnp.where(keep, kb, jnp.zeros_like(kb))) * dmask_ref[2 * p + e]
            masked[p, e] = s.astype(bf16)
    outs = {}
    for p in pairs:
        pv = [_dot(masked[p, e], v_of(p)) for e in range(2)]
        outs[p] = jnp.where(_lane_half_mask(pv[0].shape), pv[0], pv[1])
    return outs


def _window_group(subs, q_of, k_parts_of, v_parts_of, masks, sink_of):
    scores = {}
    for key in subs:
        parts = [_dot_nt(q_of(key), k) for k in k_parts_of(key)]
        scores[key] = [sc if mk is None else jnp.where(mk, sc, NEG_BIG) for sc, mk in zip(parts, masks)]
    probs = {}
    for key in subs:
        es, denom = _softmax_parts(scores[key], sink_of(key))
        probs[key] = ([ex.astype(bf16) for ex in es], denom)
    outs = {}
    for key in subs:
        es, denom = probs[key]
        pv = functools.reduce(lambda x, y: x + y, [_dot(ex, v) for ex, v in zip(es, v_parts_of(key))])
        outs[key] = pv / denom
    return outs


def _ctx_ab_kernel(lg_ref, sink_ref, rq_ref, rk_ref, rv_ref, rg_ref, wq_ref, wk_ref, wv_ref,
                   lgf_ref, lgb_ref, gmat_ref, gng_ref, gnb_ref,
                   ro_ref, wo_ref, st_ref, dmask_ref, kdec_ref, ret_ref):
    t_len = SEQ

    @pl.when(pl.program_id(0) == 0)
    def _():
        _retention_tables(lg_ref, lgf_ref, lgb_ref, dmask_ref, kdec_ref, t_len)

    first = _lane_half_mask((t_len, PAIR_W))
    for sq in range(CTX_SEQS):
        rows = slice(sq * t_len, (sq + 1) * t_len)
        psl = lambda p: slice(p * PAIR_W, (p + 1) * PAIR_W)
        for p0 in range(0, H_RET // 2, RET_GROUP):
            pairs = list(range(p0, p0 + RET_GROUP))
            intra = _retention_intra(pairs, lambda p: rq_ref[rows, psl(p)], lambda p: rk_ref[rows, psl(p)],
                                     lambda p: rv_ref[rows, psl(p)], dmask_ref)
            for p in pairs:
                ret_ref[rows, psl(p)] = intra[p]
        for p in range(H_RET // 2):
            sl = psl(p)
            kb = rk_ref[rows, sl]
            v = rv_ref[rows, sl]
            for d in range(2):
                kd_t = (kb * kdec_ref[d, :, sl]).T.astype(bf16)
                st = _dot(kd_t, v)
                st_ref[sq, d, 2 * p] = st[0:HEAD_DIM, 0:HEAD_DIM]
                st_ref[sq, d, 2 * p + 1] = pltpu.roll(st[HEAD_DIM:, :], HEAD_DIM, 1)[:, 0:HEAD_DIM]
        ro_ref[rows, :] = _group_norm_gate(ret_ref[rows, :], rg_ref[rows, :], gmat_ref[...], gng_ref[...],
                                           gnb_ref[...]).astype(bf16)

        k_dup = [_dup_head(wk_ref[rows, :], j).astype(bf16) for j in range(KV_WIN)]
        v_dup = [_dup_head(wv_ref[rows, :], j).astype(bf16) for j in range(KV_WIN)]

        def q_masked(key):
            qp, e = key
            qb = wq_ref[rows, qp * PAIR_W:(qp + 1) * PAIR_W]
            return jnp.where(first if e == 0 else jnp.logical_not(first), qb, jnp.zeros_like(qb))

        kv_of = lambda key: key[0] * 2 // G_WIN
        for g0 in range(0, H_WIN // 2, WIN_GROUP):
            subs = [(qp, e) for qp in range(g0, g0 + WIN_GROUP) for e in range(2)]
            outs = _window_group(subs, q_masked, lambda key: [k_dup[kv_of(key)]], lambda key: [v_dup[kv_of(key)]],
                                 [None], lambda key: sink_ref[0, 2 * key[0] + key[1]] * LOG2_E)
            for qp in range(g0, g0 + WIN_GROUP):
                wo_ref[rows, qp * PAIR_W:(qp + 1) * PAIR_W] = jnp.where(first, outs[qp, 0], outs[qp, 1]).astype(bf16)


def _ctx_ab(proj, log_gamma, sink, lgf_lanes, lgb_lanes, gmat, gn_g, gn_b):
    t = SEQ
    tb = CTX_SEQS * t
    smem = pl.BlockSpec(memory_space=pltpu.SMEM)
    const = lambda b: (0, 0)
    col = lambda c: (lambda b: (b, c))
    return pl.pallas_call(
        _ctx_ab_kernel,
        grid=(BATCH // CTX_SEQS,),
        in_specs=[smem, smem,
                  pl.BlockSpec((tb, RET_W), col(0)), pl.BlockSpec((tb, RET_W), col(1)),
                  pl.BlockSpec((tb, RET_W), col(2)), pl.BlockSpec((tb, RET_W), col(3)),
                  pl.BlockSpec((tb, WIN_W), col(4)),
                  pl.BlockSpec((tb, KV_W), col((4 * RET_W + WIN_W) // KV_W)),
                  pl.BlockSpec((tb, KV_W), col((4 * RET_W + WIN_W) // KV_W + 1)),
                  pl.BlockSpec((1, RET_W), const), pl.BlockSpec((1, RET_W), const),
                  pl.BlockSpec((N_CHUNK, N_CHUNK), const),
                  pl.BlockSpec((1, RET_W), const), pl.BlockSpec((1, RET_W), const)],
        out_specs=[pl.BlockSpec((tb, RET_W), lambda b: (b, 0)),
                   pl.BlockSpec((tb, WIN_W), lambda b: (b, 0)),
                   pl.BlockSpec((CTX_SEQS, 2, H_RET, HEAD_DIM, HEAD_DIM), lambda b: (b, 0, 0, 0, 0))],
        out_shape=[jax.ShapeDtypeStruct((BATCH * t, RET_W), bf16),
                   jax.ShapeDtypeStruct((BATCH * t, WIN_W), bf16),
                   jax.ShapeDtypeStruct((BATCH, 2, H_RET, HEAD_DIM, HEAD_DIM), f32)],
        scratch_shapes=[pltpu.VMEM((H_RET, t, t), f32), pltpu.VMEM((2, t, RET_W), f32),
                        pltpu.VMEM((tb, RET_W), f32)],
        compiler_params=_params(1),
        name="ctx_ab",
    )(log_gamma, sink, proj, proj, proj, proj, proj, proj, proj, lgf_lanes, lgb_lanes, gmat, gn_g, gn_b)


def _pair_state(s0_ref, d, p):
    zero = jnp.zeros((HEAD_DIM, HEAD_DIM), f32)
    top = jnp.concatenate([s0_ref[0, 0, d, 2 * p], zero], 1)
    bottom = jnp.concatenate([zero, s0_ref[0, 0, d, 2 * p + 1]], 1)
    return jnp.concatenate([top, bottom], 0)


def _lat_ab_kernel(lg_ref, sink_ref, rq_ref, rk_ref, rv_ref, rg_ref, wq_ref, wk_ref, wv_ref, ck_ref, cv_ref,
                   s0_ref, lgf_ref, lgb_ref, gmat_ref, gng_ref, gnb_ref,
                   ro_ref, wo_ref, ret_ref, dmask_ref, kdec_ref, qdec_ref, sf_ref, sb_ref):
    t_len = DEC_SEQ
    n_chunks = t_len // TQ
    step = pl.program_id(1)
    first = _lane_half_mask((TQ, PAIR_W))

    @pl.when(jnp.logical_and(pl.program_id(0) == 0, step == 0))
    def _():
        _retention_tables(lg_ref, lgf_ref, lgb_ref, dmask_ref, kdec_ref, TQ)
        t = lax.broadcasted_iota(jnp.int32, (TQ, RET_W), 0).astype(f32)
        qdec_ref[0] = jnp.exp(lgf_ref[...] * (t + 1.0))
        qdec_ref[1] = jnp.exp(lgb_ref[...] * (TQ - t))

    @pl.when(step == 0)
    def _():
        r = lax.broadcasted_iota(jnp.int32, (PAIR_W, PAIR_W), 0)
        c_ = lax.broadcasted_iota(jnp.int32, (PAIR_W, PAIR_W), 1)
        same_head = (r < HEAD_DIM) == (c_ < HEAD_DIM)
        for p in range(H_RET // 2):
            sl = slice(p * PAIR_W, (p + 1) * PAIR_W)
            kv = []
            for c in range(n_chunks):
                rows = slice(c * TQ, (c + 1) * TQ)
                kc = rk_ref[rows, sl]
                vc = rv_ref[rows, sl]
                kv.append([jnp.where(same_head, _dot((kc * kdec_ref[d, :, sl]).T.astype(bf16), vc), 0.0)
                           for d in range(2)])
            state = _pair_state(s0_ref, 0, p)
            for c in range(n_chunks):
                sf_ref[c, p] = state
                state = state * jnp.exp(lgf_ref[:, sl] * TQ) + kv[c][0]
            state = _pair_state(s0_ref, 1, p)
            for c in reversed(range(n_chunks)):
                sb_ref[c, p] = state
                state = state * jnp.exp(lgb_ref[:, sl] * TQ) + kv[c][1]

    for sub in range(LAT_SUB):
        chunk = step * LAT_SUB + sub
        q0 = pl.multiple_of(chunk * TQ, TQ)
        rows = slice(sub * TQ, (sub + 1) * TQ)
        psl = lambda p: slice(p * PAIR_W, (p + 1) * PAIR_W)
        intra = {}
        for p0 in range(0, H_RET // 2, RET_GROUP):
            intra.update(_retention_intra(list(range(p0, p0 + RET_GROUP)), lambda p: rq_ref[rows, psl(p)],
                                          lambda p: rk_ref[pl.ds(q0, TQ), psl(p)],
                                          lambda p: rv_ref[pl.ds(q0, TQ), psl(p)], dmask_ref))
        for p in range(H_RET // 2):
            sl = psl(p)
            q = rq_ref[rows, sl]
            o = intra[p]
            o = o + _dot(q, sf_ref[chunk, p].astype(bf16)) * qdec_ref[0, :, sl]
            o = o + _dot(q, sb_ref[chunk, p].astype(bf16)) * qdec_ref[1, :, sl]
            ret_ref[rows, sl] = o
        ro_ref[rows, :] = _group_norm_gate(ret_ref[rows, :], rg_ref[rows, :], gmat_ref[...], gng_ref[...],
                                           gnb_ref[...]).astype(bf16)

        band = TQ + 2 * WINDOW
        k_start = pl.multiple_of(jnp.clip(q0 - WINDOW, 0, t_len - band), LANES)
        qi = q0 + lax.broadcasted_iota(jnp.int32, (TQ, band), 0)
        kj = k_start + lax.broadcasted_iota(jnp.int32, (TQ, band), 1)
        in_band = jnp.abs(qi - kj) <= WINDOW
        k_parts = [[_dup_head(wk_ref[pl.ds(k_start, band), :], j).astype(bf16), _dup_head(ck_ref[0], j).astype(bf16)]
                   for j in range(KV_WIN)]
        v_parts = [[_dup_head(wv_ref[pl.ds(k_start, band), :], j).astype(bf16), _dup_head(cv_ref[0], j).astype(bf16)]
                   for j in range(KV_WIN)]

        def q_masked(key):
            qp, e = key
            qb = wq_ref[rows, qp * PAIR_W:(qp + 1) * PAIR_W]
            return jnp.where(first if e == 0 else jnp.logical_not(first), qb, jnp.zeros_like(qb))

        kv_of = lambda key: key[0] * 2 // G_WIN
        for g0 in range(0, H_WIN // 2, WIN_GROUP_LAT):
            subs = [(qp, e) for qp in range(g0, g0 + WIN_GROUP_LAT) for e in range(2)]
            outs = _window_group(subs, q_masked, lambda key: k_parts[kv_of(key)], lambda key: v_parts[kv_of(key)],
                                 [in_band, None], lambda key: sink_ref[0, 2 * key[0] + key[1]] * LOG2_E)
            for qp in range(g0, g0 + WIN_GROUP_LAT):
                wo_ref[rows, qp * PAIR_W:(qp + 1) * PAIR_W] = jnp.where(first, outs[qp, 0], outs[qp, 1]).astype(bf16)


def _lat_ab(proj, log_gamma, sink, ck, cv, state, layer, lgf_lanes, lgb_lanes, gmat, gn_g, gn_b):
    t = DEC_SEQ
    tb = LAT_SUB * TQ
    nq = t // tb
    smem = pl.BlockSpec(memory_space=pltpu.SMEM)
    const = lambda b, i: (0, 0)
    qcol = lambda c: (lambda b, i: (N_CTX // tb + b * nq + i, c))
    bcol = lambda c: (lambda b, i: (N_CTX // t + b, c))
    kv_col = (4 * RET_W + WIN_W) // KV_W
    return pl.pallas_call(
        _lat_ab_kernel,
        grid=(DEC_BATCH, nq),
        in_specs=[smem, smem,
                  pl.BlockSpec((tb, RET_W), qcol(0)), pl.BlockSpec((t, RET_W), bcol(1)),
                  pl.BlockSpec((t, RET_W), bcol(2)), pl.BlockSpec((tb, RET_W), qcol(3)),
                  pl.BlockSpec((tb, WIN_W), qcol(4)),
                  pl.BlockSpec((t, KV_W), bcol(kv_col)), pl.BlockSpec((t, KV_W), bcol(kv_col + 1)),
                  pl.BlockSpec((1, PAST_LEN, KV_W), lambda b, i: (b, 0, 0)),
                  pl.BlockSpec((1, PAST_LEN, KV_W), lambda b, i: (b, 0, 0)),
                  pl.BlockSpec((1, 1, 2, H_RET, HEAD_DIM, HEAD_DIM), lambda b, i: (b, layer, 0, 0, 0, 0)),
                  pl.BlockSpec((1, RET_W), const), pl.BlockSpec((1, RET_W), const),
                  pl.BlockSpec((N_CHUNK, N_CHUNK), const),
                  pl.BlockSpec((1, RET_W), const), pl.BlockSpec((1, RET_W), const)],
        out_specs=[pl.BlockSpec((tb, RET_W), lambda b, i: (b * nq + i, 0)),
                   pl.BlockSpec((tb, WIN_W), lambda b, i: (b * nq + i, 0))],
        out_shape=[jax.ShapeDtypeStruct((DEC_BATCH * t, RET_W), bf16),
                   jax.ShapeDtypeStruct((DEC_BATCH * t, WIN_W), bf16)],
        scratch_shapes=[pltpu.VMEM((tb, RET_W), f32), pltpu.VMEM((H_RET, TQ, TQ), f32),
                        pltpu.VMEM((2, TQ, RET_W), f32), pltpu.VMEM((2, TQ, RET_W), f32),
                        pltpu.VMEM((t // TQ, H_RET // 2, PAIR_W, PAIR_W), f32),
                        pltpu.VMEM((t // TQ, H_RET // 2, PAIR_W, PAIR_W), f32)],
        compiler_params=_params(2),
        name="lat_ab",
    )(log_gamma, sink, proj, proj, proj, proj, proj, proj, proj, ck, cv, state,
      lgf_lanes, lgb_lanes, gmat, gn_g, gn_b)


def _lambda_full(lam_ref, lam_init):
    lam = lam_ref[...]
    a = jnp.sum(lam[0:1, :] * lam[1:2, :], -1, keepdims=True)
    b = jnp.sum(lam[2:3, :] * lam[3:4, :], -1, keepdims=True)
    return jnp.exp(a) - jnp.exp(b) + lam_init


def _diff_heads(q_of, k_parts_of, v_parts_of, lam, subln, lam_init, group):
    res = []
    for h0 in range(0, H_DIFF, group):
        res += _diff_head_group(range(h0, h0 + group), q_of, k_parts_of, v_parts_of, lam, subln, lam_init)
    return res


def _diff_head_group(heads, q_of, k_parts_of, v_parts_of, lam, subln, lam_init):
    subs = [(h, e) for h in heads for e in range(2)]
    scores = {}
    for h, e in subs:
        q = q_of(h)
        fm = _lane_half_mask(q.shape)
        q_sub = jnp.where(fm if e == 0 else jnp.logical_not(fm), q, jnp.zeros_like(q))
        scores[h, e] = [_dot(q_sub, k) if transposed else _dot_nt(q_sub, k) for k, transposed in k_parts_of(h)]
    probs = {}
    for key in subs:
        m = scores[key][0].max(-1, keepdims=True)
        for sc in scores[key][1:]:
            m = jnp.maximum(m, sc.max(-1, keepdims=True))
        es = [jnp.exp2(sc - m) for sc in scores[key]]
        denom = es[0].sum(-1, keepdims=True)
        for ex in es[1:]:
            denom = denom + ex.sum(-1, keepdims=True)
        probs[key] = ([ex.astype(bf16) for ex in es], denom)
    outs = {}
    for h, e in subs:
        es, denom = probs[h, e]
        pv = functools.reduce(lambda x, y: x + y, [_dot(ex, v) for v, ex in zip(v_parts_of(h), es)])
        outs[h, e] = pv / denom
    res = []
    for h in heads:
        a = outs[h, 0] - lam * outs[h, 1]
        res.append(a * lax.rsqrt(jnp.mean(a * a, -1, keepdims=True) + LN_EPS) * subln * (1.0 - lam_init))
    return res


def _fourier_rows(ct_ref, st_ref, z, bdc_ref, bds_ref):
    zc = _dot(z, bdc_ref[...].astype(bf16)).astype(bf16)
    zs = _dot(z, bds_ref[...].astype(bf16)).astype(bf16)
    return _dot(ct_ref[...].astype(bf16), zc) - _dot(st_ref[...].astype(bf16), zs)


def _ctx_cd_kernel(q_ref, k_ref, v_ref, z_ref, lam_ref, subln_ref, ct_ref, st_ref, bdc_ref, bds_ref,
                   a_ref, zf_ref, *, lam_init):
    lam = _lambda_full(lam_ref, lam_init)
    for sq in range(CTX_SEQS_CD):
        rows = slice(sq * SEQ, (sq + 1) * SEQ)
        sl = lambda h: slice(h * PAIR_W, (h + 1) * PAIR_W)
        heads = _diff_heads(lambda h: q_ref[rows, sl(h)], lambda h: [(k_ref[rows, sl(h)], False)],
                            lambda h: [v_ref[rows, sl(h)]], lam, subln_ref[...], lam_init, DIFF_GROUP)
        for h in range(H_DIFF):
            a_ref[rows, sl(h)] = heads[h].astype(bf16)
        zf_ref[rows, :] = _fourier_rows(ct_ref, st_ref, z_ref[rows, :], bdc_ref, bds_ref).astype(bf16)


def _ctx_cd(proj, lam, subln, ct, st, bdc, bds, lam_init):
    t = SEQ
    tb = CTX_SEQS_CD * t
    const = lambda b: (0, 0)
    col = lambda c: (lambda b: (b, c))
    return pl.pallas_call(
        functools.partial(_ctx_cd_kernel, lam_init=lam_init),
        grid=(BATCH // CTX_SEQS_CD,),
        in_specs=[pl.BlockSpec((tb, DIFF_W), col(0)), pl.BlockSpec((tb, DIFF_W), col(1)),
                  pl.BlockSpec((tb, DIFF_W), col(2)), pl.BlockSpec((tb, FNET_W), col(3 * DIFF_W // FNET_W)),
                  pl.BlockSpec((4, HEAD_DIM), const), pl.BlockSpec((1, PAIR_W), const),
                  pl.BlockSpec((t, t), const), pl.BlockSpec((t, t), const),
                  pl.BlockSpec((FNET_W, FNET_W), const), pl.BlockSpec((FNET_W, FNET_W), const)],
        out_specs=[pl.BlockSpec((tb, DIFF_W), lambda b: (b, 0)), pl.BlockSpec((tb, FNET_W), lambda b: (b, 0))],
        out_shape=[jax.ShapeDtypeStruct((BATCH * t, DIFF_W), bf16),
                   jax.ShapeDtypeStruct((BATCH * t, FNET_W), bf16)],
        compiler_params=_params(1),
        name="ctx_cd",
    )(proj, proj, proj, proj, lam, subln, ct, st, bdc, bds)


def _lat_cd_kernel(q_ref, k_ref, v_ref, z_ref, ckt_ref, cv_ref, lam_ref, subln_ref, ct_ref, st_ref, bdc_ref, bds_ref,
                   a_ref, zf_ref, *, lam_init):
    lam = _lambda_full(lam_ref, lam_init)
    sl = lambda h: slice(h * PAIR_W, (h + 1) * PAIR_W)
    heads = _diff_heads(lambda h: q_ref[:, sl(h)],
                        lambda h: [(k_ref[:, sl(h)], False), (ckt_ref[0, h].astype(bf16), True)],
                        lambda h: [v_ref[:, sl(h)], cv_ref[0, h].astype(bf16)], lam, subln_ref[...], lam_init,
                        DIFF_GROUP_LAT)
    for h in range(H_DIFF):
        a_ref[:, sl(h)] = heads[h].astype(bf16)
    zf_ref[...] = _fourier_rows(ct_ref, st_ref, z_ref[...], bdc_ref, bds_ref).astype(bf16)


def _lat_cd(proj, ck, cv, lam, subln, ct, st, bdc, bds, lam_init):
    t = DEC_SEQ
    nq = t // TQ_CD
    const = lambda b, i: (0, 0)
    return pl.pallas_call(
        functools.partial(_lat_cd_kernel, lam_init=lam_init),
        grid=(DEC_BATCH, nq),
        in_specs=[pl.BlockSpec((TQ_CD, DIFF_W), lambda b, i: (N_CTX // TQ_CD + b * nq + i, 0)),
                  pl.BlockSpec((t, DIFF_W), lambda b, i: (N_CTX // t + b, 1)),
                  pl.BlockSpec((t, DIFF_W), lambda b, i: (N_CTX // t + b, 2)),
                  pl.BlockSpec((t, FNET_W), lambda b, i: (N_CTX // t + b, 3 * DIFF_W // FNET_W)),
                  pl.BlockSpec((1, H_DIFF, PAIR_W, PAST_LEN), lambda b, i: (b, 0, 0, 0)),
                  pl.BlockSpec((1, H_DIFF, PAST_LEN, PAIR_W), lambda b, i: (b, 0, 0, 0)),
                  pl.BlockSpec((4, HEAD_DIM), const), pl.BlockSpec((1, PAIR_W), const),
                  pl.BlockSpec((TQ_CD, t), lambda b, i: (i, 0)), pl.BlockSpec((TQ_CD, t), lambda b, i: (i, 0)),
                  pl.BlockSpec((FNET_W, FNET_W), const), pl.BlockSpec((FNET_W, FNET_W), const)],
        out_specs=[pl.BlockSpec((TQ_CD, DIFF_W), lambda b, i: (b * nq + i, 0)),
                   pl.BlockSpec((TQ_CD, FNET_W), lambda b, i: (b * nq + i, 0))],
        out_shape=[jax.ShapeDtypeStruct((DEC_BATCH * t, DIFF_W), bf16),
                   jax.ShapeDtypeStruct((DEC_BATCH * t, FNET_W), bf16)],
        compiler_params=_params(2),
        name="lat_cd",
    )(proj, proj, proj, proj, ck, cv, lam, subln, ct, st, bdc, bds)


def _rope_tables():
    t = np.arange(DEC_SEQ)
    quarter = HEAD_DIM // 4
    inv = ROPE_BASE ** (-np.arange(quarter, dtype=np.float64) / quarter)
    ang = np.concatenate([(t // GRID_W)[:, None] * inv, (t % GRID_W)[:, None] * inv], -1)
    cos, sin = np.cos(ang), np.sin(ang)
    reps = LANES // HEAD_DIM
    return (np.tile(np.concatenate([cos, cos], -1), (1, reps)).astype(np.float32),
            np.tile(np.concatenate([-sin, sin], -1), (1, reps)).astype(np.float32))


def _dft_tables(n):
    k = np.arange(n)
    ang = (2.0 * math.pi / n) * ((k[:, None] * k[None, :]) % n)
    return (np.cos(ang) / math.sqrt(n)).astype(np.float32), (np.sin(ang) / math.sqrt(n)).astype(np.float32)


def _block_diag(m, reps):
    return np.kron(np.eye(reps, dtype=m.dtype), m)


def kernel(x_prompt, x_sample, state_ret, cache_win_k, cache_win_v, cache_diff_k, cache_diff_v, c, c_ctx, w_mod, b_mod, ln_g, ln_b, w_in_ab, w_out_ab, ret_log_gamma, ret_gn_g, ret_gn_b, win_sink, w_in_cd, w_out_cd, diff_lambda, diff_subln_g, w_gate, w_up, w_down):
    cond = jnp.concatenate([c_ctx[None, :], c, jnp.zeros((SUBLANES - 1 - DEC_BATCH, D_MODEL), f32)], 0)
    mod = _modulation(cond, w_mod, b_mod).reshape(DEPTH, SUBLANES, 6, D_MODEL)

    rope_tabs = _rope_tables()
    gmat = jnp.asarray(_block_diag(np.full((HEAD_DIM, HEAD_DIM), 1.0 / HEAD_DIM, np.float32),
                                   N_CHUNK // HEAD_DIM), bf16)
    c64, s64 = _dft_tables(FNET_DIM)
    bdc = _block_diag(c64, FNET_GROUPS)
    bds = _block_diag(s64, FNET_GROUPS)
    dft_ctx = _dft_tables(SEQ)
    dft_lat = _dft_tables(DEC_SEQ)

    x_parts = [x_prompt.reshape(N_CTX, D_MODEL), x_sample.reshape(N_LAT, D_MODEL)]
    outs = {}
    for l in range(DEPTH):
        i = l // 2
        if l % 2 == 0:
            lgf = jnp.repeat(ret_log_gamma[i, 0], HEAD_DIM)[None, :]
            lgb = jnp.repeat(ret_log_gamma[i, 1], HEAD_DIM)[None, :]
            gn_g = ret_gn_g[i][None, :]
            gn_b = ret_gn_b[i][None, :]
            sink = win_sink[i][None, :]
            rope_tiles = tuple(range(0, 2 * RET_W // LANES)) + tuple(
                range(4 * RET_W // LANES, (4 * RET_W + WIN_W + KV_W) // LANES))
            kv_tile = (4 * RET_W + WIN_W) // LANES
            kv_shape = (BATCH, 1, KV_WIN, HEAD_DIM, SEQ)
            scale_tiles = tuple(range(4 * RET_W // LANES, (4 * RET_W + WIN_W) // LANES))
            proj, wk_t, wv_t = _proj(x_parts, mod, l, w_in_ab, i, scale_tiles, rope_tabs, rope_tiles,
                                     (kv_shape, kv_shape),
                                     {kv_tile: ("heads", 0, 0), kv_tile + 1: ("heads", 1, 0)})
            ro_c, wo_c, st_c = _ctx_ab(proj, ret_log_gamma[i], sink, lgf, lgb, gmat, gn_g, gn_b)
            ck = cache_win_k[:, i].reshape(DEC_BATCH, PAST_LEN, KV_W)
            cv = cache_win_v[:, i].reshape(DEC_BATCH, PAST_LEN, KV_W)
            ro_l, wo_l = _lat_ab(proj, ret_log_gamma[i], sink, ck, cv, state_ret, i, lgf, lgb, gmat, gn_g, gn_b)
            mix_a, mix_b, w_out = (ro_c, ro_l), (wo_c, wo_l), w_out_ab
            outs.setdefault('state', []).append(st_c[:, None])
            outs.setdefault('win_k', []).append(jnp.transpose(wk_t, (0, 1, 4, 2, 3)))
            outs.setdefault('win_v', []).append(jnp.transpose(wv_t, (0, 1, 4, 2, 3)))
        else:
            lam_init = 0.8 - 0.6 * math.exp(-0.3 * l)
            subln = diff_subln_g[i][None, :]
            rope_tiles = tuple(range(0, 2 * DIFF_W // LANES))
            plan = {}
            for h in range(H_DIFF):
                plan[DIFF_W // LANES + h] = ("pairs", 0, h)
                plan[2 * DIFF_W // LANES + h] = ("plain", 1, h)
            scale_tiles = tuple(range(0, DIFF_W // LANES))
            proj, dk_t, dv_h = _proj(
                x_parts, mod, l, w_in_cd, i, scale_tiles, rope_tabs, rope_tiles,
                ((BATCH, 1, H_DIFF, 2, HEAD_DIM, SEQ), (BATCH, 1, H_DIFF, SEQ, 2 * HEAD_DIM)), plan)
            a_c, z_c = _ctx_cd(proj, diff_lambda[i], subln, dft_ctx[0], dft_ctx[1], bdc, bds, lam_init)
            ck = jnp.transpose(cache_diff_k[:, i], (0, 2, 3, 4, 1)).reshape(DEC_BATCH, H_DIFF, PAIR_W, PAST_LEN)
            cv = jnp.transpose(cache_diff_v[:, i], (0, 2, 1, 3))
            a_l, z_l = _lat_cd(proj, ck, cv, diff_lambda[i], subln, dft_lat[0], dft_lat[1], bdc, bds, lam_init)
            mix_a, mix_b, w_out = (a_c, a_l), (z_c, z_l), w_out_cd
            outs.setdefault('diff_k', []).append(jnp.transpose(dk_t, (0, 1, 5, 2, 3, 4)))
            outs.setdefault('diff_v', []).append(jnp.transpose(dv_h, (0, 1, 3, 2, 4)))
        x_parts = _post(x_parts, mix_a, mix_b, mod, ln_g, ln_b, w_out, w_gate, w_up, w_down, l, i,
                        split_out=(l == DEPTH - 1))

    y_prompt = x_parts[0].reshape(BATCH, SEQ, D_MODEL)
    y_sample = x_parts[1].reshape(DEC_BATCH, DEC_SEQ, D_MODEL)
    cat = lambda parts: parts[0] if len(parts) == 1 else jnp.concatenate(parts, 1)
    return (y_prompt, y_sample, cat(outs['state']), cat(outs['win_k']), cat(outs['win_v']),
            cat(outs['diff_k']), cat(outs['diff_v']))
```

```python
import functools
import math

import jax
import jax.numpy as jnp
import numpy as np
from jax import lax
from jax.experimental import pallas as pl
from jax.experimental.pallas import tpu as pltpu

D_MODEL = 1024
BATCH = 32
SEQ = 256
DEPTH = 2
DEC_BATCH = 2
DEC_SEQ = 1024
PAST_LEN = 512
GRID_W = 64
HEAD_DIM = 64
ROPE_BASE = 10000.0
H_RET = 8
H_WIN = 8
KV_WIN = 2
G_WIN = H_WIN // KV_WIN
WINDOW = 128
H_DIFF = 6
FNET_GROUPS = 4
FNET_DIM = 64
D_FF = 256 * math.ceil(8 * D_MODEL / 3 / 256)
RET_W = H_RET * HEAD_DIM
WIN_W = H_WIN * HEAD_DIM
KV_W = KV_WIN * HEAD_DIM
AB_IN = 4 * RET_W + WIN_W + 2 * KV_W
DIFF_W = H_DIFF * 2 * HEAD_DIM
FNET_W = FNET_GROUPS * FNET_DIM
CD_IN = 3 * DIFF_W + FNET_W
ALPHA = (2 * DEPTH) ** 0.25
LN_EPS = 1e-5
QK_SCALE = HEAD_DIM ** -0.5
LOG2_E = math.log2(math.e)

N_CTX = BATCH * SEQ
N_LAT = DEC_BATCH * DEC_SEQ
N_TOK = N_CTX + N_LAT

LANES = 128
SUBLANES = 8
PAIR_W = 2 * HEAD_DIM
TM = 512
TM_PROJ = 1024
CTX_BLOCKS = N_CTX // TM
TOK_BLOCKS = N_TOK // TM
ROW_GROUPS = 2
FFN_SKEW = 2
TQ = 256
LAT_SUB = 4
TQ_CD = 512
CTX_SEQS = 4
CTX_SEQS_CD = 4
RET_GROUP = 4
WIN_GROUP = 4
WIN_GROUP_LAT = 4
DIFF_GROUP_LAT = 1
DIFF_GROUP = 3
N_CHUNK = 256
MOD_TN = 1536
NEG_BIG = -1e30
VMEM_LIMIT = 56 * 1024 * 1024

f32 = jnp.float32
bf16 = jnp.bfloat16


def _params(n_axes):
    return pltpu.CompilerParams(dimension_semantics=("arbitrary",) * n_axes,
                                vmem_limit_bytes=VMEM_LIMIT)


def _dot(a, b):
    return jnp.dot(a, b, preferred_element_type=f32)


def _dot_nt(a, b):
    return lax.dot_general(a, b, (((1,), (1,)), ((), ())), preferred_element_type=f32)


def _ln(x):
    mu = jnp.mean(x, -1, keepdims=True)
    d = x - mu
    var = jnp.mean(d * d, -1, keepdims=True)
    return d * lax.rsqrt(var + LN_EPS)


def _silu(x):
    return x * jax.nn.sigmoid(x)


def _split_bf16(x):
    hi = x.astype(bf16)
    lo = (x - hi.astype(f32)).astype(bf16)
    return hi, lo


def _lane_half_mask(shape):
    return (lax.broadcasted_iota(jnp.int32, shape, len(shape) - 1) & HEAD_DIM) == 0


def _mod_kernel(c_ref, w_ref, b_ref, o_ref):
    layer = pl.program_id(0)
    a = _silu(c_ref[...])
    rows = a.shape[0]
    a_hi, a_lo = _split_bf16(a)
    w_hi, w_lo = _split_bf16(w_ref[0])
    both = _dot(jnp.concatenate([a_hi, a_lo], 0), w_hi)
    o_ref[0] = both[:rows] + both[rows:] + _dot(a_hi, w_lo) + b_ref[pl.ds(layer, 1), :]


def _modulation(cond, w_mod, b_mod):
    tn = MOD_TN
    rows = cond.shape[0]
    return pl.pallas_call(
        _mod_kernel,
        grid=(DEPTH, 6 * D_MODEL // tn),
        in_specs=[pl.BlockSpec((rows, D_MODEL), lambda l, j: (0, 0)),
                  pl.BlockSpec((1, D_MODEL, tn), lambda l, j: (l, 0, j)),
                  pl.BlockSpec((DEPTH, tn), lambda l, j: (0, j))],
        out_specs=pl.BlockSpec((1, rows, tn), lambda l, j: (l, 0, j)),
        out_shape=jax.ShapeDtypeStruct((DEPTH, rows, 6 * D_MODEL), f32),
        compiler_params=_params(2),
        name="modulation",
    )(cond, w_mod, b_mod)


def _tok(i, n_w):
    return jnp.maximum(i - n_w, 0)


def _ctx_blk(t, tm=TM):
    return jnp.minimum(t, N_CTX // tm - 1)


def _lat_blk(t, tm=TM):
    return jnp.maximum(t - N_CTX // tm, 0)


def _mod_row(t, tm=TM):
    return jnp.where(t < N_CTX // tm, 0, 1 + _lat_blk(t, tm) * tm // DEC_SEQ)


def _token_specs(parts, n_w, tm=TM):
    width = parts[0].shape[1]
    if len(parts) == 1:
        return [pl.BlockSpec((tm, width), lambda i: (_tok(i, n_w), 0))]
    return [pl.BlockSpec((tm, width), lambda i: (_ctx_blk(_tok(i, n_w), tm), 0)),
            pl.BlockSpec((tm, width), lambda i: (_lat_blk(_tok(i, n_w), tm), 0))]


def _pick(refs, is_ctx, rs):
    return refs[0 if (len(refs) == 1 or is_ctx) else 1][rs, :]


def _rope_pair(y, cos, sin_signed):
    first_half = (lax.broadcasted_iota(jnp.int32, y.shape, 1) & (HEAD_DIM // 2)) == 0
    swapped = jnp.where(first_half, pltpu.roll(y, LANES - HEAD_DIM // 2, 1), pltpu.roll(y, HEAD_DIM // 2, 1))
    return y * cos + swapped * sin_signed


def _proj_kernel(*refs, n_x, n_cache, n_w, rope_tiles, scale_tiles, cache_plan):
    x_refs = refs[:n_x]
    mod_ref, w_ref, cos_ref, sin_ref, o_ref = refs[n_x:n_x + 5]
    cache_refs = refs[n_x + 5:n_x + 5 + n_cache]
    wbf_ref, u_ref = refs[n_x + 5 + n_cache:]
    i = pl.program_id(0)

    @pl.when(i == 0)
    def _():
        for c in range(n_w):
            wbf_ref[c] = w_ref[0, :, c * N_CHUNK:(c + 1) * N_CHUNK].astype(bf16)

    def tokens(is_ctx):
        x_ref = x_refs[0] if is_ctx else x_refs[-1]
        shift = mod_ref[0, 0:1, :]
        scale = mod_ref[0, 1:2, :]
        groups = [slice(b * SEQ, (b + 1) * SEQ) for b in range(TM_PROJ // SEQ)]
        for rs in groups:
            u_ref[rs, :] = (_ln(x_ref[rs, :]) * (1.0 + scale) + shift).astype(bf16)
        for c in range(n_w):
            y_all = _dot(u_ref[...], wbf_ref[c])
            for b, rs in enumerate(groups):
                y = y_all[rs, :]
                for t in range(N_CHUNK // LANES):
                    tile = c * (N_CHUNK // LANES) + t
                    piece = y[:, t * LANES:(t + 1) * LANES]
                    if tile in rope_tiles and not is_ctx:
                        piece = _rope_pair(piece, cos_ref[rs, :], sin_ref[rs, :])
                    if tile in scale_tiles:
                        piece = piece * (QK_SCALE * LOG2_E)
                    o_ref[rs, tile * LANES:(tile + 1) * LANES] = piece.astype(o_ref.dtype)
                    if tile in cache_plan and is_ctx:
                        kind, out_idx, slot = cache_plan[tile]
                        c_ref = cache_refs[out_idx]
                        if kind == "plain":
                            c_ref[b, 0, slot] = piece
                        else:
                            piece_t = piece.T
                            if kind == "heads":
                                c_ref[b, 0, 0] = piece_t[0:HEAD_DIM]
                                c_ref[b, 0, 1] = piece_t[HEAD_DIM:]
                            else:
                                c_ref[b, 0, slot, 0] = piece_t[0:HEAD_DIM]
                                c_ref[b, 0, slot, 1] = piece_t[HEAD_DIM:]

    t = i - 1

    @pl.when(jnp.logical_and(t >= 0, t < N_CTX // TM_PROJ))
    def _():
        tokens(True)

    @pl.when(t >= N_CTX // TM_PROJ)
    def _():
        tokens(False)


def _proj(x_parts, mod, mod_layer, w_all, layer, scale_tiles, rope_tabs, rope_tiles, cache_shapes, cache_plan):
    n_out = w_all.shape[2]
    n_w = n_out // N_CHUNK
    tm = TM_PROJ
    nb = DEC_SEQ // tm
    tok = lambda i: _tok(i, 1)
    in_specs = _token_specs(x_parts, 1, tm) + [
        pl.BlockSpec((None, 1, 6, D_MODEL), lambda i: (mod_layer, _mod_row(tok(i), tm), 0, 0)),
        pl.BlockSpec((1, D_MODEL, n_out), lambda i: (layer, 0, 0), pipeline_mode=pl.Buffered(1)),
        pl.BlockSpec((tm, LANES), lambda i: (_lat_blk(tok(i), tm) % nb, 0)),
        pl.BlockSpec((tm, LANES), lambda i: (_lat_blk(tok(i), tm) % nb, 0))]
    out_specs = [pl.BlockSpec((tm, n_out), lambda i: (tok(i), 0))]
    out_shape = [jax.ShapeDtypeStruct((N_TOK, n_out), bf16)]
    for shp in cache_shapes:
        blk = (tm // SEQ,) + tuple(shp[1:])
        out_specs.append(pl.BlockSpec(blk, lambda i, nd=len(shp): (_ctx_blk(tok(i), tm),) + (0,) * (nd - 1)))
        out_shape.append(jax.ShapeDtypeStruct(tuple(shp), f32))
    return pl.pallas_call(
        functools.partial(_proj_kernel, n_x=len(x_parts), n_cache=len(cache_shapes), n_w=n_w,
                          rope_tiles=frozenset(rope_tiles), scale_tiles=frozenset(scale_tiles),
                          cache_plan=dict(cache_plan)),
        grid=(1 + N_TOK // tm,),
        in_specs=in_specs,
        out_specs=out_specs,
        out_shape=out_shape,
        scratch_shapes=[pltpu.VMEM((n_w, D_MODEL, N_CHUNK), bf16), pltpu.VMEM((tm, D_MODEL), bf16)],
        compiler_params=_params(1),
        name="proj",
    )(*x_parts, mod, w_all, *rope_tabs)


def _post_kernel(*refs, n_x, n_y, ka, kb, n_w):
    x_refs = refs[:n_x]
    (ac_ref, al_ref, bc_ref, bl_ref, mod_ref, lng_ref, lnb_ref,
     wo_ref, wg_ref, wu_ref, wd_ref) = refs[n_x:n_x + 11]
    y_refs = refs[n_x + 11:n_x + 11 + n_y]
    wo_s, wg_s, wu_s, wd_s, x1_ref, u_ref, h_ref, y_ref = refs[n_x + 11 + n_y:]
    i = pl.program_id(0)
    lead = n_w - 1
    gate1 = mod_ref[0, 2:3, :]
    shift2 = mod_ref[0, 3:4, :]
    scale2 = mod_ref[0, 4:5, :]
    gate2 = mod_ref[0, 5:6, :]
    groups = [slice(r * TM // ROW_GROUPS, (r + 1) * TM // ROW_GROUPS) for r in range(ROW_GROUPS)]

    def mix_in(rs, is_ctx):
        a = _pick((ac_ref, al_ref), is_ctx, rs)
        b = _pick((bc_ref, bl_ref), is_ctx, rs)
        pieces = ([a[:, c:c + N_CHUNK] for c in range(0, ka, N_CHUNK)]
                  + [b[:, c:c + N_CHUNK] for c in range(0, kb, N_CHUNK)])
        h = functools.reduce(lambda s, p: s + p, [_dot(p, wo_s[c]) for c, p in enumerate(pieces)])
        x1 = _ln(ALPHA * _pick(x_refs, is_ctx, rs) + gate1 * h) * lng_ref[0, 0:1, :] + lnb_ref[0, 0:1, :]
        x1_ref[rs, :] = x1
        u_ref[rs, :] = (_ln(x1) * (1.0 + scale2) + shift2).astype(bf16)

    def finish(rs, ffn, is_ctx):
        y = _ln(ALPHA * x1_ref[rs, :] + gate2 * ffn) * lng_ref[0, 1:2, :] + lnb_ref[0, 1:2, :]
        y_refs[0 if (n_y == 1 or is_ctx) else 1][rs, :] = y

    @pl.when(i < n_w)
    def _():
        wg_s[i] = wg_ref[0].astype(bf16)
        wu_s[i] = wu_ref[0].astype(bf16)
        wd_s[i] = wd_ref[0].astype(bf16)

        @pl.when(i == 0)
        def _():
            for c in range((ka + kb) // N_CHUNK):
                wo_s[c] = wo_ref[0, c * N_CHUNK:(c + 1) * N_CHUNK, :].astype(bf16)
            for rs in groups:
                mix_in(rs, True)
                y_ref[rs, :] = jnp.zeros((TM // ROW_GROUPS, D_MODEL), f32)

        g = _dot(u_ref[...], wg_s[i])
        up = _dot(u_ref[...], wu_s[i])
        y_ref[...] += _dot((_silu(g) * up).astype(bf16), wd_s[i])

        @pl.when(i == lead)
        def _():
            for rs in groups:
                finish(rs, y_ref[rs, :], True)

    def token_block(is_ctx):
        for rs in groups:
            mix_in(rs, is_ctx)

        def ffn_chunk(rs, c):
            g = _dot(u_ref[rs, :], wg_s[c])
            up = _dot(u_ref[rs, :], wu_s[c])
            h_ref[rs, c * N_CHUNK:(c + 1) * N_CHUNK] = (_silu(g) * up).astype(bf16)

        def ffn_down(rs):
            finish(rs, functools.reduce(
                lambda s, p: s + p,
                [_dot(h_ref[rs, c * N_CHUNK:(c + 1) * N_CHUNK], wd_s[c]) for c in range(n_w)]), is_ctx)

        for c in range(n_w + FFN_SKEW * (ROW_GROUPS - 1)):
            for r, rs in enumerate(groups):
                cc = c - FFN_SKEW * r
                if 0 <= cc < n_w:
                    ffn_chunk(rs, cc)
                if cc == n_w - 1:
                    ffn_down(rs)

    blk = i - lead

    @pl.when(jnp.logical_and(i >= n_w, blk < CTX_BLOCKS))
    def _():
        token_block(True)

    @pl.when(blk >= CTX_BLOCKS)
    def _():
        token_block(False)


def _post(x_parts, mix_a, mix_b, mod, ln_g, ln_b, w_out, w_gate, w_up, w_down, layer, mix_layer, split_out):
    ka, kb = mix_a[0].shape[1], mix_b[0].shape[1]
    n_w = D_FF // N_CHUNK
    lead = n_w - 1
    tok = lambda i: _tok(i, lead)
    lay = lambda i: (layer, 0, 0)
    in_specs = (_token_specs(x_parts, lead) + _token_specs(mix_a, lead) + _token_specs(mix_b, lead) + [
        pl.BlockSpec((None, 1, 6, D_MODEL), lambda i: (layer, _mod_row(tok(i)), 0, 0)),
        pl.BlockSpec((1, 2, D_MODEL), lay),
        pl.BlockSpec((1, 2, D_MODEL), lay),
        pl.BlockSpec((1, ka + kb, D_MODEL), lambda i: (mix_layer, 0, 0), pipeline_mode=pl.Buffered(1)),
        pl.BlockSpec((1, D_MODEL, N_CHUNK), lambda i: (layer, 0, jnp.minimum(i, n_w - 1))),
        pl.BlockSpec((1, D_MODEL, N_CHUNK), lambda i: (layer, 0, jnp.minimum(i, n_w - 1))),
        pl.BlockSpec((1, N_CHUNK, D_MODEL), lambda i: (layer, jnp.minimum(i, n_w - 1), 0))])
    if split_out:
        out_specs = [pl.BlockSpec((TM, D_MODEL), lambda i: (_ctx_blk(tok(i)), 0)),
                     pl.BlockSpec((TM, D_MODEL), lambda i: (_lat_blk(tok(i)), 0))]
        out_shape = [jax.ShapeDtypeStruct((N_CTX, D_MODEL), f32), jax.ShapeDtypeStruct((N_LAT, D_MODEL), f32)]
    else:
        out_specs = [pl.BlockSpec((TM, D_MODEL), lambda i: (tok(i), 0))]
        out_shape = [jax.ShapeDtypeStruct((N_TOK, D_MODEL), f32)]
    return pl.pallas_call(
        functools.partial(_post_kernel, n_x=len(x_parts), n_y=len(out_shape), ka=ka, kb=kb, n_w=n_w),
        grid=(lead + TOK_BLOCKS,),
        in_specs=in_specs,
        out_specs=out_specs,
        out_shape=out_shape,
        scratch_shapes=[pltpu.VMEM(((ka + kb) // N_CHUNK, N_CHUNK, D_MODEL), bf16),
                        pltpu.VMEM((n_w, D_MODEL, N_CHUNK), bf16),
                        pltpu.VMEM((n_w, D_MODEL, N_CHUNK), bf16), pltpu.VMEM((n_w, N_CHUNK, D_MODEL), bf16),
                        pltpu.VMEM((TM, D_MODEL), f32), pltpu.VMEM((TM, D_MODEL), bf16),
                        pltpu.VMEM((TM, D_FF), bf16), pltpu.VMEM((TM, D_MODEL), f32)],
        compiler_params=_params(1),
        name="post",
    )(*x_parts, *mix_a, *mix_b, mod, ln_g, ln_b, w_out, w_gate, w_up, w_down)


def _group_norm_gate(ro, rg, gmat, gn_g, gn_b):
    def gmean(parts):
        cols = []
        for c in range(0, RET_W, N_CHUNK):
            cols.append(sum(_dot(p[:, c:c + N_CHUNK], gmat) for p in parts))
        return jnp.concatenate(cols, -1)

    d = ro - gmean(_split_bf16(ro))
    var = gmean([(d * d).astype(bf16)])
    y = d * lax.rsqrt(var + LN_EPS) * gn_g + gn_b
    return _silu(rg.astype(f32)) * y


def _dup_head(x, j):
    first = _lane_half_mask(x.shape)
    keep = first if j == 0 else jnp.logical_not(first)
    xm = jnp.where(keep, x.astype(f32), 0.0)
    return xm + pltpu.roll(xm, HEAD_DIM, 1)


def _softmax_parts(scores, sink):
    m = sink
    for s in scores:
        m = jnp.maximum(m, jnp.max(s, -1, keepdims=True))
    es = [jnp.exp2(s - m) for s in scores]
    denom = jnp.exp2(sink - m)
    for e in es:
        denom = denom + jnp.sum(e, -1, keepdims=True)
    return es, denom


def _retention_tables(lg_ref, lgf_ref, lgb_ref, dmask_ref, kdec_ref, n):
    row = lax.broadcasted_iota(jnp.int32, (n, n), 0)
    col = lax.broadcasted_iota(jnp.int32, (n, n), 1)
    diff = (row - col).astype(f32)
    diag = jnp.where(row == col, 2.0 * QK_SCALE, QK_SCALE)
    for h in range(H_RET):
        dmask_ref[h] = jnp.exp(jnp.where(diff >= 0, lg_ref[0, h] * diff, -lg_ref[1, h] * diff)) * diag
    t = lax.broadcasted_iota(jnp.int32, (n, RET_W), 0).astype(f32)
    kdec_ref[0] = jnp.exp(lgf_ref[...] * (n - 1.0 - t)) * QK_SCALE
    kdec_ref[1] = jnp.exp(lgb_ref[...] * t) * QK_SCALE


def _retention_intra(pairs, q_of, k_of, v_of, dmask_ref):
    first = _lane_half_mask(k_of(pairs[0]).shape)
    masked = {}
    for p in pairs:
        kb = k_of(p)
        for e in range(2):
            keep = first if e == 0 else jnp.logical_not(first)
            s = _dot_nt(q_of(p), jnp.where(keep, kb, jnp.zeros_like(kb))) * dmask_ref[2 * p + e]
            masked[p, e] = s.astype(bf16)
    outs = {}
    for p in pairs:
        pv = [_dot(masked[p, e], v_of(p)) for e in range(2)]
        outs[p] = jnp.where(_lane_half_mask(pv[0].shape), pv[0], pv[1])
    return outs


def _window_group(subs, q_of, k_parts_of, v_parts_of, masks, sink_of):
    scores = {}
    for key in subs:
        parts = [_dot_nt(q_of(key), k) for k in k_parts_of(key)]
        scores[key] = [sc if mk is None else jnp.where(mk, sc, NEG_BIG) for sc, mk in zip(parts, masks)]
    probs = {}
    for key in subs:
        es, denom = _softmax_parts(scores[key], sink_of(key))
        probs[key] = ([ex.astype(bf16) for ex in es], denom)
    outs = {}
    for key in subs:
        es, denom = probs[key]
        pv = functools.reduce(lambda x, y: x + y, [_dot(ex, v) for ex, v in zip(es, v_parts_of(key))])
        outs[key] = pv / denom
    return outs


def _ctx_ab_kernel(lg_ref, sink_ref, rq_ref, rk_ref, rv_ref, rg_ref, wq_ref, wk_ref, wv_ref,
                   lgf_ref, lgb_ref, gmat_ref, gng_ref, gnb_ref,
                   ro_ref, wo_ref, st_ref, dmask_ref, kdec_ref, ret_ref):
    t_len = SEQ

    @pl.when(pl.program_id(0) == 0)
    def _():
        _retention_tables(lg_ref, lgf_ref, lgb_ref, dmask_ref, kdec_ref, t_len)

    first = _lane_half_mask((t_len, PAIR_W))
    for sq in range(CTX_SEQS):
        rows = slice(sq * t_len, (sq + 1) * t_len)
        psl = lambda p: slice(p * PAIR_W, (p + 1) * PAIR_W)
        for p0 in range(0, H_RET // 2, RET_GROUP):
            pairs = list(range(p0, p0 + RET_GROUP))
            intra = _retention_intra(pairs, lambda p: rq_ref[rows, psl(p)], lambda p: rk_ref[rows, psl(p)],
                                     lambda p: rv_ref[rows, psl(p)], dmask_ref)
            for p in pairs:
                ret_ref[rows, psl(p)] = intra[p]
        for p in range(H_RET // 2):
            sl = psl(p)
            kb = rk_ref[rows, sl]
            v = rv_ref[rows, sl]
            for d in range(2):
                kd_t = (kb * kdec_ref[d, :, sl]).T.astype(bf16)
                st = _dot(kd_t, v)
                st_ref[sq, d, 2 * p] = st[0:HEAD_DIM, 0:HEAD_DIM]
                st_ref[sq, d, 2 * p + 1] = pltpu.roll(st[HEAD_DIM:, :], HEAD_DIM, 1)[:, 0:HEAD_DIM]
        ro_ref[rows, :] = _group_norm_gate(ret_ref[rows, :], rg_ref[rows, :], gmat_ref[...], gng_ref[...],
                                           gnb_ref[...]).astype(bf16)

        k_dup = [_dup_head(wk_ref[rows, :], j).astype(bf16) for j in range(KV_WIN)]
        v_dup = [_dup_head(wv_ref[rows, :], j).astype(bf16) for j in range(KV_WIN)]

        def q_masked(key):
            qp, e = key
            qb = wq_ref[rows, qp * PAIR_W:(qp + 1) * PAIR_W]
            return jnp.where(first if e == 0 else jnp.logical_not(first), qb, jnp.zeros_like(qb))

        kv_of = lambda key: key[0] * 2 // G_WIN
        for g0 in range(0, H_WIN // 2, WIN_GROUP):
            subs = [(qp, e) for qp in range(g0, g0 + WIN_GROUP) for e in range(2)]
            outs = _window_group(subs, q_masked, lambda key: [k_dup[kv_of(key)]], lambda key: [v_dup[kv_of(key)]],
                                 [None], lambda key: sink_ref[0, 2 * key[0] + key[1]] * LOG2_E)
            for qp in range(g0, g0 + WIN_GROUP):
                wo_ref[rows, qp * PAIR_W:(qp + 1) * PAIR_W] = jnp.where(first, outs[qp, 0], outs[qp, 1]).astype(bf16)


def _ctx_ab(proj, log_gamma, sink, lgf_lanes, lgb_lanes, gmat, gn_g, gn_b):
    t = SEQ
    tb = CTX_SEQS * t
    smem = pl.BlockSpec(memory_space=pltpu.SMEM)
    const = lambda b: (0, 0)
    col = lambda c: (lambda b: (b, c))
    return pl.pallas_call(
        _ctx_ab_kernel,
        grid=(BATCH // CTX_SEQS,),
        in_specs=[smem, smem,
                  pl.BlockSpec((tb, RET_W), col(0)), pl.BlockSpec((tb, RET_W), col(1)),
                  pl.BlockSpec((tb, RET_W), col(2)), pl.BlockSpec((tb, RET_W), col(3)),
                  pl.BlockSpec((tb, WIN_W), col(4)),
                  pl.BlockSpec((tb, KV_W), col((4 * RET_W + WIN_W) // KV_W)),
                  pl.BlockSpec((tb, KV_W), col((4 * RET_W + WIN_W) // KV_W + 1)),
                  pl.BlockSpec((1, RET_W), const), pl.BlockSpec((1, RET_W), const),
                  pl.BlockSpec((N_CHUNK, N_CHUNK), const),
                  pl.BlockSpec((1, RET_W), const), pl.BlockSpec((1, RET_W), const)],
        out_specs=[pl.BlockSpec((tb, RET_W), lambda b: (b, 0)),
                   pl.BlockSpec((tb, WIN_W), lambda b: (b, 0)),
                   pl.BlockSpec((CTX_SEQS, 2, H_RET, HEAD_DIM, HEAD_DIM), lambda b: (b, 0, 0, 0, 0))],
        out_shape=[jax.ShapeDtypeStruct((BATCH * t, RET_W), bf16),
                   jax.ShapeDtypeStruct((BATCH * t, WIN_W), bf16),
                   jax.ShapeDtypeStruct((BATCH, 2, H_RET, HEAD_DIM, HEAD_DIM), f32)],
        scratch_shapes=[pltpu.VMEM((H_RET, t, t), f32), pltpu.VMEM((2, t, RET_W), f32),
                        pltpu.VMEM((tb, RET_W), f32)],
        compiler_params=_params(1),
        name="ctx_ab",
    )(log_gamma, sink, proj, proj, proj, proj, proj, proj, proj, lgf_lanes, lgb_lanes, gmat, gn_g, gn_b)


def _pair_state(s0_ref, d, p):
    zero = jnp.zeros((HEAD_DIM, HEAD_DIM), f32)
    top = jnp.concatenate([s0_ref[0, 0, d, 2 * p], zero], 1)
    bottom = jnp.concatenate([zero, s0_ref[0, 0, d, 2 * p + 1]], 1)
    return jnp.concatenate([top, bottom], 0)


def _lat_ab_kernel(lg_ref, sink_ref, rq_ref, rk_ref, rv_ref, rg_ref, wq_ref, wk_ref, wv_ref, ck_ref, cv_ref,
                   s0_ref, lgf_ref, lgb_ref, gmat_ref, gng_ref, gnb_ref,
                   ro_ref, wo_ref, ret_ref, dmask_ref, kdec_ref, qdec_ref, sf_ref, sb_ref):
    t_len = DEC_SEQ
    n_chunks = t_len // TQ
    step = pl.program_id(1)
    first = _lane_half_mask((TQ, PAIR_W))

    @pl.when(jnp.logical_and(pl.program_id(0) == 0, step == 0))
    def _():
        _retention_tables(lg_ref, lgf_ref, lgb_ref, dmask_ref, kdec_ref, TQ)
        t = lax.broadcasted_iota(jnp.int32, (TQ, RET_W), 0).astype(f32)
        qdec_ref[0] = jnp.exp(lgf_ref[...] * (t + 1.0))
        qdec_ref[1] = jnp.exp(lgb_ref[...] * (TQ - t))

    @pl.when(step == 0)
    def _():
        r = lax.broadcasted_iota(jnp.int32, (PAIR_W, PAIR_W), 0)
        c_ = lax.broadcasted_iota(jnp.int32, (PAIR_W, PAIR_W), 1)
        same_head = (r < HEAD_DIM) == (c_ < HEAD_DIM)
        for p in range(H_RET // 2):
            sl = slice(p * PAIR_W, (p + 1) * PAIR_W)
            kv = []
            for c in range(n_chunks):
                rows = slice(c * TQ, (c + 1) * TQ)
                kc = rk_ref[rows, sl]
                vc = rv_ref[rows, sl]
                kv.append([jnp.where(same_head, _dot((kc * kdec_ref[d, :, sl]).T.astype(bf16), vc), 0.0)
                           for d in range(2)])
            state = _pair_state(s0_ref, 0, p)
            for c in range(n_chunks):
                sf_ref[c, p] = state
                state = state * jnp.exp(lgf_ref[:, sl] * TQ) + kv[c][0]
            state = _pair_state(s0_ref, 1, p)
            for c in reversed(range(n_chunks)):
                sb_ref[c, p] = state
                state = state * jnp.exp(lgb_ref[:, sl] * TQ) + kv[c][1]

    for sub in range(LAT_SUB):
        chunk = step * LAT_SUB + sub
        q0 = pl.multiple_of(chunk * TQ, TQ)
        rows = slice(sub * TQ, (sub + 1) * TQ)
        psl = lambda p: slice(p * PAIR_W, (p + 1) * PAIR_W)
        intra = {}
        for p0 in range(0, H_RET // 2, RET_GROUP):
            intra.update(_retention_intra(list(range(p0, p0 + RET_GROUP)), lambda p: rq_ref[rows, psl(p)],
                                          lambda p: rk_ref[pl.ds(q0, TQ), psl(p)],
                                          lambda p: rv_ref[pl.ds(q0, TQ), psl(p)], dmask_ref))
        for p in range(H_RET // 2):
            sl = psl(p)
            q = rq_ref[rows, sl]
            o = intra[p]
            o = o + _dot(q, sf_ref[chunk, p].astype(bf16)) * qdec_ref[0, :, sl]
            o = o + _dot(q, sb_ref[chunk, p].astype(bf16)) * qdec_ref[1, :, sl]
            ret_ref[rows, sl] = o
        ro_ref[rows, :] = _group_norm_gate(ret_ref[rows, :], rg_ref[rows, :], gmat_ref[...], gng_ref[...],
                                           gnb_ref[...]).astype(bf16)

        band = TQ + 2 * WINDOW
        k_start = pl.multiple_of(jnp.clip(q0 - WINDOW, 0, t_len - band), LANES)
        qi = q0 + lax.broadcasted_iota(jnp.int32, (TQ, band), 0)
        kj = k_start + lax.broadcasted_iota(jnp.int32, (TQ, band), 1)
        in_band = jnp.abs(qi - kj) <= WINDOW
        k_parts = [[_dup_head(wk_ref[pl.ds(k_start, band), :], j).astype(bf16), _dup_head(ck_ref[0], j).astype(bf16)]
                   for j in range(KV_WIN)]
        v_parts = [[_dup_head(wv_ref[pl.ds(k_start, band), :], j).astype(bf16), _dup_head(cv_ref[0], j).astype(bf16)]
                   for j in range(KV_WIN)]

        def q_masked(key):
            qp, e = key
            qb = wq_ref[rows, qp * PAIR_W:(qp + 1) * PAIR_W]
            return jnp.where(first if e == 0 else jnp.logical_not(first), qb, jnp.zeros_like(qb))

        kv_of = lambda key: key[0] * 2 // G_WIN
        for g0 in range(0, H_WIN // 2, WIN_GROUP_LAT):
            subs = [(qp, e) for qp in range(g0, g0 + WIN_GROUP_LAT) for e in range(2)]
            outs = _window_group(subs, q_masked, lambda key: k_parts[kv_of(key)], lambda key: v_parts[kv_of(key)],
                                 [in_band, None], lambda key: sink_ref[0, 2 * key[0] + key[1]] * LOG2_E)
            for qp in range(g0, g0 + WIN_GROUP_LAT):
                wo_ref[rows, qp * PAIR_W:(qp + 1) * PAIR_W] = jnp.where(first, outs[qp, 0], outs[qp, 1]).astype(bf16)


def _lat_ab(proj, log_gamma, sink, ck, cv, state, layer, lgf_lanes, lgb_lanes, gmat, gn_g, gn_b):
    t = DEC_SEQ
    tb = LAT_SUB * TQ
    nq = t // tb
    smem = pl.BlockSpec(memory_space=pltpu.SMEM)
    const = lambda b, i: (0, 0)
    qcol = lambda c: (lambda b, i: (N_CTX // tb + b * nq + i, c))
    bcol = lambda c: (lambda b, i: (N_CTX // t + b, c))
    kv_col = (4 * RET_W + WIN_W) // KV_W
    return pl.pallas_call(
        _lat_ab_kernel,
        grid=(DEC_BATCH, nq),
        in_specs=[smem, smem,
                  pl.BlockSpec((tb, RET_W), qcol(0)), pl.BlockSpec((t, RET_W), bcol(1)),
                  pl.BlockSpec((t, RET_W), bcol(2)), pl.BlockSpec((tb, RET_W), qcol(3)),
                  pl.BlockSpec((tb, WIN_W), qcol(4)),
                  pl.BlockSpec((t, KV_W), bcol(kv_col)), pl.BlockSpec((t, KV_W), bcol(kv_col + 1)),
                  pl.BlockSpec((1, PAST_LEN, KV_W), lambda b, i: (b, 0, 0)),
                  pl.BlockSpec((1, PAST_LEN, KV_W), lambda b, i: (b, 0, 0)),
                  pl.BlockSpec((1, 1, 2, H_RET, HEAD_DIM, HEAD_DIM), lambda b, i: (b, layer, 0, 0, 0, 0)),
                  pl.BlockSpec((1, RET_W), const), pl.BlockSpec((1, RET_W), const),
                  pl.BlockSpec((N_CHUNK, N_CHUNK), const),
                  pl.BlockSpec((1, RET_W), const), pl.BlockSpec((1, RET_W), const)],
        out_specs=[pl.BlockSpec((tb, RET_W), lambda b, i: (b * nq + i, 0)),
                   pl.BlockSpec((tb, WIN_W), lambda b, i: (b * nq + i, 0))],
        out_shape=[jax.ShapeDtypeStruct((DEC_BATCH * t, RET_W), bf16),
                   jax.ShapeDtypeStruct((DEC_BATCH * t, WIN_W), bf16)],
        scratch_shapes=[pltpu.VMEM((tb, RET_W), f32), pltpu.VMEM((H_RET, TQ, TQ), f32),
                        pltpu.VMEM((2, TQ, RET_W), f32), pltpu.VMEM((2, TQ, RET_W), f32),
                        pltpu.VMEM((t // TQ, H_RET // 2, PAIR_W, PAIR_W), f32),
                        pltpu.VMEM((t // TQ, H_RET // 2, PAIR_W, PAIR_W), f32)],
        compiler_params=_params(2),
        name="lat_ab",
    )(log_gamma, sink, proj, proj, proj, proj, proj, proj, proj, ck, cv, state,
      lgf_lanes, lgb_lanes, gmat, gn_g, gn_b)


def _lambda_full(lam_ref, lam_init):
    lam = lam_ref[...]
    a = jnp.sum(lam[0:1, :] * lam[1:2, :], -1, keepdims=True)
    b = jnp.sum(lam[2:3, :] * lam[3:4, :], -1, keepdims=True)
    return jnp.exp(a) - jnp.exp(b) + lam_init


def _diff_heads(q_of, k_parts_of, v_parts_of, lam, subln, lam_init, group):
    res = []
    for h0 in range(0, H_DIFF, group):
        res += _diff_head_group(range(h0, h0 + group), q_of, k_parts_of, v_parts_of, lam, subln, lam_init)
    return res


def _diff_head_group(heads, q_of, k_parts_of, v_parts_of, lam, subln, lam_init):
    subs = [(h, e) for h in heads for e in range(2)]
    scores = {}
    for h, e in subs:
        q = q_of(h)
        fm = _lane_half_mask(q.shape)
        q_sub = jnp.where(fm if e == 0 else jnp.logical_not(fm), q, jnp.zeros_like(q))
        scores[h, e] = [_dot(q_sub, k) if transposed else _dot_nt(q_sub, k) for k, transposed in k_parts_of(h)]
    probs = {}
    for key in subs:
        m = scores[key][0].max(-1, keepdims=True)
        for sc in scores[key][1:]:
            m = jnp.maximum(m, sc.max(-1, keepdims=True))
        es = [jnp.exp2(sc - m) for sc in scores[key]]
        denom = es[0].sum(-1, keepdims=True)
        for ex in es[1:]:
            denom = denom + ex.sum(-1, keepdims=True)
        probs[key] = ([ex.astype(bf16) for ex in es], denom)
    outs = {}
    for h, e in subs:
        es, denom = probs[h, e]
        pv = functools.reduce(lambda x, y: x + y, [_dot(ex, v) for v, ex in zip(v_parts_of(h), es)])
        outs[h, e] = pv / denom
    res = []
    for h in heads:
        a = outs[h, 0] - lam * outs[h, 1]
        res.append(a * lax.rsqrt(jnp.mean(a * a, -1, keepdims=True) + LN_EPS) * subln * (1.0 - lam_init))
    return res


def _fourier_rows(ct_ref, st_ref, z, bdc_ref, bds_ref):
    zc = _dot(z, bdc_ref[...].astype(bf16)).astype(bf16)
    zs = _dot(z, bds_ref[...].astype(bf16)).astype(bf16)
    return _dot(ct_ref[...].astype(bf16), zc) - _dot(st_ref[...].astype(bf16), zs)


def _ctx_cd_kernel(q_ref, k_ref, v_ref, z_ref, lam_ref, subln_ref, ct_ref, st_ref, bdc_ref, bds_ref,
                   a_ref, zf_ref, *, lam_init):
    lam = _lambda_full(lam_ref, lam_init)
    for sq in range(CTX_SEQS_CD):
        rows = slice(sq * SEQ, (sq + 1) * SEQ)
        sl = lambda h: slice(h * PAIR_W, (h + 1) * PAIR_W)
        heads = _diff_heads(lambda h: q_ref[rows, sl(h)], lambda h: [(k_ref[rows, sl(h)], False)],
                            lambda h: [v_ref[rows, sl(h)]], lam, subln_ref[...], lam_init, DIFF_GROUP)
        for h in range(H_DIFF):
            a_ref[rows, sl(h)] = heads[h].astype(bf16)
        zf_ref[rows, :] = _fourier_rows(ct_ref, st_ref, z_ref[rows, :], bdc_ref, bds_ref).astype(bf16)


def _ctx_cd(proj, lam, subln, ct, st, bdc, bds, lam_init):
    t = SEQ
    tb = CTX_SEQS_CD * t
    const = lambda b: (0, 0)
    col = lambda c: (lambda b: (b, c))
    return pl.pallas_call(
        functools.partial(_ctx_cd_kernel, lam_init=lam_init),
        grid=(BATCH // CTX_SEQS_CD,),
        in_specs=[pl.BlockSpec((tb, DIFF_W), col(0)), pl.BlockSpec((tb, DIFF_W), col(1)),
                  pl.BlockSpec((tb, DIFF_W), col(2)), pl.BlockSpec((tb, FNET_W), col(3 * DIFF_W // FNET_W)),
                  pl.BlockSpec((4, HEAD_DIM), const), pl.BlockSpec((1, PAIR_W), const),
                  pl.BlockSpec((t, t), const), pl.BlockSpec((t, t), const),
                  pl.BlockSpec((FNET_W, FNET_W), const), pl.BlockSpec((FNET_W, FNET_W), const)],
        out_specs=[pl.BlockSpec((tb, DIFF_W), lambda b: (b, 0)), pl.BlockSpec((tb, FNET_W), lambda b: (b, 0))],
        out_shape=[jax.ShapeDtypeStruct((BATCH * t, DIFF_W), bf16),
                   jax.ShapeDtypeStruct((BATCH * t, FNET_W), bf16)],
        compiler_params=_params(1),
        name="ctx_cd",
    )(proj, proj, proj, proj, lam, subln, ct, st, bdc, bds)


def _lat_cd_kernel(q_ref, k_ref, v_ref, z_ref, ckt_ref, cv_ref, lam_ref, subln_ref, ct_ref, st_ref, bdc_ref, bds_ref,
                   a_ref, zf_ref, *, lam_init):
    lam = _lambda_full(lam_ref, lam_init)
    sl = lambda h: slice(h * PAIR_W, (h + 1) * PAIR_W)
    heads = _diff_heads(lambda h: q_ref[:, sl(h)],
                        lambda h: [(k_ref[:, sl(h)], False), (ckt_ref[0, h].astype(bf16), True)],
                        lambda h: [v_ref[:, sl(h)], cv_ref[0, h].astype(bf16)], lam, subln_ref[...], lam_init,
                        DIFF_GROUP_LAT)
    for h in range(H_DIFF):
        a_ref[:, sl(h)] = heads[h].astype(bf16)
    zf_ref[...] = _fourier_rows(ct_ref, st_ref, z_ref[...], bdc_ref, bds_ref).astype(bf16)


def _lat_cd(proj, ck, cv, lam, subln, ct, st, bdc, bds, lam_init):
    t = DEC_SEQ
    nq = t // TQ_CD
    const = lambda b, i: (0, 0)
    return pl.pallas_call(
        functools.partial(_lat_cd_kernel, lam_init=lam_init),
        grid=(DEC_BATCH, nq),
        in_specs=[pl.BlockSpec((TQ_CD, DIFF_W), lambda b, i: (N_CTX // TQ_CD + b * nq + i, 0)),
                  pl.BlockSpec((t, DIFF_W), lambda b, i: (N_CTX // t + b, 1)),
                  pl.BlockSpec((t, DIFF_W), lambda b, i: (N_CTX // t + b, 2)),
                  pl.BlockSpec((t, FNET_W), lambda b, i: (N_CTX // t + b, 3 * DIFF_W // FNET_W)),
                  pl.BlockSpec((1, H_DIFF, PAIR_W, PAST_LEN), lambda b, i: (b, 0, 0, 0)),
                  pl.BlockSpec((1, H_DIFF, PAST_LEN, PAIR_W), lambda b, i: (b, 0, 0, 0)),
                  pl.BlockSpec((4, HEAD_DIM), const), pl.BlockSpec((1, PAIR_W), const),
                  pl.BlockSpec((TQ_CD, t), lambda b, i: (i, 0)), pl.BlockSpec((TQ_CD, t), lambda b, i: (i, 0)),
                  pl.BlockSpec((FNET_W, FNET_W), const), pl.BlockSpec((FNET_W, FNET_W), const)],
        out_specs=[pl.BlockSpec((TQ_CD, DIFF_W), lambda b, i: (b * nq + i, 0)),
                   pl.BlockSpec((TQ_CD, FNET_W), lambda b, i: (b * nq + i, 0))],
        out_shape=[jax.ShapeDtypeStruct((DEC_BATCH * t, DIFF_W), bf16),
                   jax.ShapeDtypeStruct((DEC_BATCH * t, FNET_W), bf16)],
        compiler_params=_params(2),
        name="lat_cd",
    )(proj, proj, proj, proj, ck, cv, lam, subln, ct, st, bdc, bds)


def _rope_tables():
    t = np.arange(DEC_SEQ)
    quarter = HEAD_DIM // 4
    inv = ROPE_BASE ** (-np.arange(quarter, dtype=np.float64) / quarter)
    ang = np.concatenate([(t // GRID_W)[:, None] * inv, (t % GRID_W)[:, None] * inv], -1)
    cos, sin = np.cos(ang), np.sin(ang)
    reps = LANES // HEAD_DIM
    return (np.tile(np.concatenate([cos, cos], -1), (1, reps)).astype(np.float32),
            np.tile(np.concatenate([-sin, sin], -1), (1, reps)).astype(np.float32))


def _dft_tables(n):
    k = np.arange(n)
    ang = (2.0 * math.pi / n) * ((k[:, None] * k[None, :]) % n)
    return (np.cos(ang) / math.sqrt(n)).astype(np.float32), (np.sin(ang) / math.sqrt(n)).astype(np.float32)


def _block_diag(m, reps):
    return np.kron(np.eye(reps, dtype=m.dtype), m)


def kernel(x_prompt, x_sample, state_ret, cache_win_k, cache_win_v, cache_diff_k, cache_diff_v, c, c_ctx, w_mod, b_mod, ln_g, ln_b, w_in_ab, w_out_ab, ret_log_gamma, ret_gn_g, ret_gn_b, win_sink, w_in_cd, w_out_cd, diff_lambda, diff_subln_g, w_gate, w_up, w_down):
    cond = jnp.concatenate([c_ctx[None, :], c, jnp.zeros((SUBLANES - 1 - DEC_BATCH, D_MODEL), f32)], 0)
    mod = _modulation(cond, w_mod, b_mod).reshape(DEPTH, SUBLANES, 6, D_MODEL)

    rope_tabs = _rope_tables()
    gmat = jnp.asarray(_block_diag(np.full((HEAD_DIM, HEAD_DIM), 1.0 / HEAD_DIM, np.float32),
                                   N_CHUNK // HEAD_DIM), bf16)
    c64, s64 = _dft_tables(FNET_DIM)
    bdc = _block_diag(c64, FNET_GROUPS)
    bds = _block_diag(s64, FNET_GROUPS)
    dft_ctx = _dft_tables(SEQ)
    dft_lat = _dft_tables(DEC_SEQ)

    x_parts = [x_prompt.reshape(N_CTX, D_MODEL), x_sample.reshape(N_LAT, D_MODEL)]
    outs = {}
    for l in range(DEPTH):
        i = l // 2
        if l % 2 == 0:
            lgf = jnp.repeat(ret_log_gamma[i, 0], HEAD_DIM)[None, :]
            lgb = jnp.repeat(ret_log_gamma[i, 1], HEAD_DIM)[None, :]
            gn_g = ret_gn_g[i][None, :]
            gn_b = ret_gn_b[i][None, :]
            sink = win_sink[i][None, :]
            rope_tiles = tuple(range(0, 2 * RET_W // LANES)) + tuple(
                range(4 * RET_W // LANES, (4 * RET_W + WIN_W + KV_W) // LANES))
            kv_tile = (4 * RET_W + WIN_W) // LANES
            kv_shape = (BATCH, 1, KV_WIN, HEAD_DIM, SEQ)
            scale_tiles = tuple(range(4 * RET_W // LANES, (4 * RET_W + WIN_W) // LANES))
            proj, wk_t, wv_t = _proj(x_parts, mod, l, w_in_ab, i, scale_tiles, rope_tabs, rope_tiles,
                                     (kv_shape, kv_shape),
                                     {kv_tile: ("heads", 0, 0), kv_tile + 1: ("heads", 1, 0)})
            ro_c, wo_c, st_c = _ctx_ab(proj, ret_log_gamma[i], sink, lgf, lgb, gmat, gn_g, gn_b)
            ck = cache_win_k[:, i].reshape(DEC_BATCH, PAST_LEN, KV_W)
            cv = cache_win_v[:, i].reshape(DEC_BATCH, PAST_LEN, KV_W)
            ro_l, wo_l = _lat_ab(proj, ret_log_gamma[i], sink, ck, cv, state_ret, i, lgf, lgb, gmat, gn_g, gn_b)
            mix_a, mix_b, w_out = (ro_c, ro_l), (wo_c, wo_l), w_out_ab
            outs.setdefault('state', []).append(st_c[:, None])
            outs.setdefault('win_k', []).append(jnp.transpose(wk_t, (0, 1, 4, 2, 3)))
            outs.setdefault('win_v', []).append(jnp.transpose(wv_t, (0, 1, 4, 2, 3)))
        else:
            lam_init = 0.8 - 0.6 * math.exp(-0.3 * l)
            subln = diff_subln_g[i][None, :]
            rope_tiles = tuple(range(0, 2 * DIFF_W // LANES))
            plan = {}
            for h in range(H_DIFF):
                plan[DIFF_W // LANES + h] = ("pairs", 0, h)
                plan[2 * DIFF_W // LANES + h] = ("plain", 1, h)
            scale_tiles = tuple(range(0, DIFF_W // LANES))
            proj, dk_t, dv_h = _proj(
                x_parts, mod, l, w_in_cd, i, scale_tiles, rope_tabs, rope_tiles,
                ((BATCH, 1, H_DIFF, 2, HEAD_DIM, SEQ), (BATCH, 1, H_DIFF, SEQ, 2 * HEAD_DIM)), plan)
            a_c, z_c = _ctx_cd(proj, diff_lambda[i], subln, dft_ctx[0], dft_ctx[1], bdc, bds, lam_init)
            ck = jnp.transpose(cache_diff_k[:, i], (0, 2, 3, 4, 1)).reshape(DEC_BATCH, H_DIFF, PAIR_W, PAST_LEN)
            cv = jnp.transpose(cache_diff_v[:, i], (0, 2, 1, 3))
            a_l, z_l = _lat_cd(proj, ck, cv, diff_lambda[i], subln, dft_lat[0], dft_lat[1], bdc, bds, lam_init)
            mix_a, mix_b, w_out = (a_c, a_l), (z_c, z_l), w_out_cd
            outs.setdefault('diff_k', []).append(jnp.transpose(dk_t, (0, 1, 5, 2, 3, 4)))
            outs.setdefault('diff_v', []).append(jnp.transpose(dv_h, (0, 1, 3, 2, 4)))
        x_parts = _post(x_parts, mix_a, mix_b, mod, ln_g, ln_b, w_out, w_gate, w_up, w_down, l, i,
                        split_out=(l == DEPTH - 1))

    y_prompt = x_parts[0].reshape(BATCH, SEQ, D_MODEL)
    y_sample = x_parts[1].reshape(DEC_BATCH, DEC_SEQ, D_MODEL)
    cat = lambda parts: parts[0] if len(parts) == 1 else jnp.concatenate(parts, 1)
    return (y_prompt, y_sample, cat(outs['state']), cat(outs['win_k']), cat(outs['win_v']),
            cat(outs['diff_k']), cat(outs['diff_v']))
```

```python
import functools
import math

import jax
import jax.numpy as jnp
import numpy as np
from jax import lax
from jax.experimental import pallas as pl
from jax.experimental.pallas import tpu as pltpu

D_MODEL = 1024
BATCH = 32
SEQ = 256
DEPTH = 2
DEC_BATCH = 2
DEC_SEQ = 1024
PAST_LEN = 512
GRID_W = 64
HEAD_DIM = 64
ROPE_BASE = 10000.0
H_RET = 8
H_WIN = 8
KV_WIN = 2
G_WIN = H_WIN // KV_WIN
WINDOW = 128
H_DIFF = 6
FNET_GROUPS = 4
FNET_DIM = 64
D_FF = 256 * math.ceil(8 * D_MODEL / 3 / 256)
RET_W = H_RET * HEAD_DIM
WIN_W = H_WIN * HEAD_DIM
KV_W = KV_WIN * HEAD_DIM
AB_IN = 4 * RET_W + WIN_W + 2 * KV_W
DIFF_W = H_DIFF * 2 * HEAD_DIM
FNET_W = FNET_GROUPS * FNET_DIM
CD_IN = 3 * DIFF_W + FNET_W
ALPHA = (2 * DEPTH) ** 0.25
LN_EPS = 1e-5
QK_SCALE = HEAD_DIM ** -0.5
LOG2_E = math.log2(math.e)

N_CTX = BATCH * SEQ
N_LAT = DEC_BATCH * DEC_SEQ
N_TOK = N_CTX + N_LAT

LANES = 128
SUBLANES = 8
PAIR_W = 2 * HEAD_DIM
TM = 512
TM_PROJ = 1024
CTX_BLOCKS = N_CTX // TM
TOK_BLOCKS = N_TOK // TM
ROW_GROUPS = 2
FFN_SKEW = 2
W_SLOTS = 2
TQ = 256
LAT_SUB = 4
TQ_CD = 512
CTX_SEQS = 4
CTX_SEQS_CD = 4
RET_GROUP = 4
WIN_GROUP = 4
WIN_GROUP_LAT = 4
DIFF_GROUP_LAT = 1
DIFF_GROUP = 3
N_CHUNK = 256
MOD_TN = 1536
NEG_BIG = -1e30
VMEM_LIMIT = 56 * 1024 * 1024

f32 = jnp.float32
bf16 = jnp.bfloat16


def _params(n_axes):
    return pltpu.CompilerParams(dimension_semantics=("arbitrary",) * n_axes,
                                vmem_limit_bytes=VMEM_LIMIT)


def _dot(a, b):
    return jnp.dot(a, b, preferred_element_type=f32)


def _dot_nt(a, b):
    return lax.dot_general(a, b, (((1,), (1,)), ((), ())), preferred_element_type=f32)


def _ln(x):
    mu = jnp.mean(x, -1, keepdims=True)
    d = x - mu
    var = jnp.mean(d * d, -1, keepdims=True)
    return d * lax.rsqrt(var + LN_EPS)


def _silu(x):
    return x * jax.nn.sigmoid(x)


def _split_bf16(x):
    hi = x.astype(bf16)
    lo = (x - hi.astype(f32)).astype(bf16)
    return hi, lo


def _lane_half_mask(shape):
    return (lax.broadcasted_iota(jnp.int32, shape, len(shape) - 1) & HEAD_DIM) == 0


def _mod_kernel(c_ref, w_ref, b_ref, o_ref):
    layer = pl.program_id(0)
    a = _silu(c_ref[...])
    rows = a.shape[0]
    a_hi, a_lo = _split_bf16(a)
    w_hi, w_lo = _split_bf16(w_ref[0])
    both = _dot(jnp.concatenate([a_hi, a_lo], 0), w_hi)
    o_ref[0] = both[:rows] + both[rows:] + _dot(a_hi, w_lo) + b_ref[pl.ds(layer, 1), :]


def _modulation(cond, w_mod, b_mod):
    tn = MOD_TN
    rows = cond.shape[0]
    return pl.pallas_call(
        _mod_kernel,
        grid=(DEPTH, 6 * D_MODEL // tn),
        in_specs=[pl.BlockSpec((rows, D_MODEL), lambda l, j: (0, 0)),
                  pl.BlockSpec((1, D_MODEL, tn), lambda l, j: (l, 0, j)),
                  pl.BlockSpec((DEPTH, tn), lambda l, j: (0, j))],
        out_specs=pl.BlockSpec((1, rows, tn), lambda l, j: (l, 0, j)),
        out_shape=jax.ShapeDtypeStruct((DEPTH, rows, 6 * D_MODEL), f32),
        compiler_params=_params(2),
        name="modulation",
    )(cond, w_mod, b_mod)


def _tok(i, n_w):
    return jnp.maximum(i - n_w, 0)


def _ctx_blk(t, tm=TM):
    return jnp.minimum(t, N_CTX // tm - 1)


def _lat_blk(t, tm=TM):
    return jnp.maximum(t - N_CTX // tm, 0)


def _mod_row(t, tm=TM):
    return jnp.where(t < N_CTX // tm, 0, 1 + _lat_blk(t, tm) * tm // DEC_SEQ)


def _token_specs(parts, n_w, tm=TM):
    width = parts[0].shape[1]
    if len(parts) == 1:
        return [pl.BlockSpec((tm, width), lambda i: (_tok(i, n_w), 0))]
    return [pl.BlockSpec((tm, width), lambda i: (_ctx_blk(_tok(i, n_w), tm), 0)),
            pl.BlockSpec((tm, width), lambda i: (_lat_blk(_tok(i, n_w), tm), 0))]


def _pick(refs, is_ctx, rs):
    return refs[0 if (len(refs) == 1 or is_ctx) else 1][rs, :]


def _rope_pair(y, cos, sin_signed):
    first_half = (lax.broadcasted_iota(jnp.int32, y.shape, 1) & (HEAD_DIM // 2)) == 0
    swapped = jnp.where(first_half, pltpu.roll(y, LANES - HEAD_DIM // 2, 1), pltpu.roll(y, HEAD_DIM // 2, 1))
    return y * cos + swapped * sin_signed


def _proj_kernel(*refs, n_x, n_cache, n_w, rope_tiles, scale_tiles, cache_plan):
    x_refs = refs[:n_x]
    mod_ref, w_ref, cos_ref, sin_ref, o_ref = refs[n_x:n_x + 5]
    cache_refs = refs[n_x + 5:n_x + 5 + n_cache]
    wbf_ref, u_ref = refs[n_x + 5 + n_cache:]
    i = pl.program_id(0)

    @pl.when(i == 0)
    def _():
        for c in range(n_w):
            wbf_ref[c] = w_ref[0, :, c * N_CHUNK:(c + 1) * N_CHUNK].astype(bf16)

    def tokens(is_ctx):
        x_ref = x_refs[0] if is_ctx else x_refs[-1]
        shift = mod_ref[0, 0:1, :]
        scale = mod_ref[0, 1:2, :]
        groups = [slice(b * SEQ, (b + 1) * SEQ) for b in range(TM_PROJ // SEQ)]
        for rs in groups:
            u_ref[rs, :] = (_ln(x_ref[rs, :]) * (1.0 + scale) + shift).astype(bf16)
        for c in range(n_w):
            y_all = _dot(u_ref[...], wbf_ref[c])
            for b, rs in enumerate(groups):
                y = y_all[rs, :]
                for t in range(N_CHUNK // LANES):
                    tile = c * (N_CHUNK // LANES) + t
                    piece = y[:, t * LANES:(t + 1) * LANES]
                    if tile in rope_tiles and not is_ctx:
                        piece = _rope_pair(piece, cos_ref[rs, :], sin_ref[rs, :])
                    if tile in scale_tiles:
                        piece = piece * (QK_SCALE * LOG2_E)
                    o_ref[rs, tile * LANES:(tile + 1) * LANES] = piece.astype(o_ref.dtype)
                    if tile in cache_plan and is_ctx:
                        kind, out_idx, slot = cache_plan[tile]
                        c_ref = cache_refs[out_idx]
                        if kind == "plain":
                            c_ref[b, 0, slot] = piece
                        else:
                            piece_t = piece.T
                            if kind == "heads":
                                c_ref[b, 0, 0] = piece_t[0:HEAD_DIM]
                                c_ref[b, 0, 1] = piece_t[HEAD_DIM:]
                            else:
                                c_ref[b, 0, slot, 0] = piece_t[0:HEAD_DIM]
                                c_ref[b, 0, slot, 1] = piece_t[HEAD_DIM:]

    t = i - 1

    @pl.when(jnp.logical_and(t >= 0, t < N_CTX // TM_PROJ))
    def _():
        tokens(True)

    @pl.when(t >= N_CTX // TM_PROJ)
    def _():
        tokens(False)


def _proj(x_parts, mod, mod_layer, w_all, layer, scale_tiles, rope_tabs, rope_tiles, cache_shapes, cache_plan):
    n_out = w_all.shape[2]
    n_w = n_out // N_CHUNK
    tm = TM_PROJ
    nb = DEC_SEQ // tm
    tok = lambda i: _tok(i, 1)
    in_specs = _token_specs(x_parts, 1, tm) + [
        pl.BlockSpec((None, 1, 6, D_MODEL), lambda i: (mod_layer, _mod_row(tok(i), tm), 0, 0)),
        pl.BlockSpec((1, D_MODEL, n_out), lambda i: (layer, 0, 0), pipeline_mode=pl.Buffered(1)),
        pl.BlockSpec((tm, LANES), lambda i: (_lat_blk(tok(i), tm) % nb, 0)),
        pl.BlockSpec((tm, LANES), lambda i: (_lat_blk(tok(i), tm) % nb, 0))]
    out_specs = [pl.BlockSpec((tm, n_out), lambda i: (tok(i), 0))]
    out_shape = [jax.ShapeDtypeStruct((N_TOK, n_out), bf16)]
    for shp in cache_shapes:
        blk = (tm // SEQ,) + tuple(shp[1:])
        out_specs.append(pl.BlockSpec(blk, lambda i, nd=len(shp): (_ctx_blk(tok(i), tm),) + (0,) * (nd - 1)))
        out_shape.append(jax.ShapeDtypeStruct(tuple(shp), f32))
    return pl.pallas_call(
        functools.partial(_proj_kernel, n_x=len(x_parts), n_cache=len(cache_shapes), n_w=n_w,
                          rope_tiles=frozenset(rope_tiles), scale_tiles=frozenset(scale_tiles),
                          cache_plan=dict(cache_plan)),
        grid=(1 + N_TOK // tm,),
        in_specs=in_specs,
        out_specs=out_specs,
        out_shape=out_shape,
        scratch_shapes=[pltpu.VMEM((n_w, D_MODEL, N_CHUNK), bf16), pltpu.VMEM((tm, D_MODEL), bf16)],
        compiler_params=_params(1),
        name="proj",
    )(*x_parts, mod, w_all, *rope_tabs)


def _post_kernel(*refs, n_x, n_y, ka, kb, n_w, layer):
    x_refs = refs[:n_x]
    (ac_ref, al_ref, bc_ref, bl_ref, mod_ref, lng_ref, lnb_ref,
     wo_ref, wg_ref, wu_ref, wd_ref) = refs[n_x:n_x + 11]
    y_refs = refs[n_x + 11:n_x + 11 + n_y]
    (wo_s, wg_s, wu_s, wd_s, x1_ref, u_ref, h_ref, y_ref,
     wg_f, wu_f, wd_f, w_sem) = refs[n_x + 11 + n_y:]
    i = pl.program_id(0)

    def chunk_copies(c, slot):
        cols = pl.ds(pl.multiple_of(c * N_CHUNK, N_CHUNK), N_CHUNK)
        return (pltpu.make_async_copy(wg_ref.at[layer, :, cols], wg_f.at[slot], w_sem.at[0, slot]),
                pltpu.make_async_copy(wu_ref.at[layer, :, cols], wu_f.at[slot], w_sem.at[1, slot]),
                pltpu.make_async_copy(wd_ref.at[layer, cols, :], wd_f.at[slot], w_sem.at[2, slot]))

    lead = n_w - 1
    gate1 = mod_ref[0, 2:3, :]
    shift2 = mod_ref[0, 3:4, :]
    scale2 = mod_ref[0, 4:5, :]
    gate2 = mod_ref[0, 5:6, :]
    groups = [slice(r * TM // ROW_GROUPS, (r + 1) * TM // ROW_GROUPS) for r in range(ROW_GROUPS)]

    def mix_in(rs, is_ctx):
        a = _pick((ac_ref, al_ref), is_ctx, rs)
        b = _pick((bc_ref, bl_ref), is_ctx, rs)
        pieces = ([a[:, c:c + N_CHUNK] for c in range(0, ka, N_CHUNK)]
                  + [b[:, c:c + N_CHUNK] for c in range(0, kb, N_CHUNK)])
        h = functools.reduce(lambda s, p: s + p, [_dot(p, wo_s[c]) for c, p in enumerate(pieces)])
        x1 = _ln(ALPHA * _pick(x_refs, is_ctx, rs) + gate1 * h) * lng_ref[0, 0:1, :] + lnb_ref[0, 0:1, :]
        x1_ref[rs, :] = x1
        u_ref[rs, :] = (_ln(x1) * (1.0 + scale2) + shift2).astype(bf16)

    def finish(rs, ffn, is_ctx):
        y = _ln(ALPHA * x1_ref[rs, :] + gate2 * ffn) * lng_ref[0, 1:2, :] + lnb_ref[0, 1:2, :]
        y_refs[0 if (n_y == 1 or is_ctx) else 1][rs, :] = y

    @pl.when(i < n_w)
    def _():
        slot = i % W_SLOTS

        @pl.when(i == 0)
        def _():
            for c in range(W_SLOTS):
                for cp in chunk_copies(c, c):
                    cp.start()
            for c in range((ka + kb) // N_CHUNK):
                wo_s[c] = wo_ref[0, c * N_CHUNK:(c + 1) * N_CHUNK, :].astype(bf16)
            for rs in groups:
                mix_in(rs, True)
                y_ref[rs, :] = jnp.zeros((TM // ROW_GROUPS, D_MODEL), f32)

        for cp in chunk_copies(i, slot):
            cp.wait()
        wg_s[i] = wg_f[slot].astype(bf16)
        wu_s[i] = wu_f[slot].astype(bf16)
        wd_s[i] = wd_f[slot].astype(bf16)

        @pl.when(i + W_SLOTS < n_w)
        def _():
            for cp in chunk_copies(i + W_SLOTS, slot):
                cp.start()

        g = _dot(u_ref[...], wg_s[i])
        up = _dot(u_ref[...], wu_s[i])
        y_ref[...] += _dot((_silu(g) * up).astype(bf16), wd_s[i])

        @pl.when(i == lead)
        def _():
            for rs in groups:
                finish(rs, y_ref[rs, :], True)

    def token_block(is_ctx):
        for rs in groups:
            mix_in(rs, is_ctx)

        def ffn_chunk(rs, c):
            g = _dot(u_ref[rs, :], wg_s[c])
            up = _dot(u_ref[rs, :], wu_s[c])
            h_ref[rs, c * N_CHUNK:(c + 1) * N_CHUNK] = (_silu(g) * up).astype(bf16)

        def ffn_down(rs):
            finish(rs, functools.reduce(
                lambda s, p: s + p,
                [_dot(h_ref[rs, c * N_CHUNK:(c + 1) * N_CHUNK], wd_s[c]) for c in range(n_w)]), is_ctx)

        for c in range(n_w + FFN_SKEW * (ROW_GROUPS - 1)):
            for r, rs in enumerate(groups):
                cc = c - FFN_SKEW * r
                if 0 <= cc < n_w:
                    ffn_chunk(rs, cc)
                if cc == n_w - 1:
                    ffn_down(rs)

    blk = i - lead

    @pl.when(jnp.logical_and(i >= n_w, blk < CTX_BLOCKS))
    def _():
        token_block(True)

    @pl.when(blk >= CTX_BLOCKS)
    def _():
        token_block(False)


def _post(x_parts, mix_a, mix_b, mod, ln_g, ln_b, w_out, w_gate, w_up, w_down, layer, mix_layer, split_out):
    ka, kb = mix_a[0].shape[1], mix_b[0].shape[1]
    n_w = D_FF // N_CHUNK
    lead = n_w - 1
    tok = lambda i: _tok(i, lead)
    lay = lambda i: (layer, 0, 0)
    in_specs = (_token_specs(x_parts, lead) + _token_specs(mix_a, lead) + _token_specs(mix_b, lead) + [
        pl.BlockSpec((None, 1, 6, D_MODEL), lambda i: (layer, _mod_row(tok(i)), 0, 0)),
        pl.BlockSpec((1, 2, D_MODEL), lay),
        pl.BlockSpec((1, 2, D_MODEL), lay),
        pl.BlockSpec((1, ka + kb, D_MODEL), lambda i: (mix_layer, 0, 0), pipeline_mode=pl.Buffered(1)),
        pl.BlockSpec(memory_space=pl.ANY), pl.BlockSpec(memory_space=pl.ANY), pl.BlockSpec(memory_space=pl.ANY)])
    if split_out:
        out_specs = [pl.BlockSpec((TM, D_MODEL), lambda i: (_ctx_blk(tok(i)), 0)),
                     pl.BlockSpec((TM, D_MODEL), lambda i: (_lat_blk(tok(i)), 0))]
        out_shape = [jax.ShapeDtypeStruct((N_CTX, D_MODEL), f32), jax.ShapeDtypeStruct((N_LAT, D_MODEL), f32)]
    else:
        out_specs = [pl.BlockSpec((TM, D_MODEL), lambda i: (tok(i), 0))]
        out_shape = [jax.ShapeDtypeStruct((N_TOK, D_MODEL), f32)]
    return pl.pallas_call(
        functools.partial(_post_kernel, n_x=len(x_parts), n_y=len(out_shape), ka=ka, kb=kb, n_w=n_w, layer=layer),
        grid=(lead + TOK_BLOCKS,),
        in_specs=in_specs,
        out_specs=out_specs,
        out_shape=out_shape,
        scratch_shapes=[pltpu.VMEM(((ka + kb) // N_CHUNK, N_CHUNK, D_MODEL), bf16),
                        pltpu.VMEM((n_w, D_MODEL, N_CHUNK), bf16),
                        pltpu.VMEM((n_w, D_MODEL, N_CHUNK), bf16), pltpu.VMEM((n_w, N_CHUNK, D_MODEL), bf16),
                        pltpu.VMEM((TM, D_MODEL), f32), pltpu.VMEM((TM, D_MODEL), bf16),
                        pltpu.VMEM((TM, D_FF), bf16), pltpu.VMEM((TM, D_MODEL), f32),
                        pltpu.VMEM((W_SLOTS, D_MODEL, N_CHUNK), f32), pltpu.VMEM((W_SLOTS, D_MODEL, N_CHUNK), f32),
                        pltpu.VMEM((W_SLOTS, N_CHUNK, D_MODEL), f32), pltpu.SemaphoreType.DMA((3, W_SLOTS))],
        compiler_params=_params(1),
        name="post",
    )(*x_parts, *mix_a, *mix_b, mod, ln_g, ln_b, w_out, w_gate, w_up, w_down)


def _group_norm_gate(ro, rg, gmat, gn_g, gn_b):
    def gmean(parts):
        cols = []
        for c in range(0, RET_W, N_CHUNK):
            cols.append(sum(_dot(p[:, c:c + N_CHUNK], gmat) for p in parts))
        return jnp.concatenate(cols, -1)

    d = ro - gmean(_split_bf16(ro))
    var = gmean([(d * d).astype(bf16)])
    y = d * lax.rsqrt(var + LN_EPS) * gn_g + gn_b
    return _silu(rg.astype(f32)) * y


def _dup_head(x, j):
    first = _lane_half_mask(x.shape)
    keep = first if j == 0 else jnp.logical_not(first)
    xm = jnp.where(keep, x.astype(f32), 0.0)
    return xm + pltpu.roll(xm, HEAD_DIM, 1)


def _softmax_parts(scores, sink):
    m = sink
    for s in scores:
        m = jnp.maximum(m, jnp.max(s, -1, keepdims=True))
    es = [jnp.exp2(s - m) for s in scores]
    denom = jnp.exp2(sink - m)
    for e in es:
        denom = denom + jnp.sum(e, -1, keepdims=True)
    return es, denom


def _retention_tables(lg_ref, lgf_ref, lgb_ref, dmask_ref, kdec_ref, n):
    row = lax.broadcasted_iota(jnp.int32, (n, n), 0)
    col = lax.broadcasted_iota(jnp.int32, (n, n), 1)
    diff = (row - col).astype(f32)
    diag = jnp.where(row == col, 2.0 * QK_SCALE, QK_SCALE)
    for h in range(H_RET):
        dmask_ref[h] = jnp.exp(jnp.where(diff >= 0, lg_ref[0, h] * diff, -lg_ref[1, h] * diff)) * diag
    t = lax.broadcasted_iota(jnp.int32, (n, RET_W), 0).astype(f32)
    kdec_ref[0] = jnp.exp(lgf_ref[...] * (n - 1.0 - t)) * QK_SCALE
    kdec_ref[1] = jnp.exp(lgb_ref[...] * t) * QK_SCALE


def _retention_intra(pairs, q_of, k_of, v_of, dmask_ref):
    first = _lane_half_mask(k_of(pairs[0]).shape)
    masked = {}
    for p in pairs:
        kb = k_of(p)
        for e in range(2):
            keep = first if e == 0 else jnp.logical_not(first)
            s = _dot_nt(q_of(p), jnp.where(keep, kb, jnp.zeros_like(kb))) * dmask_ref[2 * p + e]
            masked[p, e] = s.astype(bf16)
    outs = {}
    for p in pairs:
        pv = [_dot(masked[p, e], v_of(p)) for e in range(2)]
        outs[p] = jnp.where(_lane_half_mask(pv[0].shape), pv[0], pv[1])
    return outs


def _window_group(subs, q_of, k_parts_of, v_parts_of, masks, sink_of):
    scores = {}
    for key in subs:
        parts = [_dot_nt(q_of(key), k) for k in k_parts_of(key)]
        scores[key] = [sc if mk is None else jnp.where(mk, sc, NEG_BIG) for sc, mk in zip(parts, masks)]
    probs = {}
    for key in subs:
        es, denom = _softmax_parts(scores[key], sink_of(key))
        probs[key] = ([ex.astype(bf16) for ex in es], denom)
    outs = {}
    for key in subs:
        es, denom = probs[key]
        pv = functools.reduce(lambda x, y: x + y, [_dot(ex, v) for ex, v in zip(es, v_parts_of(key))])
        outs[key] = pv / denom
    return outs


def _ctx_ab_kernel(lg_ref, sink_ref, rq_ref, rk_ref, rv_ref, rg_ref, wq_ref, wk_ref, wv_ref,
                   lgf_ref, lgb_ref, gmat_ref, gng_ref, gnb_ref,
                   ro_ref, wo_ref, st_ref, dmask_ref, kdec_ref, ret_ref):
    t_len = SEQ

    @pl.when(pl.program_id(0) == 0)
    def _():
        _retention_tables(lg_ref, lgf_ref, lgb_ref, dmask_ref, kdec_ref, t_len)

    first = _lane_half_mask((t_len, PAIR_W))
    for sq in range(CTX_SEQS):
        rows = slice(sq * t_len, (sq + 1) * t_len)
        psl = lambda p: slice(p * PAIR_W, (p + 1) * PAIR_W)
        for p0 in range(0, H_RET // 2, RET_GROUP):
            pairs = list(range(p0, p0 + RET_GROUP))
            intra = _retention_intra(pairs, lambda p: rq_ref[rows, psl(p)], lambda p: rk_ref[rows, psl(p)],
                                     lambda p: rv_ref[rows, psl(p)], dmask_ref)
            for p in pairs:
                ret_ref[rows, psl(p)] = intra[p]
        for p in range(H_RET // 2):
            sl = psl(p)
            kb = rk_ref[rows, sl]
            v = rv_ref[rows, sl]
            for d in range(2):
                kd_t = (kb * kdec_ref[d, :, sl]).T.astype(bf16)
                st = _dot(kd_t, v)
                st_ref[sq, d, 2 * p] = st[0:HEAD_DIM, 0:HEAD_DIM]
                st_ref[sq, d, 2 * p + 1] = pltpu.roll(st[HEAD_DIM:, :], HEAD_DIM, 1)[:, 0:HEAD_DIM]
        ro_ref[rows, :] = _group_norm_gate(ret_ref[rows, :], rg_ref[rows, :], gmat_ref[...], gng_ref[...],
                                           gnb_ref[...]).astype(bf16)

        k_dup = [_dup_head(wk_ref[rows, :], j).astype(bf16) for j in range(KV_WIN)]
        v_dup = [_dup_head(wv_ref[rows, :], j).astype(bf16) for j in range(KV_WIN)]

        def q_masked(key):
            qp, e = key
            qb = wq_ref[rows, qp * PAIR_W:(qp + 1) * PAIR_W]
            return jnp.where(first if e == 0 else jnp.logical_not(first), qb, jnp.zeros_like(qb))

        kv_of = lambda key: key[0] * 2 // G_WIN
        for g0 in range(0, H_WIN // 2, WIN_GROUP):
            subs = [(qp, e) for qp in range(g0, g0 + WIN_GROUP) for e in range(2)]
            outs = _window_group(subs, q_masked, lambda key: [k_dup[kv_of(key)]], lambda key: [v_dup[kv_of(key)]],
                                 [None], lambda key: sink_ref[0, 2 * key[0] + key[1]] * LOG2_E)
            for qp in range(g0, g0 + WIN_GROUP):
                wo_ref[rows, qp * PAIR_W:(qp + 1) * PAIR_W] = jnp.where(first, outs[qp, 0], outs[qp, 1]).astype(bf16)


def _ctx_ab(proj, log_gamma, sink, lgf_lanes, lgb_lanes, gmat, gn_g, gn_b):
    t = SEQ
    tb = CTX_SEQS * t
    smem = pl.BlockSpec(memory_space=pltpu.SMEM)
    const = lambda b: (0, 0)
    col = lambda c: (lambda b: (b, c))
    return pl.pallas_call(
        _ctx_ab_kernel,
        grid=(BATCH // CTX_SEQS,),
        in_specs=[smem, smem,
                  pl.BlockSpec((tb, RET_W), col(0)), pl.BlockSpec((tb, RET_W), col(1)),
                  pl.BlockSpec((tb, RET_W), col(2)), pl.BlockSpec((tb, RET_W), col(3)),
                  pl.BlockSpec((tb, WIN_W), col(4)),
                  pl.BlockSpec((tb, KV_W), col((4 * RET_W + WIN_W) // KV_W)),
                  pl.BlockSpec((tb, KV_W), col((4 * RET_W + WIN_W) // KV_W + 1)),
                  pl.BlockSpec((1, RET_W), const), pl.BlockSpec((1, RET_W), const),
                  pl.BlockSpec((N_CHUNK, N_CHUNK), const),
                  pl.BlockSpec((1, RET_W), const), pl.BlockSpec((1, RET_W), const)],
        out_specs=[pl.BlockSpec((tb, RET_W), lambda b: (b, 0)),
                   pl.BlockSpec((tb, WIN_W), lambda b: (b, 0)),
                   pl.BlockSpec((CTX_SEQS, 2, H_RET, HEAD_DIM, HEAD_DIM), lambda b: (b, 0, 0, 0, 0))],
        out_shape=[jax.ShapeDtypeStruct((BATCH * t, RET_W), bf16),
                   jax.ShapeDtypeStruct((BATCH * t, WIN_W), bf16),
                   jax.ShapeDtypeStruct((BATCH, 2, H_RET, HEAD_DIM, HEAD_DIM), f32)],
        scratch_shapes=[pltpu.VMEM((H_RET, t, t), f32), pltpu.VMEM((2, t, RET_W), f32),
                        pltpu.VMEM((tb, RET_W), f32)],
        compiler_params=_params(1),
        name="ctx_ab",
    )(log_gamma, sink, proj, proj, proj, proj, proj, proj, proj, lgf_lanes, lgb_lanes, gmat, gn_g, gn_b)


def _pair_state(s0_ref, d, p):
    zero = jnp.zeros((HEAD_DIM, HEAD_DIM), f32)
    top = jnp.concatenate([s0_ref[0, 0, d, 2 * p], zero], 1)
    bottom = jnp.concatenate([zero, s0_ref[0, 0, d, 2 * p + 1]], 1)
    return jnp.concatenate([top, bottom], 0)


def _lat_ab_kernel(lg_ref, sink_ref, rq_ref, rk_ref, rv_ref, rg_ref, wq_ref, wk_ref, wv_ref, ck_ref, cv_ref,
                   s0_ref, lgf_ref, lgb_ref, gmat_ref, gng_ref, gnb_ref,
                   ro_ref, wo_ref, ret_ref, dmask_ref, kdec_ref, qdec_ref, sf_ref, sb_ref):
    t_len = DEC_SEQ
    n_chunks = t_len // TQ
    step = pl.program_id(1)
    first = _lane_half_mask((TQ, PAIR_W))

    @pl.when(jnp.logical_and(pl.program_id(0) == 0, step == 0))
    def _():
        _retention_tables(lg_ref, lgf_ref, lgb_ref, dmask_ref, kdec_ref, TQ)
        t = lax.broadcasted_iota(jnp.int32, (TQ, RET_W), 0).astype(f32)
        qdec_ref[0] = jnp.exp(lgf_ref[...] * (t + 1.0))
        qdec_ref[1] = jnp.exp(lgb_ref[...] * (TQ - t))

    @pl.when(step == 0)
    def _():
        r = lax.broadcasted_iota(jnp.int32, (PAIR_W, PAIR_W), 0)
        c_ = lax.broadcasted_iota(jnp.int32, (PAIR_W, PAIR_W), 1)
        same_head = (r < HEAD_DIM) == (c_ < HEAD_DIM)
        for p in range(H_RET // 2):
            sl = slice(p * PAIR_W, (p + 1) * PAIR_W)
            kv = []
            for c in range(n_chunks):
                rows = slice(c * TQ, (c + 1) * TQ)
                kc = rk_ref[rows, sl]
                vc = rv_ref[rows, sl]
                kv.append([jnp.where(same_head, _dot((kc * kdec_ref[d, :, sl]).T.astype(bf16), vc), 0.0)
                           for d in range(2)])
            state = _pair_state(s0_ref, 0, p)
            for c in range(n_chunks):
                sf_ref[c, p] = state
                state = state * jnp.exp(lgf_ref[:, sl] * TQ) + kv[c][0]
            state = _pair_state(s0_ref, 1, p)
            for c in reversed(range(n_chunks)):
                sb_ref[c, p] = state
                state = state * jnp.exp(lgb_ref[:, sl] * TQ) + kv[c][1]

    for sub in range(LAT_SUB):
        chunk = step * LAT_SUB + sub
        q0 = pl.multiple_of(chunk * TQ, TQ)
        rows = slice(sub * TQ, (sub + 1) * TQ)
        psl = lambda p: slice(p * PAIR_W, (p + 1) * PAIR_W)
        intra = {}
        for p0 in range(0, H_RET // 2, RET_GROUP):
            intra.update(_retention_intra(list(range(p0, p0 + RET_GROUP)), lambda p: rq_ref[rows, psl(p)],
                                          lambda p: rk_ref[pl.ds(q0, TQ), psl(p)],
                                          lambda p: rv_ref[pl.ds(q0, TQ), psl(p)], dmask_ref))
        for p in range(H_RET // 2):
            sl = psl(p)
            q = rq_ref[rows, sl]
            o = intra[p]
            o = o + _dot(q, sf_ref[chunk, p].astype(bf16)) * qdec_ref[0, :, sl]
            o = o + _dot(q, sb_ref[chunk, p].astype(bf16)) * qdec_ref[1, :, sl]
            ret_ref[rows, sl] = o
        ro_ref[rows, :] = _group_norm_gate(ret_ref[rows, :], rg_ref[rows, :], gmat_ref[...], gng_ref[...],
                                           gnb_ref[...]).astype(bf16)

        band = TQ + 2 * WINDOW
        k_start = pl.multiple_of(jnp.clip(q0 - WINDOW, 0, t_len - band), LANES)
        qi = q0 + lax.broadcasted_iota(jnp.int32, (TQ, band), 0)
        kj = k_start + lax.broadcasted_iota(jnp.int32, (TQ, band), 1)
        in_band = jnp.abs(qi - kj) <= WINDOW
        k_parts = [[_dup_head(wk_ref[pl.ds(k_start, band), :], j).astype(bf16), _dup_head(ck_ref[0], j).astype(bf16)]
                   for j in range(KV_WIN)]
        v_parts = [[_dup_head(wv_ref[pl.ds(k_start, band), :], j).astype(bf16), _dup_head(cv_ref[0], j).astype(bf16)]
                   for j in range(KV_WIN)]

        def q_masked(key):
            qp, e = key
            qb = wq_ref[rows, qp * PAIR_W:(qp + 1) * PAIR_W]
            return jnp.where(first if e == 0 else jnp.logical_not(first), qb, jnp.zeros_like(qb))

        kv_of = lambda key: key[0] * 2 // G_WIN
        for g0 in range(0, H_WIN // 2, WIN_GROUP_LAT):
            subs = [(qp, e) for qp in range(g0, g0 + WIN_GROUP_LAT) for e in range(2)]
            outs = _window_group(subs, q_masked, lambda key: k_parts[kv_of(key)], lambda key: v_parts[kv_of(key)],
                                 [in_band, None], lambda key: sink_ref[0, 2 * key[0] + key[1]] * LOG2_E)
            for qp in range(g0, g0 + WIN_GROUP_LAT):
                wo_ref[rows, qp * PAIR_W:(qp + 1) * PAIR_W] = jnp.where(first, outs[qp, 0], outs[qp, 1]).astype(bf16)


def _lat_ab(proj, log_gamma, sink, ck, cv, state, layer, lgf_lanes, lgb_lanes, gmat, gn_g, gn_b):
    t = DEC_SEQ
    tb = LAT_SUB * TQ
    nq = t // tb
    smem = pl.BlockSpec(memory_space=pltpu.SMEM)
    const = lambda b, i: (0, 0)
    qcol = lambda c: (lambda b, i: (N_CTX // tb + b * nq + i, c))
    bcol = lambda c: (lambda b, i: (N_CTX // t + b, c))
    kv_col = (4 * RET_W + WIN_W) // KV_W
    return pl.pallas_call(
        _lat_ab_kernel,
        grid=(DEC_BATCH, nq),
        in_specs=[smem, smem,
                  pl.BlockSpec((tb, RET_W), qcol(0)), pl.BlockSpec((t, RET_W), bcol(1)),
                  pl.BlockSpec((t, RET_W), bcol(2)), pl.BlockSpec((tb, RET_W), qcol(3)),
                  pl.BlockSpec((tb, WIN_W), qcol(4)),
                  pl.BlockSpec((t, KV_W), bcol(kv_col)), pl.BlockSpec((t, KV_W), bcol(kv_col + 1)),
                  pl.BlockSpec((1, PAST_LEN, KV_W), lambda b, i: (b, 0, 0)),
                  pl.BlockSpec((1, PAST_LEN, KV_W), lambda b, i: (b, 0, 0)),
                  pl.BlockSpec((1, 1, 2, H_RET, HEAD_DIM, HEAD_DIM), lambda b, i: (b, layer, 0, 0, 0, 0)),
                  pl.BlockSpec((1, RET_W), const), pl.BlockSpec((1, RET_W), const),
                  pl.BlockSpec((N_CHUNK, N_CHUNK), const),
                  pl.BlockSpec((1, RET_W), const), pl.BlockSpec((1, RET_W), const)],
        out_specs=[pl.BlockSpec((tb, RET_W), lambda b, i: (b * nq + i, 0)),
                   pl.BlockSpec((tb, WIN_W), lambda b, i: (b * nq + i, 0))],
        out_shape=[jax.ShapeDtypeStruct((DEC_BATCH * t, RET_W), bf16),
                   jax.ShapeDtypeStruct((DEC_BATCH * t, WIN_W), bf16)],
        scratch_shapes=[pltpu.VMEM((tb, RET_W), f32), pltpu.VMEM((H_RET, TQ, TQ), f32),
                        pltpu.VMEM((2, TQ, RET_W), f32), pltpu.VMEM((2, TQ, RET_W), f32),
                        pltpu.VMEM((t // TQ, H_RET // 2, PAIR_W, PAIR_W), f32),
                        pltpu.VMEM((t // TQ, H_RET // 2, PAIR_W, PAIR_W), f32)],
        compiler_params=_params(2),
        name="lat_ab",
    )(log_gamma, sink, proj, proj, proj, proj, proj, proj, proj, ck, cv, state,
      lgf_lanes, lgb_lanes, gmat, gn_g, gn_b)


def _lambda_full(lam_ref, lam_init):
    lam = lam_ref[...]
    a = jnp.sum(lam[0:1, :] * lam[1:2, :], -1, keepdims=True)
    b = jnp.sum(lam[2:3, :] * lam[3:4, :], -1, keepdims=True)
    return jnp.exp(a) - jnp.exp(b) + lam_init


def _diff_heads(q_of, k_parts_of, v_parts_of, lam, subln, lam_init, group):
    res = []
    for h0 in range(0, H_DIFF, group):
        res += _diff_head_group(range(h0, h0 + group), q_of, k_parts_of, v_parts_of, lam, subln, lam_init)
    return res


def _diff_head_group(heads, q_of, k_parts_of, v_parts_of, lam, subln, lam_init):
    subs = [(h, e) for h in heads for e in range(2)]
    scores = {}
    for h, e in subs:
        q = q_of(h)
        fm = _lane_half_mask(q.shape)
        q_sub = jnp.where(fm if e == 0 else jnp.logical_not(fm), q, jnp.zeros_like(q))
        scores[h, e] = [_dot(q_sub, k) if transposed else _dot_nt(q_sub, k) for k, transposed in k_parts_of(h)]
    probs = {}
    for key in subs:
        m = scores[key][0].max(-1, keepdims=True)
        for sc in scores[key][1:]:
            m = jnp.maximum(m, sc.max(-1, keepdims=True))
        es = [jnp.exp2(sc - m) for sc in scores[key]]
        denom = es[0].sum(-1, keepdims=True)
        for ex in es[1:]:
            denom = denom + ex.sum(-1, keepdims=True)
        probs[key] = ([ex.astype(bf16) for ex in es], denom)
    outs = {}
    for h, e in subs:
        es, denom = probs[h, e]
        pv = functools.reduce(lambda x, y: x + y, [_dot(ex, v) for v, ex in zip(v_parts_of(h), es)])
        outs[h, e] = pv / denom
    res = []
    for h in heads:
        a = outs[h, 0] - lam * outs[h, 1]
        res.append(a * lax.rsqrt(jnp.mean(a * a, -1, keepdims=True) + LN_EPS) * subln * (1.0 - lam_init))
    return res


def _fourier_rows(ct_ref, st_ref, z, bdc_ref, bds_ref):
    zc = _dot(z, bdc_ref[...].astype(bf16)).astype(bf16)
    zs = _dot(z, bds_ref[...].astype(bf16)).astype(bf16)
    return _dot(ct_ref[...].astype(bf16), zc) - _dot(st_ref[...].astype(bf16), zs)


def _ctx_cd_kernel(q_ref, k_ref, v_ref, z_ref, lam_ref, subln_ref, ct_ref, st_ref, bdc_ref, bds_ref,
                   a_ref, zf_ref, *, lam_init):
    lam = _lambda_full(lam_ref, lam_init)
    for sq in range(CTX_SEQS_CD):
        rows = slice(sq * SEQ, (sq + 1) * SEQ)
        sl = lambda h: slice(h * PAIR_W, (h + 1) * PAIR_W)
        heads = _diff_heads(lambda h: q_ref[rows, sl(h)], lambda h: [(k_ref[rows, sl(h)], False)],
                            lambda h: [v_ref[rows, sl(h)]], lam, subln_ref[...], lam_init, DIFF_GROUP)
        for h in range(H_DIFF):
            a_ref[rows, sl(h)] = heads[h].astype(bf16)
        zf_ref[rows, :] = _fourier_rows(ct_ref, st_ref, z_ref[rows, :], bdc_ref, bds_ref).astype(bf16)


def _ctx_cd(proj, lam, subln, ct, st, bdc, bds, lam_init):
    t = SEQ
    tb = CTX_SEQS_CD * t
    const = lambda b: (0, 0)
    col = lambda c: (lambda b: (b, c))
    return pl.pallas_call(
        functools.partial(_ctx_cd_kernel, lam_init=lam_init),
        grid=(BATCH // CTX_SEQS_CD,),
        in_specs=[pl.BlockSpec((tb, DIFF_W), col(0)), pl.BlockSpec((tb, DIFF_W), col(1)),
                  pl.BlockSpec((tb, DIFF_W), col(2)), pl.BlockSpec((tb, FNET_W), col(3 * DIFF_W // FNET_W)),
                  pl.BlockSpec((4, HEAD_DIM), const), pl.BlockSpec((1, PAIR_W), const),
                  pl.BlockSpec((t, t), const), pl.BlockSpec((t, t), const),
                  pl.BlockSpec((FNET_W, FNET_W), const), pl.BlockSpec((FNET_W, FNET_W), const)],
        out_specs=[pl.BlockSpec((tb, DIFF_W), lambda b: (b, 0)), pl.BlockSpec((tb, FNET_W), lambda b: (b, 0))],
        out_shape=[jax.ShapeDtypeStruct((BATCH * t, DIFF_W), bf16),
                   jax.ShapeDtypeStruct((BATCH * t, FNET_W), bf16)],
        compiler_params=_params(1),
        name="ctx_cd",
    )(proj, proj, proj, proj, lam, subln, ct, st, bdc, bds)


def _lat_cd_kernel(q_ref, k_ref, v_ref, z_ref, ckt_ref, cv_ref, lam_ref, subln_ref, ct_ref, st_ref, bdc_ref, bds_ref,
                   a_ref, zf_ref, *, lam_init):
    lam = _lambda_full(lam_ref, lam_init)
    sl = lambda h: slice(h * PAIR_W, (h + 1) * PAIR_W)
    heads = _diff_heads(lambda h: q_ref[:, sl(h)],
                        lambda h: [(k_ref[:, sl(h)], False), (ckt_ref[0, h].astype(bf16), True)],
                        lambda h: [v_ref[:, sl(h)], cv_ref[0, h].astype(bf16)], lam, subln_ref[...], lam_init,
                        DIFF_GROUP_LAT)
    for h in range(H_DIFF):
        a_ref[:, sl(h)] = heads[h].astype(bf16)
    zf_ref[...] = _fourier_rows(ct_ref, st_ref, z_ref[...], bdc_ref, bds_ref).astype(bf16)


def _lat_cd(proj, ck, cv, lam, subln, ct, st, bdc, bds, lam_init):
    t = DEC_SEQ
    nq = t // TQ_CD
    const = lambda b, i: (0, 0)
    return pl.pallas_call(
        functools.partial(_lat_cd_kernel, lam_init=lam_init),
        grid=(DEC_BATCH, nq),
        in_specs=[pl.BlockSpec((TQ_CD, DIFF_W), lambda b, i: (N_CTX // TQ_CD + b * nq + i, 0)),
                  pl.BlockSpec((t, DIFF_W), lambda b, i: (N_CTX // t + b, 1)),
                  pl.BlockSpec((t, DIFF_W), lambda b, i: (N_CTX // t + b, 2)),
                  pl.BlockSpec((t, FNET_W), lambda b, i: (N_CTX // t + b, 3 * DIFF_W // FNET_W)),
                  pl.BlockSpec((1, H_DIFF, PAIR_W, PAST_LEN), lambda b, i: (b, 0, 0, 0)),
                  pl.BlockSpec((1, H_DIFF, PAST_LEN, PAIR_W), lambda b, i: (b, 0, 0, 0)),
                  pl.BlockSpec((4, HEAD_DIM), const), pl.BlockSpec((1, PAIR_W), const),
                  pl.BlockSpec((TQ_CD, t), lambda b, i: (i, 0)), pl.BlockSpec((TQ_CD, t), lambda b, i: (i, 0)),
                  pl.BlockSpec((FNET_W, FNET_W), const), pl.BlockSpec((FNET_W, FNET_W), const)],
        out_specs=[pl.BlockSpec((TQ_CD, DIFF_W), lambda b, i: (b * nq + i, 0)),
                   pl.BlockSpec((TQ_CD, FNET_W), lambda b, i: (b * nq + i, 0))],
        out_shape=[jax.ShapeDtypeStruct((DEC_BATCH * t, DIFF_W), bf16),
                   jax.ShapeDtypeStruct((DEC_BATCH * t, FNET_W), bf16)],
        compiler_params=_params(2),
        name="lat_cd",
    )(proj, proj, proj, proj, ck, cv, lam, subln, ct, st, bdc, bds)


def _rope_tables():
    t = np.arange(DEC_SEQ)
    quarter = HEAD_DIM // 4
    inv = ROPE_BASE ** (-np.arange(quarter, dtype=np.float64) / quarter)
    ang = np.concatenate([(t // GRID_W)[:, None] * inv, (t % GRID_W)[:, None] * inv], -1)
    cos, sin = np.cos(ang), np.sin(ang)
    reps = LANES // HEAD_DIM
    return (np.tile(np.concatenate([cos, cos], -1), (1, reps)).astype(np.float32),
            np.tile(np.concatenate([-sin, sin], -1), (1, reps)).astype(np.float32))


def _dft_tables(n):
    k = np.arange(n)
    ang = (2.0 * math.pi / n) * ((k[:, None] * k[None, :]) % n)
    return (np.cos(ang) / math.sqrt(n)).astype(np.float32), (np.sin(ang) / math.sqrt(n)).astype(np.float32)


def _block_diag(m, reps):
    return np.kron(np.eye(reps, dtype=m.dtype), m)


def kernel(x_prompt, x_sample, state_ret, cache_win_k, cache_win_v, cache_diff_k, cache_diff_v, c, c_ctx, w_mod, b_mod, ln_g, ln_b, w_in_ab, w_out_ab, ret_log_gamma, ret_gn_g, ret_gn_b, win_sink, w_in_cd, w_out_cd, diff_lambda, diff_subln_g, w_gate, w_up, w_down):
    cond = jnp.concatenate([c_ctx[None, :], c, jnp.zeros((SUBLANES - 1 - DEC_BATCH, D_MODEL), f32)], 0)
    mod = _modulation(cond, w_mod, b_mod).reshape(DEPTH, SUBLANES, 6, D_MODEL)

    rope_tabs = _rope_tables()
    gmat = jnp.asarray(_block_diag(np.full((HEAD_DIM, HEAD_DIM), 1.0 / HEAD_DIM, np.float32),
                                   N_CHUNK // HEAD_DIM), bf16)
    c64, s64 = _dft_tables(FNET_DIM)
    bdc = _block_diag(c64, FNET_GROUPS)
    bds = _block_diag(s64, FNET_GROUPS)
    dft_ctx = _dft_tables(SEQ)
    dft_lat = _dft_tables(DEC_SEQ)

    x_parts = [x_prompt.reshape(N_CTX, D_MODEL), x_sample.reshape(N_LAT, D_MODEL)]
    outs = {}
    for l in range(DEPTH):
        i = l // 2
        if l % 2 == 0:
            lgf = jnp.repeat(ret_log_gamma[i, 0], HEAD_DIM)[None, :]
            lgb = jnp.repeat(ret_log_gamma[i, 1], HEAD_DIM)[None, :]
            gn_g = ret_gn_g[i][None, :]
            gn_b = ret_gn_b[i][None, :]
            sink = win_sink[i][None, :]
            rope_tiles = tuple(range(0, 2 * RET_W // LANES)) + tuple(
                range(4 * RET_W // LANES, (4 * RET_W + WIN_W + KV_W) // LANES))
            kv_tile = (4 * RET_W + WIN_W) // LANES
            kv_shape = (BATCH, 1, KV_WIN, HEAD_DIM, SEQ)
            scale_tiles = tuple(range(4 * RET_W // LANES, (4 * RET_W + WIN_W) // LANES))
            proj, wk_t, wv_t = _proj(x_parts, mod, l, w_in_ab, i, scale_tiles, rope_tabs, rope_tiles,
                                     (kv_shape, kv_shape),
                                     {kv_tile: ("heads", 0, 0), kv_tile + 1: ("heads", 1, 0)})
            ro_c, wo_c, st_c = _ctx_ab(proj, ret_log_gamma[i], sink, lgf, lgb, gmat, gn_g, gn_b)
            ck = cache_win_k[:, i].reshape(DEC_BATCH, PAST_LEN, KV_W)
            cv = cache_win_v[:, i].reshape(DEC_BATCH, PAST_LEN, KV_W)
            ro_l, wo_l = _lat_ab(proj, ret_log_gamma[i], sink, ck, cv, state_ret, i, lgf, lgb, gmat, gn_g, gn_b)
            mix_a, mix_b, w_out = (ro_c, ro_l), (wo_c, wo_l), w_out_ab
            outs.setdefault('state', []).append(st_c[:, None])
            outs.setdefault('win_k', []).append(jnp.transpose(wk_t, (0, 1, 4, 2, 3)))
            outs.setdefault('win_v', []).append(jnp.transpose(wv_t, (0, 1, 4, 2, 3)))
        else:
            lam_init = 0.8 - 0.6 * math.exp(-0.3 * l)
            subln = diff_subln_g[i][None, :]
            rope_tiles = tuple(range(0, 2 * DIFF_W // LANES))
            plan = {}
            for h in range(H_DIFF):
                plan[DIFF_W // LANES + h] = ("pairs", 0, h)
                plan[2 * DIFF_W // LANES + h] = ("plain", 1, h)
            scale_tiles = tuple(range(0, DIFF_W // LANES))
            proj, dk_t, dv_h = _proj(
                x_parts, mod, l, w_in_cd, i, scale_tiles, rope_tabs, rope_tiles,
                ((BATCH, 1, H_DIFF, 2, HEAD_DIM, SEQ), (BATCH, 1, H_DIFF, SEQ, 2 * HEAD_DIM)), plan)
            a_c, z_c = _ctx_cd(proj, diff_lambda[i], subln, dft_ctx[0], dft_ctx[1], bdc, bds, lam_init)
            ck = jnp.transpose(cache_diff_k[:, i], (0, 2, 3, 4, 1)).reshape(DEC_BATCH, H_DIFF, PAIR_W, PAST_LEN)
            cv = jnp.transpose(cache_diff_v[:, i], (0, 2, 1, 3))
            a_l, z_l = _lat_cd(proj, ck, cv, diff_lambda[i], subln, dft_lat[0], dft_lat[1], bdc, bds, lam_init)
            mix_a, mix_b, w_out = (a_c, a_l), (z_c, z_l), w_out_cd
            outs.setdefault('diff_k', []).append(jnp.transpose(dk_t, (0, 1, 5, 2, 3, 4)))
            outs.setdefault('diff_v', []).append(jnp.transpose(dv_h, (0, 1, 3, 2, 4)))
        x_parts = _post(x_parts, mix_a, mix_b, mod, ln_g, ln_b, w_out, w_gate, w_up, w_down, l, i,
                        split_out=(l == DEPTH - 1))

    y_prompt = x_parts[0].reshape(BATCH, SEQ, D_MODEL)
    y_sample = x_parts[1].reshape(DEC_BATCH, DEC_SEQ, D_MODEL)
    cat = lambda parts: parts[0] if len(parts) == 1 else jnp.concatenate(parts, 1)
    return (y_prompt, y_sample, cat(outs['state']), cat(outs['win_k']), cat(outs['win_v']),
            cat(outs['diff_k']), cat(outs['diff_v']))
```

```python
import functools
import math

import jax
import jax.numpy as jnp
import numpy as np
from jax import lax
from jax.experimental import pallas as pl
from jax.experimental.pallas import tpu as pltpu

D_MODEL = 1024
BATCH = 32
SEQ = 256
DEPTH = 2
DEC_BATCH = 2
DEC_SEQ = 1024
PAST_LEN = 512
GRID_W = 64
HEAD_DIM = 64
ROPE_BASE = 10000.0
H_RET = 8
H_WIN = 8
KV_WIN = 2
G_WIN = H_WIN // KV_WIN
WINDOW = 128
H_DIFF = 6
FNET_GROUPS = 4
FNET_DIM = 64
D_FF = 256 * math.ceil(8 * D_MODEL / 3 / 256)
RET_W = H_RET * HEAD_DIM
WIN_W = H_WIN * HEAD_DIM
KV_W = KV_WIN * HEAD_DIM
AB_IN = 4 * RET_W + WIN_W + 2 * KV_W
DIFF_W = H_DIFF * 2 * HEAD_DIM
FNET_W = FNET_GROUPS * FNET_DIM
CD_IN = 3 * DIFF_W + FNET_W
ALPHA = (2 * DEPTH) ** 0.25
LN_EPS = 1e-5
QK_SCALE = HEAD_DIM ** -0.5
LOG2_E = math.log2(math.e)

N_CTX = BATCH * SEQ
N_LAT = DEC_BATCH * DEC_SEQ
N_TOK = N_CTX + N_LAT

LANES = 128
SUBLANES = 8
PAIR_W = 2 * HEAD_DIM
TM = 512
TM_PROJ = 1024
CTX_BLOCKS = N_CTX // TM
TOK_BLOCKS = N_TOK // TM
ROW_GROUPS = 2
FFN_SKEW = 2
W_SLOTS = 2
TQ = 256
LAT_SUB = 4
TQ_CD = 512
CTX_SEQS = 4
CTX_SEQS_CD = 4
RET_GROUP = 4
WIN_GROUP = 4
WIN_GROUP_LAT = 4
DIFF_GROUP_LAT = 1
DIFF_GROUP = 3
N_CHUNK = 256
MOD_TN = 1536
NEG_BIG = -1e30
VMEM_LIMIT = 56 * 1024 * 1024

f32 = jnp.float32
bf16 = jnp.bfloat16


def _params(n_axes):
    return pltpu.CompilerParams(dimension_semantics=("arbitrary",) * n_axes,
                                vmem_limit_bytes=VMEM_LIMIT)


def _dot(a, b):
    return jnp.dot(a, b, preferred_element_type=f32)


def _dot_nt(a, b):
    return lax.dot_general(a, b, (((1,), (1,)), ((), ())), preferred_element_type=f32)


def _ln(x):
    mu = jnp.mean(x, -1, keepdims=True)
    d = x - mu
    var = jnp.mean(d * d, -1, keepdims=True)
    return d * lax.rsqrt(var + LN_EPS)


def _silu(x):
    return x * jax.nn.sigmoid(x)


def _split_bf16(x):
    hi = x.astype(bf16)
    lo = (x - hi.astype(f32)).astype(bf16)
    return hi, lo


def _lane_half_mask(shape):
    return (lax.broadcasted_iota(jnp.int32, shape, len(shape) - 1) & HEAD_DIM) == 0


def _mod_kernel(c_ref, w_ref, b_ref, o_ref):
    layer = pl.program_id(0)
    a = _silu(c_ref[...])
    rows = a.shape[0]
    a_hi, a_lo = _split_bf16(a)
    w_hi, w_lo = _split_bf16(w_ref[0])
    both = _dot(jnp.concatenate([a_hi, a_lo], 0), w_hi)
    o_ref[0] = both[:rows] + both[rows:] + _dot(a_hi, w_lo) + b_ref[pl.ds(layer, 1), :]


def _modulation(cond, w_mod, b_mod):
    tn = MOD_TN
    rows = cond.shape[0]
    return pl.pallas_call(
        _mod_kernel,
        grid=(DEPTH, 6 * D_MODEL // tn),
        in_specs=[pl.BlockSpec((rows, D_MODEL), lambda l, j: (0, 0)),
                  pl.BlockSpec((1, D_MODEL, tn), lambda l, j: (l, 0, j)),
                  pl.BlockSpec((DEPTH, tn), lambda l, j: (0, j))],
        out_specs=pl.BlockSpec((1, rows, tn), lambda l, j: (l, 0, j)),
        out_shape=jax.ShapeDtypeStruct((DEPTH, rows, 6 * D_MODEL), f32),
        compiler_params=_params(2),
        name="modulation",
    )(cond, w_mod, b_mod)


def _tok(i, n_w):
    return jnp.maximum(i - n_w, 0)


def _ctx_blk(t, tm=TM):
    return jnp.minimum(t, N_CTX // tm - 1)


def _lat_blk(t, tm=TM):
    return jnp.maximum(t - N_CTX // tm, 0)


def _mod_row(t, tm=TM):
    return jnp.where(t < N_CTX // tm, 0, 1 + _lat_blk(t, tm) * tm // DEC_SEQ)


def _token_specs(parts, n_w, tm=TM):
    width = parts[0].shape[1]
    if len(parts) == 1:
        return [pl.BlockSpec((tm, width), lambda i: (_tok(i, n_w), 0))]
    return [pl.BlockSpec((tm, width), lambda i: (_ctx_blk(_tok(i, n_w), tm), 0)),
            pl.BlockSpec((tm, width), lambda i: (_lat_blk(_tok(i, n_w), tm), 0))]


def _pick(refs, is_ctx, rs):
    return refs[0 if (len(refs) == 1 or is_ctx) else 1][rs, :]


def _rope_pair(y, cos, sin_signed):
    first_half = (lax.broadcasted_iota(jnp.int32, y.shape, 1) & (HEAD_DIM // 2)) == 0
    swapped = jnp.where(first_half, pltpu.roll(y, LANES - HEAD_DIM // 2, 1), pltpu.roll(y, HEAD_DIM // 2, 1))
    return y * cos + swapped * sin_signed


def _proj_kernel(*refs, n_x, n_cache, n_w, layer, rope_tiles, scale_tiles, cache_plan):
    x_refs = refs[:n_x]
    mod_ref, w_ref, cos_ref, sin_ref, o_ref = refs[n_x:n_x + 5]
    cache_refs = refs[n_x + 5:n_x + 5 + n_cache]
    wbf_ref, u_ref, wf_ref, w_sem = refs[n_x + 5 + n_cache:]
    i = pl.program_id(0)

    @pl.when(i == 0)
    def _():
        copies = [pltpu.make_async_copy(w_ref.at[layer, :, c * N_CHUNK:(c + 1) * N_CHUNK], wf_ref.at[c], w_sem.at[c])
                  for c in range(n_w)]
        for cp in copies:
            cp.start()
        for c, cp in enumerate(copies):
            cp.wait()
            wbf_ref[c] = wf_ref[c].astype(bf16)

    def tokens(is_ctx):
        x_ref = x_refs[0] if is_ctx else x_refs[-1]
        shift = mod_ref[0, 0:1, :]
        scale = mod_ref[0, 1:2, :]
        groups = [slice(b * SEQ, (b + 1) * SEQ) for b in range(TM_PROJ // SEQ)]
        for rs in groups:
            u_ref[rs, :] = (_ln(x_ref[rs, :]) * (1.0 + scale) + shift).astype(bf16)
        for c in range(n_w):
            y_all = _dot(u_ref[...], wbf_ref[c])
            for b, rs in enumerate(groups):
                y = y_all[rs, :]
                for t in range(N_CHUNK // LANES):
                    tile = c * (N_CHUNK // LANES) + t
                    piece = y[:, t * LANES:(t + 1) * LANES]
                    if tile in rope_tiles and not is_ctx:
                        piece = _rope_pair(piece, cos_ref[rs, :], sin_ref[rs, :])
                    if tile in scale_tiles:
                        piece = piece * (QK_SCALE * LOG2_E)
                    o_ref[rs, tile * LANES:(tile + 1) * LANES] = piece.astype(o_ref.dtype)
                    if tile in cache_plan and is_ctx:
                        kind, out_idx, slot = cache_plan[tile]
                        c_ref = cache_refs[out_idx]
                        if kind == "plain":
                            c_ref[b, 0, slot] = piece
                        else:
                            piece_t = piece.T
                            if kind == "heads":
                                c_ref[b, 0, 0] = piece_t[0:HEAD_DIM]
                                c_ref[b, 0, 1] = piece_t[HEAD_DIM:]
                            else:
                                c_ref[b, 0, slot, 0] = piece_t[0:HEAD_DIM]
                                c_ref[b, 0, slot, 1] = piece_t[HEAD_DIM:]

    t = i - 1

    @pl.when(jnp.logical_and(t >= 0, t < N_CTX // TM_PROJ))
    def _():
        tokens(True)

    @pl.when(t >= N_CTX // TM_PROJ)
    def _():
        tokens(False)


def _proj(x_parts, mod, mod_layer, w_all, layer, scale_tiles, rope_tabs, rope_tiles, cache_shapes, cache_plan):
    n_out = w_all.shape[2]
    n_w = n_out // N_CHUNK
    tm = TM_PROJ
    nb = DEC_SEQ // tm
    tok = lambda i: _tok(i, 1)
    in_specs = _token_specs(x_parts, 1, tm) + [
        pl.BlockSpec((None, 1, 6, D_MODEL), lambda i: (mod_layer, _mod_row(tok(i), tm), 0, 0)),
        pl.BlockSpec(memory_space=pl.ANY),
        pl.BlockSpec((tm, LANES), lambda i: (_lat_blk(tok(i), tm) % nb, 0)),
        pl.BlockSpec((tm, LANES), lambda i: (_lat_blk(tok(i), tm) % nb, 0))]
    out_specs = [pl.BlockSpec((tm, n_out), lambda i: (tok(i), 0))]
    out_shape = [jax.ShapeDtypeStruct((N_TOK, n_out), bf16)]
    for shp in cache_shapes:
        blk = (tm // SEQ,) + tuple(shp[1:])
        out_specs.append(pl.BlockSpec(blk, lambda i, nd=len(shp): (_ctx_blk(tok(i), tm),) + (0,) * (nd - 1)))
        out_shape.append(jax.ShapeDtypeStruct(tuple(shp), f32))
    return pl.pallas_call(
        functools.partial(_proj_kernel, n_x=len(x_parts), n_cache=len(cache_shapes), n_w=n_w,
                          rope_tiles=frozenset(rope_tiles), scale_tiles=frozenset(scale_tiles),
                          cache_plan=dict(cache_plan), layer=layer),
        grid=(1 + N_TOK // tm,),
        in_specs=in_specs,
        out_specs=out_specs,
        out_shape=out_shape,
        scratch_shapes=[pltpu.VMEM((n_w, D_MODEL, N_CHUNK), bf16), pltpu.VMEM((tm, D_MODEL), bf16),
                        pltpu.VMEM((n_w, D_MODEL, N_CHUNK), f32), pltpu.SemaphoreType.DMA((n_w,))],
        compiler_params=_params(1),
        name="proj",
    )(*x_parts, mod, w_all, *rope_tabs)


def _post_kernel(*refs, n_x, n_y, ka, kb, n_w, layer):
    x_refs = refs[:n_x]
    (ac_ref, al_ref, bc_ref, bl_ref, mod_ref, lng_ref, lnb_ref,
     wo_ref, wg_ref, wu_ref, wd_ref) = refs[n_x:n_x + 11]
    y_refs = refs[n_x + 11:n_x + 11 + n_y]
    (wo_s, wg_s, wu_s, wd_s, x1_ref, u_ref, h_ref, y_ref,
     wg_f, wu_f, wd_f, w_sem) = refs[n_x + 11 + n_y:]
    i = pl.program_id(0)

    def chunk_copies(c, slot):
        cols = pl.ds(pl.multiple_of(c * N_CHUNK, N_CHUNK), N_CHUNK)
        return (pltpu.make_async_copy(wg_ref.at[layer, :, cols], wg_f.at[slot], w_sem.at[0, slot]),
                pltpu.make_async_copy(wu_ref.at[layer, :, cols], wu_f.at[slot], w_sem.at[1, slot]),
                pltpu.make_async_copy(wd_ref.at[layer, cols, :], wd_f.at[slot], w_sem.at[2, slot]))

    lead = n_w - 1
    gate1 = mod_ref[0, 2:3, :]
    shift2 = mod_ref[0, 3:4, :]
    scale2 = mod_ref[0, 4:5, :]
    gate2 = mod_ref[0, 5:6, :]
    groups = [slice(r * TM // ROW_GROUPS, (r + 1) * TM // ROW_GROUPS) for r in range(ROW_GROUPS)]

    def mix_in(rs, is_ctx):
        a = _pick((ac_ref, al_ref), is_ctx, rs)
        b = _pick((bc_ref, bl_ref), is_ctx, rs)
        pieces = ([a[:, c:c + N_CHUNK] for c in range(0, ka, N_CHUNK)]
                  + [b[:, c:c + N_CHUNK] for c in range(0, kb, N_CHUNK)])
        h = functools.reduce(lambda s, p: s + p, [_dot(p, wo_s[c]) for c, p in enumerate(pieces)])
        x1 = _ln(ALPHA * _pick(x_refs, is_ctx, rs) + gate1 * h) * lng_ref[0, 0:1, :] + lnb_ref[0, 0:1, :]
        x1_ref[rs, :] = x1
        u_ref[rs, :] = (_ln(x1) * (1.0 + scale2) + shift2).astype(bf16)

    def finish(rs, ffn, is_ctx):
        y = _ln(ALPHA * x1_ref[rs, :] + gate2 * ffn) * lng_ref[0, 1:2, :] + lnb_ref[0, 1:2, :]
        y_refs[0 if (n_y == 1 or is_ctx) else 1][rs, :] = y

    @pl.when(i < n_w)
    def _():
        slot = i % W_SLOTS

        @pl.when(i == 0)
        def _():
            for c in range(W_SLOTS):
                for cp in chunk_copies(c, c):
                    cp.start()
            for c in range((ka + kb) // N_CHUNK):
                wo_s[c] = wo_ref[0, c * N_CHUNK:(c + 1) * N_CHUNK, :].astype(bf16)
            for rs in groups:
                mix_in(rs, True)
                y_ref[rs, :] = jnp.zeros((TM // ROW_GROUPS, D_MODEL), f32)

        for cp in chunk_copies(i, slot):
            cp.wait()
        wg_s[i] = wg_f[slot].astype(bf16)
        wu_s[i] = wu_f[slot].astype(bf16)
        wd_s[i] = wd_f[slot].astype(bf16)

        @pl.when(i + W_SLOTS < n_w)
        def _():
            for cp in chunk_copies(i + W_SLOTS, slot):
                cp.start()

        g = _dot(u_ref[...], wg_s[i])
        up = _dot(u_ref[...], wu_s[i])
        y_ref[...] += _dot((_silu(g) * up).astype(bf16), wd_s[i])

        @pl.when(i == lead)
        def _():
            for rs in groups:
                finish(rs, y_ref[rs, :], True)

    def token_block(is_ctx):
        for rs in groups:
            mix_in(rs, is_ctx)

        def ffn_chunk(rs, c):
            g = _dot(u_ref[rs, :], wg_s[c])
            up = _dot(u_ref[rs, :], wu_s[c])
            h_ref[rs, c * N_CHUNK:(c + 1) * N_CHUNK] = (_silu(g) * up).astype(bf16)

        def ffn_down(rs):
            finish(rs, functools.reduce(
                lambda s, p: s + p,
                [_dot(h_ref[rs, c * N_CHUNK:(c + 1) * N_CHUNK], wd_s[c]) for c in range(n_w)]), is_ctx)

        for c in range(n_w + FFN_SKEW * (ROW_GROUPS - 1)):
            for r, rs in enumerate(groups):
                cc = c - FFN_SKEW * r
                if 0 <= cc < n_w:
                    ffn_chunk(rs, cc)
                if cc == n_w - 1:
                    ffn_down(rs)

    blk = i - lead

    @pl.when(jnp.logical_and(i >= n_w, blk < CTX_BLOCKS))
    def _():
        token_block(True)

    @pl.when(blk >= CTX_BLOCKS)
    def _():
        token_block(False)


def _post(x_parts, mix_a, mix_b, mod, ln_g, ln_b, w_out, w_gate, w_up, w_down, layer, mix_layer, split_out):
    ka, kb = mix_a[0].shape[1], mix_b[0].shape[1]
    n_w = D_FF // N_CHUNK
    lead = n_w - 1
    tok = lambda i: _tok(i, lead)
    lay = lambda i: (layer, 0, 0)
    in_specs = (_token_specs(x_parts, lead) + _token_specs(mix_a, lead) + _token_specs(mix_b, lead) + [
        pl.BlockSpec((None, 1, 6, D_MODEL), lambda i: (layer, _mod_row(tok(i)), 0, 0)),
        pl.BlockSpec((1, 2, D_MODEL), lay),
        pl.BlockSpec((1, 2, D_MODEL), lay),
        pl.BlockSpec((1, ka + kb, D_MODEL), lambda i: (mix_layer, 0, 0), pipeline_mode=pl.Buffered(1)),
        pl.BlockSpec(memory_space=pl.ANY), pl.BlockSpec(memory_space=pl.ANY), pl.BlockSpec(memory_space=pl.ANY)])
    if split_out:
        out_specs = [pl.BlockSpec((TM, D_MODEL), lambda i: (_ctx_blk(tok(i)), 0)),
                     pl.BlockSpec((TM, D_MODEL), lambda i: (_lat_blk(tok(i)), 0))]
        out_shape = [jax.ShapeDtypeStruct((N_CTX, D_MODEL), f32), jax.ShapeDtypeStruct((N_LAT, D_MODEL), f32)]
    else:
        out_specs = [pl.BlockSpec((TM, D_MODEL), lambda i: (tok(i), 0))]
        out_shape = [jax.ShapeDtypeStruct((N_TOK, D_MODEL), f32)]
    return pl.pallas_call(
        functools.partial(_post_kernel, n_x=len(x_parts), n_y=len(out_shape), ka=ka, kb=kb, n_w=n_w, layer=layer),
        grid=(lead + TOK_BLOCKS,),
        in_specs=in_specs,
        out_specs=out_specs,
        out_shape=out_shape,
        scratch_shapes=[pltpu.VMEM(((ka + kb) // N_CHUNK, N_CHUNK, D_MODEL), bf16),
                        pltpu.VMEM((n_w, D_MODEL, N_CHUNK), bf16),
                        pltpu.VMEM((n_w, D_MODEL, N_CHUNK), bf16), pltpu.VMEM((n_w, N_CHUNK, D_MODEL), bf16),
                        pltpu.VMEM((TM, D_MODEL), f32), pltpu.VMEM((TM, D_MODEL), bf16),
                        pltpu.VMEM((TM, D_FF), bf16), pltpu.VMEM((TM, D_MODEL), f32),
                        pltpu.VMEM((W_SLOTS, D_MODEL, N_CHUNK), f32), pltpu.VMEM((W_SLOTS, D_MODEL, N_CHUNK), f32),
                        pltpu.VMEM((W_SLOTS, N_CHUNK, D_MODEL), f32), pltpu.SemaphoreType.DMA((3, W_SLOTS))],
        compiler_params=_params(1),
        name="post",
    )(*x_parts, *mix_a, *mix_b, mod, ln_g, ln_b, w_out, w_gate, w_up, w_down)


def _group_norm_gate(ro, rg, gmat, gn_g, gn_b):
    def gmean(parts):
        cols = []
        for c in range(0, RET_W, N_CHUNK):
            cols.append(sum(_dot(p[:, c:c + N_CHUNK], gmat) for p in parts))
        return jnp.concatenate(cols, -1)

    d = ro - gmean(_split_bf16(ro))
    var = gmean([(d * d).astype(bf16)])
    y = d * lax.rsqrt(var + LN_EPS) * gn_g + gn_b
    return _silu(rg.astype(f32)) * y


def _dup_head(x, j):
    first = _lane_half_mask(x.shape)
    keep = first if j == 0 else jnp.logical_not(first)
    xm = jnp.where(keep, x.astype(f32), 0.0)
    return xm + pltpu.roll(xm, HEAD_DIM, 1)


def _softmax_parts(scores, sink):
    m = sink
    for s in scores:
        m = jnp.maximum(m, jnp.max(s, -1, keepdims=True))
    es = [jnp.exp2(s - m) for s in scores]
    denom = jnp.exp2(sink - m)
    for e in es:
        denom = denom + jnp.sum(e, -1, keepdims=True)
    return es, denom


def _retention_tables(lg_ref, lgf_ref, lgb_ref, dmask_ref, kdec_ref, n):
    row = lax.broadcasted_iota(jnp.int32, (n, n), 0)
    col = lax.broadcasted_iota(jnp.int32, (n, n), 1)
    diff = (row - col).astype(f32)
    diag = jnp.where(row == col, 2.0 * QK_SCALE, QK_SCALE)
    for h in range(H_RET):
        dmask_ref[h] = jnp.exp(jnp.where(diff >= 0, lg_ref[0, h] * diff, -lg_ref[1, h] * diff)) * diag
    t = lax.broadcasted_iota(jnp.int32, (n, RET_W), 0).astype(f32)
    kdec_ref[0] = jnp.exp(lgf_ref[...] * (n - 1.0 - t)) * QK_SCALE
    kdec_ref[1] = jnp.exp(lgb_ref[...] * t) * QK_SCALE


def _retention_intra(pairs, q_of, k_of, v_of, dmask_ref):
    first = _lane_half_mask(k_of(pairs[0]).shape)
    masked = {}
    for p in pairs:
        kb = k_of(p)
        for e in range(2):
            keep = first if e == 0 else jnp.logical_not(first)
            s = _dot_nt(q_of(p), jnp.where(keep, kb, jnp.zeros_like(kb))) * dmask_ref[2 * p + e]
            masked[p, e] = s.astype(bf16)
    outs = {}
    for p in pairs:
        pv = [_dot(masked[p, e], v_of(p)) for e in range(2)]
        outs[p] = jnp.where(_lane_half_mask(pv[0].shape), pv[0], pv[1])
    return outs


def _window_group(subs, q_of, k_parts_of, v_parts_of, masks, sink_of):
    scores = {}
    for key in subs:
        parts = [_dot_nt(q_of(key), k) for k in k_parts_of(key)]
        scores[key] = [sc if mk is None else jnp.where(mk, sc, NEG_BIG) for sc, mk in zip(parts, masks)]
    probs = {}
    for key in subs:
        es, denom = _softmax_parts(scores[key], sink_of(key))
        probs[key] = ([ex.astype(bf16) for ex in es], denom)
    outs = {}
    for key in subs:
        es, denom = probs[key]
        pv = functools.reduce(lambda x, y: x + y, [_dot(ex, v) for ex, v in zip(es, v_parts_of(key))])
        outs[key] = pv / denom
    return outs


def _ctx_ab_kernel(lg_ref, sink_ref, rq_ref, rk_ref, rv_ref, rg_ref, wq_ref, wk_ref, wv_ref,
                   lgf_ref, lgb_ref, gmat_ref, gng_ref, gnb_ref,
                   ro_ref, wo_ref, st_ref, dmask_ref, kdec_ref, ret_ref):
    t_len = SEQ

    @pl.when(pl.program_id(0) == 0)
    def _():
        _retention_tables(lg_ref, lgf_ref, lgb_ref, dmask_ref, kdec_ref, t_len)

    first = _lane_half_mask((t_len, PAIR_W))
    for sq in range(CTX_SEQS):
        rows = slice(sq * t_len, (sq + 1) * t_len)
        psl = lambda p: slice(p * PAIR_W, (p + 1) * PAIR_W)
        for p0 in range(0, H_RET // 2, RET_GROUP):
            pairs = list(range(p0, p0 + RET_GROUP))
            intra = _retention_intra(pairs, lambda p: rq_ref[rows, psl(p)], lambda p: rk_ref[rows, psl(p)],
                                     lambda p: rv_ref[rows, psl(p)], dmask_ref)
            for p in pairs:
                ret_ref[rows, psl(p)] = intra[p]
        for p in range(H_RET // 2):
            sl = psl(p)
            kb = rk_ref[rows, sl]
            v = rv_ref[rows, sl]
            for d in range(2):
                kd_t = (kb * kdec_ref[d, :, sl]).T.astype(bf16)
                st = _dot(kd_t, v)
                st_ref[sq, d, 2 * p] = st[0:HEAD_DIM, 0:HEAD_DIM]
                st_ref[sq, d, 2 * p + 1] = pltpu.roll(st[HEAD_DIM:, :], HEAD_DIM, 1)[:, 0:HEAD_DIM]
        ro_ref[rows, :] = _group_norm_gate(ret_ref[rows, :], rg_ref[rows, :], gmat_ref[...], gng_ref[...],
                                           gnb_ref[...]).astype(bf16)

        k_dup = [_dup_head(wk_ref[rows, :], j).astype(bf16) for j in range(KV_WIN)]
        v_dup = [_dup_head(wv_ref[rows, :], j).astype(bf16) for j in range(KV_WIN)]

        def q_masked(key):
            qp, e = key
            qb = wq_ref[rows, qp * PAIR_W:(qp + 1) * PAIR_W]
            return jnp.where(first if e == 0 else jnp.logical_not(first), qb, jnp.zeros_like(qb))

        kv_of = lambda key: key[0] * 2 // G_WIN
        for g0 in range(0, H_WIN // 2, WIN_GROUP):
            subs = [(qp, e) for qp in range(g0, g0 + WIN_GROUP) for e in range(2)]
            outs = _window_group(subs, q_masked, lambda key: [k_dup[kv_of(key)]], lambda key: [v_dup[kv_of(key)]],
                                 [None], lambda key: sink_ref[0, 2 * key[0] + key[1]] * LOG2_E)
            for qp in range(g0, g0 + WIN_GROUP):
                wo_ref[rows, qp * PAIR_W:(qp + 1) * PAIR_W] = jnp.where(first, outs[qp, 0], outs[qp, 1]).astype(bf16)


def _ctx_ab(proj, log_gamma, sink, lgf_lanes, lgb_lanes, gmat, gn_g, gn_b):
    t = SEQ
    tb = CTX_SEQS * t
    smem = pl.BlockSpec(memory_space=pltpu.SMEM)
    const = lambda b: (0, 0)
    col = lambda c: (lambda b: (b, c))
    return pl.pallas_call(
        _ctx_ab_kernel,
        grid=(BATCH // CTX_SEQS,),
        in_specs=[smem, smem,
                  pl.BlockSpec((tb, RET_W), col(0)), pl.BlockSpec((tb, RET_W), col(1)),
                  pl.BlockSpec((tb, RET_W), col(2)), pl.BlockSpec((tb, RET_W), col(3)),
                  pl.BlockSpec((tb, WIN_W), col(4)),
                  pl.BlockSpec((tb, KV_W), col((4 * RET_W + WIN_W) // KV_W)),
                  pl.BlockSpec((tb, KV_W), col((4 * RET_W + WIN_W) // KV_W + 1)),
                  pl.BlockSpec((1, RET_W), const), pl.BlockSpec((1, RET_W), const),
                  pl.BlockSpec((N_CHUNK, N_CHUNK), const),
                  pl.BlockSpec((1, RET_W), const), pl.BlockSpec((1, RET_W), const)],
        out_specs=[pl.BlockSpec((tb, RET_W), lambda b: (b, 0)),
                   pl.BlockSpec((tb, WIN_W), lambda b: (b, 0)),
                   pl.BlockSpec((CTX_SEQS, 2, H_RET, HEAD_DIM, HEAD_DIM), lambda b: (b, 0, 0, 0, 0))],
        out_shape=[jax.ShapeDtypeStruct((BATCH * t, RET_W), bf16),
                   jax.ShapeDtypeStruct((BATCH * t, WIN_W), bf16),
                   jax.ShapeDtypeStruct((BATCH, 2, H_RET, HEAD_DIM, HEAD_DIM), f32)],
        scratch_shapes=[pltpu.VMEM((H_RET, t, t), f32), pltpu.VMEM((2, t, RET_W), f32),
                        pltpu.VMEM((tb, RET_W), f32)],
        compiler_params=_params(1),
        name="ctx_ab",
    )(log_gamma, sink, proj, proj, proj, proj, proj, proj, proj, lgf_lanes, lgb_lanes, gmat, gn_g, gn_b)


def _pair_state(s0_ref, d, p):
    zero = jnp.zeros((HEAD_DIM, HEAD_DIM), f32)
    top = jnp.concatenate([s0_ref[0, 0, d, 2 * p], zero], 1)
    bottom = jnp.concatenate([zero, s0_ref[0, 0, d, 2 * p + 1]], 1)
    return jnp.concatenate([top, bottom], 0)


def _lat_ab_kernel(lg_ref, sink_ref, rq_ref, rk_ref, rv_ref, rg_ref, wq_ref, wk_ref, wv_ref, ck_ref, cv_ref,
                   s0_ref, lgf_ref, lgb_ref, gmat_ref, gng_ref, gnb_ref,
                   ro_ref, wo_ref, ret_ref, dmask_ref, kdec_ref, qdec_ref, sf_ref, sb_ref):
    t_len = DEC_SEQ
    n_chunks = t_len // TQ
    step = pl.program_id(1)
    first = _lane_half_mask((TQ, PAIR_W))

    @pl.when(jnp.logical_and(pl.program_id(0) == 0, step == 0))
    def _():
        _retention_tables(lg_ref, lgf_ref, lgb_ref, dmask_ref, kdec_ref, TQ)
        t = lax.broadcasted_iota(jnp.int32, (TQ, RET_W), 0).astype(f32)
        qdec_ref[0] = jnp.exp(lgf_ref[...] * (t + 1.0))
        qdec_ref[1] = jnp.exp(lgb_ref[...] * (TQ - t))

    @pl.when(step == 0)
    def _():
        r = lax.broadcasted_iota(jnp.int32, (PAIR_W, PAIR_W), 0)
        c_ = lax.broadcasted_iota(jnp.int32, (PAIR_W, PAIR_W), 1)
        same_head = (r < HEAD_DIM) == (c_ < HEAD_DIM)
        for p in range(H_RET // 2):
            sl = slice(p * PAIR_W, (p + 1) * PAIR_W)
            kv = []
            for c in range(n_chunks):
                rows = slice(c * TQ, (c + 1) * TQ)
                kc = rk_ref[rows, sl]
                vc = rv_ref[rows, sl]
                kv.append([jnp.where(same_head, _dot((kc * kdec_ref[d, :, sl]).T.astype(bf16), vc), 0.0)
                           for d in range(2)])
            state = _pair_state(s0_ref, 0, p)
            for c in range(n_chunks):
                sf_ref[c, p] = state
                state = state * jnp.exp(lgf_ref[:, sl] * TQ) + kv[c][0]
            state = _pair_state(s0_ref, 1, p)
            for c in reversed(range(n_chunks)):
                sb_ref[c, p] = state
                state = state * jnp.exp(lgb_ref[:, sl] * TQ) + kv[c][1]

    for sub in range(LAT_SUB):
        chunk = step * LAT_SUB + sub
        q0 = pl.multiple_of(chunk * TQ, TQ)
        rows = slice(sub * TQ, (sub + 1) * TQ)
        psl = lambda p: slice(p * PAIR_W, (p + 1) * PAIR_W)
        intra = {}
        for p0 in range(0, H_RET // 2, RET_GROUP):
            intra.update(_retention_intra(list(range(p0, p0 + RET_GROUP)), lambda p: rq_ref[rows, psl(p)],
                                          lambda p: rk_ref[pl.ds(q0, TQ), psl(p)],
                                          lambda p: rv_ref[pl.ds(q0, TQ), psl(p)], dmask_ref))
        for p in range(H_RET // 2):
            sl = psl(p)
            q = rq_ref[rows, sl]
            o = intra[p]
            o = o + _dot(q, sf_ref[chunk, p].astype(bf16)) * qdec_ref[0, :, sl]
            o = o + _dot(q, sb_ref[chunk, p].astype(bf16)) * qdec_ref[1, :, sl]
            ret_ref[rows, sl] = o
        ro_ref[rows, :] = _group_norm_gate(ret_ref[rows, :], rg_ref[rows, :], gmat_ref[...], gng_ref[...],
                                           gnb_ref[...]).astype(bf16)

        band = TQ + 2 * WINDOW
        k_start = pl.multiple_of(jnp.clip(q0 - WINDOW, 0, t_len - band), LANES)
        qi = q0 + lax.broadcasted_iota(jnp.int32, (TQ, band), 0)
        kj = k_start + lax.broadcasted_iota(jnp.int32, (TQ, band), 1)
        in_band = jnp.abs(qi - kj) <= WINDOW
        k_parts = [[_dup_head(wk_ref[pl.ds(k_start, band), :], j).astype(bf16), _dup_head(ck_ref[0], j).astype(bf16)]
                   for j in range(KV_WIN)]
        v_parts = [[_dup_head(wv_ref[pl.ds(k_start, band), :], j).astype(bf16), _dup_head(cv_ref[0], j).astype(bf16)]
                   for j in range(KV_WIN)]

        def q_masked(key):
            qp, e = key
            qb = wq_ref[rows, qp * PAIR_W:(qp + 1) * PAIR_W]
            return jnp.where(first if e == 0 else jnp.logical_not(first), qb, jnp.zeros_like(qb))

        kv_of = lambda key: key[0] * 2 // G_WIN
        for g0 in range(0, H_WIN // 2, WIN_GROUP_LAT):
            subs = [(qp, e) for qp in range(g0, g0 + WIN_GROUP_LAT) for e in range(2)]
            outs = _window_group(subs, q_masked, lambda key: k_parts[kv_of(key)], lambda key: v_parts[kv_of(key)],
                                 [in_band, None], lambda key: sink_ref[0, 2 * key[0] + key[1]] * LOG2_E)
            for qp in range(g0, g0 + WIN_GROUP_LAT):
                wo_ref[rows, qp * PAIR_W:(qp + 1) * PAIR_W] = jnp.where(first, outs[qp, 0], outs[qp, 1]).astype(bf16)


def _lat_ab(proj, log_gamma, sink, ck, cv, state, layer, lgf_lanes, lgb_lanes, gmat, gn_g, gn_b):
    t = DEC_SEQ
    tb = LAT_SUB * TQ
    nq = t // tb
    smem = pl.BlockSpec(memory_space=pltpu.SMEM)
    const = lambda b, i: (0, 0)
    qcol = lambda c: (lambda b, i: (N_CTX // tb + b * nq + i, c))
    bcol = lambda c: (lambda b, i: (N_CTX // t + b, c))
    kv_col = (4 * RET_W + WIN_W) // KV_W
    return pl.pallas_call(
        _lat_ab_kernel,
        grid=(DEC_BATCH, nq),
        in_specs=[smem, smem,
                  pl.BlockSpec((tb, RET_W), qcol(0)), pl.BlockSpec((t, RET_W), bcol(1)),
                  pl.BlockSpec((t, RET_W), bcol(2)), pl.BlockSpec((tb, RET_W), qcol(3)),
                  pl.BlockSpec((tb, WIN_W), qcol(4)),
                  pl.BlockSpec((t, KV_W), bcol(kv_col)), pl.BlockSpec((t, KV_W), bcol(kv_col + 1)),
                  pl.BlockSpec((1, PAST_LEN, KV_W), lambda b, i: (b, 0, 0)),
                  pl.BlockSpec((1, PAST_LEN, KV_W), lambda b, i: (b, 0, 0)),
                  pl.BlockSpec((1, 1, 2, H_RET, HEAD_DIM, HEAD_DIM), lambda b, i: (b, layer, 0, 0, 0, 0)),
                  pl.BlockSpec((1, RET_W), const), pl.BlockSpec((1, RET_W), const),
                  pl.BlockSpec((N_CHUNK, N_CHUNK), const),
                  pl.BlockSpec((1, RET_W), const), pl.BlockSpec((1, RET_W), const)],
        out_specs=[pl.BlockSpec((tb, RET_W), lambda b, i: (b * nq + i, 0)),
                   pl.BlockSpec((tb, WIN_W), lambda b, i: (b * nq + i, 0))],
        out_shape=[jax.ShapeDtypeStruct((DEC_BATCH * t, RET_W), bf16),
                   jax.ShapeDtypeStruct((DEC_BATCH * t, WIN_W), bf16)],
        scratch_shapes=[pltpu.VMEM((tb, RET_W), f32), pltpu.VMEM((H_RET, TQ, TQ), f32),
                        pltpu.VMEM((2, TQ, RET_W), f32), pltpu.VMEM((2, TQ, RET_W), f32),
                        pltpu.VMEM((t // TQ, H_RET // 2, PAIR_W, PAIR_W), f32),
                        pltpu.VMEM((t // TQ, H_RET // 2, PAIR_W, PAIR_W), f32)],
        compiler_params=_params(2),
        name="lat_ab",
    )(log_gamma, sink, proj, proj, proj, proj, proj, proj, proj, ck, cv, state,
      lgf_lanes, lgb_lanes, gmat, gn_g, gn_b)


def _lambda_full(lam_ref, lam_init):
    lam = lam_ref[...]
    a = jnp.sum(lam[0:1, :] * lam[1:2, :], -1, keepdims=True)
    b = jnp.sum(lam[2:3, :] * lam[3:4, :], -1, keepdims=True)
    return jnp.exp(a) - jnp.exp(b) + lam_init


def _diff_heads(q_of, k_parts_of, v_parts_of, lam, subln, lam_init, group):
    res = []
    for h0 in range(0, H_DIFF, group):
        res += _diff_head_group(range(h0, h0 + group), q_of, k_parts_of, v_parts_of, lam, subln, lam_init)
    return res


def _diff_head_group(heads, q_of, k_parts_of, v_parts_of, lam, subln, lam_init):
    subs = [(h, e) for h in heads for e in range(2)]
    scores = {}
    for h, e in subs:
        q = q_of(h)
        fm = _lane_half_mask(q.shape)
        q_sub = jnp.where(fm if e == 0 else jnp.logical_not(fm), q, jnp.zeros_like(q))
        scores[h, e] = [_dot(q_sub, k) if transposed else _dot_nt(q_sub, k) for k, transposed in k_parts_of(h)]
    probs = {}
    for key in subs:
        m = scores[key][0].max(-1, keepdims=True)
        for sc in scores[key][1:]:
            m = jnp.maximum(m, sc.max(-1, keepdims=True))
        es = [jnp.exp2(sc - m) for sc in scores[key]]
        denom = es[0].sum(-1, keepdims=True)
        for ex in es[1:]:
            denom = denom + ex.sum(-1, keepdims=True)
        probs[key] = ([ex.astype(bf16) for ex in es], denom)
    outs = {}
    for h, e in subs:
        es, denom = probs[h, e]
        pv = functools.reduce(lambda x, y: x + y, [_dot(ex, v) for v, ex in zip(v_parts_of(h), es)])
        outs[h, e] = pv / denom
    res = []
    for h in heads:
        a = outs[h, 0] - lam * outs[h, 1]
        res.append(a * lax.rsqrt(jnp.mean(a * a, -1, keepdims=True) + LN_EPS) * subln * (1.0 - lam_init))
    return res


def _fourier_rows(ct_ref, st_ref, z, bdc_ref, bds_ref):
    zc = _dot(z, bdc_ref[...].astype(bf16)).astype(bf16)
    zs = _dot(z, bds_ref[...].astype(bf16)).astype(bf16)
    return _dot(ct_ref[...].astype(bf16), zc) - _dot(st_ref[...].astype(bf16), zs)


def _ctx_cd_kernel(q_ref, k_ref, v_ref, z_ref, lam_ref, subln_ref, ct_ref, st_ref, bdc_ref, bds_ref,
                   a_ref, zf_ref, *, lam_init):
    lam = _lambda_full(lam_ref, lam_init)
    for sq in range(CTX_SEQS_CD):
        rows = slice(sq * SEQ, (sq + 1) * SEQ)
        sl = lambda h: slice(h * PAIR_W, (h + 1) * PAIR_W)
        heads = _diff_heads(lambda h: q_ref[rows, sl(h)], lambda h: [(k_ref[rows, sl(h)], False)],
                            lambda h: [v_ref[rows, sl(h)]], lam, subln_ref[...], lam_init, DIFF_GROUP)
        for h in range(H_DIFF):
            a_ref[rows, sl(h)] = heads[h].astype(bf16)
        zf_ref[rows, :] = _fourier_rows(ct_ref, st_ref, z_ref[rows, :], bdc_ref, bds_ref).astype(bf16)


def _ctx_cd(proj, lam, subln, ct, st, bdc, bds, lam_init):
    t = SEQ
    tb = CTX_SEQS_CD * t
    const = lambda b: (0, 0)
    col = lambda c: (lambda b: (b, c))
    return pl.pallas_call(
        functools.partial(_ctx_cd_kernel, lam_init=lam_init),
        grid=(BATCH // CTX_SEQS_CD,),
        in_specs=[pl.BlockSpec((tb, DIFF_W), col(0)), pl.BlockSpec((tb, DIFF_W), col(1)),
                  pl.BlockSpec((tb, DIFF_W), col(2)), pl.BlockSpec((tb, FNET_W), col(3 * DIFF_W // FNET_W)),
                  pl.BlockSpec((4, HEAD_DIM), const), pl.BlockSpec((1, PAIR_W), const),
                  pl.BlockSpec((t, t), const), pl.BlockSpec((t, t), const),
                  pl.BlockSpec((FNET_W, FNET_W), const), pl.BlockSpec((FNET_W, FNET_W), const)],
        out_specs=[pl.BlockSpec((tb, DIFF_W), lambda b: (b, 0)), pl.BlockSpec((tb, FNET_W), lambda b: (b, 0))],
        out_shape=[jax.ShapeDtypeStruct((BATCH * t, DIFF_W), bf16),
                   jax.ShapeDtypeStruct((BATCH * t, FNET_W), bf16)],
        compiler_params=_params(1),
        name="ctx_cd",
    )(proj, proj, proj, proj, lam, subln, ct, st, bdc, bds)


def _lat_cd_kernel(q_ref, k_ref, v_ref, z_ref, ckt_ref, cv_ref, lam_ref, subln_ref, ct_ref, st_ref, bdc_ref, bds_ref,
                   a_ref, zf_ref, *, lam_init):
    lam = _lambda_full(lam_ref, lam_init)
    sl = lambda h: slice(h * PAIR_W, (h + 1) * PAIR_W)
    heads = _diff_heads(lambda h: q_ref[:, sl(h)],
                        lambda h: [(k_ref[:, sl(h)], False), (ckt_ref[0, h].astype(bf16), True)],
                        lambda h: [v_ref[:, sl(h)], cv_ref[0, h].astype(bf16)], lam, subln_ref[...], lam_init,
                        DIFF_GROUP_LAT)
    for h in range(H_DIFF):
        a_ref[:, sl(h)] = heads[h].astype(bf16)
    zf_ref[...] = _fourier_rows(ct_ref, st_ref, z_ref[...], bdc_ref, bds_ref).astype(bf16)


def _lat_cd(proj, ck, cv, lam, subln, ct, st, bdc, bds, lam_init):
    t = DEC_SEQ
    nq = t // TQ_CD
    const = lambda b, i: (0, 0)
    return pl.pallas_call(
        functools.partial(_lat_cd_kernel, lam_init=lam_init),
        grid=(DEC_BATCH, nq),
        in_specs=[pl.BlockSpec((TQ_CD, DIFF_W), lambda b, i: (N_CTX // TQ_CD + b * nq + i, 0)),
                  pl.BlockSpec((t, DIFF_W), lambda b, i: (N_CTX // t + b, 1)),
                  pl.BlockSpec((t, DIFF_W), lambda b, i: (N_CTX // t + b, 2)),
                  pl.BlockSpec((t, FNET_W), lambda b, i: (N_CTX // t + b, 3 * DIFF_W // FNET_W)),
                  pl.BlockSpec((1, H_DIFF, PAIR_W, PAST_LEN), lambda b, i: (b, 0, 0, 0)),
                  pl.BlockSpec((1, H_DIFF, PAST_LEN, PAIR_W), lambda b, i: (b, 0, 0, 0)),
                  pl.BlockSpec((4, HEAD_DIM), const), pl.BlockSpec((1, PAIR_W), const),
                  pl.BlockSpec((TQ_CD, t), lambda b, i: (i, 0)), pl.BlockSpec((TQ_CD, t), lambda b, i: (i, 0)),
                  pl.BlockSpec((FNET_W, FNET_W), const), pl.BlockSpec((FNET_W, FNET_W), const)],
        out_specs=[pl.BlockSpec((TQ_CD, DIFF_W), lambda b, i: (b * nq + i, 0)),
                   pl.BlockSpec((TQ_CD, FNET_W), lambda b, i: (b * nq + i, 0))],
        out_shape=[jax.ShapeDtypeStruct((DEC_BATCH * t, DIFF_W), bf16),
                   jax.ShapeDtypeStruct((DEC_BATCH * t, FNET_W), bf16)],
        compiler_params=_params(2),
        name="lat_cd",
    )(proj, proj, proj, proj, ck, cv, lam, subln, ct, st, bdc, bds)


def _rope_tables():
    t = np.arange(DEC_SEQ)
    quarter = HEAD_DIM // 4
    inv = ROPE_BASE ** (-np.arange(quarter, dtype=np.float64) / quarter)
    ang = np.concatenate([(t // GRID_W)[:, None] * inv, (t % GRID_W)[:, None] * inv], -1)
    cos, sin = np.cos(ang), np.sin(ang)
    reps = LANES // HEAD_DIM
    return (np.tile(np.concatenate([cos, cos], -1), (1, reps)).astype(np.float32),
            np.tile(np.concatenate([-sin, sin], -1), (1, reps)).astype(np.float32))


def _dft_tables(n):
    k = np.arange(n)
    ang = (2.0 * math.pi / n) * ((k[:, None] * k[None, :]) % n)
    return (np.cos(ang) / math.sqrt(n)).astype(np.float32), (np.sin(ang) / math.sqrt(n)).astype(np.float32)


def _block_diag(m, reps):
    return np.kron(np.eye(reps, dtype=m.dtype), m)


def kernel(x_prompt, x_sample, state_ret, cache_win_k, cache_win_v, cache_diff_k, cache_diff_v, c, c_ctx, w_mod, b_mod, ln_g, ln_b, w_in_ab, w_out_ab, ret_log_gamma, ret_gn_g, ret_gn_b, win_sink, w_in_cd, w_out_cd, diff_lambda, diff_subln_g, w_gate, w_up, w_down):
    cond = jnp.concatenate([c_ctx[None, :], c, jnp.zeros((SUBLANES - 1 - DEC_BATCH, D_MODEL), f32)], 0)
    mod = _modulation(cond, w_mod, b_mod).reshape(DEPTH, SUBLANES, 6, D_MODEL)

    rope_tabs = _rope_tables()
    gmat = jnp.asarray(_block_diag(np.full((HEAD_DIM, HEAD_DIM), 1.0 / HEAD_DIM, np.float32),
                                   N_CHUNK // HEAD_DIM), bf16)
    c64, s64 = _dft_tables(FNET_DIM)
    bdc = _block_diag(c64, FNET_GROUPS)
    bds = _block_diag(s64, FNET_GROUPS)
    dft_ctx = _dft_tables(SEQ)
    dft_lat = _dft_tables(DEC_SEQ)

    x_parts = [x_prompt.reshape(N_CTX, D_MODEL), x_sample.reshape(N_LAT, D_MODEL)]
    outs = {}
    for l in range(DEPTH):
        i = l // 2
        if l % 2 == 0:
            lgf = jnp.repeat(ret_log_gamma[i, 0], HEAD_DIM)[None, :]
            lgb = jnp.repeat(ret_log_gamma[i, 1], HEAD_DIM)[None, :]
            gn_g = ret_gn_g[i][None, :]
            gn_b = ret_gn_b[i][None, :]
            sink = win_sink[i][None, :]
            rope_tiles = tuple(range(0, 2 * RET_W // LANES)) + tuple(
                range(4 * RET_W // LANES, (4 * RET_W + WIN_W + KV_W) // LANES))
            kv_tile = (4 * RET_W + WIN_W) // LANES
            kv_shape = (BATCH, 1, KV_WIN, HEAD_DIM, SEQ)
            scale_tiles = tuple(range(4 * RET_W // LANES, (4 * RET_W + WIN_W) // LANES))
            proj, wk_t, wv_t = _proj(x_parts, mod, l, w_in_ab, i, scale_tiles, rope_tabs, rope_tiles,
                                     (kv_shape, kv_shape),
                                     {kv_tile: ("heads", 0, 0), kv_tile + 1: ("heads", 1, 0)})
            ro_c, wo_c, st_c = _ctx_ab(proj, ret_log_gamma[i], sink, lgf, lgb, gmat, gn_g, gn_b)
            ck = cache_win_k[:, i].reshape(DEC_BATCH, PAST_LEN, KV_W)
            cv = cache_win_v[:, i].reshape(DEC_BATCH, PAST_LEN, KV_W)
            ro_l, wo_l = _lat_ab(proj, ret_log_gamma[i], sink, ck, cv, state_ret, i, lgf, lgb, gmat, gn_g, gn_b)
            mix_a, mix_b, w_out = (ro_c, ro_l), (wo_c, wo_l), w_out_ab
            outs.setdefault('state', []).append(st_c[:, None])
            outs.setdefault('win_k', []).append(jnp.transpose(wk_t, (0, 1, 4, 2, 3)))
            outs.setdefault('win_v', []).append(jnp.transpose(wv_t, (0, 1, 4, 2, 3)))
        else:
            lam_init = 0.8 - 0.6 * math.exp(-0.3 * l)
            subln = diff_subln_g[i][None, :]
            rope_tiles = tuple(range(0, 2 * DIFF_W // LANES))
            plan = {}
            for h in range(H_DIFF):
                plan[DIFF_W // LANES + h] = ("pairs", 0, h)
                plan[2 * DIFF_W // LANES + h] = ("plain", 1, h)
            scale_tiles = tuple(range(0, DIFF_W // LANES))
            proj, dk_t, dv_h = _proj(
                x_parts, mod, l, w_in_cd, i, scale_tiles, rope_tabs, rope_tiles,
                ((BATCH, 1, H_DIFF, 2, HEAD_DIM, SEQ), (BATCH, 1, H_DIFF, SEQ, 2 * HEAD_DIM)), plan)
            a_c, z_c = _ctx_cd(proj, diff_lambda[i], subln, dft_ctx[0], dft_ctx[1], bdc, bds, lam_init)
            ck = jnp.transpose(cache_diff_k[:, i], (0, 2, 3, 4, 1)).reshape(DEC_BATCH, H_DIFF, PAIR_W, PAST_LEN)
            cv = jnp.transpose(cache_diff_v[:, i], (0, 2, 1, 3))
            a_l, z_l = _lat_cd(proj, ck, cv, diff_lambda[i], subln, dft_lat[0], dft_lat[1], bdc, bds, lam_init)
            mix_a, mix_b, w_out = (a_c, a_l), (z_c, z_l), w_out_cd
            outs.setdefault('diff_k', []).append(jnp.transpose(dk_t, (0, 1, 5, 2, 3, 4)))
            outs.setdefault('diff_v', []).append(jnp.transpose(dv_h, (0, 1, 3, 2, 4)))
        x_parts = _post(x_parts, mix_a, mix_b, mod, ln_g, ln_b, w_out, w_gate, w_up, w_down, l, i,
                        split_out=(l == DEPTH - 1))

    y_prompt = x_parts[0].reshape(BATCH, SEQ, D_MODEL)
    y_sample = x_parts[1].reshape(DEC_BATCH, DEC_SEQ, D_MODEL)
    cat = lambda parts: parts[0] if len(parts) == 1 else jnp.concatenate(parts, 1)
    return (y_prompt, y_sample, cat(outs['state']), cat(outs['win_k']), cat(outs['win_v']),
            cat(outs['diff_k']), cat(outs['diff_v']))
```

```python
import functools
import math

import jax
import jax.numpy as jnp
import numpy as np
from jax import lax
from jax.experimental import pallas as pl
from jax.experimental.pallas import tpu as pltpu

D_MODEL = 1024
BATCH = 32
SEQ = 256
DEPTH = 2
DEC_BATCH = 2
DEC_SEQ = 1024
PAST_LEN = 512
GRID_W = 64
HEAD_DIM = 64
ROPE_BASE = 10000.0
H_RET = 8
H_WIN = 8
KV_WIN = 2
G_WIN = H_WIN // KV_WIN
WINDOW = 128
H_DIFF = 6
FNET_GROUPS = 4
FNET_DIM = 64
D_FF = 256 * math.ceil(8 * D_MODEL / 3 / 256)
RET_W = H_RET * HEAD_DIM
WIN_W = H_WIN * HEAD_DIM
KV_W = KV_WIN * HEAD_DIM
AB_IN = 4 * RET_W + WIN_W + 2 * KV_W
DIFF_W = H_DIFF * 2 * HEAD_DIM
FNET_W = FNET_GROUPS * FNET_DIM
CD_IN = 3 * DIFF_W + FNET_W
ALPHA = (2 * DEPTH) ** 0.25
LN_EPS = 1e-5
QK_SCALE = HEAD_DIM ** -0.5
LOG2_E = math.log2(math.e)

N_CTX = BATCH * SEQ
N_LAT = DEC_BATCH * DEC_SEQ
N_TOK = N_CTX + N_LAT

LANES = 128
SUBLANES = 8
PAIR_W = 2 * HEAD_DIM
TM = 512
TM_PROJ = 1024
CTX_BLOCKS = N_CTX // TM
TOK_BLOCKS = N_TOK // TM
ROW_GROUPS = 2
FFN_SKEW = 2
W_SLOTS = 3
TQ = 256
LAT_SUB = 4
TQ_CD = 512
CTX_SEQS = 4
CTX_SEQS_CD = 4
RET_GROUP = 4
WIN_GROUP = 4
WIN_GROUP_LAT = 4
DIFF_GROUP_LAT = 1
DIFF_GROUP = 3
N_CHUNK = 256
MOD_TN = 1536
NEG_BIG = -1e30
VMEM_LIMIT = 60 * 1024 * 1024

f32 = jnp.float32
bf16 = jnp.bfloat16


def _params(n_axes):
    return pltpu.CompilerParams(dimension_semantics=("arbitrary",) * n_axes,
                                vmem_limit_bytes=VMEM_LIMIT)


def _dot(a, b):
    return jnp.dot(a, b, preferred_element_type=f32)


def _dot_nt(a, b):
    return lax.dot_general(a, b, (((1,), (1,)), ((), ())), preferred_element_type=f32)


def _ln(x):
    mu = jnp.mean(x, -1, keepdims=True)
    d = x - mu
    var = jnp.mean(d * d, -1, keepdims=True)
    return d * lax.rsqrt(var + LN_EPS)


def _silu(x):
    return x * jax.nn.sigmoid(x)


def _split_bf16(x):
    hi = x.astype(bf16)
    lo = (x - hi.astype(f32)).astype(bf16)
    return hi, lo


def _lane_half_mask(shape):
    return (lax.broadcasted_iota(jnp.int32, shape, len(shape) - 1) & HEAD_DIM) == 0


def _mod_kernel(c_ref, w_ref, b_ref, o_ref):
    layer = pl.program_id(0)
    a = _silu(c_ref[...])
    rows = a.shape[0]
    a_hi, a_lo = _split_bf16(a)
    w_hi, w_lo = _split_bf16(w_ref[0])
    both = _dot(jnp.concatenate([a_hi, a_lo], 0), w_hi)
    o_ref[0] = both[:rows] + both[rows:] + _dot(a_hi, w_lo) + b_ref[pl.ds(layer, 1), :]


def _modulation(cond, w_mod, b_mod):
    tn = MOD_TN
    rows = cond.shape[0]
    return pl.pallas_call(
        _mod_kernel,
        grid=(DEPTH, 6 * D_MODEL // tn),
        in_specs=[pl.BlockSpec((rows, D_MODEL), lambda l, j: (0, 0)),
                  pl.BlockSpec((1, D_MODEL, tn), lambda l, j: (l, 0, j)),
                  pl.BlockSpec((DEPTH, tn), lambda l, j: (0, j))],
        out_specs=pl.BlockSpec((1, rows, tn), lambda l, j: (l, 0, j)),
        out_shape=jax.ShapeDtypeStruct((DEPTH, rows, 6 * D_MODEL), f32),
        compiler_params=_params(2),
        name="modulation",
    )(cond, w_mod, b_mod)


def _tok(i, n_w):
    return jnp.maximum(i - n_w, 0)


def _ctx_blk(t, tm=TM):
    return jnp.minimum(t, N_CTX // tm - 1)


def _lat_blk(t, tm=TM):
    return jnp.maximum(t - N_CTX // tm, 0)


def _mod_row(t, tm=TM):
    return jnp.where(t < N_CTX // tm, 0, 1 + _lat_blk(t, tm) * tm // DEC_SEQ)


def _token_specs(parts, n_w, tm=TM):
    width = parts[0].shape[1]
    if len(parts) == 1:
        return [pl.BlockSpec((tm, width), lambda i: (_tok(i, n_w), 0))]
    return [pl.BlockSpec((tm, width), lambda i: (_ctx_blk(_tok(i, n_w), tm), 0)),
            pl.BlockSpec((tm, width), lambda i: (_lat_blk(_tok(i, n_w), tm), 0))]


def _pick(refs, is_ctx, rs):
    return refs[0 if (len(refs) == 1 or is_ctx) else 1][rs, :]


def _rope_pair(y, cos, sin_signed):
    first_half = (lax.broadcasted_iota(jnp.int32, y.shape, 1) & (HEAD_DIM // 2)) == 0
    swapped = jnp.where(first_half, pltpu.roll(y, LANES - HEAD_DIM // 2, 1), pltpu.roll(y, HEAD_DIM // 2, 1))
    return y * cos + swapped * sin_signed


def _proj_kernel(*refs, n_x, n_cache, n_w, rope_tiles, scale_tiles, cache_plan):
    x_refs = refs[:n_x]
    mod_ref, w_ref, cos_ref, sin_ref, o_ref = refs[n_x:n_x + 5]
    cache_refs = refs[n_x + 5:n_x + 5 + n_cache]
    wbf_ref, u_ref = refs[n_x + 5 + n_cache:]
    i = pl.program_id(0)

    @pl.when(i == 0)
    def _():
        for c in range(n_w):
            wbf_ref[c] = w_ref[0, :, c * N_CHUNK:(c + 1) * N_CHUNK].astype(bf16)

    def tokens(is_ctx):
        x_ref = x_refs[0] if is_ctx else x_refs[-1]
        shift = mod_ref[0, 0:1, :]
        scale = mod_ref[0, 1:2, :]
        groups = [slice(b * SEQ, (b + 1) * SEQ) for b in range(TM_PROJ // SEQ)]
        for rs in groups:
            u_ref[rs, :] = (_ln(x_ref[rs, :]) * (1.0 + scale) + shift).astype(bf16)
        for c in range(n_w):
            y_all = _dot(u_ref[...], wbf_ref[c])
            for b, rs in enumerate(groups):
                y = y_all[rs, :]
                for t in range(N_CHUNK // LANES):
                    tile = c * (N_CHUNK // LANES) + t
                    piece = y[:, t * LANES:(t + 1) * LANES]
                    if tile in rope_tiles and not is_ctx:
                        piece = _rope_pair(piece, cos_ref[rs, :], sin_ref[rs, :])
                    if tile in scale_tiles:
                        piece = piece * (QK_SCALE * LOG2_E)
                    o_ref[rs, tile * LANES:(tile + 1) * LANES] = piece.astype(o_ref.dtype)
                    if tile in cache_plan and is_ctx:
                        kind, out_idx, slot = cache_plan[tile]
                        c_ref = cache_refs[out_idx]
                        if kind == "plain":
                            c_ref[b, 0, slot] = piece
                        else:
                            piece_t = piece.T
                            if kind == "heads":
                                c_ref[b, 0, 0] = piece_t[0:HEAD_DIM]
                                c_ref[b, 0, 1] = piece_t[HEAD_DIM:]
                            else:
                                c_ref[b, 0, slot, 0] = piece_t[0:HEAD_DIM]
                                c_ref[b, 0, slot, 1] = piece_t[HEAD_DIM:]

    t = i - 1

    @pl.when(jnp.logical_and(t >= 0, t < N_CTX // TM_PROJ))
    def _():
        tokens(True)

    @pl.when(t >= N_CTX // TM_PROJ)
    def _():
        tokens(False)


def _proj(x_parts, mod, mod_layer, w_all, layer, scale_tiles, rope_tabs, rope_tiles, cache_shapes, cache_plan):
    n_out = w_all.shape[2]
    n_w = n_out // N_CHUNK
    tm = TM_PROJ
    nb = DEC_SEQ // tm
    tok = lambda i: _tok(i, 1)
    in_specs = _token_specs(x_parts, 1, tm) + [
        pl.BlockSpec((None, 1, 6, D_MODEL), lambda i: (mod_layer, _mod_row(tok(i), tm), 0, 0)),
        pl.BlockSpec((1, D_MODEL, n_out), lambda i: (layer, 0, 0), pipeline_mode=pl.Buffered(1)),
        pl.BlockSpec((tm, LANES), lambda i: (_lat_blk(tok(i), tm) % nb, 0)),
        pl.BlockSpec((tm, LANES), lambda i: (_lat_blk(tok(i), tm) % nb, 0))]
    out_specs = [pl.BlockSpec((tm, n_out), lambda i: (tok(i), 0))]
    out_shape = [jax.ShapeDtypeStruct((N_TOK, n_out), bf16)]
    for shp in cache_shapes:
        blk = (tm // SEQ,) + tuple(shp[1:])
        out_specs.append(pl.BlockSpec(blk, lambda i, nd=len(shp): (_ctx_blk(tok(i), tm),) + (0,) * (nd - 1)))
        out_shape.append(jax.ShapeDtypeStruct(tuple(shp), f32))
    return pl.pallas_call(
        functools.partial(_proj_kernel, n_x=len(x_parts), n_cache=len(cache_shapes), n_w=n_w,
                          rope_tiles=frozenset(rope_tiles), scale_tiles=frozenset(scale_tiles),
                          cache_plan=dict(cache_plan)),
        grid=(1 + N_TOK // tm,),
        in_specs=in_specs,
        out_specs=out_specs,
        out_shape=out_shape,
        scratch_shapes=[pltpu.VMEM((n_w, D_MODEL, N_CHUNK), bf16), pltpu.VMEM((tm, D_MODEL), bf16)],
        compiler_params=_params(1),
        name="proj",
    )(*x_parts, mod, w_all, *rope_tabs)


def _post_kernel(*refs, n_x, n_y, ka, kb, n_w, layer):
    x_refs = refs[:n_x]
    (ac_ref, al_ref, bc_ref, bl_ref, mod_ref, lng_ref, lnb_ref,
     wo_ref, wg_ref, wu_ref, wd_ref) = refs[n_x:n_x + 11]
    y_refs = refs[n_x + 11:n_x + 11 + n_y]
    (wo_s, wg_s, wu_s, wd_s, x1_ref, u_ref, h_ref, y_ref,
     wg_f, wu_f, wd_f, w_sem) = refs[n_x + 11 + n_y:]
    i = pl.program_id(0)

    def chunk_copies(c, slot):
        cols = pl.ds(pl.multiple_of(c * N_CHUNK, N_CHUNK), N_CHUNK)
        return (pltpu.make_async_copy(wg_ref.at[layer, :, cols], wg_f.at[slot], w_sem.at[0, slot]),
                pltpu.make_async_copy(wu_ref.at[layer, :, cols], wu_f.at[slot], w_sem.at[1, slot]),
                pltpu.make_async_copy(wd_ref.at[layer, cols, :], wd_f.at[slot], w_sem.at[2, slot]))

    lead = n_w - 1
    gate1 = mod_ref[0, 2:3, :]
    shift2 = mod_ref[0, 3:4, :]
    scale2 = mod_ref[0, 4:5, :]
    gate2 = mod_ref[0, 5:6, :]
    groups = [slice(r * TM // ROW_GROUPS, (r + 1) * TM // ROW_GROUPS) for r in range(ROW_GROUPS)]

    def mix_in(rs, is_ctx):
        a = _pick((ac_ref, al_ref), is_ctx, rs)
        b = _pick((bc_ref, bl_ref), is_ctx, rs)
        pieces = ([a[:, c:c + N_CHUNK] for c in range(0, ka, N_CHUNK)]
                  + [b[:, c:c + N_CHUNK] for c in range(0, kb, N_CHUNK)])
        h = functools.reduce(lambda s, p: s + p, [_dot(p, wo_s[c]) for c, p in enumerate(pieces)])
        x1 = _ln(ALPHA * _pick(x_refs, is_ctx, rs) + gate1 * h) * lng_ref[0, 0:1, :] + lnb_ref[0, 0:1, :]
        x1_ref[rs, :] = x1
        u_ref[rs, :] = (_ln(x1) * (1.0 + scale2) + shift2).astype(bf16)

    def finish(rs, ffn, is_ctx):
        y = _ln(ALPHA * x1_ref[rs, :] + gate2 * ffn) * lng_ref[0, 1:2, :] + lnb_ref[0, 1:2, :]
        y_refs[0 if (n_y == 1 or is_ctx) else 1][rs, :] = y

    @pl.when(i < n_w)
    def _():
        slot = i % W_SLOTS

        @pl.when(i == 0)
        def _():
            for c in range(W_SLOTS):
                for cp in chunk_copies(c, c):
                    cp.start()
            for c in range((ka + kb) // N_CHUNK):
                wo_s[c] = wo_ref[0, c * N_CHUNK:(c + 1) * N_CHUNK, :].astype(bf16)
            for rs in groups:
                mix_in(rs, True)
                y_ref[rs, :] = jnp.zeros((TM // ROW_GROUPS, D_MODEL), f32)

        for cp in chunk_copies(i, slot):
            cp.wait()
        wg_s[i] = wg_f[slot].astype(bf16)
        wu_s[i] = wu_f[slot].astype(bf16)
        wd_s[i] = wd_f[slot].astype(bf16)

        @pl.when(i + W_SLOTS < n_w)
        def _():
            for cp in chunk_copies(i + W_SLOTS, slot):
                cp.start()

        g = _dot(u_ref[...], wg_s[i])
        up = _dot(u_ref[...], wu_s[i])
        y_ref[...] += _dot((_silu(g) * up).astype(bf16), wd_s[i])

        @pl.when(i == lead)
        def _():
            for rs in groups:
                finish(rs, y_ref[rs, :], True)

    def token_block(is_ctx):
        for rs in groups:
            mix_in(rs, is_ctx)

        def ffn_chunk(rs, c):
            g = _dot(u_ref[rs, :], wg_s[c])
            up = _dot(u_ref[rs, :], wu_s[c])
            h_ref[rs, c * N_CHUNK:(c + 1) * N_CHUNK] = (_silu(g) * up).astype(bf16)

        def ffn_down(rs):
            finish(rs, functools.reduce(
                lambda s, p: s + p,
                [_dot(h_ref[rs, c * N_CHUNK:(c + 1) * N_CHUNK], wd_s[c]) for c in range(n_w)]), is_ctx)

        for c in range(n_w + FFN_SKEW * (ROW_GROUPS - 1)):
            for r, rs in enumerate(groups):
                cc = c - FFN_SKEW * r
                if 0 <= cc < n_w:
                    ffn_chunk(rs, cc)
                if cc == n_w - 1:
                    ffn_down(rs)

    blk = i - lead

    @pl.when(jnp.logical_and(i >= n_w, blk < CTX_BLOCKS))
    def _():
        token_block(True)

    @pl.when(blk >= CTX_BLOCKS)
    def _():
        token_block(False)


def _post(x_parts, mix_a, mix_b, mod, ln_g, ln_b, w_out, w_gate, w_up, w_down, layer, mix_layer, split_out):
    ka, kb = mix_a[0].shape[1], mix_b[0].shape[1]
    n_w = D_FF // N_CHUNK
    lead = n_w - 1
    tok = lambda i: _tok(i, lead)
    lay = lambda i: (layer, 0, 0)
    in_specs = (_token_specs(x_parts, lead) + _token_specs(mix_a, lead) + _token_specs(mix_b, lead) + [
        pl.BlockSpec((None, 1, 6, D_MODEL), lambda i: (layer, _mod_row(tok(i)), 0, 0)),
        pl.BlockSpec((1, 2, D_MODEL), lay),
        pl.BlockSpec((1, 2, D_MODEL), lay),
        pl.BlockSpec((1, ka + kb, D_MODEL), lambda i: (mix_layer, 0, 0), pipeline_mode=pl.Buffered(1)),
        pl.BlockSpec(memory_space=pl.ANY), pl.BlockSpec(memory_space=pl.ANY), pl.BlockSpec(memory_space=pl.ANY)])
    if split_out:
        out_specs = [pl.BlockSpec((TM, D_MODEL), lambda i: (_ctx_blk(tok(i)), 0)),
                     pl.BlockSpec((TM, D_MODEL), lambda i: (_lat_blk(tok(i)), 0))]
        out_shape = [jax.ShapeDtypeStruct((N_CTX, D_MODEL), f32), jax.ShapeDtypeStruct((N_LAT, D_MODEL), f32)]
    else:
        out_specs = [pl.BlockSpec((TM, D_MODEL), lambda i: (tok(i), 0))]
        out_shape = [jax.ShapeDtypeStruct((N_TOK, D_MODEL), f32)]
    return pl.pallas_call(
        functools.partial(_post_kernel, n_x=len(x_parts), n_y=len(out_shape), ka=ka, kb=kb, n_w=n_w, layer=layer),
        grid=(lead + TOK_BLOCKS,),
        in_specs=in_specs,
        out_specs=out_specs,
        out_shape=out_shape,
        scratch_shapes=[pltpu.VMEM(((ka + kb) // N_CHUNK, N_CHUNK, D_MODEL), bf16),
                        pltpu.VMEM((n_w, D_MODEL, N_CHUNK), bf16),
                        pltpu.VMEM((n_w, D_MODEL, N_CHUNK), bf16), pltpu.VMEM((n_w, N_CHUNK, D_MODEL), bf16),
                        pltpu.VMEM((TM, D_MODEL), f32), pltpu.VMEM((TM, D_MODEL), bf16),
                        pltpu.VMEM((TM, D_FF), bf16), pltpu.VMEM((TM, D_MODEL), f32),
                        pltpu.VMEM((W_SLOTS, D_MODEL, N_CHUNK), f32), pltpu.VMEM((W_SLOTS, D_MODEL, N_CHUNK), f32),
                        pltpu.VMEM((W_SLOTS, N_CHUNK, D_MODEL), f32), pltpu.SemaphoreType.DMA((3, W_SLOTS))],
        compiler_params=_params(1),
        name="post",
    )(*x_parts, *mix_a, *mix_b, mod, ln_g, ln_b, w_out, w_gate, w_up, w_down)


def _group_norm_gate(ro, rg, gmat, gn_g, gn_b):
    def gmean(parts):
        cols = []
        for c in range(0, RET_W, N_CHUNK):
            cols.append(sum(_dot(p[:, c:c + N_CHUNK], gmat) for p in parts))
        return jnp.concatenate(cols, -1)

    d = ro - gmean(_split_bf16(ro))
    var = gmean([(d * d).astype(bf16)])
    y = d * lax.rsqrt(var + LN_EPS) * gn_g + gn_b
    return _silu(rg.astype(f32)) * y


def _dup_head(x, j):
    first = _lane_half_mask(x.shape)
    keep = first if j == 0 else jnp.logical_not(first)
    xm = jnp.where(keep, x.astype(f32), 0.0)
    return xm + pltpu.roll(xm, HEAD_DIM, 1)


def _softmax_parts(scores, sink):
    m = sink
    for s in scores:
        m = jnp.maximum(m, jnp.max(s, -1, keepdims=True))
    es = [jnp.exp2(s - m) for s in scores]
    denom = jnp.exp2(sink - m)
    for e in es:
        denom = denom + jnp.sum(e, -1, keepdims=True)
    return es, denom


def _retention_tables(lg_ref, lgf_ref, lgb_ref, dmask_ref, kdec_ref, n):
    row = lax.broadcasted_iota(jnp.int32, (n, n), 0)
    col = lax.broadcasted_iota(jnp.int32, (n, n), 1)
    diff = (row - col).astype(f32)
    diag = jnp.where(row == col, 2.0 * QK_SCALE, QK_SCALE)
    for h in range(H_RET):
        dmask_ref[h] = jnp.exp(jnp.where(diff >= 0, lg_ref[0, h] * diff, -lg_ref[1, h] * diff)) * diag
    t = lax.broadcasted_iota(jnp.int32, (n, RET_W), 0).astype(f32)
    kdec_ref[0] = jnp.exp(lgf_ref[...] * (n - 1.0 - t)) * QK_SCALE
    kdec_ref[1] = jnp.exp(lgb_ref[...] * t) * QK_SCALE


def _retention_intra(pairs, q_of, k_of, v_of, dmask_ref):
    first = _lane_half_mask(k_of(pairs[0]).shape)
    masked = {}
    for p in pairs:
        kb = k_of(p)
        for e in range(2):
            keep = first if e == 0 else jnp.logical_not(first)
            s = _dot_nt(q_of(p), jnp.where(keep, kb, jnp.zeros_like(kb))) * dmask_ref[2 * p + e]
            masked[p, e] = s.astype(bf16)
    outs = {}
    for p in pairs:
        pv = [_dot(masked[p, e], v_of(p)) for e in range(2)]
        outs[p] = jnp.where(_lane_half_mask(pv[0].shape), pv[0], pv[1])
    return outs


def _window_group(subs, q_of, k_parts_of, v_parts_of, masks, sink_of):
    scores = {}
    for key in subs:
        parts = [_dot_nt(q_of(key), k) for k in k_parts_of(key)]
        scores[key] = [sc if mk is None else jnp.where(mk, sc, NEG_BIG) for sc, mk in zip(parts, masks)]
    probs = {}
    for key in subs:
        es, denom = _softmax_parts(scores[key], sink_of(key))
        probs[key] = ([ex.astype(bf16) for ex in es], denom)
    outs = {}
    for key in subs:
        es, denom = probs[key]
        pv = functools.reduce(lambda x, y: x + y, [_dot(ex, v) for ex, v in zip(es, v_parts_of(key))])
        outs[key] = pv / denom
    return outs


def _ctx_ab_kernel(lg_ref, sink_ref, rq_ref, rk_ref, rv_ref, rg_ref, wq_ref, wk_ref, wv_ref,
                   lgf_ref, lgb_ref, gmat_ref, gng_ref, gnb_ref,
                   ro_ref, wo_ref, st_ref, dmask_ref, kdec_ref, ret_ref):
    t_len = SEQ

    @pl.when(pl.program_id(0) == 0)
    def _():
        _retention_tables(lg_ref, lgf_ref, lgb_ref, dmask_ref, kdec_ref, t_len)

    first = _lane_half_mask((t_len, PAIR_W))
    for sq in range(CTX_SEQS):
        rows = slice(sq * t_len, (sq + 1) * t_len)
        psl = lambda p: slice(p * PAIR_W, (p + 1) * PAIR_W)
        for p0 in range(0, H_RET // 2, RET_GROUP):
            pairs = list(range(p0, p0 + RET_GROUP))
            intra = _retention_intra(pairs, lambda p: rq_ref[rows, psl(p)], lambda p: rk_ref[rows, psl(p)],
                                     lambda p: rv_ref[rows, psl(p)], dmask_ref)
            for p in pairs:
                ret_ref[rows, psl(p)] = intra[p]
        for p in range(H_RET // 2):
            sl = psl(p)
            kb = rk_ref[rows, sl]
            v = rv_ref[rows, sl]
            for d in range(2):
                kd_t = (kb * kdec_ref[d, :, sl]).T.astype(bf16)
                st = _dot(kd_t, v)
                st_ref[sq, d, 2 * p] = st[0:HEAD_DIM, 0:HEAD_DIM]
                st_ref[sq, d, 2 * p + 1] = pltpu.roll(st[HEAD_DIM:, :], HEAD_DIM, 1)[:, 0:HEAD_DIM]
        ro_ref[rows, :] = _group_norm_gate(ret_ref[rows, :], rg_ref[rows, :], gmat_ref[...], gng_ref[...],
                                           gnb_ref[...]).astype(bf16)

        k_dup = [_dup_head(wk_ref[rows, :], j).astype(bf16) for j in range(KV_WIN)]
        v_dup = [_dup_head(wv_ref[rows, :], j).astype(bf16) for j in range(KV_WIN)]

        def q_masked(key):
            qp, e = key
            qb = wq_ref[rows, qp * PAIR_W:(qp + 1) * PAIR_W]
            return jnp.where(first if e == 0 else jnp.logical_not(first), qb, jnp.zeros_like(qb))

        kv_of = lambda key: key[0] * 2 // G_WIN
        for g0 in range(0, H_WIN // 2, WIN_GROUP):
            subs = [(qp, e) for qp in range(g0, g0 + WIN_GROUP) for e in range(2)]
            outs = _window_group(subs, q_masked, lambda key: [k_dup[kv_of(key)]], lambda key: [v_dup[kv_of(key)]],
                                 [None], lambda key: sink_ref[0, 2 * key[0] + key[1]] * LOG2_E)
            for qp in range(g0, g0 + WIN_GROUP):
                wo_ref[rows, qp * PAIR_W:(qp + 1) * PAIR_W] = jnp.where(first, outs[qp, 0], outs[qp, 1]).astype(bf16)


def _ctx_ab(proj, log_gamma, sink, lgf_lanes, lgb_lanes, gmat, gn_g, gn_b):
    t = SEQ
    tb = CTX_SEQS * t
    smem = pl.BlockSpec(memory_space=pltpu.SMEM)
    const = lambda b: (0, 0)
    col = lambda c: (lambda b: (b, c))
    return pl.pallas_call(
        _ctx_ab_kernel,
        grid=(BATCH // CTX_SEQS,),
        in_specs=[smem, smem,
                  pl.BlockSpec((tb, RET_W), col(0)), pl.BlockSpec((tb, RET_W), col(1)),
                  pl.BlockSpec((tb, RET_W), col(2)), pl.BlockSpec((tb, RET_W), col(3)),
                  pl.BlockSpec((tb, WIN_W), col(4)),
                  pl.BlockSpec((tb, KV_W), col((4 * RET_W + WIN_W) // KV_W)),
                  pl.BlockSpec((tb, KV_W), col((4 * RET_W + WIN_W) // KV_W + 1)),
                  pl.BlockSpec((1, RET_W), const), pl.BlockSpec((1, RET_W), const),
                  pl.BlockSpec((N_CHUNK, N_CHUNK), const),
                  pl.BlockSpec((1, RET_W), const), pl.BlockSpec((1, RET_W), const)],
        out_specs=[pl.BlockSpec((tb, RET_W), lambda b: (b, 0)),
                   pl.BlockSpec((tb, WIN_W), lambda b: (b, 0)),
                   pl.BlockSpec((CTX_SEQS, 2, H_RET, HEAD_DIM, HEAD_DIM), lambda b: (b, 0, 0, 0, 0))],
        out_shape=[jax.ShapeDtypeStruct((BATCH * t, RET_W), bf16),
                   jax.ShapeDtypeStruct((BATCH * t, WIN_W), bf16),
                   jax.ShapeDtypeStruct((BATCH, 2, H_RET, HEAD_DIM, HEAD_DIM), f32)],
        scratch_shapes=[pltpu.VMEM((H_RET, t, t), f32), pltpu.VMEM((2, t, RET_W), f32),
                        pltpu.VMEM((tb, RET_W), f32)],
        compiler_params=_params(1),
        name="ctx_ab",
    )(log_gamma, sink, proj, proj, proj, proj, proj, proj, proj, lgf_lanes, lgb_lanes, gmat, gn_g, gn_b)


def _pair_state(s0_ref, d, p):
    zero = jnp.zeros((HEAD_DIM, HEAD_DIM), f32)
    top = jnp.concatenate([s0_ref[0, 0, d, 2 * p], zero], 1)
    bottom = jnp.concatenate([zero, s0_ref[0, 0, d, 2 * p + 1]], 1)
    return jnp.concatenate([top, bottom], 0)


def _lat_ab_kernel(lg_ref, sink_ref, rq_ref, rk_ref, rv_ref, rg_ref, wq_ref, wk_ref, wv_ref, ck_ref, cv_ref,
                   s0_ref, lgf_ref, lgb_ref, gmat_ref, gng_ref, gnb_ref,
                   ro_ref, wo_ref, ret_ref, dmask_ref, kdec_ref, qdec_ref, sf_ref, sb_ref):
    t_len = DEC_SEQ
    n_chunks = t_len // TQ
    step = pl.program_id(1)
    first = _lane_half_mask((TQ, PAIR_W))

    @pl.when(jnp.logical_and(pl.program_id(0) == 0, step == 0))
    def _():
        _retention_tables(lg_ref, lgf_ref, lgb_ref, dmask_ref, kdec_ref, TQ)
        t = lax.broadcasted_iota(jnp.int32, (TQ, RET_W), 0).astype(f32)
        qdec_ref[0] = jnp.exp(lgf_ref[...] * (t + 1.0))
        qdec_ref[1] = jnp.exp(lgb_ref[...] * (TQ - t))

    @pl.when(step == 0)
    def _():
        r = lax.broadcasted_iota(jnp.int32, (PAIR_W, PAIR_W), 0)
        c_ = lax.broadcasted_iota(jnp.int32, (PAIR_W, PAIR_W), 1)
        same_head = (r < HEAD_DIM) == (c_ < HEAD_DIM)
        for p in range(H_RET // 2):
            sl = slice(p * PAIR_W, (p + 1) * PAIR_W)
            kv = []
            for c in range(n_chunks):
                rows = slice(c * TQ, (c + 1) * TQ)
                kc = rk_ref[rows, sl]
                vc = rv_ref[rows, sl]
                kv.append([jnp.where(same_head, _dot((kc * kdec_ref[d, :, sl]).T.astype(bf16), vc), 0.0)
                           for d in range(2)])
            state = _pair_state(s0_ref, 0, p)
            for c in range(n_chunks):
                sf_ref[c, p] = state
                state = state * jnp.exp(lgf_ref[:, sl] * TQ) + kv[c][0]
            state = _pair_state(s0_ref, 1, p)
            for c in reversed(range(n_chunks)):
                sb_ref[c, p] = state
                state = state * jnp.exp(lgb_ref[:, sl] * TQ) + kv[c][1]

    for sub in range(LAT_SUB):
        chunk = step * LAT_SUB + sub
        q0 = pl.multiple_of(chunk * TQ, TQ)
        rows = slice(sub * TQ, (sub + 1) * TQ)
        psl = lambda p: slice(p * PAIR_W, (p + 1) * PAIR_W)
        intra = {}
        for p0 in range(0, H_RET // 2, RET_GROUP):
            intra.update(_retention_intra(list(range(p0, p0 + RET_GROUP)), lambda p: rq_ref[rows, psl(p)],
                                          lambda p: rk_ref[pl.ds(q0, TQ), psl(p)],
                                          lambda p: rv_ref[pl.ds(q0, TQ), psl(p)], dmask_ref))
        for p in range(H_RET // 2):
            sl = psl(p)
            q = rq_ref[rows, sl]
            o = intra[p]
            o = o + _dot(q, sf_ref[chunk, p].astype(bf16)) * qdec_ref[0, :, sl]
            o = o + _dot(q, sb_ref[chunk, p].astype(bf16)) * qdec_ref[1, :, sl]
            ret_ref[rows, sl] = o
        ro_ref[rows, :] = _group_norm_gate(ret_ref[rows, :], rg_ref[rows, :], gmat_ref[...], gng_ref[...],
                                           gnb_ref[...]).astype(bf16)

        band = TQ + 2 * WINDOW
        k_start = pl.multiple_of(jnp.clip(q0 - WINDOW, 0, t_len - band), LANES)
        qi = q0 + lax.broadcasted_iota(jnp.int32, (TQ, band), 0)
        kj = k_start + lax.broadcasted_iota(jnp.int32, (TQ, band), 1)
        in_band = jnp.abs(qi - kj) <= WINDOW
        k_parts = [[_dup_head(wk_ref[pl.ds(k_start, band), :], j).astype(bf16), _dup_head(ck_ref[0], j).astype(bf16)]
                   for j in range(KV_WIN)]
        v_parts = [[_dup_head(wv_ref[pl.ds(k_start, band), :], j).astype(bf16), _dup_head(cv_ref[0], j).astype(bf16)]
                   for j in range(KV_WIN)]

        def q_masked(key):
            qp, e = key
            qb = wq_ref[rows, qp * PAIR_W:(qp + 1) * PAIR_W]
            return jnp.where(first if e == 0 else jnp.logical_not(first), qb, jnp.zeros_like(qb))

        kv_of = lambda key: key[0] * 2 // G_WIN
        for g0 in range(0, H_WIN // 2, WIN_GROUP_LAT):
            subs = [(qp, e) for qp in range(g0, g0 + WIN_GROUP_LAT) for e in range(2)]
            outs = _window_group(subs, q_masked, lambda key: k_parts[kv_of(key)], lambda key: v_parts[kv_of(key)],
                                 [in_band, None], lambda key: sink_ref[0, 2 * key[0] + key[1]] * LOG2_E)
            for qp in range(g0, g0 + WIN_GROUP_LAT):
                wo_ref[rows, qp * PAIR_W:(qp + 1) * PAIR_W] = jnp.where(first, outs[qp, 0], outs[qp, 1]).astype(bf16)


def _lat_ab(proj, log_gamma, sink, ck, cv, state, layer, lgf_lanes, lgb_lanes, gmat, gn_g, gn_b):
    t = DEC_SEQ
    tb = LAT_SUB * TQ
    nq = t // tb
    smem = pl.BlockSpec(memory_space=pltpu.SMEM)
    const = lambda b, i: (0, 0)
    qcol = lambda c: (lambda b, i: (N_CTX // tb + b * nq + i, c))
    bcol = lambda c: (lambda b, i: (N_CTX // t + b, c))
    kv_col = (4 * RET_W + WIN_W) // KV_W
    return pl.pallas_call(
        _lat_ab_kernel,
        grid=(DEC_BATCH, nq),
        in_specs=[smem, smem,
                  pl.BlockSpec((tb, RET_W), qcol(0)), pl.BlockSpec((t, RET_W), bcol(1)),
                  pl.BlockSpec((t, RET_W), bcol(2)), pl.BlockSpec((tb, RET_W), qcol(3)),
                  pl.BlockSpec((tb, WIN_W), qcol(4)),
                  pl.BlockSpec((t, KV_W), bcol(kv_col)), pl.BlockSpec((t, KV_W), bcol(kv_col + 1)),
                  pl.BlockSpec((1, PAST_LEN, KV_W), lambda b, i: (b, 0, 0)),
                  pl.BlockSpec((1, PAST_LEN, KV_W), lambda b, i: (b, 0, 0)),
                  pl.BlockSpec((1, 1, 2, H_RET, HEAD_DIM, HEAD_DIM), lambda b, i: (b, layer, 0, 0, 0, 0)),
                  pl.BlockSpec((1, RET_W), const), pl.BlockSpec((1, RET_W), const),
                  pl.BlockSpec((N_CHUNK, N_CHUNK), const),
                  pl.BlockSpec((1, RET_W), const), pl.BlockSpec((1, RET_W), const)],
        out_specs=[pl.BlockSpec((tb, RET_W), lambda b, i: (b * nq + i, 0)),
                   pl.BlockSpec((tb, WIN_W), lambda b, i: (b * nq + i, 0))],
        out_shape=[jax.ShapeDtypeStruct((DEC_BATCH * t, RET_W), bf16),
                   jax.ShapeDtypeStruct((DEC_BATCH * t, WIN_W), bf16)],
        scratch_shapes=[pltpu.VMEM((tb, RET_W), f32), pltpu.VMEM((H_RET, TQ, TQ), f32),
                        pltpu.VMEM((2, TQ, RET_W), f32), pltpu.VMEM((2, TQ, RET_W), f32),
                        pltpu.VMEM((t // TQ, H_RET // 2, PAIR_W, PAIR_W), f32),
                        pltpu.VMEM((t // TQ, H_RET // 2, PAIR_W, PAIR_W), f32)],
        compiler_params=_params(2),
        name="lat_ab",
    )(log_gamma, sink, proj, proj, proj, proj, proj, proj, proj, ck, cv, state,
      lgf_lanes, lgb_lanes, gmat, gn_g, gn_b)


def _lambda_full(lam_ref, lam_init):
    lam = lam_ref[...]
    a = jnp.sum(lam[0:1, :] * lam[1:2, :], -1, keepdims=True)
    b = jnp.sum(lam[2:3, :] * lam[3:4, :], -1, keepdims=True)
    return jnp.exp(a) - jnp.exp(b) + lam_init


def _diff_heads(q_of, k_parts_of, v_parts_of, lam, subln, lam_init, group):
    res = []
    for h0 in range(0, H_DIFF, group):
        res += _diff_head_group(range(h0, h0 + group), q_of, k_parts_of, v_parts_of, lam, subln, lam_init)
    return res


def _diff_head_group(heads, q_of, k_parts_of, v_parts_of, lam, subln, lam_init):
    subs = [(h, e) for h in heads for e in range(2)]
    scores = {}
    for h, e in subs:
        q = q_of(h)
        fm = _lane_half_mask(q.shape)
        q_sub = jnp.where(fm if e == 0 else jnp.logical_not(fm), q, jnp.zeros_like(q))
        scores[h, e] = [_dot(q_sub, k) if transposed else _dot_nt(q_sub, k) for k, transposed in k_parts_of(h)]
    probs = {}
    for key in subs:
        m = scores[key][0].max(-1, keepdims=True)
        for sc in scores[key][1:]:
            m = jnp.maximum(m, sc.max(-1, keepdims=True))
        es = [jnp.exp2(sc - m) for sc in scores[key]]
        denom = es[0].sum(-1, keepdims=True)
        for ex in es[1:]:
            denom = denom + ex.sum(-1, keepdims=True)
        probs[key] = ([ex.astype(bf16) for ex in es], denom)
    outs = {}
    for h, e in subs:
        es, denom = probs[h, e]
        pv = functools.reduce(lambda x, y: x + y, [_dot(ex, v) for v, ex in zip(v_parts_of(h), es)])
        outs[h, e] = pv / denom
    res = []
    for h in heads:
        a = outs[h, 0] - lam * outs[h, 1]
        res.append(a * lax.rsqrt(jnp.mean(a * a, -1, keepdims=True) + LN_EPS) * subln * (1.0 - lam_init))
    return res


def _fourier_rows(ct_ref, st_ref, z, bdc_ref, bds_ref):
    zc = _dot(z, bdc_ref[...].astype(bf16)).astype(bf16)
    zs = _dot(z, bds_ref[...].astype(bf16)).astype(bf16)
    return _dot(ct_ref[...].astype(bf16), zc) - _dot(st_ref[...].astype(bf16), zs)


def _ctx_cd_kernel(q_ref, k_ref, v_ref, z_ref, lam_ref, subln_ref, ct_ref, st_ref, bdc_ref, bds_ref,
                   a_ref, zf_ref, *, lam_init):
    lam = _lambda_full(lam_ref, lam_init)
    for sq in range(CTX_SEQS_CD):
        rows = slice(sq * SEQ, (sq + 1) * SEQ)
        sl = lambda h: slice(h * PAIR_W, (h + 1) * PAIR_W)
        heads = _diff_heads(lambda h: q_ref[rows, sl(h)], lambda h: [(k_ref[rows, sl(h)], False)],
                            lambda h: [v_ref[rows, sl(h)]], lam, subln_ref[...], lam_init, DIFF_GROUP)
        for h in range(H_DIFF):
            a_ref[rows, sl(h)] = heads[h].astype(bf16)
        zf_ref[rows, :] = _fourier_rows(ct_ref, st_ref, z_ref[rows, :], bdc_ref, bds_ref).astype(bf16)


def _ctx_cd(proj, lam, subln, ct, st, bdc, bds, lam_init):
    t = SEQ
    tb = CTX_SEQS_CD * t
    const = lambda b: (0, 0)
    col = lambda c: (lambda b: (b, c))
    return pl.pallas_call(
        functools.partial(_ctx_cd_kernel, lam_init=lam_init),
        grid=(BATCH // CTX_SEQS_CD,),
        in_specs=[pl.BlockSpec((tb, DIFF_W), col(0)), pl.BlockSpec((tb, DIFF_W), col(1)),
                  pl.BlockSpec((tb, DIFF_W), col(2)), pl.BlockSpec((tb, FNET_W), col(3 * DIFF_W // FNET_W)),
                  pl.BlockSpec((4, HEAD_DIM), const), pl.BlockSpec((1, PAIR_W), const),
                  pl.BlockSpec((t, t), const), pl.BlockSpec((t, t), const),
                  pl.BlockSpec((FNET_W, FNET_W), const), pl.BlockSpec((FNET_W, FNET_W), const)],
        out_specs=[pl.BlockSpec((tb, DIFF_W), lambda b: (b, 0)), pl.BlockSpec((tb, FNET_W), lambda b: (b, 0))],
        out_shape=[jax.ShapeDtypeStruct((BATCH * t, DIFF_W), bf16),
                   jax.ShapeDtypeStruct((BATCH * t, FNET_W), bf16)],
        compiler_params=_params(1),
        name="ctx_cd",
    )(proj, proj, proj, proj, lam, subln, ct, st, bdc, bds)


def _lat_cd_kernel(q_ref, k_ref, v_ref, z_ref, ckt_ref, cv_ref, lam_ref, subln_ref, ct_ref, st_ref, bdc_ref, bds_ref,
                   a_ref, zf_ref, *, lam_init):
    lam = _lambda_full(lam_ref, lam_init)
    sl = lambda h: slice(h * PAIR_W, (h + 1) * PAIR_W)
    heads = _diff_heads(lambda h: q_ref[:, sl(h)],
                        lambda h: [(k_ref[:, sl(h)], False), (ckt_ref[0, h].astype(bf16), True)],
                        lambda h: [v_ref[:, sl(h)], cv_ref[0, h].astype(bf16)], lam, subln_ref[...], lam_init,
                        DIFF_GROUP_LAT)
    for h in range(H_DIFF):
        a_ref[:, sl(h)] = heads[h].astype(bf16)
    zf_ref[...] = _fourier_rows(ct_ref, st_ref, z_ref[...], bdc_ref, bds_ref).astype(bf16)


def _lat_cd(proj, ck, cv, lam, subln, ct, st, bdc, bds, lam_init):
    t = DEC_SEQ
    nq = t // TQ_CD
    const = lambda b, i: (0, 0)
    return pl.pallas_call(
        functools.partial(_lat_cd_kernel, lam_init=lam_init),
        grid=(DEC_BATCH, nq),
        in_specs=[pl.BlockSpec((TQ_CD, DIFF_W), lambda b, i: (N_CTX // TQ_CD + b * nq + i, 0)),
                  pl.BlockSpec((t, DIFF_W), lambda b, i: (N_CTX // t + b, 1)),
                  pl.BlockSpec((t, DIFF_W), lambda b, i: (N_CTX // t + b, 2)),
                  pl.BlockSpec((t, FNET_W), lambda b, i: (N_CTX // t + b, 3 * DIFF_W // FNET_W)),
                  pl.BlockSpec((1, H_DIFF, PAIR_W, PAST_LEN), lambda b, i: (b, 0, 0, 0)),
                  pl.BlockSpec((1, H_DIFF, PAST_LEN, PAIR_W), lambda b, i: (b, 0, 0, 0)),
                  pl.BlockSpec((4, HEAD_DIM), const), pl.BlockSpec((1, PAIR_W), const),
                  pl.BlockSpec((TQ_CD, t), lambda b, i: (i, 0)), pl.BlockSpec((TQ_CD, t), lambda b, i: (i, 0)),
                  pl.BlockSpec((FNET_W, FNET_W), const), pl.BlockSpec((FNET_W, FNET_W), const)],
        out_specs=[pl.BlockSpec((TQ_CD, DIFF_W), lambda b, i: (b * nq + i, 0)),
                   pl.BlockSpec((TQ_CD, FNET_W), lambda b, i: (b * nq + i, 0))],
        out_shape=[jax.ShapeDtypeStruct((DEC_BATCH * t, DIFF_W), bf16),
                   jax.ShapeDtypeStruct((DEC_BATCH * t, FNET_W), bf16)],
        compiler_params=_params(2),
        name="lat_cd",
    )(proj, proj, proj, proj, ck, cv, lam, subln, ct, st, bdc, bds)


def _rope_tables():
    t = np.arange(DEC_SEQ)
    quarter = HEAD_DIM // 4
    inv = ROPE_BASE ** (-np.arange(quarter, dtype=np.float64) / quarter)
    ang = np.concatenate([(t // GRID_W)[:, None] * inv, (t % GRID_W)[:, None] * inv], -1)
    cos, sin = np.cos(ang), np.sin(ang)
    reps = LANES // HEAD_DIM
    return (np.tile(np.concatenate([cos, cos], -1), (1, reps)).astype(np.float32),
            np.tile(np.concatenate([-sin, sin], -1), (1, reps)).astype(np.float32))


def _dft_tables(n):
    k = np.arange(n)
    ang = (2.0 * math.pi / n) * ((k[:, None] * k[None, :]) % n)
    return (np.cos(ang) / math.sqrt(n)).astype(np.float32), (np.sin(ang) / math.sqrt(n)).astype(np.float32)


def _block_diag(m, reps):
    return np.kron(np.eye(reps, dtype=m.dtype), m)


def kernel(x_prompt, x_sample, state_ret, cache_win_k, cache_win_v, cache_diff_k, cache_diff_v, c, c_ctx, w_mod, b_mod, ln_g, ln_b, w_in_ab, w_out_ab, ret_log_gamma, ret_gn_g, ret_gn_b, win_sink, w_in_cd, w_out_cd, diff_lambda, diff_subln_g, w_gate, w_up, w_down):
    cond = jnp.concatenate([c_ctx[None, :], c, jnp.zeros((SUBLANES - 1 - DEC_BATCH, D_MODEL), f32)], 0)
    mod = _modulation(cond, w_mod, b_mod).reshape(DEPTH, SUBLANES, 6, D_MODEL)

    rope_tabs = _rope_tables()
    gmat = jnp.asarray(_block_diag(np.full((HEAD_DIM, HEAD_DIM), 1.0 / HEAD_DIM, np.float32),
                                   N_CHUNK // HEAD_DIM), bf16)
    c64, s64 = _dft_tables(FNET_DIM)
    bdc = _block_diag(c64, FNET_GROUPS)
    bds = _block_diag(s64, FNET_GROUPS)
    dft_ctx = _dft_tables(SEQ)
    dft_lat = _dft_tables(DEC_SEQ)

    x_parts = [x_prompt.reshape(N_CTX, D_MODEL), x_sample.reshape(N_LAT, D_MODEL)]
    outs = {}
    for l in range(DEPTH):
        i = l // 2
        if l % 2 == 0:
            lgf = jnp.repeat(ret_log_gamma[i, 0], HEAD_DIM)[None, :]
            lgb = jnp.repeat(ret_log_gamma[i, 1], HEAD_DIM)[None, :]
            gn_g = ret_gn_g[i][None, :]
            gn_b = ret_gn_b[i][None, :]
            sink = win_sink[i][None, :]
            rope_tiles = tuple(range(0, 2 * RET_W // LANES)) + tuple(
                range(4 * RET_W // LANES, (4 * RET_W + WIN_W + KV_W) // LANES))
            kv_tile = (4 * RET_W + WIN_W) // LANES
            kv_shape = (BATCH, 1, KV_WIN, HEAD_DIM, SEQ)
            scale_tiles = tuple(range(4 * RET_W // LANES, (4 * RET_W + WIN_W) // LANES))
            proj, wk_t, wv_t = _proj(x_parts, mod, l, w_in_ab, i, scale_tiles, rope_tabs, rope_tiles,
                                     (kv_shape, kv_shape),
                                     {kv_tile: ("heads", 0, 0), kv_tile + 1: ("heads", 1, 0)})
            ro_c, wo_c, st_c = _ctx_ab(proj, ret_log_gamma[i], sink, lgf, lgb, gmat, gn_g, gn_b)
            ck = cache_win_k[:, i].reshape(DEC_BATCH, PAST_LEN, KV_W)
            cv = cache_win_v[:, i].reshape(DEC_BATCH, PAST_LEN, KV_W)
            ro_l, wo_l = _lat_ab(proj, ret_log_gamma[i], sink, ck, cv, state_ret, i, lgf, lgb, gmat, gn_g, gn_b)
            mix_a, mix_b, w_out = (ro_c, ro_l), (wo_c, wo_l), w_out_ab
            outs.setdefault('state', []).append(st_c[:, None])
            outs.setdefault('win_k', []).append(jnp.transpose(wk_t, (0, 1, 4, 2, 3)))
            outs.setdefault('win_v', []).append(jnp.transpose(wv_t, (0, 1, 4, 2, 3)))
        else:
            lam_init = 0.8 - 0.6 * math.exp(-0.3 * l)
            subln = diff_subln_g[i][None, :]
            rope_tiles = tuple(range(0, 2 * DIFF_W // LANES))
            plan = {}
            for h in range(H_DIFF):
                plan[DIFF_W // LANES + h] = ("pairs", 0, h)
                plan[2 * DIFF_W // LANES + h] = ("plain", 1, h)
            scale_tiles = tuple(range(0, DIFF_W // LANES))
            proj, dk_t, dv_h = _proj(
                x_parts, mod, l, w_in_cd, i, scale_tiles, rope_tabs, rope_tiles,
                ((BATCH, 1, H_DIFF, 2, HEAD_DIM, SEQ), (BATCH, 1, H_DIFF, SEQ, 2 * HEAD_DIM)), plan)
            a_c, z_c = _ctx_cd(proj, diff_lambda[i], subln, dft_ctx[0], dft_ctx[1], bdc, bds, lam_init)
            ck = jnp.transpose(cache_diff_k[:, i], (0, 2, 3, 4, 1)).reshape(DEC_BATCH, H_DIFF, PAIR_W, PAST_LEN)
            cv = jnp.transpose(cache_diff_v[:, i], (0, 2, 1, 3))
            a_l, z_l = _lat_cd(proj, ck, cv, diff_lambda[i], subln, dft_lat[0], dft_lat[1], bdc, bds, lam_init)
            mix_a, mix_b, w_out = (a_c, a_l), (z_c, z_l), w_out_cd
            outs.setdefault('diff_k', []).append(jnp.transpose(dk_t, (0, 1, 5, 2, 3, 4)))
            outs.setdefault('diff_v', []).append(jnp.transpose(dv_h, (0, 1, 3, 2, 4)))
        x_parts = _post(x_parts, mix_a, mix_b, mod, ln_g, ln_b, w_out, w_gate, w_up, w_down, l, i,
                        split_out=(l == DEPTH - 1))

    y_prompt = x_parts[0].reshape(BATCH, SEQ, D_MODEL)
    y_sample = x_parts[1].reshape(DEC_BATCH, DEC_SEQ, D_MODEL)
    cat = lambda parts: parts[0] if len(parts) == 1 else jnp.concatenate(parts, 1)
    return (y_prompt, y_sample, cat(outs['state']), cat(outs['win_k']), cat(outs['win_v']),
            cat(outs['diff_k']), cat(outs['diff_v']))
```
